```python
import jax, jax.numpy as jnp
from jax import lax
import numpy as np

D_MODEL = 1024
BATCH = 8
SEQ = 8192
DEPTH = 2

MIX_WIDTH = D_MODEL
GM_WIDTH = MIX_WIDTH // 4
GM_HEADS = 4
GM_HEAD_DIM = GM_WIDTH // GM_HEADS
GM_CHUNK = 128
RET_WIDTH = MIX_WIDTH // 2
RET_HEADS = 4
RET_HEAD_DIM = RET_WIDTH // RET_HEADS
RET_CHUNK = 128
CONV_WIDTH = MIX_WIDTH - GM_WIDTH - RET_WIDTH
CONV_KERNEL = 31
IN_WIDTH = 2 * GM_WIDTH + 4 * RET_WIDTH + 2 * CONV_WIDTH
FFN_HIDDEN = ((8 * D_MODEL // 3 + 255) // 256) * 256
ROPE_BASE = 10000.0
EPS = 1e-6

kernel_name = 'hybrid_gmlp_retention_conformer_encoder'


def _rmsnorm(x, g):
    x32 = x.astype(jnp.float32)
    y = x32 * lax.rsqrt(jnp.mean(x32 * x32, axis=-1, keepdims=True) + EPS)
    return (y * g.astype(jnp.float32)).astype(x.dtype)


def _standardize(x):
    x32 = x.astype(jnp.float32)
    mu = jnp.mean(x32, axis=-1, keepdims=True)
    var = jnp.mean(jnp.square(x32 - mu), axis=-1, keepdims=True)
    return (x32 - mu) * lax.rsqrt(var + EPS)


def _layernorm(x, g, b):
    return (_standardize(x) * g.astype(jnp.float32) + b.astype(jnp.float32)).astype(x.dtype)


def _spatial_gating(u, v, ln_g, ln_b, w_s, b_s):
    bsz, seq, _ = v.shape
    n = seq // GM_CHUNK
    v = _layernorm(v, ln_g, ln_b).reshape(bsz, n, GM_CHUNK, GM_HEADS, GM_HEAD_DIM)
    mixed = jnp.einsum('hpq,bnqhd->bnphd', w_s, v) + b_s.T[:, :, None]
    return (u.reshape(mixed.shape) * mixed).reshape(bsz, seq, GM_WIDTH)


def _rotary(t, cos, sin):
    t1, t2 = jnp.split(t, 2, axis=-1)
    return jnp.concatenate([t1 * cos - t2 * sin, t1 * sin + t2 * cos], axis=-1)


def _retention_direction(q, k, v, gamma, include_diag):
    dt = q.dtype
    idx = jnp.arange(RET_CHUNK, dtype=jnp.float32)
    log_g = jnp.log(gamma)[:, None]
    diff = idx[:, None] - idx[None, :]
    mask = (diff >= 0) if include_diag else (diff > 0)
    d_intra = jnp.where(mask, jnp.exp(log_g[:, :, None] * jnp.where(mask, diff, 0.0)), 0.0).astype(dt)
    zeta = jnp.exp(log_g * (RET_CHUNK - 1 - idx)).astype(dt)
    xi = jnp.exp(log_g * (idx + 1)).astype(dt)
    gamma_c = jnp.exp(log_g * RET_CHUNK).astype(dt)[:, :, None]
    scores = jnp.einsum('bhncd,bhnmd->bhncm', q, k) * d_intra[:, None]
    intra = jnp.einsum('bhncm,bhnme->bhnce', scores, v)
    kv = jnp.einsum('bhnmd,bhnme,hm->nbhde', k, v, zeta)

    def step(state, kv_n):
        return gamma_c * state + kv_n, state

    _, prev = lax.scan(step, jnp.zeros_like(kv[0]), kv)
    cross = jnp.einsum('bhncd,nbhde,hc->bhnce', q, prev, xi)
    return intra + cross


def _retention(q, k, v, g, cos, sin):
    bsz, seq, _ = q.shape
    n = seq // RET_CHUNK
    shp = (bsz, seq, RET_HEADS, RET_HEAD_DIM)
    q = _rotary(q.reshape(shp), cos, sin)
    k = _rotary(k.reshape(shp), cos, sin) * (RET_HEAD_DIM ** -0.5)
    v = v.reshape(shp)

    def chunk(t):
        return t.reshape(bsz, n, RET_CHUNK, RET_HEADS, RET_HEAD_DIM).transpose(0, 3, 1, 2, 4)

    def unchunk(t):
        return t.transpose(0, 2, 3, 1, 4).reshape(shp)

    def rev(t):
        return jnp.flip(t, axis=1)

    gamma_fwd = 1.0 - jnp.exp2(-5.0 - jnp.arange(RET_HEADS, dtype=jnp.float32))
    gamma_bwd = gamma_fwd[::-1]
    fwd = unchunk(_retention_direction(chunk(q), chunk(k), chunk(v), gamma_fwd, True))
    bwd = rev(unchunk(_retention_direction(chunk(rev(q)), chunk(rev(k)), chunk(rev(v)), gamma_bwd, False)))
    o = _standardize(fwd + bwd).astype(q.dtype)
    return o.reshape(bsz, seq, RET_WIDTH) * jax.nn.silu(g)


def _conformer_conv(a, gate, dw_w, dw_b, ln_g, ln_b):
    h = a * jax.nn.sigmoid(gate)
    pad = CONV_KERNEL // 2
    h = lax.conv_general_dilated(h, dw_w[:, None, :], window_strides=(1,), padding=[(pad, pad)],
                                 dimension_numbers=('NWC', 'WIO', 'NWC'),
                                 feature_group_count=CONV_WIDTH) + dw_b
    return jax.nn.silu(_layernorm(h, ln_g, ln_b))


def _fwd_setup_inputs(seed: int = 0) -> dict:
    key = jax.random.key(seed)
    ks = jax.random.split(key, 16)
    f32 = jnp.float32

    def nrm(k, shape, scale):
        return jax.random.normal(k, shape, f32) * scale

    return {
        'x': nrm(ks[0], (BATCH, SEQ, D_MODEL), 1.0),
        'norm1_g': 1.0 + nrm(ks[1], (DEPTH, D_MODEL), 0.02),
        'w_in': nrm(ks[2], (DEPTH, D_MODEL, IN_WIDTH), D_MODEL ** -0.5),
        'gm_ln_g': 1.0 + nrm(ks[3], (DEPTH, GM_WIDTH), 0.02),
        'gm_ln_b': nrm(ks[4], (DEPTH, GM_WIDTH), 0.02),
        'gm_ws': nrm(ks[5], (DEPTH, GM_HEADS, GM_CHUNK, GM_CHUNK), GM_CHUNK ** -0.5),
        'gm_bs': 1.0 + nrm(ks[6], (DEPTH, GM_HEADS, GM_CHUNK), 0.02),
        'conv_w': nrm(ks[7], (DEPTH, CONV_KERNEL, CONV_WIDTH), CONV_KERNEL ** -0.5),
        'conv_b': nrm(ks[8], (DEPTH, CONV_WIDTH), 0.02),
        'conv_ln_g': 1.0 + nrm(ks[9], (DEPTH, CONV_WIDTH), 0.02),
        'conv_ln_b': nrm(ks[10], (DEPTH, CONV_WIDTH), 0.02),
        'w_out': nrm(ks[11], (DEPTH, MIX_WIDTH, D_MODEL), MIX_WIDTH ** -0.5),
        'norm2_g': 1.0 + nrm(ks[12], (DEPTH, D_MODEL), 0.02),
        'w_ffn_in': nrm(ks[13], (DEPTH, D_MODEL, 2 * FFN_HIDDEN), D_MODEL ** -0.5),
        'w_ffn_out': nrm(ks[14], (DEPTH, FFN_HIDDEN, D_MODEL), FFN_HIDDEN ** -0.5),
        'final_g': 1.0 + nrm(ks[15], (D_MODEL,), 0.02),
    }


def _fwd_reference(x, norm1_g, w_in, gm_ln_g, gm_ln_b, gm_ws, gm_bs, conv_w, conv_b, conv_ln_g,
              conv_ln_b, w_out, norm2_g, w_ffn_in, w_ffn_out, final_g):
    seq = x.shape[1]
    pos = jnp.arange(seq, dtype=jnp.float32)
    half = RET_HEAD_DIM // 2
    inv_freq = ROPE_BASE ** (-jnp.arange(half, dtype=jnp.float32) / half)
    ang = pos[:, None] * inv_freq[None, :]
    cos = jnp.cos(ang)[:, None, :].astype(x.dtype)
    sin = jnp.sin(ang)[:, None, :].astype(x.dtype)
    splits = np.cumsum([GM_WIDTH, GM_WIDTH, RET_WIDTH, RET_WIDTH, RET_WIDTH, RET_WIDTH, CONV_WIDTH])
    for l in range(DEPTH):
        h = _rmsnorm(x, norm1_g[l])
        proj = h @ w_in[l]
        gm_u, gm_v, q, k, v, g, cv_a, cv_gate = jnp.split(proj, splits, axis=-1)
        y_gm = _spatial_gating(jax.nn.gelu(gm_u, approximate=False), jax.nn.gelu(gm_v, approximate=False),
                               gm_ln_g[l], gm_ln_b[l], gm_ws[l], gm_bs[l])
        y_ret = _retention(q, k, v, g, cos, sin)
        y_cv = _conformer_conv(cv_a, cv_gate, conv_w[l], conv_b[l], conv_ln_g[l], conv_ln_b[l])
        x = x + jnp.concatenate([y_gm, y_ret, y_cv], axis=-1) @ w_out[l]
        h = _rmsnorm(x, norm2_g[l])
        gate, up = jnp.split(h @ w_ffn_in[l], 2, axis=-1)
        x = x + (jax.nn.silu(gate) * up) @ w_ffn_out[l]
    return _rmsnorm(x, final_g)


import jax as _jax
import jax.numpy as _jnp

TWIN_FORMAT = 'train_step'
FWD_PARAMS = ['x', 'norm1_g', 'w_in', 'gm_ln_g', 'gm_ln_b', 'gm_ws', 'gm_bs', 'conv_w', 'conv_b', 'conv_ln_g', 'conv_ln_b', 'w_out', 'norm2_g', 'w_ffn_in', 'w_ffn_out', 'final_g']
TWIN_WEIGHTS = ['norm1_g', 'w_in', 'gm_ln_g', 'gm_ln_b', 'gm_ws', 'gm_bs', 'conv_w', 'conv_b', 'conv_ln_g', 'conv_ln_b', 'w_out', 'norm2_g', 'w_ffn_in', 'w_ffn_out', 'final_g']
TWIN_DIFF_INPUT = 'x'
TWIN_INPUTS = ['x', 'norm1_g', 'w_in', 'gm_ln_g', 'gm_ln_b', 'gm_ws', 'gm_bs', 'conv_w', 'conv_b', 'conv_ln_g', 'conv_ln_b', 'w_out', 'norm2_g', 'w_ffn_in', 'w_ffn_out', 'final_g', 'loss_target', 'm_norm1_g', 'm_w_in', 'm_gm_ln_g', 'm_gm_ln_b', 'm_gm_ws', 'm_gm_bs', 'm_conv_w', 'm_conv_b', 'm_conv_ln_g', 'm_conv_ln_b', 'm_w_out', 'm_norm2_g', 'm_w_ffn_in', 'm_w_ffn_out', 'm_final_g', 'v_norm1_g', 'v_w_in', 'v_gm_ln_g', 'v_gm_ln_b', 'v_gm_ws', 'v_gm_bs', 'v_conv_w', 'v_conv_b', 'v_conv_ln_g', 'v_conv_ln_b', 'v_w_out', 'v_norm2_g', 'v_w_ffn_in', 'v_w_ffn_out', 'v_final_g']
TWIN_OUTPUTS = ['loss', 'grad_x', 'grad_norm1_g', 'grad_w_in', 'grad_gm_ln_g', 'grad_gm_ln_b', 'grad_gm_ws', 'grad_gm_bs', 'grad_conv_w', 'grad_conv_b', 'grad_conv_ln_g', 'grad_conv_ln_b', 'grad_w_out', 'grad_norm2_g', 'grad_w_ffn_in', 'grad_w_ffn_out', 'grad_final_g', 'delta_norm1_g', 'delta_w_in', 'delta_gm_ln_g', 'delta_gm_ln_b', 'delta_gm_ws', 'delta_gm_bs', 'delta_conv_w', 'delta_conv_b', 'delta_conv_ln_g', 'delta_conv_ln_b', 'delta_w_out', 'delta_norm2_g', 'delta_w_ffn_in', 'delta_w_ffn_out', 'delta_final_g', 'new_m_norm1_g', 'new_m_w_in', 'new_m_gm_ln_g', 'new_m_gm_ln_b', 'new_m_gm_ws', 'new_m_gm_bs', 'new_m_conv_w', 'new_m_conv_b', 'new_m_conv_ln_g', 'new_m_conv_ln_b', 'new_m_w_out', 'new_m_norm2_g', 'new_m_w_ffn_in', 'new_m_w_ffn_out', 'new_m_final_g', 'new_v_norm1_g', 'new_v_w_in', 'new_v_gm_ln_g', 'new_v_gm_ln_b', 'new_v_gm_ws', 'new_v_gm_bs', 'new_v_conv_w', 'new_v_conv_b', 'new_v_conv_ln_g', 'new_v_conv_ln_b', 'new_v_w_out', 'new_v_norm2_g', 'new_v_w_ffn_in', 'new_v_w_ffn_out', 'new_v_final_g']
TWIN_LEAF_KINDS = {'loss': 'loss', 'grad_x': 'grad_x', 'grad_norm1_g': 'grad_w', 'grad_w_in': 'grad_w', 'grad_gm_ln_g': 'grad_w', 'grad_gm_ln_b': 'grad_w', 'grad_gm_ws': 'grad_w', 'grad_gm_bs': 'grad_w', 'grad_conv_w': 'grad_w', 'grad_conv_b': 'grad_w', 'grad_conv_ln_g': 'grad_w', 'grad_conv_ln_b': 'grad_w', 'grad_w_out': 'grad_w', 'grad_norm2_g': 'grad_w', 'grad_w_ffn_in': 'grad_w', 'grad_w_ffn_out': 'grad_w', 'grad_final_g': 'grad_w', 'delta_norm1_g': 'delta_w', 'delta_w_in': 'delta_w', 'delta_gm_ln_g': 'delta_w', 'delta_gm_ln_b': 'delta_w', 'delta_gm_ws': 'delta_w', 'delta_gm_bs': 'delta_w', 'delta_conv_w': 'delta_w', 'delta_conv_b': 'delta_w', 'delta_conv_ln_g': 'delta_w', 'delta_conv_ln_b': 'delta_w', 'delta_w_out': 'delta_w', 'delta_norm2_g': 'delta_w', 'delta_w_ffn_in': 'delta_w', 'delta_w_ffn_out': 'delta_w', 'delta_final_g': 'delta_w', 'new_m_norm1_g': 'new_m', 'new_m_w_in': 'new_m', 'new_m_gm_ln_g': 'new_m', 'new_m_gm_ln_b': 'new_m', 'new_m_gm_ws': 'new_m', 'new_m_gm_bs': 'new_m', 'new_m_conv_w': 'new_m', 'new_m_conv_b': 'new_m', 'new_m_conv_ln_g': 'new_m', 'new_m_conv_ln_b': 'new_m', 'new_m_w_out': 'new_m', 'new_m_norm2_g': 'new_m', 'new_m_w_ffn_in': 'new_m', 'new_m_w_ffn_out': 'new_m', 'new_m_final_g': 'new_m', 'new_v_norm1_g': 'new_v', 'new_v_w_in': 'new_v', 'new_v_gm_ln_g': 'new_v', 'new_v_gm_ln_b': 'new_v', 'new_v_gm_ws': 'new_v', 'new_v_gm_bs': 'new_v', 'new_v_conv_w': 'new_v', 'new_v_conv_b': 'new_v', 'new_v_conv_ln_g': 'new_v', 'new_v_conv_ln_b': 'new_v', 'new_v_w_out': 'new_v', 'new_v_norm2_g': 'new_v', 'new_v_w_ffn_in': 'new_v', 'new_v_w_ffn_out': 'new_v', 'new_v_final_g': 'new_v'}


def _forward(args):
    return _fwd_reference(*[args[k] for k in FWD_PARAMS])


def _output_shape():
    def fwd():
        inp = _fwd_setup_inputs(0)
        return _fwd_reference(*[inp[k] for k in FWD_PARAMS])
    out = _jax.eval_shape(fwd)
    return out.shape, out.dtype

N_MICROBATCH = 1
ADAM_LR = 0.001
ADAM_B1 = 0.9
ADAM_B2 = 0.999
ADAM_EPS = 1e-08
ADAM_WD = 0.01
ADAM_STEP = 10
PER_EXAMPLE_BATCH_AXIS = {'x': 0, 'loss_target': 0}
SHARED_INPUTS = []
_WEIGHT_DTYPES = {'norm1_g': _jnp.float32, 'w_in': _jnp.float32, 'gm_ln_g': _jnp.float32, 'gm_ln_b': _jnp.float32, 'gm_ws': _jnp.float32, 'gm_bs': _jnp.float32, 'conv_w': _jnp.float32, 'conv_b': _jnp.float32, 'conv_ln_g': _jnp.float32, 'conv_ln_b': _jnp.float32, 'w_out': _jnp.float32, 'norm2_g': _jnp.float32, 'w_ffn_in': _jnp.float32, 'w_ffn_out': _jnp.float32, 'final_g': _jnp.float32}
MOMENT_SCALE = {'norm1_g': 2.572148e-01, 'w_in': 1.465426e-01, 'gm_ln_g': 1.459025e-01, 'gm_ln_b': 1.554575e-01, 'gm_ws': 1.057328e-01, 'gm_bs': 1.112807e-01, 'conv_w': 1.453541e-01, 'conv_b': 4.002708e-01, 'conv_ln_g': 2.225617e-01, 'conv_ln_b': 2.186196e-01, 'w_out': 1.636347e-01, 'norm2_g': 1.788234e-01, 'w_ffn_in': 7.040322e-02, 'w_ffn_out': 1.151697e-01, 'final_g': 6.400505e+01}


def _to_microbatches(a, axis):
    t = _jnp.moveaxis(a, axis, 0)
    t = t.reshape((N_MICROBATCH, t.shape[0] // N_MICROBATCH) + t.shape[1:])
    return _jnp.moveaxis(t, 1, axis + 1)


def setup_inputs(seed: int = 0) -> dict:
    inp = _fwd_setup_inputs(seed)
    key = _jax.random.fold_in(_jax.random.key(seed), 7919)
    shape, _ = _output_shape()
    out = dict(inp)
    out["loss_target"] = _jax.random.normal(_jax.random.fold_in(key, 0), shape, _jnp.float32)
    for i, name in enumerate(TWIN_WEIGHTS):
        w = inp[name].astype(_jnp.float32)
        if MOMENT_SCALE is None:
            s = _jnp.sqrt(_jnp.mean(_jnp.square(w)) + 1e-30)
        else:
            s = MOMENT_SCALE[name]
        km, kv = _jax.random.split(_jax.random.fold_in(key, i + 1))
        out[name] = w
        out["m_" + name] = s * _jax.random.normal(km, w.shape, _jnp.float32)
        out["v_" + name] = (s * s) * _jax.random.uniform(kv, w.shape, _jnp.float32, 0.5, 1.5)
    if N_MICROBATCH > 1:
        for name, axis in PER_EXAMPLE_BATCH_AXIS.items():
            out[name] = _to_microbatches(out[name], axis)
    return {'x': out['x'], 'norm1_g': out['norm1_g'], 'w_in': out['w_in'], 'gm_ln_g': out['gm_ln_g'], 'gm_ln_b': out['gm_ln_b'], 'gm_ws': out['gm_ws'], 'gm_bs': out['gm_bs'], 'conv_w': out['conv_w'], 'conv_b': out['conv_b'], 'conv_ln_g': out['conv_ln_g'], 'conv_ln_b': out['conv_ln_b'], 'w_out': out['w_out'], 'norm2_g': out['norm2_g'], 'w_ffn_in': out['w_ffn_in'], 'w_ffn_out': out['w_ffn_out'], 'final_g': out['final_g'], 'loss_target': out['loss_target'], 'm_norm1_g': out['m_norm1_g'], 'm_w_in': out['m_w_in'], 'm_gm_ln_g': out['m_gm_ln_g'], 'm_gm_ln_b': out['m_gm_ln_b'], 'm_gm_ws': out['m_gm_ws'], 'm_gm_bs': out['m_gm_bs'], 'm_conv_w': out['m_conv_w'], 'm_conv_b': out['m_conv_b'], 'm_conv_ln_g': out['m_conv_ln_g'], 'm_conv_ln_b': out['m_conv_ln_b'], 'm_w_out': out['m_w_out'], 'm_norm2_g': out['m_norm2_g'], 'm_w_ffn_in': out['m_w_ffn_in'], 'm_w_ffn_out': out['m_w_ffn_out'], 'm_final_g': out['m_final_g'], 'v_norm1_g': out['v_norm1_g'], 'v_w_in': out['v_w_in'], 'v_gm_ln_g': out['v_gm_ln_g'], 'v_gm_ln_b': out['v_gm_ln_b'], 'v_gm_ws': out['v_gm_ws'], 'v_gm_bs': out['v_gm_bs'], 'v_conv_w': out['v_conv_w'], 'v_conv_b': out['v_conv_b'], 'v_conv_ln_g': out['v_conv_ln_g'], 'v_conv_ln_b': out['v_conv_ln_b'], 'v_w_out': out['v_w_out'], 'v_norm2_g': out['v_norm2_g'], 'v_w_ffn_in': out['v_w_ffn_in'], 'v_w_ffn_out': out['v_w_ffn_out'], 'v_final_g': out['v_final_g']}


def _loss(weights, diff, rest, loss_target):
    with _jax.named_scope("forward"):
        args = {**rest, TWIN_DIFF_INPUT: diff, **{k: w.astype(_WEIGHT_DTYPES[k]) for k, w in weights.items()}}
        y = _forward(args)
    with _jax.named_scope("loss_head"):
        err = _jnp.square(y.astype(_jnp.float32) - loss_target)
        return 0.5 * _jnp.sum(_jnp.mean(err, axis=-1)) if err.ndim else 0.5 * err


def _adamw(w, g, m, v):
    m = ADAM_B1 * m + (1.0 - ADAM_B1) * g
    v = ADAM_B2 * v + (1.0 - ADAM_B2) * _jnp.square(g)
    m_hat = m / (1.0 - ADAM_B1 ** ADAM_STEP)
    v_hat = v / (1.0 - ADAM_B2 ** ADAM_STEP)
    delta = -ADAM_LR * (m_hat / (_jnp.sqrt(v_hat) + ADAM_EPS) + ADAM_WD * w)
    return delta, m, v


def reference(x, norm1_g, w_in, gm_ln_g, gm_ln_b, gm_ws, gm_bs, conv_w, conv_b, conv_ln_g, conv_ln_b, w_out, norm2_g, w_ffn_in, w_ffn_out, final_g, loss_target, m_norm1_g, m_w_in, m_gm_ln_g, m_gm_ln_b, m_gm_ws, m_gm_bs, m_conv_w, m_conv_b, m_conv_ln_g, m_conv_ln_b, m_w_out, m_norm2_g, m_w_ffn_in, m_w_ffn_out, m_final_g, v_norm1_g, v_w_in, v_gm_ln_g, v_gm_ln_b, v_gm_ws, v_gm_bs, v_conv_w, v_conv_b, v_conv_ln_g, v_conv_ln_b, v_w_out, v_norm2_g, v_w_ffn_in, v_w_ffn_out, v_final_g):
    given = dict(x=x, norm1_g=norm1_g, w_in=w_in, gm_ln_g=gm_ln_g, gm_ln_b=gm_ln_b, gm_ws=gm_ws, gm_bs=gm_bs, conv_w=conv_w, conv_b=conv_b, conv_ln_g=conv_ln_g, conv_ln_b=conv_ln_b, w_out=w_out, norm2_g=norm2_g, w_ffn_in=w_ffn_in, w_ffn_out=w_ffn_out, final_g=final_g, loss_target=loss_target, m_norm1_g=m_norm1_g, m_w_in=m_w_in, m_gm_ln_g=m_gm_ln_g, m_gm_ln_b=m_gm_ln_b, m_gm_ws=m_gm_ws, m_gm_bs=m_gm_bs, m_conv_w=m_conv_w, m_conv_b=m_conv_b, m_conv_ln_g=m_conv_ln_g, m_conv_ln_b=m_conv_ln_b, m_w_out=m_w_out, m_norm2_g=m_norm2_g, m_w_ffn_in=m_w_ffn_in, m_w_ffn_out=m_w_ffn_out, m_final_g=m_final_g, v_norm1_g=v_norm1_g, v_w_in=v_w_in, v_gm_ln_g=v_gm_ln_g, v_gm_ln_b=v_gm_ln_b, v_gm_ws=v_gm_ws, v_gm_bs=v_gm_bs, v_conv_w=v_conv_w, v_conv_b=v_conv_b, v_conv_ln_g=v_conv_ln_g, v_conv_ln_b=v_conv_ln_b, v_w_out=v_w_out, v_norm2_g=v_norm2_g, v_w_ffn_in=v_w_ffn_in, v_w_ffn_out=v_w_ffn_out, v_final_g=v_final_g)
    weights = {n: given[n] for n in TWIN_WEIGHTS}
    shared = {n: given[n] for n in SHARED_INPUTS}
    per_example = {n: given[n] for n in ['x']}
    grad_fn = _jax.value_and_grad(_loss, argnums=(0, 1))

    def one_microbatch(ex, loss_target):
        ex = dict(ex)
        diff = ex.pop(TWIN_DIFF_INPUT)
        return grad_fn(weights, diff, {**shared, **ex}, loss_target)

    if N_MICROBATCH == 1:
        loss, (grad_w, grad_x) = one_microbatch(per_example, given["loss_target"])
    else:
        def body(carry, xs):
            loss_sum, grad_sum = carry
            l_k, (gw_k, gx_k) = one_microbatch(xs[0], xs[1])
            with _jax.named_scope("update"):
                return (loss_sum + l_k, _jax.tree.map(_jnp.add, grad_sum, gw_k)), gx_k

        init = (_jnp.zeros((), _jnp.float32), _jax.tree.map(_jnp.zeros_like, weights))
        (loss, grad_w), grad_x = _jax.lax.scan(body, init, (per_example, given["loss_target"]))
    with _jax.named_scope("update"):
        delta_w, new_m, new_v = {}, {}, {}
        for n in TWIN_WEIGHTS:
            delta_w[n], new_m[n], new_v[n] = _adamw(weights[n], grad_w[n], given["m_" + n], given["v_" + n])
    return (loss, grad_x, *[grad_w[n] for n in TWIN_WEIGHTS], *[delta_w[n] for n in TWIN_WEIGHTS],
            *[new_m[n] for n in TWIN_WEIGHTS], *[new_v[n] for n in TWIN_WEIGHTS])
```

```python
import functools

import numpy as np
import jax
import jax.numpy as jnp
from jax import lax
from jax.experimental import pallas as pl
from jax.experimental.pallas import tpu as pltpu

F32, BF16 = jnp.float32, jnp.bfloat16
S = jax.ShapeDtypeStruct

D = 1024
INW = 3072
GMW = 256
RETW = 512
CVW = 256
HEADS = 4
DH = 128
C = 128
KW = 31
HALO = 16
FFH = 2816
NDEV = 8
FFB = 2 * FFH // NDEV
EPS = 1e-6
LAYERS = 2
SCALE = DH ** -0.5
VMEM_LIMIT = 56 * 1024 * 1024

ADAM_LR, ADAM_B1, ADAM_B2, ADAM_EPS, ADAM_WD, ADAM_STEP = 0.001, 0.9, 0.999, 1e-08, 0.01, 10

_SQRT_HALF = 0.7071067811865476
_INV_SQRT_2PI = 0.3989422804014327


def _params(*sem):
    return pltpu.CompilerParams(dimension_semantics=sem or None, vmem_limit_bytes=VMEM_LIMIT)


def _resident(shape, index_map):
    return pl.BlockSpec(shape, index_map, pipeline_mode=pl.Buffered(1))


def _dot(a, b):
    return jnp.dot(a, b, preferred_element_type=F32)


def _dot_nt(a, b):
    return lax.dot_general(a, b, (((1,), (1,)), ((), ())), preferred_element_type=F32)


def _dot_tn(a, b):
    return lax.dot_general(a, b, (((0,), (0,)), ((), ())), preferred_element_type=F32)


def _sigmoid(x):
    return 1.0 / (1.0 + jnp.exp(-x))


def _gelu_and_grad(x):
    cdf = 0.5 * (1.0 + lax.erf(x * _SQRT_HALF))
    return x * cdf, cdf + x * jnp.exp(-0.5 * x * x) * _INV_SQRT_2PI


def _silu_and_grad(x):
    s = _sigmoid(x)
    return x * s, s * (1.0 + x * (1.0 - s))


def _standardize(x):
    mu = jnp.mean(x, axis=-1, keepdims=True)
    d = x - mu
    rstd = lax.rsqrt(jnp.mean(d * d, axis=-1, keepdims=True) + EPS)
    return d * rstd, rstd


def _standardize_bwd(dxhat, xhat, rstd):
    m1 = jnp.mean(dxhat, axis=-1, keepdims=True)
    m2 = jnp.mean(dxhat * xhat, axis=-1, keepdims=True)
    return rstd * (dxhat - m1 - xhat * m2)


def _rms(x):
    return lax.rsqrt(jnp.mean(x * x, axis=-1, keepdims=True) + EPS)


def _rmsnorm_bwd(dy, x, r, g):
    u = dy * g
    return r * u - x * (r * r * r) * jnp.mean(u * x, axis=-1, keepdims=True)


def _col_sum(a):
    return jnp.sum(a, axis=0, keepdims=True)


def _rot(t, cos2, sin2):
    return t * cos2 + pltpu.roll(t, DH // 2, axis=1) * sin2


def _rot_t(dt, cos2, sin2):
    return dt * cos2 + pltpu.roll(dt * sin2, DH // 2, axis=1)


def _ret_consts():
    idx = np.arange(C, dtype=np.float32)
    gf = (1.0 - np.exp2(-5.0 - np.arange(HEADS, dtype=np.float32))).astype(np.float32)
    out = {}
    for name, gamma, fwd in (("f", gf, True), ("b", gf[::-1].copy(), False)):
        lg = np.log(gamma).astype(np.float32)[:, None]
        diff = idx[:, None] - idx[None, :]
        if fwd:
            mask = diff >= 0
            dist = np.where(mask, diff, 0.0)
            zeta = np.exp(lg * (C - 1 - idx))
            xi = np.exp(lg * (idx + 1))
        else:
            mask = diff < 0
            dist = np.where(mask, -diff, 0.0)
            zeta = np.exp(lg * idx)
            xi = np.exp(lg * (C - idx))
        dm = np.where(mask[None], np.exp(lg[:, :, None] * dist[None]), 0.0).astype(np.float32)
        bc = lambda vec: np.ascontiguousarray(np.broadcast_to(vec.astype(np.float32)[:, :, None], (HEADS, C, DH)))
        out[name] = dict(D=dm, XI=bc(xi), ZETA=bc(zeta), gC=[float(v) for v in np.exp(lg[:, 0] * C).astype(np.float32)])
    return out


def _rope_tables(t):
    pos = jnp.arange(t, dtype=F32)
    half = DH // 2
    inv_freq = 10000.0 ** (-jnp.arange(half, dtype=F32) / half)
    ang = pos[:, None] * inv_freq[None, :]
    cos, sin = jnp.cos(ang), jnp.sin(ang)
    return jnp.concatenate([cos, cos], axis=1), jnp.concatenate([-sin, sin], axis=1)


def _f_inproj(x, g1, w_all, l, name):
    t = x.shape[0]
    tm = 512

    def body(x_ref, g_ref, w_ref, proj_ref, h_ref):
        xv = x_ref[...]
        h = (xv * _rms(xv) * g_ref[...]).astype(BF16)
        h_ref[...] = h
        for nb in range(INW // 512):
            cs = slice(nb * 512, (nb + 1) * 512)
            proj_ref[:, cs] = _dot(h, w_ref[:, cs]).astype(BF16)

    return pl.pallas_call(
        body, grid=(t // tm,), name=name,
        in_specs=[pl.BlockSpec((tm, D), lambda i: (i, 0)),
                  pl.BlockSpec((None, 1, D), lambda i: (l, 0, 0)),
                  _resident((None, D, INW), lambda i: (l, 0, 0))],
        out_specs=[pl.BlockSpec((tm, INW), lambda i: (i, 0)), pl.BlockSpec((tm, D), lambda i: (i, 0))],
        out_shape=[S((t, INW), BF16), S((t, D), BF16)],
        compiler_params=_params("parallel"),
    )(x, g1, w_all)


def _gm_chunk_fwd(u, v, lng, lnb, ws_ref, bias):
    au, dau = _gelu_and_grad(u)
    av, dav = _gelu_and_grad(v)
    vhat, rstd = _standardize(av)
    vn = (vhat * lng + lnb).astype(BF16)
    head = lax.broadcasted_iota(jnp.int32, (C, GMW), 1) // (GMW // HEADS)
    mixed = bias
    for h in range(HEADS):
        mixed = mixed + jnp.where(head == h, _dot(ws_ref[h], vn), 0.0)
    return au, dau, dav, vhat, rstd, vn, mixed, head


def _f_gm(proj, lng, lnb, ws_bf, bias, l, name):
    t = proj.shape[0]
    tm = 512

    def body(p_ref, lng_ref, lnb_ref, ws_ref, bias_ref, y_ref):
        for ci in range(tm // C):
            rows = slice(ci * C, (ci + 1) * C)
            u = p_ref[rows, 0:GMW].astype(F32)
            v = p_ref[rows, GMW:2 * GMW].astype(F32)
            au, _, _, _, _, _, mixed, _ = _gm_chunk_fwd(u, v, lng_ref[...], lnb_ref[...], ws_ref, bias_ref[...])
            y_ref[rows, :] = (au * mixed).astype(BF16)

    return pl.pallas_call(
        body, grid=(t // tm,), name=name,
        in_specs=[pl.BlockSpec((tm, 2 * GMW), lambda i: (i, 0)),
                  pl.BlockSpec((None, 1, GMW), lambda i: (l, 0, 0)),
                  pl.BlockSpec((None, 1, GMW), lambda i: (l, 0, 0)),
                  pl.BlockSpec((None, HEADS, C, C), lambda i: (l, 0, 0, 0)),
                  pl.BlockSpec((None, C, GMW), lambda i: (l, 0, 0))],
        out_specs=pl.BlockSpec((tm, GMW), lambda i: (i, 0)),
        out_shape=S((t, GMW), BF16),
        compiler_params=_params("parallel"),
    )(proj, lng, lnb, ws_bf, bias)


def _ret_qkv(q_ref, k_ref, v_ref, cos_ref, sin_ref, h):
    sl = slice(h * DH, (h + 1) * DH)
    cos2, sin2 = cos_ref[...], sin_ref[...]
    qr = _rot(q_ref[:, sl].astype(F32), cos2, sin2)
    kr = _rot(k_ref[:, sl].astype(F32), cos2, sin2) * SCALE
    return sl, qr, kr, v_ref[:, sl], cos2, sin2


def _f_ret(proj, cos2, sin2, rc, name):
    t = proj.shape[0]
    n = t // C

    def body(qf, kf, vf, cf, sf, qb, kb, vb, cb, sb, Df, XIf, ZEf, Db, XIb, ZEb,
             of_ref, ob_ref, Sf_out, Sb_out, Sf, Sb):
        @pl.when(pl.program_id(0) == 0)
        def _():
            Sf[...] = jnp.zeros_like(Sf)
            Sb[...] = jnp.zeros_like(Sb)

        def one(q_ref, k_ref, v_ref, cos_ref, sin_ref, D_ref, XI_ref, ZE_ref, gC, S_scr, o_ref, S_out):
            for h in range(HEADS):
                sl, qr, kr, vh, _, _ = _ret_qkv(q_ref, k_ref, v_ref, cos_ref, sin_ref, h)
                st = S_scr[h]
                S_out[h] = st
                p = _dot_nt(qr.astype(BF16), kr.astype(BF16)) * D_ref[h]
                o = _dot(p.astype(BF16), vh) + _dot((qr * XI_ref[h]).astype(BF16), st.astype(BF16))
                o_ref[:, sl] = o
                S_scr[h] = gC[h] * st + _dot_tn((kr * ZE_ref[h]).astype(BF16), vh)

        one(qf, kf, vf, cf, sf, Df, XIf, ZEf, rc["f"]["gC"], Sf, of_ref, Sf_out)
        one(qb, kb, vb, cb, sb, Db, XIb, ZEb, rc["b"]["gC"], Sb, ob_ref, Sb_out)

    fw = lambda i: i
    bw = lambda i: n - 1 - i

    def qkv_specs(ix):
        return [pl.BlockSpec((C, RETW), lambda i, cb=cb: (ix(i), cb)) for cb in (1, 2, 3)] + \
               [pl.BlockSpec((C, DH), lambda i: (ix(i), 0)), pl.BlockSpec((C, DH), lambda i: (ix(i), 0))]

    const = lambda: pl.BlockSpec((HEADS, C, DH), lambda i: (0, 0, 0))
    return pl.pallas_call(
        body, grid=(n,), name=name,
        in_specs=qkv_specs(fw) + qkv_specs(bw) + [const() for _ in range(6)],
        out_specs=[pl.BlockSpec((C, RETW), lambda i: (fw(i), 0)), pl.BlockSpec((C, RETW), lambda i: (bw(i), 0)),
                   pl.BlockSpec((None, HEADS, DH, DH), lambda i: (fw(i), 0, 0, 0)),
                   pl.BlockSpec((None, HEADS, DH, DH), lambda i: (bw(i), 0, 0, 0))],
        out_shape=[S((t, RETW), F32), S((t, RETW), F32), S((n, HEADS, DH, DH), F32), S((n, HEADS, DH, DH), F32)],
        scratch_shapes=[pltpu.VMEM((HEADS, DH, DH), F32), pltpu.VMEM((HEADS, DH, DH), F32)],
        compiler_params=_params("arbitrary"),
    )(proj, proj, proj, cos2, sin2, proj, proj, proj, cos2, sin2,
      rc["f"]["D"], rc["f"]["XI"], rc["f"]["ZETA"], rc["b"]["D"], rc["b"]["XI"], rc["b"]["ZETA"])


def _conv_halo_specs(t, tm, width, col):
    r = tm // HALO
    last = t // HALO - 1
    return [pl.BlockSpec((HALO, width), lambda i: (jnp.maximum(i * r - 1, 0), col)),
            pl.BlockSpec((tm, width), lambda i: (i, col)),
            pl.BlockSpec((HALO, width), lambda i: (jnp.minimum((i + 1) * r, last), col))]


def _fill_ext(ext, prev, cur, nxt, i, nt, tm):
    ext[0:HALO, :] = jnp.where(i > 0, prev, 0.0)
    ext[HALO:HALO + tm, :] = cur
    ext[HALO + tm:2 * HALO + tm, :] = jnp.where(i < nt - 1, nxt, 0.0)


def _glu(a_ref, g_ref):
    return a_ref[...].astype(F32) * _sigmoid(g_ref[...].astype(F32))


def _f_conv(proj, cw, cb, lng, lnb, l, name):
    t = proj.shape[0]
    tm = 256
    nt = t // tm
    rb = 64

    def body(ap, ac, an, gp, gc, gn, cw_ref, cb_ref, lng_ref, lnb_ref, c_ref, y_ref, hext):
        i = pl.program_id(0)
        _fill_ext(hext, _glu(ap, gp), _glu(ac, gc), _glu(an, gn), i, nt, tm)
        for r0 in range(0, tm, rb):
            acc = jnp.zeros((rb, CVW), F32) + cb_ref[...]
            for j in range(KW):
                acc = acc + cw_ref[j:j + 1, :] * hext[pl.ds(r0 + j + 1, rb), :]
            c_ref[r0:r0 + rb, :] = acc
            chat, _ = _standardize(acc)
            z = chat * lng_ref[...] + lnb_ref[...]
            y_ref[r0:r0 + rb, :] = (z * _sigmoid(z)).astype(BF16)

    vec = lambda: pl.BlockSpec((None, 1, CVW), lambda i: (l, 0, 0))
    return pl.pallas_call(
        body, grid=(nt,), name=name,
        in_specs=_conv_halo_specs(t, tm, CVW, 10) + _conv_halo_specs(t, tm, CVW, 11) +
                 [pl.BlockSpec((None, 32, CVW), lambda i: (l, 0, 0)), vec(), vec(), vec()],
        out_specs=[pl.BlockSpec((tm, CVW), lambda i: (i, 0)), pl.BlockSpec((tm, CVW), lambda i: (i, 0))],
        out_shape=[S((t, CVW), F32), S((t, CVW), BF16)],
        scratch_shapes=[pltpu.VMEM((tm + 2 * HALO, CVW), F32)],
        compiler_params=_params("parallel"),
    )(proj, proj, proj, proj, proj, proj, cw, cb, lng, lnb)


def _f_mixout(x, y_gm, y_cv, o_f, o_b, proj, w_all, l, name):
    t = x.shape[0]
    tm = 512

    def body(x_ref, ygm_ref, ycv_ref, of_ref, ob_ref, g_ref, w_ref, xm_ref, ycat_ref):
        ycat_ref[:, 0:GMW] = ygm_ref[...]
        ycat_ref[:, GMW + RETW:D] = ycv_ref[...]
        for h in range(HEADS):
            sl = slice(h * DH, (h + 1) * DH)
            ohat, _ = _standardize(of_ref[:, sl] + ob_ref[:, sl])
            g = g_ref[:, sl].astype(F32)
            ycat_ref[:, GMW + h * DH:GMW + (h + 1) * DH] = (ohat * (g * _sigmoid(g))).astype(BF16)
        xm_ref[...] = x_ref[...] + _dot(ycat_ref[...], w_ref[...])

    return pl.pallas_call(
        body, grid=(t // tm,), name=name,
        in_specs=[pl.BlockSpec((tm, D), lambda i: (i, 0)),
                  pl.BlockSpec((tm, GMW), lambda i: (i, 0)),
                  pl.BlockSpec((tm, CVW), lambda i: (i, 0)),
                  pl.BlockSpec((tm, RETW), lambda i: (i, 0)),
                  pl.BlockSpec((tm, RETW), lambda i: (i, 0)),
                  pl.BlockSpec((tm, RETW), lambda i: (i, 4)),
                  _resident((None, D, D), lambda i: (l, 0, 0))],
        out_specs=[pl.BlockSpec((tm, D), lambda i: (i, 0)), pl.BlockSpec((tm, D), lambda i: (i, 0))],
        out_shape=[S((t, D), F32), S((t, D), BF16)],
        compiler_params=_params("parallel"),
    )(x, y_gm, y_cv, o_f, o_b, proj, w_all)


def _f_ffn(xm, g2, w1_all, w2_all, l, name):
    t = xm.shape[0]
    tm = 256
    half = NDEV // 2

    def body(x_ref, g_ref, w1_ref, w2_ref, xo_ref, h_ref, gu_ref, act_ref):
        xv = x_ref[...]
        h = (xv * _rms(xv) * g_ref[...]).astype(BF16)
        h_ref[...] = h
        acc = xv
        for j in range(half):
            gate = _dot(h, w1_ref[j])
            up = _dot(h, w1_ref[half + j])
            gu_ref[j] = gate.astype(BF16)
            gu_ref[half + j] = up.astype(BF16)
            a = ((gate * _sigmoid(gate)) * up).astype(BF16)
            act_ref[j] = a
            acc = acc + _dot(a, w2_ref[j * FFB:(j + 1) * FFB, :])
        xo_ref[...] = acc

    return pl.pallas_call(
        body, grid=(t // tm,), name=name,
        in_specs=[pl.BlockSpec((tm, D), lambda i: (i, 0)),
                  pl.BlockSpec((None, 1, D), lambda i: (l, 0, 0)),
                  _resident((NDEV, None, D, FFB), lambda i: (0, l, 0, 0)),
                  _resident((None, FFH, D), lambda i: (l, 0, 0))],
        out_specs=[pl.BlockSpec((tm, D), lambda i: (i, 0)), pl.BlockSpec((tm, D), lambda i: (i, 0)),
                   pl.BlockSpec((NDEV, tm, FFB), lambda i: (0, i, 0)), pl.BlockSpec((half, tm, FFB), lambda i: (0, i, 0))],
        out_shape=[S((t, D), F32), S((t, D), BF16), S((NDEV, t, FFB), BF16), S((half, t, FFB), BF16)],
        compiler_params=_params("parallel"),
    )(xm, g2, w1_all, w2_all)


def _b_loss(x, fg, tgt, name):
    t = x.shape[0]
    tm = 512

    def body(x_ref, g_ref, t_ref, loss_ref, dx_ref, dg_ref):
        @pl.when(pl.program_id(0) == 0)
        def _():
            loss_ref[...] = jnp.zeros_like(loss_ref)
            dg_ref[...] = jnp.zeros_like(dg_ref)

        xv = x_ref[...]
        r = _rms(xv)
        xr = xv * r
        err = xr * g_ref[...] - t_ref[...]
        loss_ref[...] += (0.5 / D) * _col_sum(jnp.sum(err * err, axis=1, keepdims=True))
        dy = err * (1.0 / D)
        dg_ref[...] += _col_sum(dy * xr)
        dx_ref[...] = _rmsnorm_bwd(dy, xv, r, g_ref[...])

    return pl.pallas_call(
        body, grid=(t // tm,), name=name,
        in_specs=[pl.BlockSpec((tm, D), lambda i: (i, 0)), pl.BlockSpec((1, D), lambda i: (0, 0)),
                  pl.BlockSpec((tm, D), lambda i: (i, 0))],
        out_specs=[pl.BlockSpec((1, 1), lambda i: (0, 0)), pl.BlockSpec((tm, D), lambda i: (i, 0)),
                   pl.BlockSpec((1, D), lambda i: (0, 0))],
        out_shape=[S((1, 1), F32), S((t, D), F32), S((1, D), F32)],
        compiler_params=_params("arbitrary"),
    )(x, fg, tgt)


def _b_ffn(dxo, xm, g2, gu, w1_all, w2_all, l, name):
    t = xm.shape[0]
    tm = 256
    half = NDEV // 2

    def body(dxo_ref, x_ref, g_ref, gu_ref, w1_ref, w2_ref, dgu_ref, dxm_ref, dxb_ref, dg_ref):
        @pl.when(pl.program_id(0) == 0)
        def _():
            dg_ref[...] = jnp.zeros_like(dg_ref)

        dxo = dxo_ref[...]
        dxb = dxo.astype(BF16)
        dxb_ref[...] = dxb
        dh = jnp.zeros((tm, D), F32)
        for j in range(half):
            dact = _dot_nt(dxb, w2_ref[j * FFB:(j + 1) * FFB, :])
            gate = gu_ref[j].astype(F32)
            up = gu_ref[half + j].astype(F32)
            sg, dsg = _silu_and_grad(gate)
            dgate = (dact * up * dsg).astype(BF16)
            dup = (dact * sg).astype(BF16)
            dgu_ref[j] = dgate
            dgu_ref[half + j] = dup
            dh = dh + _dot_nt(dgate, w1_ref[j]) + _dot_nt(dup, w1_ref[half + j])
        xv = x_ref[...]
        r = _rms(xv)
        dg_ref[...] += _col_sum(dh * xv * r)
        dxm_ref[...] = dxo + _rmsnorm_bwd(dh, xv, r, g_ref[...])

    return pl.pallas_call(
        body, grid=(t // tm,), name=name,
        in_specs=[pl.BlockSpec((tm, D), lambda i: (i, 0)), pl.BlockSpec((tm, D), lambda i: (i, 0)),
                  pl.BlockSpec((None, 1, D), lambda i: (l, 0, 0)),
                  pl.BlockSpec((NDEV, tm, FFB), lambda i: (0, i, 0)),
                  _resident((NDEV, None, D, FFB), lambda i: (0, l, 0, 0)),
                  _resident((None, FFH, D), lambda i: (l, 0, 0))],
        out_specs=[pl.BlockSpec((NDEV, tm, FFB), lambda i: (0, i, 0)), pl.BlockSpec((tm, D), lambda i: (i, 0)),
                   pl.BlockSpec((tm, D), lambda i: (i, 0)), pl.BlockSpec((1, D), lambda i: (0, 0))],
        out_shape=[S((NDEV, t, FFB), BF16), S((t, D), F32), S((t, D), BF16), S((1, D), F32)],
        compiler_params=_params("arbitrary"),
    )(dxo, xm, g2, gu, w1_all, w2_all)


def _mm_tn(a, b, pieces, a_mode, b_mode, name):
    bt = 512

    def spec(arr, mode):
        if mode == "shared":
            return arr.shape[0], arr.shape[1], pl.BlockSpec((bt, arr.shape[1]), lambda j, tt: (tt, 0))
        if mode == "cols":
            w = arr.shape[1] // pieces
            return arr.shape[0], w, pl.BlockSpec((bt, w), lambda j, tt: (tt, j))
        return arr.shape[1], arr.shape[2], pl.BlockSpec((None, bt, arr.shape[2]), lambda j, tt: (j, tt, 0))

    t, ka, a_spec = spec(a, a_mode)
    _, nb, b_spec = spec(b, b_mode)
    nt = t // bt

    def body(a_ref, b_ref, o_ref, acc):
        tt = pl.program_id(1)

        @pl.when(tt == 0)
        def _():
            acc[...] = jnp.zeros_like(acc)

        acc[...] += _dot_tn(a_ref[...], b_ref[...])

        @pl.when(tt == nt - 1)
        def _():
            o_ref[...] = acc[...].astype(BF16)

    return pl.pallas_call(
        body, grid=(pieces, nt), name=name,
        in_specs=[a_spec, b_spec],
        out_specs=pl.BlockSpec((None, ka, nb), lambda j, tt: (j, 0, 0)),
        out_shape=S((pieces, ka, nb), BF16),
        scratch_shapes=[pltpu.VMEM((ka, nb), F32)],
        compiler_params=_params("parallel", "arbitrary"),
    )(a, b)


def _b_mixout(dxm, w_all, o_f, o_b, proj, c, lng, lnb, l, name):
    t = dxm.shape[0]
    tm = 256

    def body(dxm_ref, w_ref, of_ref, ob_ref, g_ref, c_ref, lng_ref, lnb_ref,
             dxb_ref, dygm_ref, dO_ref, dg_ref, dc_ref, dlg_ref, dlb_ref, dcb_ref):
        @pl.when(pl.program_id(0) == 0)
        def _():
            dlg_ref[...] = jnp.zeros_like(dlg_ref)
            dlb_ref[...] = jnp.zeros_like(dlb_ref)
            dcb_ref[...] = jnp.zeros_like(dcb_ref)

        dxb = dxm_ref[...].astype(BF16)
        dxb_ref[...] = dxb
        dy = _dot_nt(dxb, w_ref[...])
        dygm_ref[...] = dy[:, 0:GMW]
        for h in range(HEADS):
            sl = slice(h * DH, (h + 1) * DH)
            ohat, rstd = _standardize(of_ref[:, sl] + ob_ref[:, sl])
            sg, dsg = _silu_and_grad(g_ref[:, sl].astype(F32))
            dyr = dy[:, GMW + h * DH:GMW + (h + 1) * DH]
            dg_ref[:, sl] = (dyr * ohat * dsg).astype(BF16)
            dO_ref[:, sl] = _standardize_bwd(dyr * sg, ohat, rstd)
        chat, rstd = _standardize(c_ref[...])
        z = chat * lng_ref[...] + lnb_ref[...]
        _, dsz = _silu_and_grad(z)
        dz = dy[:, GMW + RETW:D] * dsz
        dlg_ref[...] += _col_sum(dz * chat)
        dlb_ref[...] += _col_sum(dz)
        dc = _standardize_bwd(dz * lng_ref[...], chat, rstd)
        dcb_ref[...] += _col_sum(dc)
        dc_ref[...] = dc

    vec = lambda: pl.BlockSpec((None, 1, CVW), lambda i: (l, 0, 0))
    acc = lambda: pl.BlockSpec((1, CVW), lambda i: (0, 0))
    return pl.pallas_call(
        body, grid=(t // tm,), name=name,
        in_specs=[pl.BlockSpec((tm, D), lambda i: (i, 0)),
                  _resident((None, D, D), lambda i: (l, 0, 0)),
                  pl.BlockSpec((tm, RETW), lambda i: (i, 0)),
                  pl.BlockSpec((tm, RETW), lambda i: (i, 0)),
                  pl.BlockSpec((tm, RETW), lambda i: (i, 4)),
                  pl.BlockSpec((tm, CVW), lambda i: (i, 0)), vec(), vec()],
        out_specs=[pl.BlockSpec((tm, D), lambda i: (i, 0)), pl.BlockSpec((tm, GMW), lambda i: (i, 0)),
                   pl.BlockSpec((tm, RETW), lambda i: (i, 0)), pl.BlockSpec((tm, RETW), lambda i: (i, 0)),
                   pl.BlockSpec((tm, CVW), lambda i: (i, 0)), acc(), acc(), acc()],
        out_shape=[S((t, D), BF16), S((t, GMW), F32), S((t, RETW), F32), S((t, RETW), BF16), S((t, CVW), F32),
                   S((1, CVW), F32), S((1, CVW), F32), S((1, CVW), F32)],
        compiler_params=_params("arbitrary"),
    )(dxm, w_all, o_f, o_b, proj, c, lng, lnb)


def _b_gm(proj, dy, lng, lnb, ws_bf, wst_bf, bias, l, name):
    t = proj.shape[0]
    tm = 512
    nt = t // tm

    def body(p_ref, dy_ref, lng_ref, lnb_ref, ws_ref, wst_ref, bias_ref,
             duv_ref, dws_ref, dbias_ref, dbs_ref, dlg_ref, dlb_ref):
        @pl.when(pl.program_id(0) == 0)
        def _():
            dws_ref[...] = jnp.zeros_like(dws_ref)
            dbias_ref[...] = jnp.zeros_like(dbias_ref)
            dbs_ref[...] = jnp.zeros_like(dbs_ref)
            dlg_ref[...] = jnp.zeros_like(dlg_ref)
            dlb_ref[...] = jnp.zeros_like(dlb_ref)

        for ci in range(tm // C):
            rows = slice(ci * C, (ci + 1) * C)
            u = p_ref[rows, 0:GMW].astype(F32)
            v = p_ref[rows, GMW:2 * GMW].astype(F32)
            au, dau, dav, vhat, rstd, vn, mixed, head = _gm_chunk_fwd(u, v, lng_ref[...], lnb_ref[...], ws_ref, bias_ref[...])
            dyc = dy_ref[rows, :]
            dmixed = dyc * au
            dmb = dmixed.astype(BF16)
            dbias_ref[...] += dmixed
            dvn = jnp.zeros((C, GMW), F32)
            for h in range(HEADS):
                dws_ref[h] += _dot_nt(jnp.where(head == h, dmixed, 0.0).astype(BF16), vn)
                dvn = dvn + jnp.where(head == h, _dot(wst_ref[h], dmb), 0.0)
            dlg_ref[...] += _col_sum(dvn * vhat)
            dlb_ref[...] += _col_sum(dvn)
            dav_in = _standardize_bwd(dvn * lng_ref[...], vhat, rstd)
            duv_ref[rows, 0:GMW] = (dyc * mixed * dau).astype(BF16)
            duv_ref[rows, GMW:2 * GMW] = (dav_in * dav).astype(BF16)

        @pl.when(pl.program_id(0) == nt - 1)
        def _():
            head = lax.broadcasted_iota(jnp.int32, (C, GMW), 1) // (GMW // HEADS)
            lane = lax.broadcasted_iota(jnp.int32, (C, 128), 1)
            fold = jnp.zeros((C, 128), F32)
            for h in range(HEADS):
                col = jnp.sum(jnp.where(head == h, dbias_ref[...], 0.0), axis=1, keepdims=True)
                fold = jnp.where(lane == h, col, fold)
            dbs_ref[...] = fold

    vec = lambda: pl.BlockSpec((None, 1, GMW), lambda i: (l, 0, 0))
    mats = lambda: pl.BlockSpec((None, HEADS, C, C), lambda i: (l, 0, 0, 0))
    return pl.pallas_call(
        body, grid=(nt,), name=name,
        in_specs=[pl.BlockSpec((tm, 2 * GMW), lambda i: (i, 0)), pl.BlockSpec((tm, GMW), lambda i: (i, 0)),
                  vec(), vec(), mats(), mats(), pl.BlockSpec((None, C, GMW), lambda i: (l, 0, 0))],
        out_specs=[pl.BlockSpec((tm, 2 * GMW), lambda i: (i, 0)),
                   pl.BlockSpec((HEADS, C, C), lambda i: (0, 0, 0)),
                   pl.BlockSpec((C, GMW), lambda i: (0, 0)), pl.BlockSpec((C, 128), lambda i: (0, 0)),
                   pl.BlockSpec((1, GMW), lambda i: (0, 0)), pl.BlockSpec((1, GMW), lambda i: (0, 0))],
        out_shape=[S((t, 2 * GMW), BF16), S((HEADS, C, C), F32), S((C, GMW), F32), S((C, 128), F32),
                   S((1, GMW), F32), S((1, GMW), F32)],
        compiler_params=_params("arbitrary"),
    )(proj, dy, lng, lnb, ws_bf, wst_bf, bias)


def _b_conv(proj, dc, cw, l, name):
    t = proj.shape[0]
    tm = 256
    nt = t // tm
    rb = 64

    def body(ap, ac, an, gp, gc, gn, dp, dcur, dn, cw_ref, dag_ref, dcw_ref, hext, dext):
        i = pl.program_id(0)

        @pl.when(i == 0)
        def _():
            dcw_ref[...] = jnp.zeros_like(dcw_ref)

        _fill_ext(hext, _glu(ap, gp), _glu(ac, gc), _glu(an, gn), i, nt, tm)
        _fill_ext(dext, dp[...], dcur[...], dn[...], i, nt, tm)
        for j in range(KW):
            dcw_ref[j:j + 1, :] += _col_sum(dcur[...] * hext[pl.ds(j + 1, tm), :])
        for r0 in range(0, tm, rb):
            dh = jnp.zeros((rb, CVW), F32)
            for j in range(KW):
                dh = dh + cw_ref[j:j + 1, :] * dext[pl.ds(r0 + 2 * HALO - 1 - j, rb), :]
            a = ac[r0:r0 + rb, :].astype(F32)
            s = _sigmoid(gc[r0:r0 + rb, :].astype(F32))
            dag_ref[r0:r0 + rb, 0:CVW] = (dh * s).astype(BF16)
            dag_ref[r0:r0 + rb, CVW:2 * CVW] = (dh * a * s * (1.0 - s)).astype(BF16)

    dspecs = _conv_halo_specs(t, tm, CVW, 0)
    return pl.pallas_call(
        body, grid=(nt,), name=name,
        in_specs=_conv_halo_specs(t, tm, CVW, 10) + _conv_halo_specs(t, tm, CVW, 11) + dspecs +
                 [pl.BlockSpec((None, 32, CVW), lambda i: (l, 0, 0))],
        out_specs=[pl.BlockSpec((tm, 2 * CVW), lambda i: (i, 0)), pl.BlockSpec((32, CVW), lambda i: (0, 0))],
        out_shape=[S((t, 2 * CVW), BF16), S((32, CVW), F32)],
        scratch_shapes=[pltpu.VMEM((tm + 2 * HALO, CVW), F32), pltpu.VMEM((tm + 2 * HALO, CVW), F32)],
        compiler_params=_params("arbitrary"),
    )(proj, proj, proj, proj, proj, proj, dc, dc, dc, cw)


def _b_ret(proj, cos2, sin2, dO, s_f, s_b, rc, name):
    t = proj.shape[0]
    n = t // C

    def body(qb, kb, vb, cb, sb, dOb, Sb_in, qf, kf, vf, cf, sf, dOf, Sf_in, Db, XIb, ZEb, Df, XIf, ZEf,
             dqb, dkb, dvb, dqf, dkf, dvf, Gb, Gf):
        @pl.when(pl.program_id(0) == 0)
        def _():
            Gb[...] = jnp.zeros_like(Gb)
            Gf[...] = jnp.zeros_like(Gf)

        def one(q_ref, k_ref, v_ref, cos_ref, sin_ref, dO_ref, S_ref, D_ref, XI_ref, ZE_ref, gC, G_scr, dq_ref, dk_ref, dv_ref):
            for h in range(HEADS):
                sl, qr, kr, vh, cos2v, sin2v = _ret_qkv(q_ref, k_ref, v_ref, cos_ref, sin_ref, h)
                qh, kh = qr.astype(BF16), kr.astype(BF16)
                dOh = dO_ref[:, sl].astype(BF16)
                st = S_ref[h].astype(BF16)
                gr = G_scr[h]
                grb = gr.astype(BF16)
                dm = D_ref[h]
                p = (_dot_nt(qh, kh) * dm).astype(BF16)
                dp = (_dot_nt(dOh, vh) * dm).astype(BF16)
                dqr = _dot(dp, kh) + XI_ref[h] * _dot_nt(dOh, st)
                dkr = (_dot_tn(dp, qh) + ZE_ref[h] * _dot_nt(vh, grb)) * SCALE
                dv = _dot_tn(p, dOh) + _dot((kr * ZE_ref[h]).astype(BF16), grb)
                G_scr[h] = gC[h] * gr + _dot_tn((qr * XI_ref[h]).astype(BF16), dOh)
                dq_ref[:, sl] = _rot_t(dqr, cos2v, sin2v).astype(BF16)
                dk_ref[:, sl] = _rot_t(dkr, cos2v, sin2v).astype(BF16)
                dv_ref[:, sl] = dv.astype(BF16)

        one(qb, kb, vb, cb, sb, dOb, Sb_in, Db, XIb, ZEb, rc["b"]["gC"], Gb, dqb, dkb, dvb)
        one(qf, kf, vf, cf, sf, dOf, Sf_in, Df, XIf, ZEf, rc["f"]["gC"], Gf, dqf, dkf, dvf)

    up = lambda i: i
    down = lambda i: n - 1 - i

    def in_specs(ix):
        return [pl.BlockSpec((C, RETW), lambda i, cb=cb: (ix(i), cb)) for cb in (1, 2, 3)] + \
               [pl.BlockSpec((C, DH), lambda i: (ix(i), 0)), pl.BlockSpec((C, DH), lambda i: (ix(i), 0)),
                pl.BlockSpec((C, RETW), lambda i: (ix(i), 0)),
                pl.BlockSpec((None, HEADS, DH, DH), lambda i: (ix(i), 0, 0, 0))]

    const = lambda: pl.BlockSpec((HEADS, C, DH), lambda i: (0, 0, 0))
    outs = lambda ix: [pl.BlockSpec((C, RETW), lambda i: (ix(i), 0)) for _ in range(3)]
    return pl.pallas_call(
        body, grid=(n,), name=name,
        in_specs=in_specs(up) + in_specs(down) + [const() for _ in range(6)],
        out_specs=outs(up) + outs(down),
        out_shape=[S((t, RETW), BF16) for _ in range(6)],
        scratch_shapes=[pltpu.VMEM((HEADS, DH, DH), F32), pltpu.VMEM((HEADS, DH, DH), F32)],
        compiler_params=_params("arbitrary"),
    )(proj, proj, proj, cos2, sin2, dO, s_b, proj, proj, proj, cos2, sin2, dO, s_f,
      rc["b"]["D"], rc["b"]["XI"], rc["b"]["ZETA"], rc["f"]["D"], rc["f"]["XI"], rc["f"]["ZETA"])


def _b_inproj(d_uv, dqkv, d_g, d_ag, w_all, x, g1, dxm, l, name):
    t = x.shape[0]
    tm = 256
    dq_b, dk_b, dv_b, dq_f, dk_f, dv_f = dqkv

    def body(duv_ref, dqb, dkb, dvb, dqf, dkf, dvf, dg_ref, dag_ref, w_ref, x_ref, g_ref, dxm_ref,
             dp_ref, dx_ref, dn_ref):
        @pl.when(pl.program_id(0) == 0)
        def _():
            dn_ref[...] = jnp.zeros_like(dn_ref)

        dp_ref[:, 0:512] = duv_ref[...]
        for k, (a, b) in enumerate(((dqb, dqf), (dkb, dkf), (dvb, dvf))):
            dp_ref[:, 512 * (k + 1):512 * (k + 2)] = (a[...].astype(F32) + b[...].astype(F32)).astype(BF16)
        dp_ref[:, 2048:2560] = dg_ref[...]
        dp_ref[:, 2560:3072] = dag_ref[...]
        dh = _dot_nt(dp_ref[...], w_ref[...])
        xv = x_ref[...]
        r = _rms(xv)
        dn_ref[...] += _col_sum(dh * xv * r)
        dx_ref[...] = dxm_ref[...] + _rmsnorm_bwd(dh, xv, r, g_ref[...])

    half = lambda: pl.BlockSpec((tm, 512), lambda i: (i, 0))
    full = lambda: pl.BlockSpec((tm, D), lambda i: (i, 0))
    return pl.pallas_call(
        body, grid=(t // tm,), name=name,
        in_specs=[half() for _ in range(9)] +
                 [_resident((None, D, INW), lambda i: (l, 0, 0)), full(),
                  pl.BlockSpec((None, 1, D), lambda i: (l, 0, 0)), full()],
        out_specs=[pl.BlockSpec((tm, INW), lambda i: (i, 0)), full(), pl.BlockSpec((1, D), lambda i: (0, 0))],
        out_shape=[S((t, INW), BF16), S((t, D), F32), S((1, D), F32)],
        compiler_params=_params("arbitrary"),
    )(d_uv, dq_b, dk_b, dv_b, dq_f, dk_f, dv_f, d_g, d_ag, w_all, x, g1, dxm)


def _local_step(x, tgt, wts):
    t = x.shape[0]
    rc = _ret_consts()
    cos2, sin2 = _rope_tables(t)
    n1 = wts["norm1_g"].reshape(LAYERS, 1, D)
    n2 = wts["norm2_g"].reshape(LAYERS, 1, D)
    gm_lng = wts["gm_ln_g"].reshape(LAYERS, 1, GMW)
    gm_lnb = wts["gm_ln_b"].reshape(LAYERS, 1, GMW)
    ws_bf = wts["gm_ws"].astype(BF16)
    wst_bf = jnp.swapaxes(wts["gm_ws"], 2, 3).astype(BF16)
    bias = jnp.repeat(jnp.swapaxes(wts["gm_bs"], 1, 2), GMW // HEADS, axis=2)
    cw = jnp.pad(wts["conv_w"], ((0, 0), (0, 32 - KW), (0, 0)))
    cb = wts["conv_b"].reshape(LAYERS, 1, CVW)
    cv_lng = wts["conv_ln_g"].reshape(LAYERS, 1, CVW)
    cv_lnb = wts["conv_ln_b"].reshape(LAYERS, 1, CVW)

    saved = []
    for l in range(LAYERS):
        proj, h1 = _f_inproj(x, n1, wts["w_in"], l, f"f_inproj_{l}")
        y_gm = _f_gm(proj, gm_lng, gm_lnb, ws_bf, bias, l, f"f_gm_{l}")
        o_f, o_b, s_f, s_b = _f_ret(proj, cos2, sin2, rc, f"f_ret_{l}")
        c, y_cv = _f_conv(proj, cw, cb, cv_lng, cv_lnb, l, f"f_conv_{l}")
        xm, ycat = _f_mixout(x, y_gm, y_cv, o_f, o_b, proj, wts["w_out"], l, f"f_mixout_{l}")
        xo, h2, gu, act = _f_ffn(xm, n2, wts["w_ffn_in"], wts["w_ffn_out"], l, f"f_ffn_{l}")
        saved.append(dict(x=x, proj=proj, h1=h1, o_f=o_f, o_b=o_b, s_f=s_f, s_b=s_b, c=c, xm=xm, ycat=ycat,
                          h2=h2, gu=gu, act=act))
        x = xo

    loss, dx, d_final = _b_loss(x, wts["final_g"].reshape(1, D), tgt, "b_loss")

    big = [None] * LAYERS
    small = [None] * LAYERS
    for l in reversed(range(LAYERS)):
        sv = saved[l]
        dgu, dxm, dxo_bf, d_n2 = _b_ffn(dx, sv["xm"], n2, sv["gu"], wts["w_ffn_in"], wts["w_ffn_out"], l, f"b_ffn_{l}")
        g_f2 = _mm_tn(sv["act"], dxo_bf, NDEV // 2, "lead", "shared", f"g_ffn_out_{l}")
        g_f1 = _mm_tn(sv["h2"], dgu, NDEV, "shared", "lead", f"g_ffn_in_{l}")
        dxm_bf, dy_gm, dO, d_g, dc, d_cvlg, d_cvlb, d_cb = _b_mixout(
            dxm, wts["w_out"], sv["o_f"], sv["o_b"], sv["proj"], sv["c"], cv_lng, cv_lnb, l, f"b_mixout_{l}")
        g_out = _mm_tn(sv["ycat"], dxm_bf, 1, "shared", "shared", f"g_out_{l}")
        d_uv, d_ws, _, d_bs_fold, d_gmlg, d_gmlb = _b_gm(sv["proj"], dy_gm, gm_lng, gm_lnb, ws_bf, wst_bf, bias, l, f"b_gm_{l}")
        d_ag, d_cw = _b_conv(sv["proj"], dc, cw, l, f"b_conv_{l}")
        dqkv = _b_ret(sv["proj"], cos2, sin2, dO, sv["s_f"], sv["s_b"], rc, f"b_ret_{l}")
        dproj, dx, d_n1 = _b_inproj(d_uv, dqkv, d_g, d_ag, wts["w_in"], sv["x"], n1, dxm, l, f"b_inproj_{l}")
        g_in = _mm_tn(sv["h1"], dproj, NDEV, "shared", "cols", f"g_in_{l}")
        big[l] = dict(w_in=g_in, w_out=g_out.reshape(NDEV, D // NDEV, D), w_ffn_in=g_f1,
                      w_ffn_out=g_f2.reshape(NDEV, FFH // NDEV, D))
        small[l] = dict(norm1_g=d_n1[0], gm_ln_g=d_gmlg[0], gm_ln_b=d_gmlb[0], gm_ws=d_ws,
                        gm_bs=d_bs_fold[:, :HEADS].T, conv_w=d_cw[:KW], conv_b=d_cb[0], conv_ln_g=d_cvlg[0],
                        conv_ln_b=d_cvlb[0], norm2_g=d_n2[0])
    small_g = {k: jnp.stack([small[l][k] for l in range(LAYERS)]) for k in small[0]}
    small_g["final_g"] = d_final[0]
    return loss, dx, big, small_g


def _me():
    return 4 * lax.axis_index("x") + 2 * lax.axis_index("y") + lax.axis_index("c")


def _peer(k):
    x, y, c = lax.axis_index("x"), lax.axis_index("y"), lax.axis_index("c")
    px = 1 - x if k & 4 else x
    py = 1 - y if k & 2 else y
    pc = 1 - c if k & 1 else c
    return (px, py, pc), 4 * px + 2 * py + pc


MESH = pl.DeviceIdType.MESH
_HBM = pl.BlockSpec(memory_space=pltpu.HBM)
_VMEM = pl.BlockSpec(memory_space=pltpu.VMEM)
CWP = 128


def _all_gather_weights(w_in, w_out, w_f1, w_f2, cw_pad):
    nf = 5

    def body(win, wout, wf1, wf2, cw, o_in, o_out, o_f1, o_f2, o_cw, b_in, b_out, b_f1, b_f2, ssem, rsem, lsem):
        me = _me()
        b_in[...] = win[...].astype(BF16)
        b_out[...] = wout[...].astype(BF16)
        b_f1[...] = wf1[...].astype(BF16)
        b_f2[...] = wf2[...].astype(BF16)
        srcs = (b_in, b_out, b_f1, b_f2, cw)

        def dst(f, slot):
            if f == 0:
                return o_in.at[:, :, pl.ds(pl.multiple_of(slot * (INW // NDEV), 128), INW // NDEV)]
            if f == 1:
                return o_out.at[:, pl.ds(pl.multiple_of(slot * (D // NDEV), 16), D // NDEV), :]
            if f == 2:
                return o_f1.at[slot]
            if f == 3:
                return o_f2.at[:, pl.ds(pl.multiple_of(slot * (FFH // NDEV), 16), FFH // NDEV), :]
            return o_cw.at[slot]

        local = [pltpu.make_async_copy(srcs[f], dst(f, me), lsem.at[f]) for f in range(nf)]
        for cp in local:
            cp.start()
        sends = []
        for k in range(1, NDEV):
            dev, _ = _peer(k)
            for f in range(nf):
                cp = pltpu.make_async_remote_copy(src_ref=srcs[f], dst_ref=dst(f, me), send_sem=ssem.at[f, k],
                                                  recv_sem=rsem.at[f, k], device_id=dev, device_id_type=MESH)
                cp.start()
                sends.append(cp)
        for k in range(1, NDEV):
            dev, idx = _peer(k)
            for f in range(nf):
                pltpu.make_async_remote_copy(src_ref=srcs[f], dst_ref=dst(f, idx), send_sem=ssem.at[f, k],
                                             recv_sem=rsem.at[f, k], device_id=dev, device_id_type=MESH).wait_recv()
        for cp in sends:
            cp.wait_send()
        for cp in local:
            cp.wait()

    return pl.pallas_call(
        body, name="all_gather_weights",
        in_specs=[_VMEM] * nf, out_specs=[_HBM] * nf,
        out_shape=[S((LAYERS, D, INW), BF16), S((LAYERS, D, D), BF16), S((NDEV, LAYERS, D, FFB), BF16),
                   S((LAYERS, FFH, D), BF16), S((NDEV, LAYERS, 32, CWP), F32)],
        scratch_shapes=[pltpu.VMEM(w_in.shape, BF16), pltpu.VMEM(w_out.shape, BF16), pltpu.VMEM(w_f1.shape, BF16),
                        pltpu.VMEM(w_f2.shape, BF16),
                        pltpu.SemaphoreType.DMA((nf, NDEV)), pltpu.SemaphoreType.DMA((nf, NDEV)),
                        pltpu.SemaphoreType.DMA((nf,))],
        compiler_params=pltpu.CompilerParams(vmem_limit_bytes=VMEM_LIMIT),
    )(w_in, w_out, w_f1, w_f2, cw_pad)


_BIG = ("w_in", "w_out", "w_ffn_in", "w_ffn_out")


def _exchange_grads(big, small_buf):
    srcs = [big[l][f] for f in _BIG for l in range(LAYERS)]
    ns = len(srcs) + 1

    def body(*refs):
        g = refs[:ns - 1]
        sm = refs[ns - 1]
        outs = refs[ns:ns + len(_BIG)]
        o_sm = refs[ns + len(_BIG)]
        ssem, rsem, lsem = refs[ns + len(_BIG) + 1:]
        me = _me()

        def pair(s, theirs, slot):
            if s == ns - 1:
                return sm, o_sm.at[slot]
            return g[s].at[theirs], outs[s // LAYERS].at[s % LAYERS, slot]

        local = []
        for s in range(ns):
            src, dst = pair(s, me, me)
            local.append(pltpu.make_async_copy(src, dst, lsem.at[s]))
            local[-1].start()
        sends = []
        for k in range(1, NDEV):
            dev, idx = _peer(k)
            for s in range(ns):
                src, dst = pair(s, idx, me)
                cp = pltpu.make_async_remote_copy(src_ref=src, dst_ref=dst, send_sem=ssem.at[s, k], recv_sem=rsem.at[s, k],
                                                  device_id=dev, device_id_type=MESH)
                cp.start()
                sends.append(cp)
        for k in range(1, NDEV):
            dev, idx = _peer(k)
            for s in range(ns):
                src, dst = pair(s, me, idx)
                pltpu.make_async_remote_copy(src_ref=src, dst_ref=dst, send_sem=ssem.at[s, k], recv_sem=rsem.at[s, k],
                                             device_id=dev, device_id_type=MESH).wait_recv()
        for cp in sends:
            cp.wait_send()
        for cp in local:
            cp.wait()

    out_shape = [S((LAYERS,) + big[0][f].shape, BF16) for f in _BIG] + [S((NDEV,) + small_buf.shape, F32)]
    return pl.pallas_call(
        body, name="exchange_grads",
        in_specs=[_HBM] * ns, out_specs=[_HBM] * (len(_BIG) + 1), out_shape=out_shape,
        scratch_shapes=[pltpu.SemaphoreType.DMA((ns, NDEV)), pltpu.SemaphoreType.DMA((ns, NDEV)),
                        pltpu.SemaphoreType.DMA((ns,))],
        compiler_params=pltpu.CompilerParams(vmem_limit_bytes=VMEM_LIMIT),
    )(*srcs, small_buf)


def _adamw(w, g, m, v):
    m = ADAM_B1 * m + (1.0 - ADAM_B1) * g
    v = ADAM_B2 * v + (1.0 - ADAM_B2) * (g * g)
    m_hat = m / (1.0 - ADAM_B1 ** ADAM_STEP)
    v_hat = v / (1.0 - ADAM_B2 ** ADAM_STEP)
    return -ADAM_LR * (m_hat / (jnp.sqrt(v_hat) + ADAM_EPS) + ADAM_WD * w), m, v


def _sum_adam(parts, w, m, v, name):
    _, _, mm, nn = parts.shape
    bm = 256 if mm % 256 == 0 else mm

    def body(p_ref, w_ref, m_ref, v_ref, g_ref, d_ref, nm_ref, nv_ref):
        g = p_ref[0].astype(F32)
        for s in range(1, NDEV):
            g = g + p_ref[s].astype(F32)
        g_ref[...] = g
        d_ref[...], nm_ref[...], nv_ref[...] = _adamw(w_ref[...], g, m_ref[...], v_ref[...])

    blk = lambda: pl.BlockSpec((None, bm, nn), lambda l, i: (l, i, 0))
    return pl.pallas_call(
        body, grid=(LAYERS, mm // bm), name=name,
        in_specs=[pl.BlockSpec((None, NDEV, bm, nn), lambda l, i: (l, 0, i, 0)), blk(), blk(), blk()],
        out_specs=[blk() for _ in range(4)],
        out_shape=[S((LAYERS, mm, nn), F32) for _ in range(4)],
        compiler_params=_params("parallel", "parallel"),
    )(parts, w, m, v)


def _sum_small(parts):
    def body(p_ref, o_ref):
        g = p_ref[0]
        for s in range(1, NDEV):
            g = g + p_ref[s]
        o_ref[...] = g

    return pl.pallas_call(body, name="sum_small", out_shape=S(parts.shape[1:], F32),
                          compiler_params=_params())(parts)


def _adam_small(g, w, m, v):
    def body(g_ref, w_ref, m_ref, v_ref, d_ref, nm_ref, nv_ref):
        d_ref[...], nm_ref[...], nv_ref[...] = _adamw(w_ref[...], g_ref[...], m_ref[...], v_ref[...])

    return pl.pallas_call(body, name="adam_small", out_shape=[S(g.shape, F32)] * 3, compiler_params=_params())(g, w, m, v)


_SMALL = ("norm1_g", "gm_ln_g", "gm_ln_b", "gm_ws", "gm_bs", "conv_w", "conv_b", "conv_ln_g", "conv_ln_b",
          "norm2_g", "final_g")
_NAMES = ("norm1_g", "w_in", "gm_ln_g", "gm_ln_b", "gm_ws", "gm_bs", "conv_w", "conv_b", "conv_ln_g", "conv_ln_b",
          "w_out", "norm2_g", "w_ffn_in", "w_ffn_out", "final_g")


def _pack(parts, rows):
    flat = jnp.concatenate([p.reshape(-1) for p in parts])
    return jnp.pad(flat, (0, rows * 1024 - flat.shape[0])).reshape(rows, 1024)


def _unpack(buf, shapes):
    flat = buf.reshape(-1)
    out, o = [], 0
    for shp in shapes:
        sz = int(np.prod(shp))
        out.append(flat[o:o + sz].reshape(shp))
        o += sz
    return out


def kernel(x, norm1_g, w_in, gm_ln_g, gm_ln_b, gm_ws, gm_bs, conv_w, conv_b, conv_ln_g, conv_ln_b, w_out, norm2_g, w_ffn_in, w_ffn_out, final_g, loss_target, m_norm1_g, m_w_in, m_gm_ln_g, m_gm_ln_b, m_gm_ws, m_gm_bs, m_conv_w, m_conv_b, m_conv_ln_g, m_conv_ln_b, m_w_out, m_norm2_g, m_w_ffn_in, m_w_ffn_out, m_final_g, v_norm1_g, v_w_in, v_gm_ln_g, v_gm_ln_b, v_gm_ws, v_gm_bs, v_conv_w, v_conv_b, v_conv_ln_g, v_conv_ln_b, v_w_out, v_norm2_g, v_w_ffn_in, v_w_ffn_out, v_final_g):
    w = dict(norm1_g=norm1_g, w_in=w_in, gm_ln_g=gm_ln_g, gm_ln_b=gm_ln_b, gm_ws=gm_ws, gm_bs=gm_bs, conv_w=conv_w,
             conv_b=conv_b, conv_ln_g=conv_ln_g, conv_ln_b=conv_ln_b, w_out=w_out, norm2_g=norm2_g, w_ffn_in=w_ffn_in,
             w_ffn_out=w_ffn_out, final_g=final_g)
    mo = dict(norm1_g=m_norm1_g, w_in=m_w_in, gm_ln_g=m_gm_ln_g, gm_ln_b=m_gm_ln_b, gm_ws=m_gm_ws, gm_bs=m_gm_bs,
              conv_w=m_conv_w, conv_b=m_conv_b, conv_ln_g=m_conv_ln_g, conv_ln_b=m_conv_ln_b, w_out=m_w_out,
              norm2_g=m_norm2_g, w_ffn_in=m_w_ffn_in, w_ffn_out=m_w_ffn_out, final_g=m_final_g)
    vo = dict(norm1_g=v_norm1_g, w_in=v_w_in, gm_ln_g=v_gm_ln_g, gm_ln_b=v_gm_ln_b, gm_ws=v_gm_ws, gm_bs=v_gm_bs,
              conv_w=v_conv_w, conv_b=v_conv_b, conv_ln_g=v_conv_ln_g, conv_ln_b=v_conv_ln_b, w_out=v_w_out,
              norm2_g=v_norm2_g, w_ffn_in=v_w_ffn_in, w_ffn_out=v_w_ffn_out, final_g=v_final_g)
    t = x.shape[1]
    me = _me()
    cshard = conv_w.shape[2]

    cw_pad = jnp.pad(conv_w, ((0, 0), (0, 32 - KW), (0, CWP - cshard)))
    g_in, g_out, g_f1, g_f2, g_cw = _all_gather_weights(w_in, w_out, w_ffn_in, w_ffn_out, cw_pad)
    conv_full = jnp.transpose(g_cw[:, :, :KW, :cshard], (1, 2, 0, 3)).reshape(LAYERS, KW, CVW)
    wts = dict(w, w_in=g_in, w_out=g_out, w_ffn_in=g_f1, w_ffn_out=g_f2, conv_w=conv_full)

    loss, dx, big, small_g = _local_step(x.reshape(t, D), loss_target.reshape(t, D), wts)

    small_rows = 152
    parts = _exchange_grads(big, _pack([small_g[k] for k in _SMALL], small_rows))
    grads, delta, new_m, new_v = {}, {}, {}, {}
    for f, p in zip(_BIG, parts[:len(_BIG)]):
        grads[f], delta[f], new_m[f], new_v[f] = _sum_adam(p, w[f], mo[f], vo[f], f"sum_adam_{f}")

    full_shapes = [(LAYERS, KW, CVW) if k == "conv_w" else w[k].shape for k in _SMALL]
    for k, g in zip(_SMALL, _unpack(_sum_small(parts[len(_BIG)]), full_shapes)):
        grads[k] = lax.dynamic_slice_in_dim(g, me * cshard, cshard, axis=2) if k == "conv_w" else g
    adam_rows = 144
    d_s, m_s, v_s = _adam_small(_pack([grads[k] for k in _SMALL], adam_rows), _pack([w[k] for k in _SMALL], adam_rows),
                                _pack([mo[k] for k in _SMALL], adam_rows), _pack([vo[k] for k in _SMALL], adam_rows))
    shapes = [w[k].shape for k in _SMALL]
    for dst, buf in ((delta, d_s), (new_m, m_s), (new_v, v_s)):
        for k, a in zip(_SMALL, _unpack(buf, shapes)):
            dst[k] = a

    total = lax.psum(loss[0, 0], ("x", "y", "c"))
    return (total, dx.reshape(1, t, D), *[grads[k] for k in _NAMES], *[delta[k] for k in _NAMES],
            *[new_m[k] for k in _NAMES], *[new_v[k] for k in _NAMES])
```

```python
import functools

import numpy as np
import jax
import jax.numpy as jnp
from jax import lax
from jax.experimental import pallas as pl
from jax.experimental.pallas import tpu as pltpu

F32, BF16 = jnp.float32, jnp.bfloat16
S = jax.ShapeDtypeStruct

D = 1024
INW = 3072
GMW = 256
RETW = 512
CVW = 256
HEADS = 4
DH = 128
C = 128
KW = 31
HALO = 16
FFH = 2816
NDEV = 8
FFB = 2 * FFH // NDEV
EPS = 1e-6
LAYERS = 2
SCALE = DH ** -0.5
VMEM_LIMIT = 56 * 1024 * 1024

ADAM_LR, ADAM_B1, ADAM_B2, ADAM_EPS, ADAM_WD, ADAM_STEP = 0.001, 0.9, 0.999, 1e-08, 0.01, 10

_SQRT_HALF = 0.7071067811865476
_INV_SQRT_2PI = 0.3989422804014327


def _params(*sem):
    return pltpu.CompilerParams(dimension_semantics=sem or None, vmem_limit_bytes=VMEM_LIMIT)


def _resident(shape, index_map):
    return pl.BlockSpec(shape, index_map, pipeline_mode=pl.Buffered(1))


def _dot(a, b):
    return jnp.dot(a, b, preferred_element_type=F32)


def _dot_nt(a, b):
    return lax.dot_general(a, b, (((1,), (1,)), ((), ())), preferred_element_type=F32)


def _dot_tn(a, b):
    return lax.dot_general(a, b, (((0,), (0,)), ((), ())), preferred_element_type=F32)


def _sigmoid(x):
    return 1.0 / (1.0 + jnp.exp(-x))


def _gelu_and_grad(x):
    cdf = 0.5 * (1.0 + lax.erf(x * _SQRT_HALF))
    return x * cdf, cdf + x * jnp.exp(-0.5 * x * x) * _INV_SQRT_2PI


def _silu_and_grad(x):
    s = _sigmoid(x)
    return x * s, s * (1.0 + x * (1.0 - s))


def _standardize(x):
    mu = jnp.mean(x, axis=-1, keepdims=True)
    d = x - mu
    rstd = lax.rsqrt(jnp.mean(d * d, axis=-1, keepdims=True) + EPS)
    return d * rstd, rstd


def _standardize_bwd(dxhat, xhat, rstd):
    m1 = jnp.mean(dxhat, axis=-1, keepdims=True)
    m2 = jnp.mean(dxhat * xhat, axis=-1, keepdims=True)
    return rstd * (dxhat - m1 - xhat * m2)


def _rms(x):
    return lax.rsqrt(jnp.mean(x * x, axis=-1, keepdims=True) + EPS)


def _rmsnorm_bwd(dy, x, r, g):
    u = dy * g
    return r * u - x * (r * r * r) * jnp.mean(u * x, axis=-1, keepdims=True)


def _col_sum(a):
    return jnp.sum(a, axis=0, keepdims=True)


def _rot(t, cos2, sin2):
    return t * cos2 + pltpu.roll(t, DH // 2, axis=1) * sin2


def _rot_t(dt, cos2, sin2):
    return dt * cos2 + pltpu.roll(dt * sin2, DH // 2, axis=1)


MESH = pl.DeviceIdType.MESH
_HBM = pl.BlockSpec(memory_space=pltpu.HBM)
_ANY = pl.BlockSpec(memory_space=pl.ANY)


def _place():
    x, y, c = lax.axis_index("x"), lax.axis_index("y"), lax.axis_index("c")
    return x, y, c, ((1 - x, y), (x, 1 - y), (1 - x, 1 - y))


def _slot(full, kind, width, i):
    if kind == "cols":
        return full.at[:, pl.ds(pl.multiple_of(i * width, 128), width)]
    if kind == "rows":
        return full.at[pl.ds(pl.multiple_of(i * width, 16), width), :]
    return full.at[i]


class _Gather:
    def __init__(self, units):
        self.units = units
        self.inputs = [u[0] for u in units]
        self.out_shape = []
        for src, kind in units:
            r, c = src.shape[-2:]
            shape = {"cols": (r, NDEV * c), "rows": (NDEV * r, c), "lead": (NDEV,) + src.shape}[kind]
            self.out_shape.append(S(shape, src.dtype))
        n = len(units)
        self.scratch = [pltpu.SemaphoreType.DMA((n, 7)), pltpu.SemaphoreType.DMA((n, 7)), pltpu.SemaphoreType.DMA((n,))]

    def run(self, phase, ins, outs, scr):
        ssem, rsem, lsem = scr
        x, y, c, chips = _place()
        me, sib = 4 * x + 2 * y + c, (x, y, 1 - c)
        idx = lambda chip, core: 4 * chip[0] + 2 * chip[1] + core
        for u, (src_arr, kind) in enumerate(self.units):
            src, full = ins[u], outs[u]
            width = src_arr.shape[-1] if kind == "cols" else src_arr.shape[-2]
            slot = functools.partial(_slot, full, kind, width)

            def copy(k, block, to, from_src=False):
                return pltpu.make_async_remote_copy(src_ref=src if from_src else slot(block), dst_ref=slot(block),
                                                    send_sem=ssem.at[u, k], recv_sem=rsem.at[u, k],
                                                    device_id=to, device_id_type=MESH)

            mine = lambda: pltpu.make_async_copy(src, slot(me), lsem.at[u])
            first = lambda: [copy(0, me, sib, True)] + [copy(1 + j, me, (*chip, c), True) for j, chip in enumerate(chips)]
            passed = lambda j: copy(4 + j, idx(chips[j], c), sib)
            if phase == "start":
                mine().start()
                for cp in first():
                    cp.start()
            elif phase == "forward":
                for j, chip in enumerate(chips):
                    copy(1 + j, idx(chip, c), sib).wait_recv()
                    passed(j).start()
            else:
                copy(0, idx((x, y), 1 - c), sib).wait_recv()
                for j, chip in enumerate(chips):
                    copy(4 + j, idx(chip, 1 - c), sib).wait_recv()
                for cp in first() + [passed(j) for j in range(3)]:
                    cp.wait_send()
                mine().wait()


class _Scatter:
    def __init__(self, units):
        self.units = units
        self.inputs = list(units)
        self.out_shape = [S(u.shape, u.dtype) for u in units]
        n = len(units)
        self.scratch = [pltpu.SemaphoreType.DMA((n, 3)), pltpu.SemaphoreType.DMA((n, 3)), pltpu.SemaphoreType.DMA((n,))]

    def run(self, phase, ins, outs, scr):
        ssem, rsem, lsem = scr
        x, y, c, chips = _place()
        myq = 2 * x + y
        for u in range(len(self.units)):
            h, p = ins[u], outs[u]

            def copy(k, chip, send_to_them):
                q = 2 * chip[0] + chip[1]
                return pltpu.make_async_remote_copy(src_ref=h.at[q], dst_ref=p.at[myq if send_to_them else q],
                                                    send_sem=ssem.at[u, k], recv_sem=rsem.at[u, k],
                                                    device_id=(*chip, c), device_id_type=MESH)

            mine = lambda: pltpu.make_async_copy(h.at[myq], p.at[myq], lsem.at[u])
            sends = lambda: [copy(k, chip, True) for k, chip in enumerate(chips)]
            if phase == "start":
                mine().start()
                for cp in sends():
                    cp.start()
            elif phase == "finish":
                for k, chip in enumerate(chips):
                    copy(k, chip, False).wait_recv()
                for cp in sends():
                    cp.wait_send()
                mine().wait()


class _Comms:
    def __init__(self, parts):
        self.parts = parts
        self.inputs = [a for p in parts for a in p.inputs]
        self.out_shape = [a for p in parts for a in p.out_shape]
        self.scratch = [a for p in parts for a in p.scratch]

    def run(self, phase, ins, outs, scr):
        i = o = s = 0
        for p in self.parts:
            ni, no, ns = len(p.inputs), len(p.out_shape), len(p.scratch)
            p.run(phase, ins[i:i + ni], outs[o:o + no], scr[s:s + ns])
            i, o, s = i + ni, o + no, s + ns


def _launch(body, grid, in_specs, out_specs, out_shape, scratch, args, name, sem, comm=None):
    if comm is None:
        outs = pl.pallas_call(body, grid=grid, name=name, in_specs=in_specs, out_specs=out_specs, out_shape=out_shape,
                              scratch_shapes=scratch, compiler_params=_params(*sem))(*args)
        return list(outs), []
    n_in, n_out, n_scr = len(args), len(out_shape), len(scratch)
    ci, co = len(comm.inputs), len(comm.out_shape)
    nsteps = grid[0]
    fwd_step = (3 * nsteps) // 4

    def hosted(*refs):
        a = refs[:n_in]
        ca = refs[n_in:n_in + ci]
        o = refs[n_in + ci:n_in + ci + n_out]
        cout = refs[n_in + ci + n_out:n_in + ci + n_out + co]
        s = refs[n_in + ci + n_out + co:n_in + ci + n_out + co + n_scr]
        cs = refs[n_in + ci + n_out + co + n_scr:]
        step = pl.program_id(0)

        @pl.when(step == 0)
        def _():
            comm.run("start", ca, cout, cs)

        body(*a, *o, *s)

        @pl.when(step == fwd_step)
        def _():
            comm.run("forward", ca, cout, cs)

        @pl.when(step == nsteps - 1)
        def _():
            comm.run("finish", ca, cout, cs)

    outs = pl.pallas_call(
        hosted, grid=grid, name=name, in_specs=list(in_specs) + [_HBM] * ci, out_specs=list(out_specs) + [_HBM] * co,
        out_shape=list(out_shape) + comm.out_shape, scratch_shapes=list(scratch) + comm.scratch,
        compiler_params=_params(*["arbitrary"] * len(grid)))(*args, *comm.inputs)
    return list(outs[:n_out]), list(outs[n_out:])


def _comm_only(comm, name):
    ci, co = len(comm.inputs), len(comm.out_shape)

    def body(*refs):
        ca, cout, cs = refs[:ci], refs[ci:ci + co], refs[ci + co:]
        for phase in ("start", "forward", "finish"):
            comm.run(phase, ca, cout, cs)

    return pl.pallas_call(body, name=name, in_specs=[_HBM] * ci, out_specs=[_HBM] * co, out_shape=comm.out_shape,
                          scratch_shapes=comm.scratch, compiler_params=_params())(*comm.inputs)


def _ret_consts():
    idx = np.arange(C, dtype=np.float32)
    gf = (1.0 - np.exp2(-5.0 - np.arange(HEADS, dtype=np.float32))).astype(np.float32)
    out = {}
    for name, gamma, fwd in (("f", gf, True), ("b", gf[::-1].copy(), False)):
        lg = np.log(gamma).astype(np.float32)[:, None]
        diff = idx[:, None] - idx[None, :]
        if fwd:
            mask = diff >= 0
            dist = np.where(mask, diff, 0.0)
            zeta = np.exp(lg * (C - 1 - idx))
            xi = np.exp(lg * (idx + 1))
        else:
            mask = diff < 0
            dist = np.where(mask, -diff, 0.0)
            zeta = np.exp(lg * idx)
            xi = np.exp(lg * (C - idx))
        dm = np.where(mask[None], np.exp(lg[:, :, None] * dist[None]), 0.0).astype(np.float32)
        bc = lambda vec: np.ascontiguousarray(np.broadcast_to(vec.astype(np.float32)[:, :, None], (HEADS, C, DH)))
        out[name] = dict(D=dm, XI=bc(xi), ZETA=bc(zeta), gC=[float(v) for v in np.exp(lg[:, 0] * C).astype(np.float32)])
    return out


def _rope_tables(t):
    pos = jnp.arange(t, dtype=F32)
    half = DH // 2
    inv_freq = 10000.0 ** (-jnp.arange(half, dtype=F32) / half)
    ang = pos[:, None] * inv_freq[None, :]
    cos, sin = jnp.cos(ang), jnp.sin(ang)
    return jnp.concatenate([cos, cos], axis=1), jnp.concatenate([-sin, sin], axis=1)


def _f_inproj(x, g1, w, l, name, comm=None):
    t = x.shape[0]
    tm = 512

    def body(x_ref, g_ref, w_ref, proj_ref, h_ref):
        xv = x_ref[...]
        h = (xv * _rms(xv) * g_ref[...]).astype(BF16)
        h_ref[...] = h
        for nb in range(INW // 512):
            cs = slice(nb * 512, (nb + 1) * 512)
            proj_ref[:, cs] = _dot(h, w_ref[:, cs]).astype(BF16)

    return _launch(
        body, (t // tm,),
        [pl.BlockSpec((tm, D), lambda i: (i, 0)),
         pl.BlockSpec((None, 1, D), lambda i: (l, 0, 0)),
         _resident((D, INW), lambda i: (0, 0))],
        [pl.BlockSpec((tm, INW), lambda i: (i, 0)), pl.BlockSpec((tm, D), lambda i: (i, 0))],
        [S((t, INW), BF16), S((t, D), BF16)], [], (x, g1, w), name, ("parallel",), comm)


def _gm_chunk_fwd(u, v, lng, lnb, ws_ref, bias):
    au, dau = _gelu_and_grad(u)
    av, dav = _gelu_and_grad(v)
    vhat, rstd = _standardize(av)
    vn = (vhat * lng + lnb).astype(BF16)
    head = lax.broadcasted_iota(jnp.int32, (C, GMW), 1) // (GMW // HEADS)
    mixed = bias
    for h in range(HEADS):
        mixed = mixed + jnp.where(head == h, _dot(ws_ref[h], vn), 0.0)
    return au, dau, dav, vhat, rstd, vn, mixed, head


def _f_gm(proj, lng, lnb, ws_bf, bias, l, name):
    t = proj.shape[0]
    tm = 512

    def body(p_ref, lng_ref, lnb_ref, ws_ref, bias_ref, y_ref):
        for ci in range(tm // C):
            rows = slice(ci * C, (ci + 1) * C)
            u = p_ref[rows, 0:GMW].astype(F32)
            v = p_ref[rows, GMW:2 * GMW].astype(F32)
            au, _, _, _, _, _, mixed, _ = _gm_chunk_fwd(u, v, lng_ref[...], lnb_ref[...], ws_ref, bias_ref[...])
            y_ref[rows, :] = (au * mixed).astype(BF16)

    return pl.pallas_call(
        body, grid=(t // tm,), name=name,
        in_specs=[pl.BlockSpec((tm, 2 * GMW), lambda i: (i, 0)),
                  pl.BlockSpec((None, 1, GMW), lambda i: (l, 0, 0)),
                  pl.BlockSpec((None, 1, GMW), lambda i: (l, 0, 0)),
                  pl.BlockSpec((None, HEADS, C, C), lambda i: (l, 0, 0, 0)),
                  pl.BlockSpec((None, C, GMW), lambda i: (l, 0, 0))],
        out_specs=pl.BlockSpec((tm, GMW), lambda i: (i, 0)),
        out_shape=S((t, GMW), BF16),
        compiler_params=_params("parallel"),
    )(proj, lng, lnb, ws_bf, bias)


def _ret_qkv(q_ref, k_ref, v_ref, cos_ref, sin_ref, h):
    sl = slice(h * DH, (h + 1) * DH)
    cos2, sin2 = cos_ref[...], sin_ref[...]
    qr = _rot(q_ref[:, sl].astype(F32), cos2, sin2)
    kr = _rot(k_ref[:, sl].astype(F32), cos2, sin2) * SCALE
    return sl, qr, kr, v_ref[:, sl], cos2, sin2


def _f_ret(proj, cos2, sin2, rc, name, comm=None):
    t = proj.shape[0]
    n = t // C

    def body(qf, kf, vf, cf, sf, qb, kb, vb, cb, sb, Df, XIf, ZEf, Db, XIb, ZEb,
             of_ref, ob_ref, Sf_out, Sb_out, Sf, Sb):
        @pl.when(pl.program_id(0) == 0)
        def _():
            Sf[...] = jnp.zeros_like(Sf)
            Sb[...] = jnp.zeros_like(Sb)

        def one(q_ref, k_ref, v_ref, cos_ref, sin_ref, D_ref, XI_ref, ZE_ref, gC, S_scr, o_ref, S_out):
            for h in range(HEADS):
                sl, qr, kr, vh, _, _ = _ret_qkv(q_ref, k_ref, v_ref, cos_ref, sin_ref, h)
                st = S_scr[h]
                S_out[h] = st
                p = _dot_nt(qr.astype(BF16), kr.astype(BF16)) * D_ref[h]
                o = _dot(p.astype(BF16), vh) + _dot((qr * XI_ref[h]).astype(BF16), st.astype(BF16))
                o_ref[:, sl] = o
                S_scr[h] = gC[h] * st + _dot_tn((kr * ZE_ref[h]).astype(BF16), vh)

        one(qf, kf, vf, cf, sf, Df, XIf, ZEf, rc["f"]["gC"], Sf, of_ref, Sf_out)
        one(qb, kb, vb, cb, sb, Db, XIb, ZEb, rc["b"]["gC"], Sb, ob_ref, Sb_out)

    fw = lambda i: i
    bw = lambda i: n - 1 - i

    def qkv_specs(ix):
        return [pl.BlockSpec((C, RETW), lambda i, cb=cb: (ix(i), cb)) for cb in (1, 2, 3)] + \
               [pl.BlockSpec((C, DH), lambda i: (ix(i), 0)), pl.BlockSpec((C, DH), lambda i: (ix(i), 0))]

    const = lambda: pl.BlockSpec((HEADS, C, DH), lambda i: (0, 0, 0))
    return _launch(
        body, (n,),
        qkv_specs(fw) + qkv_specs(bw) + [const() for _ in range(6)],
        [pl.BlockSpec((C, RETW), lambda i: (fw(i), 0)), pl.BlockSpec((C, RETW), lambda i: (bw(i), 0)),
         pl.BlockSpec((None, HEADS, DH, DH), lambda i: (fw(i), 0, 0, 0)),
         pl.BlockSpec((None, HEADS, DH, DH), lambda i: (bw(i), 0, 0, 0))],
        [S((t, RETW), F32), S((t, RETW), F32), S((n, HEADS, DH, DH), F32), S((n, HEADS, DH, DH), F32)],
        [pltpu.VMEM((HEADS, DH, DH), F32), pltpu.VMEM((HEADS, DH, DH), F32)],
        (proj, proj, proj, cos2, sin2, proj, proj, proj, cos2, sin2,
         rc["f"]["D"], rc["f"]["XI"], rc["f"]["ZETA"], rc["b"]["D"], rc["b"]["XI"], rc["b"]["ZETA"]),
        name, ("arbitrary",), comm)


def _conv_halo_specs(t, tm, width, col):
    r = tm // HALO
    last = t // HALO - 1
    return [pl.BlockSpec((HALO, width), lambda i: (jnp.maximum(i * r - 1, 0), col)),
            pl.BlockSpec((tm, width), lambda i: (i, col)),
            pl.BlockSpec((HALO, width), lambda i: (jnp.minimum((i + 1) * r, last), col))]


def _fill_ext(ext, prev, cur, nxt, i, nt, tm):
    ext[0:HALO, :] = jnp.where(i > 0, prev, 0.0)
    ext[HALO:HALO + tm, :] = cur
    ext[HALO + tm:2 * HALO + tm, :] = jnp.where(i < nt - 1, nxt, 0.0)


def _glu(a_ref, g_ref):
    return a_ref[...].astype(F32) * _sigmoid(g_ref[...].astype(F32))


def _f_conv(proj, cw, cb, lng, lnb, l, name):
    t = proj.shape[0]
    tm = 256
    nt = t // tm
    rb = 64

    def body(ap, ac, an, gp, gc, gn, cw_ref, cb_ref, lng_ref, lnb_ref, c_ref, y_ref, hext):
        i = pl.program_id(0)
        _fill_ext(hext, _glu(ap, gp), _glu(ac, gc), _glu(an, gn), i, nt, tm)
        for r0 in range(0, tm, rb):
            acc = jnp.zeros((rb, CVW), F32) + cb_ref[...]
            for j in range(KW):
                acc = acc + cw_ref[j:j + 1, :] * hext[pl.ds(r0 + j + 1, rb), :]
            c_ref[r0:r0 + rb, :] = acc
            chat, _ = _standardize(acc)
            z = chat * lng_ref[...] + lnb_ref[...]
            y_ref[r0:r0 + rb, :] = (z * _sigmoid(z)).astype(BF16)

    vec = lambda: pl.BlockSpec((None, 1, CVW), lambda i: (l, 0, 0))
    return pl.pallas_call(
        body, grid=(nt,), name=name,
        in_specs=_conv_halo_specs(t, tm, CVW, 10) + _conv_halo_specs(t, tm, CVW, 11) +
                 [pl.BlockSpec((None, 32, CVW), lambda i: (l, 0, 0)), vec(), vec(), vec()],
        out_specs=[pl.BlockSpec((tm, CVW), lambda i: (i, 0)), pl.BlockSpec((tm, CVW), lambda i: (i, 0))],
        out_shape=[S((t, CVW), F32), S((t, CVW), BF16)],
        scratch_shapes=[pltpu.VMEM((tm + 2 * HALO, CVW), F32)],
        compiler_params=_params("parallel"),
    )(proj, proj, proj, proj, proj, proj, cw, cb, lng, lnb)


def _f_mixout(x, y_gm, y_cv, o_f, o_b, proj, w, name):
    t = x.shape[0]
    tm = 512

    def body(x_ref, ygm_ref, ycv_ref, of_ref, ob_ref, g_ref, w_ref, xm_ref, ycat_ref):
        ycat_ref[:, 0:GMW] = ygm_ref[...]
        ycat_ref[:, GMW + RETW:D] = ycv_ref[...]
        for h in range(HEADS):
            sl = slice(h * DH, (h + 1) * DH)
            ohat, _ = _standardize(of_ref[:, sl] + ob_ref[:, sl])
            g = g_ref[:, sl].astype(F32)
            ycat_ref[:, GMW + h * DH:GMW + (h + 1) * DH] = (ohat * (g * _sigmoid(g))).astype(BF16)
        xm_ref[...] = x_ref[...] + _dot(ycat_ref[...], w_ref[...])

    return pl.pallas_call(
        body, grid=(t // tm,), name=name,
        in_specs=[pl.BlockSpec((tm, D), lambda i: (i, 0)),
                  pl.BlockSpec((tm, GMW), lambda i: (i, 0)),
                  pl.BlockSpec((tm, CVW), lambda i: (i, 0)),
                  pl.BlockSpec((tm, RETW), lambda i: (i, 0)),
                  pl.BlockSpec((tm, RETW), lambda i: (i, 0)),
                  pl.BlockSpec((tm, RETW), lambda i: (i, 4)),
                  _resident((D, D), lambda i: (0, 0))],
        out_specs=[pl.BlockSpec((tm, D), lambda i: (i, 0)), pl.BlockSpec((tm, D), lambda i: (i, 0))],
        out_shape=[S((t, D), F32), S((t, D), BF16)],
        compiler_params=_params("parallel"),
    )(x, y_gm, y_cv, o_f, o_b, proj, w)


def _f_ffn(xm, g2, w1, w2, l, name, comm=None):
    t = xm.shape[0]
    tm = 256
    half = NDEV // 2

    def body(x_ref, g_ref, w1_ref, w2_ref, xo_ref, h_ref, gu_ref, act_ref):
        xv = x_ref[...]
        h = (xv * _rms(xv) * g_ref[...]).astype(BF16)
        h_ref[...] = h
        acc = xv
        for j in range(half):
            gate = _dot(h, w1_ref[j])
            up = _dot(h, w1_ref[half + j])
            gu_ref[j] = gate.astype(BF16)
            gu_ref[half + j] = up.astype(BF16)
            a = ((gate * _sigmoid(gate)) * up).astype(BF16)
            act_ref[j] = a
            acc = acc + _dot(a, w2_ref[j * FFB:(j + 1) * FFB, :])
        xo_ref[...] = acc

    return _launch(
        body, (t // tm,),
        [pl.BlockSpec((tm, D), lambda i: (i, 0)),
         pl.BlockSpec((None, 1, D), lambda i: (l, 0, 0)),
         _resident((NDEV, D, FFB), lambda i: (0, 0, 0)),
         _resident((FFH, D), lambda i: (0, 0))],
        [pl.BlockSpec((tm, D), lambda i: (i, 0)), pl.BlockSpec((tm, D), lambda i: (i, 0)),
         pl.BlockSpec((NDEV, tm, FFB), lambda i: (0, i, 0)), pl.BlockSpec((half, tm, FFB), lambda i: (0, i, 0))],
        [S((t, D), F32), S((t, D), BF16), S((NDEV, t, FFB), BF16), S((half, t, FFB), BF16)],
        [], (xm, g2, w1, w2), name, ("parallel",), comm)


def _b_loss(x, fg, tgt, name):
    t = x.shape[0]
    tm = 512

    def body(x_ref, g_ref, t_ref, loss_ref, dx_ref, dg_ref):
        @pl.when(pl.program_id(0) == 0)
        def _():
            loss_ref[...] = jnp.zeros_like(loss_ref)
            dg_ref[...] = jnp.zeros_like(dg_ref)

        xv = x_ref[...]
        r = _rms(xv)
        xr = xv * r
        err = xr * g_ref[...] - t_ref[...]
        loss_ref[...] += (0.5 / D) * _col_sum(jnp.sum(err * err, axis=1, keepdims=True))
        dy = err * (1.0 / D)
        dg_ref[...] += _col_sum(dy * xr)
        dx_ref[...] = _rmsnorm_bwd(dy, xv, r, g_ref[...])

    return pl.pallas_call(
        body, grid=(t // tm,), name=name,
        in_specs=[pl.BlockSpec((tm, D), lambda i: (i, 0)), pl.BlockSpec((1, D), lambda i: (0, 0)),
                  pl.BlockSpec((tm, D), lambda i: (i, 0))],
        out_specs=[pl.BlockSpec((1, 1), lambda i: (0, 0)), pl.BlockSpec((tm, D), lambda i: (i, 0)),
                   pl.BlockSpec((1, D), lambda i: (0, 0))],
        out_shape=[S((1, 1), F32), S((t, D), F32), S((1, D), F32)],
        compiler_params=_params("arbitrary"),
    )(x, fg, tgt)


def _b_ffn(dxo, xm, g2, gu, w1, w2, l, name, comm=None):
    t = xm.shape[0]
    tm = 256
    half = NDEV // 2

    def body(dxo_ref, x_ref, g_ref, gu_ref, w1_ref, w2_ref, dgu_ref, dxm_ref, dxb_ref, dg_ref):
        @pl.when(pl.program_id(0) == 0)
        def _():
            dg_ref[...] = jnp.zeros_like(dg_ref)

        dxo = dxo_ref[...]
        dxb = dxo.astype(BF16)
        dxb_ref[...] = dxb
        dh = jnp.zeros((tm, D), F32)
        for j in range(half):
            dact = _dot_nt(dxb, w2_ref[j * FFB:(j + 1) * FFB, :])
            gate = gu_ref[j].astype(F32)
            up = gu_ref[half + j].astype(F32)
            sg, dsg = _silu_and_grad(gate)
            dgate = (dact * up * dsg).astype(BF16)
            dup = (dact * sg).astype(BF16)
            dgu_ref[j] = dgate
            dgu_ref[half + j] = dup
            dh = dh + _dot_nt(dgate, w1_ref[j]) + _dot_nt(dup, w1_ref[half + j])
        xv = x_ref[...]
        r = _rms(xv)
        dg_ref[...] += _col_sum(dh * xv * r)
        dxm_ref[...] = dxo + _rmsnorm_bwd(dh, xv, r, g_ref[...])

    return _launch(
        body, (t // tm,),
        [pl.BlockSpec((tm, D), lambda i: (i, 0)), pl.BlockSpec((tm, D), lambda i: (i, 0)),
         pl.BlockSpec((None, 1, D), lambda i: (l, 0, 0)),
         pl.BlockSpec((NDEV, tm, FFB), lambda i: (0, i, 0)),
         _resident((NDEV, D, FFB), lambda i: (0, 0, 0)),
         _resident((FFH, D), lambda i: (0, 0))],
        [pl.BlockSpec((NDEV, tm, FFB), lambda i: (0, i, 0)), pl.BlockSpec((tm, D), lambda i: (i, 0)),
         pl.BlockSpec((tm, D), lambda i: (i, 0)), pl.BlockSpec((1, D), lambda i: (0, 0))],
        [S((NDEV, t, FFB), BF16), S((t, D), F32), S((t, D), BF16), S((1, D), F32)],
        [], (dxo, xm, g2, gu, w1, w2), name, ("arbitrary",), comm)


def _mm_tn(a, b, pieces, a_mode, b_mode, name):
    bt = 512

    def spec(arr, mode):
        if mode == "shared":
            return arr.shape[0], arr.shape[1], pl.BlockSpec((bt, arr.shape[1]), lambda j, tt: (tt, 0))
        if mode == "cols":
            w = arr.shape[1] // pieces
            return arr.shape[0], w, pl.BlockSpec((bt, w), lambda j, tt: (tt, j))
        return arr.shape[1], arr.shape[2], pl.BlockSpec((None, bt, arr.shape[2]), lambda j, tt: (j, tt, 0))

    t, ka, a_spec = spec(a, a_mode)
    _, nb, b_spec = spec(b, b_mode)
    nt = t // bt

    def body(a_ref, b_ref, o_ref, acc):
        tt = pl.program_id(1)

        @pl.when(tt == 0)
        def _():
            acc[...] = jnp.zeros_like(acc)

        acc[...] += _dot_tn(a_ref[...], b_ref[...])

        @pl.when(tt == nt - 1)
        def _():
            o_ref[...] = acc[...].astype(BF16)

    return pl.pallas_call(
        body, grid=(pieces, nt), name=name,
        in_specs=[a_spec, b_spec],
        out_specs=pl.BlockSpec((None, ka, nb), lambda j, tt: (j, 0, 0)),
        out_shape=S((pieces, ka, nb), BF16),
        scratch_shapes=[pltpu.VMEM((ka, nb), F32)],
        compiler_params=_params("parallel", "arbitrary"),
    )(a, b)


def _b_mixout(dxm, w, o_f, o_b, proj, c, lng, lnb, l, name):
    t = dxm.shape[0]
    tm = 256

    def body(dxm_ref, w_ref, of_ref, ob_ref, g_ref, c_ref, lng_ref, lnb_ref,
             dxb_ref, dygm_ref, dO_ref, dg_ref, dc_ref, dlg_ref, dlb_ref, dcb_ref):
        @pl.when(pl.program_id(0) == 0)
        def _():
            dlg_ref[...] = jnp.zeros_like(dlg_ref)
            dlb_ref[...] = jnp.zeros_like(dlb_ref)
            dcb_ref[...] = jnp.zeros_like(dcb_ref)

        dxb = dxm_ref[...].astype(BF16)
        dxb_ref[...] = dxb
        dy = _dot_nt(dxb, w_ref[...])
        dygm_ref[...] = dy[:, 0:GMW]
        for h in range(HEADS):
            sl = slice(h * DH, (h + 1) * DH)
            ohat, rstd = _standardize(of_ref[:, sl] + ob_ref[:, sl])
            sg, dsg = _silu_and_grad(g_ref[:, sl].astype(F32))
            dyr = dy[:, GMW + h * DH:GMW + (h + 1) * DH]
            dg_ref[:, sl] = (dyr * ohat * dsg).astype(BF16)
            dO_ref[:, sl] = _standardize_bwd(dyr * sg, ohat, rstd)
        chat, rstd = _standardize(c_ref[...])
        z = chat * lng_ref[...] + lnb_ref[...]
        _, dsz = _silu_and_grad(z)
        dz = dy[:, GMW + RETW:D] * dsz
        dlg_ref[...] += _col_sum(dz * chat)
        dlb_ref[...] += _col_sum(dz)
        dc = _standardize_bwd(dz * lng_ref[...], chat, rstd)
        dcb_ref[...] += _col_sum(dc)
        dc_ref[...] = dc

    vec = lambda: pl.BlockSpec((None, 1, CVW), lambda i: (l, 0, 0))
    acc = lambda: pl.BlockSpec((1, CVW), lambda i: (0, 0))
    return pl.pallas_call(
        body, grid=(t // tm,), name=name,
        in_specs=[pl.BlockSpec((tm, D), lambda i: (i, 0)),
                  _resident((D, D), lambda i: (0, 0)),
                  pl.BlockSpec((tm, RETW), lambda i: (i, 0)),
                  pl.BlockSpec((tm, RETW), lambda i: (i, 0)),
                  pl.BlockSpec((tm, RETW), lambda i: (i, 4)),
                  pl.BlockSpec((tm, CVW), lambda i: (i, 0)), vec(), vec()],
        out_specs=[pl.BlockSpec((tm, D), lambda i: (i, 0)), pl.BlockSpec((tm, GMW), lambda i: (i, 0)),
                   pl.BlockSpec((tm, RETW), lambda i: (i, 0)), pl.BlockSpec((tm, RETW), lambda i: (i, 0)),
                   pl.BlockSpec((tm, CVW), lambda i: (i, 0)), acc(), acc(), acc()],
        out_shape=[S((t, D), BF16), S((t, GMW), F32), S((t, RETW), F32), S((t, RETW), BF16), S((t, CVW), F32),
                   S((1, CVW), F32), S((1, CVW), F32), S((1, CVW), F32)],
        compiler_params=_params("arbitrary"),
    )(dxm, w, o_f, o_b, proj, c, lng, lnb)


def _b_gm(proj, dy, lng, lnb, ws_bf, wst_bf, bias, l, name):
    t = proj.shape[0]
    tm = 512
    nt = t // tm

    def body(p_ref, dy_ref, lng_ref, lnb_ref, ws_ref, wst_ref, bias_ref,
             duv_ref, dws_ref, dbias_ref, dbs_ref, dlg_ref, dlb_ref):
        @pl.when(pl.program_id(0) == 0)
        def _():
            dws_ref[...] = jnp.zeros_like(dws_ref)
            dbias_ref[...] = jnp.zeros_like(dbias_ref)
            dbs_ref[...] = jnp.zeros_like(dbs_ref)
            dlg_ref[...] = jnp.zeros_like(dlg_ref)
            dlb_ref[...] = jnp.zeros_like(dlb_ref)

        for ci in range(tm // C):
            rows = slice(ci * C, (ci + 1) * C)
            u = p_ref[rows, 0:GMW].astype(F32)
            v = p_ref[rows, GMW:2 * GMW].astype(F32)
            au, dau, dav, vhat, rstd, vn, mixed, head = _gm_chunk_fwd(u, v, lng_ref[...], lnb_ref[...], ws_ref, bias_ref[...])
            dyc = dy_ref[rows, :]
            dmixed = dyc * au
            dmb = dmixed.astype(BF16)
            dbias_ref[...] += dmixed
            dvn = jnp.zeros((C, GMW), F32)
            for h in range(HEADS):
                dws_ref[h] += _dot_nt(jnp.where(head == h, dmixed, 0.0).astype(BF16), vn)
                dvn = dvn + jnp.where(head == h, _dot(wst_ref[h], dmb), 0.0)
            dlg_ref[...] += _col_sum(dvn * vhat)
            dlb_ref[...] += _col_sum(dvn)
            dav_in = _standardize_bwd(dvn * lng_ref[...], vhat, rstd)
            duv_ref[rows, 0:GMW] = (dyc * mixed * dau).astype(BF16)
            duv_ref[rows, GMW:2 * GMW] = (dav_in * dav).astype(BF16)

        @pl.when(pl.program_id(0) == nt - 1)
        def _():
            head = lax.broadcasted_iota(jnp.int32, (C, GMW), 1) // (GMW // HEADS)
            lane = lax.broadcasted_iota(jnp.int32, (C, 128), 1)
            fold = jnp.zeros((C, 128), F32)
            for h in range(HEADS):
                col = jnp.sum(jnp.where(head == h, dbias_ref[...], 0.0), axis=1, keepdims=True)
                fold = jnp.where(lane == h, col, fold)
            dbs_ref[...] = fold

    vec = lambda: pl.BlockSpec((None, 1, GMW), lambda i: (l, 0, 0))
    mats = lambda: pl.BlockSpec((None, HEADS, C, C), lambda i: (l, 0, 0, 0))
    return pl.pallas_call(
        body, grid=(nt,), name=name,
        in_specs=[pl.BlockSpec((tm, 2 * GMW), lambda i: (i, 0)), pl.BlockSpec((tm, GMW), lambda i: (i, 0)),
                  vec(), vec(), mats(), mats(), pl.BlockSpec((None, C, GMW), lambda i: (l, 0, 0))],
        out_specs=[pl.BlockSpec((tm, 2 * GMW), lambda i: (i, 0)),
                   pl.BlockSpec((HEADS, C, C), lambda i: (0, 0, 0)),
                   pl.BlockSpec((C, GMW), lambda i: (0, 0)), pl.BlockSpec((C, 128), lambda i: (0, 0)),
                   pl.BlockSpec((1, GMW), lambda i: (0, 0)), pl.BlockSpec((1, GMW), lambda i: (0, 0))],
        out_shape=[S((t, 2 * GMW), BF16), S((HEADS, C, C), F32), S((C, GMW), F32), S((C, 128), F32),
                   S((1, GMW), F32), S((1, GMW), F32)],
        compiler_params=_params("arbitrary"),
    )(proj, dy, lng, lnb, ws_bf, wst_bf, bias)


def _b_conv(proj, dc, cw, l, name):
    t = proj.shape[0]
    tm = 256
    nt = t // tm
    rb = 64

    def body(ap, ac, an, gp, gc, gn, dp, dcur, dn, cw_ref, dag_ref, dcw_ref, hext, dext):
        i = pl.program_id(0)

        @pl.when(i == 0)
        def _():
            dcw_ref[...] = jnp.zeros_like(dcw_ref)

        _fill_ext(hext, _glu(ap, gp), _glu(ac, gc), _glu(an, gn), i, nt, tm)
        _fill_ext(dext, dp[...], dcur[...], dn[...], i, nt, tm)
        for j in range(KW):
            dcw_ref[j:j + 1, :] += _col_sum(dcur[...] * hext[pl.ds(j + 1, tm), :])
        for r0 in range(0, tm, rb):
            dh = jnp.zeros((rb, CVW), F32)
            for j in range(KW):
                dh = dh + cw_ref[j:j + 1, :] * dext[pl.ds(r0 + 2 * HALO - 1 - j, rb), :]
            a = ac[r0:r0 + rb, :].astype(F32)
            s = _sigmoid(gc[r0:r0 + rb, :].astype(F32))
            dag_ref[r0:r0 + rb, 0:CVW] = (dh * s).astype(BF16)
            dag_ref[r0:r0 + rb, CVW:2 * CVW] = (dh * a * s * (1.0 - s)).astype(BF16)

    dspecs = _conv_halo_specs(t, tm, CVW, 0)
    return pl.pallas_call(
        body, grid=(nt,), name=name,
        in_specs=_conv_halo_specs(t, tm, CVW, 10) + _conv_halo_specs(t, tm, CVW, 11) + dspecs +
                 [pl.BlockSpec((None, 32, CVW), lambda i: (l, 0, 0))],
        out_specs=[pl.BlockSpec((tm, 2 * CVW), lambda i: (i, 0)), pl.BlockSpec((32, CVW), lambda i: (0, 0))],
        out_shape=[S((t, 2 * CVW), BF16), S((32, CVW), F32)],
        scratch_shapes=[pltpu.VMEM((tm + 2 * HALO, CVW), F32), pltpu.VMEM((tm + 2 * HALO, CVW), F32)],
        compiler_params=_params("arbitrary"),
    )(proj, proj, proj, proj, proj, proj, dc, dc, dc, cw)


def _b_ret(proj, cos2, sin2, dO, s_f, s_b, rc, name, comm=None):
    t = proj.shape[0]
    n = t // C

    def body(qb, kb, vb, cb, sb, dOb, Sb_in, qf, kf, vf, cf, sf, dOf, Sf_in, Db, XIb, ZEb, Df, XIf, ZEf,
             dqb, dkb, dvb, dqf, dkf, dvf, Gb, Gf):
        @pl.when(pl.program_id(0) == 0)
        def _():
            Gb[...] = jnp.zeros_like(Gb)
            Gf[...] = jnp.zeros_like(Gf)

        def one(q_ref, k_ref, v_ref, cos_ref, sin_ref, dO_ref, S_ref, D_ref, XI_ref, ZE_ref, gC, G_scr, dq_ref, dk_ref, dv_ref):
            for h in range(HEADS):
                sl, qr, kr, vh, cos2v, sin2v = _ret_qkv(q_ref, k_ref, v_ref, cos_ref, sin_ref, h)
                qh, kh = qr.astype(BF16), kr.astype(BF16)
                dOh = dO_ref[:, sl].astype(BF16)
                st = S_ref[h].astype(BF16)
                gr = G_scr[h]
                grb = gr.astype(BF16)
                dm = D_ref[h]
                p = (_dot_nt(qh, kh) * dm).astype(BF16)
                dp = (_dot_nt(dOh, vh) * dm).astype(BF16)
                dqr = _dot(dp, kh) + XI_ref[h] * _dot_nt(dOh, st)
                dkr = (_dot_tn(dp, qh) + ZE_ref[h] * _dot_nt(vh, grb)) * SCALE
                dv = _dot_tn(p, dOh) + _dot((kr * ZE_ref[h]).astype(BF16), grb)
                G_scr[h] = gC[h] * gr + _dot_tn((qr * XI_ref[h]).astype(BF16), dOh)
                dq_ref[:, sl] = _rot_t(dqr, cos2v, sin2v).astype(BF16)
                dk_ref[:, sl] = _rot_t(dkr, cos2v, sin2v).astype(BF16)
                dv_ref[:, sl] = dv.astype(BF16)

        one(qb, kb, vb, cb, sb, dOb, Sb_in, Db, XIb, ZEb, rc["b"]["gC"], Gb, dqb, dkb, dvb)
        one(qf, kf, vf, cf, sf, dOf, Sf_in, Df, XIf, ZEf, rc["f"]["gC"], Gf, dqf, dkf, dvf)

    up = lambda i: i
    down = lambda i: n - 1 - i

    def in_specs(ix):
        return [pl.BlockSpec((C, RETW), lambda i, cb=cb: (ix(i), cb)) for cb in (1, 2, 3)] + \
               [pl.BlockSpec((C, DH), lambda i: (ix(i), 0)), pl.BlockSpec((C, DH), lambda i: (ix(i), 0)),
                pl.BlockSpec((C, RETW), lambda i: (ix(i), 0)),
                pl.BlockSpec((None, HEADS, DH, DH), lambda i: (ix(i), 0, 0, 0))]

    const = lambda: pl.BlockSpec((HEADS, C, DH), lambda i: (0, 0, 0))
    outs = lambda ix: [pl.BlockSpec((C, RETW), lambda i: (ix(i), 0)) for _ in range(3)]
    return _launch(
        body, (n,),
        in_specs(up) + in_specs(down) + [const() for _ in range(6)],
        outs(up) + outs(down),
        [S((t, RETW), BF16) for _ in range(6)],
        [pltpu.VMEM((HEADS, DH, DH), F32), pltpu.VMEM((HEADS, DH, DH), F32)],
        (proj, proj, proj, cos2, sin2, dO, s_b, proj, proj, proj, cos2, sin2, dO, s_f,
         rc["b"]["D"], rc["b"]["XI"], rc["b"]["ZETA"], rc["f"]["D"], rc["f"]["XI"], rc["f"]["ZETA"]),
        name, ("arbitrary",), comm)


def _b_inproj(d_uv, dqkv, d_g, d_ag, w, x, g1, dxm, l, name):
    t = x.shape[0]
    tm = 256
    dq_b, dk_b, dv_b, dq_f, dk_f, dv_f = dqkv

    def body(duv_ref, dqb, dkb, dvb, dqf, dkf, dvf, dg_ref, dag_ref, w_ref, x_ref, g_ref, dxm_ref,
             dp_ref, dx_ref, dn_ref):
        @pl.when(pl.program_id(0) == 0)
        def _():
            dn_ref[...] = jnp.zeros_like(dn_ref)

        dp_ref[:, 0:512] = duv_ref[...]
        for k, (a, b) in enumerate(((dqb, dqf), (dkb, dkf), (dvb, dvf))):
            dp_ref[:, 512 * (k + 1):512 * (k + 2)] = (a[...].astype(F32) + b[...].astype(F32)).astype(BF16)
        dp_ref[:, 2048:2560] = dg_ref[...]
        dp_ref[:, 2560:3072] = dag_ref[...]
        dh = _dot_nt(dp_ref[...], w_ref[...])
        xv = x_ref[...]
        r = _rms(xv)
        dn_ref[...] += _col_sum(dh * xv * r)
        dx_ref[...] = dxm_ref[...] + _rmsnorm_bwd(dh, xv, r, g_ref[...])

    half = lambda: pl.BlockSpec((tm, 512), lambda i: (i, 0))
    full = lambda: pl.BlockSpec((tm, D), lambda i: (i, 0))
    return pl.pallas_call(
        body, grid=(t // tm,), name=name,
        in_specs=[half() for _ in range(9)] +
                 [_resident((D, INW), lambda i: (0, 0)), full(),
                  pl.BlockSpec((None, 1, D), lambda i: (l, 0, 0)), full()],
        out_specs=[pl.BlockSpec((tm, INW), lambda i: (i, 0)), full(), pl.BlockSpec((1, D), lambda i: (0, 0))],
        out_shape=[S((t, INW), BF16), S((t, D), F32), S((1, D), F32)],
        compiler_params=_params("arbitrary"),
    )(d_uv, dq_b, dk_b, dv_b, dq_f, dk_f, dv_f, d_g, d_ag, w, x, g1, dxm)


def _pair_exchange(gs, name):
    n = len(gs)

    def body(*refs):
        g, q, ssem, rsem = refs[:n], refs[n:2 * n], refs[2 * n], refs[2 * n + 1]
        x, y, c, _ = _place()
        copies = [pltpu.make_async_remote_copy(src_ref=g[u].at[2 * chip + (1 - c)], dst_ref=q[u].at[chip],
                                               send_sem=ssem.at[u, chip], recv_sem=rsem.at[u, chip],
                                               device_id=(x, y, 1 - c), device_id_type=MESH)
                  for u in range(n) for chip in range(4)]
        for cp in copies:
            cp.start()
        for cp in copies:
            cp.wait_recv()
        for cp in copies:
            cp.wait_send()

    return pl.pallas_call(body, name=name, in_specs=[_HBM] * n, out_specs=[_HBM] * n,
                          out_shape=[S((4,) + g.shape[1:], BF16) for g in gs],
                          scratch_shapes=[pltpu.SemaphoreType.DMA((n, 4)), pltpu.SemaphoreType.DMA((n, 4))],
                          compiler_params=_params())(*gs)


def _pair_add(g, q, name):
    _, mm, nn = g.shape
    bm = 256 if mm % 256 == 0 else mm

    def body(g_ref, q_ref, h_ref):
        h_ref[...] = (g_ref[lax.axis_index("c")].astype(F32) + q_ref[...].astype(F32)).astype(BF16)

    return pl.pallas_call(
        body, grid=(4, mm // bm), name=name,
        in_specs=[pl.BlockSpec((None, 2, bm, nn), lambda qq, i: (qq, 0, i, 0)),
                  pl.BlockSpec((None, bm, nn), lambda qq, i: (qq, i, 0))],
        out_specs=pl.BlockSpec((None, bm, nn), lambda qq, i: (qq, i, 0)),
        out_shape=S((4, mm, nn), BF16),
        compiler_params=_params("parallel", "parallel"),
    )(g.reshape(4, 2, mm, nn), q)


def _pair_reduce(named, l):
    names = [k for k, _ in named]
    qs = _pair_exchange([g for _, g in named], f"pair_exchange_{names[0]}_{l}")
    return [_pair_add(g, q, f"pair_add_{k}_{l}") for (k, g), q in zip(named, qs)]


_BIG = ("w_in", "w_out", "w_ffn_in", "w_ffn_out")
_KIND = dict(w_in="cols", w_out="rows", w_ffn_in="lead", w_ffn_out="rows")
CWP = 128
SMALL_ROWS = 152


def _step(x, tgt, wts, sh, cw_pad):
    t = x.shape[0]
    rc = _ret_consts()
    cos2, sin2 = _rope_tables(t)
    n1 = wts["norm1_g"].reshape(LAYERS, 1, D)
    n2 = wts["norm2_g"].reshape(LAYERS, 1, D)
    gm_lng = wts["gm_ln_g"].reshape(LAYERS, 1, GMW)
    gm_lnb = wts["gm_ln_b"].reshape(LAYERS, 1, GMW)
    ws_bf = wts["gm_ws"].astype(BF16)
    wst_bf = jnp.swapaxes(wts["gm_ws"], 2, 3).astype(BF16)
    bias = jnp.repeat(jnp.swapaxes(wts["gm_bs"], 1, 2), GMW // HEADS, axis=2)
    cb = wts["conv_b"].reshape(LAYERS, 1, CVW)
    cv_lng = wts["conv_ln_g"].reshape(LAYERS, 1, CVW)
    cv_lnb = wts["conv_ln_b"].reshape(LAYERS, 1, CVW)
    unit = lambda f, l: (sh[f][l], _KIND[f])
    cshard = CVW // NDEV

    full = {f: [None] * LAYERS for f in _BIG}
    full["w_in"][0], cw_all = _comm_only(_Gather([unit("w_in", 0), (cw_pad, "lead")]), "gather_first")
    cw = jnp.transpose(cw_all[:, :, :, :cshard], (1, 2, 0, 3)).reshape(LAYERS, 32, CVW)

    saved = []
    for l in range(LAYERS):
        first = l == 0
        (proj, h1), got = _f_inproj(x, n1, full["w_in"][l], l, f"f_inproj_{l}",
                                    _Gather([unit("w_out", 0), unit("w_ffn_out", 0)]) if first else None)
        if first:
            full["w_out"][0], full["w_ffn_out"][0] = got
        y_gm = _f_gm(proj, gm_lng, gm_lnb, ws_bf, bias, l, f"f_gm_{l}")
        (o_f, o_b, s_f, s_b), got = _f_ret(proj, cos2, sin2, rc, f"f_ret_{l}", _Gather([unit("w_ffn_in", l)]))
        full["w_ffn_in"][l], = got
        c, y_cv = _f_conv(proj, cw, cb, cv_lng, cv_lnb, l, f"f_conv_{l}")
        xm, ycat = _f_mixout(x, y_gm, y_cv, o_f, o_b, proj, full["w_out"][l], f"f_mixout_{l}")
        (xo, h2, gu, act), got = _f_ffn(xm, n2, full["w_ffn_in"][l], full["w_ffn_out"][l], l, f"f_ffn_{l}",
                                        _Gather([unit("w_in", 1), unit("w_out", 1), unit("w_ffn_out", 1)]) if first else None)
        if first:
            full["w_in"][1], full["w_out"][1], full["w_ffn_out"][1] = got
        saved.append(dict(x=x, proj=proj, h1=h1, o_f=o_f, o_b=o_b, s_f=s_f, s_b=s_b, c=c, xm=xm, ycat=ycat,
                          h2=h2, gu=gu, act=act))
        x = xo

    loss, dx, d_final = _b_loss(x, wts["final_g"].reshape(1, D), tgt, "b_loss")

    parts = {f: [None] * LAYERS for f in _BIG}
    small = [None] * LAYERS
    upper = None
    for l in reversed(range(LAYERS)):
        sv = saved[l]
        (dgu, dxm, dxo_bf, d_n2), got = _b_ffn(dx, sv["xm"], n2, sv["gu"], full["w_ffn_in"][l], full["w_ffn_out"][l], l,
                                               f"b_ffn_{l}", _Scatter(upper) if upper else None)
        if upper:
            for f, p in zip(_BIG, got):
                parts[f][l + 1] = p
        g_f2 = _mm_tn(sv["act"], dxo_bf, NDEV // 2, "lead", "shared", f"g_ffn_out_{l}").reshape(NDEV, FFH // NDEV, D)
        g_f1 = _mm_tn(sv["h2"], dgu, NDEV, "shared", "lead", f"g_ffn_in_{l}")
        dxm_bf, dy_gm, dO, d_g, dc, d_cvlg, d_cvlb, d_cb = _b_mixout(
            dxm, full["w_out"][l], sv["o_f"], sv["o_b"], sv["proj"], sv["c"], cv_lng, cv_lnb, l, f"b_mixout_{l}")
        g_out = _mm_tn(sv["ycat"], dxm_bf, 1, "shared", "shared", f"g_out_{l}").reshape(NDEV, D // NDEV, D)
        last = l == 0
        early = _pair_reduce([("w_out", g_out), ("w_ffn_in", g_f1), ("w_ffn_out", g_f2)], l) if last else None
        d_uv, d_ws, _, d_bs_fold, d_gmlg, d_gmlb = _b_gm(sv["proj"], dy_gm, gm_lng, gm_lnb, ws_bf, wst_bf, bias, l, f"b_gm_{l}")
        d_ag, d_cw = _b_conv(sv["proj"], dc, cw, l, f"b_conv_{l}")
        dqkv, got = _b_ret(sv["proj"], cos2, sin2, dO, sv["s_f"], sv["s_b"], rc, f"b_ret_{l}", _Scatter(early) if last else None)
        if last:
            parts["w_out"][l], parts["w_ffn_in"][l], parts["w_ffn_out"][l] = got
        dproj, dx, d_n1 = _b_inproj(d_uv, dqkv, d_g, d_ag, full["w_in"][l], sv["x"], n1, dxm, l, f"b_inproj_{l}")
        g_in = _mm_tn(sv["h1"], dproj, NDEV, "shared", "cols", f"g_in_{l}")
        if last:
            tail = _pair_reduce([("w_in", g_in)], l)
        else:
            upper = _pair_reduce([("w_in", g_in), ("w_out", g_out), ("w_ffn_in", g_f1), ("w_ffn_out", g_f2)], l)
        small[l] = dict(norm1_g=d_n1[0], gm_ln_g=d_gmlg[0], gm_ln_b=d_gmlb[0], gm_ws=d_ws,
                        gm_bs=d_bs_fold[:, :HEADS].T, conv_w=d_cw[:KW], conv_b=d_cb[0], conv_ln_g=d_cvlg[0],
                        conv_ln_b=d_cvlb[0], norm2_g=d_n2[0])
    small_g = {k: jnp.stack([small[l][k] for l in range(LAYERS)]) for k in small[0]}
    small_g["final_g"] = d_final[0]
    small_buf = _pack([small_g[k] for k in _SMALL], SMALL_ROWS)
    parts["w_in"][0], small_parts = _comm_only(_Comms([_Scatter(tail), _Gather([(small_buf, "lead")])]), "exchange_last")
    return loss, dx, parts, small_parts


def _adamw(w, g, m, v):
    m = ADAM_B1 * m + (1.0 - ADAM_B1) * g
    v = ADAM_B2 * v + (1.0 - ADAM_B2) * (g * g)
    m_hat = m / (1.0 - ADAM_B1 ** ADAM_STEP)
    v_hat = v / (1.0 - ADAM_B2 ** ADAM_STEP)
    return -ADAM_LR * (m_hat / (jnp.sqrt(v_hat) + ADAM_EPS) + ADAM_WD * w), m, v


def _cast_blocks(ws):
    def body(*refs):
        ins, outs = refs[:len(ws)], refs[len(ws):]
        for k, src in enumerate(ins):
            for l in range(LAYERS):
                outs[k * LAYERS + l][...] = src[l].astype(BF16)

    outs = pl.pallas_call(body, name="cast_blocks", out_shape=[S(w.shape[1:], BF16) for w in ws for _ in range(LAYERS)],
                          compiler_params=_params())(*ws)
    return [list(outs[k * LAYERS:(k + 1) * LAYERS]) for k in range(len(ws))]


def _sum_adam(parts, w, m, v, l, prev, name):
    _, mm, nn = parts.shape
    bm = 256 if mm % 256 == 0 else mm

    def body(p_ref, w_ref, m_ref, v_ref, *rest):
        g_ref, d_ref, nm_ref, nv_ref = rest[-4:]
        g = p_ref[0].astype(F32)
        for s in range(1, 4):
            g = g + p_ref[s].astype(F32)
        g_ref[...] = g
        d_ref[...], nm_ref[...], nv_ref[...] = _adamw(w_ref[...], g, m_ref[...], v_ref[...])

    blk = lambda: pl.BlockSpec((None, bm, nn), lambda i: (l, i, 0))
    prev = list(prev) if prev else []
    return pl.pallas_call(
        body, grid=(mm // bm,), name=name,
        in_specs=[pl.BlockSpec((4, bm, nn), lambda i: (0, i, 0)), blk(), blk(), blk()] + [_ANY] * len(prev),
        out_specs=[blk() for _ in range(4)],
        out_shape=[S(w.shape, F32) for _ in range(4)],
        input_output_aliases={4 + j: j for j in range(len(prev))},
        compiler_params=_params("parallel"),
    )(parts, w, m, v, *prev)


def _sum_small(parts):
    def body(p_ref, o_ref):
        g = p_ref[0]
        for s in range(1, NDEV):
            g = g + p_ref[s]
        o_ref[...] = g

    return pl.pallas_call(body, name="sum_small", out_shape=S(parts.shape[1:], F32),
                          compiler_params=_params())(parts)


def _adam_small(g, w, m, v):
    def body(g_ref, w_ref, m_ref, v_ref, d_ref, nm_ref, nv_ref):
        d_ref[...], nm_ref[...], nv_ref[...] = _adamw(w_ref[...], g_ref[...], m_ref[...], v_ref[...])

    return pl.pallas_call(body, name="adam_small", out_shape=[S(g.shape, F32)] * 3, compiler_params=_params())(g, w, m, v)


_SMALL = ("norm1_g", "gm_ln_g", "gm_ln_b", "gm_ws", "gm_bs", "conv_w", "conv_b", "conv_ln_g", "conv_ln_b",
          "norm2_g", "final_g")
_NAMES = ("norm1_g", "w_in", "gm_ln_g", "gm_ln_b", "gm_ws", "gm_bs", "conv_w", "conv_b", "conv_ln_g", "conv_ln_b",
          "w_out", "norm2_g", "w_ffn_in", "w_ffn_out", "final_g")


def _pack(parts, rows):
    flat = jnp.concatenate([p.reshape(-1) for p in parts])
    return jnp.pad(flat, (0, rows * 1024 - flat.shape[0])).reshape(rows, 1024)


def _unpack(buf, shapes):
    flat = buf.reshape(-1)
    out, o = [], 0
    for shp in shapes:
        sz = int(np.prod(shp))
        out.append(flat[o:o + sz].reshape(shp))
        o += sz
    return out


def kernel(x, norm1_g, w_in, gm_ln_g, gm_ln_b, gm_ws, gm_bs, conv_w, conv_b, conv_ln_g, conv_ln_b, w_out, norm2_g, w_ffn_in, w_ffn_out, final_g, loss_target, m_norm1_g, m_w_in, m_gm_ln_g, m_gm_ln_b, m_gm_ws, m_gm_bs, m_conv_w, m_conv_b, m_conv_ln_g, m_conv_ln_b, m_w_out, m_norm2_g, m_w_ffn_in, m_w_ffn_out, m_final_g, v_norm1_g, v_w_in, v_gm_ln_g, v_gm_ln_b, v_gm_ws, v_gm_bs, v_conv_w, v_conv_b, v_conv_ln_g, v_conv_ln_b, v_w_out, v_norm2_g, v_w_ffn_in, v_w_ffn_out, v_final_g):
    w = dict(norm1_g=norm1_g, w_in=w_in, gm_ln_g=gm_ln_g, gm_ln_b=gm_ln_b, gm_ws=gm_ws, gm_bs=gm_bs, conv_w=conv_w,
             conv_b=conv_b, conv_ln_g=conv_ln_g, conv_ln_b=conv_ln_b, w_out=w_out, norm2_g=norm2_g, w_ffn_in=w_ffn_in,
             w_ffn_out=w_ffn_out, final_g=final_g)
    mo = dict(norm1_g=m_norm1_g, w_in=m_w_in, gm_ln_g=m_gm_ln_g, gm_ln_b=m_gm_ln_b, gm_ws=m_gm_ws, gm_bs=m_gm_bs,
              conv_w=m_conv_w, conv_b=m_conv_b, conv_ln_g=m_conv_ln_g, conv_ln_b=m_conv_ln_b, w_out=m_w_out,
              norm2_g=m_norm2_g, w_ffn_in=m_w_ffn_in, w_ffn_out=m_w_ffn_out, final_g=m_final_g)
    vo = dict(norm1_g=v_norm1_g, w_in=v_w_in, gm_ln_g=v_gm_ln_g, gm_ln_b=v_gm_ln_b, gm_ws=v_gm_ws, gm_bs=v_gm_bs,
              conv_w=v_conv_w, conv_b=v_conv_b, conv_ln_g=v_conv_ln_g, conv_ln_b=v_conv_ln_b, w_out=v_w_out,
              norm2_g=v_norm2_g, w_ffn_in=v_w_ffn_in, w_ffn_out=v_w_ffn_out, final_g=v_final_g)
    t = x.shape[1]
    me = 4 * lax.axis_index("x") + 2 * lax.axis_index("y") + lax.axis_index("c")
    cshard = conv_w.shape[2]

    cw_pad = jnp.pad(conv_w, ((0, 0), (0, 32 - KW), (0, CWP - cshard)))
    sh = dict(zip(_BIG, _cast_blocks([w[f] for f in _BIG])))
    loss, dx, parts, small_parts = _step(x.reshape(t, D), loss_target.reshape(t, D), w, sh, cw_pad)

    grads, delta, new_m, new_v = {}, {}, {}, {}
    for f in _BIG:
        outs = None
        for l in reversed(range(LAYERS)):
            outs = _sum_adam(parts[f][l], w[f], mo[f], vo[f], l, outs, f"sum_adam_{f}_{l}")
        grads[f], delta[f], new_m[f], new_v[f] = outs

    full_shapes = [(LAYERS, KW, CVW) if k == "conv_w" else w[k].shape for k in _SMALL]
    for k, g in zip(_SMALL, _unpack(_sum_small(small_parts), full_shapes)):
        grads[k] = lax.dynamic_slice_in_dim(g, me * cshard, cshard, axis=2) if k == "conv_w" else g
    adam_rows = 144
    d_s, m_s, v_s = _adam_small(_pack([grads[k] for k in _SMALL], adam_rows), _pack([w[k] for k in _SMALL], adam_rows),
                                _pack([mo[k] for k in _SMALL], adam_rows), _pack([vo[k] for k in _SMALL], adam_rows))
    shapes = [w[k].shape for k in _SMALL]
    for dst, buf in ((delta, d_s), (new_m, m_s), (new_v, v_s)):
        for k, a in zip(_SMALL, _unpack(buf, shapes)):
            dst[k] = a

    total = lax.psum(loss[0, 0], ("x", "y", "c"))
    return (total, dx.reshape(1, t, D), *[grads[k] for k in _NAMES], *[delta[k] for k in _NAMES],
            *[new_m[k] for k in _NAMES], *[new_v[k] for k in _NAMES])
```

```python
import functools

import numpy as np
import jax
import jax.numpy as jnp
from jax import lax
from jax.experimental import pallas as pl
from jax.experimental.pallas import tpu as pltpu

F32, BF16 = jnp.float32, jnp.bfloat16
S = jax.ShapeDtypeStruct

D = 1024
INW = 3072
GMW = 256
RETW = 512
CVW = 256
HEADS = 4
DH = 128
C = 128
KW = 31
HALO = 16
FFH = 2816
NDEV = 8
FFB = 2 * FFH // NDEV
EPS = 1e-6
LAYERS = 2
SCALE = DH ** -0.5
VMEM_LIMIT = 56 * 1024 * 1024

ADAM_LR, ADAM_B1, ADAM_B2, ADAM_EPS, ADAM_WD, ADAM_STEP = 0.001, 0.9, 0.999, 1e-08, 0.01, 10

_SQRT_HALF = 0.7071067811865476
_INV_SQRT_2PI = 0.3989422804014327


def _params(*sem):
    return pltpu.CompilerParams(dimension_semantics=sem or None, vmem_limit_bytes=VMEM_LIMIT)


def _resident(shape, index_map):
    return pl.BlockSpec(shape, index_map, pipeline_mode=pl.Buffered(1))


def _dot(a, b):
    return jnp.dot(a, b, preferred_element_type=F32)


def _dot_nt(a, b):
    return lax.dot_general(a, b, (((1,), (1,)), ((), ())), preferred_element_type=F32)


def _dot_tn(a, b):
    return lax.dot_general(a, b, (((0,), (0,)), ((), ())), preferred_element_type=F32)


def _sigmoid(x):
    return 1.0 / (1.0 + jnp.exp(-x))


def _gelu_and_grad(x):
    cdf = 0.5 * (1.0 + lax.erf(x * _SQRT_HALF))
    return x * cdf, cdf + x * jnp.exp(-0.5 * x * x) * _INV_SQRT_2PI


def _silu_and_grad(x):
    s = _sigmoid(x)
    return x * s, s * (1.0 + x * (1.0 - s))


def _standardize(x):
    mu = jnp.mean(x, axis=-1, keepdims=True)
    d = x - mu
    rstd = lax.rsqrt(jnp.mean(d * d, axis=-1, keepdims=True) + EPS)
    return d * rstd, rstd


def _standardize_bwd(dxhat, xhat, rstd):
    m1 = jnp.mean(dxhat, axis=-1, keepdims=True)
    m2 = jnp.mean(dxhat * xhat, axis=-1, keepdims=True)
    return rstd * (dxhat - m1 - xhat * m2)


def _rms(x):
    return lax.rsqrt(jnp.mean(x * x, axis=-1, keepdims=True) + EPS)


def _rmsnorm_bwd(dy, x, r, g):
    u = dy * g
    return r * u - x * (r * r * r) * jnp.mean(u * x, axis=-1, keepdims=True)


def _col_sum(a):
    return jnp.sum(a, axis=0, keepdims=True)


def _rot(t, cos2, sin2):
    return t * cos2 + pltpu.roll(t, DH // 2, axis=1) * sin2


def _rot_t(dt, cos2, sin2):
    return dt * cos2 + pltpu.roll(dt * sin2, DH // 2, axis=1)


MESH = pl.DeviceIdType.MESH
_HBM = pl.BlockSpec(memory_space=pltpu.HBM)
_ANY = pl.BlockSpec(memory_space=pl.ANY)


def _place():
    x, y, c = lax.axis_index("x"), lax.axis_index("y"), lax.axis_index("c")
    return x, y, c, ((1 - x, y), (x, 1 - y), (1 - x, 1 - y))


def _slot(full, kind, width, i):
    if kind == "cols":
        return full.at[:, pl.ds(pl.multiple_of(i * width, 128), width)]
    if kind == "rows":
        return full.at[pl.ds(pl.multiple_of(i * width, 16), width), :]
    return full.at[i]


class _Gather:
    def __init__(self, units):
        self.units = units
        self.inputs = [u[0] for u in units]
        self.out_shape = []
        for src, kind in units:
            r, c = src.shape[-2:]
            shape = {"cols": (r, NDEV * c), "rows": (NDEV * r, c), "lead": (NDEV,) + src.shape}[kind]
            self.out_shape.append(S(shape, src.dtype))
        n = len(units)
        self.scratch = [pltpu.SemaphoreType.DMA((n, 7)), pltpu.SemaphoreType.DMA((n, 7)), pltpu.SemaphoreType.DMA((n,))]

    def run(self, phase, ins, outs, scr):
        ssem, rsem, lsem = scr
        x, y, c, chips = _place()
        me, sib = 4 * x + 2 * y + c, (x, y, 1 - c)
        idx = lambda chip, core: 4 * chip[0] + 2 * chip[1] + core
        for u, (src_arr, kind) in enumerate(self.units):
            src, full = ins[u], outs[u]
            width = src_arr.shape[-1] if kind == "cols" else src_arr.shape[-2]
            slot = functools.partial(_slot, full, kind, width)

            def copy(k, block, to, from_src=False):
                return pltpu.make_async_remote_copy(src_ref=src if from_src else slot(block), dst_ref=slot(block),
                                                    send_sem=ssem.at[u, k], recv_sem=rsem.at[u, k],
                                                    device_id=to, device_id_type=MESH)

            mine = lambda: pltpu.make_async_copy(src, slot(me), lsem.at[u])
            first = lambda: [copy(0, me, sib, True)] + [copy(1 + j, me, (*chip, c), True) for j, chip in enumerate(chips)]
            passed = lambda j: copy(4 + j, idx(chips[j], c), sib)
            if phase == "start":
                mine().start()
                for cp in first():
                    cp.start()
            elif phase == "forward":
                for j, chip in enumerate(chips):
                    copy(1 + j, idx(chip, c), sib).wait_recv()
                    passed(j).start()
            else:
                copy(0, idx((x, y), 1 - c), sib).wait_recv()
                for j, chip in enumerate(chips):
                    copy(4 + j, idx(chip, 1 - c), sib).wait_recv()
                for cp in first() + [passed(j) for j in range(3)]:
                    cp.wait_send()
                mine().wait()


class _Scatter:
    def __init__(self, units):
        self.units = units
        self.inputs = list(units)
        self.out_shape = [S(u.shape, u.dtype) for u in units]
        n = len(units)
        self.scratch = [pltpu.SemaphoreType.DMA((n, 3)), pltpu.SemaphoreType.DMA((n, 3)), pltpu.SemaphoreType.DMA((n,))]

    def run(self, phase, ins, outs, scr):
        ssem, rsem, lsem = scr
        x, y, c, chips = _place()
        myq = 2 * x + y
        for u in range(len(self.units)):
            h, p = ins[u], outs[u]

            def copy(k, chip, send_to_them):
                q = 2 * chip[0] + chip[1]
                return pltpu.make_async_remote_copy(src_ref=h.at[q], dst_ref=p.at[myq if send_to_them else q],
                                                    send_sem=ssem.at[u, k], recv_sem=rsem.at[u, k],
                                                    device_id=(*chip, c), device_id_type=MESH)

            mine = lambda: pltpu.make_async_copy(h.at[myq], p.at[myq], lsem.at[u])
            sends = lambda: [copy(k, chip, True) for k, chip in enumerate(chips)]
            if phase == "start":
                mine().start()
                for cp in sends():
                    cp.start()
            elif phase == "finish":
                for k, chip in enumerate(chips):
                    copy(k, chip, False).wait_recv()
                for cp in sends():
                    cp.wait_send()
                mine().wait()


class _Comms:
    def __init__(self, parts):
        self.parts = parts
        self.inputs = [a for p in parts for a in p.inputs]
        self.out_shape = [a for p in parts for a in p.out_shape]
        self.scratch = [a for p in parts for a in p.scratch]

    def run(self, phase, ins, outs, scr):
        i = o = s = 0
        for p in self.parts:
            ni, no, ns = len(p.inputs), len(p.out_shape), len(p.scratch)
            p.run(phase, ins[i:i + ni], outs[o:o + no], scr[s:s + ns])
            i, o, s = i + ni, o + no, s + ns


def _launch(body, grid, in_specs, out_specs, out_shape, scratch, args, name, sem, comm=None):
    if comm is None:
        outs = pl.pallas_call(body, grid=grid, name=name, in_specs=in_specs, out_specs=out_specs, out_shape=out_shape,
                              scratch_shapes=scratch, compiler_params=_params(*sem))(*args)
        return list(outs), []
    n_in, n_out, n_scr = len(args), len(out_shape), len(scratch)
    ci, co = len(comm.inputs), len(comm.out_shape)
    nsteps = grid[0]
    fwd_step = (3 * nsteps) // 4

    def hosted(*refs):
        a = refs[:n_in]
        ca = refs[n_in:n_in + ci]
        o = refs[n_in + ci:n_in + ci + n_out]
        cout = refs[n_in + ci + n_out:n_in + ci + n_out + co]
        s = refs[n_in + ci + n_out + co:n_in + ci + n_out + co + n_scr]
        cs = refs[n_in + ci + n_out + co + n_scr:]
        step = pl.program_id(0)

        @pl.when(step == 0)
        def _():
            comm.run("start", ca, cout, cs)

        body(*a, *o, *s)

        @pl.when(step == fwd_step)
        def _():
            comm.run("forward", ca, cout, cs)

        @pl.when(step == nsteps - 1)
        def _():
            comm.run("finish", ca, cout, cs)

    outs = pl.pallas_call(
        hosted, grid=grid, name=name, in_specs=list(in_specs) + [_HBM] * ci, out_specs=list(out_specs) + [_HBM] * co,
        out_shape=list(out_shape) + comm.out_shape, scratch_shapes=list(scratch) + comm.scratch,
        compiler_params=_params(*["arbitrary"] * len(grid)))(*args, *comm.inputs)
    return list(outs[:n_out]), list(outs[n_out:])


def _comm_only(comm, name):
    ci, co = len(comm.inputs), len(comm.out_shape)

    def body(*refs):
        ca, cout, cs = refs[:ci], refs[ci:ci + co], refs[ci + co:]
        for phase in ("start", "forward", "finish"):
            comm.run(phase, ca, cout, cs)

    return pl.pallas_call(body, name=name, in_specs=[_HBM] * ci, out_specs=[_HBM] * co, out_shape=comm.out_shape,
                          scratch_shapes=comm.scratch, compiler_params=_params())(*comm.inputs)


def _ret_consts():
    idx = np.arange(C, dtype=np.float32)
    gf = (1.0 - np.exp2(-5.0 - np.arange(HEADS, dtype=np.float32))).astype(np.float32)
    out = {}
    for name, gamma, fwd in (("f", gf, True), ("b", gf[::-1].copy(), False)):
        lg = np.log(gamma).astype(np.float32)[:, None]
        diff = idx[:, None] - idx[None, :]
        if fwd:
            mask = diff >= 0
            dist = np.where(mask, diff, 0.0)
            zeta = np.exp(lg * (C - 1 - idx))
            xi = np.exp(lg * (idx + 1))
        else:
            mask = diff < 0
            dist = np.where(mask, -diff, 0.0)
            zeta = np.exp(lg * idx)
            xi = np.exp(lg * (C - idx))
        dm = np.where(mask[None], np.exp(lg[:, :, None] * dist[None]), 0.0).astype(np.float32)
        bc = lambda vec: np.ascontiguousarray(np.broadcast_to(vec.astype(np.float32)[:, :, None], (HEADS, C, DH)))
        out[name] = dict(D=dm, XI=bc(xi), ZETA=bc(zeta), gC=[float(v) for v in np.exp(lg[:, 0] * C).astype(np.float32)])
    return out


def _rope_tables(t):
    half = DH // 2
    inv_freq = (np.float32(10000.0) ** (-np.arange(half, dtype=np.float32) / np.float32(half))).astype(np.float32)
    ang = (np.arange(t, dtype=np.float32)[:, None] * inv_freq[None, :]).astype(np.float64)
    cos, sin = np.cos(ang).astype(np.float32), np.sin(ang).astype(np.float32)
    return np.concatenate([cos, cos], axis=1), np.concatenate([-sin, sin], axis=1)


def _f_inproj(x, g1, w, l, name, comm=None):
    t = x.shape[0]
    tm = 512

    def body(x_ref, g_ref, w_ref, proj_ref, ht_ref):
        xv = x_ref[...]
        h = (xv * _rms(xv) * g_ref[...]).astype(BF16)
        ht_ref[...] = h.T
        for nb in range(INW // 512):
            cs = slice(nb * 512, (nb + 1) * 512)
            proj_ref[:, cs] = _dot(h, w_ref[:, cs]).astype(BF16)

    return _launch(
        body, (t // tm,),
        [pl.BlockSpec((tm, D), lambda i: (i, 0)),
         pl.BlockSpec((None, 1, D), lambda i: (l, 0, 0)),
         _resident((D, INW), lambda i: (0, 0))],
        [pl.BlockSpec((tm, INW), lambda i: (i, 0)), pl.BlockSpec((D, tm), lambda i: (0, i))],
        [S((t, INW), BF16), S((D, t), BF16)], [], (x, g1, w), name, ("parallel",), comm)


def _gm_chunk_fwd(u, v, lng, lnb, ws_ref, bias):
    au, dau = _gelu_and_grad(u)
    av, dav = _gelu_and_grad(v)
    vhat, rstd = _standardize(av)
    vn = (vhat * lng + lnb).astype(BF16)
    head = lax.broadcasted_iota(jnp.int32, (C, GMW), 1) // (GMW // HEADS)
    mixed = bias
    for h in range(HEADS):
        mixed = mixed + jnp.where(head == h, _dot(ws_ref[h], vn), 0.0)
    return au, dau, dav, vhat, rstd, vn, mixed, head


def _f_gm(proj, lng, lnb, ws_bf, bias, l, name):
    t = proj.shape[0]
    tm = 512

    def body(p_ref, lng_ref, lnb_ref, ws_ref, bias_ref, y_ref):
        for ci in range(tm // C):
            rows = slice(ci * C, (ci + 1) * C)
            u = p_ref[rows, 0:GMW].astype(F32)
            v = p_ref[rows, GMW:2 * GMW].astype(F32)
            au, _, _, _, _, _, mixed, _ = _gm_chunk_fwd(u, v, lng_ref[...], lnb_ref[...], ws_ref, bias_ref[...])
            y_ref[rows, :] = (au * mixed).astype(BF16)

    return pl.pallas_call(
        body, grid=(t // tm,), name=name,
        in_specs=[pl.BlockSpec((tm, 2 * GMW), lambda i: (i, 0)),
                  pl.BlockSpec((None, 1, GMW), lambda i: (l, 0, 0)),
                  pl.BlockSpec((None, 1, GMW), lambda i: (l, 0, 0)),
                  pl.BlockSpec((None, HEADS, C, C), lambda i: (l, 0, 0, 0)),
                  pl.BlockSpec((None, C, GMW), lambda i: (l, 0, 0))],
        out_specs=pl.BlockSpec((tm, GMW), lambda i: (i, 0)),
        out_shape=S((t, GMW), BF16),
        compiler_params=_params("parallel"),
    )(proj, lng, lnb, ws_bf, bias)


def _ret_qkv(q_ref, k_ref, v_ref, cos_ref, sin_ref, h):
    sl = slice(h * DH, (h + 1) * DH)
    cos2, sin2 = cos_ref[...], sin_ref[...]
    qr = _rot(q_ref[:, sl].astype(F32), cos2, sin2)
    kr = _rot(k_ref[:, sl].astype(F32), cos2, sin2) * SCALE
    return sl, qr, kr, v_ref[:, sl], cos2, sin2


def _f_ret(proj, cos2, sin2, rc, name, comm=None):
    t = proj.shape[0]
    n = t // C

    def body(qf, kf, vf, cf, sf, qb, kb, vb, cb, sb, Df, XIf, ZEf, Db, XIb, ZEb,
             of_ref, ob_ref, Sf_out, Sb_out, Sf, Sb):
        @pl.when(pl.program_id(0) == 0)
        def _():
            Sf[...] = jnp.zeros_like(Sf)
            Sb[...] = jnp.zeros_like(Sb)

        def one(q_ref, k_ref, v_ref, cos_ref, sin_ref, D_ref, XI_ref, ZE_ref, gC, S_scr, o_ref, S_out):
            for h in range(HEADS):
                sl, qr, kr, vh, _, _ = _ret_qkv(q_ref, k_ref, v_ref, cos_ref, sin_ref, h)
                st = S_scr[h]
                S_out[h] = st
                p = _dot_nt(qr.astype(BF16), kr.astype(BF16)) * D_ref[h]
                o = _dot(p.astype(BF16), vh) + _dot((qr * XI_ref[h]).astype(BF16), st.astype(BF16))
                o_ref[:, sl] = o
                S_scr[h] = gC[h] * st + _dot_tn((kr * ZE_ref[h]).astype(BF16), vh)

        one(qf, kf, vf, cf, sf, Df, XIf, ZEf, rc["f"]["gC"], Sf, of_ref, Sf_out)
        one(qb, kb, vb, cb, sb, Db, XIb, ZEb, rc["b"]["gC"], Sb, ob_ref, Sb_out)

    fw = lambda i: i
    bw = lambda i: n - 1 - i

    def qkv_specs(ix):
        return [pl.BlockSpec((C, RETW), lambda i, cb=cb: (ix(i), cb)) for cb in (1, 2, 3)] + \
               [pl.BlockSpec((C, DH), lambda i: (ix(i), 0)), pl.BlockSpec((C, DH), lambda i: (ix(i), 0))]

    const = lambda: pl.BlockSpec((HEADS, C, DH), lambda i: (0, 0, 0))
    return _launch(
        body, (n,),
        qkv_specs(fw) + qkv_specs(bw) + [const() for _ in range(6)],
        [pl.BlockSpec((C, RETW), lambda i: (fw(i), 0)), pl.BlockSpec((C, RETW), lambda i: (bw(i), 0)),
         pl.BlockSpec((None, HEADS, DH, DH), lambda i: (fw(i), 0, 0, 0)),
         pl.BlockSpec((None, HEADS, DH, DH), lambda i: (bw(i), 0, 0, 0))],
        [S((t, RETW), F32), S((t, RETW), F32), S((n, HEADS, DH, DH), F32), S((n, HEADS, DH, DH), F32)],
        [pltpu.VMEM((HEADS, DH, DH), F32), pltpu.VMEM((HEADS, DH, DH), F32)],
        (proj, proj, proj, cos2, sin2, proj, proj, proj, cos2, sin2,
         rc["f"]["D"], rc["f"]["XI"], rc["f"]["ZETA"], rc["b"]["D"], rc["b"]["XI"], rc["b"]["ZETA"]),
        name, ("arbitrary",), comm)


def _conv_halo_specs(t, tm, width, col):
    r = tm // HALO
    last = t // HALO - 1
    return [pl.BlockSpec((HALO, width), lambda i: (jnp.maximum(i * r - 1, 0), col)),
            pl.BlockSpec((tm, width), lambda i: (i, col)),
            pl.BlockSpec((HALO, width), lambda i: (jnp.minimum((i + 1) * r, last), col))]


def _fill_ext(ext, prev, cur, nxt, i, nt, tm):
    ext[0:HALO, :] = jnp.where(i > 0, prev, 0.0)
    ext[HALO:HALO + tm, :] = cur
    ext[HALO + tm:2 * HALO + tm, :] = jnp.where(i < nt - 1, nxt, 0.0)


def _glu(a_ref, g_ref):
    return a_ref[...].astype(F32) * _sigmoid(g_ref[...].astype(F32))


def _f_conv(proj, cw, cb, lng, lnb, l, name):
    t = proj.shape[0]
    tm = 256
    nt = t // tm
    rb = 64

    def body(ap, ac, an, gp, gc, gn, cw_ref, cb_ref, lng_ref, lnb_ref, c_ref, y_ref, hext):
        i = pl.program_id(0)
        _fill_ext(hext, _glu(ap, gp), _glu(ac, gc), _glu(an, gn), i, nt, tm)
        for r0 in range(0, tm, rb):
            acc = jnp.zeros((rb, CVW), F32) + cb_ref[...]
            for j in range(KW):
                acc = acc + cw_ref[j:j + 1, :] * hext[pl.ds(r0 + j + 1, rb), :]
            c_ref[r0:r0 + rb, :] = acc
            chat, _ = _standardize(acc)
            z = chat * lng_ref[...] + lnb_ref[...]
            y_ref[r0:r0 + rb, :] = (z * _sigmoid(z)).astype(BF16)

    vec = lambda: pl.BlockSpec((None, 1, CVW), lambda i: (l, 0, 0))
    return pl.pallas_call(
        body, grid=(nt,), name=name,
        in_specs=_conv_halo_specs(t, tm, CVW, 10) + _conv_halo_specs(t, tm, CVW, 11) +
                 [pl.BlockSpec((None, 32, CVW), lambda i: (l, 0, 0)), vec(), vec(), vec()],
        out_specs=[pl.BlockSpec((tm, CVW), lambda i: (i, 0)), pl.BlockSpec((tm, CVW), lambda i: (i, 0))],
        out_shape=[S((t, CVW), F32), S((t, CVW), BF16)],
        scratch_shapes=[pltpu.VMEM((tm + 2 * HALO, CVW), F32)],
        compiler_params=_params("parallel"),
    )(proj, proj, proj, proj, proj, proj, cw, cb, lng, lnb)


def _f_mixout(x, y_gm, y_cv, o_f, o_b, proj, w, name):
    t = x.shape[0]
    tm = 512

    def body(x_ref, ygm_ref, ycv_ref, of_ref, ob_ref, g_ref, w_ref, xm_ref, ycat_t_ref, ycat):
        ycat[:, 0:GMW] = ygm_ref[...]
        ycat[:, GMW + RETW:D] = ycv_ref[...]
        for h in range(HEADS):
            sl = slice(h * DH, (h + 1) * DH)
            ohat, _ = _standardize(of_ref[:, sl] + ob_ref[:, sl])
            g = g_ref[:, sl].astype(F32)
            ycat[:, GMW + h * DH:GMW + (h + 1) * DH] = (ohat * (g * _sigmoid(g))).astype(BF16)
        yc = ycat[...]
        ycat_t_ref[...] = yc.T
        xm_ref[...] = x_ref[...] + _dot(yc, w_ref[...])

    return pl.pallas_call(
        body, grid=(t // tm,), name=name,
        in_specs=[pl.BlockSpec((tm, D), lambda i: (i, 0)),
                  pl.BlockSpec((tm, GMW), lambda i: (i, 0)),
                  pl.BlockSpec((tm, CVW), lambda i: (i, 0)),
                  pl.BlockSpec((tm, RETW), lambda i: (i, 0)),
                  pl.BlockSpec((tm, RETW), lambda i: (i, 0)),
                  pl.BlockSpec((tm, RETW), lambda i: (i, 4)),
                  _resident((D, D), lambda i: (0, 0))],
        out_specs=[pl.BlockSpec((tm, D), lambda i: (i, 0)), pl.BlockSpec((D, tm), lambda i: (0, i))],
        out_shape=[S((t, D), F32), S((D, t), BF16)],
        scratch_shapes=[pltpu.VMEM((tm, D), BF16)],
        compiler_params=_params("parallel"),
    )(x, y_gm, y_cv, o_f, o_b, proj, w)


def _f_ffn(xm, g2, w1, w2, l, name, comm=None):
    t = xm.shape[0]
    tm = 256
    half = NDEV // 2

    def body(x_ref, g_ref, w1_ref, w2_ref, xo_ref, ht_ref, gu_ref, act_t_ref):
        xv = x_ref[...]
        h = (xv * _rms(xv) * g_ref[...]).astype(BF16)
        ht_ref[...] = h.T
        acc = xv
        for j in range(half):
            gate = _dot(h, w1_ref[j])
            up = _dot(h, w1_ref[half + j])
            gu_ref[j] = gate.astype(BF16)
            gu_ref[half + j] = up.astype(BF16)
            a = ((gate * _sigmoid(gate)) * up).astype(BF16)
            act_t_ref[j] = a.T
            acc = acc + _dot(a, w2_ref[j * FFB:(j + 1) * FFB, :])
        xo_ref[...] = acc

    return _launch(
        body, (t // tm,),
        [pl.BlockSpec((tm, D), lambda i: (i, 0)),
         pl.BlockSpec((None, 1, D), lambda i: (l, 0, 0)),
         _resident((NDEV, D, FFB), lambda i: (0, 0, 0)),
         _resident((FFH, D), lambda i: (0, 0))],
        [pl.BlockSpec((tm, D), lambda i: (i, 0)), pl.BlockSpec((D, tm), lambda i: (0, i)),
         pl.BlockSpec((NDEV, tm, FFB), lambda i: (0, i, 0)), pl.BlockSpec((half, FFB, tm), lambda i: (0, 0, i))],
        [S((t, D), F32), S((D, t), BF16), S((NDEV, t, FFB), BF16), S((half, FFB, t), BF16)],
        [], (xm, g2, w1, w2), name, ("parallel",), comm)


def _b_loss(x, fg, tgt, name):
    t = x.shape[0]
    tm = 512

    def body(x_ref, g_ref, t_ref, loss_ref, dx_ref, dg_ref):
        @pl.when(pl.program_id(0) == 0)
        def _():
            loss_ref[...] = jnp.zeros_like(loss_ref)
            dg_ref[...] = jnp.zeros_like(dg_ref)

        xv = x_ref[...]
        r = _rms(xv)
        xr = xv * r
        err = xr * g_ref[...] - t_ref[...]
        loss_ref[...] += (0.5 / D) * _col_sum(jnp.sum(err * err, axis=1, keepdims=True))
        dy = err * (1.0 / D)
        dg_ref[...] += _col_sum(dy * xr)
        dx_ref[...] = _rmsnorm_bwd(dy, xv, r, g_ref[...])

    return pl.pallas_call(
        body, grid=(t // tm,), name=name,
        in_specs=[pl.BlockSpec((tm, D), lambda i: (i, 0)), pl.BlockSpec((1, D), lambda i: (0, 0)),
                  pl.BlockSpec((tm, D), lambda i: (i, 0))],
        out_specs=[pl.BlockSpec((1, 1), lambda i: (0, 0)), pl.BlockSpec((tm, D), lambda i: (i, 0)),
                   pl.BlockSpec((1, D), lambda i: (0, 0))],
        out_shape=[S((1, 1), F32), S((t, D), F32), S((1, D), F32)],
        compiler_params=_params("arbitrary"),
    )(x, fg, tgt)


def _b_ffn(dxo, xm, g2, gu, w1, w2, l, name, comm=None):
    t = xm.shape[0]
    tm = 256
    half = NDEV // 2

    def body(dxo_ref, x_ref, g_ref, gu_ref, w1_ref, w2_ref, dgu_ref, dxm_ref, dxb_ref, dg_ref):
        @pl.when(pl.program_id(0) == 0)
        def _():
            dg_ref[...] = jnp.zeros_like(dg_ref)

        dxo = dxo_ref[...]
        dxb = dxo.astype(BF16)
        dxb_ref[...] = dxb
        dh = jnp.zeros((tm, D), F32)
        for j in range(half):
            dact = _dot_nt(dxb, w2_ref[j * FFB:(j + 1) * FFB, :])
            gate = gu_ref[j].astype(F32)
            up = gu_ref[half + j].astype(F32)
            sg, dsg = _silu_and_grad(gate)
            dgate = (dact * up * dsg).astype(BF16)
            dup = (dact * sg).astype(BF16)
            dgu_ref[j] = dgate
            dgu_ref[half + j] = dup
            dh = dh + _dot_nt(dgate, w1_ref[j]) + _dot_nt(dup, w1_ref[half + j])
        xv = x_ref[...]
        r = _rms(xv)
        dg_ref[...] += _col_sum(dh * xv * r)
        dxm_ref[...] = dxo + _rmsnorm_bwd(dh, xv, r, g_ref[...])

    return _launch(
        body, (t // tm,),
        [pl.BlockSpec((tm, D), lambda i: (i, 0)), pl.BlockSpec((tm, D), lambda i: (i, 0)),
         pl.BlockSpec((None, 1, D), lambda i: (l, 0, 0)),
         pl.BlockSpec((NDEV, tm, FFB), lambda i: (0, i, 0)),
         _resident((NDEV, D, FFB), lambda i: (0, 0, 0)),
         _resident((FFH, D), lambda i: (0, 0))],
        [pl.BlockSpec((NDEV, tm, FFB), lambda i: (0, i, 0)), pl.BlockSpec((tm, D), lambda i: (i, 0)),
         pl.BlockSpec((tm, D), lambda i: (i, 0)), pl.BlockSpec((1, D), lambda i: (0, 0))],
        [S((NDEV, t, FFB), BF16), S((t, D), F32), S((t, D), BF16), S((1, D), F32)],
        [], (dxo, xm, g2, gu, w1, w2), name, ("arbitrary",), comm)


def _mm_wgrad(at, b, pieces, at_lead, b_mode, group, name):
    bt = 1024
    t = at.shape[-1]
    bt = min(bt, t)
    nt = t // bt
    ka = at.shape[-2]
    if at_lead:
        a_spec = pl.BlockSpec((None, ka, bt), lambda j, tt: (j, 0, tt))
    else:
        a_spec = pl.BlockSpec((ka, bt), lambda j, tt: (0, tt))
    if b_mode == "shared":
        nb, b_spec = b.shape[1], pl.BlockSpec((bt, b.shape[1]), lambda j, tt: (tt, 0))
    elif b_mode == "cols":
        nb = b.shape[1] // pieces
        b_spec = pl.BlockSpec((bt, group * nb), lambda j, tt: (tt, j))
    else:
        nb, b_spec = b.shape[2], pl.BlockSpec((None, bt, b.shape[2]), lambda j, tt: (j, tt, 0))
    assert group == 1 or b_mode == "cols"

    def body(a_ref, b_ref, o_ref, acc):
        tt = pl.program_id(1)

        @pl.when(tt == 0)
        def _():
            acc[...] = jnp.zeros_like(acc)

        acc[...] += _dot(a_ref[...], b_ref[...])

        @pl.when(tt == nt - 1)
        def _():
            for k in range(group):
                o_ref[k] = acc[:, k * nb:(k + 1) * nb].astype(BF16)

    return pl.pallas_call(
        body, grid=(pieces // group, nt), name=name,
        in_specs=[a_spec, b_spec],
        out_specs=pl.BlockSpec((group, ka, nb), lambda j, tt: (j, 0, 0)),
        out_shape=S((pieces, ka, nb), BF16),
        scratch_shapes=[pltpu.VMEM((ka, group * nb), F32)],
        compiler_params=_params("parallel", "arbitrary"),
    )(at, b)


def _b_mixout(dxm, w, o_f, o_b, proj, c, lng, lnb, l, name):
    t = dxm.shape[0]
    tm = 256

    def body(dxm_ref, w_ref, of_ref, ob_ref, g_ref, c_ref, lng_ref, lnb_ref,
             dxb_ref, dygm_ref, dO_ref, dg_ref, dc_ref, dlg_ref, dlb_ref, dcb_ref):
        @pl.when(pl.program_id(0) == 0)
        def _():
            dlg_ref[...] = jnp.zeros_like(dlg_ref)
            dlb_ref[...] = jnp.zeros_like(dlb_ref)
            dcb_ref[...] = jnp.zeros_like(dcb_ref)

        dxb = dxm_ref[...].astype(BF16)
        dxb_ref[...] = dxb
        dy = _dot_nt(dxb, w_ref[...])
        dygm_ref[...] = dy[:, 0:GMW]
        for h in range(HEADS):
            sl = slice(h * DH, (h + 1) * DH)
            ohat, rstd = _standardize(of_ref[:, sl] + ob_ref[:, sl])
            sg, dsg = _silu_and_grad(g_ref[:, sl].astype(F32))
            dyr = dy[:, GMW + h * DH:GMW + (h + 1) * DH]
            dg_ref[:, sl] = (dyr * ohat * dsg).astype(BF16)
            dO_ref[:, sl] = _standardize_bwd(dyr * sg, ohat, rstd)
        chat, rstd = _standardize(c_ref[...])
        z = chat * lng_ref[...] + lnb_ref[...]
        _, dsz = _silu_and_grad(z)
        dz = dy[:, GMW + RETW:D] * dsz
        dlg_ref[...] += _col_sum(dz * chat)
        dlb_ref[...] += _col_sum(dz)
        dc = _standardize_bwd(dz * lng_ref[...], chat, rstd)
        dcb_ref[...] += _col_sum(dc)
        dc_ref[...] = dc

    vec = lambda: pl.BlockSpec((None, 1, CVW), lambda i: (l, 0, 0))
    acc = lambda: pl.BlockSpec((1, CVW), lambda i: (0, 0))
    return pl.pallas_call(
        body, grid=(t // tm,), name=name,
        in_specs=[pl.BlockSpec((tm, D), lambda i: (i, 0)),
                  _resident((D, D), lambda i: (0, 0)),
                  pl.BlockSpec((tm, RETW), lambda i: (i, 0)),
                  pl.BlockSpec((tm, RETW), lambda i: (i, 0)),
                  pl.BlockSpec((tm, RETW), lambda i: (i, 4)),
                  pl.BlockSpec((tm, CVW), lambda i: (i, 0)), vec(), vec()],
        out_specs=[pl.BlockSpec((tm, D), lambda i: (i, 0)), pl.BlockSpec((tm, GMW), lambda i: (i, 0)),
                   pl.BlockSpec((tm, RETW), lambda i: (i, 0)), pl.BlockSpec((tm, RETW), lambda i: (i, 0)),
                   pl.BlockSpec((tm, CVW), lambda i: (i, 0)), acc(), acc(), acc()],
        out_shape=[S((t, D), BF16), S((t, GMW), F32), S((t, RETW), F32), S((t, RETW), BF16), S((t, CVW), F32),
                   S((1, CVW), F32), S((1, CVW), F32), S((1, CVW), F32)],
        compiler_params=_params("arbitrary"),
    )(dxm, w, o_f, o_b, proj, c, lng, lnb)


def _b_gm(proj, dy, lng, lnb, ws_bf, wst_bf, bias, l, name):
    t = proj.shape[0]
    tm = 512
    nt = t // tm

    def body(p_ref, dy_ref, lng_ref, lnb_ref, ws_ref, wst_ref, bias_ref,
             duv_ref, dws_ref, dbias_ref, dbs_ref, dlg_ref, dlb_ref):
        @pl.when(pl.program_id(0) == 0)
        def _():
            dws_ref[...] = jnp.zeros_like(dws_ref)
            dbias_ref[...] = jnp.zeros_like(dbias_ref)
            dbs_ref[...] = jnp.zeros_like(dbs_ref)
            dlg_ref[...] = jnp.zeros_like(dlg_ref)
            dlb_ref[...] = jnp.zeros_like(dlb_ref)

        for ci in range(tm // C):
            rows = slice(ci * C, (ci + 1) * C)
            u = p_ref[rows, 0:GMW].astype(F32)
            v = p_ref[rows, GMW:2 * GMW].astype(F32)
            au, dau, dav, vhat, rstd, vn, mixed, head = _gm_chunk_fwd(u, v, lng_ref[...], lnb_ref[...], ws_ref, bias_ref[...])
            dyc = dy_ref[rows, :]
            dmixed = dyc * au
            dmb = dmixed.astype(BF16)
            dbias_ref[...] += dmixed
            dvn = jnp.zeros((C, GMW), F32)
            for h in range(HEADS):
                dws_ref[h] += _dot_nt(jnp.where(head == h, dmixed, 0.0).astype(BF16), vn)
                dvn = dvn + jnp.where(head == h, _dot(wst_ref[h], dmb), 0.0)
            dlg_ref[...] += _col_sum(dvn * vhat)
            dlb_ref[...] += _col_sum(dvn)
            dav_in = _standardize_bwd(dvn * lng_ref[...], vhat, rstd)
            duv_ref[rows, 0:GMW] = (dyc * mixed * dau).astype(BF16)
            duv_ref[rows, GMW:2 * GMW] = (dav_in * dav).astype(BF16)

        @pl.when(pl.program_id(0) == nt - 1)
        def _():
            head = lax.broadcasted_iota(jnp.int32, (C, GMW), 1) // (GMW // HEADS)
            lane = lax.broadcasted_iota(jnp.int32, (C, 128), 1)
            fold = jnp.zeros((C, 128), F32)
            for h in range(HEADS):
                col = jnp.sum(jnp.where(head == h, dbias_ref[...], 0.0), axis=1, keepdims=True)
                fold = jnp.where(lane == h, col, fold)
            dbs_ref[...] = fold

    vec = lambda: pl.BlockSpec((None, 1, GMW), lambda i: (l, 0, 0))
    mats = lambda: pl.BlockSpec((None, HEADS, C, C), lambda i: (l, 0, 0, 0))
    return pl.pallas_call(
        body, grid=(nt,), name=name,
        in_specs=[pl.BlockSpec((tm, 2 * GMW), lambda i: (i, 0)), pl.BlockSpec((tm, GMW), lambda i: (i, 0)),
                  vec(), vec(), mats(), mats(), pl.BlockSpec((None, C, GMW), lambda i: (l, 0, 0))],
        out_specs=[pl.BlockSpec((tm, 2 * GMW), lambda i: (i, 0)),
                   pl.BlockSpec((HEADS, C, C), lambda i: (0, 0, 0)),
                   pl.BlockSpec((C, GMW), lambda i: (0, 0)), pl.BlockSpec((C, 128), lambda i: (0, 0)),
                   pl.BlockSpec((1, GMW), lambda i: (0, 0)), pl.BlockSpec((1, GMW), lambda i: (0, 0))],
        out_shape=[S((t, 2 * GMW), BF16), S((HEADS, C, C), F32), S((C, GMW), F32), S((C, 128), F32),
                   S((1, GMW), F32), S((1, GMW), F32)],
        compiler_params=_params("arbitrary"),
    )(proj, dy, lng, lnb, ws_bf, wst_bf, bias)


def _b_conv(proj, dc, cw, l, name):
    t = proj.shape[0]
    tm = 256
    nt = t // tm
    rb = 64

    def body(ap, ac, an, gp, gc, gn, dp, dcur, dn, cw_ref, dag_ref, dcw_ref, hext, dext):
        i = pl.program_id(0)

        @pl.when(i == 0)
        def _():
            dcw_ref[...] = jnp.zeros_like(dcw_ref)

        _fill_ext(hext, _glu(ap, gp), _glu(ac, gc), _glu(an, gn), i, nt, tm)
        _fill_ext(dext, dp[...], dcur[...], dn[...], i, nt, tm)
        for j in range(KW):
            dcw_ref[j:j + 1, :] += _col_sum(dcur[...] * hext[pl.ds(j + 1, tm), :])
        for r0 in range(0, tm, rb):
            dh = jnp.zeros((rb, CVW), F32)
            for j in range(KW):
                dh = dh + cw_ref[j:j + 1, :] * dext[pl.ds(r0 + 2 * HALO - 1 - j, rb), :]
            a = ac[r0:r0 + rb, :].astype(F32)
            s = _sigmoid(gc[r0:r0 + rb, :].astype(F32))
            dag_ref[r0:r0 + rb, 0:CVW] = (dh * s).astype(BF16)
            dag_ref[r0:r0 + rb, CVW:2 * CVW] = (dh * a * s * (1.0 - s)).astype(BF16)

    dspecs = _conv_halo_specs(t, tm, CVW, 0)
    return pl.pallas_call(
        body, grid=(nt,), name=name,
        in_specs=_conv_halo_specs(t, tm, CVW, 10) + _conv_halo_specs(t, tm, CVW, 11) + dspecs +
                 [pl.BlockSpec((None, 32, CVW), lambda i: (l, 0, 0))],
        out_specs=[pl.BlockSpec((tm, 2 * CVW), lambda i: (i, 0)), pl.BlockSpec((32, CVW), lambda i: (0, 0))],
        out_shape=[S((t, 2 * CVW), BF16), S((32, CVW), F32)],
        scratch_shapes=[pltpu.VMEM((tm + 2 * HALO, CVW), F32), pltpu.VMEM((tm + 2 * HALO, CVW), F32)],
        compiler_params=_params("arbitrary"),
    )(proj, proj, proj, proj, proj, proj, dc, dc, dc, cw)


def _b_ret(proj, cos2, sin2, dO, s_f, s_b, rc, name, comm=None):
    t = proj.shape[0]
    n = t // C

    def body(qb, kb, vb, cb, sb, dOb, Sb_in, qf, kf, vf, cf, sf, dOf, Sf_in, Db, XIb, ZEb, Df, XIf, ZEf,
             dqb, dkb, dvb, dqf, dkf, dvf, Gb, Gf):
        @pl.when(pl.program_id(0) == 0)
        def _():
            Gb[...] = jnp.zeros_like(Gb)
            Gf[...] = jnp.zeros_like(Gf)

        def one(q_ref, k_ref, v_ref, cos_ref, sin_ref, dO_ref, S_ref, D_ref, XI_ref, ZE_ref, gC, G_scr, dq_ref, dk_ref, dv_ref):
            for h in range(HEADS):
                sl, qr, kr, vh, cos2v, sin2v = _ret_qkv(q_ref, k_ref, v_ref, cos_ref, sin_ref, h)
                qh, kh = qr.astype(BF16), kr.astype(BF16)
                dOh = dO_ref[:, sl].astype(BF16)
                st = S_ref[h].astype(BF16)
                gr = G_scr[h]
                grb = gr.astype(BF16)
                dm = D_ref[h]
                p = (_dot_nt(qh, kh) * dm).astype(BF16)
                dp = (_dot_nt(dOh, vh) * dm).astype(BF16)
                dqr = _dot(dp, kh) + XI_ref[h] * _dot_nt(dOh, st)
                dkr = (_dot_tn(dp, qh) + ZE_ref[h] * _dot_nt(vh, grb)) * SCALE
                dv = _dot_tn(p, dOh) + _dot((kr * ZE_ref[h]).astype(BF16), grb)
                G_scr[h] = gC[h] * gr + _dot_tn((qr * XI_ref[h]).astype(BF16), dOh)
                dq_ref[:, sl] = _rot_t(dqr, cos2v, sin2v).astype(BF16)
                dk_ref[:, sl] = _rot_t(dkr, cos2v, sin2v).astype(BF16)
                dv_ref[:, sl] = dv.astype(BF16)

        one(qb, kb, vb, cb, sb, dOb, Sb_in, Db, XIb, ZEb, rc["b"]["gC"], Gb, dqb, dkb, dvb)
        one(qf, kf, vf, cf, sf, dOf, Sf_in, Df, XIf, ZEf, rc["f"]["gC"], Gf, dqf, dkf, dvf)

    up = lambda i: i
    down = lambda i: n - 1 - i

    def in_specs(ix):
        return [pl.BlockSpec((C, RETW), lambda i, cb=cb: (ix(i), cb)) for cb in (1, 2, 3)] + \
               [pl.BlockSpec((C, DH), lambda i: (ix(i), 0)), pl.BlockSpec((C, DH), lambda i: (ix(i), 0)),
                pl.BlockSpec((C, RETW), lambda i: (ix(i), 0)),
                pl.BlockSpec((None, HEADS, DH, DH), lambda i: (ix(i), 0, 0, 0))]

    const = lambda: pl.BlockSpec((HEADS, C, DH), lambda i: (0, 0, 0))
    outs = lambda ix: [pl.BlockSpec((C, RETW), lambda i: (ix(i), 0)) for _ in range(3)]
    return _launch(
        body, (n,),
        in_specs(up) + in_specs(down) + [const() for _ in range(6)],
        outs(up) + outs(down),
        [S((t, RETW), BF16) for _ in range(6)],
        [pltpu.VMEM((HEADS, DH, DH), F32), pltpu.VMEM((HEADS, DH, DH), F32)],
        (proj, proj, proj, cos2, sin2, dO, s_b, proj, proj, proj, cos2, sin2, dO, s_f,
         rc["b"]["D"], rc["b"]["XI"], rc["b"]["ZETA"], rc["f"]["D"], rc["f"]["XI"], rc["f"]["ZETA"]),
        name, ("arbitrary",), comm)


def _b_inproj(d_uv, dqkv, d_g, d_ag, w, x, g1, dxm, l, name):
    t = x.shape[0]
    tm = 256
    dq_b, dk_b, dv_b, dq_f, dk_f, dv_f = dqkv

    def body(duv_ref, dqb, dkb, dvb, dqf, dkf, dvf, dg_ref, dag_ref, w_ref, x_ref, g_ref, dxm_ref,
             dp_ref, dx_ref, dn_ref):
        @pl.when(pl.program_id(0) == 0)
        def _():
            dn_ref[...] = jnp.zeros_like(dn_ref)

        dp_ref[:, 0:512] = duv_ref[...]
        for k, (a, b) in enumerate(((dqb, dqf), (dkb, dkf), (dvb, dvf))):
            dp_ref[:, 512 * (k + 1):512 * (k + 2)] = (a[...].astype(F32) + b[...].astype(F32)).astype(BF16)
        dp_ref[:, 2048:2560] = dg_ref[...]
        dp_ref[:, 2560:3072] = dag_ref[...]
        dh = _dot_nt(dp_ref[...], w_ref[...])
        xv = x_ref[...]
        r = _rms(xv)
        dn_ref[...] += _col_sum(dh * xv * r)
        dx_ref[...] = dxm_ref[...] + _rmsnorm_bwd(dh, xv, r, g_ref[...])

    half = lambda: pl.BlockSpec((tm, 512), lambda i: (i, 0))
    full = lambda: pl.BlockSpec((tm, D), lambda i: (i, 0))
    return pl.pallas_call(
        body, grid=(t // tm,), name=name,
        in_specs=[half() for _ in range(9)] +
                 [_resident((D, INW), lambda i: (0, 0)), full(),
                  pl.BlockSpec((None, 1, D), lambda i: (l, 0, 0)), full()],
        out_specs=[pl.BlockSpec((tm, INW), lambda i: (i, 0)), full(), pl.BlockSpec((1, D), lambda i: (0, 0))],
        out_shape=[S((t, INW), BF16), S((t, D), F32), S((1, D), F32)],
        compiler_params=_params("arbitrary"),
    )(d_uv, dq_b, dk_b, dv_b, dq_f, dk_f, dv_f, d_g, d_ag, w, x, g1, dxm)


def _pair_exchange(gs, name):
    n = len(gs)

    def body(*refs):
        g, q, ssem, rsem = refs[:n], refs[n:2 * n], refs[2 * n], refs[2 * n + 1]
        x, y, c, _ = _place()
        copies = [pltpu.make_async_remote_copy(src_ref=g[u].at[2 * chip + (1 - c)], dst_ref=q[u].at[chip],
                                               send_sem=ssem.at[u, chip], recv_sem=rsem.at[u, chip],
                                               device_id=(x, y, 1 - c), device_id_type=MESH)
                  for u in range(n) for chip in range(4)]
        for cp in copies:
            cp.start()
        for cp in copies:
            cp.wait_recv()
        for cp in copies:
            cp.wait_send()

    return pl.pallas_call(body, name=name, in_specs=[_HBM] * n, out_specs=[_HBM] * n,
                          out_shape=[S((4,) + g.shape[1:], BF16) for g in gs],
                          scratch_shapes=[pltpu.SemaphoreType.DMA((n, 4)), pltpu.SemaphoreType.DMA((n, 4))],
                          compiler_params=_params())(*gs)


def _pair_add(g, q, name):
    _, mm, nn = g.shape
    bm = 256 if mm % 256 == 0 else mm

    def body(g_ref, q_ref, h_ref):
        h_ref[...] = (g_ref[lax.axis_index("c")].astype(F32) + q_ref[...].astype(F32)).astype(BF16)

    return pl.pallas_call(
        body, grid=(4, mm // bm), name=name,
        in_specs=[pl.BlockSpec((None, 2, bm, nn), lambda qq, i: (qq, 0, i, 0)),
                  pl.BlockSpec((None, bm, nn), lambda qq, i: (qq, i, 0))],
        out_specs=pl.BlockSpec((None, bm, nn), lambda qq, i: (qq, i, 0)),
        out_shape=S((4, mm, nn), BF16),
        compiler_params=_params("parallel", "parallel"),
    )(g.reshape(4, 2, mm, nn), q)


def _pair_reduce(named, l):
    names = [k for k, _ in named]
    qs = _pair_exchange([g for _, g in named], f"pair_exchange_{names[0]}_{l}")
    return [_pair_add(g, q, f"pair_add_{k}_{l}") for (k, g), q in zip(named, qs)]


_BIG = ("w_in", "w_out", "w_ffn_in", "w_ffn_out")
_KIND = dict(w_in="cols", w_out="rows", w_ffn_in="lead", w_ffn_out="rows")
CWP = 128
SMALL_ROWS = 152


def _step(x, tgt, wts, sh, cw_pad):
    t = x.shape[0]
    rc = _ret_consts()
    cos2, sin2 = (jnp.asarray(a) for a in _rope_tables(t))
    n1 = wts["norm1_g"].reshape(LAYERS, 1, D)
    n2 = wts["norm2_g"].reshape(LAYERS, 1, D)
    gm_lng = wts["gm_ln_g"].reshape(LAYERS, 1, GMW)
    gm_lnb = wts["gm_ln_b"].reshape(LAYERS, 1, GMW)
    ws_bf = wts["gm_ws"].astype(BF16)
    wst_bf = jnp.swapaxes(wts["gm_ws"], 2, 3).astype(BF16)
    bias = jnp.repeat(jnp.swapaxes(wts["gm_bs"], 1, 2), GMW // HEADS, axis=2)
    cb = wts["conv_b"].reshape(LAYERS, 1, CVW)
    cv_lng = wts["conv_ln_g"].reshape(LAYERS, 1, CVW)
    cv_lnb = wts["conv_ln_b"].reshape(LAYERS, 1, CVW)
    unit = lambda f, l: (sh[f][l], _KIND[f])
    cshard = CVW // NDEV

    full = {f: [None] * LAYERS for f in _BIG}
    full["w_in"][0], cw_all = _comm_only(_Gather([unit("w_in", 0), (cw_pad, "lead")]), "gather_first")
    cw = jnp.transpose(cw_all[:, :, :, :cshard], (1, 2, 0, 3)).reshape(LAYERS, 32, CVW)

    saved = []
    for l in range(LAYERS):
        first = l == 0
        (proj, h1), got = _f_inproj(x, n1, full["w_in"][l], l, f"f_inproj_{l}",
                                    _Gather([unit("w_out", 0), unit("w_ffn_out", 0)]) if first else None)
        if first:
            full["w_out"][0], full["w_ffn_out"][0] = got
        y_gm = _f_gm(proj, gm_lng, gm_lnb, ws_bf, bias, l, f"f_gm_{l}")
        (o_f, o_b, s_f, s_b), got = _f_ret(proj, cos2, sin2, rc, f"f_ret_{l}", _Gather([unit("w_ffn_in", l)]))
        full["w_ffn_in"][l], = got
        c, y_cv = _f_conv(proj, cw, cb, cv_lng, cv_lnb, l, f"f_conv_{l}")
        xm, ycat = _f_mixout(x, y_gm, y_cv, o_f, o_b, proj, full["w_out"][l], f"f_mixout_{l}")
        (xo, h2, gu, act), got = _f_ffn(xm, n2, full["w_ffn_in"][l], full["w_ffn_out"][l], l, f"f_ffn_{l}",
                                        _Gather([unit("w_in", 1), unit("w_out", 1), unit("w_ffn_out", 1)]) if first else None)
        if first:
            full["w_in"][1], full["w_out"][1], full["w_ffn_out"][1] = got
        saved.append(dict(x=x, proj=proj, h1=h1, o_f=o_f, o_b=o_b, s_f=s_f, s_b=s_b, c=c, xm=xm, ycat=ycat,
                          h2=h2, gu=gu, act=act))
        x = xo

    loss, dx, d_final = _b_loss(x, wts["final_g"].reshape(1, D), tgt, "b_loss")

    parts = {f: [None] * LAYERS for f in _BIG}
    small = [None] * LAYERS
    upper = None
    for l in reversed(range(LAYERS)):
        sv = saved[l]
        (dgu, dxm, dxo_bf, d_n2), got = _b_ffn(dx, sv["xm"], n2, sv["gu"], full["w_ffn_in"][l], full["w_ffn_out"][l], l,
                                               f"b_ffn_{l}", _Scatter(upper) if upper else None)
        if upper:
            for f, p in zip(_BIG, got):
                parts[f][l + 1] = p
        g_f2 = _mm_wgrad(sv["act"], dxo_bf, NDEV // 2, True, "shared", 1, f"g_ffn_out_{l}").reshape(NDEV, FFH // NDEV, D)
        g_f1 = _mm_wgrad(sv["h2"], dgu, NDEV, False, "lead", 1, f"g_ffn_in_{l}")
        dxm_bf, dy_gm, dO, d_g, dc, d_cvlg, d_cvlb, d_cb = _b_mixout(
            dxm, full["w_out"][l], sv["o_f"], sv["o_b"], sv["proj"], sv["c"], cv_lng, cv_lnb, l, f"b_mixout_{l}")
        g_out = _mm_wgrad(sv["ycat"], dxm_bf, 1, False, "shared", 1, f"g_out_{l}").reshape(NDEV, D // NDEV, D)
        last = l == 0
        early = _pair_reduce([("w_out", g_out), ("w_ffn_in", g_f1), ("w_ffn_out", g_f2)], l) if last else None
        d_uv, d_ws, _, d_bs_fold, d_gmlg, d_gmlb = _b_gm(sv["proj"], dy_gm, gm_lng, gm_lnb, ws_bf, wst_bf, bias, l, f"b_gm_{l}")
        d_ag, d_cw = _b_conv(sv["proj"], dc, cw, l, f"b_conv_{l}")
        dqkv, got = _b_ret(sv["proj"], cos2, sin2, dO, sv["s_f"], sv["s_b"], rc, f"b_ret_{l}", _Scatter(early) if last else None)
        if last:
            parts["w_out"][l], parts["w_ffn_in"][l], parts["w_ffn_out"][l] = got
        dproj, dx, d_n1 = _b_inproj(d_uv, dqkv, d_g, d_ag, full["w_in"][l], sv["x"], n1, dxm, l, f"b_inproj_{l}")
        g_in = _mm_wgrad(sv["h1"], dproj, NDEV, False, "cols", 2, f"g_in_{l}")
        if last:
            tail = _pair_reduce([("w_in", g_in)], l)
        else:
            upper = _pair_reduce([("w_in", g_in), ("w_out", g_out), ("w_ffn_in", g_f1), ("w_ffn_out", g_f2)], l)
        small[l] = dict(norm1_g=d_n1[0], gm_ln_g=d_gmlg[0], gm_ln_b=d_gmlb[0], gm_ws=d_ws,
                        gm_bs=d_bs_fold[:, :HEADS].T, conv_w=d_cw[:KW], conv_b=d_cb[0], conv_ln_g=d_cvlg[0],
                        conv_ln_b=d_cvlb[0], norm2_g=d_n2[0])
    small_g = {k: jnp.stack([small[l][k] for l in range(LAYERS)]) for k in small[0]}
    small_g["final_g"] = d_final[0]
    small_buf = _pack([small_g[k] for k in _SMALL], SMALL_ROWS)
    parts["w_in"][0], small_parts = _comm_only(_Comms([_Scatter(tail), _Gather([(small_buf, "lead")])]), "exchange_last")
    return loss, dx, parts, small_parts


def _adamw(w, g, m, v):
    m = ADAM_B1 * m + (1.0 - ADAM_B1) * g
    v = ADAM_B2 * v + (1.0 - ADAM_B2) * (g * g)
    m_hat = m / (1.0 - ADAM_B1 ** ADAM_STEP)
    v_hat = v / (1.0 - ADAM_B2 ** ADAM_STEP)
    return -ADAM_LR * (m_hat / (jnp.sqrt(v_hat) + ADAM_EPS) + ADAM_WD * w), m, v


def _cast_blocks(ws):
    def body(*refs):
        ins, outs = refs[:len(ws)], refs[len(ws):]
        for k, src in enumerate(ins):
            for l in range(LAYERS):
                outs[k * LAYERS + l][...] = src[l].astype(BF16)

    outs = pl.pallas_call(body, name="cast_blocks", out_shape=[S(w.shape[1:], BF16) for w in ws for _ in range(LAYERS)],
                          compiler_params=_params())(*ws)
    return [list(outs[k * LAYERS:(k + 1) * LAYERS]) for k in range(len(ws))]


def _sum_adam(parts, w, m, v, l, prev, name):
    _, mm, nn = parts.shape
    bm = 256 if mm % 256 == 0 else mm

    def body(p_ref, w_ref, m_ref, v_ref, *rest):
        g_ref, d_ref, nm_ref, nv_ref = rest[-4:]
        g = p_ref[0].astype(F32)
        for s in range(1, 4):
            g = g + p_ref[s].astype(F32)
        g_ref[...] = g
        d_ref[...], nm_ref[...], nv_ref[...] = _adamw(w_ref[...], g, m_ref[...], v_ref[...])

    blk = lambda: pl.BlockSpec((None, bm, nn), lambda i: (l, i, 0))
    prev = list(prev) if prev else []
    return pl.pallas_call(
        body, grid=(mm // bm,), name=name,
        in_specs=[pl.BlockSpec((4, bm, nn), lambda i: (0, i, 0)), blk(), blk(), blk()] + [_ANY] * len(prev),
        out_specs=[blk() for _ in range(4)],
        out_shape=[S(w.shape, F32) for _ in range(4)],
        input_output_aliases={4 + j: j for j in range(len(prev))},
        compiler_params=_params("parallel"),
    )(parts, w, m, v, *prev)


def _sum_small(parts):
    def body(p_ref, o_ref):
        g = p_ref[0]
        for s in range(1, NDEV):
            g = g + p_ref[s]
        o_ref[...] = g

    return pl.pallas_call(body, name="sum_small", out_shape=S(parts.shape[1:], F32),
                          compiler_params=_params())(parts)


def _adam_small(g, w, m, v):
    def body(g_ref, w_ref, m_ref, v_ref, d_ref, nm_ref, nv_ref):
        d_ref[...], nm_ref[...], nv_ref[...] = _adamw(w_ref[...], g_ref[...], m_ref[...], v_ref[...])

    return pl.pallas_call(body, name="adam_small", out_shape=[S(g.shape, F32)] * 3, compiler_params=_params())(g, w, m, v)


_SMALL = ("norm1_g", "gm_ln_g", "gm_ln_b", "gm_ws", "gm_bs", "conv_w", "conv_b", "conv_ln_g", "conv_ln_b",
          "norm2_g", "final_g")
_NAMES = ("norm1_g", "w_in", "gm_ln_g", "gm_ln_b", "gm_ws", "gm_bs", "conv_w", "conv_b", "conv_ln_g", "conv_ln_b",
          "w_out", "norm2_g", "w_ffn_in", "w_ffn_out", "final_g")


def _pack(parts, rows):
    flat = jnp.concatenate([p.reshape(-1) for p in parts])
    return jnp.pad(flat, (0, rows * 1024 - flat.shape[0])).reshape(rows, 1024)


def _unpack(buf, shapes):
    flat = buf.reshape(-1)
    out, o = [], 0
    for shp in shapes:
        sz = int(np.prod(shp))
        out.append(flat[o:o + sz].reshape(shp))
        o += sz
    return out


def kernel(x, norm1_g, w_in, gm_ln_g, gm_ln_b, gm_ws, gm_bs, conv_w, conv_b, conv_ln_g, conv_ln_b, w_out, norm2_g, w_ffn_in, w_ffn_out, final_g, loss_target, m_norm1_g, m_w_in, m_gm_ln_g, m_gm_ln_b, m_gm_ws, m_gm_bs, m_conv_w, m_conv_b, m_conv_ln_g, m_conv_ln_b, m_w_out, m_norm2_g, m_w_ffn_in, m_w_ffn_out, m_final_g, v_norm1_g, v_w_in, v_gm_ln_g, v_gm_ln_b, v_gm_ws, v_gm_bs, v_conv_w, v_conv_b, v_conv_ln_g, v_conv_ln_b, v_w_out, v_norm2_g, v_w_ffn_in, v_w_ffn_out, v_final_g):
    w = dict(norm1_g=norm1_g, w_in=w_in, gm_ln_g=gm_ln_g, gm_ln_b=gm_ln_b, gm_ws=gm_ws, gm_bs=gm_bs, conv_w=conv_w,
             conv_b=conv_b, conv_ln_g=conv_ln_g, conv_ln_b=conv_ln_b, w_out=w_out, norm2_g=norm2_g, w_ffn_in=w_ffn_in,
             w_ffn_out=w_ffn_out, final_g=final_g)
    mo = dict(norm1_g=m_norm1_g, w_in=m_w_in, gm_ln_g=m_gm_ln_g, gm_ln_b=m_gm_ln_b, gm_ws=m_gm_ws, gm_bs=m_gm_bs,
              conv_w=m_conv_w, conv_b=m_conv_b, conv_ln_g=m_conv_ln_g, conv_ln_b=m_conv_ln_b, w_out=m_w_out,
              norm2_g=m_norm2_g, w_ffn_in=m_w_ffn_in, w_ffn_out=m_w_ffn_out, final_g=m_final_g)
    vo = dict(norm1_g=v_norm1_g, w_in=v_w_in, gm_ln_g=v_gm_ln_g, gm_ln_b=v_gm_ln_b, gm_ws=v_gm_ws, gm_bs=v_gm_bs,
              conv_w=v_conv_w, conv_b=v_conv_b, conv_ln_g=v_conv_ln_g, conv_ln_b=v_conv_ln_b, w_out=v_w_out,
              norm2_g=v_norm2_g, w_ffn_in=v_w_ffn_in, w_ffn_out=v_w_ffn_out, final_g=v_final_g)
    t = x.shape[1]
    me = 4 * lax.axis_index("x") + 2 * lax.axis_index("y") + lax.axis_index("c")
    cshard = conv_w.shape[2]

    cw_pad = jnp.pad(conv_w, ((0, 0), (0, 32 - KW), (0, CWP - cshard)))
    sh = dict(zip(_BIG, _cast_blocks([w[f] for f in _BIG])))
    loss, dx, parts, small_parts = _step(x.reshape(t, D), loss_target.reshape(t, D), w, sh, cw_pad)

    grads, delta, new_m, new_v = {}, {}, {}, {}
    for f in _BIG:
        outs = None
        for l in reversed(range(LAYERS)):
            outs = _sum_adam(parts[f][l], w[f], mo[f], vo[f], l, outs, f"sum_adam_{f}_{l}")
        grads[f], delta[f], new_m[f], new_v[f] = outs

    full_shapes = [(LAYERS, KW, CVW) if k == "conv_w" else w[k].shape for k in _SMALL]
    for k, g in zip(_SMALL, _unpack(_sum_small(small_parts), full_shapes)):
        grads[k] = lax.dynamic_slice_in_dim(g, me * cshard, cshard, axis=2) if k == "conv_w" else g
    adam_rows = 144
    d_s, m_s, v_s = _adam_small(_pack([grads[k] for k in _SMALL], adam_rows), _pack([w[k] for k in _SMALL], adam_rows),
                                _pack([mo[k] for k in _SMALL], adam_rows), _pack([vo[k] for k in _SMALL], adam_rows))
    shapes = [w[k].shape for k in _SMALL]
    for dst, buf in ((delta, d_s), (new_m, m_s), (new_v, v_s)):
        for k, a in zip(_SMALL, _unpack(buf, shapes)):
            dst[k] = a

    total = lax.psum(loss[0, 0], ("x", "y", "c"))
    return (total, dx.reshape(1, t, D), *[grads[k] for k in _NAMES], *[delta[k] for k in _NAMES],
            *[new_m[k] for k in _NAMES], *[new_v[k] for k in _NAMES])
```

```python
import functools

import numpy as np
import jax
import jax.numpy as jnp
from jax import lax
from jax.experimental import pallas as pl
from jax.experimental.pallas import tpu as pltpu

F32, BF16 = jnp.float32, jnp.bfloat16
S = jax.ShapeDtypeStruct

D = 1024
INW = 3072
GMW = 256
RETW = 512
CVW = 256
HEADS = 4
DH = 128
C = 128
KW = 31
HALO = 16
FFH = 2816
NDEV = 8
FFB = 2 * FFH // NDEV
EPS = 1e-6
LAYERS = 2
SCALE = DH ** -0.5
VMEM_LIMIT = 56 * 1024 * 1024

ADAM_LR, ADAM_B1, ADAM_B2, ADAM_EPS, ADAM_WD, ADAM_STEP = 0.001, 0.9, 0.999, 1e-08, 0.01, 10

_SQRT_HALF = 0.7071067811865476
_INV_SQRT_2PI = 0.3989422804014327


def _params(*sem):
    return pltpu.CompilerParams(dimension_semantics=sem or None, vmem_limit_bytes=VMEM_LIMIT)


def _resident(shape, index_map):
    return pl.BlockSpec(shape, index_map, pipeline_mode=pl.Buffered(1))


def _dot(a, b):
    return jnp.dot(a, b, preferred_element_type=F32)


def _dot_nt(a, b):
    return lax.dot_general(a, b, (((1,), (1,)), ((), ())), preferred_element_type=F32)


def _dot_tn(a, b):
    return lax.dot_general(a, b, (((0,), (0,)), ((), ())), preferred_element_type=F32)


def _sigmoid(x):
    return 1.0 / (1.0 + jnp.exp(-x))


def _gelu_and_grad(x):
    cdf = 0.5 * (1.0 + lax.erf(x * _SQRT_HALF))
    return x * cdf, cdf + x * jnp.exp(-0.5 * x * x) * _INV_SQRT_2PI


def _silu_and_grad(x):
    s = _sigmoid(x)
    return x * s, s * (1.0 + x * (1.0 - s))


def _standardize(x):
    mu = jnp.mean(x, axis=-1, keepdims=True)
    d = x - mu
    rstd = lax.rsqrt(jnp.mean(d * d, axis=-1, keepdims=True) + EPS)
    return d * rstd, rstd


def _standardize_bwd(dxhat, xhat, rstd):
    m1 = jnp.mean(dxhat, axis=-1, keepdims=True)
    m2 = jnp.mean(dxhat * xhat, axis=-1, keepdims=True)
    return rstd * (dxhat - m1 - xhat * m2)


def _rms(x):
    return lax.rsqrt(jnp.mean(x * x, axis=-1, keepdims=True) + EPS)


def _rmsnorm_bwd(dy, x, r, g):
    u = dy * g
    return r * u - x * (r * r * r) * jnp.mean(u * x, axis=-1, keepdims=True)


def _col_sum(a):
    return jnp.sum(a, axis=0, keepdims=True)


def _rot(t, cos2, sin2):
    return t * cos2 + pltpu.roll(t, DH // 2, axis=1) * sin2


def _rot_t(dt, cos2, sin2):
    return dt * cos2 + pltpu.roll(dt * sin2, DH // 2, axis=1)


MESH = pl.DeviceIdType.MESH
_HBM = pl.BlockSpec(memory_space=pltpu.HBM)
_ANY = pl.BlockSpec(memory_space=pl.ANY)


def _place():
    x, y, c = lax.axis_index("x"), lax.axis_index("y"), lax.axis_index("c")
    return x, y, c, ((1 - x, y), (x, 1 - y), (1 - x, 1 - y))


def _slot(full, kind, width, i):
    if kind == "cols":
        return full.at[:, pl.ds(pl.multiple_of(i * width, 128), width)]
    if kind == "rows":
        return full.at[pl.ds(pl.multiple_of(i * width, 16), width), :]
    return full.at[i]


class _Gather:
    def __init__(self, units):
        self.units = units
        self.inputs = [u[0] for u in units]
        self.out_shape = []
        for src, kind in units:
            r, c = src.shape[-2:]
            shape = {"cols": (r, NDEV * c), "rows": (NDEV * r, c), "lead": (NDEV,) + src.shape}[kind]
            self.out_shape.append(S(shape, src.dtype))
        n = len(units)
        self.scratch = [pltpu.SemaphoreType.DMA((n, 7)), pltpu.SemaphoreType.DMA((n, 7)), pltpu.SemaphoreType.DMA((n,))]

    def run(self, phase, ins, outs, scr):
        ssem, rsem, lsem = scr
        x, y, c, chips = _place()
        me, sib = 4 * x + 2 * y + c, (x, y, 1 - c)
        idx = lambda chip, core: 4 * chip[0] + 2 * chip[1] + core
        for u, (src_arr, kind) in enumerate(self.units):
            src, full = ins[u], outs[u]
            width = src_arr.shape[-1] if kind == "cols" else src_arr.shape[-2]
            slot = functools.partial(_slot, full, kind, width)

            def copy(k, block, to, from_src=False):
                return pltpu.make_async_remote_copy(src_ref=src if from_src else slot(block), dst_ref=slot(block),
                                                    send_sem=ssem.at[u, k], recv_sem=rsem.at[u, k],
                                                    device_id=to, device_id_type=MESH)

            mine = lambda: pltpu.make_async_copy(src, slot(me), lsem.at[u])
            first = lambda: [copy(0, me, sib, True)] + [copy(1 + j, me, (*chip, c), True) for j, chip in enumerate(chips)]
            passed = lambda j: copy(4 + j, idx(chips[j], c), sib)
            if phase == "start":
                mine().start()
                for cp in first():
                    cp.start()
            elif phase == "forward":
                for j, chip in enumerate(chips):
                    copy(1 + j, idx(chip, c), sib).wait_recv()
                    passed(j).start()
            else:
                copy(0, idx((x, y), 1 - c), sib).wait_recv()
                for j, chip in enumerate(chips):
                    copy(4 + j, idx(chip, 1 - c), sib).wait_recv()
                for cp in first() + [passed(j) for j in range(3)]:
                    cp.wait_send()
                mine().wait()


class _Scatter:
    def __init__(self, units):
        self.units = units
        self.inputs = list(units)
        self.out_shape = [S(u.shape, u.dtype) for u in units]
        n = len(units)
        self.scratch = [pltpu.SemaphoreType.DMA((n, 3)), pltpu.SemaphoreType.DMA((n, 3)), pltpu.SemaphoreType.DMA((n,))]

    def run(self, phase, ins, outs, scr):
        ssem, rsem, lsem = scr
        x, y, c, chips = _place()
        myq = 2 * x + y
        for u in range(len(self.units)):
            h, p = ins[u], outs[u]

            def copy(k, chip, send_to_them):
                q = 2 * chip[0] + chip[1]
                return pltpu.make_async_remote_copy(src_ref=h.at[q], dst_ref=p.at[myq if send_to_them else q],
                                                    send_sem=ssem.at[u, k], recv_sem=rsem.at[u, k],
                                                    device_id=(*chip, c), device_id_type=MESH)

            mine = lambda: pltpu.make_async_copy(h.at[myq], p.at[myq], lsem.at[u])
            sends = lambda: [copy(k, chip, True) for k, chip in enumerate(chips)]
            if phase == "start":
                mine().start()
                for cp in sends():
                    cp.start()
            elif phase == "finish":
                for k, chip in enumerate(chips):
                    copy(k, chip, False).wait_recv()
                for cp in sends():
                    cp.wait_send()
                mine().wait()


class _Comms:
    def __init__(self, parts):
        self.parts = parts
        self.inputs = [a for p in parts for a in p.inputs]
        self.out_shape = [a for p in parts for a in p.out_shape]
        self.scratch = [a for p in parts for a in p.scratch]

    def run(self, phase, ins, outs, scr):
        i = o = s = 0
        for p in self.parts:
            ni, no, ns = len(p.inputs), len(p.out_shape), len(p.scratch)
            p.run(phase, ins[i:i + ni], outs[o:o + no], scr[s:s + ns])
            i, o, s = i + ni, o + no, s + ns


def _launch(body, grid, in_specs, out_specs, out_shape, scratch, args, name, sem, comm=None):
    if comm is None:
        outs = pl.pallas_call(body, grid=grid, name=name, in_specs=in_specs, out_specs=out_specs, out_shape=out_shape,
                              scratch_shapes=scratch, compiler_params=_params(*sem))(*args)
        return list(outs), []
    n_in, n_out, n_scr = len(args), len(out_shape), len(scratch)
    ci, co = len(comm.inputs), len(comm.out_shape)
    nsteps = grid[0]
    fwd_step = (7 * nsteps) // 8

    def hosted(*refs):
        a = refs[:n_in]
        ca = refs[n_in:n_in + ci]
        o = refs[n_in + ci:n_in + ci + n_out]
        cout = refs[n_in + ci + n_out:n_in + ci + n_out + co]
        s = refs[n_in + ci + n_out + co:n_in + ci + n_out + co + n_scr]
        cs = refs[n_in + ci + n_out + co + n_scr:]
        step = pl.program_id(0)

        @pl.when(step == 0)
        def _():
            comm.run("start", ca, cout, cs)

        body(*a, *o, *s)

        @pl.when(step == fwd_step)
        def _():
            comm.run("forward", ca, cout, cs)

        @pl.when(step == nsteps - 1)
        def _():
            comm.run("finish", ca, cout, cs)

    outs = pl.pallas_call(
        hosted, grid=grid, name=name, in_specs=list(in_specs) + [_HBM] * ci, out_specs=list(out_specs) + [_HBM] * co,
        out_shape=list(out_shape) + comm.out_shape, scratch_shapes=list(scratch) + comm.scratch,
        compiler_params=_params(*["arbitrary"] * len(grid)))(*args, *comm.inputs)
    return list(outs[:n_out]), list(outs[n_out:])


def _comm_only(comm, name):
    ci, co = len(comm.inputs), len(comm.out_shape)

    def body(*refs):
        ca, cout, cs = refs[:ci], refs[ci:ci + co], refs[ci + co:]
        for phase in ("start", "forward", "finish"):
            comm.run(phase, ca, cout, cs)

    return pl.pallas_call(body, name=name, in_specs=[_HBM] * ci, out_specs=[_HBM] * co, out_shape=comm.out_shape,
                          scratch_shapes=comm.scratch, compiler_params=_params())(*comm.inputs)


def _ret_consts():
    idx = np.arange(C, dtype=np.float32)
    gf = (1.0 - np.exp2(-5.0 - np.arange(HEADS, dtype=np.float32))).astype(np.float32)
    out = {}
    for name, gamma, fwd in (("f", gf, True), ("b", gf[::-1].copy(), False)):
        lg = np.log(gamma).astype(np.float32)[:, None]
        diff = idx[:, None] - idx[None, :]
        if fwd:
            mask = diff >= 0
            dist = np.where(mask, diff, 0.0)
            zeta = np.exp(lg * (C - 1 - idx))
            xi = np.exp(lg * (idx + 1))
        else:
            mask = diff < 0
            dist = np.where(mask, -diff, 0.0)
            zeta = np.exp(lg * idx)
            xi = np.exp(lg * (C - idx))
        dm = np.where(mask[None], np.exp(lg[:, :, None] * dist[None]), 0.0).astype(np.float32)
        bc = lambda vec: np.ascontiguousarray(np.broadcast_to(vec.astype(np.float32)[:, :, None], (HEADS, C, DH)))
        out[name] = dict(D=dm, XI=bc(xi), ZETA=bc(zeta), gC=[float(v) for v in np.exp(lg[:, 0] * C).astype(np.float32)])
    return out


def _rope_tables(t):
    half = DH // 2
    inv_freq = (np.float32(10000.0) ** (-np.arange(half, dtype=np.float32) / np.float32(half))).astype(np.float32)
    ang = (np.arange(t, dtype=np.float32)[:, None] * inv_freq[None, :]).astype(np.float64)
    cos, sin = np.cos(ang).astype(np.float32), np.sin(ang).astype(np.float32)
    return np.concatenate([cos, cos], axis=1), np.concatenate([-sin, sin], axis=1)


def _f_inproj(x, g1, w, l, name, comm=None):
    t = x.shape[0]
    tm = 512

    def body(x_ref, g_ref, w_ref, proj_ref, ht_ref):
        xv = x_ref[...]
        h = (xv * _rms(xv) * g_ref[...]).astype(BF16)
        ht_ref[...] = h.T
        for nb in range(INW // 512):
            cs = slice(nb * 512, (nb + 1) * 512)
            proj_ref[:, cs] = _dot(h, w_ref[:, cs]).astype(BF16)

    return _launch(
        body, (t // tm,),
        [pl.BlockSpec((tm, D), lambda i: (i, 0)),
         pl.BlockSpec((None, 1, D), lambda i: (l, 0, 0)),
         _resident((D, INW), lambda i: (0, 0))],
        [pl.BlockSpec((tm, INW), lambda i: (i, 0)), pl.BlockSpec((D, tm), lambda i: (0, i))],
        [S((t, INW), BF16), S((D, t), BF16)], [], (x, g1, w), name, ("parallel",), comm)


def _gm_chunk_fwd(u, v, lng, lnb, ws_ref, bias):
    au, dau = _gelu_and_grad(u)
    av, dav = _gelu_and_grad(v)
    vhat, rstd = _standardize(av)
    vn = (vhat * lng + lnb).astype(BF16)
    head = lax.broadcasted_iota(jnp.int32, (C, GMW), 1) // (GMW // HEADS)
    mixed = bias
    for h in range(HEADS):
        mixed = mixed + jnp.where(head == h, _dot(ws_ref[h], vn), 0.0)
    return au, dau, dav, vhat, rstd, vn, mixed, head


def _f_gm(proj, lng, lnb, ws_bf, bias, l, name):
    t = proj.shape[0]
    tm = 512

    def body(p_ref, lng_ref, lnb_ref, ws_ref, bias_ref, y_ref):
        for ci in range(tm // C):
            rows = slice(ci * C, (ci + 1) * C)
            u = p_ref[rows, 0:GMW].astype(F32)
            v = p_ref[rows, GMW:2 * GMW].astype(F32)
            au, _, _, _, _, _, mixed, _ = _gm_chunk_fwd(u, v, lng_ref[...], lnb_ref[...], ws_ref, bias_ref[...])
            y_ref[rows, :] = (au * mixed).astype(BF16)

    return pl.pallas_call(
        body, grid=(t // tm,), name=name,
        in_specs=[pl.BlockSpec((tm, 2 * GMW), lambda i: (i, 0)),
                  pl.BlockSpec((None, 1, GMW), lambda i: (l, 0, 0)),
                  pl.BlockSpec((None, 1, GMW), lambda i: (l, 0, 0)),
                  pl.BlockSpec((None, HEADS, C, C), lambda i: (l, 0, 0, 0)),
                  pl.BlockSpec((None, C, GMW), lambda i: (l, 0, 0))],
        out_specs=pl.BlockSpec((tm, GMW), lambda i: (i, 0)),
        out_shape=S((t, GMW), BF16),
        compiler_params=_params("parallel"),
    )(proj, lng, lnb, ws_bf, bias)


def _scan_pair(proj, cos2, sin2, other, col, wf, wb, gcf, gcb, first_is_f, name):
    t = proj.shape[0]
    n = t // C
    other_is_proj = other is None

    def body(a1, c1, s1, o1, a2, c2, s2, o2, w1_ref, w2_ref, out1, out2, st1, st2):
        @pl.when(pl.program_id(0) == 0)
        def _():
            st1[...] = jnp.zeros_like(st1)
            st2[...] = jnp.zeros_like(st2)

        def one(a_ref, cos_ref, sin_ref, o_ref, w_ref, gc, st, out):
            cos_v, sin_v = cos_ref[...], sin_ref[...]
            for h in range(HEADS):
                sl = slice(h * DH, (h + 1) * DH)
                ar = _rot(a_ref[:, sl].astype(F32), cos_v, sin_v)
                inc = _dot_tn((ar * w_ref[h]).astype(BF16), o_ref[:, sl].astype(BF16))
                cur = st[h]
                out[h] = cur.astype(BF16)
                st[h] = gc[h] * cur + inc

        g1, g2 = (gcf, gcb) if first_is_f else (gcb, gcf)
        one(a1, c1, s1, o1, w1_ref, g1, st1, out1)
        one(a2, c2, s2, o2, w2_ref, g2, st2, out2)

    up = lambda i: i
    down = lambda i: n - 1 - i

    def specs(ix):
        o_spec = pl.BlockSpec((C, RETW), lambda i: (ix(i), 3 if other_is_proj else 0))
        return [pl.BlockSpec((C, RETW), lambda i: (ix(i), col)),
                pl.BlockSpec((C, DH), lambda i: (ix(i), 0)), pl.BlockSpec((C, DH), lambda i: (ix(i), 0)), o_spec]

    const = lambda: pl.BlockSpec((HEADS, C, DH), lambda i: (0, 0, 0))
    oth = proj if other_is_proj else other
    w1, w2 = (wf, wb) if first_is_f else (wb, wf)
    out1, out2 = pl.pallas_call(
        body, grid=(n,), name=name,
        in_specs=specs(up) + specs(down) + [const(), const()],
        out_specs=[pl.BlockSpec((None, HEADS, DH, DH), lambda i: (up(i), 0, 0, 0)),
                   pl.BlockSpec((None, HEADS, DH, DH), lambda i: (down(i), 0, 0, 0))],
        out_shape=[S((n, HEADS, DH, DH), BF16), S((n, HEADS, DH, DH), BF16)],
        scratch_shapes=[pltpu.VMEM((HEADS, DH, DH), F32), pltpu.VMEM((HEADS, DH, DH), F32)],
        compiler_params=_params("arbitrary"),
    )(proj, cos2, sin2, oth, proj, cos2, sin2, oth, w1, w2)
    return (out1, out2) if first_is_f else (out2, out1)


RET_CHUNKS = 2


def _ret_heads(q_ref, k_ref, v_ref, cos_ref, sin_ref, rows, h):
    sl = slice(h * DH, (h + 1) * DH)
    cos_v, sin_v = cos_ref[rows, :], sin_ref[rows, :]
    qr = _rot(q_ref[rows, sl].astype(F32), cos_v, sin_v)
    kr = _rot(k_ref[rows, sl].astype(F32), cos_v, sin_v) * SCALE
    return sl, qr, kr, v_ref[rows, sl], cos_v, sin_v


def _f_ret_out(proj, cos2, sin2, s_f, s_b, rc, name):
    t = proj.shape[0]
    tm = RET_CHUNKS * C

    def body(q_ref, k_ref, v_ref, cos_ref, sin_ref, sf_ref, sb_ref, d_ref, xif_ref, xib_ref, o_ref):
        for ci in range(RET_CHUNKS):
            rows = slice(ci * C, (ci + 1) * C)
            for h in range(HEADS):
                sl, qr, kr, vh, _, _ = _ret_heads(q_ref, k_ref, v_ref, cos_ref, sin_ref, rows, h)
                p = (_dot_nt(qr.astype(BF16), kr.astype(BF16)) * d_ref[h]).astype(BF16)
                qx = jnp.concatenate([(qr * xif_ref[h]).astype(BF16), (qr * xib_ref[h]).astype(BF16)], axis=1)
                st = jnp.concatenate([sf_ref[ci, h], sb_ref[ci, h]], axis=0)
                o_ref[rows, sl] = _dot(p, vh) + _dot(qx, st)

    const = lambda: pl.BlockSpec((HEADS, C, DH), lambda i: (0, 0, 0))
    state = lambda: pl.BlockSpec((RET_CHUNKS, HEADS, DH, DH), lambda i: (i, 0, 0, 0))
    return pl.pallas_call(
        body, grid=(t // tm,), name=name,
        in_specs=[pl.BlockSpec((tm, RETW), lambda i, cb=cb: (i, cb)) for cb in (1, 2, 3)] +
                 [pl.BlockSpec((tm, DH), lambda i: (i, 0)), pl.BlockSpec((tm, DH), lambda i: (i, 0)),
                  state(), state(), const(), const(), const()],
        out_specs=pl.BlockSpec((tm, RETW), lambda i: (i, 0)),
        out_shape=S((t, RETW), F32),
        compiler_params=_params("parallel"),
    )(proj, proj, proj, cos2, sin2, s_f, s_b, rc["f"]["D"] + rc["b"]["D"], rc["f"]["XI"], rc["b"]["XI"])


def _conv_halo_specs(t, tm, width, col):
    r = tm // HALO
    last = t // HALO - 1
    return [pl.BlockSpec((HALO, width), lambda i: (jnp.maximum(i * r - 1, 0), col)),
            pl.BlockSpec((tm, width), lambda i: (i, col)),
            pl.BlockSpec((HALO, width), lambda i: (jnp.minimum((i + 1) * r, last), col))]


def _fill_ext(ext, prev, cur, nxt, i, nt, tm):
    ext[0:HALO, :] = jnp.where(i > 0, prev, 0.0)
    ext[HALO:HALO + tm, :] = cur
    ext[HALO + tm:2 * HALO + tm, :] = jnp.where(i < nt - 1, nxt, 0.0)


def _glu(a_ref, g_ref):
    return a_ref[...].astype(F32) * _sigmoid(g_ref[...].astype(F32))


def _f_conv(proj, cw, cb, lng, lnb, l, name, comm=None):
    t = proj.shape[0]
    tm = 256
    nt = t // tm
    rb = 64

    def body(ap, ac, an, gp, gc, gn, cw_ref, cb_ref, lng_ref, lnb_ref, c_ref, y_ref, hext):
        i = pl.program_id(0)
        _fill_ext(hext, _glu(ap, gp), _glu(ac, gc), _glu(an, gn), i, nt, tm)
        for r0 in range(0, tm, rb):
            acc = jnp.zeros((rb, CVW), F32) + cb_ref[...]
            for j in range(KW):
                acc = acc + cw_ref[j:j + 1, :] * hext[pl.ds(r0 + j + 1, rb), :]
            c_ref[r0:r0 + rb, :] = acc
            chat, _ = _standardize(acc)
            z = chat * lng_ref[...] + lnb_ref[...]
            y_ref[r0:r0 + rb, :] = (z * _sigmoid(z)).astype(BF16)

    vec = lambda: pl.BlockSpec((None, 1, CVW), lambda i: (l, 0, 0))
    return _launch(
        body, (nt,),
        _conv_halo_specs(t, tm, CVW, 10) + _conv_halo_specs(t, tm, CVW, 11) +
        [pl.BlockSpec((None, 32, CVW), lambda i: (l, 0, 0)), vec(), vec(), vec()],
        [pl.BlockSpec((tm, CVW), lambda i: (i, 0)), pl.BlockSpec((tm, CVW), lambda i: (i, 0))],
        [S((t, CVW), F32), S((t, CVW), BF16)],
        [pltpu.VMEM((tm + 2 * HALO, CVW), F32)],
        (proj, proj, proj, proj, proj, proj, cw, cb, lng, lnb), name, ("parallel",), comm)


def _f_mixout(x, y_gm, y_cv, o, proj, w, name):
    t = x.shape[0]
    tm = 512

    def body(x_ref, ygm_ref, ycv_ref, o_ref, g_ref, w_ref, xm_ref, ycat_t_ref, ycat):
        ycat[:, 0:GMW] = ygm_ref[...]
        ycat[:, GMW + RETW:D] = ycv_ref[...]
        for h in range(HEADS):
            sl = slice(h * DH, (h + 1) * DH)
            ohat, _ = _standardize(o_ref[:, sl])
            g = g_ref[:, sl].astype(F32)
            ycat[:, GMW + h * DH:GMW + (h + 1) * DH] = (ohat * (g * _sigmoid(g))).astype(BF16)
        yc = ycat[...]
        ycat_t_ref[...] = yc.T
        xm_ref[...] = x_ref[...] + _dot(yc, w_ref[...])

    return pl.pallas_call(
        body, grid=(t // tm,), name=name,
        in_specs=[pl.BlockSpec((tm, D), lambda i: (i, 0)),
                  pl.BlockSpec((tm, GMW), lambda i: (i, 0)),
                  pl.BlockSpec((tm, CVW), lambda i: (i, 0)),
                  pl.BlockSpec((tm, RETW), lambda i: (i, 0)),
                  pl.BlockSpec((tm, RETW), lambda i: (i, 4)),
                  _resident((D, D), lambda i: (0, 0))],
        out_specs=[pl.BlockSpec((tm, D), lambda i: (i, 0)), pl.BlockSpec((D, tm), lambda i: (0, i))],
        out_shape=[S((t, D), F32), S((D, t), BF16)],
        scratch_shapes=[pltpu.VMEM((tm, D), BF16)],
        compiler_params=_params("parallel"),
    )(x, y_gm, y_cv, o, proj, w)


def _f_ffn(xm, g2, w1, w2, l, name, comm=None):
    t = xm.shape[0]
    tm = 256
    half = NDEV // 2

    def body(x_ref, g_ref, w1_ref, w2_ref, xo_ref, ht_ref, gu_ref, act_t_ref):
        xv = x_ref[...]
        h = (xv * _rms(xv) * g_ref[...]).astype(BF16)
        ht_ref[...] = h.T
        acc = xv
        for j in range(half):
            gate = _dot(h, w1_ref[j])
            up = _dot(h, w1_ref[half + j])
            gu_ref[j] = gate.astype(BF16)
            gu_ref[half + j] = up.astype(BF16)
            a = ((gate * _sigmoid(gate)) * up).astype(BF16)
            act_t_ref[j] = a.T
            acc = acc + _dot(a, w2_ref[j * FFB:(j + 1) * FFB, :])
        xo_ref[...] = acc

    return _launch(
        body, (t // tm,),
        [pl.BlockSpec((tm, D), lambda i: (i, 0)),
         pl.BlockSpec((None, 1, D), lambda i: (l, 0, 0)),
         _resident((NDEV, D, FFB), lambda i: (0, 0, 0)),
         _resident((FFH, D), lambda i: (0, 0))],
        [pl.BlockSpec((tm, D), lambda i: (i, 0)), pl.BlockSpec((D, tm), lambda i: (0, i)),
         pl.BlockSpec((NDEV, tm, FFB), lambda i: (0, i, 0)), pl.BlockSpec((half, FFB, tm), lambda i: (0, 0, i))],
        [S((t, D), F32), S((D, t), BF16), S((NDEV, t, FFB), BF16), S((half, FFB, t), BF16)],
        [], (xm, g2, w1, w2), name, ("parallel",), comm)


def _b_loss(x, fg, tgt, name):
    t = x.shape[0]
    tm = 512

    def body(x_ref, g_ref, t_ref, loss_ref, dx_ref, dg_ref):
        @pl.when(pl.program_id(0) == 0)
        def _():
            loss_ref[...] = jnp.zeros_like(loss_ref)
            dg_ref[...] = jnp.zeros_like(dg_ref)

        xv = x_ref[...]
        r = _rms(xv)
        xr = xv * r
        err = xr * g_ref[...] - t_ref[...]
        loss_ref[...] += (0.5 / D) * _col_sum(jnp.sum(err * err, axis=1, keepdims=True))
        dy = err * (1.0 / D)
        dg_ref[...] += _col_sum(dy * xr)
        dx_ref[...] = _rmsnorm_bwd(dy, xv, r, g_ref[...])

    return pl.pallas_call(
        body, grid=(t // tm,), name=name,
        in_specs=[pl.BlockSpec((tm, D), lambda i: (i, 0)), pl.BlockSpec((1, D), lambda i: (0, 0)),
                  pl.BlockSpec((tm, D), lambda i: (i, 0))],
        out_specs=[pl.BlockSpec((1, 1), lambda i: (0, 0)), pl.BlockSpec((tm, D), lambda i: (i, 0)),
                   pl.BlockSpec((1, D), lambda i: (0, 0))],
        out_shape=[S((1, 1), F32), S((t, D), F32), S((1, D), F32)],
        compiler_params=_params("arbitrary"),
    )(x, fg, tgt)


def _b_ffn(dxo, xm, g2, gu, w1, w2, l, name, comm=None):
    t = xm.shape[0]
    tm = 256
    half = NDEV // 2

    def body(dxo_ref, x_ref, g_ref, gu_ref, w1_ref, w2_ref, dgu_ref, dxm_ref, dxb_ref, dg_ref):
        @pl.when(pl.program_id(0) == 0)
        def _():
            dg_ref[...] = jnp.zeros_like(dg_ref)

        dxo = dxo_ref[...]
        dxb = dxo.astype(BF16)
        dxb_ref[...] = dxb
        dh = jnp.zeros((tm, D), F32)
        for j in range(half):
            dact = _dot_nt(dxb, w2_ref[j * FFB:(j + 1) * FFB, :])
            gate = gu_ref[j].astype(F32)
            up = gu_ref[half + j].astype(F32)
            sg, dsg = _silu_and_grad(gate)
            dgate = (dact * up * dsg).astype(BF16)
            dup = (dact * sg).astype(BF16)
            dgu_ref[j] = dgate
            dgu_ref[half + j] = dup
            dh = dh + _dot_nt(dgate, w1_ref[j]) + _dot_nt(dup, w1_ref[half + j])
        xv = x_ref[...]
        r = _rms(xv)
        dg_ref[...] += _col_sum(dh * xv * r)
        dxm_ref[...] = dxo + _rmsnorm_bwd(dh, xv, r, g_ref[...])

    return _launch(
        body, (t // tm,),
        [pl.BlockSpec((tm, D), lambda i: (i, 0)), pl.BlockSpec((tm, D), lambda i: (i, 0)),
         pl.BlockSpec((None, 1, D), lambda i: (l, 0, 0)),
         pl.BlockSpec((NDEV, tm, FFB), lambda i: (0, i, 0)),
         _resident((NDEV, D, FFB), lambda i: (0, 0, 0)),
         _resident((FFH, D), lambda i: (0, 0))],
        [pl.BlockSpec((NDEV, tm, FFB), lambda i: (0, i, 0)), pl.BlockSpec((tm, D), lambda i: (i, 0)),
         pl.BlockSpec((tm, D), lambda i: (i, 0)), pl.BlockSpec((1, D), lambda i: (0, 0))],
        [S((NDEV, t, FFB), BF16), S((t, D), F32), S((t, D), BF16), S((1, D), F32)],
        [], (dxo, xm, g2, gu, w1, w2), name, ("arbitrary",), comm)


def _mm_wgrad(at, b, pieces, at_lead, b_mode, group, name):
    bt = 1024
    t = at.shape[-1]
    bt = min(bt, t)
    nt = t // bt
    ka = at.shape[-2]
    if at_lead:
        a_spec = pl.BlockSpec((None, ka, bt), lambda j, tt: (j, 0, tt))
    else:
        a_spec = pl.BlockSpec((ka, bt), lambda j, tt: (0, tt))
    if b_mode == "shared":
        nb, b_spec = b.shape[1], pl.BlockSpec((bt, b.shape[1]), lambda j, tt: (tt, 0))
    elif b_mode == "cols":
        nb = b.shape[1] // pieces
        b_spec = pl.BlockSpec((bt, group * nb), lambda j, tt: (tt, j))
    else:
        nb, b_spec = b.shape[2], pl.BlockSpec((None, bt, b.shape[2]), lambda j, tt: (j, tt, 0))
    assert group == 1 or b_mode == "cols"

    def body(a_ref, b_ref, o_ref, acc):
        tt = pl.program_id(1)

        @pl.when(tt == 0)
        def _():
            acc[...] = jnp.zeros_like(acc)

        acc[...] += _dot(a_ref[...], b_ref[...])

        @pl.when(tt == nt - 1)
        def _():
            for k in range(group):
                o_ref[k] = acc[:, k * nb:(k + 1) * nb].astype(BF16)

    return pl.pallas_call(
        body, grid=(pieces // group, nt), name=name,
        in_specs=[a_spec, b_spec],
        out_specs=pl.BlockSpec((group, ka, nb), lambda j, tt: (j, 0, 0)),
        out_shape=S((pieces, ka, nb), BF16),
        scratch_shapes=[pltpu.VMEM((ka, group * nb), F32)],
        compiler_params=_params("parallel", "arbitrary"),
    )(at, b)


def _b_mixout(dxm, w, o, proj, c, lng, lnb, l, name):
    t = dxm.shape[0]
    tm = 256

    def body(dxm_ref, w_ref, o_ref, g_ref, c_ref, lng_ref, lnb_ref,
             dxb_ref, dygm_ref, dO_ref, dg_ref, dc_ref, dlg_ref, dlb_ref, dcb_ref):
        @pl.when(pl.program_id(0) == 0)
        def _():
            dlg_ref[...] = jnp.zeros_like(dlg_ref)
            dlb_ref[...] = jnp.zeros_like(dlb_ref)
            dcb_ref[...] = jnp.zeros_like(dcb_ref)

        dxb = dxm_ref[...].astype(BF16)
        dxb_ref[...] = dxb
        dy = _dot_nt(dxb, w_ref[...])
        dygm_ref[...] = dy[:, 0:GMW]
        for h in range(HEADS):
            sl = slice(h * DH, (h + 1) * DH)
            ohat, rstd = _standardize(o_ref[:, sl])
            sg, dsg = _silu_and_grad(g_ref[:, sl].astype(F32))
            dyr = dy[:, GMW + h * DH:GMW + (h + 1) * DH]
            dg_ref[:, sl] = (dyr * ohat * dsg).astype(BF16)
            dO_ref[:, sl] = _standardize_bwd(dyr * sg, ohat, rstd)
        chat, rstd = _standardize(c_ref[...])
        z = chat * lng_ref[...] + lnb_ref[...]
        _, dsz = _silu_and_grad(z)
        dz = dy[:, GMW + RETW:D] * dsz
        dlg_ref[...] += _col_sum(dz * chat)
        dlb_ref[...] += _col_sum(dz)
        dc = _standardize_bwd(dz * lng_ref[...], chat, rstd)
        dcb_ref[...] += _col_sum(dc)
        dc_ref[...] = dc

    vec = lambda: pl.BlockSpec((None, 1, CVW), lambda i: (l, 0, 0))
    acc = lambda: pl.BlockSpec((1, CVW), lambda i: (0, 0))
    return pl.pallas_call(
        body, grid=(t // tm,), name=name,
        in_specs=[pl.BlockSpec((tm, D), lambda i: (i, 0)),
                  _resident((D, D), lambda i: (0, 0)),
                  pl.BlockSpec((tm, RETW), lambda i: (i, 0)),
                  pl.BlockSpec((tm, RETW), lambda i: (i, 4)),
                  pl.BlockSpec((tm, CVW), lambda i: (i, 0)), vec(), vec()],
        out_specs=[pl.BlockSpec((tm, D), lambda i: (i, 0)), pl.BlockSpec((tm, GMW), lambda i: (i, 0)),
                   pl.BlockSpec((tm, RETW), lambda i: (i, 0)), pl.BlockSpec((tm, RETW), lambda i: (i, 0)),
                   pl.BlockSpec((tm, CVW), lambda i: (i, 0)), acc(), acc(), acc()],
        out_shape=[S((t, D), BF16), S((t, GMW), F32), S((t, RETW), F32), S((t, RETW), BF16), S((t, CVW), F32),
                   S((1, CVW), F32), S((1, CVW), F32), S((1, CVW), F32)],
        compiler_params=_params("arbitrary"),
    )(dxm, w, o, proj, c, lng, lnb)


def _b_gm(proj, dy, lng, lnb, ws_bf, wst_bf, bias, l, name):
    t = proj.shape[0]
    tm = 512
    nt = t // tm

    def body(p_ref, dy_ref, lng_ref, lnb_ref, ws_ref, wst_ref, bias_ref,
             duv_ref, dws_ref, dbias_ref, dbs_ref, dlg_ref, dlb_ref):
        @pl.when(pl.program_id(0) == 0)
        def _():
            dws_ref[...] = jnp.zeros_like(dws_ref)
            dbias_ref[...] = jnp.zeros_like(dbias_ref)
            dbs_ref[...] = jnp.zeros_like(dbs_ref)
            dlg_ref[...] = jnp.zeros_like(dlg_ref)
            dlb_ref[...] = jnp.zeros_like(dlb_ref)

        for ci in range(tm // C):
            rows = slice(ci * C, (ci + 1) * C)
            u = p_ref[rows, 0:GMW].astype(F32)
            v = p_ref[rows, GMW:2 * GMW].astype(F32)
            au, dau, dav, vhat, rstd, vn, mixed, head = _gm_chunk_fwd(u, v, lng_ref[...], lnb_ref[...], ws_ref, bias_ref[...])
            dyc = dy_ref[rows, :]
            dmixed = dyc * au
            dmb = dmixed.astype(BF16)
            dbias_ref[...] += dmixed
            dvn = jnp.zeros((C, GMW), F32)
            for h in range(HEADS):
                dws_ref[h] += _dot_nt(jnp.where(head == h, dmixed, 0.0).astype(BF16), vn)
                dvn = dvn + jnp.where(head == h, _dot(wst_ref[h], dmb), 0.0)
            dlg_ref[...] += _col_sum(dvn * vhat)
            dlb_ref[...] += _col_sum(dvn)
            dav_in = _standardize_bwd(dvn * lng_ref[...], vhat, rstd)
            duv_ref[rows, 0:GMW] = (dyc * mixed * dau).astype(BF16)
            duv_ref[rows, GMW:2 * GMW] = (dav_in * dav).astype(BF16)

        @pl.when(pl.program_id(0) == nt - 1)
        def _():
            head = lax.broadcasted_iota(jnp.int32, (C, GMW), 1) // (GMW // HEADS)
            lane = lax.broadcasted_iota(jnp.int32, (C, 128), 1)
            fold = jnp.zeros((C, 128), F32)
            for h in range(HEADS):
                col = jnp.sum(jnp.where(head == h, dbias_ref[...], 0.0), axis=1, keepdims=True)
                fold = jnp.where(lane == h, col, fold)
            dbs_ref[...] = fold

    vec = lambda: pl.BlockSpec((None, 1, GMW), lambda i: (l, 0, 0))
    mats = lambda: pl.BlockSpec((None, HEADS, C, C), lambda i: (l, 0, 0, 0))
    return pl.pallas_call(
        body, grid=(nt,), name=name,
        in_specs=[pl.BlockSpec((tm, 2 * GMW), lambda i: (i, 0)), pl.BlockSpec((tm, GMW), lambda i: (i, 0)),
                  vec(), vec(), mats(), mats(), pl.BlockSpec((None, C, GMW), lambda i: (l, 0, 0))],
        out_specs=[pl.BlockSpec((tm, 2 * GMW), lambda i: (i, 0)),
                   pl.BlockSpec((HEADS, C, C), lambda i: (0, 0, 0)),
                   pl.BlockSpec((C, GMW), lambda i: (0, 0)), pl.BlockSpec((C, 128), lambda i: (0, 0)),
                   pl.BlockSpec((1, GMW), lambda i: (0, 0)), pl.BlockSpec((1, GMW), lambda i: (0, 0))],
        out_shape=[S((t, 2 * GMW), BF16), S((HEADS, C, C), F32), S((C, GMW), F32), S((C, 128), F32),
                   S((1, GMW), F32), S((1, GMW), F32)],
        compiler_params=_params("arbitrary"),
    )(proj, dy, lng, lnb, ws_bf, wst_bf, bias)


def _b_conv(proj, dc, cw, l, name, comm=None):
    t = proj.shape[0]
    tm = 256
    nt = t // tm
    rb = 64

    def body(ap, ac, an, gp, gc, gn, dp, dcur, dn, cw_ref, dag_ref, dcw_ref, hext, dext):
        i = pl.program_id(0)

        @pl.when(i == 0)
        def _():
            dcw_ref[...] = jnp.zeros_like(dcw_ref)

        _fill_ext(hext, _glu(ap, gp), _glu(ac, gc), _glu(an, gn), i, nt, tm)
        _fill_ext(dext, dp[...], dcur[...], dn[...], i, nt, tm)
        for j in range(KW):
            dcw_ref[j:j + 1, :] += _col_sum(dcur[...] * hext[pl.ds(j + 1, tm), :])
        for r0 in range(0, tm, rb):
            dh = jnp.zeros((rb, CVW), F32)
            for j in range(KW):
                dh = dh + cw_ref[j:j + 1, :] * dext[pl.ds(r0 + 2 * HALO - 1 - j, rb), :]
            a = ac[r0:r0 + rb, :].astype(F32)
            s = _sigmoid(gc[r0:r0 + rb, :].astype(F32))
            dag_ref[r0:r0 + rb, 0:CVW] = (dh * s).astype(BF16)
            dag_ref[r0:r0 + rb, CVW:2 * CVW] = (dh * a * s * (1.0 - s)).astype(BF16)

    dspecs = _conv_halo_specs(t, tm, CVW, 0)
    return _launch(
        body, (nt,),
        _conv_halo_specs(t, tm, CVW, 10) + _conv_halo_specs(t, tm, CVW, 11) + dspecs +
        [pl.BlockSpec((None, 32, CVW), lambda i: (l, 0, 0))],
        [pl.BlockSpec((tm, 2 * CVW), lambda i: (i, 0)), pl.BlockSpec((32, CVW), lambda i: (0, 0))],
        [S((t, 2 * CVW), BF16), S((32, CVW), F32)],
        [pltpu.VMEM((tm + 2 * HALO, CVW), F32), pltpu.VMEM((tm + 2 * HALO, CVW), F32)],
        (proj, proj, proj, proj, proj, proj, dc, dc, dc, cw), name, ("arbitrary",), comm)


def _b_ret_out(proj, cos2, sin2, dO, s_f, s_b, g_f, g_b, rc, name):
    t = proj.shape[0]
    tm = RET_CHUNKS * C

    def body(q_ref, k_ref, v_ref, cos_ref, sin_ref, dO_ref, sf_ref, sb_ref, gf_ref, gb_ref,
             d_ref, xif_ref, xib_ref, zef_ref, zeb_ref, dq_ref, dk_ref, dv_ref):
        for ci in range(RET_CHUNKS):
            rows = slice(ci * C, (ci + 1) * C)
            for h in range(HEADS):
                sl, qr, kr, vh, cos_v, sin_v = _ret_heads(q_ref, k_ref, v_ref, cos_ref, sin_ref, rows, h)
                qh, kh = qr.astype(BF16), kr.astype(BF16)
                dOh = dO_ref[rows, sl].astype(BF16)
                st = jnp.concatenate([sf_ref[ci, h], sb_ref[ci, h]], axis=0)
                gr = jnp.concatenate([gf_ref[ci, h], gb_ref[ci, h]], axis=0)
                dm = d_ref[h]
                p = (_dot_nt(qh, kh) * dm).astype(BF16)
                dp = (_dot_nt(dOh, vh) * dm).astype(BF16)
                from_s = _dot_nt(dOh, st)
                from_g = _dot_nt(vh, gr)
                dqr = _dot(dp, kh) + xif_ref[h] * from_s[:, 0:DH] + xib_ref[h] * from_s[:, DH:2 * DH]
                dkr = (_dot_tn(dp, qh) + zef_ref[h] * from_g[:, 0:DH] + zeb_ref[h] * from_g[:, DH:2 * DH]) * SCALE
                kz = jnp.concatenate([(kr * zef_ref[h]).astype(BF16), (kr * zeb_ref[h]).astype(BF16)], axis=1)
                dv = _dot_tn(p, dOh) + _dot(kz, gr)
                dq_ref[rows, sl] = _rot_t(dqr, cos_v, sin_v).astype(BF16)
                dk_ref[rows, sl] = _rot_t(dkr, cos_v, sin_v).astype(BF16)
                dv_ref[rows, sl] = dv.astype(BF16)

    const = lambda: pl.BlockSpec((HEADS, C, DH), lambda i: (0, 0, 0))
    state = lambda: pl.BlockSpec((RET_CHUNKS, HEADS, DH, DH), lambda i: (i, 0, 0, 0))
    tok = lambda: pl.BlockSpec((tm, RETW), lambda i: (i, 0))
    return pl.pallas_call(
        body, grid=(t // tm,), name=name,
        in_specs=[pl.BlockSpec((tm, RETW), lambda i, cb=cb: (i, cb)) for cb in (1, 2, 3)] +
                 [pl.BlockSpec((tm, DH), lambda i: (i, 0)), pl.BlockSpec((tm, DH), lambda i: (i, 0)), tok(),
                  state(), state(), state(), state()] + [const() for _ in range(5)],
        out_specs=[tok(), tok(), tok()],
        out_shape=[S((t, RETW), BF16) for _ in range(3)],
        compiler_params=_params("parallel"),
    )(proj, proj, proj, cos2, sin2, dO, s_f, s_b, g_f, g_b, rc["f"]["D"] + rc["b"]["D"],
      rc["f"]["XI"], rc["b"]["XI"], rc["f"]["ZETA"], rc["b"]["ZETA"])


def _b_inproj(d_uv, dqkv, d_g, d_ag, w, x, g1, dxm, l, name, comm=None):
    t = x.shape[0]
    tm = 256

    def body(duv_ref, dq_ref, dk_ref, dv_ref, dg_ref, dag_ref, w_ref, x_ref, g_ref, dxm_ref,
             dp_ref, dx_ref, dn_ref):
        @pl.when(pl.program_id(0) == 0)
        def _():
            dn_ref[...] = jnp.zeros_like(dn_ref)

        for k, part in enumerate((duv_ref, dq_ref, dk_ref, dv_ref, dg_ref, dag_ref)):
            dp_ref[:, 512 * k:512 * (k + 1)] = part[...]
        dh = _dot_nt(dp_ref[...], w_ref[...])
        xv = x_ref[...]
        r = _rms(xv)
        dn_ref[...] += _col_sum(dh * xv * r)
        dx_ref[...] = dxm_ref[...] + _rmsnorm_bwd(dh, xv, r, g_ref[...])

    half = lambda: pl.BlockSpec((tm, 512), lambda i: (i, 0))
    full = lambda: pl.BlockSpec((tm, D), lambda i: (i, 0))
    return _launch(
        body, (t // tm,),
        [half() for _ in range(6)] +
        [_resident((D, INW), lambda i: (0, 0)), full(), pl.BlockSpec((None, 1, D), lambda i: (l, 0, 0)), full()],
        [pl.BlockSpec((tm, INW), lambda i: (i, 0)), full(), pl.BlockSpec((1, D), lambda i: (0, 0))],
        [S((t, INW), BF16), S((t, D), F32), S((1, D), F32)], [],
        (d_uv, *dqkv, d_g, d_ag, w, x, g1, dxm), name, ("arbitrary",), comm)


def _pair_exchange(gs, name):
    n = len(gs)

    def body(*refs):
        g, q, ssem, rsem = refs[:n], refs[n:2 * n], refs[2 * n], refs[2 * n + 1]
        x, y, c, _ = _place()
        copies = [pltpu.make_async_remote_copy(src_ref=g[u].at[2 * chip + (1 - c)], dst_ref=q[u].at[chip],
                                               send_sem=ssem.at[u, chip], recv_sem=rsem.at[u, chip],
                                               device_id=(x, y, 1 - c), device_id_type=MESH)
                  for u in range(n) for chip in range(4)]
        for cp in copies:
            cp.start()
        for cp in copies:
            cp.wait_recv()
        for cp in copies:
            cp.wait_send()

    return pl.pallas_call(body, name=name, in_specs=[_HBM] * n, out_specs=[_HBM] * n,
                          out_shape=[S((4,) + g.shape[1:], BF16) for g in gs],
                          scratch_shapes=[pltpu.SemaphoreType.DMA((n, 4)), pltpu.SemaphoreType.DMA((n, 4))],
                          compiler_params=_params())(*gs)


def _pair_add(g, q, name):
    _, mm, nn = g.shape
    bm = 256 if mm % 256 == 0 else mm

    def body(g_ref, q_ref, h_ref):
        h_ref[...] = (g_ref[lax.axis_index("c")].astype(F32) + q_ref[...].astype(F32)).astype(BF16)

    return pl.pallas_call(
        body, grid=(4, mm // bm), name=name,
        in_specs=[pl.BlockSpec((None, 2, bm, nn), lambda qq, i: (qq, 0, i, 0)),
                  pl.BlockSpec((None, bm, nn), lambda qq, i: (qq, i, 0))],
        out_specs=pl.BlockSpec((None, bm, nn), lambda qq, i: (qq, i, 0)),
        out_shape=S((4, mm, nn), BF16),
        compiler_params=_params("parallel", "parallel"),
    )(g.reshape(4, 2, mm, nn), q)


def _pair_reduce(named, l):
    names = [k for k, _ in named]
    qs = _pair_exchange([g for _, g in named], f"pair_exchange_{names[0]}_{l}")
    return [_pair_add(g, q, f"pair_add_{k}_{l}") for (k, g), q in zip(named, qs)]


_BIG = ("w_in", "w_out", "w_ffn_in", "w_ffn_out")
_KIND = dict(w_in="cols", w_out="rows", w_ffn_in="lead", w_ffn_out="rows")
CWP = 128
SMALL_ROWS = 152


def _step(x, tgt, wts, sh, cw_pad):
    t = x.shape[0]
    rc = _ret_consts()
    cos2, sin2 = (jnp.asarray(a) for a in _rope_tables(t))
    n1 = wts["norm1_g"].reshape(LAYERS, 1, D)
    n2 = wts["norm2_g"].reshape(LAYERS, 1, D)
    gm_lng = wts["gm_ln_g"].reshape(LAYERS, 1, GMW)
    gm_lnb = wts["gm_ln_b"].reshape(LAYERS, 1, GMW)
    ws_bf = wts["gm_ws"].astype(BF16)
    wst_bf = jnp.swapaxes(wts["gm_ws"], 2, 3).astype(BF16)
    bias = jnp.repeat(jnp.swapaxes(wts["gm_bs"], 1, 2), GMW // HEADS, axis=2)
    cb = wts["conv_b"].reshape(LAYERS, 1, CVW)
    cv_lng = wts["conv_ln_g"].reshape(LAYERS, 1, CVW)
    cv_lnb = wts["conv_ln_b"].reshape(LAYERS, 1, CVW)
    unit = lambda f, l: (sh[f][l], _KIND[f])
    cshard = CVW // NDEV

    full = {f: [None] * LAYERS for f in _BIG}
    full["w_in"][0], cw_all = _comm_only(_Gather([unit("w_in", 0), (cw_pad, "lead")]), "gather_first")
    cw = jnp.transpose(cw_all[:, :, :, :cshard], (1, 2, 0, 3)).reshape(LAYERS, 32, CVW)

    zeta_f, zeta_b = rc["f"]["ZETA"] * np.float32(SCALE), rc["b"]["ZETA"] * np.float32(SCALE)
    gcf, gcb = rc["f"]["gC"], rc["b"]["gC"]
    saved = []
    for l in range(LAYERS):
        first = l == 0
        (proj, h1), got = _f_inproj(x, n1, full["w_in"][l], l, f"f_inproj_{l}",
                                    _Gather([unit("w_ffn_in", 0)]) if first else None)
        if first:
            full["w_ffn_in"][0], = got
        y_gm = _f_gm(proj, gm_lng, gm_lnb, ws_bf, bias, l, f"f_gm_{l}")
        s_f, s_b = _scan_pair(proj, cos2, sin2, None, 2, zeta_f, zeta_b, gcf, gcb, True, f"f_ret_state_{l}")
        o = _f_ret_out(proj, cos2, sin2, s_f, s_b, rc, f"f_ret_out_{l}")
        (c, y_cv), got = _f_conv(proj, cw, cb, cv_lng, cv_lnb, l, f"f_conv_{l}",
                                 _Gather([unit("w_out", 0), unit("w_ffn_out", 0)]) if first else None)
        if first:
            full["w_out"][0], full["w_ffn_out"][0] = got
        xm, ycat = _f_mixout(x, y_gm, y_cv, o, proj, full["w_out"][l], f"f_mixout_{l}")
        (xo, h2, gu, act), got = _f_ffn(xm, n2, full["w_ffn_in"][l], full["w_ffn_out"][l], l, f"f_ffn_{l}",
                                        _Gather([unit(f, 1) for f in _BIG]) if first else None)
        if first:
            full["w_in"][1], full["w_out"][1], full["w_ffn_in"][1], full["w_ffn_out"][1] = got
        saved.append(dict(x=x, proj=proj, h1=h1, o=o, s_f=s_f, s_b=s_b, c=c, xm=xm, ycat=ycat, h2=h2, gu=gu, act=act))
        x = xo

    loss, dx, d_final = _b_loss(x, wts["final_g"].reshape(1, D), tgt, "b_loss")

    parts = {f: [None] * LAYERS for f in _BIG}
    small = [None] * LAYERS
    upper = None
    for l in reversed(range(LAYERS)):
        sv = saved[l]
        (dgu, dxm, dxo_bf, d_n2), got = _b_ffn(dx, sv["xm"], n2, sv["gu"], full["w_ffn_in"][l], full["w_ffn_out"][l], l,
                                               f"b_ffn_{l}", _Scatter(upper) if upper else None)
        if upper:
            for f, p in zip(_BIG, got):
                parts[f][l + 1] = p
        g_f2 = _mm_wgrad(sv["act"], dxo_bf, NDEV // 2, True, "shared", 1, f"g_ffn_out_{l}").reshape(NDEV, FFH // NDEV, D)
        g_f1 = _mm_wgrad(sv["h2"], dgu, NDEV, False, "lead", 1, f"g_ffn_in_{l}")
        dxm_bf, dy_gm, dO, d_g, dc, d_cvlg, d_cvlb, d_cb = _b_mixout(
            dxm, full["w_out"][l], sv["o"], sv["proj"], sv["c"], cv_lng, cv_lnb, l, f"b_mixout_{l}")
        g_out = _mm_wgrad(sv["ycat"], dxm_bf, 1, False, "shared", 1, f"g_out_{l}").reshape(NDEV, D // NDEV, D)
        last = l == 0
        early = _pair_reduce([("w_out", g_out), ("w_ffn_in", g_f1), ("w_ffn_out", g_f2)], l) if last else None
        d_uv, d_ws, _, d_bs_fold, d_gmlg, d_gmlb = _b_gm(sv["proj"], dy_gm, gm_lng, gm_lnb, ws_bf, wst_bf, bias, l, f"b_gm_{l}")
        (d_ag, d_cw), got = _b_conv(sv["proj"], dc, cw, l, f"b_conv_{l}", _Scatter(early[1:2]) if last else None)
        if last:
            parts["w_ffn_in"][l], = got
        g_f, g_b = _scan_pair(sv["proj"], cos2, sin2, dO, 1, rc["f"]["XI"], rc["b"]["XI"], gcf, gcb, False, f"b_ret_state_{l}")
        dqkv = _b_ret_out(sv["proj"], cos2, sin2, dO, sv["s_f"], sv["s_b"], g_f, g_b, rc, f"b_ret_out_{l}")
        (dproj, dx, d_n1), got = _b_inproj(d_uv, dqkv, d_g, d_ag, full["w_in"][l], sv["x"], n1, dxm, l, f"b_inproj_{l}",
                                           _Scatter([early[0], early[2]]) if last else None)
        if last:
            parts["w_out"][l], parts["w_ffn_out"][l] = got
        g_in = _mm_wgrad(sv["h1"], dproj, NDEV, False, "cols", 2, f"g_in_{l}")
        if last:
            tail = _pair_reduce([("w_in", g_in)], l)
        else:
            upper = _pair_reduce([("w_in", g_in), ("w_out", g_out), ("w_ffn_in", g_f1), ("w_ffn_out", g_f2)], l)
        small[l] = dict(norm1_g=d_n1[0], gm_ln_g=d_gmlg[0], gm_ln_b=d_gmlb[0], gm_ws=d_ws,
                        gm_bs=d_bs_fold[:, :HEADS].T, conv_w=d_cw[:KW], conv_b=d_cb[0], conv_ln_g=d_cvlg[0],
                        conv_ln_b=d_cvlb[0], norm2_g=d_n2[0])
    small_g = {k: jnp.stack([small[l][k] for l in range(LAYERS)]) for k in small[0]}
    small_g["final_g"] = d_final[0]
    small_buf = _pack([small_g[k] for k in _SMALL], SMALL_ROWS)
    parts["w_in"][0], small_parts = _comm_only(_Comms([_Scatter(tail), _Gather([(small_buf, "lead")])]), "exchange_last")
    return loss, dx, parts, small_parts


def _adamw(w, g, m, v):
    m = ADAM_B1 * m + (1.0 - ADAM_B1) * g
    v = ADAM_B2 * v + (1.0 - ADAM_B2) * (g * g)
    m_hat = m / (1.0 - ADAM_B1 ** ADAM_STEP)
    v_hat = v / (1.0 - ADAM_B2 ** ADAM_STEP)
    return -ADAM_LR * (m_hat / (jnp.sqrt(v_hat) + ADAM_EPS) + ADAM_WD * w), m, v


def _cast_blocks(ws):
    def body(*refs):
        ins, outs = refs[:len(ws)], refs[len(ws):]
        for k, src in enumerate(ins):
            for l in range(LAYERS):
                outs[k * LAYERS + l][...] = src[l].astype(BF16)

    outs = pl.pallas_call(body, name="cast_blocks", out_shape=[S(w.shape[1:], BF16) for w in ws for _ in range(LAYERS)],
                          compiler_params=_params())(*ws)
    return [list(outs[k * LAYERS:(k + 1) * LAYERS]) for k in range(len(ws))]


def _sum_adam(parts, w, m, v, l, prev, name):
    _, mm, nn = parts.shape
    bm = 256 if mm % 256 == 0 else mm

    def body(p_ref, w_ref, m_ref, v_ref, *rest):
        g_ref, d_ref, nm_ref, nv_ref = rest[-4:]
        g = p_ref[0].astype(F32)
        for s in range(1, 4):
            g = g + p_ref[s].astype(F32)
        g_ref[...] = g
        d_ref[...], nm_ref[...], nv_ref[...] = _adamw(w_ref[...], g, m_ref[...], v_ref[...])

    blk = lambda: pl.BlockSpec((None, bm, nn), lambda i: (l, i, 0))
    prev = list(prev) if prev else []
    return pl.pallas_call(
        body, grid=(mm // bm,), name=name,
        in_specs=[pl.BlockSpec((4, bm, nn), lambda i: (0, i, 0)), blk(), blk(), blk()] + [_ANY] * len(prev),
        out_specs=[blk() for _ in range(4)],
        out_shape=[S(w.shape, F32) for _ in range(4)],
        input_output_aliases={4 + j: j for j in range(len(prev))},
        compiler_params=_params("parallel"),
    )(parts, w, m, v, *prev)


def _sum_small(parts):
    def body(p_ref, o_ref):
        g = p_ref[0]
        for s in range(1, NDEV):
            g = g + p_ref[s]
        o_ref[...] = g

    return pl.pallas_call(body, name="sum_small", out_shape=S(parts.shape[1:], F32),
                          compiler_params=_params())(parts)


def _adam_small(g, w, m, v):
    def body(g_ref, w_ref, m_ref, v_ref, d_ref, nm_ref, nv_ref):
        d_ref[...], nm_ref[...], nv_ref[...] = _adamw(w_ref[...], g_ref[...], m_ref[...], v_ref[...])

    return pl.pallas_call(body, name="adam_small", out_shape=[S(g.shape, F32)] * 3, compiler_params=_params())(g, w, m, v)


_SMALL = ("norm1_g", "gm_ln_g", "gm_ln_b", "gm_ws", "gm_bs", "conv_w", "conv_b", "conv_ln_g", "conv_ln_b",
          "norm2_g", "final_g")
_NAMES = ("norm1_g", "w_in", "gm_ln_g", "gm_ln_b", "gm_ws", "gm_bs", "conv_w", "conv_b", "conv_ln_g", "conv_ln_b",
          "w_out", "norm2_g", "w_ffn_in", "w_ffn_out", "final_g")


def _pack(parts, rows):
    flat = jnp.concatenate([p.reshape(-1) for p in parts])
    return jnp.pad(flat, (0, rows * 1024 - flat.shape[0])).reshape(rows, 1024)


def _unpack(buf, shapes):
    flat = buf.reshape(-1)
    out, o = [], 0
    for shp in shapes:
        sz = int(np.prod(shp))
        out.append(flat[o:o + sz].reshape(shp))
        o += sz
    return out


def kernel(x, norm1_g, w_in, gm_ln_g, gm_ln_b, gm_ws, gm_bs, conv_w, conv_b, conv_ln_g, conv_ln_b, w_out, norm2_g, w_ffn_in, w_ffn_out, final_g, loss_target, m_norm1_g, m_w_in, m_gm_ln_g, m_gm_ln_b, m_gm_ws, m_gm_bs, m_conv_w, m_conv_b, m_conv_ln_g, m_conv_ln_b, m_w_out, m_norm2_g, m_w_ffn_in, m_w_ffn_out, m_final_g, v_norm1_g, v_w_in, v_gm_ln_g, v_gm_ln_b, v_gm_ws, v_gm_bs, v_conv_w, v_conv_b, v_conv_ln_g, v_conv_ln_b, v_w_out, v_norm2_g, v_w_ffn_in, v_w_ffn_out, v_final_g):
    w = dict(norm1_g=norm1_g, w_in=w_in, gm_ln_g=gm_ln_g, gm_ln_b=gm_ln_b, gm_ws=gm_ws, gm_bs=gm_bs, conv_w=conv_w,
             conv_b=conv_b, conv_ln_g=conv_ln_g, conv_ln_b=conv_ln_b, w_out=w_out, norm2_g=norm2_g, w_ffn_in=w_ffn_in,
             w_ffn_out=w_ffn_out, final_g=final_g)
    mo = dict(norm1_g=m_norm1_g, w_in=m_w_in, gm_ln_g=m_gm_ln_g, gm_ln_b=m_gm_ln_b, gm_ws=m_gm_ws, gm_bs=m_gm_bs,
              conv_w=m_conv_w, conv_b=m_conv_b, conv_ln_g=m_conv_ln_g, conv_ln_b=m_conv_ln_b, w_out=m_w_out,
              norm2_g=m_norm2_g, w_ffn_in=m_w_ffn_in, w_ffn_out=m_w_ffn_out, final_g=m_final_g)
    vo = dict(norm1_g=v_norm1_g, w_in=v_w_in, gm_ln_g=v_gm_ln_g, gm_ln_b=v_gm_ln_b, gm_ws=v_gm_ws, gm_bs=v_gm_bs,
              conv_w=v_conv_w, conv_b=v_conv_b, conv_ln_g=v_conv_ln_g, conv_ln_b=v_conv_ln_b, w_out=v_w_out,
              norm2_g=v_norm2_g, w_ffn_in=v_w_ffn_in, w_ffn_out=v_w_ffn_out, final_g=v_final_g)
    t = x.shape[1]
    me = 4 * lax.axis_index("x") + 2 * lax.axis_index("y") + lax.axis_index("c")
    cshard = conv_w.shape[2]

    cw_pad = jnp.pad(conv_w, ((0, 0), (0, 32 - KW), (0, CWP - cshard)))
    sh = dict(zip(_BIG, _cast_blocks([w[f] for f in _BIG])))
    loss, dx, parts, small_parts = _step(x.reshape(t, D), loss_target.reshape(t, D), w, sh, cw_pad)

    grads, delta, new_m, new_v = {}, {}, {}, {}
    for f in _BIG:
        outs = None
        for l in reversed(range(LAYERS)):
            outs = _sum_adam(parts[f][l], w[f], mo[f], vo[f], l, outs, f"sum_adam_{f}_{l}")
        grads[f], delta[f], new_m[f], new_v[f] = outs

    full_shapes = [(LAYERS, KW, CVW) if k == "conv_w" else w[k].shape for k in _SMALL]
    for k, g in zip(_SMALL, _unpack(_sum_small(small_parts), full_shapes)):
        grads[k] = lax.dynamic_slice_in_dim(g, me * cshard, cshard, axis=2) if k == "conv_w" else g
    adam_rows = 144
    d_s, m_s, v_s = _adam_small(_pack([grads[k] for k in _SMALL], adam_rows), _pack([w[k] for k in _SMALL], adam_rows),
                                _pack([mo[k] for k in _SMALL], adam_rows), _pack([vo[k] for k in _SMALL], adam_rows))
    shapes = [w[k].shape for k in _SMALL]
    for dst, buf in ((delta, d_s), (new_m, m_s), (new_v, v_s)):
        for k, a in zip(_SMALL, _unpack(buf, shapes)):
            dst[k] = a

    total = lax.psum(loss[0, 0], ("x", "y", "c"))
    return (total, dx.reshape(1, t, D), *[grads[k] for k in _NAMES], *[delta[k] for k in _NAMES],
            *[new_m[k] for k in _NAMES], *[new_v[k] for k in _NAMES])
```

```python
import functools

import numpy as np
import jax
import jax.numpy as jnp
from jax import lax
from jax.experimental import pallas as pl
from jax.experimental.pallas import tpu as pltpu

F32, BF16 = jnp.float32, jnp.bfloat16
S = jax.ShapeDtypeStruct

D = 1024
INW = 3072
GMW = 256
RETW = 512
CVW = 256
HEADS = 4
DH = 128
C = 128
KW = 31
HALO = 16
FFH = 2816
NDEV = 8
FFB = 2 * FFH // NDEV
EPS = 1e-6
LAYERS = 2
SCALE = DH ** -0.5
VMEM_LIMIT = 56 * 1024 * 1024

ADAM_LR, ADAM_B1, ADAM_B2, ADAM_EPS, ADAM_WD, ADAM_STEP = 0.001, 0.9, 0.999, 1e-08, 0.01, 10

_SQRT_HALF = 0.7071067811865476
_INV_SQRT_2PI = 0.3989422804014327


def _params(*sem):
    return pltpu.CompilerParams(dimension_semantics=sem or None, vmem_limit_bytes=VMEM_LIMIT)


def _resident(shape, index_map):
    return pl.BlockSpec(shape, index_map, pipeline_mode=pl.Buffered(1))


def _dot(a, b):
    return jnp.dot(a, b, preferred_element_type=F32)


def _dot_nt(a, b):
    return lax.dot_general(a, b, (((1,), (1,)), ((), ())), preferred_element_type=F32)


def _dot_tn(a, b):
    return lax.dot_general(a, b, (((0,), (0,)), ((), ())), preferred_element_type=F32)


def _sigmoid(x):
    return 1.0 / (1.0 + jnp.exp(-x))


def _gelu_and_grad(x):
    cdf = 0.5 * (1.0 + lax.erf(x * _SQRT_HALF))
    return x * cdf, cdf + x * jnp.exp(-0.5 * x * x) * _INV_SQRT_2PI


def _silu_and_grad(x):
    s = _sigmoid(x)
    return x * s, s * (1.0 + x * (1.0 - s))


def _standardize(x):
    mu = jnp.mean(x, axis=-1, keepdims=True)
    d = x - mu
    rstd = lax.rsqrt(jnp.mean(d * d, axis=-1, keepdims=True) + EPS)
    return d * rstd, rstd


def _standardize_bwd(dxhat, xhat, rstd):
    m1 = jnp.mean(dxhat, axis=-1, keepdims=True)
    m2 = jnp.mean(dxhat * xhat, axis=-1, keepdims=True)
    return rstd * (dxhat - m1 - xhat * m2)


def _rms(x):
    return lax.rsqrt(jnp.mean(x * x, axis=-1, keepdims=True) + EPS)


def _rmsnorm_bwd(dy, x, r, g):
    u = dy * g
    return r * u - x * (r * r * r) * jnp.mean(u * x, axis=-1, keepdims=True)


def _col_sum(a):
    return jnp.sum(a, axis=0, keepdims=True)


def _rot(t, cos2, sin2):
    return t * cos2 + pltpu.roll(t, DH // 2, axis=1) * sin2


def _rot_t(dt, cos2, sin2):
    return dt * cos2 + pltpu.roll(dt * sin2, DH // 2, axis=1)


MESH = pl.DeviceIdType.MESH
_HBM = pl.BlockSpec(memory_space=pltpu.HBM)
_ANY = pl.BlockSpec(memory_space=pl.ANY)


def _place():
    x, y, c = lax.axis_index("x"), lax.axis_index("y"), lax.axis_index("c")
    return x, y, c, ((1 - x, y), (x, 1 - y), (1 - x, 1 - y))


def _slot(full, kind, width, i):
    if kind == "cols":
        return full.at[:, pl.ds(pl.multiple_of(i * width, 128), width)]
    if kind == "rows":
        return full.at[pl.ds(pl.multiple_of(i * width, 16), width), :]
    return full.at[i]


class _Gather:
    def __init__(self, units):
        self.units = units
        self.inputs = [u[0] for u in units]
        self.out_shape = []
        for src, kind in units:
            r, c = src.shape[-2:]
            shape = {"cols": (r, NDEV * c), "rows": (NDEV * r, c), "lead": (NDEV,) + src.shape}[kind]
            self.out_shape.append(S(shape, src.dtype))
        n = len(units)
        self.scratch = [pltpu.SemaphoreType.DMA((n, 7)), pltpu.SemaphoreType.DMA((n, 7)), pltpu.SemaphoreType.DMA((n,))]

    def run(self, phase, ins, outs, scr):
        ssem, rsem, lsem = scr
        x, y, c, chips = _place()
        me, sib = 4 * x + 2 * y + c, (x, y, 1 - c)
        idx = lambda chip, core: 4 * chip[0] + 2 * chip[1] + core
        for u, (src_arr, kind) in enumerate(self.units):
            src, full = ins[u], outs[u]
            width = src_arr.shape[-1] if kind == "cols" else src_arr.shape[-2]
            slot = functools.partial(_slot, full, kind, width)

            def copy(k, block, to, from_src=False):
                return pltpu.make_async_remote_copy(src_ref=src if from_src else slot(block), dst_ref=slot(block),
                                                    send_sem=ssem.at[u, k], recv_sem=rsem.at[u, k],
                                                    device_id=to, device_id_type=MESH)

            mine = lambda: pltpu.make_async_copy(src, slot(me), lsem.at[u])
            first = lambda: [copy(0, me, sib, True)] + [copy(1 + j, me, (*chip, c), True) for j, chip in enumerate(chips)]
            passed = lambda j: copy(4 + j, idx(chips[j], c), sib)
            if phase == "start":
                mine().start()
                for cp in first():
                    cp.start()
            elif phase == "forward":
                for j, chip in enumerate(chips):
                    copy(1 + j, idx(chip, c), sib).wait_recv()
                    passed(j).start()
            else:
                copy(0, idx((x, y), 1 - c), sib).wait_recv()
                for j, chip in enumerate(chips):
                    copy(4 + j, idx(chip, 1 - c), sib).wait_recv()
                for cp in first() + [passed(j) for j in range(3)]:
                    cp.wait_send()
                mine().wait()


class _Scatter:
    def __init__(self, units):
        self.units = units
        self.inputs = list(units)
        self.out_shape = [S(u.shape, u.dtype) for u in units]
        n = len(units)
        self.scratch = [pltpu.SemaphoreType.DMA((n, 3)), pltpu.SemaphoreType.DMA((n, 3)), pltpu.SemaphoreType.DMA((n,))]

    def run(self, phase, ins, outs, scr):
        ssem, rsem, lsem = scr
        x, y, c, chips = _place()
        myq = 2 * x + y
        for u in range(len(self.units)):
            h, p = ins[u], outs[u]

            def copy(k, chip, send_to_them):
                q = 2 * chip[0] + chip[1]
                return pltpu.make_async_remote_copy(src_ref=h.at[q], dst_ref=p.at[myq if send_to_them else q],
                                                    send_sem=ssem.at[u, k], recv_sem=rsem.at[u, k],
                                                    device_id=(*chip, c), device_id_type=MESH)

            mine = lambda: pltpu.make_async_copy(h.at[myq], p.at[myq], lsem.at[u])
            sends = lambda: [copy(k, chip, True) for k, chip in enumerate(chips)]
            if phase == "start":
                mine().start()
                for cp in sends():
                    cp.start()
            elif phase == "finish":
                for k, chip in enumerate(chips):
                    copy(k, chip, False).wait_recv()
                for cp in sends():
                    cp.wait_send()
                mine().wait()


class _Comms:
    def __init__(self, parts):
        self.parts = parts
        self.inputs = [a for p in parts for a in p.inputs]
        self.out_shape = [a for p in parts for a in p.out_shape]
        self.scratch = [a for p in parts for a in p.scratch]

    def run(self, phase, ins, outs, scr):
        i = o = s = 0
        for p in self.parts:
            ni, no, ns = len(p.inputs), len(p.out_shape), len(p.scratch)
            p.run(phase, ins[i:i + ni], outs[o:o + no], scr[s:s + ns])
            i, o, s = i + ni, o + no, s + ns


def _launch(body, grid, in_specs, out_specs, out_shape, scratch, args, name, sem, comm=None):
    if comm is None:
        outs = pl.pallas_call(body, grid=grid, name=name, in_specs=in_specs, out_specs=out_specs, out_shape=out_shape,
                              scratch_shapes=scratch, compiler_params=_params(*sem))(*args)
        return list(outs), []
    n_in, n_out, n_scr = len(args), len(out_shape), len(scratch)
    ci, co = len(comm.inputs), len(comm.out_shape)
    nsteps = grid[0]
    fwd_step = (7 * nsteps) // 8

    def hosted(*refs):
        a = refs[:n_in]
        ca = refs[n_in:n_in + ci]
        o = refs[n_in + ci:n_in + ci + n_out]
        cout = refs[n_in + ci + n_out:n_in + ci + n_out + co]
        s = refs[n_in + ci + n_out + co:n_in + ci + n_out + co + n_scr]
        cs = refs[n_in + ci + n_out + co + n_scr:]
        step = pl.program_id(0)

        @pl.when(step == 0)
        def _():
            comm.run("start", ca, cout, cs)

        body(*a, *o, *s)

        @pl.when(step == fwd_step)
        def _():
            comm.run("forward", ca, cout, cs)

        @pl.when(step == nsteps - 1)
        def _():
            comm.run("finish", ca, cout, cs)

    outs = pl.pallas_call(
        hosted, grid=grid, name=name, in_specs=list(in_specs) + [_HBM] * ci, out_specs=list(out_specs) + [_HBM] * co,
        out_shape=list(out_shape) + comm.out_shape, scratch_shapes=list(scratch) + comm.scratch,
        compiler_params=_params(*["arbitrary"] * len(grid)))(*args, *comm.inputs)
    return list(outs[:n_out]), list(outs[n_out:])


def _comm_only(comm, name):
    ci, co = len(comm.inputs), len(comm.out_shape)

    def body(*refs):
        ca, cout, cs = refs[:ci], refs[ci:ci + co], refs[ci + co:]
        for phase in ("start", "forward", "finish"):
            comm.run(phase, ca, cout, cs)

    return pl.pallas_call(body, name=name, in_specs=[_HBM] * ci, out_specs=[_HBM] * co, out_shape=comm.out_shape,
                          scratch_shapes=comm.scratch, compiler_params=_params())(*comm.inputs)


def _ret_consts():
    idx = np.arange(C, dtype=np.float32)
    gf = (1.0 - np.exp2(-5.0 - np.arange(HEADS, dtype=np.float32))).astype(np.float32)
    out = {}
    for name, gamma, fwd in (("f", gf, True), ("b", gf[::-1].copy(), False)):
        lg = np.log(gamma).astype(np.float32)[:, None]
        diff = idx[:, None] - idx[None, :]
        if fwd:
            mask = diff >= 0
            dist = np.where(mask, diff, 0.0)
            zeta = np.exp(lg * (C - 1 - idx))
            xi = np.exp(lg * (idx + 1))
        else:
            mask = diff < 0
            dist = np.where(mask, -diff, 0.0)
            zeta = np.exp(lg * idx)
            xi = np.exp(lg * (C - idx))
        dm = np.where(mask[None], np.exp(lg[:, :, None] * dist[None]), 0.0).astype(np.float32)
        bc = lambda vec: np.ascontiguousarray(np.broadcast_to(vec.astype(np.float32)[:, :, None], (HEADS, C, DH)))
        out[name] = dict(D=dm, XI=bc(xi), ZETA=bc(zeta), gC=[float(v) for v in np.exp(lg[:, 0] * C).astype(np.float32)])
    return out


def _rope_tables(t):
    half = DH // 2
    inv_freq = (np.float32(10000.0) ** (-np.arange(half, dtype=np.float32) / np.float32(half))).astype(np.float32)
    ang = (np.arange(t, dtype=np.float32)[:, None] * inv_freq[None, :]).astype(np.float64)
    cos, sin = np.cos(ang).astype(np.float32), np.sin(ang).astype(np.float32)
    return np.concatenate([cos, cos], axis=1), np.concatenate([-sin, sin], axis=1)


def _f_inproj(x, g1, w, l, name, comm=None):
    t = x.shape[0]
    tm = 512

    def body(x_ref, g_ref, w_ref, proj_ref, ht_ref):
        xv = x_ref[...]
        h = (xv * _rms(xv) * g_ref[...]).astype(BF16)
        ht_ref[...] = h.T
        for nb in range(INW // 512):
            cs = slice(nb * 512, (nb + 1) * 512)
            proj_ref[:, cs] = _dot(h, w_ref[:, cs]).astype(BF16)

    return _launch(
        body, (t // tm,),
        [pl.BlockSpec((tm, D), lambda i: (i, 0)),
         pl.BlockSpec((None, 1, D), lambda i: (l, 0, 0)),
         _resident((D, INW), lambda i: (0, 0))],
        [pl.BlockSpec((tm, INW), lambda i: (i, 0)), pl.BlockSpec((D, tm), lambda i: (0, i))],
        [S((t, INW), BF16), S((D, t), BF16)], [], (x, g1, w), name, ("parallel",), comm)


def _gm_chunk_fwd(u, v, lng, lnb, ws_ref, bias):
    au, dau = _gelu_and_grad(u)
    av, dav = _gelu_and_grad(v)
    vhat, rstd = _standardize(av)
    vn = (vhat * lng + lnb).astype(BF16)
    head = lax.broadcasted_iota(jnp.int32, (C, GMW), 1) // (GMW // HEADS)
    mixed = bias
    for h in range(HEADS):
        mixed = mixed + jnp.where(head == h, _dot(ws_ref[h], vn), 0.0)
    return au, dau, dav, vhat, rstd, vn, mixed, head


def _f_gm(proj, lng, lnb, ws_bf, bias, l, name):
    t = proj.shape[0]
    tm = 512

    def body(p_ref, lng_ref, lnb_ref, ws_ref, bias_ref, y_ref):
        for ci in range(tm // C):
            rows = slice(ci * C, (ci + 1) * C)
            u = p_ref[rows, 0:GMW].astype(F32)
            v = p_ref[rows, GMW:2 * GMW].astype(F32)
            au, _, _, _, _, _, mixed, _ = _gm_chunk_fwd(u, v, lng_ref[...], lnb_ref[...], ws_ref, bias_ref[...])
            y_ref[rows, :] = (au * mixed).astype(BF16)

    return pl.pallas_call(
        body, grid=(t // tm,), name=name,
        in_specs=[pl.BlockSpec((tm, 2 * GMW), lambda i: (i, 0)),
                  pl.BlockSpec((None, 1, GMW), lambda i: (l, 0, 0)),
                  pl.BlockSpec((None, 1, GMW), lambda i: (l, 0, 0)),
                  pl.BlockSpec((None, HEADS, C, C), lambda i: (l, 0, 0, 0)),
                  pl.BlockSpec((None, C, GMW), lambda i: (l, 0, 0))],
        out_specs=pl.BlockSpec((tm, GMW), lambda i: (i, 0)),
        out_shape=S((t, GMW), BF16),
        compiler_params=_params("parallel"),
    )(proj, lng, lnb, ws_bf, bias)


def _scan_pair(proj, cos2, sin2, other, col, wf, wb, gcf, gcb, first_is_f, name):
    t = proj.shape[0]
    n = t // C
    sc = SCAN_CHUNKS
    nsteps = n // sc
    other_is_proj = other is None

    def body(a1, c1, s1, o1, a2, c2, s2, o2, w1_ref, w2_ref, out1, out2, st1, st2):
        @pl.when(pl.program_id(0) == 0)
        def _():
            st1[...] = jnp.zeros_like(st1)
            st2[...] = jnp.zeros_like(st2)

        def one(a_ref, cos_ref, sin_ref, o_ref, w_ref, gc, st, out, order):
            for h in range(HEADS):
                sl = slice(h * DH, (h + 1) * DH)
                incs = {}
                for j in order:
                    rows = slice(j * C, (j + 1) * C)
                    ar = _rot(a_ref[rows, sl].astype(F32), cos_ref[rows, :], sin_ref[rows, :])
                    incs[j] = _dot_tn((ar * w_ref[h]).astype(BF16), o_ref[rows, sl].astype(BF16))
                cur = st[h]
                for j in order:
                    out[j, h] = cur.astype(BF16)
                    cur = gc[h] * cur + incs[j]
                st[h] = cur

        g1, g2 = (gcf, gcb) if first_is_f else (gcb, gcf)
        one(a1, c1, s1, o1, w1_ref, g1, st1, out1, range(sc))
        one(a2, c2, s2, o2, w2_ref, g2, st2, out2, range(sc - 1, -1, -1))

    up = lambda i: i
    down = lambda i: nsteps - 1 - i

    def specs(ix):
        o_spec = pl.BlockSpec((sc * C, RETW), lambda i: (ix(i), 3 if other_is_proj else 0))
        return [pl.BlockSpec((sc * C, RETW), lambda i: (ix(i), col)),
                pl.BlockSpec((sc * C, DH), lambda i: (ix(i), 0)), pl.BlockSpec((sc * C, DH), lambda i: (ix(i), 0)), o_spec]

    const = lambda: pl.BlockSpec((HEADS, C, DH), lambda i: (0, 0, 0))
    oth = proj if other_is_proj else other
    w1, w2 = (wf, wb) if first_is_f else (wb, wf)
    out1, out2 = pl.pallas_call(
        body, grid=(nsteps,), name=name,
        in_specs=specs(up) + specs(down) + [const(), const()],
        out_specs=[pl.BlockSpec((sc, HEADS, DH, DH), lambda i: (up(i), 0, 0, 0)),
                   pl.BlockSpec((sc, HEADS, DH, DH), lambda i: (down(i), 0, 0, 0))],
        out_shape=[S((n, HEADS, DH, DH), BF16), S((n, HEADS, DH, DH), BF16)],
        scratch_shapes=[pltpu.VMEM((HEADS, DH, DH), F32), pltpu.VMEM((HEADS, DH, DH), F32)],
        compiler_params=_params("arbitrary"),
    )(proj, cos2, sin2, oth, proj, cos2, sin2, oth, w1, w2)
    return (out1, out2) if first_is_f else (out2, out1)


SCAN_CHUNKS = 4


RET_CHUNKS = 2


def _ret_heads(q_ref, k_ref, v_ref, cos_ref, sin_ref, rows, h):
    sl = slice(h * DH, (h + 1) * DH)
    cos_v, sin_v = cos_ref[rows, :], sin_ref[rows, :]
    qr = _rot(q_ref[rows, sl].astype(F32), cos_v, sin_v)
    kr = _rot(k_ref[rows, sl].astype(F32), cos_v, sin_v) * SCALE
    return sl, qr, kr, v_ref[rows, sl], cos_v, sin_v


def _f_ret_out(proj, cos2, sin2, s_f, s_b, rc, name):
    t = proj.shape[0]
    tm = RET_CHUNKS * C

    def body(q_ref, k_ref, v_ref, cos_ref, sin_ref, sf_ref, sb_ref, d_ref, xif_ref, xib_ref, o_ref):
        for ci in range(RET_CHUNKS):
            rows = slice(ci * C, (ci + 1) * C)
            for h in range(HEADS):
                sl, qr, kr, vh, _, _ = _ret_heads(q_ref, k_ref, v_ref, cos_ref, sin_ref, rows, h)
                p = (_dot_nt(qr.astype(BF16), kr.astype(BF16)) * d_ref[h]).astype(BF16)
                qx = jnp.concatenate([(qr * xif_ref[h]).astype(BF16), (qr * xib_ref[h]).astype(BF16)], axis=1)
                st = jnp.concatenate([sf_ref[ci, h], sb_ref[ci, h]], axis=0)
                o_ref[rows, sl] = _dot(p, vh) + _dot(qx, st)

    const = lambda: pl.BlockSpec((HEADS, C, DH), lambda i: (0, 0, 0))
    state = lambda: pl.BlockSpec((RET_CHUNKS, HEADS, DH, DH), lambda i: (i, 0, 0, 0))
    return pl.pallas_call(
        body, grid=(t // tm,), name=name,
        in_specs=[pl.BlockSpec((tm, RETW), lambda i, cb=cb: (i, cb)) for cb in (1, 2, 3)] +
                 [pl.BlockSpec((tm, DH), lambda i: (i, 0)), pl.BlockSpec((tm, DH), lambda i: (i, 0)),
                  state(), state(), const(), const(), const()],
        out_specs=pl.BlockSpec((tm, RETW), lambda i: (i, 0)),
        out_shape=S((t, RETW), F32),
        compiler_params=_params("parallel"),
    )(proj, proj, proj, cos2, sin2, s_f, s_b, rc["f"]["D"] + rc["b"]["D"], rc["f"]["XI"], rc["b"]["XI"])


def _conv_halo_specs(t, tm, width, col):
    r = tm // HALO
    last = t // HALO - 1
    return [pl.BlockSpec((HALO, width), lambda i: (jnp.maximum(i * r - 1, 0), col)),
            pl.BlockSpec((tm, width), lambda i: (i, col)),
            pl.BlockSpec((HALO, width), lambda i: (jnp.minimum((i + 1) * r, last), col))]


def _fill_ext(ext, prev, cur, nxt, i, nt, tm):
    ext[0:HALO, :] = jnp.where(i > 0, prev, 0.0)
    ext[HALO:HALO + tm, :] = cur
    ext[HALO + tm:2 * HALO + tm, :] = jnp.where(i < nt - 1, nxt, 0.0)


def _glu(a_ref, g_ref):
    return a_ref[...].astype(F32) * _sigmoid(g_ref[...].astype(F32))


def _shifted_copies(ext, rot, tm):
    rows = tm + 2 * HALO - 8
    for b in range(1, 8):
        rot[b - 1, :, :] = ext[pl.ds(b, rows), :]


def _tap(ext, rot, r0, s, rb):
    a, b = divmod(s, 8)
    return ext[pl.ds(r0 + 8 * a, rb), :] if b == 0 else rot[b - 1, pl.ds(r0 + 8 * a, rb), :]


def _f_conv(proj, cw, cb, lng, lnb, l, name, comm=None):
    t = proj.shape[0]
    tm = 256
    nt = t // tm
    rb = 64

    def body(ap, ac, an, gp, gc, gn, cw_ref, cb_ref, lng_ref, lnb_ref, c_ref, y_ref, hext, hrot):
        i = pl.program_id(0)
        _fill_ext(hext, _glu(ap, gp), _glu(ac, gc), _glu(an, gn), i, nt, tm)
        _shifted_copies(hext, hrot, tm)
        for r0 in range(0, tm, rb):
            acc = jnp.zeros((rb, CVW), F32) + cb_ref[...]
            for j in range(KW):
                acc = acc + cw_ref[j:j + 1, :] * _tap(hext, hrot, r0, j + 1, rb)
            c_ref[r0:r0 + rb, :] = acc
            chat, _ = _standardize(acc)
            z = chat * lng_ref[...] + lnb_ref[...]
            y_ref[r0:r0 + rb, :] = (z * _sigmoid(z)).astype(BF16)

    vec = lambda: pl.BlockSpec((None, 1, CVW), lambda i: (l, 0, 0))
    return _launch(
        body, (nt,),
        _conv_halo_specs(t, tm, CVW, 10) + _conv_halo_specs(t, tm, CVW, 11) +
        [pl.BlockSpec((None, 32, CVW), lambda i: (l, 0, 0)), vec(), vec(), vec()],
        [pl.BlockSpec((tm, CVW), lambda i: (i, 0)), pl.BlockSpec((tm, CVW), lambda i: (i, 0))],
        [S((t, CVW), F32), S((t, CVW), BF16)],
        [pltpu.VMEM((tm + 2 * HALO, CVW), F32), pltpu.VMEM((7, tm + 2 * HALO - 8, CVW), F32)],
        (proj, proj, proj, proj, proj, proj, cw, cb, lng, lnb), name, ("parallel",), comm)


def _f_mixout(x, y_gm, y_cv, o, proj, w, name):
    t = x.shape[0]
    tm = 512

    def body(x_ref, ygm_ref, ycv_ref, o_ref, g_ref, w_ref, xm_ref, ycat_t_ref, ycat):
        ycat[:, 0:GMW] = ygm_ref[...]
        ycat[:, GMW + RETW:D] = ycv_ref[...]
        for h in range(HEADS):
            sl = slice(h * DH, (h + 1) * DH)
            ohat, _ = _standardize(o_ref[:, sl])
            g = g_ref[:, sl].astype(F32)
            ycat[:, GMW + h * DH:GMW + (h + 1) * DH] = (ohat * (g * _sigmoid(g))).astype(BF16)
        yc = ycat[...]
        ycat_t_ref[...] = yc.T
        xm_ref[...] = x_ref[...] + _dot(yc, w_ref[...])

    return pl.pallas_call(
        body, grid=(t // tm,), name=name,
        in_specs=[pl.BlockSpec((tm, D), lambda i: (i, 0)),
                  pl.BlockSpec((tm, GMW), lambda i: (i, 0)),
                  pl.BlockSpec((tm, CVW), lambda i: (i, 0)),
                  pl.BlockSpec((tm, RETW), lambda i: (i, 0)),
                  pl.BlockSpec((tm, RETW), lambda i: (i, 4)),
                  _resident((D, D), lambda i: (0, 0))],
        out_specs=[pl.BlockSpec((tm, D), lambda i: (i, 0)), pl.BlockSpec((D, tm), lambda i: (0, i))],
        out_shape=[S((t, D), F32), S((D, t), BF16)],
        scratch_shapes=[pltpu.VMEM((tm, D), BF16)],
        compiler_params=_params("parallel"),
    )(x, y_gm, y_cv, o, proj, w)


def _f_ffn(xm, g2, w1, w2, l, name, comm=None):
    t = xm.shape[0]
    tm = 256
    half = NDEV // 2

    def body(x_ref, g_ref, w1_ref, w2_ref, xo_ref, ht_ref, gu_ref, act_t_ref):
        xv = x_ref[...]
        h = (xv * _rms(xv) * g_ref[...]).astype(BF16)
        ht_ref[...] = h.T
        acc = xv
        for j in range(half):
            gate = _dot(h, w1_ref[j])
            up = _dot(h, w1_ref[half + j])
            gu_ref[j] = gate.astype(BF16)
            gu_ref[half + j] = up.astype(BF16)
            a = ((gate * _sigmoid(gate)) * up).astype(BF16)
            act_t_ref[j] = a.T
            acc = acc + _dot(a, w2_ref[j * FFB:(j + 1) * FFB, :])
        xo_ref[...] = acc

    return _launch(
        body, (t // tm,),
        [pl.BlockSpec((tm, D), lambda i: (i, 0)),
         pl.BlockSpec((None, 1, D), lambda i: (l, 0, 0)),
         _resident((NDEV, D, FFB), lambda i: (0, 0, 0)),
         _resident((FFH, D), lambda i: (0, 0))],
        [pl.BlockSpec((tm, D), lambda i: (i, 0)), pl.BlockSpec((D, tm), lambda i: (0, i)),
         pl.BlockSpec((NDEV, tm, FFB), lambda i: (0, i, 0)), pl.BlockSpec((half, FFB, tm), lambda i: (0, 0, i))],
        [S((t, D), F32), S((D, t), BF16), S((NDEV, t, FFB), BF16), S((half, FFB, t), BF16)],
        [], (xm, g2, w1, w2), name, ("parallel",), comm)


def _b_loss(x, fg, tgt, name):
    t = x.shape[0]
    tm = 512

    def body(x_ref, g_ref, t_ref, loss_ref, dx_ref, dg_ref):
        @pl.when(pl.program_id(0) == 0)
        def _():
            loss_ref[...] = jnp.zeros_like(loss_ref)
            dg_ref[...] = jnp.zeros_like(dg_ref)

        xv = x_ref[...]
        r = _rms(xv)
        xr = xv * r
        err = xr * g_ref[...] - t_ref[...]
        loss_ref[...] += (0.5 / D) * _col_sum(jnp.sum(err * err, axis=1, keepdims=True))
        dy = err * (1.0 / D)
        dg_ref[...] += _col_sum(dy * xr)
        dx_ref[...] = _rmsnorm_bwd(dy, xv, r, g_ref[...])

    return pl.pallas_call(
        body, grid=(t // tm,), name=name,
        in_specs=[pl.BlockSpec((tm, D), lambda i: (i, 0)), pl.BlockSpec((1, D), lambda i: (0, 0)),
                  pl.BlockSpec((tm, D), lambda i: (i, 0))],
        out_specs=[pl.BlockSpec((1, 1), lambda i: (0, 0)), pl.BlockSpec((tm, D), lambda i: (i, 0)),
                   pl.BlockSpec((1, D), lambda i: (0, 0))],
        out_shape=[S((1, 1), F32), S((t, D), F32), S((1, D), F32)],
        compiler_params=_params("arbitrary"),
    )(x, fg, tgt)


def _b_ffn(dxo, xm, g2, gu, w1, w2, l, name, comm=None):
    t = xm.shape[0]
    tm = 256
    half = NDEV // 2

    def body(dxo_ref, x_ref, g_ref, gu_ref, w1_ref, w2_ref, dgu_ref, dxm_ref, dxb_ref, dg_ref):
        @pl.when(pl.program_id(0) == 0)
        def _():
            dg_ref[...] = jnp.zeros_like(dg_ref)

        dxo = dxo_ref[...]
        dxb = dxo.astype(BF16)
        dxb_ref[...] = dxb
        dh = jnp.zeros((tm, D), F32)
        for j in range(half):
            dact = _dot_nt(dxb, w2_ref[j * FFB:(j + 1) * FFB, :])
            gate = gu_ref[j].astype(F32)
            up = gu_ref[half + j].astype(F32)
            sg, dsg = _silu_and_grad(gate)
            dgate = (dact * up * dsg).astype(BF16)
            dup = (dact * sg).astype(BF16)
            dgu_ref[j] = dgate
            dgu_ref[half + j] = dup
            dh = dh + _dot_nt(dgate, w1_ref[j]) + _dot_nt(dup, w1_ref[half + j])
        xv = x_ref[...]
        r = _rms(xv)
        dg_ref[...] += _col_sum(dh * xv * r)
        dxm_ref[...] = dxo + _rmsnorm_bwd(dh, xv, r, g_ref[...])

    return _launch(
        body, (t // tm,),
        [pl.BlockSpec((tm, D), lambda i: (i, 0)), pl.BlockSpec((tm, D), lambda i: (i, 0)),
         pl.BlockSpec((None, 1, D), lambda i: (l, 0, 0)),
         pl.BlockSpec((NDEV, tm, FFB), lambda i: (0, i, 0)),
         _resident((NDEV, D, FFB), lambda i: (0, 0, 0)),
         _resident((FFH, D), lambda i: (0, 0))],
        [pl.BlockSpec((NDEV, tm, FFB), lambda i: (0, i, 0)), pl.BlockSpec((tm, D), lambda i: (i, 0)),
         pl.BlockSpec((tm, D), lambda i: (i, 0)), pl.BlockSpec((1, D), lambda i: (0, 0))],
        [S((NDEV, t, FFB), BF16), S((t, D), F32), S((t, D), BF16), S((1, D), F32)],
        [], (dxo, xm, g2, gu, w1, w2), name, ("arbitrary",), comm)


def _mm_wgrad(at, b, pieces, at_lead, b_mode, group, name):
    bt = 1024
    t = at.shape[-1]
    bt = min(bt, t)
    nt = t // bt
    ka = at.shape[-2]
    if at_lead:
        a_spec = pl.BlockSpec((None, ka, bt), lambda j, tt: (j, 0, tt))
    else:
        a_spec = pl.BlockSpec((ka, bt), lambda j, tt: (0, tt))
    if b_mode == "shared":
        nb, b_spec = b.shape[1], pl.BlockSpec((bt, b.shape[1]), lambda j, tt: (tt, 0))
    elif b_mode == "cols":
        nb = b.shape[1] // pieces
        b_spec = pl.BlockSpec((bt, group * nb), lambda j, tt: (tt, j))
    else:
        nb, b_spec = b.shape[2], pl.BlockSpec((None, bt, b.shape[2]), lambda j, tt: (j, tt, 0))
    assert group == 1 or b_mode == "cols"

    def body(a_ref, b_ref, o_ref, acc):
        tt = pl.program_id(1)

        @pl.when(tt == 0)
        def _():
            acc[...] = jnp.zeros_like(acc)

        acc[...] += _dot(a_ref[...], b_ref[...])

        @pl.when(tt == nt - 1)
        def _():
            for k in range(group):
                o_ref[k] = acc[:, k * nb:(k + 1) * nb].astype(BF16)

    return pl.pallas_call(
        body, grid=(pieces // group, nt), name=name,
        in_specs=[a_spec, b_spec],
        out_specs=pl.BlockSpec((group, ka, nb), lambda j, tt: (j, 0, 0)),
        out_shape=S((pieces, ka, nb), BF16),
        scratch_shapes=[pltpu.VMEM((ka, group * nb), F32)],
        compiler_params=_params("parallel", "arbitrary"),
    )(at, b)


def _b_mixout(dxm, w, o, proj, c, lng, lnb, l, name):
    t = dxm.shape[0]
    tm = 256

    def body(dxm_ref, w_ref, o_ref, g_ref, c_ref, lng_ref, lnb_ref,
             dxb_ref, dygm_ref, dO_ref, dg_ref, dc_ref, dlg_ref, dlb_ref, dcb_ref):
        @pl.when(pl.program_id(0) == 0)
        def _():
            dlg_ref[...] = jnp.zeros_like(dlg_ref)
            dlb_ref[...] = jnp.zeros_like(dlb_ref)
            dcb_ref[...] = jnp.zeros_like(dcb_ref)

        dxb = dxm_ref[...].astype(BF16)
        dxb_ref[...] = dxb
        dy = _dot_nt(dxb, w_ref[...])
        dygm_ref[...] = dy[:, 0:GMW]
        for h in range(HEADS):
            sl = slice(h * DH, (h + 1) * DH)
            ohat, rstd = _standardize(o_ref[:, sl])
            sg, dsg = _silu_and_grad(g_ref[:, sl].astype(F32))
            dyr = dy[:, GMW + h * DH:GMW + (h + 1) * DH]
            dg_ref[:, sl] = (dyr * ohat * dsg).astype(BF16)
            dO_ref[:, sl] = _standardize_bwd(dyr * sg, ohat, rstd)
        chat, rstd = _standardize(c_ref[...])
        z = chat * lng_ref[...] + lnb_ref[...]
        _, dsz = _silu_and_grad(z)
        dz = dy[:, GMW + RETW:D] * dsz
        dlg_ref[...] += _col_sum(dz * chat)
        dlb_ref[...] += _col_sum(dz)
        dc = _standardize_bwd(dz * lng_ref[...], chat, rstd)
        dcb_ref[...] += _col_sum(dc)
        dc_ref[...] = dc

    vec = lambda: pl.BlockSpec((None, 1, CVW), lambda i: (l, 0, 0))
    acc = lambda: pl.BlockSpec((1, CVW), lambda i: (0, 0))
    return pl.pallas_call(
        body, grid=(t // tm,), name=name,
        in_specs=[pl.BlockSpec((tm, D), lambda i: (i, 0)),
                  _resident((D, D), lambda i: (0, 0)),
                  pl.BlockSpec((tm, RETW), lambda i: (i, 0)),
                  pl.BlockSpec((tm, RETW), lambda i: (i, 4)),
                  pl.BlockSpec((tm, CVW), lambda i: (i, 0)), vec(), vec()],
        out_specs=[pl.BlockSpec((tm, D), lambda i: (i, 0)), pl.BlockSpec((tm, GMW), lambda i: (i, 0)),
                   pl.BlockSpec((tm, RETW), lambda i: (i, 0)), pl.BlockSpec((tm, RETW), lambda i: (i, 0)),
                   pl.BlockSpec((tm, CVW), lambda i: (i, 0)), acc(), acc(), acc()],
        out_shape=[S((t, D), BF16), S((t, GMW), F32), S((t, RETW), F32), S((t, RETW), BF16), S((t, CVW), F32),
                   S((1, CVW), F32), S((1, CVW), F32), S((1, CVW), F32)],
        compiler_params=_params("arbitrary"),
    )(dxm, w, o, proj, c, lng, lnb)


def _b_gm(proj, dy, lng, lnb, ws_bf, wst_bf, bias, l, name):
    t = proj.shape[0]
    tm = 512
    nt = t // tm

    def body(p_ref, dy_ref, lng_ref, lnb_ref, ws_ref, wst_ref, bias_ref,
             duv_ref, dws_ref, dbias_ref, dbs_ref, dlg_ref, dlb_ref):
        @pl.when(pl.program_id(0) == 0)
        def _():
            dws_ref[...] = jnp.zeros_like(dws_ref)
            dbias_ref[...] = jnp.zeros_like(dbias_ref)
            dbs_ref[...] = jnp.zeros_like(dbs_ref)
            dlg_ref[...] = jnp.zeros_like(dlg_ref)
            dlb_ref[...] = jnp.zeros_like(dlb_ref)

        for ci in range(tm // C):
            rows = slice(ci * C, (ci + 1) * C)
            u = p_ref[rows, 0:GMW].astype(F32)
            v = p_ref[rows, GMW:2 * GMW].astype(F32)
            au, dau, dav, vhat, rstd, vn, mixed, head = _gm_chunk_fwd(u, v, lng_ref[...], lnb_ref[...], ws_ref, bias_ref[...])
            dyc = dy_ref[rows, :]
            dmixed = dyc * au
            dmb = dmixed.astype(BF16)
            dbias_ref[...] += dmixed
            dvn = jnp.zeros((C, GMW), F32)
            for h in range(HEADS):
                dws_ref[h] += _dot_nt(jnp.where(head == h, dmixed, 0.0).astype(BF16), vn)
                dvn = dvn + jnp.where(head == h, _dot(wst_ref[h], dmb), 0.0)
            dlg_ref[...] += _col_sum(dvn * vhat)
            dlb_ref[...] += _col_sum(dvn)
            dav_in = _standardize_bwd(dvn * lng_ref[...], vhat, rstd)
            duv_ref[rows, 0:GMW] = (dyc * mixed * dau).astype(BF16)
            duv_ref[rows, GMW:2 * GMW] = (dav_in * dav).astype(BF16)

        @pl.when(pl.program_id(0) == nt - 1)
        def _():
            head = lax.broadcasted_iota(jnp.int32, (C, GMW), 1) // (GMW // HEADS)
            lane = lax.broadcasted_iota(jnp.int32, (C, 128), 1)
            fold = jnp.zeros((C, 128), F32)
            for h in range(HEADS):
                col = jnp.sum(jnp.where(head == h, dbias_ref[...], 0.0), axis=1, keepdims=True)
                fold = jnp.where(lane == h, col, fold)
            dbs_ref[...] = fold

    vec = lambda: pl.BlockSpec((None, 1, GMW), lambda i: (l, 0, 0))
    mats = lambda: pl.BlockSpec((None, HEADS, C, C), lambda i: (l, 0, 0, 0))
    return pl.pallas_call(
        body, grid=(nt,), name=name,
        in_specs=[pl.BlockSpec((tm, 2 * GMW), lambda i: (i, 0)), pl.BlockSpec((tm, GMW), lambda i: (i, 0)),
                  vec(), vec(), mats(), mats(), pl.BlockSpec((None, C, GMW), lambda i: (l, 0, 0))],
        out_specs=[pl.BlockSpec((tm, 2 * GMW), lambda i: (i, 0)),
                   pl.BlockSpec((HEADS, C, C), lambda i: (0, 0, 0)),
                   pl.BlockSpec((C, GMW), lambda i: (0, 0)), pl.BlockSpec((C, 128), lambda i: (0, 0)),
                   pl.BlockSpec((1, GMW), lambda i: (0, 0)), pl.BlockSpec((1, GMW), lambda i: (0, 0))],
        out_shape=[S((t, 2 * GMW), BF16), S((HEADS, C, C), F32), S((C, GMW), F32), S((C, 128), F32),
                   S((1, GMW), F32), S((1, GMW), F32)],
        compiler_params=_params("arbitrary"),
    )(proj, dy, lng, lnb, ws_bf, wst_bf, bias)


def _b_conv(proj, dc, cw, l, name, comm=None):
    t = proj.shape[0]
    tm = 256
    nt = t // tm
    rb = 64

    def body(ap, ac, an, gp, gc, gn, dp, dcur, dn, cw_ref, dag_ref, dcw_ref, hext, dext, hrot, drot):
        i = pl.program_id(0)

        @pl.when(i == 0)
        def _():
            dcw_ref[...] = jnp.zeros_like(dcw_ref)

        _fill_ext(hext, _glu(ap, gp), _glu(ac, gc), _glu(an, gn), i, nt, tm)
        _fill_ext(dext, dp[...], dcur[...], dn[...], i, nt, tm)
        _shifted_copies(hext, hrot, tm)
        _shifted_copies(dext, drot, tm)
        for j in range(KW):
            dcw_ref[j:j + 1, :] += _col_sum(dcur[...] * _tap(hext, hrot, 0, j + 1, tm))
        for r0 in range(0, tm, rb):
            dh = jnp.zeros((rb, CVW), F32)
            for j in range(KW):
                dh = dh + cw_ref[j:j + 1, :] * _tap(dext, drot, r0, 2 * HALO - 1 - j, rb)
            a = ac[r0:r0 + rb, :].astype(F32)
            s = _sigmoid(gc[r0:r0 + rb, :].astype(F32))
            dag_ref[r0:r0 + rb, 0:CVW] = (dh * s).astype(BF16)
            dag_ref[r0:r0 + rb, CVW:2 * CVW] = (dh * a * s * (1.0 - s)).astype(BF16)

    dspecs = _conv_halo_specs(t, tm, CVW, 0)
    return _launch(
        body, (nt,),
        _conv_halo_specs(t, tm, CVW, 10) + _conv_halo_specs(t, tm, CVW, 11) + dspecs +
        [pl.BlockSpec((None, 32, CVW), lambda i: (l, 0, 0))],
        [pl.BlockSpec((tm, 2 * CVW), lambda i: (i, 0)), pl.BlockSpec((32, CVW), lambda i: (0, 0))],
        [S((t, 2 * CVW), BF16), S((32, CVW), F32)],
        [pltpu.VMEM((tm + 2 * HALO, CVW), F32), pltpu.VMEM((tm + 2 * HALO, CVW), F32),
         pltpu.VMEM((7, tm + 2 * HALO - 8, CVW), F32), pltpu.VMEM((7, tm + 2 * HALO - 8, CVW), F32)],
        (proj, proj, proj, proj, proj, proj, dc, dc, dc, cw), name, ("arbitrary",), comm)


def _b_ret_out(proj, cos2, sin2, dO, s_f, s_b, g_f, g_b, rc, name, comm=None):
    t = proj.shape[0]
    tm = RET_CHUNKS * C

    def body(q_ref, k_ref, v_ref, cos_ref, sin_ref, dO_ref, sf_ref, sb_ref, gf_ref, gb_ref,
             d_ref, xif_ref, xib_ref, zef_ref, zeb_ref, dq_ref, dk_ref, dv_ref):
        for ci in range(RET_CHUNKS):
            rows = slice(ci * C, (ci + 1) * C)
            for h in range(HEADS):
                sl, qr, kr, vh, cos_v, sin_v = _ret_heads(q_ref, k_ref, v_ref, cos_ref, sin_ref, rows, h)
                qh, kh = qr.astype(BF16), kr.astype(BF16)
                dOh = dO_ref[rows, sl].astype(BF16)
                st = jnp.concatenate([sf_ref[ci, h], sb_ref[ci, h]], axis=0)
                gr = jnp.concatenate([gf_ref[ci, h], gb_ref[ci, h]], axis=0)
                dm = d_ref[h]
                p = (_dot_nt(qh, kh) * dm).astype(BF16)
                dp = (_dot_nt(dOh, vh) * dm).astype(BF16)
                from_s = _dot_nt(dOh, st)
                from_g = _dot_nt(vh, gr)
                dqr = _dot(dp, kh) + xif_ref[h] * from_s[:, 0:DH] + xib_ref[h] * from_s[:, DH:2 * DH]
                dkr = (_dot_tn(dp, qh) + zef_ref[h] * from_g[:, 0:DH] + zeb_ref[h] * from_g[:, DH:2 * DH]) * SCALE
                kz = jnp.concatenate([(kr * zef_ref[h]).astype(BF16), (kr * zeb_ref[h]).astype(BF16)], axis=1)
                dv = _dot_tn(p, dOh) + _dot(kz, gr)
                dq_ref[rows, sl] = _rot_t(dqr, cos_v, sin_v).astype(BF16)
                dk_ref[rows, sl] = _rot_t(dkr, cos_v, sin_v).astype(BF16)
                dv_ref[rows, sl] = dv.astype(BF16)

    const = lambda: pl.BlockSpec((HEADS, C, DH), lambda i: (0, 0, 0))
    state = lambda: pl.BlockSpec((RET_CHUNKS, HEADS, DH, DH), lambda i: (i, 0, 0, 0))
    tok = lambda: pl.BlockSpec((tm, RETW), lambda i: (i, 0))
    return _launch(
        body, (t // tm,),
        [pl.BlockSpec((tm, RETW), lambda i, cb=cb: (i, cb)) for cb in (1, 2, 3)] +
        [pl.BlockSpec((tm, DH), lambda i: (i, 0)), pl.BlockSpec((tm, DH), lambda i: (i, 0)), tok(),
         state(), state(), state(), state()] + [const() for _ in range(5)],
        [tok(), tok(), tok()],
        [S((t, RETW), BF16) for _ in range(3)], [],
        (proj, proj, proj, cos2, sin2, dO, s_f, s_b, g_f, g_b, rc["f"]["D"] + rc["b"]["D"],
         rc["f"]["XI"], rc["b"]["XI"], rc["f"]["ZETA"], rc["b"]["ZETA"]), name, ("parallel",), comm)


def _b_inproj(d_uv, dqkv, d_g, d_ag, w, x, g1, dxm, l, name, comm=None):
    t = x.shape[0]
    tm = 256

    def body(duv_ref, dq_ref, dk_ref, dv_ref, dg_ref, dag_ref, w_ref, x_ref, g_ref, dxm_ref,
             dp_ref, dx_ref, dn_ref):
        @pl.when(pl.program_id(0) == 0)
        def _():
            dn_ref[...] = jnp.zeros_like(dn_ref)

        for k, part in enumerate((duv_ref, dq_ref, dk_ref, dv_ref, dg_ref, dag_ref)):
            dp_ref[:, 512 * k:512 * (k + 1)] = part[...]
        dh = _dot_nt(dp_ref[...], w_ref[...])
        xv = x_ref[...]
        r = _rms(xv)
        dn_ref[...] += _col_sum(dh * xv * r)
        dx_ref[...] = dxm_ref[...] + _rmsnorm_bwd(dh, xv, r, g_ref[...])

    half = lambda: pl.BlockSpec((tm, 512), lambda i: (i, 0))
    full = lambda: pl.BlockSpec((tm, D), lambda i: (i, 0))
    return _launch(
        body, (t // tm,),
        [half() for _ in range(6)] +
        [_resident((D, INW), lambda i: (0, 0)), full(), pl.BlockSpec((None, 1, D), lambda i: (l, 0, 0)), full()],
        [pl.BlockSpec((tm, INW), lambda i: (i, 0)), full(), pl.BlockSpec((1, D), lambda i: (0, 0))],
        [S((t, INW), BF16), S((t, D), F32), S((1, D), F32)], [],
        (d_uv, *dqkv, d_g, d_ag, w, x, g1, dxm), name, ("arbitrary",), comm)


def _pair_exchange(gs, name):
    n = len(gs)

    def body(*refs):
        g, q, ssem, rsem = refs[:n], refs[n:2 * n], refs[2 * n], refs[2 * n + 1]
        x, y, c, _ = _place()
        copies = [pltpu.make_async_remote_copy(src_ref=g[u].at[2 * chip + (1 - c)], dst_ref=q[u].at[chip],
                                               send_sem=ssem.at[u, chip], recv_sem=rsem.at[u, chip],
                                               device_id=(x, y, 1 - c), device_id_type=MESH)
                  for u in range(n) for chip in range(4)]
        for cp in copies:
            cp.start()
        for cp in copies:
            cp.wait_recv()
        for cp in copies:
            cp.wait_send()

    return pl.pallas_call(body, name=name, in_specs=[_HBM] * n, out_specs=[_HBM] * n,
                          out_shape=[S((4,) + g.shape[1:], BF16) for g in gs],
                          scratch_shapes=[pltpu.SemaphoreType.DMA((n, 4)), pltpu.SemaphoreType.DMA((n, 4))],
                          compiler_params=_params())(*gs)


def _pair_add(g, q, name):
    _, mm, nn = g.shape
    bm = 256 if mm % 256 == 0 else mm

    def body(g_ref, q_ref, h_ref):
        h_ref[...] = (g_ref[lax.axis_index("c")].astype(F32) + q_ref[...].astype(F32)).astype(BF16)

    return pl.pallas_call(
        body, grid=(4, mm // bm), name=name,
        in_specs=[pl.BlockSpec((None, 2, bm, nn), lambda qq, i: (qq, 0, i, 0)),
                  pl.BlockSpec((None, bm, nn), lambda qq, i: (qq, i, 0))],
        out_specs=pl.BlockSpec((None, bm, nn), lambda qq, i: (qq, i, 0)),
        out_shape=S((4, mm, nn), BF16),
        compiler_params=_params("parallel", "parallel"),
    )(g.reshape(4, 2, mm, nn), q)


def _pair_reduce(named, l):
    names = [k for k, _ in named]
    qs = _pair_exchange([g for _, g in named], f"pair_exchange_{names[0]}_{l}")
    return [_pair_add(g, q, f"pair_add_{k}_{l}") for (k, g), q in zip(named, qs)]


_BIG = ("w_in", "w_out", "w_ffn_in", "w_ffn_out")
_KIND = dict(w_in="cols", w_out="rows", w_ffn_in="lead", w_ffn_out="rows")
CWP = 128
SMALL_ROWS = 152


def _step(x, tgt, wts, sh, cw_pad):
    t = x.shape[0]
    rc = _ret_consts()
    cos2, sin2 = (jnp.asarray(a) for a in _rope_tables(t))
    n1 = wts["norm1_g"].reshape(LAYERS, 1, D)
    n2 = wts["norm2_g"].reshape(LAYERS, 1, D)
    gm_lng = wts["gm_ln_g"].reshape(LAYERS, 1, GMW)
    gm_lnb = wts["gm_ln_b"].reshape(LAYERS, 1, GMW)
    ws_bf = wts["gm_ws"].astype(BF16)
    wst_bf = jnp.swapaxes(wts["gm_ws"], 2, 3).astype(BF16)
    bias = jnp.repeat(jnp.swapaxes(wts["gm_bs"], 1, 2), GMW // HEADS, axis=2)
    cb = wts["conv_b"].reshape(LAYERS, 1, CVW)
    cv_lng = wts["conv_ln_g"].reshape(LAYERS, 1, CVW)
    cv_lnb = wts["conv_ln_b"].reshape(LAYERS, 1, CVW)
    unit = lambda f, l: (sh[f][l], _KIND[f])
    cshard = CVW // NDEV

    full = {f: [None] * LAYERS for f in _BIG}
    full["w_in"][0], cw_all = _comm_only(_Gather([unit("w_in", 0), (cw_pad, "lead")]), "gather_first")
    cw = jnp.transpose(cw_all[:, :, :, :cshard], (1, 2, 0, 3)).reshape(LAYERS, 32, CVW)

    zeta_f, zeta_b = rc["f"]["ZETA"] * np.float32(SCALE), rc["b"]["ZETA"] * np.float32(SCALE)
    gcf, gcb = rc["f"]["gC"], rc["b"]["gC"]
    saved = []
    for l in range(LAYERS):
        first = l == 0
        (proj, h1), got = _f_inproj(x, n1, full["w_in"][l], l, f"f_inproj_{l}",
                                    _Gather([unit("w_ffn_in", 0)]) if first else None)
        if first:
            full["w_ffn_in"][0], = got
        y_gm = _f_gm(proj, gm_lng, gm_lnb, ws_bf, bias, l, f"f_gm_{l}")
        s_f, s_b = _scan_pair(proj, cos2, sin2, None, 2, zeta_f, zeta_b, gcf, gcb, True, f"f_ret_state_{l}")
        o = _f_ret_out(proj, cos2, sin2, s_f, s_b, rc, f"f_ret_out_{l}")
        (c, y_cv), got = _f_conv(proj, cw, cb, cv_lng, cv_lnb, l, f"f_conv_{l}",
                                 _Gather([unit("w_out", 0), unit("w_ffn_out", 0)]) if first else None)
        if first:
            full["w_out"][0], full["w_ffn_out"][0] = got
        xm, ycat = _f_mixout(x, y_gm, y_cv, o, proj, full["w_out"][l], f"f_mixout_{l}")
        (xo, h2, gu, act), got = _f_ffn(xm, n2, full["w_ffn_in"][l], full["w_ffn_out"][l], l, f"f_ffn_{l}",
                                        _Gather([unit(f, 1) for f in _BIG]) if first else None)
        if first:
            full["w_in"][1], full["w_out"][1], full["w_ffn_in"][1], full["w_ffn_out"][1] = got
        saved.append(dict(x=x, proj=proj, h1=h1, o=o, s_f=s_f, s_b=s_b, c=c, xm=xm, ycat=ycat, h2=h2, gu=gu, act=act))
        x = xo

    loss, dx, d_final = _b_loss(x, wts["final_g"].reshape(1, D), tgt, "b_loss")

    parts = {f: [None] * LAYERS for f in _BIG}
    small = [None] * LAYERS
    upper = None
    for l in reversed(range(LAYERS)):
        sv = saved[l]
        (dgu, dxm, dxo_bf, d_n2), got = _b_ffn(dx, sv["xm"], n2, sv["gu"], full["w_ffn_in"][l], full["w_ffn_out"][l], l,
                                               f"b_ffn_{l}", _Scatter(upper) if upper else None)
        if upper:
            for f, p in zip(_BIG, got):
                parts[f][l + 1] = p
        g_f2 = _mm_wgrad(sv["act"], dxo_bf, NDEV // 2, True, "shared", 1, f"g_ffn_out_{l}").reshape(NDEV, FFH // NDEV, D)
        g_f1 = _mm_wgrad(sv["h2"], dgu, NDEV, False, "lead", 1, f"g_ffn_in_{l}")
        dxm_bf, dy_gm, dO, d_g, dc, d_cvlg, d_cvlb, d_cb = _b_mixout(
            dxm, full["w_out"][l], sv["o"], sv["proj"], sv["c"], cv_lng, cv_lnb, l, f"b_mixout_{l}")
        g_out = _mm_wgrad(sv["ycat"], dxm_bf, 1, False, "shared", 1, f"g_out_{l}").reshape(NDEV, D // NDEV, D)
        last = l == 0
        early = _pair_reduce([("w_out", g_out), ("w_ffn_in", g_f1), ("w_ffn_out", g_f2)], l) if last else None
        d_uv, d_ws, _, d_bs_fold, d_gmlg, d_gmlb = _b_gm(sv["proj"], dy_gm, gm_lng, gm_lnb, ws_bf, wst_bf, bias, l, f"b_gm_{l}")
        (d_ag, d_cw), got = _b_conv(sv["proj"], dc, cw, l, f"b_conv_{l}", _Scatter(early[1:2]) if last else None)
        if last:
            parts["w_ffn_in"][l], = got
        g_f, g_b = _scan_pair(sv["proj"], cos2, sin2, dO, 1, rc["f"]["XI"], rc["b"]["XI"], gcf, gcb, False, f"b_ret_state_{l}")
        dqkv, got = _b_ret_out(sv["proj"], cos2, sin2, dO, sv["s_f"], sv["s_b"], g_f, g_b, rc, f"b_ret_out_{l}",
                               _Scatter([early[0], early[2]]) if last else None)
        if last:
            parts["w_out"][l], parts["w_ffn_out"][l] = got
        (dproj, dx, d_n1), _ = _b_inproj(d_uv, dqkv, d_g, d_ag, full["w_in"][l], sv["x"], n1, dxm, l, f"b_inproj_{l}")
        g_in = _mm_wgrad(sv["h1"], dproj, NDEV, False, "cols", 2, f"g_in_{l}")
        if last:
            tail = _pair_reduce([("w_in", g_in)], l)
        else:
            upper = _pair_reduce([("w_in", g_in), ("w_out", g_out), ("w_ffn_in", g_f1), ("w_ffn_out", g_f2)], l)
        small[l] = dict(norm1_g=d_n1[0], gm_ln_g=d_gmlg[0], gm_ln_b=d_gmlb[0], gm_ws=d_ws,
                        gm_bs=d_bs_fold[:, :HEADS].T, conv_w=d_cw[:KW], conv_b=d_cb[0], conv_ln_g=d_cvlg[0],
                        conv_ln_b=d_cvlb[0], norm2_g=d_n2[0])
    small_g = {k: jnp.stack([small[l][k] for l in range(LAYERS)]) for k in small[0]}
    small_g["final_g"] = d_final[0]
    small_buf = _pack([small_g[k] for k in _SMALL], SMALL_ROWS)
    parts["w_in"][0], small_parts = _comm_only(_Comms([_Scatter(tail), _Gather([(small_buf, "lead")])]), "exchange_last")
    return loss, dx, parts, small_parts


def _adamw(w, g, m, v):
    m = ADAM_B1 * m + (1.0 - ADAM_B1) * g
    v = ADAM_B2 * v + (1.0 - ADAM_B2) * (g * g)
    m_hat = m / (1.0 - ADAM_B1 ** ADAM_STEP)
    v_hat = v / (1.0 - ADAM_B2 ** ADAM_STEP)
    return -ADAM_LR * (m_hat / (jnp.sqrt(v_hat) + ADAM_EPS) + ADAM_WD * w), m, v


def _cast_blocks(ws):
    def body(*refs):
        ins, outs = refs[:len(ws)], refs[len(ws):]
        for k, src in enumerate(ins):
            for l in range(LAYERS):
                outs[k * LAYERS + l][...] = src[l].astype(BF16)

    outs = pl.pallas_call(body, name="cast_blocks", out_shape=[S(w.shape[1:], BF16) for w in ws for _ in range(LAYERS)],
                          compiler_params=_params())(*ws)
    return [list(outs[k * LAYERS:(k + 1) * LAYERS]) for k in range(len(ws))]


def _sum_adam(parts, w, m, v, l, prev, name):
    _, mm, nn = parts.shape
    bm = 256 if mm % 256 == 0 else mm

    def body(p_ref, w_ref, m_ref, v_ref, *rest):
        g_ref, d_ref, nm_ref, nv_ref = rest[-4:]
        g = p_ref[0].astype(F32)
        for s in range(1, 4):
            g = g + p_ref[s].astype(F32)
        g_ref[...] = g
        d_ref[...], nm_ref[...], nv_ref[...] = _adamw(w_ref[...], g, m_ref[...], v_ref[...])

    blk = lambda: pl.BlockSpec((None, bm, nn), lambda i: (l, i, 0))
    prev = list(prev) if prev else []
    return pl.pallas_call(
        body, grid=(mm // bm,), name=name,
        in_specs=[pl.BlockSpec((4, bm, nn), lambda i: (0, i, 0)), blk(), blk(), blk()] + [_ANY] * len(prev),
        out_specs=[blk() for _ in range(4)],
        out_shape=[S(w.shape, F32) for _ in range(4)],
        input_output_aliases={4 + j: j for j in range(len(prev))},
        compiler_params=_params("parallel"),
    )(parts, w, m, v, *prev)


def _sum_small(parts):
    def body(p_ref, o_ref):
        g = p_ref[0]
        for s in range(1, NDEV):
            g = g + p_ref[s]
        o_ref[...] = g

    return pl.pallas_call(body, name="sum_small", out_shape=S(parts.shape[1:], F32),
                          compiler_params=_params())(parts)


def _adam_small(g, w, m, v):
    def body(g_ref, w_ref, m_ref, v_ref, d_ref, nm_ref, nv_ref):
        d_ref[...], nm_ref[...], nv_ref[...] = _adamw(w_ref[...], g_ref[...], m_ref[...], v_ref[...])

    return pl.pallas_call(body, name="adam_small", out_shape=[S(g.shape, F32)] * 3, compiler_params=_params())(g, w, m, v)


_SMALL = ("norm1_g", "gm_ln_g", "gm_ln_b", "gm_ws", "gm_bs", "conv_w", "conv_b", "conv_ln_g", "conv_ln_b",
          "norm2_g", "final_g")
_NAMES = ("norm1_g", "w_in", "gm_ln_g", "gm_ln_b", "gm_ws", "gm_bs", "conv_w", "conv_b", "conv_ln_g", "conv_ln_b",
          "w_out", "norm2_g", "w_ffn_in", "w_ffn_out", "final_g")


def _pack(parts, rows):
    flat = jnp.concatenate([p.reshape(-1) for p in parts])
    return jnp.pad(flat, (0, rows * 1024 - flat.shape[0])).reshape(rows, 1024)


def _unpack(buf, shapes):
    flat = buf.reshape(-1)
    out, o = [], 0
    for shp in shapes:
        sz = int(np.prod(shp))
        out.append(flat[o:o + sz].reshape(shp))
        o += sz
    return out


def kernel(x, norm1_g, w_in, gm_ln_g, gm_ln_b, gm_ws, gm_bs, conv_w, conv_b, conv_ln_g, conv_ln_b, w_out, norm2_g, w_ffn_in, w_ffn_out, final_g, loss_target, m_norm1_g, m_w_in, m_gm_ln_g, m_gm_ln_b, m_gm_ws, m_gm_bs, m_conv_w, m_conv_b, m_conv_ln_g, m_conv_ln_b, m_w_out, m_norm2_g, m_w_ffn_in, m_w_ffn_out, m_final_g, v_norm1_g, v_w_in, v_gm_ln_g, v_gm_ln_b, v_gm_ws, v_gm_bs, v_conv_w, v_conv_b, v_conv_ln_g, v_conv_ln_b, v_w_out, v_norm2_g, v_w_ffn_in, v_w_ffn_out, v_final_g):
    w = dict(norm1_g=norm1_g, w_in=w_in, gm_ln_g=gm_ln_g, gm_ln_b=gm_ln_b, gm_ws=gm_ws, gm_bs=gm_bs, conv_w=conv_w,
             conv_b=conv_b, conv_ln_g=conv_ln_g, conv_ln_b=conv_ln_b, w_out=w_out, norm2_g=norm2_g, w_ffn_in=w_ffn_in,
             w_ffn_out=w_ffn_out, final_g=final_g)
    mo = dict(norm1_g=m_norm1_g, w_in=m_w_in, gm_ln_g=m_gm_ln_g, gm_ln_b=m_gm_ln_b, gm_ws=m_gm_ws, gm_bs=m_gm_bs,
              conv_w=m_conv_w, conv_b=m_conv_b, conv_ln_g=m_conv_ln_g, conv_ln_b=m_conv_ln_b, w_out=m_w_out,
              norm2_g=m_norm2_g, w_ffn_in=m_w_ffn_in, w_ffn_out=m_w_ffn_out, final_g=m_final_g)
    vo = dict(norm1_g=v_norm1_g, w_in=v_w_in, gm_ln_g=v_gm_ln_g, gm_ln_b=v_gm_ln_b, gm_ws=v_gm_ws, gm_bs=v_gm_bs,
              conv_w=v_conv_w, conv_b=v_conv_b, conv_ln_g=v_conv_ln_g, conv_ln_b=v_conv_ln_b, w_out=v_w_out,
              norm2_g=v_norm2_g, w_ffn_in=v_w_ffn_in, w_ffn_out=v_w_ffn_out, final_g=v_final_g)
    t = x.shape[1]
    me = 4 * lax.axis_index("x") + 2 * lax.axis_index("y") + lax.axis_index("c")
    cshard = conv_w.shape[2]

    cw_pad = jnp.pad(conv_w, ((0, 0), (0, 32 - KW), (0, CWP - cshard)))
    sh = dict(zip(_BIG, _cast_blocks([w[f] for f in _BIG])))
    loss, dx, parts, small_parts = _step(x.reshape(t, D), loss_target.reshape(t, D), w, sh, cw_pad)

    grads, delta, new_m, new_v = {}, {}, {}, {}
    for f in _BIG:
        outs = None
        for l in reversed(range(LAYERS)):
            outs = _sum_adam(parts[f][l], w[f], mo[f], vo[f], l, outs, f"sum_adam_{f}_{l}")
        grads[f], delta[f], new_m[f], new_v[f] = outs

    full_shapes = [(LAYERS, KW, CVW) if k == "conv_w" else w[k].shape for k in _SMALL]
    for k, g in zip(_SMALL, _unpack(_sum_small(small_parts), full_shapes)):
        grads[k] = lax.dynamic_slice_in_dim(g, me * cshard, cshard, axis=2) if k == "conv_w" else g
    adam_rows = 144
    d_s, m_s, v_s = _adam_small(_pack([grads[k] for k in _SMALL], adam_rows), _pack([w[k] for k in _SMALL], adam_rows),
                                _pack([mo[k] for k in _SMALL], adam_rows), _pack([vo[k] for k in _SMALL], adam_rows))
    shapes = [w[k].shape for k in _SMALL]
    for dst, buf in ((delta, d_s), (new_m, m_s), (new_v, v_s)):
        for k, a in zip(_SMALL, _unpack(buf, shapes)):
            dst[k] = a

    total = lax.psum(loss[0, 0], ("x", "y", "c"))
    return (total, dx.reshape(1, t, D), *[grads[k] for k in _NAMES], *[delta[k] for k in _NAMES],
            *[new_m[k] for k in _NAMES], *[new_v[k] for k in _NAMES])
```

```python
import functools

import numpy as np
import jax
import jax.numpy as jnp
from jax import lax
from jax.experimental import pallas as pl
from jax.experimental.pallas import tpu as pltpu

F32, BF16 = jnp.float32, jnp.bfloat16
S = jax.ShapeDtypeStruct

D = 1024
INW = 3072
GMW = 256
RETW = 512
CVW = 256
HEADS = 4
DH = 128
C = 128
KW = 31
HALO = 16
FFH = 2816
NDEV = 8
FFB = 2 * FFH // NDEV
EPS = 1e-6
LAYERS = 2
SCALE = DH ** -0.5
VMEM_LIMIT = 56 * 1024 * 1024

ADAM_LR, ADAM_B1, ADAM_B2, ADAM_EPS, ADAM_WD, ADAM_STEP = 0.001, 0.9, 0.999, 1e-08, 0.01, 10

_SQRT_HALF = 0.7071067811865476
_INV_SQRT_2PI = 0.3989422804014327


def _params(*sem):
    return pltpu.CompilerParams(dimension_semantics=sem or None, vmem_limit_bytes=VMEM_LIMIT)


def _resident(shape, index_map):
    return pl.BlockSpec(shape, index_map, pipeline_mode=pl.Buffered(1))


def _dot(a, b):
    return jnp.dot(a, b, preferred_element_type=F32)


def _dot_nt(a, b):
    return lax.dot_general(a, b, (((1,), (1,)), ((), ())), preferred_element_type=F32)


def _dot_tn(a, b):
    return lax.dot_general(a, b, (((0,), (0,)), ((), ())), preferred_element_type=F32)


def _sigmoid(x):
    return 1.0 / (1.0 + jnp.exp(-x))


def _gelu_and_grad(x):
    cdf = 0.5 * (1.0 + lax.erf(x * _SQRT_HALF))
    return x * cdf, cdf + x * jnp.exp(-0.5 * x * x) * _INV_SQRT_2PI


def _silu_and_grad(x):
    s = _sigmoid(x)
    return x * s, s * (1.0 + x * (1.0 - s))


def _standardize(x):
    mu = jnp.mean(x, axis=-1, keepdims=True)
    d = x - mu
    rstd = lax.rsqrt(jnp.mean(d * d, axis=-1, keepdims=True) + EPS)
    return d * rstd, rstd


def _standardize_bwd(dxhat, xhat, rstd):
    m1 = jnp.mean(dxhat, axis=-1, keepdims=True)
    m2 = jnp.mean(dxhat * xhat, axis=-1, keepdims=True)
    return rstd * (dxhat - m1 - xhat * m2)


def _rms(x):
    return lax.rsqrt(jnp.mean(x * x, axis=-1, keepdims=True) + EPS)


def _rmsnorm_bwd(dy, x, r, g):
    u = dy * g
    return r * u - x * (r * r * r) * jnp.mean(u * x, axis=-1, keepdims=True)


def _col_sum(a):
    return jnp.sum(a, axis=0, keepdims=True)


def _rot(t, cos2, sin2):
    return t * cos2 + pltpu.roll(t, DH // 2, axis=1) * sin2


def _rot_t(dt, cos2, sin2):
    return dt * cos2 + pltpu.roll(dt * sin2, DH // 2, axis=1)


MESH = pl.DeviceIdType.MESH
_HBM = pl.BlockSpec(memory_space=pltpu.HBM)
_ANY = pl.BlockSpec(memory_space=pl.ANY)


def _place():
    x, y, c = lax.axis_index("x"), lax.axis_index("y"), lax.axis_index("c")
    return x, y, c, ((1 - x, y), (x, 1 - y), (1 - x, 1 - y))


def _slot(full, kind, width, i):
    if kind == "cols":
        return full.at[:, pl.ds(pl.multiple_of(i * width, 128), width)]
    if kind == "rows":
        return full.at[pl.ds(pl.multiple_of(i * width, 16), width), :]
    return full.at[i]


class _Gather:
    def __init__(self, units):
        self.units = units
        self.inputs = [u[0] for u in units]
        self.out_shape = []
        for src, kind in units:
            r, c = src.shape[-2:]
            shape = {"cols": (r, NDEV * c), "rows": (NDEV * r, c), "lead": (NDEV,) + src.shape}[kind]
            self.out_shape.append(S(shape, src.dtype))
        n = len(units)
        self.scratch = [pltpu.SemaphoreType.DMA((n, 7)), pltpu.SemaphoreType.DMA((n, 7)), pltpu.SemaphoreType.DMA((n,))]

    def run(self, phase, ins, outs, scr):
        ssem, rsem, lsem = scr
        x, y, c, chips = _place()
        me, sib = 4 * x + 2 * y + c, (x, y, 1 - c)
        idx = lambda chip, core: 4 * chip[0] + 2 * chip[1] + core
        for u, (src_arr, kind) in enumerate(self.units):
            src, full = ins[u], outs[u]
            width = src_arr.shape[-1] if kind == "cols" else src_arr.shape[-2]
            slot = functools.partial(_slot, full, kind, width)

            def copy(k, block, to, from_src=False):
                return pltpu.make_async_remote_copy(src_ref=src if from_src else slot(block), dst_ref=slot(block),
                                                    send_sem=ssem.at[u, k], recv_sem=rsem.at[u, k],
                                                    device_id=to, device_id_type=MESH)

            mine = lambda: pltpu.make_async_copy(src, slot(me), lsem.at[u])
            first = lambda: [copy(0, me, sib, True)] + [copy(1 + j, me, (*chip, c), True) for j, chip in enumerate(chips)]
            passed = lambda j: copy(4 + j, idx(chips[j], c), sib)
            if phase == "start":
                mine().start()
                for cp in first():
                    cp.start()
            elif phase == "forward":
                for j, chip in enumerate(chips):
                    copy(1 + j, idx(chip, c), sib).wait_recv()
                    passed(j).start()
            else:
                copy(0, idx((x, y), 1 - c), sib).wait_recv()
                for j, chip in enumerate(chips):
                    copy(4 + j, idx(chip, 1 - c), sib).wait_recv()
                for cp in first() + [passed(j) for j in range(3)]:
                    cp.wait_send()
                mine().wait()


class _Scatter:
    def __init__(self, units):
        self.units = units
        self.inputs = list(units)
        self.out_shape = [S(u.shape, u.dtype) for u in units]
        n = len(units)
        self.scratch = [pltpu.SemaphoreType.DMA((n, 3)), pltpu.SemaphoreType.DMA((n, 3)), pltpu.SemaphoreType.DMA((n,))]

    def run(self, phase, ins, outs, scr):
        ssem, rsem, lsem = scr
        x, y, c, chips = _place()
        myq = 2 * x + y
        for u in range(len(self.units)):
            h, p = ins[u], outs[u]

            def copy(k, chip, send_to_them):
                q = 2 * chip[0] + chip[1]
                return pltpu.make_async_remote_copy(src_ref=h.at[q], dst_ref=p.at[myq if send_to_them else q],
                                                    send_sem=ssem.at[u, k], recv_sem=rsem.at[u, k],
                                                    device_id=(*chip, c), device_id_type=MESH)

            mine = lambda: pltpu.make_async_copy(h.at[myq], p.at[myq], lsem.at[u])
            sends = lambda: [copy(k, chip, True) for k, chip in enumerate(chips)]
            if phase == "start":
                mine().start()
                for cp in sends():
                    cp.start()
            elif phase == "finish":
                for k, chip in enumerate(chips):
                    copy(k, chip, False).wait_recv()
                for cp in sends():
                    cp.wait_send()
                mine().wait()


class _Comms:
    def __init__(self, parts):
        self.parts = parts
        self.inputs = [a for p in parts for a in p.inputs]
        self.out_shape = [a for p in parts for a in p.out_shape]
        self.scratch = [a for p in parts for a in p.scratch]

    def run(self, phase, ins, outs, scr):
        i = o = s = 0
        for p in self.parts:
            ni, no, ns = len(p.inputs), len(p.out_shape), len(p.scratch)
            p.run(phase, ins[i:i + ni], outs[o:o + no], scr[s:s + ns])
            i, o, s = i + ni, o + no, s + ns


def _launch(body, grid, in_specs, out_specs, out_shape, scratch, args, name, sem, comm=None):
    if comm is None:
        outs = pl.pallas_call(body, grid=grid, name=name, in_specs=in_specs, out_specs=out_specs, out_shape=out_shape,
                              scratch_shapes=scratch, compiler_params=_params(*sem))(*args)
        return list(outs), []
    n_in, n_out, n_scr = len(args), len(out_shape), len(scratch)
    ci, co = len(comm.inputs), len(comm.out_shape)
    nsteps = grid[0]
    fwd_step = (7 * nsteps) // 8

    def hosted(*refs):
        a = refs[:n_in]
        ca = refs[n_in:n_in + ci]
        o = refs[n_in + ci:n_in + ci + n_out]
        cout = refs[n_in + ci + n_out:n_in + ci + n_out + co]
        s = refs[n_in + ci + n_out + co:n_in + ci + n_out + co + n_scr]
        cs = refs[n_in + ci + n_out + co + n_scr:]
        step = pl.program_id(0)

        @pl.when(step == 0)
        def _():
            comm.run("start", ca, cout, cs)

        body(*a, *o, *s)

        @pl.when(step == fwd_step)
        def _():
            comm.run("forward", ca, cout, cs)

        @pl.when(step == nsteps - 1)
        def _():
            comm.run("finish", ca, cout, cs)

    outs = pl.pallas_call(
        hosted, grid=grid, name=name, in_specs=list(in_specs) + [_HBM] * ci, out_specs=list(out_specs) + [_HBM] * co,
        out_shape=list(out_shape) + comm.out_shape, scratch_shapes=list(scratch) + comm.scratch,
        compiler_params=_params(*["arbitrary"] * len(grid)))(*args, *comm.inputs)
    return list(outs[:n_out]), list(outs[n_out:])


def _comm_only(comm, name):
    ci, co = len(comm.inputs), len(comm.out_shape)

    def body(*refs):
        ca, cout, cs = refs[:ci], refs[ci:ci + co], refs[ci + co:]
        for phase in ("start", "forward", "finish"):
            comm.run(phase, ca, cout, cs)

    return pl.pallas_call(body, name=name, in_specs=[_HBM] * ci, out_specs=[_HBM] * co, out_shape=comm.out_shape,
                          scratch_shapes=comm.scratch, compiler_params=_params())(*comm.inputs)


def _ret_consts():
    idx = np.arange(C, dtype=np.float32)
    gf = (1.0 - np.exp2(-5.0 - np.arange(HEADS, dtype=np.float32))).astype(np.float32)
    out = {}
    for name, gamma, fwd in (("f", gf, True), ("b", gf[::-1].copy(), False)):
        lg = np.log(gamma).astype(np.float32)[:, None]
        diff = idx[:, None] - idx[None, :]
        if fwd:
            mask = diff >= 0
            dist = np.where(mask, diff, 0.0)
            zeta = np.exp(lg * (C - 1 - idx))
            xi = np.exp(lg * (idx + 1))
        else:
            mask = diff < 0
            dist = np.where(mask, -diff, 0.0)
            zeta = np.exp(lg * idx)
            xi = np.exp(lg * (C - idx))
        dm = np.where(mask[None], np.exp(lg[:, :, None] * dist[None]), 0.0).astype(np.float32)
        bc = lambda vec: np.ascontiguousarray(np.broadcast_to(vec.astype(np.float32)[:, :, None], (HEADS, C, DH)))
        out[name] = dict(D=dm, XI=bc(xi), ZETA=bc(zeta), gC=[float(v) for v in np.exp(lg[:, 0] * C).astype(np.float32)])
    return out


def _rope_tables(t):
    half = DH // 2
    inv_freq = (np.float32(10000.0) ** (-np.arange(half, dtype=np.float32) / np.float32(half))).astype(np.float32)
    ang = (np.arange(t, dtype=np.float32)[:, None] * inv_freq[None, :]).astype(np.float64)
    cos, sin = np.cos(ang).astype(np.float32), np.sin(ang).astype(np.float32)
    return np.concatenate([cos, cos], axis=1), np.concatenate([-sin, sin], axis=1)


def _f_inproj(x, g1, w, cos2, sin2, l, name, comm=None):
    t = x.shape[0]
    tm = 512

    def body(x_ref, g_ref, w_ref, cos_ref, sin_ref, proj_ref, ht_ref):
        xv = x_ref[...]
        h = (xv * _rms(xv) * g_ref[...]).astype(BF16)
        ht_ref[...] = h.T
        for nb in range(INW // 512):
            cs = slice(nb * 512, (nb + 1) * 512)
            res = _dot(h, w_ref[:, cs])
            if nb in (1, 2):
                for hh in range(HEADS):
                    r = _rot(res[:, hh * DH:(hh + 1) * DH], cos_ref[...], sin_ref[...])
                    proj_ref[:, nb * 512 + hh * DH:nb * 512 + (hh + 1) * DH] = (r * SCALE if nb == 2 else r).astype(BF16)
            else:
                proj_ref[:, cs] = res.astype(BF16)

    return _launch(
        body, (t // tm,),
        [pl.BlockSpec((tm, D), lambda i: (i, 0)),
         pl.BlockSpec((None, 1, D), lambda i: (l, 0, 0)),
         _resident((D, INW), lambda i: (0, 0)),
         pl.BlockSpec((tm, DH), lambda i: (i, 0)), pl.BlockSpec((tm, DH), lambda i: (i, 0))],
        [pl.BlockSpec((tm, INW), lambda i: (i, 0)), pl.BlockSpec((D, tm), lambda i: (0, i))],
        [S((t, INW), BF16), S((D, t), BF16)], [], (x, g1, w, cos2, sin2), name, ("parallel",), comm)


def _gm_chunk_fwd(u, v, lng, lnb, ws_ref, bias):
    au, dau = _gelu_and_grad(u)
    av, dav = _gelu_and_grad(v)
    vhat, rstd = _standardize(av)
    vn = (vhat * lng + lnb).astype(BF16)
    head = lax.broadcasted_iota(jnp.int32, (C, GMW), 1) // (GMW // HEADS)
    mixed = bias
    for h in range(HEADS):
        mixed = mixed + jnp.where(head == h, _dot(ws_ref[h], vn), 0.0)
    return au, dau, dav, vhat, rstd, vn, mixed, head


def _f_gm(proj, lng, lnb, ws_bf, bias, l, name):
    t = proj.shape[0]
    tm = 512

    def body(p_ref, lng_ref, lnb_ref, ws_ref, bias_ref, y_ref):
        for ci in range(tm // C):
            rows = slice(ci * C, (ci + 1) * C)
            u = p_ref[rows, 0:GMW].astype(F32)
            v = p_ref[rows, GMW:2 * GMW].astype(F32)
            au, _, _, _, _, _, mixed, _ = _gm_chunk_fwd(u, v, lng_ref[...], lnb_ref[...], ws_ref, bias_ref[...])
            y_ref[rows, :] = (au * mixed).astype(BF16)

    return pl.pallas_call(
        body, grid=(t // tm,), name=name,
        in_specs=[pl.BlockSpec((tm, 2 * GMW), lambda i: (i, 0)),
                  pl.BlockSpec((None, 1, GMW), lambda i: (l, 0, 0)),
                  pl.BlockSpec((None, 1, GMW), lambda i: (l, 0, 0)),
                  pl.BlockSpec((None, HEADS, C, C), lambda i: (l, 0, 0, 0)),
                  pl.BlockSpec((None, C, GMW), lambda i: (l, 0, 0))],
        out_specs=pl.BlockSpec((tm, GMW), lambda i: (i, 0)),
        out_shape=S((t, GMW), BF16),
        compiler_params=_params("parallel"),
    )(proj, lng, lnb, ws_bf, bias)


def _scan_pair(proj, other, col, wf, wb, gcf, gcb, first_is_f, name):
    t = proj.shape[0]
    n = t // C
    sc = SCAN_CHUNKS
    nsteps = n // sc
    other_is_proj = other is None

    def body(a1, o1, a2, o2, w1_ref, w2_ref, out1, out2, st1, st2):
        @pl.when(pl.program_id(0) == 0)
        def _():
            st1[...] = jnp.zeros_like(st1)
            st2[...] = jnp.zeros_like(st2)

        def one(a_ref, o_ref, w_ref, gc, st, out, order):
            for h in range(HEADS):
                sl = slice(h * DH, (h + 1) * DH)
                incs = {}
                for j in order:
                    rows = slice(j * C, (j + 1) * C)
                    aw = (a_ref[rows, sl].astype(F32) * w_ref[h]).astype(BF16)
                    incs[j] = _dot_tn(aw, o_ref[rows, sl].astype(BF16))
                cur = st[h]
                for j in order:
                    out[j, h] = cur.astype(BF16)
                    cur = gc[h] * cur + incs[j]
                st[h] = cur

        g1, g2 = (gcf, gcb) if first_is_f else (gcb, gcf)
        one(a1, o1, w1_ref, g1, st1, out1, range(sc))
        one(a2, o2, w2_ref, g2, st2, out2, range(sc - 1, -1, -1))

    up = lambda i: i
    down = lambda i: nsteps - 1 - i

    def specs(ix):
        o_spec = pl.BlockSpec((sc * C, RETW), lambda i: (ix(i), 3 if other_is_proj else 0))
        return [pl.BlockSpec((sc * C, RETW), lambda i: (ix(i), col)), o_spec]

    const = lambda: pl.BlockSpec((HEADS, C, DH), lambda i: (0, 0, 0))
    oth = proj if other_is_proj else other
    w1, w2 = (wf, wb) if first_is_f else (wb, wf)
    out1, out2 = pl.pallas_call(
        body, grid=(nsteps,), name=name,
        in_specs=specs(up) + specs(down) + [const(), const()],
        out_specs=[pl.BlockSpec((sc, HEADS, DH, DH), lambda i: (up(i), 0, 0, 0)),
                   pl.BlockSpec((sc, HEADS, DH, DH), lambda i: (down(i), 0, 0, 0))],
        out_shape=[S((n, HEADS, DH, DH), BF16), S((n, HEADS, DH, DH), BF16)],
        scratch_shapes=[pltpu.VMEM((HEADS, DH, DH), F32), pltpu.VMEM((HEADS, DH, DH), F32)],
        compiler_params=_params("arbitrary"),
    )(proj, oth, proj, oth, w1, w2)
    return (out1, out2) if first_is_f else (out2, out1)


SCAN_CHUNKS = 4


RET_CHUNKS = 4


def _f_ret_out(proj, s_f, s_b, rc, name):
    t = proj.shape[0]
    tm = RET_CHUNKS * C

    def body(q_ref, k_ref, v_ref, sf_ref, sb_ref, d_ref, xif_ref, xib_ref, o_ref):
        for ci in range(RET_CHUNKS):
            rows = slice(ci * C, (ci + 1) * C)
            for h in range(HEADS):
                sl = slice(h * DH, (h + 1) * DH)
                qh, kh, vh = q_ref[rows, sl], k_ref[rows, sl], v_ref[rows, sl]
                p = (_dot_nt(qh, kh) * d_ref[h]).astype(BF16)
                cross = _dot(qh, jnp.concatenate([sf_ref[ci, h], sb_ref[ci, h]], axis=1))
                o_ref[rows, sl] = _dot(p, vh) + xif_ref[h] * cross[:, 0:DH] + xib_ref[h] * cross[:, DH:2 * DH]

    const = lambda: pl.BlockSpec((HEADS, C, DH), lambda i: (0, 0, 0))
    state = lambda: pl.BlockSpec((RET_CHUNKS, HEADS, DH, DH), lambda i: (i, 0, 0, 0))
    return pl.pallas_call(
        body, grid=(t // tm,), name=name,
        in_specs=[pl.BlockSpec((tm, RETW), lambda i, cb=cb: (i, cb)) for cb in (1, 2, 3)] +
                 [state(), state(), const(), const(), const()],
        out_specs=pl.BlockSpec((tm, RETW), lambda i: (i, 0)),
        out_shape=S((t, RETW), F32),
        compiler_params=_params("parallel"),
    )(proj, proj, proj, s_f, s_b, rc["f"]["D"] + rc["b"]["D"], rc["f"]["XI"], rc["b"]["XI"])


def _conv_halo_specs(t, tm, width, col):
    r = tm // HALO
    last = t // HALO - 1
    return [pl.BlockSpec((HALO, width), lambda i: (jnp.maximum(i * r - 1, 0), col)),
            pl.BlockSpec((tm, width), lambda i: (i, col)),
            pl.BlockSpec((HALO, width), lambda i: (jnp.minimum((i + 1) * r, last), col))]


def _fill_ext(ext, prev, cur, nxt, i, nt, tm):
    ext[0:HALO, :] = jnp.where(i > 0, prev, 0.0)
    ext[HALO:HALO + tm, :] = cur
    ext[HALO + tm:2 * HALO + tm, :] = jnp.where(i < nt - 1, nxt, 0.0)


def _glu(a_ref, g_ref):
    return a_ref[...].astype(F32) * _sigmoid(g_ref[...].astype(F32))


def _shifted_copies(ext, rot, tm):
    rows = tm + 2 * HALO - 8
    for b in range(1, 8):
        rot[b - 1, :, :] = ext[pl.ds(b, rows), :]


def _tap(ext, rot, r0, s, rb):
    a, b = divmod(s, 8)
    return ext[pl.ds(r0 + 8 * a, rb), :] if b == 0 else rot[b - 1, pl.ds(r0 + 8 * a, rb), :]


def _f_conv(proj, cw, cb, lng, lnb, l, name, comm=None):
    t = proj.shape[0]
    tm = 256
    nt = t // tm
    rb = 64

    def body(ap, ac, an, gp, gc, gn, cw_ref, cb_ref, lng_ref, lnb_ref, c_ref, y_ref, hext, hrot):
        i = pl.program_id(0)
        _fill_ext(hext, _glu(ap, gp), _glu(ac, gc), _glu(an, gn), i, nt, tm)
        _shifted_copies(hext, hrot, tm)
        for r0 in range(0, tm, rb):
            acc = jnp.zeros((rb, CVW), F32) + cb_ref[...]
            for j in range(KW):
                acc = acc + cw_ref[j:j + 1, :] * _tap(hext, hrot, r0, j + 1, rb)
            c_ref[r0:r0 + rb, :] = acc
            chat, _ = _standardize(acc)
            z = chat * lng_ref[...] + lnb_ref[...]
            y_ref[r0:r0 + rb, :] = (z * _sigmoid(z)).astype(BF16)

    vec = lambda: pl.BlockSpec((None, 1, CVW), lambda i: (l, 0, 0))
    return _launch(
        body, (nt,),
        _conv_halo_specs(t, tm, CVW, 10) + _conv_halo_specs(t, tm, CVW, 11) +
        [pl.BlockSpec((None, 32, CVW), lambda i: (l, 0, 0)), vec(), vec(), vec()],
        [pl.BlockSpec((tm, CVW), lambda i: (i, 0)), pl.BlockSpec((tm, CVW), lambda i: (i, 0))],
        [S((t, CVW), F32), S((t, CVW), BF16)],
        [pltpu.VMEM((tm + 2 * HALO, CVW), F32), pltpu.VMEM((7, tm + 2 * HALO - 8, CVW), F32)],
        (proj, proj, proj, proj, proj, proj, cw, cb, lng, lnb), name, ("parallel",), comm)


def _f_mixout(x, y_gm, y_cv, o, proj, w, name):
    t = x.shape[0]
    tm = 512

    def body(x_ref, ygm_ref, ycv_ref, o_ref, g_ref, w_ref, xm_ref, ycat_t_ref, ycat):
        ycat[:, 0:GMW] = ygm_ref[...]
        ycat[:, GMW + RETW:D] = ycv_ref[...]
        for h in range(HEADS):
            sl = slice(h * DH, (h + 1) * DH)
            ohat, _ = _standardize(o_ref[:, sl])
            g = g_ref[:, sl].astype(F32)
            ycat[:, GMW + h * DH:GMW + (h + 1) * DH] = (ohat * (g * _sigmoid(g))).astype(BF16)
        yc = ycat[...]
        ycat_t_ref[...] = yc.T
        xm_ref[...] = x_ref[...] + _dot(yc, w_ref[...])

    return pl.pallas_call(
        body, grid=(t // tm,), name=name,
        in_specs=[pl.BlockSpec((tm, D), lambda i: (i, 0)),
                  pl.BlockSpec((tm, GMW), lambda i: (i, 0)),
                  pl.BlockSpec((tm, CVW), lambda i: (i, 0)),
                  pl.BlockSpec((tm, RETW), lambda i: (i, 0)),
                  pl.BlockSpec((tm, RETW), lambda i: (i, 4)),
                  _resident((D, D), lambda i: (0, 0))],
        out_specs=[pl.BlockSpec((tm, D), lambda i: (i, 0)), pl.BlockSpec((D, tm), lambda i: (0, i))],
        out_shape=[S((t, D), F32), S((D, t), BF16)],
        scratch_shapes=[pltpu.VMEM((tm, D), BF16)],
        compiler_params=_params("parallel"),
    )(x, y_gm, y_cv, o, proj, w)


def _f_ffn(xm, g2, w1, w2, l, name, comm=None):
    t = xm.shape[0]
    tm = 512
    half = NDEV // 2

    def body(x_ref, g_ref, w1_ref, w2_ref, xo_ref, ht_ref, gu_ref, act_t_ref):
        xv = x_ref[...]
        h = (xv * _rms(xv) * g_ref[...]).astype(BF16)
        ht_ref[...] = h.T
        acc = xv
        for j in range(half):
            gate = _dot(h, w1_ref[j])
            up = _dot(h, w1_ref[half + j])
            gu_ref[j] = gate.astype(BF16)
            gu_ref[half + j] = up.astype(BF16)
            a = ((gate * _sigmoid(gate)) * up).astype(BF16)
            act_t_ref[j] = a.T
            acc = acc + _dot(a, w2_ref[j * FFB:(j + 1) * FFB, :])
        xo_ref[...] = acc

    return _launch(
        body, (t // tm,),
        [pl.BlockSpec((tm, D), lambda i: (i, 0)),
         pl.BlockSpec((None, 1, D), lambda i: (l, 0, 0)),
         _resident((NDEV, D, FFB), lambda i: (0, 0, 0)),
         _resident((FFH, D), lambda i: (0, 0))],
        [pl.BlockSpec((tm, D), lambda i: (i, 0)), pl.BlockSpec((D, tm), lambda i: (0, i)),
         pl.BlockSpec((NDEV, tm, FFB), lambda i: (0, i, 0)), pl.BlockSpec((half, FFB, tm), lambda i: (0, 0, i))],
        [S((t, D), F32), S((D, t), BF16), S((NDEV, t, FFB), BF16), S((half, FFB, t), BF16)],
        [], (xm, g2, w1, w2), name, ("parallel",), comm)


def _b_loss(x, fg, tgt, name):
    t = x.shape[0]
    tm = 512

    def body(x_ref, g_ref, t_ref, loss_ref, dx_ref, dg_ref):
        @pl.when(pl.program_id(0) == 0)
        def _():
            loss_ref[...] = jnp.zeros_like(loss_ref)
            dg_ref[...] = jnp.zeros_like(dg_ref)

        xv = x_ref[...]
        r = _rms(xv)
        xr = xv * r
        err = xr * g_ref[...] - t_ref[...]
        loss_ref[...] += (0.5 / D) * _col_sum(jnp.sum(err * err, axis=1, keepdims=True))
        dy = err * (1.0 / D)
        dg_ref[...] += _col_sum(dy * xr)
        dx_ref[...] = _rmsnorm_bwd(dy, xv, r, g_ref[...])

    return pl.pallas_call(
        body, grid=(t // tm,), name=name,
        in_specs=[pl.BlockSpec((tm, D), lambda i: (i, 0)), pl.BlockSpec((1, D), lambda i: (0, 0)),
                  pl.BlockSpec((tm, D), lambda i: (i, 0))],
        out_specs=[pl.BlockSpec((1, 1), lambda i: (0, 0)), pl.BlockSpec((tm, D), lambda i: (i, 0)),
                   pl.BlockSpec((1, D), lambda i: (0, 0))],
        out_shape=[S((1, 1), F32), S((t, D), F32), S((1, D), F32)],
        compiler_params=_params("arbitrary"),
    )(x, fg, tgt)


def _b_ffn(dxo, xm, g2, gu, w1, w2, l, name, comm=None):
    t = xm.shape[0]
    tm = 256
    half = NDEV // 2

    def body(dxo_ref, x_ref, g_ref, gu_ref, w1_ref, w2_ref, dgu_ref, dxm_ref, dxb_ref, dg_ref):
        @pl.when(pl.program_id(0) == 0)
        def _():
            dg_ref[...] = jnp.zeros_like(dg_ref)

        dxo = dxo_ref[...]
        dxb = dxo.astype(BF16)
        dxb_ref[...] = dxb
        dh = jnp.zeros((tm, D), F32)
        for j in range(half):
            dact = _dot_nt(dxb, w2_ref[j * FFB:(j + 1) * FFB, :])
            gate = gu_ref[j].astype(F32)
            up = gu_ref[half + j].astype(F32)
            sg, dsg = _silu_and_grad(gate)
            dgate = (dact * up * dsg).astype(BF16)
            dup = (dact * sg).astype(BF16)
            dgu_ref[j] = dgate
            dgu_ref[half + j] = dup
            dh = dh + _dot_nt(dgate, w1_ref[j]) + _dot_nt(dup, w1_ref[half + j])
        xv = x_ref[...]
        r = _rms(xv)
        dg_ref[...] += _col_sum(dh * xv * r)
        dxm_ref[...] = dxo + _rmsnorm_bwd(dh, xv, r, g_ref[...])

    return _launch(
        body, (t // tm,),
        [pl.BlockSpec((tm, D), lambda i: (i, 0)), pl.BlockSpec((tm, D), lambda i: (i, 0)),
         pl.BlockSpec((None, 1, D), lambda i: (l, 0, 0)),
         pl.BlockSpec((NDEV, tm, FFB), lambda i: (0, i, 0)),
         _resident((NDEV, D, FFB), lambda i: (0, 0, 0)),
         _resident((FFH, D), lambda i: (0, 0))],
        [pl.BlockSpec((NDEV, tm, FFB), lambda i: (0, i, 0)), pl.BlockSpec((tm, D), lambda i: (i, 0)),
         pl.BlockSpec((tm, D), lambda i: (i, 0)), pl.BlockSpec((1, D), lambda i: (0, 0))],
        [S((NDEV, t, FFB), BF16), S((t, D), F32), S((t, D), BF16), S((1, D), F32)],
        [], (dxo, xm, g2, gu, w1, w2), name, ("arbitrary",), comm)


def _mm_wgrad(at, b, pieces, at_lead, b_mode, group, name):
    bt = 1024
    t = at.shape[-1]
    bt = min(bt, t)
    nt = t // bt
    ka = at.shape[-2]
    if at_lead:
        a_spec = pl.BlockSpec((None, ka, bt), lambda j, tt: (j, 0, tt))
    else:
        a_spec = pl.BlockSpec((ka, bt), lambda j, tt: (0, tt))
    if b_mode == "shared":
        nb, b_spec = b.shape[1], pl.BlockSpec((bt, b.shape[1]), lambda j, tt: (tt, 0))
    elif b_mode == "cols":
        nb = b.shape[1] // pieces
        b_spec = pl.BlockSpec((bt, group * nb), lambda j, tt: (tt, j))
    else:
        nb, b_spec = b.shape[2], pl.BlockSpec((None, bt, b.shape[2]), lambda j, tt: (j, tt, 0))
    assert group == 1 or b_mode == "cols"

    def body(a_ref, b_ref, o_ref, acc):
        tt = pl.program_id(1)

        @pl.when(tt == 0)
        def _():
            acc[...] = jnp.zeros_like(acc)

        acc[...] += _dot(a_ref[...], b_ref[...])

        @pl.when(tt == nt - 1)
        def _():
            for k in range(group):
                o_ref[k] = acc[:, k * nb:(k + 1) * nb].astype(BF16)

    return pl.pallas_call(
        body, grid=(pieces // group, nt), name=name,
        in_specs=[a_spec, b_spec],
        out_specs=pl.BlockSpec((group, ka, nb), lambda j, tt: (j, 0, 0)),
        out_shape=S((pieces, ka, nb), BF16),
        scratch_shapes=[pltpu.VMEM((ka, group * nb), F32)],
        compiler_params=_params("parallel", "arbitrary"),
    )(at, b)


def _b_mixout(dxm, w, o, proj, c, lng, lnb, l, name):
    t = dxm.shape[0]
    tm = 256

    def body(dxm_ref, w_ref, o_ref, g_ref, c_ref, lng_ref, lnb_ref,
             dxb_ref, dygm_ref, dO_ref, dg_ref, dc_ref, dlg_ref, dlb_ref, dcb_ref):
        @pl.when(pl.program_id(0) == 0)
        def _():
            dlg_ref[...] = jnp.zeros_like(dlg_ref)
            dlb_ref[...] = jnp.zeros_like(dlb_ref)
            dcb_ref[...] = jnp.zeros_like(dcb_ref)

        dxb = dxm_ref[...].astype(BF16)
        dxb_ref[...] = dxb
        dy = _dot_nt(dxb, w_ref[...])
        dygm_ref[...] = dy[:, 0:GMW]
        for h in range(HEADS):
            sl = slice(h * DH, (h + 1) * DH)
            ohat, rstd = _standardize(o_ref[:, sl])
            sg, dsg = _silu_and_grad(g_ref[:, sl].astype(F32))
            dyr = dy[:, GMW + h * DH:GMW + (h + 1) * DH]
            dg_ref[:, sl] = (dyr * ohat * dsg).astype(BF16)
            dO_ref[:, sl] = _standardize_bwd(dyr * sg, ohat, rstd)
        chat, rstd = _standardize(c_ref[...])
        z = chat * lng_ref[...] + lnb_ref[...]
        _, dsz = _silu_and_grad(z)
        dz = dy[:, GMW + RETW:D] * dsz
        dlg_ref[...] += _col_sum(dz * chat)
        dlb_ref[...] += _col_sum(dz)
        dc = _standardize_bwd(dz * lng_ref[...], chat, rstd)
        dcb_ref[...] += _col_sum(dc)
        dc_ref[...] = dc

    vec = lambda: pl.BlockSpec((None, 1, CVW), lambda i: (l, 0, 0))
    acc = lambda: pl.BlockSpec((1, CVW), lambda i: (0, 0))
    return pl.pallas_call(
        body, grid=(t // tm,), name=name,
        in_specs=[pl.BlockSpec((tm, D), lambda i: (i, 0)),
                  _resident((D, D), lambda i: (0, 0)),
                  pl.BlockSpec((tm, RETW), lambda i: (i, 0)),
                  pl.BlockSpec((tm, RETW), lambda i: (i, 4)),
                  pl.BlockSpec((tm, CVW), lambda i: (i, 0)), vec(), vec()],
        out_specs=[pl.BlockSpec((tm, D), lambda i: (i, 0)), pl.BlockSpec((tm, GMW), lambda i: (i, 0)),
                   pl.BlockSpec((tm, RETW), lambda i: (i, 0)), pl.BlockSpec((tm, RETW), lambda i: (i, 0)),
                   pl.BlockSpec((tm, CVW), lambda i: (i, 0)), acc(), acc(), acc()],
        out_shape=[S((t, D), BF16), S((t, GMW), F32), S((t, RETW), F32), S((t, RETW), BF16), S((t, CVW), F32),
                   S((1, CVW), F32), S((1, CVW), F32), S((1, CVW), F32)],
        compiler_params=_params("arbitrary"),
    )(dxm, w, o, proj, c, lng, lnb)


def _b_gm(proj, dy, lng, lnb, ws_bf, wst_bf, bias, l, name):
    t = proj.shape[0]
    tm = 512
    nt = t // tm

    def body(p_ref, dy_ref, lng_ref, lnb_ref, ws_ref, wst_ref, bias_ref,
             duv_ref, dws_ref, dbias_ref, dbs_ref, dlg_ref, dlb_ref):
        @pl.when(pl.program_id(0) == 0)
        def _():
            dws_ref[...] = jnp.zeros_like(dws_ref)
            dbias_ref[...] = jnp.zeros_like(dbias_ref)
            dbs_ref[...] = jnp.zeros_like(dbs_ref)
            dlg_ref[...] = jnp.zeros_like(dlg_ref)
            dlb_ref[...] = jnp.zeros_like(dlb_ref)

        for ci in range(tm // C):
            rows = slice(ci * C, (ci + 1) * C)
            u = p_ref[rows, 0:GMW].astype(F32)
            v = p_ref[rows, GMW:2 * GMW].astype(F32)
            au, dau, dav, vhat, rstd, vn, mixed, head = _gm_chunk_fwd(u, v, lng_ref[...], lnb_ref[...], ws_ref, bias_ref[...])
            dyc = dy_ref[rows, :]
            dmixed = dyc * au
            dmb = dmixed.astype(BF16)
            dbias_ref[...] += dmixed
            dvn = jnp.zeros((C, GMW), F32)
            for h in range(HEADS):
                dws_ref[h] += _dot_nt(jnp.where(head == h, dmixed, 0.0).astype(BF16), vn)
                dvn = dvn + jnp.where(head == h, _dot(wst_ref[h], dmb), 0.0)
            dlg_ref[...] += _col_sum(dvn * vhat)
            dlb_ref[...] += _col_sum(dvn)
            dav_in = _standardize_bwd(dvn * lng_ref[...], vhat, rstd)
            duv_ref[rows, 0:GMW] = (dyc * mixed * dau).astype(BF16)
            duv_ref[rows, GMW:2 * GMW] = (dav_in * dav).astype(BF16)

        @pl.when(pl.program_id(0) == nt - 1)
        def _():
            head = lax.broadcasted_iota(jnp.int32, (C, GMW), 1) // (GMW // HEADS)
            lane = lax.broadcasted_iota(jnp.int32, (C, 128), 1)
            fold = jnp.zeros((C, 128), F32)
            for h in range(HEADS):
                col = jnp.sum(jnp.where(head == h, dbias_ref[...], 0.0), axis=1, keepdims=True)
                fold = jnp.where(lane == h, col, fold)
            dbs_ref[...] = fold

    vec = lambda: pl.BlockSpec((None, 1, GMW), lambda i: (l, 0, 0))
    mats = lambda: pl.BlockSpec((None, HEADS, C, C), lambda i: (l, 0, 0, 0))
    return pl.pallas_call(
        body, grid=(nt,), name=name,
        in_specs=[pl.BlockSpec((tm, 2 * GMW), lambda i: (i, 0)), pl.BlockSpec((tm, GMW), lambda i: (i, 0)),
                  vec(), vec(), mats(), mats(), pl.BlockSpec((None, C, GMW), lambda i: (l, 0, 0))],
        out_specs=[pl.BlockSpec((tm, 2 * GMW), lambda i: (i, 0)),
                   pl.BlockSpec((HEADS, C, C), lambda i: (0, 0, 0)),
                   pl.BlockSpec((C, GMW), lambda i: (0, 0)), pl.BlockSpec((C, 128), lambda i: (0, 0)),
                   pl.BlockSpec((1, GMW), lambda i: (0, 0)), pl.BlockSpec((1, GMW), lambda i: (0, 0))],
        out_shape=[S((t, 2 * GMW), BF16), S((HEADS, C, C), F32), S((C, GMW), F32), S((C, 128), F32),
                   S((1, GMW), F32), S((1, GMW), F32)],
        compiler_params=_params("arbitrary"),
    )(proj, dy, lng, lnb, ws_bf, wst_bf, bias)


def _b_conv(proj, dc, cw, l, name, comm=None):
    t = proj.shape[0]
    tm = 256
    nt = t // tm
    rb = 64

    def body(ap, ac, an, gp, gc, gn, dp, dcur, dn, cw_ref, dag_ref, dcw_ref, hext, dext, hrot, drot):
        i = pl.program_id(0)

        @pl.when(i == 0)
        def _():
            dcw_ref[...] = jnp.zeros_like(dcw_ref)

        _fill_ext(hext, _glu(ap, gp), _glu(ac, gc), _glu(an, gn), i, nt, tm)
        _fill_ext(dext, dp[...], dcur[...], dn[...], i, nt, tm)
        _shifted_copies(hext, hrot, tm)
        _shifted_copies(dext, drot, tm)
        for j in range(KW):
            dcw_ref[j:j + 1, :] += _col_sum(dcur[...] * _tap(hext, hrot, 0, j + 1, tm))
        for r0 in range(0, tm, rb):
            dh = jnp.zeros((rb, CVW), F32)
            for j in range(KW):
                dh = dh + cw_ref[j:j + 1, :] * _tap(dext, drot, r0, 2 * HALO - 1 - j, rb)
            a = ac[r0:r0 + rb, :].astype(F32)
            s = _sigmoid(gc[r0:r0 + rb, :].astype(F32))
            dag_ref[r0:r0 + rb, 0:CVW] = (dh * s).astype(BF16)
            dag_ref[r0:r0 + rb, CVW:2 * CVW] = (dh * a * s * (1.0 - s)).astype(BF16)

    dspecs = _conv_halo_specs(t, tm, CVW, 0)
    return _launch(
        body, (nt,),
        _conv_halo_specs(t, tm, CVW, 10) + _conv_halo_specs(t, tm, CVW, 11) + dspecs +
        [pl.BlockSpec((None, 32, CVW), lambda i: (l, 0, 0))],
        [pl.BlockSpec((tm, 2 * CVW), lambda i: (i, 0)), pl.BlockSpec((32, CVW), lambda i: (0, 0))],
        [S((t, 2 * CVW), BF16), S((32, CVW), F32)],
        [pltpu.VMEM((tm + 2 * HALO, CVW), F32), pltpu.VMEM((tm + 2 * HALO, CVW), F32),
         pltpu.VMEM((7, tm + 2 * HALO - 8, CVW), F32), pltpu.VMEM((7, tm + 2 * HALO - 8, CVW), F32)],
        (proj, proj, proj, proj, proj, proj, dc, dc, dc, cw), name, ("arbitrary",), comm)


def _b_ret_out(proj, cos2, sin2, dO, s_f, s_b, g_f, g_b, rc, name, comm=None):
    t = proj.shape[0]
    tm = RET_CHUNKS * C

    def body(q_ref, k_ref, v_ref, cos_ref, sin_ref, dO_ref, sf_ref, sb_ref, gf_ref, gb_ref,
             d_ref, xif_ref, xib_ref, zef_ref, zeb_ref, dq_ref, dk_ref, dv_ref):
        for ci in range(RET_CHUNKS):
            rows = slice(ci * C, (ci + 1) * C)
            cos_v, sin_v = cos_ref[rows, :], sin_ref[rows, :]
            for h in range(HEADS):
                sl = slice(h * DH, (h + 1) * DH)
                qh, kh, vh = q_ref[rows, sl], k_ref[rows, sl], v_ref[rows, sl]
                dOh = dO_ref[rows, sl].astype(BF16)
                dm = d_ref[h]
                p = (_dot_nt(qh, kh) * dm).astype(BF16)
                dp = (_dot_nt(dOh, vh) * dm).astype(BF16)
                from_s = _dot_nt(dOh, jnp.concatenate([sf_ref[ci, h], sb_ref[ci, h]], axis=0))
                from_g = _dot_nt(vh, jnp.concatenate([gf_ref[ci, h], gb_ref[ci, h]], axis=0))
                kg = _dot(kh, jnp.concatenate([gf_ref[ci, h], gb_ref[ci, h]], axis=1))
                dqr = _dot(dp, kh) + xif_ref[h] * from_s[:, 0:DH] + xib_ref[h] * from_s[:, DH:2 * DH]
                dkr = (_dot_tn(dp, qh) + zef_ref[h] * from_g[:, 0:DH] + zeb_ref[h] * from_g[:, DH:2 * DH]) * SCALE
                dv = _dot_tn(p, dOh) + zef_ref[h] * kg[:, 0:DH] + zeb_ref[h] * kg[:, DH:2 * DH]
                dq_ref[rows, sl] = _rot_t(dqr, cos_v, sin_v).astype(BF16)
                dk_ref[rows, sl] = _rot_t(dkr, cos_v, sin_v).astype(BF16)
                dv_ref[rows, sl] = dv.astype(BF16)

    const = lambda: pl.BlockSpec((HEADS, C, DH), lambda i: (0, 0, 0))
    state = lambda: pl.BlockSpec((RET_CHUNKS, HEADS, DH, DH), lambda i: (i, 0, 0, 0))
    tok = lambda: pl.BlockSpec((tm, RETW), lambda i: (i, 0))
    return _launch(
        body, (t // tm,),
        [pl.BlockSpec((tm, RETW), lambda i, cb=cb: (i, cb)) for cb in (1, 2, 3)] +
        [pl.BlockSpec((tm, DH), lambda i: (i, 0)), pl.BlockSpec((tm, DH), lambda i: (i, 0)), tok(),
         state(), state(), state(), state()] + [const() for _ in range(5)],
        [tok(), tok(), tok()],
        [S((t, RETW), BF16) for _ in range(3)], [],
        (proj, proj, proj, cos2, sin2, dO, s_f, s_b, g_f, g_b, rc["f"]["D"] + rc["b"]["D"],
         rc["f"]["XI"], rc["b"]["XI"], rc["f"]["ZETA"], rc["b"]["ZETA"]), name, ("parallel",), comm)


def _b_inproj(d_uv, dqkv, d_g, d_ag, w, x, g1, dxm, l, name, comm=None):
    t = x.shape[0]
    tm = 256

    def body(duv_ref, dq_ref, dk_ref, dv_ref, dg_ref, dag_ref, w_ref, x_ref, g_ref, dxm_ref,
             dp_ref, dx_ref, dn_ref):
        @pl.when(pl.program_id(0) == 0)
        def _():
            dn_ref[...] = jnp.zeros_like(dn_ref)

        for k, part in enumerate((duv_ref, dq_ref, dk_ref, dv_ref, dg_ref, dag_ref)):
            dp_ref[:, 512 * k:512 * (k + 1)] = part[...]
        dh = _dot_nt(dp_ref[...], w_ref[...])
        xv = x_ref[...]
        r = _rms(xv)
        dn_ref[...] += _col_sum(dh * xv * r)
        dx_ref[...] = dxm_ref[...] + _rmsnorm_bwd(dh, xv, r, g_ref[...])

    half = lambda: pl.BlockSpec((tm, 512), lambda i: (i, 0))
    full = lambda: pl.BlockSpec((tm, D), lambda i: (i, 0))
    return _launch(
        body, (t // tm,),
        [half() for _ in range(6)] +
        [_resident((D, INW), lambda i: (0, 0)), full(), pl.BlockSpec((None, 1, D), lambda i: (l, 0, 0)), full()],
        [pl.BlockSpec((tm, INW), lambda i: (i, 0)), full(), pl.BlockSpec((1, D), lambda i: (0, 0))],
        [S((t, INW), BF16), S((t, D), F32), S((1, D), F32)], [],
        (d_uv, *dqkv, d_g, d_ag, w, x, g1, dxm), name, ("arbitrary",), comm)


def _pair_exchange(gs, name):
    n = len(gs)

    def body(*refs):
        g, q, ssem, rsem = refs[:n], refs[n:2 * n], refs[2 * n], refs[2 * n + 1]
        x, y, c, _ = _place()
        copies = [pltpu.make_async_remote_copy(src_ref=g[u].at[2 * chip + (1 - c)], dst_ref=q[u].at[chip],
                                               send_sem=ssem.at[u, chip], recv_sem=rsem.at[u, chip],
                                               device_id=(x, y, 1 - c), device_id_type=MESH)
                  for u in range(n) for chip in range(4)]
        for cp in copies:
            cp.start()
        for cp in copies:
            cp.wait_recv()
        for cp in copies:
            cp.wait_send()

    return pl.pallas_call(body, name=name, in_specs=[_HBM] * n, out_specs=[_HBM] * n,
                          out_shape=[S((4,) + g.shape[1:], BF16) for g in gs],
                          scratch_shapes=[pltpu.SemaphoreType.DMA((n, 4)), pltpu.SemaphoreType.DMA((n, 4))],
                          compiler_params=_params())(*gs)


def _pair_add(g, q, name):
    _, mm, nn = g.shape
    bm = 256 if mm % 256 == 0 else mm

    def body(g_ref, q_ref, h_ref):
        h_ref[...] = (g_ref[lax.axis_index("c")].astype(F32) + q_ref[...].astype(F32)).astype(BF16)

    return pl.pallas_call(
        body, grid=(4, mm // bm), name=name,
        in_specs=[pl.BlockSpec((None, 2, bm, nn), lambda qq, i: (qq, 0, i, 0)),
                  pl.BlockSpec((None, bm, nn), lambda qq, i: (qq, i, 0))],
        out_specs=pl.BlockSpec((None, bm, nn), lambda qq, i: (qq, i, 0)),
        out_shape=S((4, mm, nn), BF16),
        compiler_params=_params("parallel", "parallel"),
    )(g.reshape(4, 2, mm, nn), q)


def _pair_reduce(named, l):
    names = [k for k, _ in named]
    qs = _pair_exchange([g for _, g in named], f"pair_exchange_{names[0]}_{l}")
    return [_pair_add(g, q, f"pair_add_{k}_{l}") for (k, g), q in zip(named, qs)]


_BIG = ("w_in", "w_out", "w_ffn_in", "w_ffn_out")
_KIND = dict(w_in="cols", w_out="rows", w_ffn_in="lead", w_ffn_out="rows")
CWP = 128
SMALL_ROWS = 152


def _step(x, tgt, wts, sh, cw_pad):
    t = x.shape[0]
    rc = _ret_consts()
    cos2, sin2 = (jnp.asarray(a) for a in _rope_tables(t))
    n1 = wts["norm1_g"].reshape(LAYERS, 1, D)
    n2 = wts["norm2_g"].reshape(LAYERS, 1, D)
    gm_lng = wts["gm_ln_g"].reshape(LAYERS, 1, GMW)
    gm_lnb = wts["gm_ln_b"].reshape(LAYERS, 1, GMW)
    ws_bf = wts["gm_ws"].astype(BF16)
    wst_bf = jnp.swapaxes(wts["gm_ws"], 2, 3).astype(BF16)
    bias = jnp.repeat(jnp.swapaxes(wts["gm_bs"], 1, 2), GMW // HEADS, axis=2)
    cb = wts["conv_b"].reshape(LAYERS, 1, CVW)
    cv_lng = wts["conv_ln_g"].reshape(LAYERS, 1, CVW)
    cv_lnb = wts["conv_ln_b"].reshape(LAYERS, 1, CVW)
    unit = lambda f, l: (sh[f][l], _KIND[f])
    cshard = CVW // NDEV

    full = {f: [None] * LAYERS for f in _BIG}
    full["w_in"][0], cw_all = _comm_only(_Gather([unit("w_in", 0), (cw_pad, "lead")]), "gather_first")
    cw = jnp.transpose(cw_all[:, :, :, :cshard], (1, 2, 0, 3)).reshape(LAYERS, 32, CVW)

    gcf, gcb = rc["f"]["gC"], rc["b"]["gC"]
    saved = []
    for l in range(LAYERS):
        first = l == 0
        (proj, h1), got = _f_inproj(x, n1, full["w_in"][l], cos2, sin2, l, f"f_inproj_{l}",
                                    _Gather([unit("w_ffn_in", 0)]) if first else None)
        if first:
            full["w_ffn_in"][0], = got
        y_gm = _f_gm(proj, gm_lng, gm_lnb, ws_bf, bias, l, f"f_gm_{l}")
        s_f, s_b = _scan_pair(proj, None, 2, rc["f"]["ZETA"], rc["b"]["ZETA"], gcf, gcb, True, f"f_ret_state_{l}")
        o = _f_ret_out(proj, s_f, s_b, rc, f"f_ret_out_{l}")
        (c, y_cv), got = _f_conv(proj, cw, cb, cv_lng, cv_lnb, l, f"f_conv_{l}",
                                 _Gather([unit("w_out", 0), unit("w_ffn_out", 0)]) if first else None)
        if first:
            full["w_out"][0], full["w_ffn_out"][0] = got
        xm, ycat = _f_mixout(x, y_gm, y_cv, o, proj, full["w_out"][l], f"f_mixout_{l}")
        (xo, h2, gu, act), got = _f_ffn(xm, n2, full["w_ffn_in"][l], full["w_ffn_out"][l], l, f"f_ffn_{l}",
                                        _Gather([unit(f, 1) for f in _BIG]) if first else None)
        if first:
            full["w_in"][1], full["w_out"][1], full["w_ffn_in"][1], full["w_ffn_out"][1] = got
        saved.append(dict(x=x, proj=proj, h1=h1, o=o, s_f=s_f, s_b=s_b, c=c, xm=xm, ycat=ycat, h2=h2, gu=gu, act=act))
        x = xo

    loss, dx, d_final = _b_loss(x, wts["final_g"].reshape(1, D), tgt, "b_loss")

    parts = {f: [None] * LAYERS for f in _BIG}
    small = [None] * LAYERS
    upper = None
    for l in reversed(range(LAYERS)):
        sv = saved[l]
        (dgu, dxm, dxo_bf, d_n2), got = _b_ffn(dx, sv["xm"], n2, sv["gu"], full["w_ffn_in"][l], full["w_ffn_out"][l], l,
                                               f"b_ffn_{l}", _Scatter(upper) if upper else None)
        if upper:
            for f, p in zip(_BIG, got):
                parts[f][l + 1] = p
        g_f2 = _mm_wgrad(sv["act"], dxo_bf, NDEV // 2, True, "shared", 1, f"g_ffn_out_{l}").reshape(NDEV, FFH // NDEV, D)
        g_f1 = _mm_wgrad(sv["h2"], dgu, NDEV, False, "lead", 1, f"g_ffn_in_{l}")
        dxm_bf, dy_gm, dO, d_g, dc, d_cvlg, d_cvlb, d_cb = _b_mixout(
            dxm, full["w_out"][l], sv["o"], sv["proj"], sv["c"], cv_lng, cv_lnb, l, f"b_mixout_{l}")
        g_out = _mm_wgrad(sv["ycat"], dxm_bf, 1, False, "shared", 1, f"g_out_{l}").reshape(NDEV, D // NDEV, D)
        last = l == 0
        early = _pair_reduce([("w_out", g_out), ("w_ffn_in", g_f1), ("w_ffn_out", g_f2)], l) if last else None
        d_uv, d_ws, _, d_bs_fold, d_gmlg, d_gmlb = _b_gm(sv["proj"], dy_gm, gm_lng, gm_lnb, ws_bf, wst_bf, bias, l, f"b_gm_{l}")
        (d_ag, d_cw), got = _b_conv(sv["proj"], dc, cw, l, f"b_conv_{l}", _Scatter(early[1:2]) if last else None)
        if last:
            parts["w_ffn_in"][l], = got
        g_f, g_b = _scan_pair(sv["proj"], dO, 1, rc["f"]["XI"], rc["b"]["XI"], gcf, gcb, False, f"b_ret_state_{l}")
        dqkv, got = _b_ret_out(sv["proj"], cos2, sin2, dO, sv["s_f"], sv["s_b"], g_f, g_b, rc, f"b_ret_out_{l}",
                               _Scatter([early[0], early[2]]) if last else None)
        if last:
            parts["w_out"][l], parts["w_ffn_out"][l] = got
        (dproj, dx, d_n1), _ = _b_inproj(d_uv, dqkv, d_g, d_ag, full["w_in"][l], sv["x"], n1, dxm, l, f"b_inproj_{l}")
        g_in = _mm_wgrad(sv["h1"], dproj, NDEV, False, "cols", 2, f"g_in_{l}")
        if last:
            tail = _pair_reduce([("w_in", g_in)], l)
        else:
            upper = _pair_reduce([("w_in", g_in), ("w_out", g_out), ("w_ffn_in", g_f1), ("w_ffn_out", g_f2)], l)
        small[l] = dict(norm1_g=d_n1[0], gm_ln_g=d_gmlg[0], gm_ln_b=d_gmlb[0], gm_ws=d_ws,
                        gm_bs=d_bs_fold[:, :HEADS].T, conv_w=d_cw[:KW], conv_b=d_cb[0], conv_ln_g=d_cvlg[0],
                        conv_ln_b=d_cvlb[0], norm2_g=d_n2[0])
    small_g = {k: jnp.stack([small[l][k] for l in range(LAYERS)]) for k in small[0]}
    small_g["final_g"] = d_final[0]
    small_buf = _pack([small_g[k] for k in _SMALL], SMALL_ROWS)
    parts["w_in"][0], small_parts = _comm_only(_Comms([_Scatter(tail), _Gather([(small_buf, "lead")])]), "exchange_last")
    return loss, dx, parts, small_parts


def _adamw(w, g, m, v):
    m = ADAM_B1 * m + (1.0 - ADAM_B1) * g
    v = ADAM_B2 * v + (1.0 - ADAM_B2) * (g * g)
    m_hat = m / (1.0 - ADAM_B1 ** ADAM_STEP)
    v_hat = v / (1.0 - ADAM_B2 ** ADAM_STEP)
    return -ADAM_LR * (m_hat / (jnp.sqrt(v_hat) + ADAM_EPS) + ADAM_WD * w), m, v


def _cast_blocks(ws):
    def body(*refs):
        ins, outs = refs[:len(ws)], refs[len(ws):]
        for k, src in enumerate(ins):
            for l in range(LAYERS):
                outs[k * LAYERS + l][...] = src[l].astype(BF16)

    outs = pl.pallas_call(body, name="cast_blocks", out_shape=[S(w.shape[1:], BF16) for w in ws for _ in range(LAYERS)],
                          compiler_params=_params())(*ws)
    return [list(outs[k * LAYERS:(k + 1) * LAYERS]) for k in range(len(ws))]


def _sum_adam(parts, w, m, v, l, prev, name):
    _, mm, nn = parts.shape
    bm = 256 if mm % 256 == 0 else mm

    def body(p_ref, w_ref, m_ref, v_ref, *rest):
        g_ref, d_ref, nm_ref, nv_ref = rest[-4:]
        g = p_ref[0].astype(F32)
        for s in range(1, 4):
            g = g + p_ref[s].astype(F32)
        g_ref[...] = g
        d_ref[...], nm_ref[...], nv_ref[...] = _adamw(w_ref[...], g, m_ref[...], v_ref[...])

    blk = lambda: pl.BlockSpec((None, bm, nn), lambda i: (l, i, 0))
    prev = list(prev) if prev else []
    return pl.pallas_call(
        body, grid=(mm // bm,), name=name,
        in_specs=[pl.BlockSpec((4, bm, nn), lambda i: (0, i, 0)), blk(), blk(), blk()] + [_ANY] * len(prev),
        out_specs=[blk() for _ in range(4)],
        out_shape=[S(w.shape, F32) for _ in range(4)],
        input_output_aliases={4 + j: j for j in range(len(prev))},
        compiler_params=_params("parallel"),
    )(parts, w, m, v, *prev)


def _sum_small(parts):
    def body(p_ref, o_ref):
        g = p_ref[0]
        for s in range(1, NDEV):
            g = g + p_ref[s]
        o_ref[...] = g

    return pl.pallas_call(body, name="sum_small", out_shape=S(parts.shape[1:], F32),
                          compiler_params=_params())(parts)


def _adam_small(g, w, m, v):
    def body(g_ref, w_ref, m_ref, v_ref, d_ref, nm_ref, nv_ref):
        d_ref[...], nm_ref[...], nv_ref[...] = _adamw(w_ref[...], g_ref[...], m_ref[...], v_ref[...])

    return pl.pallas_call(body, name="adam_small", out_shape=[S(g.shape, F32)] * 3, compiler_params=_params())(g, w, m, v)


_SMALL = ("norm1_g", "gm_ln_g", "gm_ln_b", "gm_ws", "gm_bs", "conv_w", "conv_b", "conv_ln_g", "conv_ln_b",
          "norm2_g", "final_g")
_NAMES = ("norm1_g", "w_in", "gm_ln_g", "gm_ln_b", "gm_ws", "gm_bs", "conv_w", "conv_b", "conv_ln_g", "conv_ln_b",
          "w_out", "norm2_g", "w_ffn_in", "w_ffn_out", "final_g")


def _pack(parts, rows):
    flat = jnp.concatenate([p.reshape(-1) for p in parts])
    return jnp.pad(flat, (0, rows * 1024 - flat.shape[0])).reshape(rows, 1024)


def _unpack(buf, shapes):
    flat = buf.reshape(-1)
    out, o = [], 0
    for shp in shapes:
        sz = int(np.prod(shp))
        out.append(flat[o:o + sz].reshape(shp))
        o += sz
    return out


def kernel(x, norm1_g, w_in, gm_ln_g, gm_ln_b, gm_ws, gm_bs, conv_w, conv_b, conv_ln_g, conv_ln_b, w_out, norm2_g, w_ffn_in, w_ffn_out, final_g, loss_target, m_norm1_g, m_w_in, m_gm_ln_g, m_gm_ln_b, m_gm_ws, m_gm_bs, m_conv_w, m_conv_b, m_conv_ln_g, m_conv_ln_b, m_w_out, m_norm2_g, m_w_ffn_in, m_w_ffn_out, m_final_g, v_norm1_g, v_w_in, v_gm_ln_g, v_gm_ln_b, v_gm_ws, v_gm_bs, v_conv_w, v_conv_b, v_conv_ln_g, v_conv_ln_b, v_w_out, v_norm2_g, v_w_ffn_in, v_w_ffn_out, v_final_g):
    w = dict(norm1_g=norm1_g, w_in=w_in, gm_ln_g=gm_ln_g, gm_ln_b=gm_ln_b, gm_ws=gm_ws, gm_bs=gm_bs, conv_w=conv_w,
             conv_b=conv_b, conv_ln_g=conv_ln_g, conv_ln_b=conv_ln_b, w_out=w_out, norm2_g=norm2_g, w_ffn_in=w_ffn_in,
             w_ffn_out=w_ffn_out, final_g=final_g)
    mo = dict(norm1_g=m_norm1_g, w_in=m_w_in, gm_ln_g=m_gm_ln_g, gm_ln_b=m_gm_ln_b, gm_ws=m_gm_ws, gm_bs=m_gm_bs,
              conv_w=m_conv_w, conv_b=m_conv_b, conv_ln_g=m_conv_ln_g, conv_ln_b=m_conv_ln_b, w_out=m_w_out,
              norm2_g=m_norm2_g, w_ffn_in=m_w_ffn_in, w_ffn_out=m_w_ffn_out, final_g=m_final_g)
    vo = dict(norm1_g=v_norm1_g, w_in=v_w_in, gm_ln_g=v_gm_ln_g, gm_ln_b=v_gm_ln_b, gm_ws=v_gm_ws, gm_bs=v_gm_bs,
              conv_w=v_conv_w, conv_b=v_conv_b, conv_ln_g=v_conv_ln_g, conv_ln_b=v_conv_ln_b, w_out=v_w_out,
              norm2_g=v_norm2_g, w_ffn_in=v_w_ffn_in, w_ffn_out=v_w_ffn_out, final_g=v_final_g)
    t = x.shape[1]
    me = 4 * lax.axis_index("x") + 2 * lax.axis_index("y") + lax.axis_index("c")
    cshard = conv_w.shape[2]

    cw_pad = jnp.pad(conv_w, ((0, 0), (0, 32 - KW), (0, CWP - cshard)))
    sh = dict(zip(_BIG, _cast_blocks([w[f] for f in _BIG])))
    loss, dx, parts, small_parts = _step(x.reshape(t, D), loss_target.reshape(t, D), w, sh, cw_pad)

    grads, delta, new_m, new_v = {}, {}, {}, {}
    for f in _BIG:
        outs = None
        for l in reversed(range(LAYERS)):
            outs = _sum_adam(parts[f][l], w[f], mo[f], vo[f], l, outs, f"sum_adam_{f}_{l}")
        grads[f], delta[f], new_m[f], new_v[f] = outs

    full_shapes = [(LAYERS, KW, CVW) if k == "conv_w" else w[k].shape for k in _SMALL]
    for k, g in zip(_SMALL, _unpack(_sum_small(small_parts), full_shapes)):
        grads[k] = lax.dynamic_slice_in_dim(g, me * cshard, cshard, axis=2) if k == "conv_w" else g
    adam_rows = 144
    d_s, m_s, v_s = _adam_small(_pack([grads[k] for k in _SMALL], adam_rows), _pack([w[k] for k in _SMALL], adam_rows),
                                _pack([mo[k] for k in _SMALL], adam_rows), _pack([vo[k] for k in _SMALL], adam_rows))
    shapes = [w[k].shape for k in _SMALL]
    for dst, buf in ((delta, d_s), (new_m, m_s), (new_v, v_s)):
        for k, a in zip(_SMALL, _unpack(buf, shapes)):
            dst[k] = a

    total = lax.psum(loss[0, 0], ("x", "y", "c"))
    return (total, dx.reshape(1, t, D), *[grads[k] for k in _NAMES], *[delta[k] for k in _NAMES],
            *[new_m[k] for k in _NAMES], *[new_v[k] for k in _NAMES])
```

```python
import functools

import numpy as np
import jax
import jax.numpy as jnp
from jax import lax
from jax.experimental import pallas as pl
from jax.experimental.pallas import tpu as pltpu

F32, BF16 = jnp.float32, jnp.bfloat16
S = jax.ShapeDtypeStruct

D = 1024
INW = 3072
GMW = 256
RETW = 512
CVW = 256
HEADS = 4
DH = 128
C = 128
KW = 31
HALO = 16
FFH = 2816
NDEV = 8
FFB = 2 * FFH // NDEV
EPS = 1e-6
LAYERS = 2
SCALE = DH ** -0.5
VMEM_LIMIT = 56 * 1024 * 1024

ADAM_LR, ADAM_B1, ADAM_B2, ADAM_EPS, ADAM_WD, ADAM_STEP = 0.001, 0.9, 0.999, 1e-08, 0.01, 10

_SQRT_HALF = 0.7071067811865476
_INV_SQRT_2PI = 0.3989422804014327


def _params(*sem):
    return pltpu.CompilerParams(dimension_semantics=sem or None, vmem_limit_bytes=VMEM_LIMIT)


def _resident(shape, index_map):
    return pl.BlockSpec(shape, index_map, pipeline_mode=pl.Buffered(1))


def _dot(a, b):
    return jnp.dot(a, b, preferred_element_type=F32)


def _dot_nt(a, b):
    return lax.dot_general(a, b, (((1,), (1,)), ((), ())), preferred_element_type=F32)


def _dot_tn(a, b):
    return lax.dot_general(a, b, (((0,), (0,)), ((), ())), preferred_element_type=F32)


def _sigmoid(x):
    return 1.0 / (1.0 + jnp.exp(-x))


def _gelu_and_grad(x):
    cdf = 0.5 * (1.0 + lax.erf(x * _SQRT_HALF))
    return x * cdf, cdf + x * jnp.exp(-0.5 * x * x) * _INV_SQRT_2PI


def _silu_and_grad(x):
    s = _sigmoid(x)
    return x * s, s * (1.0 + x * (1.0 - s))


def _standardize(x):
    mu = jnp.mean(x, axis=-1, keepdims=True)
    d = x - mu
    rstd = lax.rsqrt(jnp.mean(d * d, axis=-1, keepdims=True) + EPS)
    return d * rstd, rstd


def _standardize_bwd(dxhat, xhat, rstd):
    m1 = jnp.mean(dxhat, axis=-1, keepdims=True)
    m2 = jnp.mean(dxhat * xhat, axis=-1, keepdims=True)
    return rstd * (dxhat - m1 - xhat * m2)


def _rms(x):
    return lax.rsqrt(jnp.mean(x * x, axis=-1, keepdims=True) + EPS)


def _rmsnorm_bwd(dy, x, r, g):
    u = dy * g
    return r * u - x * (r * r * r) * jnp.mean(u * x, axis=-1, keepdims=True)


def _col_sum(a):
    return jnp.sum(a, axis=0, keepdims=True)


def _rot(t, cos2, sin2):
    return t * cos2 + pltpu.roll(t, DH // 2, axis=1) * sin2


def _rot_t(dt, cos2, sin2):
    return dt * cos2 + pltpu.roll(dt * sin2, DH // 2, axis=1)


MESH = pl.DeviceIdType.MESH
_HBM = pl.BlockSpec(memory_space=pltpu.HBM)
_ANY = pl.BlockSpec(memory_space=pl.ANY)


def _place():
    x, y, c = lax.axis_index("x"), lax.axis_index("y"), lax.axis_index("c")
    return x, y, c, ((1 - x, y), (x, 1 - y), (1 - x, 1 - y))


def _slot(full, kind, width, i):
    if kind == "cols":
        return full.at[:, pl.ds(pl.multiple_of(i * width, 128), width)]
    if kind == "rows":
        return full.at[pl.ds(pl.multiple_of(i * width, 16), width), :]
    return full.at[i]


class _Gather:
    def __init__(self, units):
        self.units = units
        self.inputs = [u[0] for u in units]
        self.out_shape = []
        for src, kind in units:
            r, c = src.shape[-2:]
            shape = {"cols": (r, NDEV * c), "rows": (NDEV * r, c), "lead": (NDEV,) + src.shape}[kind]
            self.out_shape.append(S(shape, src.dtype))
        n = len(units)
        self.scratch = [pltpu.SemaphoreType.DMA((n, 7)), pltpu.SemaphoreType.DMA((n, 7)), pltpu.SemaphoreType.DMA((n,))]

    def run(self, phase, ins, outs, scr):
        ssem, rsem, lsem = scr
        x, y, c, chips = _place()
        me, sib = 4 * x + 2 * y + c, (x, y, 1 - c)
        idx = lambda chip, core: 4 * chip[0] + 2 * chip[1] + core
        for u, (src_arr, kind) in enumerate(self.units):
            src, full = ins[u], outs[u]
            width = src_arr.shape[-1] if kind == "cols" else src_arr.shape[-2]
            slot = functools.partial(_slot, full, kind, width)

            def copy(k, block, to, from_src=False):
                return pltpu.make_async_remote_copy(src_ref=src if from_src else slot(block), dst_ref=slot(block),
                                                    send_sem=ssem.at[u, k], recv_sem=rsem.at[u, k],
                                                    device_id=to, device_id_type=MESH)

            mine = lambda: pltpu.make_async_copy(src, slot(me), lsem.at[u])
            first = lambda: [copy(0, me, sib, True)] + [copy(1 + j, me, (*chip, c), True) for j, chip in enumerate(chips)]
            passed = lambda j: copy(4 + j, idx(chips[j], c), sib)
            if phase == "start":
                mine().start()
                for cp in first():
                    cp.start()
            elif phase == "forward":
                for j, chip in enumerate(chips):
                    copy(1 + j, idx(chip, c), sib).wait_recv()
                    passed(j).start()
            else:
                copy(0, idx((x, y), 1 - c), sib).wait_recv()
                for j, chip in enumerate(chips):
                    copy(4 + j, idx(chip, 1 - c), sib).wait_recv()
                for cp in first() + [passed(j) for j in range(3)]:
                    cp.wait_send()
                mine().wait()


class _Scatter:
    def __init__(self, units):
        self.units = units
        self.inputs = list(units)
        self.out_shape = [S(u.shape, u.dtype) for u in units]
        n = len(units)
        self.scratch = [pltpu.SemaphoreType.DMA((n, 3)), pltpu.SemaphoreType.DMA((n, 3)), pltpu.SemaphoreType.DMA((n,))]

    def run(self, phase, ins, outs, scr):
        ssem, rsem, lsem = scr
        x, y, c, chips = _place()
        myq = 2 * x + y
        for u in range(len(self.units)):
            h, p = ins[u], outs[u]

            def copy(k, chip, send_to_them):
                q = 2 * chip[0] + chip[1]
                return pltpu.make_async_remote_copy(src_ref=h.at[q], dst_ref=p.at[myq if send_to_them else q],
                                                    send_sem=ssem.at[u, k], recv_sem=rsem.at[u, k],
                                                    device_id=(*chip, c), device_id_type=MESH)

            mine = lambda: pltpu.make_async_copy(h.at[myq], p.at[myq], lsem.at[u])
            sends = lambda: [copy(k, chip, True) for k, chip in enumerate(chips)]
            if phase == "start":
                mine().start()
                for cp in sends():
                    cp.start()
            elif phase == "finish":
                for k, chip in enumerate(chips):
                    copy(k, chip, False).wait_recv()
                for cp in sends():
                    cp.wait_send()
                mine().wait()


class _Comms:
    def __init__(self, parts):
        self.parts = parts
        self.inputs = [a for p in parts for a in p.inputs]
        self.out_shape = [a for p in parts for a in p.out_shape]
        self.scratch = [a for p in parts for a in p.scratch]

    def run(self, phase, ins, outs, scr):
        i = o = s = 0
        for p in self.parts:
            ni, no, ns = len(p.inputs), len(p.out_shape), len(p.scratch)
            p.run(phase, ins[i:i + ni], outs[o:o + no], scr[s:s + ns])
            i, o, s = i + ni, o + no, s + ns


def _launch(body, grid, in_specs, out_specs, out_shape, scratch, args, name, sem, comm=None):
    if comm is None:
        outs = pl.pallas_call(body, grid=grid, name=name, in_specs=in_specs, out_specs=out_specs, out_shape=out_shape,
                              scratch_shapes=scratch, compiler_params=_params(*sem))(*args)
        return list(outs), []
    n_in, n_out, n_scr = len(args), len(out_shape), len(scratch)
    ci, co = len(comm.inputs), len(comm.out_shape)
    nsteps = grid[0]
    fwd_step = (7 * nsteps) // 8

    def hosted(*refs):
        a = refs[:n_in]
        ca = refs[n_in:n_in + ci]
        o = refs[n_in + ci:n_in + ci + n_out]
        cout = refs[n_in + ci + n_out:n_in + ci + n_out + co]
        s = refs[n_in + ci + n_out + co:n_in + ci + n_out + co + n_scr]
        cs = refs[n_in + ci + n_out + co + n_scr:]
        step = pl.program_id(0)

        @pl.when(step == 0)
        def _():
            comm.run("start", ca, cout, cs)

        body(*a, *o, *s)

        @pl.when(step == fwd_step)
        def _():
            comm.run("forward", ca, cout, cs)

        @pl.when(step == nsteps - 1)
        def _():
            comm.run("finish", ca, cout, cs)

    outs = pl.pallas_call(
        hosted, grid=grid, name=name, in_specs=list(in_specs) + [_HBM] * ci, out_specs=list(out_specs) + [_HBM] * co,
        out_shape=list(out_shape) + comm.out_shape, scratch_shapes=list(scratch) + comm.scratch,
        compiler_params=_params(*["arbitrary"] * len(grid)))(*args, *comm.inputs)
    return list(outs[:n_out]), list(outs[n_out:])


def _comm_only(comm, name):
    ci, co = len(comm.inputs), len(comm.out_shape)

    def body(*refs):
        ca, cout, cs = refs[:ci], refs[ci:ci + co], refs[ci + co:]
        for phase in ("start", "forward", "finish"):
            comm.run(phase, ca, cout, cs)

    return pl.pallas_call(body, name=name, in_specs=[_HBM] * ci, out_specs=[_HBM] * co, out_shape=comm.out_shape,
                          scratch_shapes=comm.scratch, compiler_params=_params())(*comm.inputs)


def _ret_consts():
    idx = np.arange(C, dtype=np.float32)
    gf = (1.0 - np.exp2(-5.0 - np.arange(HEADS, dtype=np.float32))).astype(np.float32)
    out = {}
    for name, gamma, fwd in (("f", gf, True), ("b", gf[::-1].copy(), False)):
        lg = np.log(gamma).astype(np.float32)[:, None]
        diff = idx[:, None] - idx[None, :]
        if fwd:
            mask = diff >= 0
            dist = np.where(mask, diff, 0.0)
            zeta = np.exp(lg * (C - 1 - idx))
            xi = np.exp(lg * (idx + 1))
        else:
            mask = diff < 0
            dist = np.where(mask, -diff, 0.0)
            zeta = np.exp(lg * idx)
            xi = np.exp(lg * (C - idx))
        dm = np.where(mask[None], np.exp(lg[:, :, None] * dist[None]), 0.0).astype(np.float32)
        bc = lambda vec: np.ascontiguousarray(np.broadcast_to(vec.astype(np.float32)[:, :, None], (HEADS, C, DH)))
        out[name] = dict(D=dm, XI=bc(xi), ZETA=bc(zeta), gC=[float(v) for v in np.exp(lg[:, 0] * C).astype(np.float32)])
    return out


def _rope_tables(t):
    half = DH // 2
    inv_freq = (np.float32(10000.0) ** (-np.arange(half, dtype=np.float32) / np.float32(half))).astype(np.float32)
    ang = (np.arange(t, dtype=np.float32)[:, None] * inv_freq[None, :]).astype(np.float64)
    cos, sin = np.cos(ang).astype(np.float32), np.sin(ang).astype(np.float32)
    return np.concatenate([cos, cos], axis=1), np.concatenate([-sin, sin], axis=1)


def _f_inproj(x, g1, w, cos2, sin2, l, name, comm=None):
    t = x.shape[0]
    tm = 512

    def body(x_ref, g_ref, w_ref, cos_ref, sin_ref, proj_ref, ht_ref):
        xv = x_ref[...]
        h = (xv * _rms(xv) * g_ref[...]).astype(BF16)
        ht_ref[...] = h.T
        for nb in range(INW // 512):
            cs = slice(nb * 512, (nb + 1) * 512)
            res = _dot(h, w_ref[:, cs])
            if nb in (1, 2):
                for hh in range(HEADS):
                    r = _rot(res[:, hh * DH:(hh + 1) * DH], cos_ref[...], sin_ref[...])
                    proj_ref[:, nb * 512 + hh * DH:nb * 512 + (hh + 1) * DH] = (r * SCALE if nb == 2 else r).astype(BF16)
            else:
                proj_ref[:, cs] = res.astype(BF16)

    return _launch(
        body, (t // tm,),
        [pl.BlockSpec((tm, D), lambda i: (i, 0)),
         pl.BlockSpec((None, 1, D), lambda i: (l, 0, 0)),
         _resident((D, INW), lambda i: (0, 0)),
         pl.BlockSpec((tm, DH), lambda i: (i, 0)), pl.BlockSpec((tm, DH), lambda i: (i, 0))],
        [pl.BlockSpec((tm, INW), lambda i: (i, 0)), pl.BlockSpec((D, tm), lambda i: (0, i))],
        [S((t, INW), BF16), S((D, t), BF16)], [], (x, g1, w, cos2, sin2), name, ("parallel",), comm)


def _gm_chunk_fwd(u, v, lng, lnb, ws_ref, bias):
    au, dau = _gelu_and_grad(u)
    av, dav = _gelu_and_grad(v)
    vhat, rstd = _standardize(av)
    vn = (vhat * lng + lnb).astype(BF16)
    head = lax.broadcasted_iota(jnp.int32, (C, GMW), 1) // (GMW // HEADS)
    mixed = bias
    for h in range(HEADS):
        mixed = mixed + jnp.where(head == h, _dot(ws_ref[h], vn), 0.0)
    return au, dau, dav, vhat, rstd, vn, mixed, head


def _f_gm(proj, lng, lnb, ws_bf, bias, l, name):
    t = proj.shape[0]
    tm = 512

    def body(p_ref, lng_ref, lnb_ref, ws_ref, bias_ref, y_ref):
        for ci in range(tm // C):
            rows = slice(ci * C, (ci + 1) * C)
            u = p_ref[rows, 0:GMW].astype(F32)
            v = p_ref[rows, GMW:2 * GMW].astype(F32)
            au, _, _, _, _, _, mixed, _ = _gm_chunk_fwd(u, v, lng_ref[...], lnb_ref[...], ws_ref, bias_ref[...])
            y_ref[rows, :] = (au * mixed).astype(BF16)

    return pl.pallas_call(
        body, grid=(t // tm,), name=name,
        in_specs=[pl.BlockSpec((tm, 2 * GMW), lambda i: (i, 0)),
                  pl.BlockSpec((None, 1, GMW), lambda i: (l, 0, 0)),
                  pl.BlockSpec((None, 1, GMW), lambda i: (l, 0, 0)),
                  pl.BlockSpec((None, HEADS, C, C), lambda i: (l, 0, 0, 0)),
                  pl.BlockSpec((None, C, GMW), lambda i: (l, 0, 0))],
        out_specs=pl.BlockSpec((tm, GMW), lambda i: (i, 0)),
        out_shape=S((t, GMW), BF16),
        compiler_params=_params("parallel"),
    )(proj, lng, lnb, ws_bf, bias)


def _scan_pair(proj, other, col, wf, wb, gcf, gcb, first_is_f, name):
    t = proj.shape[0]
    n = t // C
    sc = SCAN_CHUNKS
    nsteps = n // sc
    other_is_proj = other is None

    def body(a1, o1, a2, o2, w1_ref, w2_ref, out1, out2, st1, st2):
        @pl.when(pl.program_id(0) == 0)
        def _():
            st1[...] = jnp.zeros_like(st1)
            st2[...] = jnp.zeros_like(st2)

        def one(a_ref, o_ref, w_ref, gc, st, out, order):
            for h in range(HEADS):
                sl = slice(h * DH, (h + 1) * DH)
                incs = {}
                for j in order:
                    rows = slice(j * C, (j + 1) * C)
                    aw = (a_ref[rows, sl].astype(F32) * w_ref[h]).astype(BF16)
                    incs[j] = _dot_tn(aw, o_ref[rows, sl].astype(BF16))
                cur = st[h]
                for j in order:
                    out[j, h] = cur.astype(BF16)
                    cur = gc[h] * cur + incs[j]
                st[h] = cur

        g1, g2 = (gcf, gcb) if first_is_f else (gcb, gcf)
        one(a1, o1, w1_ref, g1, st1, out1, range(sc))
        one(a2, o2, w2_ref, g2, st2, out2, range(sc - 1, -1, -1))

    up = lambda i: i
    down = lambda i: nsteps - 1 - i

    def specs(ix):
        o_spec = pl.BlockSpec((sc * C, RETW), lambda i: (ix(i), 3 if other_is_proj else 0))
        return [pl.BlockSpec((sc * C, RETW), lambda i: (ix(i), col)), o_spec]

    const = lambda: pl.BlockSpec((HEADS, C, DH), lambda i: (0, 0, 0))
    oth = proj if other_is_proj else other
    w1, w2 = (wf, wb) if first_is_f else (wb, wf)
    out1, out2 = pl.pallas_call(
        body, grid=(nsteps,), name=name,
        in_specs=specs(up) + specs(down) + [const(), const()],
        out_specs=[pl.BlockSpec((sc, HEADS, DH, DH), lambda i: (up(i), 0, 0, 0)),
                   pl.BlockSpec((sc, HEADS, DH, DH), lambda i: (down(i), 0, 0, 0))],
        out_shape=[S((n, HEADS, DH, DH), BF16), S((n, HEADS, DH, DH), BF16)],
        scratch_shapes=[pltpu.VMEM((HEADS, DH, DH), F32), pltpu.VMEM((HEADS, DH, DH), F32)],
        compiler_params=_params("arbitrary"),
    )(proj, oth, proj, oth, w1, w2)
    return (out1, out2) if first_is_f else (out2, out1)


SCAN_CHUNKS = 4


RET_CHUNKS = 4


def _f_ret_out(proj, s_f, s_b, rc, name):
    t = proj.shape[0]
    tm = RET_CHUNKS * C

    def body(q_ref, k_ref, v_ref, sf_ref, sb_ref, d_ref, xif_ref, xib_ref, o_ref):
        for ci in range(RET_CHUNKS):
            rows = slice(ci * C, (ci + 1) * C)
            for h in range(HEADS):
                sl = slice(h * DH, (h + 1) * DH)
                qh, kh, vh = q_ref[rows, sl], k_ref[rows, sl], v_ref[rows, sl]
                p = (_dot_nt(qh, kh) * d_ref[h]).astype(BF16)
                cross = _dot(qh, jnp.concatenate([sf_ref[ci, h], sb_ref[ci, h]], axis=1))
                o_ref[rows, sl] = _dot(p, vh) + xif_ref[h] * cross[:, 0:DH] + xib_ref[h] * cross[:, DH:2 * DH]

    const = lambda: pl.BlockSpec((HEADS, C, DH), lambda i: (0, 0, 0))
    state = lambda: pl.BlockSpec((RET_CHUNKS, HEADS, DH, DH), lambda i: (i, 0, 0, 0))
    return pl.pallas_call(
        body, grid=(t // tm,), name=name,
        in_specs=[pl.BlockSpec((tm, RETW), lambda i, cb=cb: (i, cb)) for cb in (1, 2, 3)] +
                 [state(), state(), const(), const(), const()],
        out_specs=pl.BlockSpec((tm, RETW), lambda i: (i, 0)),
        out_shape=S((t, RETW), F32),
        compiler_params=_params("parallel"),
    )(proj, proj, proj, s_f, s_b, rc["f"]["D"] + rc["b"]["D"], rc["f"]["XI"], rc["b"]["XI"])


def _conv_halo_specs(t, tm, width, col):
    r = tm // HALO
    last = t // HALO - 1
    return [pl.BlockSpec((HALO, width), lambda i: (jnp.maximum(i * r - 1, 0), col)),
            pl.BlockSpec((tm, width), lambda i: (i, col)),
            pl.BlockSpec((HALO, width), lambda i: (jnp.minimum((i + 1) * r, last), col))]


def _fill_ext(ext, prev, cur, nxt, i, nt, tm):
    ext[0:HALO, :] = jnp.where(i > 0, prev, 0.0)
    ext[HALO:HALO + tm, :] = cur
    ext[HALO + tm:2 * HALO + tm, :] = jnp.where(i < nt - 1, nxt, 0.0)


def _glu(a_ref, g_ref):
    return a_ref[...].astype(F32) * _sigmoid(g_ref[...].astype(F32))


def _shifted_copies(ext, rot, tm):
    rows = tm + 2 * HALO - 8
    for b in range(1, 8):
        rot[b - 1, :, :] = ext[pl.ds(b, rows), :]


def _tap(ext, rot, r0, s, rb):
    a, b = divmod(s, 8)
    return ext[pl.ds(r0 + 8 * a, rb), :] if b == 0 else rot[b - 1, pl.ds(r0 + 8 * a, rb), :]


def _f_conv(proj, cw, cb, lng, lnb, l, name, comm=None):
    t = proj.shape[0]
    tm = 256
    nt = t // tm
    rb = 64

    def body(ap, ac, an, gp, gc, gn, cw_ref, cb_ref, lng_ref, lnb_ref, c_ref, y_ref, hext, hrot):
        i = pl.program_id(0)
        _fill_ext(hext, _glu(ap, gp), _glu(ac, gc), _glu(an, gn), i, nt, tm)
        _shifted_copies(hext, hrot, tm)
        for r0 in range(0, tm, rb):
            acc = jnp.zeros((rb, CVW), F32) + cb_ref[...]
            for j in range(KW):
                acc = acc + cw_ref[j:j + 1, :] * _tap(hext, hrot, r0, j + 1, rb)
            c_ref[r0:r0 + rb, :] = acc
            chat, _ = _standardize(acc)
            z = chat * lng_ref[...] + lnb_ref[...]
            y_ref[r0:r0 + rb, :] = (z * _sigmoid(z)).astype(BF16)

    vec = lambda: pl.BlockSpec((None, 1, CVW), lambda i: (l, 0, 0))
    return _launch(
        body, (nt,),
        _conv_halo_specs(t, tm, CVW, 10) + _conv_halo_specs(t, tm, CVW, 11) +
        [pl.BlockSpec((None, 32, CVW), lambda i: (l, 0, 0)), vec(), vec(), vec()],
        [pl.BlockSpec((tm, CVW), lambda i: (i, 0)), pl.BlockSpec((tm, CVW), lambda i: (i, 0))],
        [S((t, CVW), F32), S((t, CVW), BF16)],
        [pltpu.VMEM((tm + 2 * HALO, CVW), F32), pltpu.VMEM((7, tm + 2 * HALO - 8, CVW), F32)],
        (proj, proj, proj, proj, proj, proj, cw, cb, lng, lnb), name, ("parallel",), comm)


def _f_mixout(x, y_gm, y_cv, o, proj, w, name):
    t = x.shape[0]
    tm = 512

    def body(x_ref, ygm_ref, ycv_ref, o_ref, g_ref, w_ref, xm_ref, ycat_t_ref, ycat):
        ycat[:, 0:GMW] = ygm_ref[...]
        ycat[:, GMW + RETW:D] = ycv_ref[...]
        for h in range(HEADS):
            sl = slice(h * DH, (h + 1) * DH)
            ohat, _ = _standardize(o_ref[:, sl])
            g = g_ref[:, sl].astype(F32)
            ycat[:, GMW + h * DH:GMW + (h + 1) * DH] = (ohat * (g * _sigmoid(g))).astype(BF16)
        yc = ycat[...]
        ycat_t_ref[...] = yc.T
        xm_ref[...] = x_ref[...] + _dot(yc, w_ref[...])

    return pl.pallas_call(
        body, grid=(t // tm,), name=name,
        in_specs=[pl.BlockSpec((tm, D), lambda i: (i, 0)),
                  pl.BlockSpec((tm, GMW), lambda i: (i, 0)),
                  pl.BlockSpec((tm, CVW), lambda i: (i, 0)),
                  pl.BlockSpec((tm, RETW), lambda i: (i, 0)),
                  pl.BlockSpec((tm, RETW), lambda i: (i, 4)),
                  _resident((D, D), lambda i: (0, 0))],
        out_specs=[pl.BlockSpec((tm, D), lambda i: (i, 0)), pl.BlockSpec((D, tm), lambda i: (0, i))],
        out_shape=[S((t, D), F32), S((D, t), BF16)],
        scratch_shapes=[pltpu.VMEM((tm, D), BF16)],
        compiler_params=_params("parallel"),
    )(x, y_gm, y_cv, o, proj, w)


def _f_ffn(xm, g2, w1, w2, l, name, comm=None):
    t = xm.shape[0]
    tm = 512
    half = NDEV // 2

    def body(x_ref, g_ref, w1_ref, w2_ref, xo_ref, h_ref, gu_ref, act_t_ref):
        xv = x_ref[...]
        h = (xv * _rms(xv) * g_ref[...]).astype(BF16)
        h_ref[...] = h
        acc = xv
        for j in range(half):
            gate = _dot_nt(h, w1_ref[j])
            up = _dot_nt(h, w1_ref[half + j])
            gu_ref[j] = gate.astype(BF16)
            gu_ref[half + j] = up.astype(BF16)
            a = ((gate * _sigmoid(gate)) * up).astype(BF16)
            act_t_ref[j] = a.T
            acc = acc + _dot(a, w2_ref[j * FFB:(j + 1) * FFB, :])
        xo_ref[...] = acc

    return _launch(
        body, (t // tm,),
        [pl.BlockSpec((tm, D), lambda i: (i, 0)),
         pl.BlockSpec((None, 1, D), lambda i: (l, 0, 0)),
         _resident((NDEV, FFB, D), lambda i: (0, 0, 0)),
         _resident((FFH, D), lambda i: (0, 0))],
        [pl.BlockSpec((tm, D), lambda i: (i, 0)), pl.BlockSpec((tm, D), lambda i: (i, 0)),
         pl.BlockSpec((NDEV, tm, FFB), lambda i: (0, i, 0)), pl.BlockSpec((half, FFB, tm), lambda i: (0, 0, i))],
        [S((t, D), F32), S((t, D), BF16), S((NDEV, t, FFB), BF16), S((half, FFB, t), BF16)],
        [], (xm, g2, w1, w2), name, ("parallel",), comm)


def _b_ffn(top, xm, g2, gu, w1, w2, l, name, comm=None):
    t = xm.shape[0]
    tm = 256
    half = NDEV // 2
    from_loss = isinstance(top, tuple)
    n_top = 3 if from_loss else 1

    def body(*refs):
        top_refs = refs[:n_top]
        x_ref, g_ref, gu_ref, w1_ref, w2_ref, dgu_ref, dxm_ref, dxb_ref, dg_ref = refs[n_top:n_top + 9]
        first = pl.program_id(0) == 0

        @pl.when(first)
        def _():
            dg_ref[...] = jnp.zeros_like(dg_ref)

        if from_loss:
            xo_ref, fg_ref, t_ref = top_refs
            loss_ref, dfg_ref = refs[n_top + 9:]

            @pl.when(first)
            def _():
                loss_ref[...] = jnp.zeros_like(loss_ref)
                dfg_ref[...] = jnp.zeros_like(dfg_ref)

            xo = xo_ref[...]
            ro = _rms(xo)
            xr = xo * ro
            err = xr * fg_ref[...] - t_ref[...]
            loss_ref[...] += (0.5 / D) * _col_sum(jnp.sum(err * err, axis=1, keepdims=True))
            dy = err * (1.0 / D)
            dfg_ref[...] += _col_sum(dy * xr)
            dxo = _rmsnorm_bwd(dy, xo, ro, fg_ref[...])
        else:
            dxo = top_refs[0][...]
        dxb = dxo.astype(BF16)
        dxb_ref[...] = dxb
        dh = jnp.zeros((tm, D), F32)
        for j in range(half):
            dact = _dot_nt(dxb, w2_ref[j * FFB:(j + 1) * FFB, :])
            gate = gu_ref[j].astype(F32)
            up = gu_ref[half + j].astype(F32)
            sg, dsg = _silu_and_grad(gate)
            dgate = (dact * up * dsg).astype(BF16)
            dup = (dact * sg).astype(BF16)
            dgu_ref[j] = dgate.T
            dgu_ref[half + j] = dup.T
            dh = dh + _dot(dgate, w1_ref[j]) + _dot(dup, w1_ref[half + j])
        xv = x_ref[...]
        r = _rms(xv)
        dg_ref[...] += _col_sum(dh * xv * r)
        dxm_ref[...] = dxo + _rmsnorm_bwd(dh, xv, r, g_ref[...])

    tok = lambda: pl.BlockSpec((tm, D), lambda i: (i, 0))
    vec = lambda: pl.BlockSpec((1, D), lambda i: (0, 0))
    top_specs = [tok(), vec(), tok()] if from_loss else [tok()]
    extra_specs = [pl.BlockSpec((1, 1), lambda i: (0, 0)), vec()] if from_loss else []
    extra_shape = [S((1, 1), F32), S((1, D), F32)] if from_loss else []
    return _launch(
        body, (t // tm,),
        top_specs + [tok(), pl.BlockSpec((None, 1, D), lambda i: (l, 0, 0)),
                     pl.BlockSpec((NDEV, tm, FFB), lambda i: (0, i, 0)),
                     _resident((NDEV, FFB, D), lambda i: (0, 0, 0)),
                     _resident((FFH, D), lambda i: (0, 0))],
        [pl.BlockSpec((NDEV, FFB, tm), lambda i: (0, 0, i)), tok(), tok(), vec()] + extra_specs,
        [S((NDEV, FFB, t), BF16), S((t, D), F32), S((t, D), BF16), S((1, D), F32)] + extra_shape,
        [], ((*top,) if from_loss else (top,)) + (xm, g2, gu, w1, w2), name, ("arbitrary",), comm)


def _mm_wgrad(at, b, pieces, at_lead, b_mode, group, name):
    bt = 1024
    t = at.shape[-1]
    bt = min(bt, t)
    nt = t // bt
    ka = at.shape[-2]
    if at_lead:
        a_spec = pl.BlockSpec((None, ka, bt), lambda j, tt: (j, 0, tt))
    else:
        a_spec = pl.BlockSpec((ka, bt), lambda j, tt: (0, tt))
    if b_mode == "shared":
        nb, b_spec = b.shape[1], pl.BlockSpec((bt, b.shape[1]), lambda j, tt: (tt, 0))
    elif b_mode == "cols":
        nb = b.shape[1] // pieces
        b_spec = pl.BlockSpec((bt, group * nb), lambda j, tt: (tt, j))
    else:
        nb, b_spec = b.shape[2], pl.BlockSpec((None, bt, b.shape[2]), lambda j, tt: (j, tt, 0))
    assert group == 1 or b_mode == "cols"

    def body(a_ref, b_ref, o_ref, acc):
        tt = pl.program_id(1)

        @pl.when(tt == 0)
        def _():
            acc[...] = jnp.zeros_like(acc)

        acc[...] += _dot(a_ref[...], b_ref[...])

        @pl.when(tt == nt - 1)
        def _():
            for k in range(group):
                o_ref[k] = acc[:, k * nb:(k + 1) * nb].astype(BF16)

    return pl.pallas_call(
        body, grid=(pieces // group, nt), name=name,
        in_specs=[a_spec, b_spec],
        out_specs=pl.BlockSpec((group, ka, nb), lambda j, tt: (j, 0, 0)),
        out_shape=S((pieces, ka, nb), BF16),
        scratch_shapes=[pltpu.VMEM((ka, group * nb), F32)],
        compiler_params=_params("parallel", "arbitrary"),
    )(at, b)


def _b_mixout(dxm, w, o, proj, c, lng, lnb, l, name):
    t = dxm.shape[0]
    tm = 256

    def body(dxm_ref, w_ref, o_ref, g_ref, c_ref, lng_ref, lnb_ref,
             dxb_ref, dygm_ref, dO_ref, dg_ref, dc_ref, dlg_ref, dlb_ref, dcb_ref):
        @pl.when(pl.program_id(0) == 0)
        def _():
            dlg_ref[...] = jnp.zeros_like(dlg_ref)
            dlb_ref[...] = jnp.zeros_like(dlb_ref)
            dcb_ref[...] = jnp.zeros_like(dcb_ref)

        dxb = dxm_ref[...].astype(BF16)
        dxb_ref[...] = dxb
        dy = _dot_nt(dxb, w_ref[...])
        dygm_ref[...] = dy[:, 0:GMW]
        for h in range(HEADS):
            sl = slice(h * DH, (h + 1) * DH)
            ohat, rstd = _standardize(o_ref[:, sl])
            sg, dsg = _silu_and_grad(g_ref[:, sl].astype(F32))
            dyr = dy[:, GMW + h * DH:GMW + (h + 1) * DH]
            dg_ref[:, sl] = (dyr * ohat * dsg).astype(BF16)
            dO_ref[:, sl] = _standardize_bwd(dyr * sg, ohat, rstd)
        chat, rstd = _standardize(c_ref[...])
        z = chat * lng_ref[...] + lnb_ref[...]
        _, dsz = _silu_and_grad(z)
        dz = dy[:, GMW + RETW:D] * dsz
        dlg_ref[...] += _col_sum(dz * chat)
        dlb_ref[...] += _col_sum(dz)
        dc = _standardize_bwd(dz * lng_ref[...], chat, rstd)
        dcb_ref[...] += _col_sum(dc)
        dc_ref[...] = dc

    vec = lambda: pl.BlockSpec((None, 1, CVW), lambda i: (l, 0, 0))
    acc = lambda: pl.BlockSpec((1, CVW), lambda i: (0, 0))
    return pl.pallas_call(
        body, grid=(t // tm,), name=name,
        in_specs=[pl.BlockSpec((tm, D), lambda i: (i, 0)),
                  _resident((D, D), lambda i: (0, 0)),
                  pl.BlockSpec((tm, RETW), lambda i: (i, 0)),
                  pl.BlockSpec((tm, RETW), lambda i: (i, 4)),
                  pl.BlockSpec((tm, CVW), lambda i: (i, 0)), vec(), vec()],
        out_specs=[pl.BlockSpec((tm, D), lambda i: (i, 0)), pl.BlockSpec((tm, GMW), lambda i: (i, 0)),
                   pl.BlockSpec((tm, RETW), lambda i: (i, 0)), pl.BlockSpec((tm, RETW), lambda i: (i, 0)),
                   pl.BlockSpec((tm, CVW), lambda i: (i, 0)), acc(), acc(), acc()],
        out_shape=[S((t, D), BF16), S((t, GMW), F32), S((t, RETW), F32), S((t, RETW), BF16), S((t, CVW), F32),
                   S((1, CVW), F32), S((1, CVW), F32), S((1, CVW), F32)],
        compiler_params=_params("arbitrary"),
    )(dxm, w, o, proj, c, lng, lnb)


def _b_gm(proj, dy, lng, lnb, ws_bf, wst_bf, bias, l, name):
    t = proj.shape[0]
    tm = 512
    nt = t // tm

    def body(p_ref, dy_ref, lng_ref, lnb_ref, ws_ref, wst_ref, bias_ref,
             duv_ref, dws_ref, dbias_ref, dbs_ref, dlg_ref, dlb_ref):
        @pl.when(pl.program_id(0) == 0)
        def _():
            dws_ref[...] = jnp.zeros_like(dws_ref)
            dbias_ref[...] = jnp.zeros_like(dbias_ref)
            dbs_ref[...] = jnp.zeros_like(dbs_ref)
            dlg_ref[...] = jnp.zeros_like(dlg_ref)
            dlb_ref[...] = jnp.zeros_like(dlb_ref)

        for ci in range(tm // C):
            rows = slice(ci * C, (ci + 1) * C)
            u = p_ref[rows, 0:GMW].astype(F32)
            v = p_ref[rows, GMW:2 * GMW].astype(F32)
            au, dau, dav, vhat, rstd, vn, mixed, head = _gm_chunk_fwd(u, v, lng_ref[...], lnb_ref[...], ws_ref, bias_ref[...])
            dyc = dy_ref[rows, :]
            dmixed = dyc * au
            dmb = dmixed.astype(BF16)
            dbias_ref[...] += dmixed
            dvn = jnp.zeros((C, GMW), F32)
            for h in range(HEADS):
                dws_ref[h] += _dot_nt(jnp.where(head == h, dmixed, 0.0).astype(BF16), vn)
                dvn = dvn + jnp.where(head == h, _dot(wst_ref[h], dmb), 0.0)
            dlg_ref[...] += _col_sum(dvn * vhat)
            dlb_ref[...] += _col_sum(dvn)
            dav_in = _standardize_bwd(dvn * lng_ref[...], vhat, rstd)
            duv_ref[rows, 0:GMW] = (dyc * mixed * dau).astype(BF16)
            duv_ref[rows, GMW:2 * GMW] = (dav_in * dav).astype(BF16)

        @pl.when(pl.program_id(0) == nt - 1)
        def _():
            head = lax.broadcasted_iota(jnp.int32, (C, GMW), 1) // (GMW // HEADS)
            lane = lax.broadcasted_iota(jnp.int32, (C, 128), 1)
            fold = jnp.zeros((C, 128), F32)
            for h in range(HEADS):
                col = jnp.sum(jnp.where(head == h, dbias_ref[...], 0.0), axis=1, keepdims=True)
                fold = jnp.where(lane == h, col, fold)
            dbs_ref[...] = fold

    vec = lambda: pl.BlockSpec((None, 1, GMW), lambda i: (l, 0, 0))
    mats = lambda: pl.BlockSpec((None, HEADS, C, C), lambda i: (l, 0, 0, 0))
    return pl.pallas_call(
        body, grid=(nt,), name=name,
        in_specs=[pl.BlockSpec((tm, 2 * GMW), lambda i: (i, 0)), pl.BlockSpec((tm, GMW), lambda i: (i, 0)),
                  vec(), vec(), mats(), mats(), pl.BlockSpec((None, C, GMW), lambda i: (l, 0, 0))],
        out_specs=[pl.BlockSpec((tm, 2 * GMW), lambda i: (i, 0)),
                   pl.BlockSpec((HEADS, C, C), lambda i: (0, 0, 0)),
                   pl.BlockSpec((C, GMW), lambda i: (0, 0)), pl.BlockSpec((C, 128), lambda i: (0, 0)),
                   pl.BlockSpec((1, GMW), lambda i: (0, 0)), pl.BlockSpec((1, GMW), lambda i: (0, 0))],
        out_shape=[S((t, 2 * GMW), BF16), S((HEADS, C, C), F32), S((C, GMW), F32), S((C, 128), F32),
                   S((1, GMW), F32), S((1, GMW), F32)],
        compiler_params=_params("arbitrary"),
    )(proj, dy, lng, lnb, ws_bf, wst_bf, bias)


def _b_conv(proj, dc, cw, l, name, comm=None):
    t = proj.shape[0]
    tm = 256
    nt = t // tm
    rb = 64

    def body(ap, ac, an, gp, gc, gn, dp, dcur, dn, cw_ref, dag_ref, dcw_ref, hext, dext, hrot, drot):
        i = pl.program_id(0)

        @pl.when(i == 0)
        def _():
            dcw_ref[...] = jnp.zeros_like(dcw_ref)

        _fill_ext(hext, _glu(ap, gp), _glu(ac, gc), _glu(an, gn), i, nt, tm)
        _fill_ext(dext, dp[...], dcur[...], dn[...], i, nt, tm)
        _shifted_copies(hext, hrot, tm)
        _shifted_copies(dext, drot, tm)
        for j in range(KW):
            dcw_ref[j:j + 1, :] += _col_sum(dcur[...] * _tap(hext, hrot, 0, j + 1, tm))
        for r0 in range(0, tm, rb):
            dh = jnp.zeros((rb, CVW), F32)
            for j in range(KW):
                dh = dh + cw_ref[j:j + 1, :] * _tap(dext, drot, r0, 2 * HALO - 1 - j, rb)
            a = ac[r0:r0 + rb, :].astype(F32)
            s = _sigmoid(gc[r0:r0 + rb, :].astype(F32))
            dag_ref[r0:r0 + rb, 0:CVW] = (dh * s).astype(BF16)
            dag_ref[r0:r0 + rb, CVW:2 * CVW] = (dh * a * s * (1.0 - s)).astype(BF16)

    dspecs = _conv_halo_specs(t, tm, CVW, 0)
    return _launch(
        body, (nt,),
        _conv_halo_specs(t, tm, CVW, 10) + _conv_halo_specs(t, tm, CVW, 11) + dspecs +
        [pl.BlockSpec((None, 32, CVW), lambda i: (l, 0, 0))],
        [pl.BlockSpec((tm, 2 * CVW), lambda i: (i, 0)), pl.BlockSpec((32, CVW), lambda i: (0, 0))],
        [S((t, 2 * CVW), BF16), S((32, CVW), F32)],
        [pltpu.VMEM((tm + 2 * HALO, CVW), F32), pltpu.VMEM((tm + 2 * HALO, CVW), F32),
         pltpu.VMEM((7, tm + 2 * HALO - 8, CVW), F32), pltpu.VMEM((7, tm + 2 * HALO - 8, CVW), F32)],
        (proj, proj, proj, proj, proj, proj, dc, dc, dc, cw), name, ("arbitrary",), comm)


def _b_ret_out(proj, cos2, sin2, dO, s_f, s_b, g_f, g_b, rc, name, comm=None):
    t = proj.shape[0]
    tm = RET_CHUNKS * C

    def body(q_ref, k_ref, v_ref, cos_ref, sin_ref, dO_ref, sf_ref, sb_ref, gf_ref, gb_ref,
             d_ref, xif_ref, xib_ref, zef_ref, zeb_ref, dq_ref, dk_ref, dv_ref):
        for ci in range(RET_CHUNKS):
            rows = slice(ci * C, (ci + 1) * C)
            cos_v, sin_v = cos_ref[rows, :], sin_ref[rows, :]
            for h in range(HEADS):
                sl = slice(h * DH, (h + 1) * DH)
                qh, kh, vh = q_ref[rows, sl], k_ref[rows, sl], v_ref[rows, sl]
                dOh = dO_ref[rows, sl].astype(BF16)
                dm = d_ref[h]
                p = (_dot_nt(qh, kh) * dm).astype(BF16)
                dp = (_dot_nt(dOh, vh) * dm).astype(BF16)
                from_s = _dot_nt(dOh, jnp.concatenate([sf_ref[ci, h], sb_ref[ci, h]], axis=0))
                from_g = _dot_nt(vh, jnp.concatenate([gf_ref[ci, h], gb_ref[ci, h]], axis=0))
                kg = _dot(kh, jnp.concatenate([gf_ref[ci, h], gb_ref[ci, h]], axis=1))
                dqr = _dot(dp, kh) + xif_ref[h] * from_s[:, 0:DH] + xib_ref[h] * from_s[:, DH:2 * DH]
                dkr = (_dot_tn(dp, qh) + zef_ref[h] * from_g[:, 0:DH] + zeb_ref[h] * from_g[:, DH:2 * DH]) * SCALE
                dv = _dot_tn(p, dOh) + zef_ref[h] * kg[:, 0:DH] + zeb_ref[h] * kg[:, DH:2 * DH]
                dq_ref[rows, sl] = _rot_t(dqr, cos_v, sin_v).astype(BF16)
                dk_ref[rows, sl] = _rot_t(dkr, cos_v, sin_v).astype(BF16)
                dv_ref[rows, sl] = dv.astype(BF16)

    const = lambda: pl.BlockSpec((HEADS, C, DH), lambda i: (0, 0, 0))
    state = lambda: pl.BlockSpec((RET_CHUNKS, HEADS, DH, DH), lambda i: (i, 0, 0, 0))
    tok = lambda: pl.BlockSpec((tm, RETW), lambda i: (i, 0))
    return _launch(
        body, (t // tm,),
        [pl.BlockSpec((tm, RETW), lambda i, cb=cb: (i, cb)) for cb in (1, 2, 3)] +
        [pl.BlockSpec((tm, DH), lambda i: (i, 0)), pl.BlockSpec((tm, DH), lambda i: (i, 0)), tok(),
         state(), state(), state(), state()] + [const() for _ in range(5)],
        [tok(), tok(), tok()],
        [S((t, RETW), BF16) for _ in range(3)], [],
        (proj, proj, proj, cos2, sin2, dO, s_f, s_b, g_f, g_b, rc["f"]["D"] + rc["b"]["D"],
         rc["f"]["XI"], rc["b"]["XI"], rc["f"]["ZETA"], rc["b"]["ZETA"]), name, ("parallel",), comm)


def _b_inproj(d_uv, dqkv, d_g, d_ag, w, x, g1, dxm, l, name, comm=None):
    t = x.shape[0]
    tm = 256

    def body(duv_ref, dq_ref, dk_ref, dv_ref, dg_ref, dag_ref, w_ref, x_ref, g_ref, dxm_ref,
             dp_ref, dx_ref, dn_ref):
        @pl.when(pl.program_id(0) == 0)
        def _():
            dn_ref[...] = jnp.zeros_like(dn_ref)

        for k, part in enumerate((duv_ref, dq_ref, dk_ref, dv_ref, dg_ref, dag_ref)):
            dp_ref[:, 512 * k:512 * (k + 1)] = part[...]
        dh = _dot_nt(dp_ref[...], w_ref[...])
        xv = x_ref[...]
        r = _rms(xv)
        dn_ref[...] += _col_sum(dh * xv * r)
        dx_ref[...] = dxm_ref[...] + _rmsnorm_bwd(dh, xv, r, g_ref[...])

    half = lambda: pl.BlockSpec((tm, 512), lambda i: (i, 0))
    full = lambda: pl.BlockSpec((tm, D), lambda i: (i, 0))
    return _launch(
        body, (t // tm,),
        [half() for _ in range(6)] +
        [_resident((D, INW), lambda i: (0, 0)), full(), pl.BlockSpec((None, 1, D), lambda i: (l, 0, 0)), full()],
        [pl.BlockSpec((tm, INW), lambda i: (i, 0)), full(), pl.BlockSpec((1, D), lambda i: (0, 0))],
        [S((t, INW), BF16), S((t, D), F32), S((1, D), F32)], [],
        (d_uv, *dqkv, d_g, d_ag, w, x, g1, dxm), name, ("arbitrary",), comm)


def _pair_exchange(gs, name):
    n = len(gs)

    def body(*refs):
        g, q, ssem, rsem = refs[:n], refs[n:2 * n], refs[2 * n], refs[2 * n + 1]
        x, y, c, _ = _place()
        copies = [pltpu.make_async_remote_copy(src_ref=g[u].at[2 * chip + (1 - c)], dst_ref=q[u].at[chip],
                                               send_sem=ssem.at[u, chip], recv_sem=rsem.at[u, chip],
                                               device_id=(x, y, 1 - c), device_id_type=MESH)
                  for u in range(n) for chip in range(4)]
        for cp in copies:
            cp.start()
        for cp in copies:
            cp.wait_recv()
        for cp in copies:
            cp.wait_send()

    return pl.pallas_call(body, name=name, in_specs=[_HBM] * n, out_specs=[_HBM] * n,
                          out_shape=[S((4,) + g.shape[1:], BF16) for g in gs],
                          scratch_shapes=[pltpu.SemaphoreType.DMA((n, 4)), pltpu.SemaphoreType.DMA((n, 4))],
                          compiler_params=_params())(*gs)


def _pair_add(g, q, name):
    _, mm, nn = g.shape
    bm = _row_block(mm)

    def body(g_ref, q_ref, h_ref):
        h_ref[...] = (g_ref[lax.axis_index("c")].astype(F32) + q_ref[...].astype(F32)).astype(BF16)

    return pl.pallas_call(
        body, grid=(4, mm // bm), name=name,
        in_specs=[pl.BlockSpec((None, 2, bm, nn), lambda qq, i: (qq, 0, i, 0)),
                  pl.BlockSpec((None, bm, nn), lambda qq, i: (qq, i, 0))],
        out_specs=pl.BlockSpec((None, bm, nn), lambda qq, i: (qq, i, 0)),
        out_shape=S((4, mm, nn), BF16),
        compiler_params=_params("parallel", "parallel"),
    )(g.reshape(4, 2, mm, nn), q)


def _pair_reduce(named, l):
    names = [k for k, _ in named]
    qs = _pair_exchange([g for _, g in named], f"pair_exchange_{names[0]}_{l}")
    return [_pair_add(g, q, f"pair_add_{k}_{l}") for (k, g), q in zip(named, qs)]


_BIG = ("w_in", "w_out", "w_ffn_in", "w_ffn_out")
_KIND = dict(w_in="cols", w_out="rows", w_ffn_in="lead", w_ffn_out="rows")
CWP = 128
EARLY_ROWS, LATE_ROWS = 152, 8


def _step(x, tgt, wts, sh, cw_pad):
    t = x.shape[0]
    rc = _ret_consts()
    cos2, sin2 = (jnp.asarray(a) for a in _rope_tables(t))
    n1 = wts["norm1_g"].reshape(LAYERS, 1, D)
    n2 = wts["norm2_g"].reshape(LAYERS, 1, D)
    gm_lng = wts["gm_ln_g"].reshape(LAYERS, 1, GMW)
    gm_lnb = wts["gm_ln_b"].reshape(LAYERS, 1, GMW)
    ws_bf = wts["gm_ws"].astype(BF16)
    wst_bf = jnp.swapaxes(wts["gm_ws"], 2, 3).astype(BF16)
    bias = jnp.repeat(jnp.swapaxes(wts["gm_bs"], 1, 2), GMW // HEADS, axis=2)
    cb = wts["conv_b"].reshape(LAYERS, 1, CVW)
    cv_lng = wts["conv_ln_g"].reshape(LAYERS, 1, CVW)
    cv_lnb = wts["conv_ln_b"].reshape(LAYERS, 1, CVW)
    unit = lambda f, l: (sh[f][l], _KIND[f])
    cshard = CVW // NDEV

    full = {f: [None] * LAYERS for f in _BIG}
    full["w_in"][0], cw_all = _comm_only(_Gather([unit("w_in", 0), (cw_pad, "lead")]), "gather_first")
    cw = jnp.transpose(cw_all[:, :, :, :cshard], (1, 2, 0, 3)).reshape(LAYERS, 32, CVW)

    gcf, gcb = rc["f"]["gC"], rc["b"]["gC"]
    saved = []
    for l in range(LAYERS):
        first = l == 0
        (proj, h1), got = _f_inproj(x, n1, full["w_in"][l], cos2, sin2, l, f"f_inproj_{l}",
                                    _Gather([unit("w_ffn_in", 0)]) if first else None)
        if first:
            full["w_ffn_in"][0], = got
        y_gm = _f_gm(proj, gm_lng, gm_lnb, ws_bf, bias, l, f"f_gm_{l}")
        s_f, s_b = _scan_pair(proj, None, 2, rc["f"]["ZETA"], rc["b"]["ZETA"], gcf, gcb, True, f"f_ret_state_{l}")
        o = _f_ret_out(proj, s_f, s_b, rc, f"f_ret_out_{l}")
        (c, y_cv), got = _f_conv(proj, cw, cb, cv_lng, cv_lnb, l, f"f_conv_{l}",
                                 _Gather([unit("w_out", 0), unit("w_ffn_out", 0)]) if first else None)
        if first:
            full["w_out"][0], full["w_ffn_out"][0] = got
        xm, ycat = _f_mixout(x, y_gm, y_cv, o, proj, full["w_out"][l], f"f_mixout_{l}")
        (xo, h2, gu, act), got = _f_ffn(xm, n2, full["w_ffn_in"][l], full["w_ffn_out"][l], l, f"f_ffn_{l}",
                                        _Gather([unit(f, 1) for f in _BIG]) if first else None)
        if first:
            full["w_in"][1], full["w_out"][1], full["w_ffn_in"][1], full["w_ffn_out"][1] = got
        saved.append(dict(x=x, proj=proj, h1=h1, o=o, s_f=s_f, s_b=s_b, c=c, xm=xm, ycat=ycat, h2=h2, gu=gu, act=act))
        x = xo

    parts = {f: [None] * LAYERS for f in _BIG}
    small = [None] * LAYERS
    norm1 = [None] * LAYERS
    upper = None
    top = (x, wts["final_g"].reshape(1, D), tgt)
    for l in reversed(range(LAYERS)):
        sv = saved[l]
        outs, got = _b_ffn(top, sv["xm"], n2, sv["gu"], full["w_ffn_in"][l], full["w_ffn_out"][l], l,
                           f"b_ffn_{l}", _Scatter(upper) if upper else None)
        dgu_t, dxm, dxo_bf, d_n2 = outs[:4]
        if l == LAYERS - 1:
            loss, d_final = outs[4:]
        if upper:
            for f, p in zip(_BIG, got):
                parts[f][l + 1] = p
        g_f2 = _mm_wgrad(sv["act"], dxo_bf, NDEV // 2, True, "shared", 1, f"g_ffn_out_{l}").reshape(NDEV, FFH // NDEV, D)
        g_f1 = _mm_wgrad(dgu_t, sv["h2"], NDEV, True, "shared", 1, f"g_ffn_in_{l}")
        dxm_bf, dy_gm, dO, d_g, dc, d_cvlg, d_cvlb, d_cb = _b_mixout(
            dxm, full["w_out"][l], sv["o"], sv["proj"], sv["c"], cv_lng, cv_lnb, l, f"b_mixout_{l}")
        g_out = _mm_wgrad(sv["ycat"], dxm_bf, 1, False, "shared", 1, f"g_out_{l}").reshape(NDEV, D // NDEV, D)
        last = l == 0
        early = _pair_reduce([("w_out", g_out), ("w_ffn_in", g_f1), ("w_ffn_out", g_f2)], l) if last else None
        d_uv, d_ws, _, d_bs_fold, d_gmlg, d_gmlb = _b_gm(sv["proj"], dy_gm, gm_lng, gm_lnb, ws_bf, wst_bf, bias, l, f"b_gm_{l}")
        (d_ag, d_cw), got = _b_conv(sv["proj"], dc, cw, l, f"b_conv_{l}", _Scatter(early[1:2]) if last else None)
        if last:
            parts["w_ffn_in"][l], = got
        small[l] = dict(gm_ln_g=d_gmlg[0], gm_ln_b=d_gmlb[0], gm_ws=d_ws, gm_bs=d_bs_fold[:, :HEADS].T, conv_w=d_cw[:KW],
                        conv_b=d_cb[0], conv_ln_g=d_cvlg[0], conv_ln_b=d_cvlb[0], norm2_g=d_n2[0])
        comm = None
        if last:
            early_g = {k: jnp.stack([small[ll][k] for ll in range(LAYERS)]) for k in small[0]}
            early_g["final_g"] = d_final[0]
            early_buf = _pack([early_g[k] for k in _SMALL_EARLY], EARLY_ROWS)
            comm = _Comms([_Scatter([early[0], early[2]]), _Gather([(early_buf, "lead")])])
        g_f, g_b = _scan_pair(sv["proj"], dO, 1, rc["f"]["XI"], rc["b"]["XI"], gcf, gcb, False, f"b_ret_state_{l}")
        dqkv, got = _b_ret_out(sv["proj"], cos2, sin2, dO, sv["s_f"], sv["s_b"], g_f, g_b, rc, f"b_ret_out_{l}", comm)
        if last:
            parts["w_out"][l], parts["w_ffn_out"][l], early_parts = got
        (dproj, top, d_n1), _ = _b_inproj(d_uv, dqkv, d_g, d_ag, full["w_in"][l], sv["x"], n1, dxm, l, f"b_inproj_{l}")
        norm1[l] = d_n1[0]
        g_in = _mm_wgrad(sv["h1"], dproj, NDEV, False, "cols", 2, f"g_in_{l}")
        if last:
            tail = _pair_reduce([("w_in", g_in)], l)
        else:
            upper = _pair_reduce([("w_in", g_in), ("w_out", g_out), ("w_ffn_in", g_f1), ("w_ffn_out", g_f2)], l)
    late_buf = _pack([jnp.stack(norm1)], LATE_ROWS)
    parts["w_in"][0], late_parts = _comm_only(_Comms([_Scatter(tail), _Gather([(late_buf, "lead")])]), "exchange_last")
    return loss, top, parts, (early_parts, late_parts)


def _adamw(w, g, m, v):
    m = ADAM_B1 * m + (1.0 - ADAM_B1) * g
    v = ADAM_B2 * v + (1.0 - ADAM_B2) * (g * g)
    m_hat = m / (1.0 - ADAM_B1 ** ADAM_STEP)
    v_hat = v / (1.0 - ADAM_B2 ** ADAM_STEP)
    return -ADAM_LR * (m_hat / (jnp.sqrt(v_hat) + ADAM_EPS) + ADAM_WD * w), m, v


def _cast_blocks(ws):
    def body(*refs):
        ins, outs = refs[:len(ws)], refs[len(ws):]
        for k, src in enumerate(ins):
            for l in range(LAYERS):
                outs[k * LAYERS + l][...] = src[l].astype(BF16)

    outs = pl.pallas_call(body, name="cast_blocks", out_shape=[S(w.shape[1:], BF16) for w in ws for _ in range(LAYERS)],
                          compiler_params=_params())(*ws)
    return [list(outs[k * LAYERS:(k + 1) * LAYERS]) for k in range(len(ws))]


def _row_block(mm):
    return next(b for b in (256, 352, 128) if mm % b == 0)


def _sum_adam(parts, w, m, v, l, prev, name):
    _, mm, nn = parts.shape
    bm = _row_block(mm)

    def body(p_ref, w_ref, m_ref, v_ref, *rest):
        g_ref, d_ref, nm_ref, nv_ref = rest[-4:]
        g = p_ref[0].astype(F32)
        for s in range(1, 4):
            g = g + p_ref[s].astype(F32)
        g_ref[...] = g
        d_ref[...], nm_ref[...], nv_ref[...] = _adamw(w_ref[...], g, m_ref[...], v_ref[...])

    blk = lambda: pl.BlockSpec((None, bm, nn), lambda i: (l, i, 0))
    prev = list(prev) if prev else []
    return pl.pallas_call(
        body, grid=(mm // bm,), name=name,
        in_specs=[pl.BlockSpec((4, bm, nn), lambda i: (0, i, 0)), blk(), blk(), blk()] + [_ANY] * len(prev),
        out_specs=[blk() for _ in range(4)],
        out_shape=[S(w.shape, F32) for _ in range(4)],
        input_output_aliases={4 + j: j for j in range(len(prev))},
        compiler_params=_params("parallel"),
    )(parts, w, m, v, *prev)


def _sum_small(parts):
    n = len(parts)

    def body(*refs):
        for p_ref, o_ref in zip(refs[:n], refs[n:]):
            g = p_ref[0]
            for s in range(1, NDEV):
                g = g + p_ref[s]
            o_ref[...] = g

    return pl.pallas_call(body, name="sum_small", out_shape=[S(p.shape[1:], F32) for p in parts],
                          compiler_params=_params())(*parts)


def _adam_small(g, w, m, v):
    def body(g_ref, w_ref, m_ref, v_ref, d_ref, nm_ref, nv_ref):
        d_ref[...], nm_ref[...], nv_ref[...] = _adamw(w_ref[...], g_ref[...], m_ref[...], v_ref[...])

    return pl.pallas_call(body, name="adam_small", out_shape=[S(g.shape, F32)] * 3, compiler_params=_params())(g, w, m, v)


_SMALL = ("norm1_g", "gm_ln_g", "gm_ln_b", "gm_ws", "gm_bs", "conv_w", "conv_b", "conv_ln_g", "conv_ln_b",
          "norm2_g", "final_g")
_SMALL_EARLY = _SMALL[1:]
_NAMES = ("norm1_g", "w_in", "gm_ln_g", "gm_ln_b", "gm_ws", "gm_bs", "conv_w", "conv_b", "conv_ln_g", "conv_ln_b",
          "w_out", "norm2_g", "w_ffn_in", "w_ffn_out", "final_g")


def _pack(parts, rows):
    flat = jnp.concatenate([p.reshape(-1) for p in parts])
    return jnp.pad(flat, (0, rows * 1024 - flat.shape[0])).reshape(rows, 1024)


def _unpack(buf, shapes):
    flat = buf.reshape(-1)
    out, o = [], 0
    for shp in shapes:
        sz = int(np.prod(shp))
        out.append(flat[o:o + sz].reshape(shp))
        o += sz
    return out


def kernel(x, norm1_g, w_in, gm_ln_g, gm_ln_b, gm_ws, gm_bs, conv_w, conv_b, conv_ln_g, conv_ln_b, w_out, norm2_g, w_ffn_in, w_ffn_out, final_g, loss_target, m_norm1_g, m_w_in, m_gm_ln_g, m_gm_ln_b, m_gm_ws, m_gm_bs, m_conv_w, m_conv_b, m_conv_ln_g, m_conv_ln_b, m_w_out, m_norm2_g, m_w_ffn_in, m_w_ffn_out, m_final_g, v_norm1_g, v_w_in, v_gm_ln_g, v_gm_ln_b, v_gm_ws, v_gm_bs, v_conv_w, v_conv_b, v_conv_ln_g, v_conv_ln_b, v_w_out, v_norm2_g, v_w_ffn_in, v_w_ffn_out, v_final_g):
    w = dict(norm1_g=norm1_g, w_in=w_in, gm_ln_g=gm_ln_g, gm_ln_b=gm_ln_b, gm_ws=gm_ws, gm_bs=gm_bs, conv_w=conv_w,
             conv_b=conv_b, conv_ln_g=conv_ln_g, conv_ln_b=conv_ln_b, w_out=w_out, norm2_g=norm2_g, w_ffn_in=w_ffn_in,
             w_ffn_out=w_ffn_out, final_g=final_g)
    mo = dict(norm1_g=m_norm1_g, w_in=m_w_in, gm_ln_g=m_gm_ln_g, gm_ln_b=m_gm_ln_b, gm_ws=m_gm_ws, gm_bs=m_gm_bs,
              conv_w=m_conv_w, conv_b=m_conv_b, conv_ln_g=m_conv_ln_g, conv_ln_b=m_conv_ln_b, w_out=m_w_out,
              norm2_g=m_norm2_g, w_ffn_in=m_w_ffn_in, w_ffn_out=m_w_ffn_out, final_g=m_final_g)
    vo = dict(norm1_g=v_norm1_g, w_in=v_w_in, gm_ln_g=v_gm_ln_g, gm_ln_b=v_gm_ln_b, gm_ws=v_gm_ws, gm_bs=v_gm_bs,
              conv_w=v_conv_w, conv_b=v_conv_b, conv_ln_g=v_conv_ln_g, conv_ln_b=v_conv_ln_b, w_out=v_w_out,
              norm2_g=v_norm2_g, w_ffn_in=v_w_ffn_in, w_ffn_out=v_w_ffn_out, final_g=v_final_g)
    t = x.shape[1]
    me = 4 * lax.axis_index("x") + 2 * lax.axis_index("y") + lax.axis_index("c")
    cshard = conv_w.shape[2]

    cw_pad = jnp.pad(conv_w, ((0, 0), (0, 32 - KW), (0, CWP - cshard)))
    flip = lambda a: jnp.swapaxes(a, 1, 2)
    big = {f: tuple(flip(a[f]) if f == "w_ffn_in" else a[f] for a in (w, mo, vo)) for f in _BIG}
    sh = dict(zip(_BIG, _cast_blocks([big[f][0] for f in _BIG])))
    loss, dx, parts, small_parts = _step(x.reshape(t, D), loss_target.reshape(t, D), w, sh, cw_pad)

    grads, delta, new_m, new_v = {}, {}, {}, {}
    for f in _BIG:
        outs = None
        for l in reversed(range(LAYERS)):
            outs = _sum_adam(parts[f][l], *big[f], l, outs, f"sum_adam_{f}_{l}")
        grads[f], delta[f], new_m[f], new_v[f] = [flip(a) for a in outs] if f == "w_ffn_in" else outs

    early_sum, late_sum = _sum_small(small_parts)
    early_shapes = [(LAYERS, KW, CVW) if k == "conv_w" else w[k].shape for k in _SMALL_EARLY]
    grads["norm1_g"], = _unpack(late_sum, [w["norm1_g"].shape])
    for k, g in zip(_SMALL_EARLY, _unpack(early_sum, early_shapes)):
        grads[k] = lax.dynamic_slice_in_dim(g, me * cshard, cshard, axis=2) if k == "conv_w" else g
    adam_rows = 144
    d_s, m_s, v_s = _adam_small(_pack([grads[k] for k in _SMALL], adam_rows), _pack([w[k] for k in _SMALL], adam_rows),
                                _pack([mo[k] for k in _SMALL], adam_rows), _pack([vo[k] for k in _SMALL], adam_rows))
    shapes = [w[k].shape for k in _SMALL]
    for dst, buf in ((delta, d_s), (new_m, m_s), (new_v, v_s)):
        for k, a in zip(_SMALL, _unpack(buf, shapes)):
            dst[k] = a

    total = lax.psum(loss[0, 0], ("x", "y", "c"))
    return (total, dx.reshape(1, t, D), *[grads[k] for k in _NAMES], *[delta[k] for k in _NAMES],
            *[new_m[k] for k in _NAMES], *[new_v[k] for k in _NAMES])
```

```python
import functools

import numpy as np
import jax
import jax.numpy as jnp
from jax import lax
from jax.experimental import pallas as pl
from jax.experimental.pallas import tpu as pltpu

F32, BF16 = jnp.float32, jnp.bfloat16
S = jax.ShapeDtypeStruct

D = 1024
INW = 3072
GMW = 256
RETW = 512
CVW = 256
HEADS = 4
DH = 128
C = 128
KW = 31
HALO = 16
FFH = 2816
NDEV = 8
FFB = 2 * FFH // NDEV
EPS = 1e-6
LAYERS = 2
SCALE = DH ** -0.5
VMEM_LIMIT = 56 * 1024 * 1024

ADAM_LR, ADAM_B1, ADAM_B2, ADAM_EPS, ADAM_WD, ADAM_STEP = 0.001, 0.9, 0.999, 1e-08, 0.01, 10

_SQRT_HALF = 0.7071067811865476
_INV_SQRT_2PI = 0.3989422804014327


def _params(*sem):
    return pltpu.CompilerParams(dimension_semantics=sem or None, vmem_limit_bytes=VMEM_LIMIT)


def _resident(shape, index_map):
    return pl.BlockSpec(shape, index_map, pipeline_mode=pl.Buffered(1))


def _dot(a, b):
    return jnp.dot(a, b, preferred_element_type=F32)


def _dot_nt(a, b):
    return lax.dot_general(a, b, (((1,), (1,)), ((), ())), preferred_element_type=F32)


def _dot_tn(a, b):
    return lax.dot_general(a, b, (((0,), (0,)), ((), ())), preferred_element_type=F32)


def _sigmoid(x):
    return 1.0 / (1.0 + jnp.exp(-x))


def _gelu_and_grad(x):
    cdf = 0.5 * (1.0 + lax.erf(x * _SQRT_HALF))
    return x * cdf, cdf + x * jnp.exp(-0.5 * x * x) * _INV_SQRT_2PI


def _silu_and_grad(x):
    s = _sigmoid(x)
    return x * s, s * (1.0 + x * (1.0 - s))


def _standardize(x):
    mu = jnp.mean(x, axis=-1, keepdims=True)
    d = x - mu
    rstd = lax.rsqrt(jnp.mean(d * d, axis=-1, keepdims=True) + EPS)
    return d * rstd, rstd


def _standardize_bwd(dxhat, xhat, rstd):
    m1 = jnp.mean(dxhat, axis=-1, keepdims=True)
    m2 = jnp.mean(dxhat * xhat, axis=-1, keepdims=True)
    return rstd * (dxhat - m1 - xhat * m2)


def _rms(x):
    return lax.rsqrt(jnp.mean(x * x, axis=-1, keepdims=True) + EPS)


def _rmsnorm_bwd(dy, x, r, g):
    u = dy * g
    return r * u - x * (r * r * r) * jnp.mean(u * x, axis=-1, keepdims=True)


def _col_sum(a):
    return jnp.sum(a, axis=0, keepdims=True)


def _rot(t, cos2, sin2):
    return t * cos2 + pltpu.roll(t, DH // 2, axis=1) * sin2


def _rot_t(dt, cos2, sin2):
    return dt * cos2 + pltpu.roll(dt * sin2, DH // 2, axis=1)


MESH = pl.DeviceIdType.MESH
_HBM = pl.BlockSpec(memory_space=pltpu.HBM)
_ANY = pl.BlockSpec(memory_space=pl.ANY)


def _place():
    x, y, c = lax.axis_index("x"), lax.axis_index("y"), lax.axis_index("c")
    return x, y, c, ((1 - x, y), (x, 1 - y), (1 - x, 1 - y))


def _slot(full, kind, width, i):
    if kind == "cols":
        return full.at[:, pl.ds(pl.multiple_of(i * width, 128), width)]
    if kind == "rows":
        return full.at[pl.ds(pl.multiple_of(i * width, 16), width), :]
    return full.at[i]


class _Gather:
    def __init__(self, units):
        self.units = units
        self.inputs = [u[0] for u in units]
        self.out_shape = []
        for src, kind in units:
            r, c = src.shape[-2:]
            shape = {"cols": (r, NDEV * c), "rows": (NDEV * r, c), "lead": (NDEV,) + src.shape}[kind]
            self.out_shape.append(S(shape, src.dtype))
        n = len(units)
        self.scratch = [pltpu.SemaphoreType.DMA((n, 7)), pltpu.SemaphoreType.DMA((n, 7)), pltpu.SemaphoreType.DMA((n,))]

    def run(self, phase, ins, outs, scr):
        ssem, rsem, lsem = scr
        x, y, c, chips = _place()
        me, sib = 4 * x + 2 * y + c, (x, y, 1 - c)
        idx = lambda chip, core: 4 * chip[0] + 2 * chip[1] + core
        for u, (src_arr, kind) in enumerate(self.units):
            src, full = ins[u], outs[u]
            width = src_arr.shape[-1] if kind == "cols" else src_arr.shape[-2]
            slot = functools.partial(_slot, full, kind, width)

            def copy(k, block, to, from_src=False):
                return pltpu.make_async_remote_copy(src_ref=src if from_src else slot(block), dst_ref=slot(block),
                                                    send_sem=ssem.at[u, k], recv_sem=rsem.at[u, k],
                                                    device_id=to, device_id_type=MESH)

            mine = lambda: pltpu.make_async_copy(src, slot(me), lsem.at[u])
            first = lambda: [copy(0, me, sib, True)] + [copy(1 + j, me, (*chip, c), True) for j, chip in enumerate(chips)]
            passed = lambda j: copy(4 + j, idx(chips[j], c), sib)
            if phase == "start":
                mine().start()
                for cp in first():
                    cp.start()
            elif phase == "forward":
                for j, chip in enumerate(chips):
                    copy(1 + j, idx(chip, c), sib).wait_recv()
                    passed(j).start()
            else:
                copy(0, idx((x, y), 1 - c), sib).wait_recv()
                for j, chip in enumerate(chips):
                    copy(4 + j, idx(chip, 1 - c), sib).wait_recv()
                for cp in first() + [passed(j) for j in range(3)]:
                    cp.wait_send()
                mine().wait()


class _Scatter:
    def __init__(self, units):
        self.units = units
        self.inputs = list(units)
        self.out_shape = [S(u.shape, u.dtype) for u in units]
        n = len(units)
        self.scratch = [pltpu.SemaphoreType.DMA((n, 3)), pltpu.SemaphoreType.DMA((n, 3)), pltpu.SemaphoreType.DMA((n,))]

    def run(self, phase, ins, outs, scr):
        ssem, rsem, lsem = scr
        x, y, c, chips = _place()
        myq = 2 * x + y
        for u in range(len(self.units)):
            h, p = ins[u], outs[u]

            def copy(k, chip, send_to_them):
                q = 2 * chip[0] + chip[1]
                return pltpu.make_async_remote_copy(src_ref=h.at[q], dst_ref=p.at[myq if send_to_them else q],
                                                    send_sem=ssem.at[u, k], recv_sem=rsem.at[u, k],
                                                    device_id=(*chip, c), device_id_type=MESH)

            mine = lambda: pltpu.make_async_copy(h.at[myq], p.at[myq], lsem.at[u])
            sends = lambda: [copy(k, chip, True) for k, chip in enumerate(chips)]
            if phase == "start":
                mine().start()
                for cp in sends():
                    cp.start()
            elif phase == "finish":
                for k, chip in enumerate(chips):
                    copy(k, chip, False).wait_recv()
                for cp in sends():
                    cp.wait_send()
                mine().wait()


class _Comms:
    def __init__(self, parts):
        self.parts = parts
        self.inputs = [a for p in parts for a in p.inputs]
        self.out_shape = [a for p in parts for a in p.out_shape]
        self.scratch = [a for p in parts for a in p.scratch]

    def run(self, phase, ins, outs, scr):
        i = o = s = 0
        for p in self.parts:
            ni, no, ns = len(p.inputs), len(p.out_shape), len(p.scratch)
            p.run(phase, ins[i:i + ni], outs[o:o + no], scr[s:s + ns])
            i, o, s = i + ni, o + no, s + ns


def _launch(body, grid, in_specs, out_specs, out_shape, scratch, args, name, sem, comm=None):
    if comm is None:
        outs = pl.pallas_call(body, grid=grid, name=name, in_specs=in_specs, out_specs=out_specs, out_shape=out_shape,
                              scratch_shapes=scratch, compiler_params=_params(*sem))(*args)
        return list(outs), []
    n_in, n_out, n_scr = len(args), len(out_shape), len(scratch)
    ci, co = len(comm.inputs), len(comm.out_shape)
    nsteps = grid[0]
    fwd_step = (7 * nsteps) // 8

    def hosted(*refs):
        a = refs[:n_in]
        ca = refs[n_in:n_in + ci]
        o = refs[n_in + ci:n_in + ci + n_out]
        cout = refs[n_in + ci + n_out:n_in + ci + n_out + co]
        s = refs[n_in + ci + n_out + co:n_in + ci + n_out + co + n_scr]
        cs = refs[n_in + ci + n_out + co + n_scr:]
        step = pl.program_id(0)

        @pl.when(step == 0)
        def _():
            comm.run("start", ca, cout, cs)

        body(*a, *o, *s)

        @pl.when(step == fwd_step)
        def _():
            comm.run("forward", ca, cout, cs)

        @pl.when(step == nsteps - 1)
        def _():
            comm.run("finish", ca, cout, cs)

    outs = pl.pallas_call(
        hosted, grid=grid, name=name, in_specs=list(in_specs) + [_HBM] * ci, out_specs=list(out_specs) + [_HBM] * co,
        out_shape=list(out_shape) + comm.out_shape, scratch_shapes=list(scratch) + comm.scratch,
        compiler_params=_params(*["arbitrary"] * len(grid)))(*args, *comm.inputs)
    return list(outs[:n_out]), list(outs[n_out:])


def _comm_only(comm, name):
    ci, co = len(comm.inputs), len(comm.out_shape)

    def body(*refs):
        ca, cout, cs = refs[:ci], refs[ci:ci + co], refs[ci + co:]
        for phase in ("start", "forward", "finish"):
            comm.run(phase, ca, cout, cs)

    return pl.pallas_call(body, name=name, in_specs=[_HBM] * ci, out_specs=[_HBM] * co, out_shape=comm.out_shape,
                          scratch_shapes=comm.scratch, compiler_params=_params())(*comm.inputs)


def _ret_consts():
    idx = np.arange(C, dtype=np.float32)
    gf = (1.0 - np.exp2(-5.0 - np.arange(HEADS, dtype=np.float32))).astype(np.float32)
    out = {}
    for name, gamma, fwd in (("f", gf, True), ("b", gf[::-1].copy(), False)):
        lg = np.log(gamma).astype(np.float32)[:, None]
        diff = idx[:, None] - idx[None, :]
        if fwd:
            mask = diff >= 0
            dist = np.where(mask, diff, 0.0)
            zeta = np.exp(lg * (C - 1 - idx))
            xi = np.exp(lg * (idx + 1))
        else:
            mask = diff < 0
            dist = np.where(mask, -diff, 0.0)
            zeta = np.exp(lg * idx)
            xi = np.exp(lg * (C - idx))
        dm = np.where(mask[None], np.exp(lg[:, :, None] * dist[None]), 0.0).astype(np.float32)
        bc = lambda vec: np.ascontiguousarray(np.broadcast_to(vec.astype(np.float32)[:, :, None], (HEADS, C, DH)))
        out[name] = dict(D=dm, XI=bc(xi), ZETA=bc(zeta), gC=[float(v) for v in np.exp(lg[:, 0] * C).astype(np.float32)])
    return out


def _rope_tables(t):
    half = DH // 2
    inv_freq = (np.float32(10000.0) ** (-np.arange(half, dtype=np.float32) / np.float32(half))).astype(np.float32)
    ang = (np.arange(t, dtype=np.float32)[:, None] * inv_freq[None, :]).astype(np.float64)
    cos, sin = np.cos(ang).astype(np.float32), np.sin(ang).astype(np.float32)
    return np.concatenate([cos, cos], axis=1), np.concatenate([-sin, sin], axis=1)


def _f_inproj(x, g1, w, cos2, sin2, l, name, comm=None):
    t = x.shape[0]
    tm = 512

    def body(x_ref, g_ref, w_ref, cos_ref, sin_ref, proj_ref, ht_ref):
        xv = x_ref[...]
        h = (xv * _rms(xv) * g_ref[...]).astype(BF16)
        ht_ref[...] = h.T
        for nb in range(INW // 512):
            cs = slice(nb * 512, (nb + 1) * 512)
            res = _dot(h, w_ref[:, cs])
            if nb in (1, 2):
                for hh in range(HEADS):
                    r = _rot(res[:, hh * DH:(hh + 1) * DH], cos_ref[...], sin_ref[...])
                    proj_ref[:, nb * 512 + hh * DH:nb * 512 + (hh + 1) * DH] = (r * SCALE if nb == 2 else r).astype(BF16)
            else:
                proj_ref[:, cs] = res.astype(BF16)

    return _launch(
        body, (t // tm,),
        [pl.BlockSpec((tm, D), lambda i: (i, 0)),
         pl.BlockSpec((None, 1, D), lambda i: (l, 0, 0)),
         _resident((D, INW), lambda i: (0, 0)),
         pl.BlockSpec((tm, DH), lambda i: (i, 0)), pl.BlockSpec((tm, DH), lambda i: (i, 0))],
        [pl.BlockSpec((tm, INW), lambda i: (i, 0)), pl.BlockSpec((D, tm), lambda i: (0, i))],
        [S((t, INW), BF16), S((D, t), BF16)], [], (x, g1, w, cos2, sin2), name, ("parallel",), comm)


def _gm_chunk_fwd(u, v, lng, lnb, ws_ref, bias):
    au, dau = _gelu_and_grad(u)
    av, dav = _gelu_and_grad(v)
    vhat, rstd = _standardize(av)
    vn = (vhat * lng + lnb).astype(BF16)
    head = lax.broadcasted_iota(jnp.int32, (C, GMW), 1) // (GMW // HEADS)
    mixed = bias
    for h in range(HEADS):
        mixed = mixed + jnp.where(head == h, _dot(ws_ref[h], vn), 0.0)
    return au, dau, dav, vhat, rstd, vn, mixed, head


def _f_gm(proj, lng, lnb, ws_bf, bias, l, name):
    t = proj.shape[0]
    tm = 512

    def body(p_ref, lng_ref, lnb_ref, ws_ref, bias_ref, y_ref):
        for ci in range(tm // C):
            rows = slice(ci * C, (ci + 1) * C)
            u = p_ref[rows, 0:GMW].astype(F32)
            v = p_ref[rows, GMW:2 * GMW].astype(F32)
            au, _, _, _, _, _, mixed, _ = _gm_chunk_fwd(u, v, lng_ref[...], lnb_ref[...], ws_ref, bias_ref[...])
            y_ref[rows, :] = (au * mixed).astype(BF16)

    return pl.pallas_call(
        body, grid=(t // tm,), name=name,
        in_specs=[pl.BlockSpec((tm, 2 * GMW), lambda i: (i, 0)),
                  pl.BlockSpec((None, 1, GMW), lambda i: (l, 0, 0)),
                  pl.BlockSpec((None, 1, GMW), lambda i: (l, 0, 0)),
                  pl.BlockSpec((None, HEADS, C, C), lambda i: (l, 0, 0, 0)),
                  pl.BlockSpec((None, C, GMW), lambda i: (l, 0, 0))],
        out_specs=pl.BlockSpec((tm, GMW), lambda i: (i, 0)),
        out_shape=S((t, GMW), BF16),
        compiler_params=_params("parallel"),
    )(proj, lng, lnb, ws_bf, bias)


def _scan_pair(proj, other, col, wf, wb, gcf, gcb, first_is_f, name):
    t = proj.shape[0]
    n = t // C
    sc = SCAN_CHUNKS
    nsteps = n // sc
    other_is_proj = other is None

    def body(a1, o1, a2, o2, w1_ref, w2_ref, out1, out2, st1, st2):
        @pl.when(pl.program_id(0) == 0)
        def _():
            st1[...] = jnp.zeros_like(st1)
            st2[...] = jnp.zeros_like(st2)

        def one(a_ref, o_ref, w_ref, gc, st, out, order):
            for h in range(HEADS):
                sl = slice(h * DH, (h + 1) * DH)
                incs = {}
                for j in order:
                    rows = slice(j * C, (j + 1) * C)
                    aw = (a_ref[rows, sl].astype(F32) * w_ref[h]).astype(BF16)
                    incs[j] = _dot_tn(aw, o_ref[rows, sl].astype(BF16))
                cur = st[h]
                for j in order:
                    out[j, h] = cur.astype(BF16)
                    cur = gc[h] * cur + incs[j]
                st[h] = cur

        g1, g2 = (gcf, gcb) if first_is_f else (gcb, gcf)
        one(a1, o1, w1_ref, g1, st1, out1, range(sc))
        one(a2, o2, w2_ref, g2, st2, out2, range(sc - 1, -1, -1))

    up = lambda i: i
    down = lambda i: nsteps - 1 - i

    def specs(ix):
        o_spec = pl.BlockSpec((sc * C, RETW), lambda i: (ix(i), 3 if other_is_proj else 0))
        return [pl.BlockSpec((sc * C, RETW), lambda i: (ix(i), col)), o_spec]

    const = lambda: pl.BlockSpec((HEADS, C, DH), lambda i: (0, 0, 0))
    oth = proj if other_is_proj else other
    w1, w2 = (wf, wb) if first_is_f else (wb, wf)
    out1, out2 = pl.pallas_call(
        body, grid=(nsteps,), name=name,
        in_specs=specs(up) + specs(down) + [const(), const()],
        out_specs=[pl.BlockSpec((sc, HEADS, DH, DH), lambda i: (up(i), 0, 0, 0)),
                   pl.BlockSpec((sc, HEADS, DH, DH), lambda i: (down(i), 0, 0, 0))],
        out_shape=[S((n, HEADS, DH, DH), BF16), S((n, HEADS, DH, DH), BF16)],
        scratch_shapes=[pltpu.VMEM((HEADS, DH, DH), F32), pltpu.VMEM((HEADS, DH, DH), F32)],
        compiler_params=_params("arbitrary"),
    )(proj, oth, proj, oth, w1, w2)
    return (out1, out2) if first_is_f else (out2, out1)


SCAN_CHUNKS = 4


RET_CHUNKS = 4


def _f_ret_out(proj, s_f, s_b, rc, name):
    t = proj.shape[0]
    tm = RET_CHUNKS * C

    def body(q_ref, k_ref, v_ref, sf_ref, sb_ref, d_ref, xif_ref, xib_ref, o_ref):
        for ci in range(RET_CHUNKS):
            rows = slice(ci * C, (ci + 1) * C)
            for h in range(HEADS):
                sl = slice(h * DH, (h + 1) * DH)
                qh, kh, vh = q_ref[rows, sl], k_ref[rows, sl], v_ref[rows, sl]
                p = (_dot_nt(qh, kh) * d_ref[h]).astype(BF16)
                cross = _dot(qh, jnp.concatenate([sf_ref[ci, h], sb_ref[ci, h]], axis=1))
                o_ref[rows, sl] = _dot(p, vh) + xif_ref[h] * cross[:, 0:DH] + xib_ref[h] * cross[:, DH:2 * DH]

    const = lambda: pl.BlockSpec((HEADS, C, DH), lambda i: (0, 0, 0))
    state = lambda: pl.BlockSpec((RET_CHUNKS, HEADS, DH, DH), lambda i: (i, 0, 0, 0))
    return pl.pallas_call(
        body, grid=(t // tm,), name=name,
        in_specs=[pl.BlockSpec((tm, RETW), lambda i, cb=cb: (i, cb)) for cb in (1, 2, 3)] +
                 [state(), state(), const(), const(), const()],
        out_specs=pl.BlockSpec((tm, RETW), lambda i: (i, 0)),
        out_shape=S((t, RETW), F32),
        compiler_params=_params("parallel"),
    )(proj, proj, proj, s_f, s_b, rc["f"]["D"] + rc["b"]["D"], rc["f"]["XI"], rc["b"]["XI"])


def _conv_halo_specs(t, tm, width, col):
    r = tm // HALO
    last = t // HALO - 1
    return [pl.BlockSpec((HALO, width), lambda i: (jnp.maximum(i * r - 1, 0), col)),
            pl.BlockSpec((tm, width), lambda i: (i, col)),
            pl.BlockSpec((HALO, width), lambda i: (jnp.minimum((i + 1) * r, last), col))]


def _fill_ext(ext, prev, cur, nxt, i, nt, tm):
    ext[0:HALO, :] = jnp.where(i > 0, prev, 0.0)
    ext[HALO:HALO + tm, :] = cur
    ext[HALO + tm:2 * HALO + tm, :] = jnp.where(i < nt - 1, nxt, 0.0)


def _glu(a_ref, g_ref):
    return a_ref[...].astype(F32) * _sigmoid(g_ref[...].astype(F32))


def _shifted_copies(ext, rot, tm):
    rows = tm + 2 * HALO - 8
    for b in range(1, 8):
        rot[b - 1, :, :] = ext[pl.ds(b, rows), :]


def _tap(ext, rot, r0, s, rb):
    a, b = divmod(s, 8)
    return ext[pl.ds(r0 + 8 * a, rb), :] if b == 0 else rot[b - 1, pl.ds(r0 + 8 * a, rb), :]


def _f_conv(proj, cw, cb, lng, lnb, l, name, comm=None):
    t = proj.shape[0]
    tm = 256
    nt = t // tm
    rb = 64

    def body(ap, ac, an, gp, gc, gn, cw_ref, cb_ref, lng_ref, lnb_ref, c_ref, y_ref, hext, hrot):
        i = pl.program_id(0)
        _fill_ext(hext, _glu(ap, gp), _glu(ac, gc), _glu(an, gn), i, nt, tm)
        _shifted_copies(hext, hrot, tm)
        for r0 in range(0, tm, rb):
            acc = jnp.zeros((rb, CVW), F32) + cb_ref[...]
            for j in range(KW):
                acc = acc + cw_ref[j:j + 1, :] * _tap(hext, hrot, r0, j + 1, rb)
            c_ref[r0:r0 + rb, :] = acc
            chat, _ = _standardize(acc)
            z = chat * lng_ref[...] + lnb_ref[...]
            y_ref[r0:r0 + rb, :] = (z * _sigmoid(z)).astype(BF16)

    vec = lambda: pl.BlockSpec((None, 1, CVW), lambda i: (l, 0, 0))
    return _launch(
        body, (nt,),
        _conv_halo_specs(t, tm, CVW, 10) + _conv_halo_specs(t, tm, CVW, 11) +
        [pl.BlockSpec((None, 32, CVW), lambda i: (l, 0, 0)), vec(), vec(), vec()],
        [pl.BlockSpec((tm, CVW), lambda i: (i, 0)), pl.BlockSpec((tm, CVW), lambda i: (i, 0))],
        [S((t, CVW), F32), S((t, CVW), BF16)],
        [pltpu.VMEM((tm + 2 * HALO, CVW), F32), pltpu.VMEM((7, tm + 2 * HALO - 8, CVW), F32)],
        (proj, proj, proj, proj, proj, proj, cw, cb, lng, lnb), name, ("parallel",), comm)


def _f_mixout(x, y_gm, y_cv, o, proj, w, name):
    t = x.shape[0]
    tm = 512

    def body(x_ref, ygm_ref, ycv_ref, o_ref, g_ref, w_ref, xm_ref, ycat_t_ref, ycat):
        ycat[:, 0:GMW] = ygm_ref[...]
        ycat[:, GMW + RETW:D] = ycv_ref[...]
        for h in range(HEADS):
            sl = slice(h * DH, (h + 1) * DH)
            ohat, _ = _standardize(o_ref[:, sl])
            g = g_ref[:, sl].astype(F32)
            ycat[:, GMW + h * DH:GMW + (h + 1) * DH] = (ohat * (g * _sigmoid(g))).astype(BF16)
        yc = ycat[...]
        ycat_t_ref[...] = yc.T
        xm_ref[...] = x_ref[...] + _dot(yc, w_ref[...])

    return pl.pallas_call(
        body, grid=(t // tm,), name=name,
        in_specs=[pl.BlockSpec((tm, D), lambda i: (i, 0)),
                  pl.BlockSpec((tm, GMW), lambda i: (i, 0)),
                  pl.BlockSpec((tm, CVW), lambda i: (i, 0)),
                  pl.BlockSpec((tm, RETW), lambda i: (i, 0)),
                  pl.BlockSpec((tm, RETW), lambda i: (i, 4)),
                  _resident((D, D), lambda i: (0, 0))],
        out_specs=[pl.BlockSpec((tm, D), lambda i: (i, 0)), pl.BlockSpec((D, tm), lambda i: (0, i))],
        out_shape=[S((t, D), F32), S((D, t), BF16)],
        scratch_shapes=[pltpu.VMEM((tm, D), BF16)],
        compiler_params=_params("parallel"),
    )(x, y_gm, y_cv, o, proj, w)


def _f_ffn(xm, g2, w1, w2, l, name, comm=None):
    t = xm.shape[0]
    tm = 512
    half = NDEV // 2

    def body(x_ref, g_ref, w1_ref, w2_ref, xo_ref, h_ref, gu_ref, act_t_ref):
        xv = x_ref[...]
        h = (xv * _rms(xv) * g_ref[...]).astype(BF16)
        h_ref[...] = h
        acc = xv
        for j in range(half):
            gate = _dot_nt(h, w1_ref[j])
            up = _dot_nt(h, w1_ref[half + j])
            gu_ref[j] = gate.astype(BF16)
            gu_ref[half + j] = up.astype(BF16)
            a = ((gate * _sigmoid(gate)) * up).astype(BF16)
            act_t_ref[j] = a.T
            acc = acc + _dot(a, w2_ref[j * FFB:(j + 1) * FFB, :])
        xo_ref[...] = acc

    return _launch(
        body, (t // tm,),
        [pl.BlockSpec((tm, D), lambda i: (i, 0)),
         pl.BlockSpec((None, 1, D), lambda i: (l, 0, 0)),
         _resident((NDEV, FFB, D), lambda i: (0, 0, 0)),
         _resident((FFH, D), lambda i: (0, 0))],
        [pl.BlockSpec((tm, D), lambda i: (i, 0)), pl.BlockSpec((tm, D), lambda i: (i, 0)),
         pl.BlockSpec((NDEV, tm, FFB), lambda i: (0, i, 0)), pl.BlockSpec((half, FFB, tm), lambda i: (0, 0, i))],
        [S((t, D), F32), S((t, D), BF16), S((NDEV, t, FFB), BF16), S((half, FFB, t), BF16)],
        [], (xm, g2, w1, w2), name, ("parallel",), comm)


def _b_ffn(top, xm, g2, gu, w1, w2, l, name, comm=None):
    t = xm.shape[0]
    tm = 256
    half = NDEV // 2
    from_loss = isinstance(top, tuple)
    n_top = 3 if from_loss else 1

    def body(*refs):
        top_refs = refs[:n_top]
        x_ref, g_ref, gu_ref, w1_ref, w2_ref, dgu_ref, dxm_ref, dxb_ref, dg_ref = refs[n_top:n_top + 9]
        first = pl.program_id(0) == 0

        @pl.when(first)
        def _():
            dg_ref[...] = jnp.zeros_like(dg_ref)

        if from_loss:
            xo_ref, fg_ref, t_ref = top_refs
            loss_ref, dfg_ref = refs[n_top + 9:]

            @pl.when(first)
            def _():
                loss_ref[...] = jnp.zeros_like(loss_ref)
                dfg_ref[...] = jnp.zeros_like(dfg_ref)

            xo = xo_ref[...]
            ro = _rms(xo)
            xr = xo * ro
            err = xr * fg_ref[...] - t_ref[...]
            loss_ref[...] += (0.5 / D) * _col_sum(jnp.sum(err * err, axis=1, keepdims=True))
            dy = err * (1.0 / D)
            dfg_ref[...] += _col_sum(dy * xr)
            dxo = _rmsnorm_bwd(dy, xo, ro, fg_ref[...])
        else:
            dxo = top_refs[0][...]
        dxb = dxo.astype(BF16)
        dxb_ref[...] = dxb
        dh = jnp.zeros((tm, D), F32)
        for j in range(half):
            dact = _dot_nt(dxb, w2_ref[j * FFB:(j + 1) * FFB, :])
            gate = gu_ref[j].astype(F32)
            up = gu_ref[half + j].astype(F32)
            sg, dsg = _silu_and_grad(gate)
            dgate = (dact * up * dsg).astype(BF16)
            dup = (dact * sg).astype(BF16)
            dgu_ref[j] = dgate.T
            dgu_ref[half + j] = dup.T
            dh = dh + _dot(dgate, w1_ref[j]) + _dot(dup, w1_ref[half + j])
        xv = x_ref[...]
        r = _rms(xv)
        dg_ref[...] += _col_sum(dh * xv * r)
        dxm_ref[...] = dxo + _rmsnorm_bwd(dh, xv, r, g_ref[...])

    tok = lambda: pl.BlockSpec((tm, D), lambda i: (i, 0))
    vec = lambda: pl.BlockSpec((1, D), lambda i: (0, 0))
    top_specs = [tok(), vec(), tok()] if from_loss else [tok()]
    extra_specs = [pl.BlockSpec((1, 1), lambda i: (0, 0)), vec()] if from_loss else []
    extra_shape = [S((1, 1), F32), S((1, D), F32)] if from_loss else []
    return _launch(
        body, (t // tm,),
        top_specs + [tok(), pl.BlockSpec((None, 1, D), lambda i: (l, 0, 0)),
                     pl.BlockSpec((NDEV, tm, FFB), lambda i: (0, i, 0)),
                     _resident((NDEV, FFB, D), lambda i: (0, 0, 0)),
                     _resident((FFH, D), lambda i: (0, 0))],
        [pl.BlockSpec((NDEV, FFB, tm), lambda i: (0, 0, i)), tok(), tok(), vec()] + extra_specs,
        [S((NDEV, FFB, t), BF16), S((t, D), F32), S((t, D), BF16), S((1, D), F32)] + extra_shape,
        [], ((*top,) if from_loss else (top,)) + (xm, g2, gu, w1, w2), name, ("arbitrary",), comm)


def _mm_wgrad(at, b, pieces, at_lead, b_mode, group, name):
    bt = 2048
    t = at.shape[-1]
    bt = min(bt, t)
    nt = t // bt
    ka = at.shape[-2]
    if at_lead:
        a_spec = pl.BlockSpec((None, ka, bt), lambda j, tt: (j, 0, tt))
    else:
        a_spec = pl.BlockSpec((ka, bt), lambda j, tt: (0, tt))
    if b_mode == "shared":
        nb, b_spec = b.shape[1], pl.BlockSpec((bt, b.shape[1]), lambda j, tt: (tt, 0))
    elif b_mode == "cols":
        nb = b.shape[1] // pieces
        b_spec = pl.BlockSpec((bt, group * nb), lambda j, tt: (tt, j))
    else:
        nb, b_spec = b.shape[2], pl.BlockSpec((None, bt, b.shape[2]), lambda j, tt: (j, tt, 0))
    assert group == 1 or b_mode == "cols"

    def body(a_ref, b_ref, o_ref, acc):
        tt = pl.program_id(1)

        @pl.when(tt == 0)
        def _():
            acc[...] = jnp.zeros_like(acc)

        acc[...] += _dot(a_ref[...], b_ref[...])

        @pl.when(tt == nt - 1)
        def _():
            for k in range(group):
                o_ref[k] = acc[:, k * nb:(k + 1) * nb].astype(BF16)

    return pl.pallas_call(
        body, grid=(pieces // group, nt), name=name,
        in_specs=[a_spec, b_spec],
        out_specs=pl.BlockSpec((group, ka, nb), lambda j, tt: (j, 0, 0)),
        out_shape=S((pieces, ka, nb), BF16),
        scratch_shapes=[pltpu.VMEM((ka, group * nb), F32)],
        compiler_params=_params("parallel", "arbitrary"),
    )(at, b)


def _b_mixout(dxm, w, o, proj, c, lng, lnb, l, name, comm=None):
    t = dxm.shape[0]
    tm = 256

    def body(dxm_ref, w_ref, o_ref, g_ref, c_ref, lng_ref, lnb_ref,
             dxb_ref, dygm_ref, dO_ref, dg_ref, dc_ref, dlg_ref, dlb_ref, dcb_ref):
        @pl.when(pl.program_id(0) == 0)
        def _():
            dlg_ref[...] = jnp.zeros_like(dlg_ref)
            dlb_ref[...] = jnp.zeros_like(dlb_ref)
            dcb_ref[...] = jnp.zeros_like(dcb_ref)

        dxb = dxm_ref[...].astype(BF16)
        dxb_ref[...] = dxb
        dy = _dot_nt(dxb, w_ref[...])
        dygm_ref[...] = dy[:, 0:GMW]
        for h in range(HEADS):
            sl = slice(h * DH, (h + 1) * DH)
            ohat, rstd = _standardize(o_ref[:, sl])
            sg, dsg = _silu_and_grad(g_ref[:, sl].astype(F32))
            dyr = dy[:, GMW + h * DH:GMW + (h + 1) * DH]
            dg_ref[:, sl] = (dyr * ohat * dsg).astype(BF16)
            dO_ref[:, sl] = _standardize_bwd(dyr * sg, ohat, rstd)
        chat, rstd = _standardize(c_ref[...])
        z = chat * lng_ref[...] + lnb_ref[...]
        _, dsz = _silu_and_grad(z)
        dz = dy[:, GMW + RETW:D] * dsz
        dlg_ref[...] += _col_sum(dz * chat)
        dlb_ref[...] += _col_sum(dz)
        dc = _standardize_bwd(dz * lng_ref[...], chat, rstd)
        dcb_ref[...] += _col_sum(dc)
        dc_ref[...] = dc

    vec = lambda: pl.BlockSpec((None, 1, CVW), lambda i: (l, 0, 0))
    acc = lambda: pl.BlockSpec((1, CVW), lambda i: (0, 0))
    return _launch(
        body, (t // tm,),
        [pl.BlockSpec((tm, D), lambda i: (i, 0)),
         _resident((D, D), lambda i: (0, 0)),
         pl.BlockSpec((tm, RETW), lambda i: (i, 0)),
         pl.BlockSpec((tm, RETW), lambda i: (i, 4)),
         pl.BlockSpec((tm, CVW), lambda i: (i, 0)), vec(), vec()],
        [pl.BlockSpec((tm, D), lambda i: (i, 0)), pl.BlockSpec((tm, GMW), lambda i: (i, 0)),
         pl.BlockSpec((tm, RETW), lambda i: (i, 0)), pl.BlockSpec((tm, RETW), lambda i: (i, 0)),
         pl.BlockSpec((tm, CVW), lambda i: (i, 0)), acc(), acc(), acc()],
        [S((t, D), BF16), S((t, GMW), F32), S((t, RETW), F32), S((t, RETW), BF16), S((t, CVW), F32),
         S((1, CVW), F32), S((1, CVW), F32), S((1, CVW), F32)],
        [], (dxm, w, o, proj, c, lng, lnb), name, ("arbitrary",), comm)


def _b_gm(proj, dy, lng, lnb, ws_bf, wst_bf, bias, l, name, comm=None):
    t = proj.shape[0]
    tm = 512
    nt = t // tm

    def body(p_ref, dy_ref, lng_ref, lnb_ref, ws_ref, wst_ref, bias_ref,
             duv_ref, dws_ref, dbias_ref, dbs_ref, dlg_ref, dlb_ref):
        @pl.when(pl.program_id(0) == 0)
        def _():
            dws_ref[...] = jnp.zeros_like(dws_ref)
            dbias_ref[...] = jnp.zeros_like(dbias_ref)
            dbs_ref[...] = jnp.zeros_like(dbs_ref)
            dlg_ref[...] = jnp.zeros_like(dlg_ref)
            dlb_ref[...] = jnp.zeros_like(dlb_ref)

        for ci in range(tm // C):
            rows = slice(ci * C, (ci + 1) * C)
            u = p_ref[rows, 0:GMW].astype(F32)
            v = p_ref[rows, GMW:2 * GMW].astype(F32)
            au, dau, dav, vhat, rstd, vn, mixed, head = _gm_chunk_fwd(u, v, lng_ref[...], lnb_ref[...], ws_ref, bias_ref[...])
            dyc = dy_ref[rows, :]
            dmixed = dyc * au
            dmb = dmixed.astype(BF16)
            dbias_ref[...] += dmixed
            dvn = jnp.zeros((C, GMW), F32)
            for h in range(HEADS):
                dws_ref[h] += _dot_nt(jnp.where(head == h, dmixed, 0.0).astype(BF16), vn)
                dvn = dvn + jnp.where(head == h, _dot(wst_ref[h], dmb), 0.0)
            dlg_ref[...] += _col_sum(dvn * vhat)
            dlb_ref[...] += _col_sum(dvn)
            dav_in = _standardize_bwd(dvn * lng_ref[...], vhat, rstd)
            duv_ref[rows, 0:GMW] = (dyc * mixed * dau).astype(BF16)
            duv_ref[rows, GMW:2 * GMW] = (dav_in * dav).astype(BF16)

        @pl.when(pl.program_id(0) == nt - 1)
        def _():
            head = lax.broadcasted_iota(jnp.int32, (C, GMW), 1) // (GMW // HEADS)
            lane = lax.broadcasted_iota(jnp.int32, (C, 128), 1)
            fold = jnp.zeros((C, 128), F32)
            for h in range(HEADS):
                col = jnp.sum(jnp.where(head == h, dbias_ref[...], 0.0), axis=1, keepdims=True)
                fold = jnp.where(lane == h, col, fold)
            dbs_ref[...] = fold

    vec = lambda: pl.BlockSpec((None, 1, GMW), lambda i: (l, 0, 0))
    mats = lambda: pl.BlockSpec((None, HEADS, C, C), lambda i: (l, 0, 0, 0))
    return _launch(
        body, (nt,),
        [pl.BlockSpec((tm, 2 * GMW), lambda i: (i, 0)), pl.BlockSpec((tm, GMW), lambda i: (i, 0)),
         vec(), vec(), mats(), mats(), pl.BlockSpec((None, C, GMW), lambda i: (l, 0, 0))],
        [pl.BlockSpec((tm, 2 * GMW), lambda i: (i, 0)),
         pl.BlockSpec((HEADS, C, C), lambda i: (0, 0, 0)),
         pl.BlockSpec((C, GMW), lambda i: (0, 0)), pl.BlockSpec((C, 128), lambda i: (0, 0)),
         pl.BlockSpec((1, GMW), lambda i: (0, 0)), pl.BlockSpec((1, GMW), lambda i: (0, 0))],
        [S((t, 2 * GMW), BF16), S((HEADS, C, C), F32), S((C, GMW), F32), S((C, 128), F32),
         S((1, GMW), F32), S((1, GMW), F32)],
        [], (proj, dy, lng, lnb, ws_bf, wst_bf, bias), name, ("arbitrary",), comm)


def _b_conv(proj, dc, cw, l, name, comm=None):
    t = proj.shape[0]
    tm = 256
    nt = t // tm
    rb = 64

    def body(ap, ac, an, gp, gc, gn, dp, dcur, dn, cw_ref, dag_ref, dcw_ref, hext, dext, hrot, drot):
        i = pl.program_id(0)

        @pl.when(i == 0)
        def _():
            dcw_ref[...] = jnp.zeros_like(dcw_ref)

        _fill_ext(hext, _glu(ap, gp), _glu(ac, gc), _glu(an, gn), i, nt, tm)
        _fill_ext(dext, dp[...], dcur[...], dn[...], i, nt, tm)
        _shifted_copies(hext, hrot, tm)
        _shifted_copies(dext, drot, tm)
        for j in range(KW):
            dcw_ref[j:j + 1, :] += _col_sum(dcur[...] * _tap(hext, hrot, 0, j + 1, tm))
        for r0 in range(0, tm, rb):
            dh = jnp.zeros((rb, CVW), F32)
            for j in range(KW):
                dh = dh + cw_ref[j:j + 1, :] * _tap(dext, drot, r0, 2 * HALO - 1 - j, rb)
            a = ac[r0:r0 + rb, :].astype(F32)
            s = _sigmoid(gc[r0:r0 + rb, :].astype(F32))
            dag_ref[r0:r0 + rb, 0:CVW] = (dh * s).astype(BF16)
            dag_ref[r0:r0 + rb, CVW:2 * CVW] = (dh * a * s * (1.0 - s)).astype(BF16)

    dspecs = _conv_halo_specs(t, tm, CVW, 0)
    return _launch(
        body, (nt,),
        _conv_halo_specs(t, tm, CVW, 10) + _conv_halo_specs(t, tm, CVW, 11) + dspecs +
        [pl.BlockSpec((None, 32, CVW), lambda i: (l, 0, 0))],
        [pl.BlockSpec((tm, 2 * CVW), lambda i: (i, 0)), pl.BlockSpec((32, CVW), lambda i: (0, 0))],
        [S((t, 2 * CVW), BF16), S((32, CVW), F32)],
        [pltpu.VMEM((tm + 2 * HALO, CVW), F32), pltpu.VMEM((tm + 2 * HALO, CVW), F32),
         pltpu.VMEM((7, tm + 2 * HALO - 8, CVW), F32), pltpu.VMEM((7, tm + 2 * HALO - 8, CVW), F32)],
        (proj, proj, proj, proj, proj, proj, dc, dc, dc, cw), name, ("arbitrary",), comm)


def _b_ret_out(proj, cos2, sin2, dO, s_f, s_b, g_f, g_b, rc, name, comm=None):
    t = proj.shape[0]
    tm = RET_CHUNKS * C

    def body(q_ref, k_ref, v_ref, cos_ref, sin_ref, dO_ref, sf_ref, sb_ref, gf_ref, gb_ref,
             d_ref, xif_ref, xib_ref, zef_ref, zeb_ref, dq_ref, dk_ref, dv_ref):
        for ci in range(RET_CHUNKS):
            rows = slice(ci * C, (ci + 1) * C)
            cos_v, sin_v = cos_ref[rows, :], sin_ref[rows, :]
            for h in range(HEADS):
                sl = slice(h * DH, (h + 1) * DH)
                qh, kh, vh = q_ref[rows, sl], k_ref[rows, sl], v_ref[rows, sl]
                dOh = dO_ref[rows, sl].astype(BF16)
                dm = d_ref[h]
                p = (_dot_nt(qh, kh) * dm).astype(BF16)
                dp = (_dot_nt(dOh, vh) * dm).astype(BF16)
                from_s = _dot_nt(dOh, jnp.concatenate([sf_ref[ci, h], sb_ref[ci, h]], axis=0))
                from_g = _dot_nt(vh, jnp.concatenate([gf_ref[ci, h], gb_ref[ci, h]], axis=0))
                kg = _dot(kh, jnp.concatenate([gf_ref[ci, h], gb_ref[ci, h]], axis=1))
                dqr = _dot(dp, kh) + xif_ref[h] * from_s[:, 0:DH] + xib_ref[h] * from_s[:, DH:2 * DH]
                dkr = (_dot_tn(dp, qh) + zef_ref[h] * from_g[:, 0:DH] + zeb_ref[h] * from_g[:, DH:2 * DH]) * SCALE
                dv = _dot_tn(p, dOh) + zef_ref[h] * kg[:, 0:DH] + zeb_ref[h] * kg[:, DH:2 * DH]
                dq_ref[rows, sl] = _rot_t(dqr, cos_v, sin_v).astype(BF16)
                dk_ref[rows, sl] = _rot_t(dkr, cos_v, sin_v).astype(BF16)
                dv_ref[rows, sl] = dv.astype(BF16)

    const = lambda: pl.BlockSpec((HEADS, C, DH), lambda i: (0, 0, 0))
    state = lambda: pl.BlockSpec((RET_CHUNKS, HEADS, DH, DH), lambda i: (i, 0, 0, 0))
    tok = lambda: pl.BlockSpec((tm, RETW), lambda i: (i, 0))
    return _launch(
        body, (t // tm,),
        [pl.BlockSpec((tm, RETW), lambda i, cb=cb: (i, cb)) for cb in (1, 2, 3)] +
        [pl.BlockSpec((tm, DH), lambda i: (i, 0)), pl.BlockSpec((tm, DH), lambda i: (i, 0)), tok(),
         state(), state(), state(), state()] + [const() for _ in range(5)],
        [tok(), tok(), tok()],
        [S((t, RETW), BF16) for _ in range(3)], [],
        (proj, proj, proj, cos2, sin2, dO, s_f, s_b, g_f, g_b, rc["f"]["D"] + rc["b"]["D"],
         rc["f"]["XI"], rc["b"]["XI"], rc["f"]["ZETA"], rc["b"]["ZETA"]), name, ("parallel",), comm)


def _b_inproj(d_uv, dqkv, d_g, d_ag, w, x, g1, dxm, l, name, comm=None):
    t = x.shape[0]
    tm = 512

    def body(duv_ref, dq_ref, dk_ref, dv_ref, dg_ref, dag_ref, w_ref, x_ref, g_ref, dxm_ref,
             dp_ref, dx_ref, dn_ref):
        @pl.when(pl.program_id(0) == 0)
        def _():
            dn_ref[...] = jnp.zeros_like(dn_ref)

        for k, part in enumerate((duv_ref, dq_ref, dk_ref, dv_ref, dg_ref, dag_ref)):
            dp_ref[:, 512 * k:512 * (k + 1)] = part[...]
        dh = _dot_nt(dp_ref[...], w_ref[...])
        xv = x_ref[...]
        r = _rms(xv)
        dn_ref[...] += _col_sum(dh * xv * r)
        dx_ref[...] = dxm_ref[...] + _rmsnorm_bwd(dh, xv, r, g_ref[...])

    half = lambda: pl.BlockSpec((tm, 512), lambda i: (i, 0))
    full = lambda: pl.BlockSpec((tm, D), lambda i: (i, 0))
    return _launch(
        body, (t // tm,),
        [half() for _ in range(6)] +
        [_resident((D, INW), lambda i: (0, 0)), full(), pl.BlockSpec((None, 1, D), lambda i: (l, 0, 0)), full()],
        [pl.BlockSpec((tm, INW), lambda i: (i, 0)), full(), pl.BlockSpec((1, D), lambda i: (0, 0))],
        [S((t, INW), BF16), S((t, D), F32), S((1, D), F32)], [],
        (d_uv, *dqkv, d_g, d_ag, w, x, g1, dxm), name, ("arbitrary",), comm)


class _PairSwap:
    def __init__(self, units):
        self.inputs = list(units)
        self.out_shape = [S((4,) + u.shape[1:], BF16) for u in units]
        n = len(units)
        self.scratch = [pltpu.SemaphoreType.DMA((n, 4)), pltpu.SemaphoreType.DMA((n, 4))]

    def run(self, phase, ins, outs, scr):
        ssem, rsem = scr
        x, y, c, _ = _place()
        copies = lambda: [pltpu.make_async_remote_copy(src_ref=ins[u].at[2 * chip + (1 - c)], dst_ref=outs[u].at[chip],
                                                       send_sem=ssem.at[u, chip], recv_sem=rsem.at[u, chip],
                                                       device_id=(x, y, 1 - c), device_id_type=MESH)
                          for u in range(len(self.inputs)) for chip in range(4)]
        if phase == "start":
            for cp in copies():
                cp.start()
        elif phase == "finish":
            for cp in copies():
                cp.wait_recv()
            for cp in copies():
                cp.wait_send()


def _pair_add(g, q, name):
    _, mm, nn = g.shape
    bm = _row_block(mm)

    def body(c_ref, g_ref, q_ref, h_ref):
        h_ref[...] = (g_ref[...].astype(F32) + q_ref[...].astype(F32)).astype(BF16)

    blk = lambda: pl.BlockSpec((None, bm, nn), lambda qq, i, c_ref: (qq, i, 0))
    return pl.pallas_call(
        body, name=name,
        grid_spec=pltpu.PrefetchScalarGridSpec(
            num_scalar_prefetch=1, grid=(4, mm // bm),
            in_specs=[pl.BlockSpec((None, None, bm, nn), lambda qq, i, c_ref: (qq, c_ref[0], i, 0)), blk()],
            out_specs=blk()),
        out_shape=S((4, mm, nn), BF16),
        compiler_params=_params("parallel", "parallel"),
    )(lax.axis_index("c").astype(jnp.int32).reshape(1), g.reshape(4, 2, mm, nn), q)


_BIG = ("w_in", "w_out", "w_ffn_in", "w_ffn_out")
_KIND = dict(w_in="cols", w_out="rows", w_ffn_in="lead", w_ffn_out="rows")
CWP = 128
EARLY_ROWS, LATE_ROWS = 152, 8


def _step(x, tgt, wts, sh, cw_pad):
    t = x.shape[0]
    rc = _ret_consts()
    cos2, sin2 = (jnp.asarray(a) for a in _rope_tables(t))
    n1 = wts["norm1_g"].reshape(LAYERS, 1, D)
    n2 = wts["norm2_g"].reshape(LAYERS, 1, D)
    gm_lng = wts["gm_ln_g"].reshape(LAYERS, 1, GMW)
    gm_lnb = wts["gm_ln_b"].reshape(LAYERS, 1, GMW)
    ws_bf = wts["gm_ws"].astype(BF16)
    wst_bf = jnp.swapaxes(wts["gm_ws"], 2, 3).astype(BF16)
    bias = jnp.repeat(jnp.swapaxes(wts["gm_bs"], 1, 2), GMW // HEADS, axis=2)
    cb = wts["conv_b"].reshape(LAYERS, 1, CVW)
    cv_lng = wts["conv_ln_g"].reshape(LAYERS, 1, CVW)
    cv_lnb = wts["conv_ln_b"].reshape(LAYERS, 1, CVW)
    unit = lambda f, l: (sh[f][l], _KIND[f])
    cshard = CVW // NDEV

    full = {f: [None] * LAYERS for f in _BIG}
    full["w_in"][0], cw_all = _comm_only(_Gather([unit("w_in", 0), (cw_pad, "lead")]), "gather_first")
    cw = jnp.transpose(cw_all[:, :, :, :cshard], (1, 2, 0, 3)).reshape(LAYERS, 32, CVW)

    gcf, gcb = rc["f"]["gC"], rc["b"]["gC"]
    saved = []
    for l in range(LAYERS):
        first = l == 0
        (proj, h1), got = _f_inproj(x, n1, full["w_in"][l], cos2, sin2, l, f"f_inproj_{l}",
                                    _Gather([unit("w_ffn_in", 0)]) if first else None)
        if first:
            full["w_ffn_in"][0], = got
        y_gm = _f_gm(proj, gm_lng, gm_lnb, ws_bf, bias, l, f"f_gm_{l}")
        s_f, s_b = _scan_pair(proj, None, 2, rc["f"]["ZETA"], rc["b"]["ZETA"], gcf, gcb, True, f"f_ret_state_{l}")
        o = _f_ret_out(proj, s_f, s_b, rc, f"f_ret_out_{l}")
        (c, y_cv), got = _f_conv(proj, cw, cb, cv_lng, cv_lnb, l, f"f_conv_{l}",
                                 _Gather([unit("w_out", 0), unit("w_ffn_out", 0)]) if first else None)
        if first:
            full["w_out"][0], full["w_ffn_out"][0] = got
        xm, ycat = _f_mixout(x, y_gm, y_cv, o, proj, full["w_out"][l], f"f_mixout_{l}")
        (xo, h2, gu, act), got = _f_ffn(xm, n2, full["w_ffn_in"][l], full["w_ffn_out"][l], l, f"f_ffn_{l}",
                                        _Gather([unit(f, 1) for f in _BIG]) if first else None)
        if first:
            full["w_in"][1], full["w_out"][1], full["w_ffn_in"][1], full["w_ffn_out"][1] = got
        saved.append(dict(x=x, proj=proj, h1=h1, o=o, s_f=s_f, s_b=s_b, c=c, xm=xm, ycat=ycat, h2=h2, gu=gu, act=act))
        x = xo

    parts = {f: [None] * LAYERS for f in _BIG}
    small = [None] * LAYERS
    norm1 = [None] * LAYERS
    upper = None
    top = (x, wts["final_g"].reshape(1, D), tgt)
    for l in reversed(range(LAYERS)):
        sv = saved[l]
        outs, got = _b_ffn(top, sv["xm"], n2, sv["gu"], full["w_ffn_in"][l], full["w_ffn_out"][l], l,
                           f"b_ffn_{l}", _Scatter(upper) if upper else None)
        dgu_t, dxm, dxo_bf, d_n2 = outs[:4]
        if l == LAYERS - 1:
            loss, d_final = outs[4:]
        if upper:
            for f, p in zip(_BIG, got):
                parts[f][l + 1] = p
        g_f2 = _mm_wgrad(sv["act"], dxo_bf, NDEV // 2, True, "shared", 1, f"g_ffn_out_{l}").reshape(NDEV, FFH // NDEV, D)
        g_f1 = _mm_wgrad(dgu_t, sv["h2"], NDEV, True, "shared", 1, f"g_ffn_in_{l}")
        (dxm_bf, dy_gm, dO, d_g, dc, d_cvlg, d_cvlb, d_cb), (q_f1, q_f2) = _b_mixout(
            dxm, full["w_out"][l], sv["o"], sv["proj"], sv["c"], cv_lng, cv_lnb, l, f"b_mixout_{l}", _PairSwap([g_f1, g_f2]))
        h_f1 = _pair_add(g_f1, q_f1, f"pair_add_w_ffn_in_{l}")
        h_f2 = _pair_add(g_f2, q_f2, f"pair_add_w_ffn_out_{l}")
        g_out = _mm_wgrad(sv["ycat"], dxm_bf, 1, False, "shared", 1, f"g_out_{l}").reshape(NDEV, D // NDEV, D)
        last = l == 0
        (d_uv, d_ws, _, d_bs_fold, d_gmlg, d_gmlb), (q_out,) = _b_gm(
            sv["proj"], dy_gm, gm_lng, gm_lnb, ws_bf, wst_bf, bias, l, f"b_gm_{l}", _PairSwap([g_out]))
        h_out = _pair_add(g_out, q_out, f"pair_add_w_out_{l}")
        (d_ag, d_cw), got = _b_conv(sv["proj"], dc, cw, l, f"b_conv_{l}", _Scatter([h_f1]) if last else None)
        if last:
            parts["w_ffn_in"][l], = got
        small[l] = dict(gm_ln_g=d_gmlg[0], gm_ln_b=d_gmlb[0], gm_ws=d_ws, gm_bs=d_bs_fold[:, :HEADS].T, conv_w=d_cw[:KW],
                        conv_b=d_cb[0], conv_ln_g=d_cvlg[0], conv_ln_b=d_cvlb[0], norm2_g=d_n2[0])
        comm = None
        if last:
            early_g = {k: jnp.stack([small[ll][k] for ll in range(LAYERS)]) for k in small[0]}
            early_g["final_g"] = d_final[0]
            early_buf = _pack([early_g[k] for k in _SMALL_EARLY], EARLY_ROWS)
            comm = _Comms([_Scatter([h_out, h_f2]), _Gather([(early_buf, "lead")])])
        g_f, g_b = _scan_pair(sv["proj"], dO, 1, rc["f"]["XI"], rc["b"]["XI"], gcf, gcb, False, f"b_ret_state_{l}")
        dqkv, got = _b_ret_out(sv["proj"], cos2, sin2, dO, sv["s_f"], sv["s_b"], g_f, g_b, rc, f"b_ret_out_{l}", comm)
        if last:
            parts["w_out"][l], parts["w_ffn_out"][l], early_parts = got
        (dproj, top, d_n1), _ = _b_inproj(d_uv, dqkv, d_g, d_ag, full["w_in"][l], sv["x"], n1, dxm, l, f"b_inproj_{l}")
        norm1[l] = d_n1[0]
        g_in = _mm_wgrad(sv["h1"], dproj, NDEV, False, "cols", 2, f"g_in_{l}")
        q_in, = _comm_only(_PairSwap([g_in]), f"pair_swap_w_in_{l}")
        h_in = _pair_add(g_in, q_in, f"pair_add_w_in_{l}")
        if last:
            tail = [h_in]
        else:
            upper = [h_in, h_out, h_f1, h_f2]
    late_buf = _pack([jnp.stack(norm1)], LATE_ROWS)
    parts["w_in"][0], late_parts = _comm_only(_Comms([_Scatter(tail), _Gather([(late_buf, "lead")])]), "exchange_last")
    return loss, top, parts, (early_parts, late_parts)


def _adamw(w, g, m, v):
    m = ADAM_B1 * m + (1.0 - ADAM_B1) * g
    v = ADAM_B2 * v + (1.0 - ADAM_B2) * (g * g)
    m_hat = m / (1.0 - ADAM_B1 ** ADAM_STEP)
    v_hat = v / (1.0 - ADAM_B2 ** ADAM_STEP)
    return -ADAM_LR * (m_hat / (jnp.sqrt(v_hat) + ADAM_EPS) + ADAM_WD * w), m, v


def _cast_blocks(ws):
    def body(*refs):
        ins, outs = refs[:len(ws)], refs[len(ws):]
        for k, src in enumerate(ins):
            for l in range(LAYERS):
                outs[k * LAYERS + l][...] = src[l].astype(BF16)

    outs = pl.pallas_call(body, name="cast_blocks", out_shape=[S(w.shape[1:], BF16) for w in ws for _ in range(LAYERS)],
                          compiler_params=_params())(*ws)
    return [list(outs[k * LAYERS:(k + 1) * LAYERS]) for k in range(len(ws))]


def _row_block(mm):
    return next(b for b in (256, 352, 128) if mm % b == 0)


def _sum_adam(parts, w, m, v, l, prev, name):
    _, mm, nn = parts.shape
    bm = _row_block(mm)

    def body(p_ref, w_ref, m_ref, v_ref, *rest):
        g_ref, d_ref, nm_ref, nv_ref = rest[-4:]
        g = p_ref[0].astype(F32)
        for s in range(1, 4):
            g = g + p_ref[s].astype(F32)
        g_ref[...] = g
        d_ref[...], nm_ref[...], nv_ref[...] = _adamw(w_ref[...], g, m_ref[...], v_ref[...])

    blk = lambda: pl.BlockSpec((None, bm, nn), lambda i: (l, i, 0))
    prev = list(prev) if prev else []
    return pl.pallas_call(
        body, grid=(mm // bm,), name=name,
        in_specs=[pl.BlockSpec((4, bm, nn), lambda i: (0, i, 0)), blk(), blk(), blk()] + [_ANY] * len(prev),
        out_specs=[blk() for _ in range(4)],
        out_shape=[S(w.shape, F32) for _ in range(4)],
        input_output_aliases={4 + j: j for j in range(len(prev))},
        compiler_params=_params("parallel"),
    )(parts, w, m, v, *prev)


def _sum_small(parts):
    n = len(parts)

    def body(*refs):
        for p_ref, o_ref in zip(refs[:n], refs[n:]):
            g = p_ref[0]
            for s in range(1, NDEV):
                g = g + p_ref[s]
            o_ref[...] = g

    return pl.pallas_call(body, name="sum_small", out_shape=[S(p.shape[1:], F32) for p in parts],
                          compiler_params=_params())(*parts)


def _adam_small(g, w, m, v):
    def body(g_ref, w_ref, m_ref, v_ref, d_ref, nm_ref, nv_ref):
        d_ref[...], nm_ref[...], nv_ref[...] = _adamw(w_ref[...], g_ref[...], m_ref[...], v_ref[...])

    return pl.pallas_call(body, name="adam_small", out_shape=[S(g.shape, F32)] * 3, compiler_params=_params())(g, w, m, v)


_SMALL = ("norm1_g", "gm_ln_g", "gm_ln_b", "gm_ws", "gm_bs", "conv_w", "conv_b", "conv_ln_g", "conv_ln_b",
          "norm2_g", "final_g")
_SMALL_EARLY = _SMALL[1:]
_NAMES = ("norm1_g", "w_in", "gm_ln_g", "gm_ln_b", "gm_ws", "gm_bs", "conv_w", "conv_b", "conv_ln_g", "conv_ln_b",
          "w_out", "norm2_g", "w_ffn_in", "w_ffn_out", "final_g")


def _pack(parts, rows):
    flat = jnp.concatenate([p.reshape(-1) for p in parts])
    return jnp.pad(flat, (0, rows * 1024 - flat.shape[0])).reshape(rows, 1024)


def _unpack(buf, shapes):
    flat = buf.reshape(-1)
    out, o = [], 0
    for shp in shapes:
        sz = int(np.prod(shp))
        out.append(flat[o:o + sz].reshape(shp))
        o += sz
    return out


def kernel(x, norm1_g, w_in, gm_ln_g, gm_ln_b, gm_ws, gm_bs, conv_w, conv_b, conv_ln_g, conv_ln_b, w_out, norm2_g, w_ffn_in, w_ffn_out, final_g, loss_target, m_norm1_g, m_w_in, m_gm_ln_g, m_gm_ln_b, m_gm_ws, m_gm_bs, m_conv_w, m_conv_b, m_conv_ln_g, m_conv_ln_b, m_w_out, m_norm2_g, m_w_ffn_in, m_w_ffn_out, m_final_g, v_norm1_g, v_w_in, v_gm_ln_g, v_gm_ln_b, v_gm_ws, v_gm_bs, v_conv_w, v_conv_b, v_conv_ln_g, v_conv_ln_b, v_w_out, v_norm2_g, v_w_ffn_in, v_w_ffn_out, v_final_g):
    w = dict(norm1_g=norm1_g, w_in=w_in, gm_ln_g=gm_ln_g, gm_ln_b=gm_ln_b, gm_ws=gm_ws, gm_bs=gm_bs, conv_w=conv_w,
             conv_b=conv_b, conv_ln_g=conv_ln_g, conv_ln_b=conv_ln_b, w_out=w_out, norm2_g=norm2_g, w_ffn_in=w_ffn_in,
             w_ffn_out=w_ffn_out, final_g=final_g)
    mo = dict(norm1_g=m_norm1_g, w_in=m_w_in, gm_ln_g=m_gm_ln_g, gm_ln_b=m_gm_ln_b, gm_ws=m_gm_ws, gm_bs=m_gm_bs,
              conv_w=m_conv_w, conv_b=m_conv_b, conv_ln_g=m_conv_ln_g, conv_ln_b=m_conv_ln_b, w_out=m_w_out,
              norm2_g=m_norm2_g, w_ffn_in=m_w_ffn_in, w_ffn_out=m_w_ffn_out, final_g=m_final_g)
    vo = dict(norm1_g=v_norm1_g, w_in=v_w_in, gm_ln_g=v_gm_ln_g, gm_ln_b=v_gm_ln_b, gm_ws=v_gm_ws, gm_bs=v_gm_bs,
              conv_w=v_conv_w, conv_b=v_conv_b, conv_ln_g=v_conv_ln_g, conv_ln_b=v_conv_ln_b, w_out=v_w_out,
              norm2_g=v_norm2_g, w_ffn_in=v_w_ffn_in, w_ffn_out=v_w_ffn_out, final_g=v_final_g)
    t = x.shape[1]
    me = 4 * lax.axis_index("x") + 2 * lax.axis_index("y") + lax.axis_index("c")
    cshard = conv_w.shape[2]

    cw_pad = jnp.pad(conv_w, ((0, 0), (0, 32 - KW), (0, CWP - cshard)))
    flip = lambda a: jnp.swapaxes(a, 1, 2)
    big = {f: tuple(flip(a[f]) if f == "w_ffn_in" else a[f] for a in (w, mo, vo)) for f in _BIG}
    sh = dict(zip(_BIG, _cast_blocks([big[f][0] for f in _BIG])))
    loss, dx, parts, small_parts = _step(x.reshape(t, D), loss_target.reshape(t, D), w, sh, cw_pad)

    grads, delta, new_m, new_v = {}, {}, {}, {}
    for f in _BIG:
        outs = None
        for l in reversed(range(LAYERS)):
            outs = _sum_adam(parts[f][l], *big[f], l, outs, f"sum_adam_{f}_{l}")
        grads[f], delta[f], new_m[f], new_v[f] = [flip(a) for a in outs] if f == "w_ffn_in" else outs

    early_sum, late_sum = _sum_small(small_parts)
    early_shapes = [(LAYERS, KW, CVW) if k == "conv_w" else w[k].shape for k in _SMALL_EARLY]
    grads["norm1_g"], = _unpack(late_sum, [w["norm1_g"].shape])
    for k, g in zip(_SMALL_EARLY, _unpack(early_sum, early_shapes)):
        grads[k] = lax.dynamic_slice_in_dim(g, me * cshard, cshard, axis=2) if k == "conv_w" else g
    adam_rows = 144
    d_s, m_s, v_s = _adam_small(_pack([grads[k] for k in _SMALL], adam_rows), _pack([w[k] for k in _SMALL], adam_rows),
                                _pack([mo[k] for k in _SMALL], adam_rows), _pack([vo[k] for k in _SMALL], adam_rows))
    shapes = [w[k].shape for k in _SMALL]
    for dst, buf in ((delta, d_s), (new_m, m_s), (new_v, v_s)):
        for k, a in zip(_SMALL, _unpack(buf, shapes)):
            dst[k] = a

    total = lax.psum(loss[0, 0], ("x", "y", "c"))
    return (total, dx.reshape(1, t, D), *[grads[k] for k in _NAMES], *[delta[k] for k in _NAMES],
            *[new_m[k] for k in _NAMES], *[new_v[k] for k in _NAMES])
```

```python
import functools

import numpy as np
import jax
import jax.numpy as jnp
from jax import lax
from jax.experimental import pallas as pl
from jax.experimental.pallas import tpu as pltpu

F32, BF16 = jnp.float32, jnp.bfloat16
S = jax.ShapeDtypeStruct

D = 1024
INW = 3072
GMW = 256
RETW = 512
CVW = 256
HEADS = 4
DH = 128
C = 128
KW = 31
HALO = 16
FFH = 2816
NDEV = 8
FFB = 2 * FFH // NDEV
FF_CHUNKS = ((0, 768), (768, 1536), (1536, 2304), (2304, FFH))
EPS = 1e-6
LAYERS = 2
SCALE = DH ** -0.5
VMEM_LIMIT = 56 * 1024 * 1024

ADAM_LR, ADAM_B1, ADAM_B2, ADAM_EPS, ADAM_WD, ADAM_STEP = 0.001, 0.9, 0.999, 1e-08, 0.01, 10

_SQRT_HALF = 0.7071067811865476
_INV_SQRT_2PI = 0.3989422804014327


def _params(*sem):
    return pltpu.CompilerParams(dimension_semantics=sem or None, vmem_limit_bytes=VMEM_LIMIT)


def _resident(shape, index_map):
    return pl.BlockSpec(shape, index_map, pipeline_mode=pl.Buffered(1))


def _dot(a, b):
    return jnp.dot(a, b, preferred_element_type=F32)


def _dot_nt(a, b):
    return lax.dot_general(a, b, (((1,), (1,)), ((), ())), preferred_element_type=F32)


def _dot_tn(a, b):
    return lax.dot_general(a, b, (((0,), (0,)), ((), ())), preferred_element_type=F32)


def _sigmoid(x):
    return 1.0 / (1.0 + jnp.exp(-x))


def _gelu_and_grad(x):
    cdf = 0.5 * (1.0 + lax.erf(x * _SQRT_HALF))
    return x * cdf, cdf + x * jnp.exp(-0.5 * x * x) * _INV_SQRT_2PI


def _silu_and_grad(x):
    s = _sigmoid(x)
    return x * s, s * (1.0 + x * (1.0 - s))


def _standardize(x):
    mu = jnp.mean(x, axis=-1, keepdims=True)
    d = x - mu
    rstd = lax.rsqrt(jnp.mean(d * d, axis=-1, keepdims=True) + EPS)
    return d * rstd, rstd


def _standardize_bwd(dxhat, xhat, rstd):
    m1 = jnp.mean(dxhat, axis=-1, keepdims=True)
    m2 = jnp.mean(dxhat * xhat, axis=-1, keepdims=True)
    return rstd * (dxhat - m1 - xhat * m2)


def _rms(x):
    return lax.rsqrt(jnp.mean(x * x, axis=-1, keepdims=True) + EPS)


def _rmsnorm_bwd(dy, x, r, g):
    u = dy * g
    return r * u - x * (r * r * r) * jnp.mean(u * x, axis=-1, keepdims=True)


def _col_sum(a):
    return jnp.sum(a, axis=0, keepdims=True)


def _rot(t, cos2, sin2):
    return t * cos2 + pltpu.roll(t, DH // 2, axis=1) * sin2


def _rot_t(dt, cos2, sin2):
    return dt * cos2 + pltpu.roll(dt * sin2, DH // 2, axis=1)


MESH = pl.DeviceIdType.MESH
_HBM = pl.BlockSpec(memory_space=pltpu.HBM)
_ANY = pl.BlockSpec(memory_space=pl.ANY)


def _place():
    x, y, c = lax.axis_index("x"), lax.axis_index("y"), lax.axis_index("c")
    return x, y, c, ((1 - x, y), (x, 1 - y), (1 - x, 1 - y))


def _slot(full, kind, width, i):
    if kind == "cols":
        return full.at[:, pl.ds(pl.multiple_of(i * width, 128), width)]
    if kind == "rows":
        return full.at[pl.ds(pl.multiple_of(i * width, 16), width), :]
    return full.at[i]


class _Gather:
    def __init__(self, units):
        self.units = units
        self.inputs = [u[0] for u in units]
        self.out_shape = []
        for src, kind in units:
            r, c = src.shape[-2:]
            shape = {"cols": (r, NDEV * c), "rows": (NDEV * r, c), "lead": (NDEV,) + src.shape}[kind]
            self.out_shape.append(S(shape, src.dtype))
        n = len(units)
        self.scratch = [pltpu.SemaphoreType.DMA((n, 7)), pltpu.SemaphoreType.DMA((n, 7)), pltpu.SemaphoreType.DMA((n,))]

    def run(self, phase, ins, outs, scr):
        ssem, rsem, lsem = scr
        x, y, c, chips = _place()
        me, sib = 4 * x + 2 * y + c, (x, y, 1 - c)
        idx = lambda chip, core: 4 * chip[0] + 2 * chip[1] + core
        for u, (src_arr, kind) in enumerate(self.units):
            src, full = ins[u], outs[u]
            width = src_arr.shape[-1] if kind == "cols" else src_arr.shape[-2]
            slot = functools.partial(_slot, full, kind, width)

            def copy(k, block, to, from_src=False):
                return pltpu.make_async_remote_copy(src_ref=src if from_src else slot(block), dst_ref=slot(block),
                                                    send_sem=ssem.at[u, k], recv_sem=rsem.at[u, k],
                                                    device_id=to, device_id_type=MESH)

            mine = lambda: pltpu.make_async_copy(src, slot(me), lsem.at[u])
            first = lambda: [copy(0, me, sib, True)] + [copy(1 + j, me, (*chip, c), True) for j, chip in enumerate(chips)]
            passed = lambda j: copy(4 + j, idx(chips[j], c), sib)
            if phase == "start":
                mine().start()
                for cp in first():
                    cp.start()
            elif phase == "forward":
                for j, chip in enumerate(chips):
                    copy(1 + j, idx(chip, c), sib).wait_recv()
                    passed(j).start()
            else:
                copy(0, idx((x, y), 1 - c), sib).wait_recv()
                for j, chip in enumerate(chips):
                    copy(4 + j, idx(chip, 1 - c), sib).wait_recv()
                for cp in first() + [passed(j) for j in range(3)]:
                    cp.wait_send()
                mine().wait()


class _Scatter:
    def __init__(self, units):
        self.units = units
        self.inputs = list(units)
        self.out_shape = [S(u.shape, u.dtype) for u in units]
        n = len(units)
        self.scratch = [pltpu.SemaphoreType.DMA((n, 3)), pltpu.SemaphoreType.DMA((n, 3)), pltpu.SemaphoreType.DMA((n,))]

    def run(self, phase, ins, outs, scr):
        ssem, rsem, lsem = scr
        x, y, c, chips = _place()
        myq = 2 * x + y
        for u in range(len(self.units)):
            h, p = ins[u], outs[u]

            def copy(k, chip, send_to_them):
                q = 2 * chip[0] + chip[1]
                return pltpu.make_async_remote_copy(src_ref=h.at[q], dst_ref=p.at[myq if send_to_them else q],
                                                    send_sem=ssem.at[u, k], recv_sem=rsem.at[u, k],
                                                    device_id=(*chip, c), device_id_type=MESH)

            mine = lambda: pltpu.make_async_copy(h.at[myq], p.at[myq], lsem.at[u])
            sends = lambda: [copy(k, chip, True) for k, chip in enumerate(chips)]
            if phase == "start":
                mine().start()
                for cp in sends():
                    cp.start()
            elif phase == "finish":
                for k, chip in enumerate(chips):
                    copy(k, chip, False).wait_recv()
                for cp in sends():
                    cp.wait_send()
                mine().wait()


class _Comms:
    def __init__(self, parts):
        self.parts = parts
        self.inputs = [a for p in parts for a in p.inputs]
        self.out_shape = [a for p in parts for a in p.out_shape]
        self.scratch = [a for p in parts for a in p.scratch]

    def run(self, phase, ins, outs, scr):
        i = o = s = 0
        for p in self.parts:
            ni, no, ns = len(p.inputs), len(p.out_shape), len(p.scratch)
            p.run(phase, ins[i:i + ni], outs[o:o + no], scr[s:s + ns])
            i, o, s = i + ni, o + no, s + ns


def _launch(body, grid, in_specs, out_specs, out_shape, scratch, args, name, sem, comm=None):
    if comm is None:
        outs = pl.pallas_call(body, grid=grid, name=name, in_specs=in_specs, out_specs=out_specs, out_shape=out_shape,
                              scratch_shapes=scratch, compiler_params=_params(*sem))(*args)
        return list(outs), []
    n_in, n_out, n_scr = len(args), len(out_shape), len(scratch)
    ci, co = len(comm.inputs), len(comm.out_shape)
    nsteps = int(np.prod(grid))
    fwd_step = (7 * nsteps) // 8

    def hosted(*refs):
        a = refs[:n_in]
        ca = refs[n_in:n_in + ci]
        o = refs[n_in + ci:n_in + ci + n_out]
        cout = refs[n_in + ci + n_out:n_in + ci + n_out + co]
        s = refs[n_in + ci + n_out + co:n_in + ci + n_out + co + n_scr]
        cs = refs[n_in + ci + n_out + co + n_scr:]
        step = pl.program_id(0)
        for d in range(1, len(grid)):
            step = step * grid[d] + pl.program_id(d)

        @pl.when(step == 0)
        def _():
            comm.run("start", ca, cout, cs)

        body(*a, *o, *s)

        @pl.when(step == fwd_step)
        def _():
            comm.run("forward", ca, cout, cs)

        @pl.when(step == nsteps - 1)
        def _():
            comm.run("finish", ca, cout, cs)

    outs = pl.pallas_call(
        hosted, grid=grid, name=name, in_specs=list(in_specs) + [_HBM] * ci, out_specs=list(out_specs) + [_HBM] * co,
        out_shape=list(out_shape) + comm.out_shape, scratch_shapes=list(scratch) + comm.scratch,
        compiler_params=_params(*["arbitrary"] * len(grid)))(*args, *comm.inputs)
    return list(outs[:n_out]), list(outs[n_out:])


def _comm_only(comm, name):
    ci, co = len(comm.inputs), len(comm.out_shape)

    def body(*refs):
        ca, cout, cs = refs[:ci], refs[ci:ci + co], refs[ci + co:]
        for phase in ("start", "forward", "finish"):
            comm.run(phase, ca, cout, cs)

    return pl.pallas_call(body, name=name, in_specs=[_HBM] * ci, out_specs=[_HBM] * co, out_shape=comm.out_shape,
                          scratch_shapes=comm.scratch, compiler_params=_params())(*comm.inputs)


def _ret_consts():
    idx = np.arange(C, dtype=np.float32)
    gf = (1.0 - np.exp2(-5.0 - np.arange(HEADS, dtype=np.float32))).astype(np.float32)
    out = {}
    for name, gamma, fwd in (("f", gf, True), ("b", gf[::-1].copy(), False)):
        lg = np.log(gamma).astype(np.float32)[:, None]
        diff = idx[:, None] - idx[None, :]
        if fwd:
            mask = diff >= 0
            dist = np.where(mask, diff, 0.0)
            zeta = np.exp(lg * (C - 1 - idx))
            xi = np.exp(lg * (idx + 1))
        else:
            mask = diff < 0
            dist = np.where(mask, -diff, 0.0)
            zeta = np.exp(lg * idx)
            xi = np.exp(lg * (C - idx))
        dm = np.where(mask[None], np.exp(lg[:, :, None] * dist[None]), 0.0).astype(np.float32)
        bc = lambda vec: np.ascontiguousarray(np.broadcast_to(vec.astype(np.float32)[:, :, None], (HEADS, C, DH)))
        out[name] = dict(D=dm, XI=bc(xi), ZETA=bc(zeta), gC=[float(v) for v in np.exp(lg[:, 0] * C).astype(np.float32)])
    return out


def _rope_tables(t):
    half = DH // 2
    inv_freq = (np.float32(10000.0) ** (-np.arange(half, dtype=np.float32) / np.float32(half))).astype(np.float32)
    ang = (np.arange(t, dtype=np.float32)[:, None] * inv_freq[None, :]).astype(np.float64)
    cos, sin = np.cos(ang).astype(np.float32), np.sin(ang).astype(np.float32)
    return np.concatenate([cos, cos], axis=1), np.concatenate([-sin, sin], axis=1)


def _f_inproj(x, g1, w, cos2, sin2, l, name, comm=None):
    t = x.shape[0]
    tm = 512

    def body(x_ref, g_ref, w_ref, cos_ref, sin_ref, proj_ref, ht_ref):
        xv = x_ref[...]
        h = (xv * _rms(xv) * g_ref[...]).astype(BF16)
        ht_ref[...] = h.T
        for nb in range(INW // 512):
            cs = slice(nb * 512, (nb + 1) * 512)
            res = _dot(h, w_ref[:, cs])
            if nb in (1, 2):
                for hh in range(HEADS):
                    r = _rot(res[:, hh * DH:(hh + 1) * DH], cos_ref[...], sin_ref[...])
                    proj_ref[:, nb * 512 + hh * DH:nb * 512 + (hh + 1) * DH] = (r * SCALE if nb == 2 else r).astype(BF16)
            else:
                proj_ref[:, cs] = res.astype(BF16)

    return _launch(
        body, (t // tm,),
        [pl.BlockSpec((tm, D), lambda i: (i, 0)),
         pl.BlockSpec((None, 1, D), lambda i: (l, 0, 0)),
         _resident((D, INW), lambda i: (0, 0)),
         pl.BlockSpec((tm, DH), lambda i: (i, 0)), pl.BlockSpec((tm, DH), lambda i: (i, 0))],
        [pl.BlockSpec((tm, INW), lambda i: (i, 0)), pl.BlockSpec((D, tm), lambda i: (0, i))],
        [S((t, INW), BF16), S((D, t), BF16)], [], (x, g1, w, cos2, sin2), name, ("parallel",), comm)


def _gm_chunk_fwd(u, v, lng, lnb, ws_ref, bias):
    au, dau = _gelu_and_grad(u)
    av, dav = _gelu_and_grad(v)
    vhat, rstd = _standardize(av)
    vn = (vhat * lng + lnb).astype(BF16)
    head = lax.broadcasted_iota(jnp.int32, (C, GMW), 1) // (GMW // HEADS)
    mixed = bias
    for h in range(HEADS):
        mixed = mixed + jnp.where(head == h, _dot(ws_ref[h], vn), 0.0)
    return au, dau, dav, vhat, rstd, vn, mixed, head


def _f_gm(proj, lng, lnb, ws_bf, bias, l, name):
    t = proj.shape[0]
    tm = 512

    def body(p_ref, lng_ref, lnb_ref, ws_ref, bias_ref, y_ref):
        for ci in range(tm // C):
            rows = slice(ci * C, (ci + 1) * C)
            u = p_ref[rows, 0:GMW].astype(F32)
            v = p_ref[rows, GMW:2 * GMW].astype(F32)
            au, _, _, _, _, _, mixed, _ = _gm_chunk_fwd(u, v, lng_ref[...], lnb_ref[...], ws_ref, bias_ref[...])
            y_ref[rows, :] = (au * mixed).astype(BF16)

    return pl.pallas_call(
        body, grid=(t // tm,), name=name,
        in_specs=[pl.BlockSpec((tm, 2 * GMW), lambda i: (i, 0)),
                  pl.BlockSpec((None, 1, GMW), lambda i: (l, 0, 0)),
                  pl.BlockSpec((None, 1, GMW), lambda i: (l, 0, 0)),
                  pl.BlockSpec((None, HEADS, C, C), lambda i: (l, 0, 0, 0)),
                  pl.BlockSpec((None, C, GMW), lambda i: (l, 0, 0))],
        out_specs=pl.BlockSpec((tm, GMW), lambda i: (i, 0)),
        out_shape=S((t, GMW), BF16),
        compiler_params=_params("parallel"),
    )(proj, lng, lnb, ws_bf, bias)


def _scan_pair(proj, other, col, wf, wb, gcf, gcb, first_is_f, name):
    t = proj.shape[0]
    n = t // C
    sc = min(SCAN_CHUNKS, n)
    nsteps = n // sc
    other_is_proj = other is None

    def body(a1, o1, a2, o2, w1_ref, w2_ref, out1, out2, st1, st2):
        @pl.when(pl.program_id(0) == 0)
        def _():
            st1[...] = jnp.zeros_like(st1)
            st2[...] = jnp.zeros_like(st2)

        def one(a_ref, o_ref, w_ref, gc, st, out, order):
            for h in range(HEADS):
                sl = slice(h * DH, (h + 1) * DH)
                incs = {}
                for j in order:
                    rows = slice(j * C, (j + 1) * C)
                    aw = (a_ref[rows, sl].astype(F32) * w_ref[h]).astype(BF16)
                    incs[j] = _dot_tn(aw, o_ref[rows, sl].astype(BF16))
                cur = st[h]
                for j in order:
                    out[j, h] = cur.astype(BF16)
                    cur = gc[h] * cur + incs[j]
                st[h] = cur

        g1, g2 = (gcf, gcb) if first_is_f else (gcb, gcf)
        one(a1, o1, w1_ref, g1, st1, out1, range(sc))
        one(a2, o2, w2_ref, g2, st2, out2, range(sc - 1, -1, -1))

    up = lambda i: i
    down = lambda i: nsteps - 1 - i

    def specs(ix):
        o_spec = pl.BlockSpec((sc * C, RETW), lambda i: (ix(i), 3 if other_is_proj else 0))
        return [pl.BlockSpec((sc * C, RETW), lambda i: (ix(i), col)), o_spec]

    const = lambda: pl.BlockSpec((HEADS, C, DH), lambda i: (0, 0, 0))
    oth = proj if other_is_proj else other
    w1, w2 = (wf, wb) if first_is_f else (wb, wf)
    out1, out2 = pl.pallas_call(
        body, grid=(nsteps,), name=name,
        in_specs=specs(up) + specs(down) + [const(), const()],
        out_specs=[pl.BlockSpec((sc, HEADS, DH, DH), lambda i: (up(i), 0, 0, 0)),
                   pl.BlockSpec((sc, HEADS, DH, DH), lambda i: (down(i), 0, 0, 0))],
        out_shape=[S((n, HEADS, DH, DH), BF16), S((n, HEADS, DH, DH), BF16)],
        scratch_shapes=[pltpu.VMEM((HEADS, DH, DH), F32), pltpu.VMEM((HEADS, DH, DH), F32)],
        compiler_params=_params("arbitrary"),
    )(proj, oth, proj, oth, w1, w2)
    return (out1, out2) if first_is_f else (out2, out1)


SCAN_CHUNKS = 8


RET_CHUNKS = 4


def _f_ret_out(proj, s_f, s_b, rc, name):
    t = proj.shape[0]
    tm = RET_CHUNKS * C

    def body(q_ref, k_ref, v_ref, sf_ref, sb_ref, d_ref, xif_ref, xib_ref, o_ref):
        for ci in range(RET_CHUNKS):
            rows = slice(ci * C, (ci + 1) * C)
            for h in range(HEADS):
                sl = slice(h * DH, (h + 1) * DH)
                qh, kh, vh = q_ref[rows, sl], k_ref[rows, sl], v_ref[rows, sl]
                p = (_dot_nt(qh, kh) * d_ref[h]).astype(BF16)
                cross = _dot(qh, jnp.concatenate([sf_ref[ci, h], sb_ref[ci, h]], axis=1))
                o_ref[rows, sl] = _dot(p, vh) + xif_ref[h] * cross[:, 0:DH] + xib_ref[h] * cross[:, DH:2 * DH]

    const = lambda: pl.BlockSpec((HEADS, C, DH), lambda i: (0, 0, 0))
    state = lambda: pl.BlockSpec((RET_CHUNKS, HEADS, DH, DH), lambda i: (i, 0, 0, 0))
    return pl.pallas_call(
        body, grid=(t // tm,), name=name,
        in_specs=[pl.BlockSpec((tm, RETW), lambda i, cb=cb: (i, cb)) for cb in (1, 2, 3)] +
                 [state(), state(), const(), const(), const()],
        out_specs=pl.BlockSpec((tm, RETW), lambda i: (i, 0)),
        out_shape=S((t, RETW), F32),
        compiler_params=_params("parallel"),
    )(proj, proj, proj, s_f, s_b, rc["f"]["D"] + rc["b"]["D"], rc["f"]["XI"], rc["b"]["XI"])


def _conv_halo_specs(t, tm, width, col):
    r = tm // HALO
    last = t // HALO - 1
    return [pl.BlockSpec((HALO, width), lambda i: (jnp.maximum(i * r - 1, 0), col)),
            pl.BlockSpec((tm, width), lambda i: (i, col)),
            pl.BlockSpec((HALO, width), lambda i: (jnp.minimum((i + 1) * r, last), col))]


def _fill_ext(ext, prev, cur, nxt, i, nt, tm):
    ext[0:HALO, :] = jnp.where(i > 0, prev, 0.0)
    ext[HALO:HALO + tm, :] = cur
    ext[HALO + tm:2 * HALO + tm, :] = jnp.where(i < nt - 1, nxt, 0.0)


def _glu(a_ref, g_ref):
    return a_ref[...].astype(F32) * _sigmoid(g_ref[...].astype(F32))


def _shifted_copies(ext, rot, tm):
    rows = tm + 2 * HALO - 8
    for b in range(1, 8):
        rot[b - 1, :, :] = ext[pl.ds(b, rows), :]


def _tap(ext, rot, r0, s, rb):
    a, b = divmod(s, 8)
    return ext[pl.ds(r0 + 8 * a, rb), :] if b == 0 else rot[b - 1, pl.ds(r0 + 8 * a, rb), :]


def _f_conv(proj, cw, cb, lng, lnb, l, name, comm=None):
    t = proj.shape[0]
    tm = 256
    nt = t // tm
    rb = 64

    def body(ap, ac, an, gp, gc, gn, cw_ref, cb_ref, lng_ref, lnb_ref, c_ref, y_ref, hext, hrot):
        i = pl.program_id(0)
        _fill_ext(hext, _glu(ap, gp), _glu(ac, gc), _glu(an, gn), i, nt, tm)
        _shifted_copies(hext, hrot, tm)
        for r0 in range(0, tm, rb):
            acc = jnp.zeros((rb, CVW), F32) + cb_ref[...]
            for j in range(KW):
                acc = acc + cw_ref[j:j + 1, :] * _tap(hext, hrot, r0, j + 1, rb)
            c_ref[r0:r0 + rb, :] = acc
            chat, _ = _standardize(acc)
            z = chat * lng_ref[...] + lnb_ref[...]
            y_ref[r0:r0 + rb, :] = (z * _sigmoid(z)).astype(BF16)

    vec = lambda: pl.BlockSpec((None, 1, CVW), lambda i: (l, 0, 0))
    return _launch(
        body, (nt,),
        _conv_halo_specs(t, tm, CVW, 10) + _conv_halo_specs(t, tm, CVW, 11) +
        [pl.BlockSpec((None, 32, CVW), lambda i: (l, 0, 0)), vec(), vec(), vec()],
        [pl.BlockSpec((tm, CVW), lambda i: (i, 0)), pl.BlockSpec((tm, CVW), lambda i: (i, 0))],
        [S((t, CVW), F32), S((t, CVW), BF16)],
        [pltpu.VMEM((tm + 2 * HALO, CVW), F32), pltpu.VMEM((7, tm + 2 * HALO - 8, CVW), F32)],
        (proj, proj, proj, proj, proj, proj, cw, cb, lng, lnb), name, ("parallel",), comm)


def _f_mixout(x, y_gm, y_cv, o, proj, w, name):
    t = x.shape[0]
    tm = 512

    def body(x_ref, ygm_ref, ycv_ref, o_ref, g_ref, w_ref, xm_ref, ycat_t_ref, ycat):
        ycat[:, 0:GMW] = ygm_ref[...]
        ycat[:, GMW + RETW:D] = ycv_ref[...]
        for h in range(HEADS):
            sl = slice(h * DH, (h + 1) * DH)
            ohat, _ = _standardize(o_ref[:, sl])
            g = g_ref[:, sl].astype(F32)
            ycat[:, GMW + h * DH:GMW + (h + 1) * DH] = (ohat * (g * _sigmoid(g))).astype(BF16)
        yc = ycat[...]
        ycat_t_ref[...] = yc.T
        xm_ref[...] = x_ref[...] + _dot(yc, w_ref[...])

    return pl.pallas_call(
        body, grid=(t // tm,), name=name,
        in_specs=[pl.BlockSpec((tm, D), lambda i: (i, 0)),
                  pl.BlockSpec((tm, GMW), lambda i: (i, 0)),
                  pl.BlockSpec((tm, CVW), lambda i: (i, 0)),
                  pl.BlockSpec((tm, RETW), lambda i: (i, 0)),
                  pl.BlockSpec((tm, RETW), lambda i: (i, 4)),
                  _resident((D, D), lambda i: (0, 0))],
        out_specs=[pl.BlockSpec((tm, D), lambda i: (i, 0)), pl.BlockSpec((D, tm), lambda i: (0, i))],
        out_shape=[S((t, D), F32), S((D, t), BF16)],
        scratch_shapes=[pltpu.VMEM((tm, D), BF16)],
        compiler_params=_params("parallel"),
    )(x, y_gm, y_cv, o, proj, w)


def _f_ffn(xm, g2, w1, w2, l, name, comm=None):
    t = xm.shape[0]
    tm = 512

    def body(x_ref, g_ref, w1_ref, w2_ref, xo_ref, h_ref, gu_ref, act_t_ref, act):
        xv = x_ref[...]
        h = (xv * _rms(xv) * g_ref[...]).astype(BF16)
        h_ref[...] = h
        for a, b in FF_CHUNKS:
            gate = _dot_nt(h, w1_ref[a:b, :])
            up = _dot_nt(h, w1_ref[FFH + a:FFH + b, :])
            gu_ref[:, a:b] = gate.astype(BF16)
            gu_ref[:, FFH + a:FFH + b] = up.astype(BF16)
            av = ((gate * _sigmoid(gate)) * up).astype(BF16)
            act[:, a:b] = av
            act_t_ref[a:b, :] = av.T
        xo_ref[...] = xv + _dot(act[...], w2_ref[...])

    return _launch(
        body, (t // tm,),
        [pl.BlockSpec((tm, D), lambda i: (i, 0)),
         pl.BlockSpec((None, 1, D), lambda i: (l, 0, 0)),
         _resident((2 * FFH, D), lambda i: (0, 0)),
         _resident((FFH, D), lambda i: (0, 0))],
        [pl.BlockSpec((tm, D), lambda i: (i, 0)), pl.BlockSpec((tm, D), lambda i: (i, 0)),
         pl.BlockSpec((tm, 2 * FFH), lambda i: (i, 0)), pl.BlockSpec((FFH, tm), lambda i: (0, i))],
        [S((t, D), F32), S((t, D), BF16), S((t, 2 * FFH), BF16), S((FFH, t), BF16)],
        [pltpu.VMEM((tm, FFH), BF16)], (xm, g2, w1, w2), name, ("parallel",), comm)


def _b_ffn(top, xm, g2, gu, w1, w2, l, name, comm=None):
    t = xm.shape[0]
    tm = 256
    from_loss = isinstance(top, tuple)
    n_top = 3 if from_loss else 1

    def body(*refs):
        top_refs = refs[:n_top]
        x_ref, g_ref, gu_ref, w1_ref, w2_ref, dgu_ref, dxm_ref, dxb_ref, dg_ref = refs[n_top:n_top + 9]
        dgu = refs[-1]
        first = pl.program_id(0) == 0

        @pl.when(first)
        def _():
            dg_ref[...] = jnp.zeros_like(dg_ref)

        if from_loss:
            xo_ref, fg_ref, t_ref = top_refs
            loss_ref, dfg_ref = refs[n_top + 9:n_top + 11]

            @pl.when(first)
            def _():
                loss_ref[...] = jnp.zeros_like(loss_ref)
                dfg_ref[...] = jnp.zeros_like(dfg_ref)

            xo = xo_ref[...]
            ro = _rms(xo)
            xr = xo * ro
            err = xr * fg_ref[...] - t_ref[...]
            loss_ref[...] += (0.5 / D) * _col_sum(jnp.sum(err * err, axis=1, keepdims=True))
            dy = err * (1.0 / D)
            dfg_ref[...] += _col_sum(dy * xr)
            dxo = _rmsnorm_bwd(dy, xo, ro, fg_ref[...])
        else:
            dxo = top_refs[0][...]
        dxb = dxo.astype(BF16)
        dxb_ref[...] = dxb
        for a, b in FF_CHUNKS:
            dact = _dot_nt(dxb, w2_ref[a:b, :])
            gate = gu_ref[:, a:b].astype(F32)
            up = gu_ref[:, FFH + a:FFH + b].astype(F32)
            sg, dsg = _silu_and_grad(gate)
            dgate = (dact * up * dsg).astype(BF16)
            dup = (dact * sg).astype(BF16)
            dgu[:, a:b] = dgate
            dgu[:, FFH + a:FFH + b] = dup
            dgu_ref[a:b, :] = dgate.T
            dgu_ref[FFH + a:FFH + b, :] = dup.T
        dh = _dot(dgu[...], w1_ref[...])
        xv = x_ref[...]
        r = _rms(xv)
        dg_ref[...] += _col_sum(dh * xv * r)
        dxm_ref[...] = dxo + _rmsnorm_bwd(dh, xv, r, g_ref[...])

    tok = lambda: pl.BlockSpec((tm, D), lambda i: (i, 0))
    vec = lambda: pl.BlockSpec((1, D), lambda i: (0, 0))
    top_specs = [tok(), vec(), tok()] if from_loss else [tok()]
    extra_specs = [pl.BlockSpec((1, 1), lambda i: (0, 0)), vec()] if from_loss else []
    extra_shape = [S((1, 1), F32), S((1, D), F32)] if from_loss else []
    return _launch(
        body, (t // tm,),
        top_specs + [tok(), pl.BlockSpec((None, 1, D), lambda i: (l, 0, 0)),
                     pl.BlockSpec((tm, 2 * FFH), lambda i: (i, 0)),
                     _resident((2 * FFH, D), lambda i: (0, 0)),
                     _resident((FFH, D), lambda i: (0, 0))],
        [pl.BlockSpec((2 * FFH, tm), lambda i: (0, i)), tok(), tok(), vec()] + extra_specs,
        [S((2 * FFH, t), BF16), S((t, D), F32), S((t, D), BF16), S((1, D), F32)] + extra_shape,
        [pltpu.VMEM((tm, 2 * FFH), BF16)],
        ((*top,) if from_loss else (top,)) + (xm, g2, gu, w1, w2), name, ("arbitrary",), comm)


def _mm_wgrad(at, b, pieces, at_rows, b_mode, group, name):
    bt = 2048
    t = at.shape[-1]
    bt = min(bt, t)
    nt = t // bt
    if at_rows:
        ka = at.shape[0] // pieces
        a_spec = pl.BlockSpec((ka, bt), lambda j, tt: (j, tt))
    else:
        ka = at.shape[0]
        a_spec = pl.BlockSpec((ka, bt), lambda j, tt: (0, tt))
    if b_mode == "shared":
        nb, b_spec = b.shape[1], pl.BlockSpec((bt, b.shape[1]), lambda j, tt: (tt, 0))
    elif b_mode == "cols":
        nb = b.shape[1] // pieces
        b_spec = pl.BlockSpec((bt, group * nb), lambda j, tt: (tt, j))
    else:
        nb, b_spec = b.shape[2], pl.BlockSpec((None, bt, b.shape[2]), lambda j, tt: (j, tt, 0))
    assert group == 1 or b_mode == "cols"

    def body(a_ref, b_ref, o_ref, acc):
        tt = pl.program_id(1)

        @pl.when(tt == 0)
        def _():
            acc[...] = jnp.zeros_like(acc)

        acc[...] += _dot(a_ref[...], b_ref[...])

        @pl.when(tt == nt - 1)
        def _():
            for k in range(group):
                o_ref[k] = acc[:, k * nb:(k + 1) * nb].astype(BF16)

    return pl.pallas_call(
        body, grid=(pieces // group, nt), name=name,
        in_specs=[a_spec, b_spec],
        out_specs=pl.BlockSpec((group, ka, nb), lambda j, tt: (j, 0, 0)),
        out_shape=S((pieces, ka, nb), BF16),
        scratch_shapes=[pltpu.VMEM((ka, group * nb), F32)],
        compiler_params=_params("parallel", "arbitrary"),
    )(at, b)


def _b_mixout(dxm, w, o, proj, c, lng, lnb, l, name, comm=None):
    t = dxm.shape[0]
    tm = 256

    def body(dxm_ref, w_ref, o_ref, g_ref, c_ref, lng_ref, lnb_ref,
             dxb_ref, dygm_ref, dO_ref, dg_ref, dc_ref, dlg_ref, dlb_ref, dcb_ref):
        @pl.when(pl.program_id(0) == 0)
        def _():
            dlg_ref[...] = jnp.zeros_like(dlg_ref)
            dlb_ref[...] = jnp.zeros_like(dlb_ref)
            dcb_ref[...] = jnp.zeros_like(dcb_ref)

        dxb = dxm_ref[...].astype(BF16)
        dxb_ref[...] = dxb
        dy = _dot_nt(dxb, w_ref[...])
        dygm_ref[...] = dy[:, 0:GMW]
        for h in range(HEADS):
            sl = slice(h * DH, (h + 1) * DH)
            ohat, rstd = _standardize(o_ref[:, sl])
            sg, dsg = _silu_and_grad(g_ref[:, sl].astype(F32))
            dyr = dy[:, GMW + h * DH:GMW + (h + 1) * DH]
            dg_ref[:, sl] = (dyr * ohat * dsg).astype(BF16)
            dO_ref[:, sl] = _standardize_bwd(dyr * sg, ohat, rstd)
        chat, rstd = _standardize(c_ref[...])
        z = chat * lng_ref[...] + lnb_ref[...]
        _, dsz = _silu_and_grad(z)
        dz = dy[:, GMW + RETW:D] * dsz
        dlg_ref[...] += _col_sum(dz * chat)
        dlb_ref[...] += _col_sum(dz)
        dc = _standardize_bwd(dz * lng_ref[...], chat, rstd)
        dcb_ref[...] += _col_sum(dc)
        dc_ref[...] = dc

    vec = lambda: pl.BlockSpec((None, 1, CVW), lambda i: (l, 0, 0))
    acc = lambda: pl.BlockSpec((1, CVW), lambda i: (0, 0))
    return _launch(
        body, (t // tm,),
        [pl.BlockSpec((tm, D), lambda i: (i, 0)),
         _resident((D, D), lambda i: (0, 0)),
         pl.BlockSpec((tm, RETW), lambda i: (i, 0)),
         pl.BlockSpec((tm, RETW), lambda i: (i, 4)),
         pl.BlockSpec((tm, CVW), lambda i: (i, 0)), vec(), vec()],
        [pl.BlockSpec((tm, D), lambda i: (i, 0)), pl.BlockSpec((tm, GMW), lambda i: (i, 0)),
         pl.BlockSpec((tm, RETW), lambda i: (i, 0)), pl.BlockSpec((tm, RETW), lambda i: (i, 0)),
         pl.BlockSpec((tm, CVW), lambda i: (i, 0)), acc(), acc(), acc()],
        [S((t, D), BF16), S((t, GMW), F32), S((t, RETW), F32), S((t, RETW), BF16), S((t, CVW), F32),
         S((1, CVW), F32), S((1, CVW), F32), S((1, CVW), F32)],
        [], (dxm, w, o, proj, c, lng, lnb), name, ("arbitrary",), comm)


def _b_gm(proj, dy, lng, lnb, ws_bf, wst_bf, bias, l, name, comm=None):
    t = proj.shape[0]
    tm = 512
    nt = t // tm

    def body(p_ref, dy_ref, lng_ref, lnb_ref, ws_ref, wst_ref, bias_ref,
             duv_ref, dws_ref, dbias_ref, dbs_ref, dlg_ref, dlb_ref):
        @pl.when(pl.program_id(0) == 0)
        def _():
            dws_ref[...] = jnp.zeros_like(dws_ref)
            dbias_ref[...] = jnp.zeros_like(dbias_ref)
            dbs_ref[...] = jnp.zeros_like(dbs_ref)
            dlg_ref[...] = jnp.zeros_like(dlg_ref)
            dlb_ref[...] = jnp.zeros_like(dlb_ref)

        for ci in range(tm // C):
            rows = slice(ci * C, (ci + 1) * C)
            u = p_ref[rows, 0:GMW].astype(F32)
            v = p_ref[rows, GMW:2 * GMW].astype(F32)
            au, dau, dav, vhat, rstd, vn, mixed, head = _gm_chunk_fwd(u, v, lng_ref[...], lnb_ref[...], ws_ref, bias_ref[...])
            dyc = dy_ref[rows, :]
            dmixed = dyc * au
            dmb = dmixed.astype(BF16)
            dbias_ref[...] += dmixed
            dvn = jnp.zeros((C, GMW), F32)
            for h in range(HEADS):
                dws_ref[h] += _dot_nt(jnp.where(head == h, dmixed, 0.0).astype(BF16), vn)
                dvn = dvn + jnp.where(head == h, _dot(wst_ref[h], dmb), 0.0)
            dlg_ref[...] += _col_sum(dvn * vhat)
            dlb_ref[...] += _col_sum(dvn)
            dav_in = _standardize_bwd(dvn * lng_ref[...], vhat, rstd)
            duv_ref[rows, 0:GMW] = (dyc * mixed * dau).astype(BF16)
            duv_ref[rows, GMW:2 * GMW] = (dav_in * dav).astype(BF16)

        @pl.when(pl.program_id(0) == nt - 1)
        def _():
            head = lax.broadcasted_iota(jnp.int32, (C, GMW), 1) // (GMW // HEADS)
            lane = lax.broadcasted_iota(jnp.int32, (C, 128), 1)
            fold = jnp.zeros((C, 128), F32)
            for h in range(HEADS):
                col = jnp.sum(jnp.where(head == h, dbias_ref[...], 0.0), axis=1, keepdims=True)
                fold = jnp.where(lane == h, col, fold)
            dbs_ref[...] = fold

    vec = lambda: pl.BlockSpec((None, 1, GMW), lambda i: (l, 0, 0))
    mats = lambda: pl.BlockSpec((None, HEADS, C, C), lambda i: (l, 0, 0, 0))
    return _launch(
        body, (nt,),
        [pl.BlockSpec((tm, 2 * GMW), lambda i: (i, 0)), pl.BlockSpec((tm, GMW), lambda i: (i, 0)),
         vec(), vec(), mats(), mats(), pl.BlockSpec((None, C, GMW), lambda i: (l, 0, 0))],
        [pl.BlockSpec((tm, 2 * GMW), lambda i: (i, 0)),
         pl.BlockSpec((HEADS, C, C), lambda i: (0, 0, 0)),
         pl.BlockSpec((C, GMW), lambda i: (0, 0)), pl.BlockSpec((C, 128), lambda i: (0, 0)),
         pl.BlockSpec((1, GMW), lambda i: (0, 0)), pl.BlockSpec((1, GMW), lambda i: (0, 0))],
        [S((t, 2 * GMW), BF16), S((HEADS, C, C), F32), S((C, GMW), F32), S((C, 128), F32),
         S((1, GMW), F32), S((1, GMW), F32)],
        [], (proj, dy, lng, lnb, ws_bf, wst_bf, bias), name, ("arbitrary",), comm)


def _b_conv(proj, dc, cw, l, name, comm=None):
    t = proj.shape[0]
    tm = 256
    nt = t // tm
    rb = 64

    def body(ap, ac, an, gp, gc, gn, dp, dcur, dn, cw_ref, dag_ref, dcw_ref, hext, dext, hrot, drot):
        i = pl.program_id(0)

        @pl.when(i == 0)
        def _():
            dcw_ref[...] = jnp.zeros_like(dcw_ref)

        _fill_ext(hext, _glu(ap, gp), _glu(ac, gc), _glu(an, gn), i, nt, tm)
        _fill_ext(dext, dp[...], dcur[...], dn[...], i, nt, tm)
        _shifted_copies(hext, hrot, tm)
        _shifted_copies(dext, drot, tm)
        for j in range(KW):
            dcw_ref[j:j + 1, :] += _col_sum(dcur[...] * _tap(hext, hrot, 0, j + 1, tm))
        for r0 in range(0, tm, rb):
            dh = jnp.zeros((rb, CVW), F32)
            for j in range(KW):
                dh = dh + cw_ref[j:j + 1, :] * _tap(dext, drot, r0, 2 * HALO - 1 - j, rb)
            a = ac[r0:r0 + rb, :].astype(F32)
            s = _sigmoid(gc[r0:r0 + rb, :].astype(F32))
            dag_ref[r0:r0 + rb, 0:CVW] = (dh * s).astype(BF16)
            dag_ref[r0:r0 + rb, CVW:2 * CVW] = (dh * a * s * (1.0 - s)).astype(BF16)

    dspecs = _conv_halo_specs(t, tm, CVW, 0)
    return _launch(
        body, (nt,),
        _conv_halo_specs(t, tm, CVW, 10) + _conv_halo_specs(t, tm, CVW, 11) + dspecs +
        [pl.BlockSpec((None, 32, CVW), lambda i: (l, 0, 0))],
        [pl.BlockSpec((tm, 2 * CVW), lambda i: (i, 0)), pl.BlockSpec((32, CVW), lambda i: (0, 0))],
        [S((t, 2 * CVW), BF16), S((32, CVW), F32)],
        [pltpu.VMEM((tm + 2 * HALO, CVW), F32), pltpu.VMEM((tm + 2 * HALO, CVW), F32),
         pltpu.VMEM((7, tm + 2 * HALO - 8, CVW), F32), pltpu.VMEM((7, tm + 2 * HALO - 8, CVW), F32)],
        (proj, proj, proj, proj, proj, proj, dc, dc, dc, cw), name, ("arbitrary",), comm)


def _b_ret_out(proj, cos2, sin2, dO, s_f, s_b, g_f, g_b, rc, name, comm=None):
    t = proj.shape[0]
    tm = RET_CHUNKS * C

    def body(q_ref, k_ref, v_ref, cos_ref, sin_ref, dO_ref, sf_ref, sb_ref, gf_ref, gb_ref,
             d_ref, xif_ref, xib_ref, zef_ref, zeb_ref, dq_ref, dk_ref, dv_ref):
        for ci in range(RET_CHUNKS):
            rows = slice(ci * C, (ci + 1) * C)
            cos_v, sin_v = cos_ref[rows, :], sin_ref[rows, :]
            for h in range(HEADS):
                sl = slice(h * DH, (h + 1) * DH)
                qh, kh, vh = q_ref[rows, sl], k_ref[rows, sl], v_ref[rows, sl]
                dOh = dO_ref[rows, sl].astype(BF16)
                dm = d_ref[h]
                p = (_dot_nt(qh, kh) * dm).astype(BF16)
                dp = (_dot_nt(dOh, vh) * dm).astype(BF16)
                from_s = _dot_nt(dOh, jnp.concatenate([sf_ref[ci, h], sb_ref[ci, h]], axis=0))
                from_g = _dot_nt(vh, jnp.concatenate([gf_ref[ci, h], gb_ref[ci, h]], axis=0))
                kg = _dot(kh, jnp.concatenate([gf_ref[ci, h], gb_ref[ci, h]], axis=1))
                dqr = _dot(dp, kh) + xif_ref[h] * from_s[:, 0:DH] + xib_ref[h] * from_s[:, DH:2 * DH]
                dkr = (_dot_tn(dp, qh) + zef_ref[h] * from_g[:, 0:DH] + zeb_ref[h] * from_g[:, DH:2 * DH]) * SCALE
                dv = _dot_tn(p, dOh) + zef_ref[h] * kg[:, 0:DH] + zeb_ref[h] * kg[:, DH:2 * DH]
                dq_ref[rows, sl] = _rot_t(dqr, cos_v, sin_v).astype(BF16)
                dk_ref[rows, sl] = _rot_t(dkr, cos_v, sin_v).astype(BF16)
                dv_ref[rows, sl] = dv.astype(BF16)

    const = lambda: pl.BlockSpec((HEADS, C, DH), lambda i: (0, 0, 0))
    state = lambda: pl.BlockSpec((RET_CHUNKS, HEADS, DH, DH), lambda i: (i, 0, 0, 0))
    tok = lambda: pl.BlockSpec((tm, RETW), lambda i: (i, 0))
    return _launch(
        body, (t // tm,),
        [pl.BlockSpec((tm, RETW), lambda i, cb=cb: (i, cb)) for cb in (1, 2, 3)] +
        [pl.BlockSpec((tm, DH), lambda i: (i, 0)), pl.BlockSpec((tm, DH), lambda i: (i, 0)), tok(),
         state(), state(), state(), state()] + [const() for _ in range(5)],
        [tok(), tok(), tok()],
        [S((t, RETW), BF16) for _ in range(3)], [],
        (proj, proj, proj, cos2, sin2, dO, s_f, s_b, g_f, g_b, rc["f"]["D"] + rc["b"]["D"],
         rc["f"]["XI"], rc["b"]["XI"], rc["f"]["ZETA"], rc["b"]["ZETA"]), name, ("parallel",), comm)


def _b_inproj(d_uv, dqkv, d_g, d_ag, w, x, g1, dxm, l, name, comm=None):
    t = x.shape[0]
    tm = 512

    def body(duv_ref, dq_ref, dk_ref, dv_ref, dg_ref, dag_ref, w_ref, x_ref, g_ref, dxm_ref,
             dp_ref, dx_ref, dn_ref):
        @pl.when(pl.program_id(0) == 0)
        def _():
            dn_ref[...] = jnp.zeros_like(dn_ref)

        for k, part in enumerate((duv_ref, dq_ref, dk_ref, dv_ref, dg_ref, dag_ref)):
            dp_ref[:, 512 * k:512 * (k + 1)] = part[...]
        dh = _dot_nt(dp_ref[...], w_ref[...])
        xv = x_ref[...]
        r = _rms(xv)
        dn_ref[...] += _col_sum(dh * xv * r)
        dx_ref[...] = dxm_ref[...] + _rmsnorm_bwd(dh, xv, r, g_ref[...])

    half = lambda: pl.BlockSpec((tm, 512), lambda i: (i, 0))
    full = lambda: pl.BlockSpec((tm, D), lambda i: (i, 0))
    return _launch(
        body, (t // tm,),
        [half() for _ in range(6)] +
        [_resident((D, INW), lambda i: (0, 0)), full(), pl.BlockSpec((None, 1, D), lambda i: (l, 0, 0)), full()],
        [pl.BlockSpec((tm, INW), lambda i: (i, 0)), full(), pl.BlockSpec((1, D), lambda i: (0, 0))],
        [S((t, INW), BF16), S((t, D), F32), S((1, D), F32)], [],
        (d_uv, *dqkv, d_g, d_ag, w, x, g1, dxm), name, ("arbitrary",), comm)


class _PairSwap:
    def __init__(self, units):
        self.inputs = list(units)
        self.out_shape = [S((4,) + u.shape[1:], BF16) for u in units]
        n = len(units)
        self.scratch = [pltpu.SemaphoreType.DMA((n, 4)), pltpu.SemaphoreType.DMA((n, 4))]

    def run(self, phase, ins, outs, scr):
        ssem, rsem = scr
        x, y, c, _ = _place()
        copies = lambda: [pltpu.make_async_remote_copy(src_ref=ins[u].at[2 * chip + (1 - c)], dst_ref=outs[u].at[chip],
                                                       send_sem=ssem.at[u, chip], recv_sem=rsem.at[u, chip],
                                                       device_id=(x, y, 1 - c), device_id_type=MESH)
                          for u in range(len(self.inputs)) for chip in range(4)]
        if phase == "start":
            for cp in copies():
                cp.start()
        elif phase == "finish":
            for cp in copies():
                cp.wait_recv()
            for cp in copies():
                cp.wait_send()


def _pair_add(g, q, name):
    _, mm, nn = g.shape
    bm = _row_block(mm)

    def body(c_ref, g_ref, q_ref, h_ref):
        h_ref[...] = (g_ref[...].astype(F32) + q_ref[...].astype(F32)).astype(BF16)

    blk = lambda: pl.BlockSpec((None, bm, nn), lambda qq, i, c_ref: (qq, i, 0))
    return pl.pallas_call(
        body, name=name,
        grid_spec=pltpu.PrefetchScalarGridSpec(
            num_scalar_prefetch=1, grid=(4, mm // bm),
            in_specs=[pl.BlockSpec((None, None, bm, nn), lambda qq, i, c_ref: (qq, c_ref[0], i, 0)), blk()],
            out_specs=blk()),
        out_shape=S((4, mm, nn), BF16),
        compiler_params=_params("parallel", "parallel"),
    )(lax.axis_index("c").astype(jnp.int32).reshape(1), g.reshape(4, 2, mm, nn), q)


_BIG = ("w_in", "w_out", "w_ffn_in", "w_ffn_out")
_KIND = dict(w_in="cols", w_out="rows", w_ffn_in="lead", w_ffn_out="rows")
CWP = 128
EARLY_ROWS, LATE_ROWS = 152, 8


def _step(x, tgt, wts, sh, cw_pad):
    t = x.shape[0]
    rc = _ret_consts()
    cos2, sin2 = (jnp.asarray(a) for a in _rope_tables(t))
    n1 = wts["norm1_g"].reshape(LAYERS, 1, D)
    n2 = wts["norm2_g"].reshape(LAYERS, 1, D)
    gm_lng = wts["gm_ln_g"].reshape(LAYERS, 1, GMW)
    gm_lnb = wts["gm_ln_b"].reshape(LAYERS, 1, GMW)
    ws_bf = wts["gm_ws"].astype(BF16)
    wst_bf = jnp.swapaxes(wts["gm_ws"], 2, 3).astype(BF16)
    bias = jnp.repeat(jnp.swapaxes(wts["gm_bs"], 1, 2), GMW // HEADS, axis=2)
    cb = wts["conv_b"].reshape(LAYERS, 1, CVW)
    cv_lng = wts["conv_ln_g"].reshape(LAYERS, 1, CVW)
    cv_lnb = wts["conv_ln_b"].reshape(LAYERS, 1, CVW)
    unit = lambda f, l: (sh[f][l], _KIND[f])
    cshard = CVW // NDEV

    full = {f: [None] * LAYERS for f in _BIG}
    full["w_in"][0], cw_all = _comm_only(_Gather([unit("w_in", 0), (cw_pad, "lead")]), "gather_first")
    cw = jnp.transpose(cw_all[:, :, :, :cshard], (1, 2, 0, 3)).reshape(LAYERS, 32, CVW)

    gcf, gcb = rc["f"]["gC"], rc["b"]["gC"]
    saved = []
    for l in range(LAYERS):
        first = l == 0
        (proj, h1), got = _f_inproj(x, n1, full["w_in"][l], cos2, sin2, l, f"f_inproj_{l}",
                                    _Gather([unit("w_ffn_in", 0)]) if first else None)
        if first:
            full["w_ffn_in"][0], = got
        y_gm = _f_gm(proj, gm_lng, gm_lnb, ws_bf, bias, l, f"f_gm_{l}")
        s_f, s_b = _scan_pair(proj, None, 2, rc["f"]["ZETA"], rc["b"]["ZETA"], gcf, gcb, True, f"f_ret_state_{l}")
        o = _f_ret_out(proj, s_f, s_b, rc, f"f_ret_out_{l}")
        (c, y_cv), got = _f_conv(proj, cw, cb, cv_lng, cv_lnb, l, f"f_conv_{l}",
                                 _Gather([unit("w_out", 0), unit("w_ffn_out", 0)]) if first else None)
        if first:
            full["w_out"][0], full["w_ffn_out"][0] = got
        xm, ycat = _f_mixout(x, y_gm, y_cv, o, proj, full["w_out"][l], f"f_mixout_{l}")
        w1t = full["w_ffn_in"][l].reshape(2 * FFH, D)
        (xo, h2, gu, act), got = _f_ffn(xm, n2, w1t, full["w_ffn_out"][l], l, f"f_ffn_{l}",
                                        _Gather([unit(f, 1) for f in _BIG]) if first else None)
        if first:
            full["w_in"][1], full["w_out"][1], full["w_ffn_in"][1], full["w_ffn_out"][1] = got
        saved.append(dict(x=x, proj=proj, h1=h1, o=o, s_f=s_f, s_b=s_b, c=c, xm=xm, ycat=ycat, h2=h2, gu=gu, act=act))
        x = xo

    parts = {f: [None] * LAYERS for f in _BIG}
    small = [None] * LAYERS
    norm1 = [None] * LAYERS
    upper = None
    top = (x, wts["final_g"].reshape(1, D), tgt)
    for l in reversed(range(LAYERS)):
        sv = saved[l]
        outs, got = _b_ffn(top, sv["xm"], n2, sv["gu"], full["w_ffn_in"][l].reshape(2 * FFH, D), full["w_ffn_out"][l], l,
                           f"b_ffn_{l}", _Scatter(upper) if upper else None)
        dgu_t, dxm, dxo_bf, d_n2 = outs[:4]
        if l == LAYERS - 1:
            loss, d_final = outs[4:]
        if upper:
            for f, p in zip(_BIG, got):
                parts[f][l + 1] = p
        g_f2 = _mm_wgrad(sv["act"], dxo_bf, NDEV // 2, True, "shared", 1, f"g_ffn_out_{l}").reshape(NDEV, FFH // NDEV, D)
        g_f1 = _mm_wgrad(dgu_t, sv["h2"], NDEV, True, "shared", 1, f"g_ffn_in_{l}")
        (dxm_bf, dy_gm, dO, d_g, dc, d_cvlg, d_cvlb, d_cb), (q_f1, q_f2) = _b_mixout(
            dxm, full["w_out"][l], sv["o"], sv["proj"], sv["c"], cv_lng, cv_lnb, l, f"b_mixout_{l}", _PairSwap([g_f1, g_f2]))
        h_f1 = _pair_add(g_f1, q_f1, f"pair_add_w_ffn_in_{l}")
        h_f2 = _pair_add(g_f2, q_f2, f"pair_add_w_ffn_out_{l}")
        g_out = _mm_wgrad(sv["ycat"], dxm_bf, 1, False, "shared", 1, f"g_out_{l}").reshape(NDEV, D // NDEV, D)
        last = l == 0
        (d_uv, d_ws, _, d_bs_fold, d_gmlg, d_gmlb), (q_out,) = _b_gm(
            sv["proj"], dy_gm, gm_lng, gm_lnb, ws_bf, wst_bf, bias, l, f"b_gm_{l}", _PairSwap([g_out]))
        h_out = _pair_add(g_out, q_out, f"pair_add_w_out_{l}")
        (d_ag, d_cw), got = _b_conv(sv["proj"], dc, cw, l, f"b_conv_{l}", _Scatter([h_f1]) if last else None)
        if last:
            parts["w_ffn_in"][l], = got
        small[l] = dict(gm_ln_g=d_gmlg[0], gm_ln_b=d_gmlb[0], gm_ws=d_ws, gm_bs=d_bs_fold[:, :HEADS].T, conv_w=d_cw[:KW],
                        conv_b=d_cb[0], conv_ln_g=d_cvlg[0], conv_ln_b=d_cvlb[0], norm2_g=d_n2[0])
        comm = None
        if last:
            early_g = {k: jnp.stack([small[ll][k] for ll in range(LAYERS)]) for k in small[0]}
            early_g["final_g"] = d_final[0]
            early_buf = _pack([early_g[k] for k in _SMALL_EARLY] + [loss], EARLY_ROWS)
            comm = _Comms([_Scatter([h_out, h_f2]), _Gather([(early_buf, "lead")])])
        g_f, g_b = _scan_pair(sv["proj"], dO, 1, rc["f"]["XI"], rc["b"]["XI"], gcf, gcb, False, f"b_ret_state_{l}")
        dqkv, got = _b_ret_out(sv["proj"], cos2, sin2, dO, sv["s_f"], sv["s_b"], g_f, g_b, rc, f"b_ret_out_{l}", comm)
        if last:
            parts["w_out"][l], parts["w_ffn_out"][l], early_parts = got
        (dproj, top, d_n1), _ = _b_inproj(d_uv, dqkv, d_g, d_ag, full["w_in"][l], sv["x"], n1, dxm, l, f"b_inproj_{l}")
        norm1[l] = d_n1[0]
        g_in = _mm_wgrad(sv["h1"], dproj, NDEV, False, "cols", 2, f"g_in_{l}")
        q_in, = _comm_only(_PairSwap([g_in]), f"pair_swap_w_in_{l}")
        h_in = _pair_add(g_in, q_in, f"pair_add_w_in_{l}")
        if last:
            tail = [h_in]
        else:
            upper = [h_in, h_out, h_f1, h_f2]
    late_buf = _pack([jnp.stack(norm1)], LATE_ROWS)
    parts["w_in"][0], late_parts = _comm_only(_Comms([_Scatter(tail), _Gather([(late_buf, "lead")])]), "exchange_last")
    return loss, top, parts, (early_parts, late_parts)


def _adamw(w, g, m, v):
    m = ADAM_B1 * m + (1.0 - ADAM_B1) * g
    v = ADAM_B2 * v + (1.0 - ADAM_B2) * (g * g)
    m_hat = m / (1.0 - ADAM_B1 ** ADAM_STEP)
    v_hat = v / (1.0 - ADAM_B2 ** ADAM_STEP)
    return -ADAM_LR * (m_hat / (jnp.sqrt(v_hat) + ADAM_EPS) + ADAM_WD * w), m, v


def _cast_blocks(ws):
    def body(*refs):
        ins, outs = refs[:len(ws)], refs[len(ws):]
        for k, src in enumerate(ins):
            for l in range(LAYERS):
                outs[k * LAYERS + l][...] = src[l].astype(BF16)

    outs = pl.pallas_call(body, name="cast_blocks", out_shape=[S(w.shape[1:], BF16) for w in ws for _ in range(LAYERS)],
                          compiler_params=_params())(*ws)
    return [list(outs[k * LAYERS:(k + 1) * LAYERS]) for k in range(len(ws))]


def _row_block(mm):
    return next(b for b in (256, 352, 128) if mm % b == 0)


def _sum_adam(parts, w, m, v, l, prev, name):
    _, mm, nn = parts.shape
    bm = _row_block(mm)

    def body(p_ref, w_ref, m_ref, v_ref, *rest):
        g_ref, d_ref, nm_ref, nv_ref = rest[-4:]
        g = p_ref[0].astype(F32)
        for s in range(1, 4):
            g = g + p_ref[s].astype(F32)
        g_ref[...] = g
        d_ref[...], nm_ref[...], nv_ref[...] = _adamw(w_ref[...], g, m_ref[...], v_ref[...])

    blk = lambda: pl.BlockSpec((None, bm, nn), lambda i: (l, i, 0))
    prev = list(prev) if prev else []
    return pl.pallas_call(
        body, grid=(mm // bm,), name=name,
        in_specs=[pl.BlockSpec((4, bm, nn), lambda i: (0, i, 0)), blk(), blk(), blk()] + [_ANY] * len(prev),
        out_specs=[blk() for _ in range(4)],
        out_shape=[S(w.shape, F32) for _ in range(4)],
        input_output_aliases={4 + j: j for j in range(len(prev))},
        compiler_params=_params("parallel"),
    )(parts, w, m, v, *prev)


def _sum_small(parts):
    n = len(parts)

    def body(*refs):
        for p_ref, o_ref in zip(refs[:n], refs[n:]):
            g = p_ref[0]
            for s in range(1, NDEV):
                g = g + p_ref[s]
            o_ref[...] = g

    return pl.pallas_call(body, name="sum_small", out_shape=[S(p.shape[1:], F32) for p in parts],
                          compiler_params=_params())(*parts)


def _adam_small(g, w, m, v):
    def body(g_ref, w_ref, m_ref, v_ref, d_ref, nm_ref, nv_ref):
        d_ref[...], nm_ref[...], nv_ref[...] = _adamw(w_ref[...], g_ref[...], m_ref[...], v_ref[...])

    return pl.pallas_call(body, name="adam_small", out_shape=[S(g.shape, F32)] * 3, compiler_params=_params())(g, w, m, v)


_SMALL = ("norm1_g", "gm_ln_g", "gm_ln_b", "gm_ws", "gm_bs", "conv_w", "conv_b", "conv_ln_g", "conv_ln_b",
          "norm2_g", "final_g")
_SMALL_EARLY = _SMALL[1:]
_NAMES = ("norm1_g", "w_in", "gm_ln_g", "gm_ln_b", "gm_ws", "gm_bs", "conv_w", "conv_b", "conv_ln_g", "conv_ln_b",
          "w_out", "norm2_g", "w_ffn_in", "w_ffn_out", "final_g")


def _pack(parts, rows):
    flat = jnp.concatenate([p.reshape(-1) for p in parts])
    return jnp.pad(flat, (0, rows * 1024 - flat.shape[0])).reshape(rows, 1024)


def _unpack(buf, shapes):
    flat = buf.reshape(-1)
    out, o = [], 0
    for shp in shapes:
        sz = int(np.prod(shp))
        out.append(flat[o:o + sz].reshape(shp))
        o += sz
    return out


def kernel(x, norm1_g, w_in, gm_ln_g, gm_ln_b, gm_ws, gm_bs, conv_w, conv_b, conv_ln_g, conv_ln_b, w_out, norm2_g, w_ffn_in, w_ffn_out, final_g, loss_target, m_norm1_g, m_w_in, m_gm_ln_g, m_gm_ln_b, m_gm_ws, m_gm_bs, m_conv_w, m_conv_b, m_conv_ln_g, m_conv_ln_b, m_w_out, m_norm2_g, m_w_ffn_in, m_w_ffn_out, m_final_g, v_norm1_g, v_w_in, v_gm_ln_g, v_gm_ln_b, v_gm_ws, v_gm_bs, v_conv_w, v_conv_b, v_conv_ln_g, v_conv_ln_b, v_w_out, v_norm2_g, v_w_ffn_in, v_w_ffn_out, v_final_g):
    w = dict(norm1_g=norm1_g, w_in=w_in, gm_ln_g=gm_ln_g, gm_ln_b=gm_ln_b, gm_ws=gm_ws, gm_bs=gm_bs, conv_w=conv_w,
             conv_b=conv_b, conv_ln_g=conv_ln_g, conv_ln_b=conv_ln_b, w_out=w_out, norm2_g=norm2_g, w_ffn_in=w_ffn_in,
             w_ffn_out=w_ffn_out, final_g=final_g)
    mo = dict(norm1_g=m_norm1_g, w_in=m_w_in, gm_ln_g=m_gm_ln_g, gm_ln_b=m_gm_ln_b, gm_ws=m_gm_ws, gm_bs=m_gm_bs,
              conv_w=m_conv_w, conv_b=m_conv_b, conv_ln_g=m_conv_ln_g, conv_ln_b=m_conv_ln_b, w_out=m_w_out,
              norm2_g=m_norm2_g, w_ffn_in=m_w_ffn_in, w_ffn_out=m_w_ffn_out, final_g=m_final_g)
    vo = dict(norm1_g=v_norm1_g, w_in=v_w_in, gm_ln_g=v_gm_ln_g, gm_ln_b=v_gm_ln_b, gm_ws=v_gm_ws, gm_bs=v_gm_bs,
              conv_w=v_conv_w, conv_b=v_conv_b, conv_ln_g=v_conv_ln_g, conv_ln_b=v_conv_ln_b, w_out=v_w_out,
              norm2_g=v_norm2_g, w_ffn_in=v_w_ffn_in, w_ffn_out=v_w_ffn_out, final_g=v_final_g)
    t = x.shape[1]
    me = 4 * lax.axis_index("x") + 2 * lax.axis_index("y") + lax.axis_index("c")
    cshard = conv_w.shape[2]

    cw_pad = jnp.pad(conv_w, ((0, 0), (0, 32 - KW), (0, CWP - cshard)))
    flip = lambda a: jnp.swapaxes(a, 1, 2)
    big = {f: tuple(flip(a[f]) if f == "w_ffn_in" else a[f] for a in (w, mo, vo)) for f in _BIG}
    sh = dict(zip(_BIG, _cast_blocks([big[f][0] for f in _BIG])))
    loss, dx, parts, small_parts = _step(x.reshape(t, D), loss_target.reshape(t, D), w, sh, cw_pad)

    grads, delta, new_m, new_v = {}, {}, {}, {}
    for f in _BIG:
        outs = None
        for l in reversed(range(LAYERS)):
            outs = _sum_adam(parts[f][l], *big[f], l, outs, f"sum_adam_{f}_{l}")
        grads[f], delta[f], new_m[f], new_v[f] = [flip(a) for a in outs] if f == "w_ffn_in" else outs

    early_sum, late_sum = _sum_small(small_parts)
    early_shapes = [(LAYERS, KW, CVW) if k == "conv_w" else w[k].shape for k in _SMALL_EARLY]
    grads["norm1_g"], = _unpack(late_sum, [w["norm1_g"].shape])
    *early, total = _unpack(early_sum, early_shapes + [()])
    for k, g in zip(_SMALL_EARLY, early):
        grads[k] = lax.dynamic_slice_in_dim(g, me * cshard, cshard, axis=2) if k == "conv_w" else g
    adam_rows = 144
    d_s, m_s, v_s = _adam_small(_pack([grads[k] for k in _SMALL], adam_rows), _pack([w[k] for k in _SMALL], adam_rows),
                                _pack([mo[k] for k in _SMALL], adam_rows), _pack([vo[k] for k in _SMALL], adam_rows))
    shapes = [w[k].shape for k in _SMALL]
    for dst, buf in ((delta, d_s), (new_m, m_s), (new_v, v_s)):
        for k, a in zip(_SMALL, _unpack(buf, shapes)):
            dst[k] = a

    return (total, dx.reshape(1, t, D), *[grads[k] for k in _NAMES], *[delta[k] for k in _NAMES],
            *[new_m[k] for k in _NAMES], *[new_v[k] for k in _NAMES])
```

```python
import functools

import numpy as np
import jax
import jax.numpy as jnp
from jax import lax
from jax.experimental import pallas as pl
from jax.experimental.pallas import tpu as pltpu

F32, BF16 = jnp.float32, jnp.bfloat16
S = jax.ShapeDtypeStruct

D = 1024
INW = 3072
GMW = 256
RETW = 512
CVW = 256
HEADS = 4
DH = 128
C = 128
KW = 31
HALO = 16
FFH = 2816
NDEV = 8
FFB = 2 * FFH // NDEV
FF_CHUNKS = ((0, 768), (768, 1536), (1536, 2304), (2304, FFH))
EPS = 1e-6
LAYERS = 2
SCALE = DH ** -0.5
VMEM_LIMIT = 56 * 1024 * 1024

ADAM_LR, ADAM_B1, ADAM_B2, ADAM_EPS, ADAM_WD, ADAM_STEP = 0.001, 0.9, 0.999, 1e-08, 0.01, 10

_SQRT_HALF = 0.7071067811865476
_INV_SQRT_2PI = 0.3989422804014327


def _params(*sem):
    return pltpu.CompilerParams(dimension_semantics=sem or None, vmem_limit_bytes=VMEM_LIMIT)


def _resident(shape, index_map):
    return pl.BlockSpec(shape, index_map, pipeline_mode=pl.Buffered(1))


def _dot(a, b):
    return jnp.dot(a, b, preferred_element_type=F32)


def _dot_nt(a, b):
    return lax.dot_general(a, b, (((1,), (1,)), ((), ())), preferred_element_type=F32)


def _dot_tn(a, b):
    return lax.dot_general(a, b, (((0,), (0,)), ((), ())), preferred_element_type=F32)


def _sigmoid(x):
    return 1.0 / (1.0 + jnp.exp(-x))


def _gelu_and_grad(x):
    cdf = 0.5 * (1.0 + lax.erf(x * _SQRT_HALF))
    return x * cdf, cdf + x * jnp.exp(-0.5 * x * x) * _INV_SQRT_2PI


def _silu_and_grad(x):
    s = _sigmoid(x)
    return x * s, s * (1.0 + x * (1.0 - s))


def _standardize(x):
    mu = jnp.mean(x, axis=-1, keepdims=True)
    d = x - mu
    rstd = lax.rsqrt(jnp.mean(d * d, axis=-1, keepdims=True) + EPS)
    return d * rstd, rstd


def _standardize_bwd(dxhat, xhat, rstd):
    m1 = jnp.mean(dxhat, axis=-1, keepdims=True)
    m2 = jnp.mean(dxhat * xhat, axis=-1, keepdims=True)
    return rstd * (dxhat - m1 - xhat * m2)


def _rms(x):
    return lax.rsqrt(jnp.mean(x * x, axis=-1, keepdims=True) + EPS)


def _rmsnorm_bwd(dy, x, r, g):
    u = dy * g
    return r * u - x * (r * r * r) * jnp.mean(u * x, axis=-1, keepdims=True)


def _col_sum(a):
    return jnp.sum(a, axis=0, keepdims=True)


def _rot(t, cos2, sin2):
    return t * cos2 + pltpu.roll(t, DH // 2, axis=1) * sin2


def _rot_t(dt, cos2, sin2):
    return dt * cos2 + pltpu.roll(dt * sin2, DH // 2, axis=1)


MESH = pl.DeviceIdType.MESH
_HBM = pl.BlockSpec(memory_space=pltpu.HBM)
_ANY = pl.BlockSpec(memory_space=pl.ANY)


def _place():
    x, y, c = lax.axis_index("x"), lax.axis_index("y"), lax.axis_index("c")
    return x, y, c, ((1 - x, y), (x, 1 - y), (1 - x, 1 - y))


def _slot(full, kind, width, i):
    if kind == "cols":
        return full.at[:, pl.ds(pl.multiple_of(i * width, 128), width)]
    if kind == "rows":
        return full.at[pl.ds(pl.multiple_of(i * width, 16), width), :]
    return full.at[i]


class _Gather:
    def __init__(self, units):
        self.units = units
        self.inputs = [u[0] for u in units]
        self.out_shape = []
        for src, kind in units:
            r, c = src.shape[-2:]
            shape = {"cols": (r, NDEV * c), "rows": (NDEV * r, c), "lead": (NDEV,) + src.shape}[kind]
            self.out_shape.append(S(shape, src.dtype))
        n = len(units)
        self.scratch = [pltpu.SemaphoreType.DMA((n, 7)), pltpu.SemaphoreType.DMA((n, 7)), pltpu.SemaphoreType.DMA((n,))]

    def run(self, phase, ins, outs, scr):
        ssem, rsem, lsem = scr
        x, y, c, chips = _place()
        me, sib = 4 * x + 2 * y + c, (x, y, 1 - c)
        idx = lambda chip, core: 4 * chip[0] + 2 * chip[1] + core
        for u, (src_arr, kind) in enumerate(self.units):
            src, full = ins[u], outs[u]
            width = src_arr.shape[-1] if kind == "cols" else src_arr.shape[-2]
            slot = functools.partial(_slot, full, kind, width)

            def copy(k, block, to, from_src=False):
                return pltpu.make_async_remote_copy(src_ref=src if from_src else slot(block), dst_ref=slot(block),
                                                    send_sem=ssem.at[u, k], recv_sem=rsem.at[u, k],
                                                    device_id=to, device_id_type=MESH)

            mine = lambda: pltpu.make_async_copy(src, slot(me), lsem.at[u])
            first = lambda: [copy(0, me, sib, True)] + [copy(1 + j, me, (*chip, c), True) for j, chip in enumerate(chips)]
            passed = lambda j: copy(4 + j, idx(chips[j], c), sib)
            if phase == "start":
                mine().start()
                for cp in first():
                    cp.start()
            elif phase == "forward":
                for j, chip in enumerate(chips):
                    copy(1 + j, idx(chip, c), sib).wait_recv()
                    passed(j).start()
            else:
                copy(0, idx((x, y), 1 - c), sib).wait_recv()
                for j, chip in enumerate(chips):
                    copy(4 + j, idx(chip, 1 - c), sib).wait_recv()
                for cp in first() + [passed(j) for j in range(3)]:
                    cp.wait_send()
                mine().wait()


class _Scatter:
    def __init__(self, units):
        self.units = units
        self.inputs = list(units)
        self.out_shape = [S(u.shape, u.dtype) for u in units]
        n = len(units)
        self.scratch = [pltpu.SemaphoreType.DMA((n, 3)), pltpu.SemaphoreType.DMA((n, 3)), pltpu.SemaphoreType.DMA((n,))]

    def run(self, phase, ins, outs, scr):
        ssem, rsem, lsem = scr
        x, y, c, chips = _place()
        myq = 2 * x + y
        for u in range(len(self.units)):
            h, p = ins[u], outs[u]

            def copy(k, chip, send_to_them):
                q = 2 * chip[0] + chip[1]
                return pltpu.make_async_remote_copy(src_ref=h.at[q], dst_ref=p.at[myq if send_to_them else q],
                                                    send_sem=ssem.at[u, k], recv_sem=rsem.at[u, k],
                                                    device_id=(*chip, c), device_id_type=MESH)

            mine = lambda: pltpu.make_async_copy(h.at[myq], p.at[myq], lsem.at[u])
            sends = lambda: [copy(k, chip, True) for k, chip in enumerate(chips)]
            if phase == "start":
                mine().start()
                for cp in sends():
                    cp.start()
            elif phase == "finish":
                for k, chip in enumerate(chips):
                    copy(k, chip, False).wait_recv()
                for cp in sends():
                    cp.wait_send()
                mine().wait()


class _Comms:
    def __init__(self, parts):
        self.parts = parts
        self.inputs = [a for p in parts for a in p.inputs]
        self.out_shape = [a for p in parts for a in p.out_shape]
        self.scratch = [a for p in parts for a in p.scratch]

    def run(self, phase, ins, outs, scr):
        i = o = s = 0
        for p in self.parts:
            ni, no, ns = len(p.inputs), len(p.out_shape), len(p.scratch)
            p.run(phase, ins[i:i + ni], outs[o:o + no], scr[s:s + ns])
            i, o, s = i + ni, o + no, s + ns


def _launch(body, grid, in_specs, out_specs, out_shape, scratch, args, name, sem, comm=None):
    if comm is None:
        outs = pl.pallas_call(body, grid=grid, name=name, in_specs=in_specs, out_specs=out_specs, out_shape=out_shape,
                              scratch_shapes=scratch, compiler_params=_params(*sem))(*args)
        return list(outs), []
    n_in, n_out, n_scr = len(args), len(out_shape), len(scratch)
    ci, co = len(comm.inputs), len(comm.out_shape)
    nsteps = int(np.prod(grid))
    fwd_step = (7 * nsteps) // 8

    def hosted(*refs):
        a = refs[:n_in]
        ca = refs[n_in:n_in + ci]
        o = refs[n_in + ci:n_in + ci + n_out]
        cout = refs[n_in + ci + n_out:n_in + ci + n_out + co]
        s = refs[n_in + ci + n_out + co:n_in + ci + n_out + co + n_scr]
        cs = refs[n_in + ci + n_out + co + n_scr:]
        step = pl.program_id(0)
        for d in range(1, len(grid)):
            step = step * grid[d] + pl.program_id(d)

        @pl.when(step == 0)
        def _():
            comm.run("start", ca, cout, cs)

        body(*a, *o, *s)

        @pl.when(step == fwd_step)
        def _():
            comm.run("forward", ca, cout, cs)

        @pl.when(step == nsteps - 1)
        def _():
            comm.run("finish", ca, cout, cs)

    outs = pl.pallas_call(
        hosted, grid=grid, name=name, in_specs=list(in_specs) + [_HBM] * ci, out_specs=list(out_specs) + [_HBM] * co,
        out_shape=list(out_shape) + comm.out_shape, scratch_shapes=list(scratch) + comm.scratch,
        compiler_params=_params(*["arbitrary"] * len(grid)))(*args, *comm.inputs)
    return list(outs[:n_out]), list(outs[n_out:])


def _comm_only(comm, name):
    ci, co = len(comm.inputs), len(comm.out_shape)

    def body(*refs):
        ca, cout, cs = refs[:ci], refs[ci:ci + co], refs[ci + co:]
        for phase in ("start", "forward", "finish"):
            comm.run(phase, ca, cout, cs)

    return pl.pallas_call(body, name=name, in_specs=[_HBM] * ci, out_specs=[_HBM] * co, out_shape=comm.out_shape,
                          scratch_shapes=comm.scratch, compiler_params=_params())(*comm.inputs)


def _ret_consts():
    idx = np.arange(C, dtype=np.float32)
    gf = (1.0 - np.exp2(-5.0 - np.arange(HEADS, dtype=np.float32))).astype(np.float32)
    out = {}
    for name, gamma, fwd in (("f", gf, True), ("b", gf[::-1].copy(), False)):
        lg = np.log(gamma).astype(np.float32)[:, None]
        diff = idx[:, None] - idx[None, :]
        if fwd:
            mask = diff >= 0
            dist = np.where(mask, diff, 0.0)
            zeta = np.exp(lg * (C - 1 - idx))
            xi = np.exp(lg * (idx + 1))
        else:
            mask = diff < 0
            dist = np.where(mask, -diff, 0.0)
            zeta = np.exp(lg * idx)
            xi = np.exp(lg * (C - idx))
        dm = np.where(mask[None], np.exp(lg[:, :, None] * dist[None]), 0.0).astype(np.float32)
        bc = lambda vec: np.ascontiguousarray(np.broadcast_to(vec.astype(np.float32)[:, :, None], (HEADS, C, DH)))
        out[name] = dict(D=dm, XI=bc(xi), ZETA=bc(zeta), gC=[float(v) for v in np.exp(lg[:, 0] * C).astype(np.float32)])
    return out


def _rope_tables(t):
    half = DH // 2
    inv_freq = (np.float32(10000.0) ** (-np.arange(half, dtype=np.float32) / np.float32(half))).astype(np.float32)
    ang = (np.arange(t, dtype=np.float32)[:, None] * inv_freq[None, :]).astype(np.float64)
    cos, sin = np.cos(ang).astype(np.float32), np.sin(ang).astype(np.float32)
    return np.concatenate([cos, cos], axis=1), np.concatenate([-sin, sin], axis=1)


def _f_inproj(x, g1, w, cos2, sin2, l, name, comm=None):
    t = x.shape[0]
    tm = 512

    def body(x_ref, g_ref, w_ref, cos_ref, sin_ref, proj_ref, ht_ref):
        xv = x_ref[...]
        h = (xv * _rms(xv) * g_ref[...]).astype(BF16)
        ht_ref[...] = h.T
        for nb in range(INW // 512):
            cs = slice(nb * 512, (nb + 1) * 512)
            res = _dot(h, w_ref[:, cs])
            if nb in (1, 2):
                for hh in range(HEADS):
                    r = _rot(res[:, hh * DH:(hh + 1) * DH], cos_ref[...], sin_ref[...])
                    proj_ref[:, nb * 512 + hh * DH:nb * 512 + (hh + 1) * DH] = (r * SCALE if nb == 2 else r).astype(BF16)
            else:
                proj_ref[:, cs] = res.astype(BF16)

    return _launch(
        body, (t // tm,),
        [pl.BlockSpec((tm, D), lambda i: (i, 0)),
         pl.BlockSpec((None, 1, D), lambda i: (l, 0, 0)),
         _resident((D, INW), lambda i: (0, 0)),
         pl.BlockSpec((tm, DH), lambda i: (i, 0)), pl.BlockSpec((tm, DH), lambda i: (i, 0))],
        [pl.BlockSpec((tm, INW), lambda i: (i, 0)), pl.BlockSpec((D, tm), lambda i: (0, i))],
        [S((t, INW), BF16), S((D, t), BF16)], [], (x, g1, w, cos2, sin2), name, ("parallel",), comm)


def _gm_chunk_fwd(u, v, lng, lnb, ws_ref, bias):
    au, dau = _gelu_and_grad(u)
    av, dav = _gelu_and_grad(v)
    vhat, rstd = _standardize(av)
    vn = (vhat * lng + lnb).astype(BF16)
    head = lax.broadcasted_iota(jnp.int32, (C, GMW), 1) // (GMW // HEADS)
    mixed = bias
    for h in range(HEADS):
        mixed = mixed + jnp.where(head == h, _dot(ws_ref[h], vn), 0.0)
    return au, dau, dav, vhat, rstd, vn, mixed, head


def _f_gm(proj, lng, lnb, ws_bf, bias, l, name):
    t = proj.shape[0]
    tm = 512

    def body(p_ref, lng_ref, lnb_ref, ws_ref, bias_ref, y_ref):
        for ci in range(tm // C):
            rows = slice(ci * C, (ci + 1) * C)
            u = p_ref[rows, 0:GMW].astype(F32)
            v = p_ref[rows, GMW:2 * GMW].astype(F32)
            au, _, _, _, _, _, mixed, _ = _gm_chunk_fwd(u, v, lng_ref[...], lnb_ref[...], ws_ref, bias_ref[...])
            y_ref[rows, :] = (au * mixed).astype(BF16)

    return pl.pallas_call(
        body, grid=(t // tm,), name=name,
        in_specs=[pl.BlockSpec((tm, 2 * GMW), lambda i: (i, 0)),
                  pl.BlockSpec((None, 1, GMW), lambda i: (l, 0, 0)),
                  pl.BlockSpec((None, 1, GMW), lambda i: (l, 0, 0)),
                  pl.BlockSpec((None, HEADS, C, C), lambda i: (l, 0, 0, 0)),
                  pl.BlockSpec((None, C, GMW), lambda i: (l, 0, 0))],
        out_specs=pl.BlockSpec((tm, GMW), lambda i: (i, 0)),
        out_shape=S((t, GMW), BF16),
        compiler_params=_params("parallel"),
    )(proj, lng, lnb, ws_bf, bias)


def _scan_pair(proj, other, col, wf, wb, gcf, gcb, first_is_f, name):
    t = proj.shape[0]
    n = t // C
    sc = min(SCAN_CHUNKS, n)
    nsteps = n // sc
    other_is_proj = other is None

    def body(a1, o1, a2, o2, w1_ref, w2_ref, out1, out2, st1, st2):
        @pl.when(pl.program_id(0) == 0)
        def _():
            st1[...] = jnp.zeros_like(st1)
            st2[...] = jnp.zeros_like(st2)

        def one(a_ref, o_ref, w_ref, gc, st, out, order):
            for h in range(HEADS):
                sl = slice(h * DH, (h + 1) * DH)
                incs = {}
                for j in order:
                    rows = slice(j * C, (j + 1) * C)
                    aw = (a_ref[rows, sl].astype(F32) * w_ref[h]).astype(BF16)
                    incs[j] = _dot_tn(aw, o_ref[rows, sl].astype(BF16))
                cur = st[h]
                for j in order:
                    out[j, h] = cur.astype(BF16)
                    cur = gc[h] * cur + incs[j]
                st[h] = cur

        g1, g2 = (gcf, gcb) if first_is_f else (gcb, gcf)
        one(a1, o1, w1_ref, g1, st1, out1, range(sc))
        one(a2, o2, w2_ref, g2, st2, out2, range(sc - 1, -1, -1))

    up = lambda i: i
    down = lambda i: nsteps - 1 - i

    def specs(ix):
        o_spec = pl.BlockSpec((sc * C, RETW), lambda i: (ix(i), 3 if other_is_proj else 0))
        return [pl.BlockSpec((sc * C, RETW), lambda i: (ix(i), col)), o_spec]

    const = lambda: pl.BlockSpec((HEADS, C, DH), lambda i: (0, 0, 0))
    oth = proj if other_is_proj else other
    w1, w2 = (wf, wb) if first_is_f else (wb, wf)
    out1, out2 = pl.pallas_call(
        body, grid=(nsteps,), name=name,
        in_specs=specs(up) + specs(down) + [const(), const()],
        out_specs=[pl.BlockSpec((sc, HEADS, DH, DH), lambda i: (up(i), 0, 0, 0)),
                   pl.BlockSpec((sc, HEADS, DH, DH), lambda i: (down(i), 0, 0, 0))],
        out_shape=[S((n, HEADS, DH, DH), BF16), S((n, HEADS, DH, DH), BF16)],
        scratch_shapes=[pltpu.VMEM((HEADS, DH, DH), F32), pltpu.VMEM((HEADS, DH, DH), F32)],
        compiler_params=_params("arbitrary"),
    )(proj, oth, proj, oth, w1, w2)
    return (out1, out2) if first_is_f else (out2, out1)


SCAN_CHUNKS = 8


RET_CHUNKS = 8


def _f_ret_out(proj, s_f, s_b, rc, name):
    t = proj.shape[0]
    nchunks = min(RET_CHUNKS, t // C)
    tm = nchunks * C

    def body(q_ref, k_ref, v_ref, sf_ref, sb_ref, d_ref, xif_ref, xib_ref, o_ref):
        for ci in range(nchunks):
            rows = slice(ci * C, (ci + 1) * C)
            for h in range(HEADS):
                sl = slice(h * DH, (h + 1) * DH)
                qh, kh, vh = q_ref[rows, sl], k_ref[rows, sl], v_ref[rows, sl]
                p = (_dot_nt(qh, kh) * d_ref[h]).astype(BF16)
                cross = _dot(qh, jnp.concatenate([sf_ref[ci, h], sb_ref[ci, h]], axis=1))
                o_ref[rows, sl] = _dot(p, vh) + xif_ref[h] * cross[:, 0:DH] + xib_ref[h] * cross[:, DH:2 * DH]

    const = lambda: pl.BlockSpec((HEADS, C, DH), lambda i: (0, 0, 0))
    state = lambda: pl.BlockSpec((nchunks, HEADS, DH, DH), lambda i: (i, 0, 0, 0))
    return pl.pallas_call(
        body, grid=(t // tm,), name=name,
        in_specs=[pl.BlockSpec((tm, RETW), lambda i, cb=cb: (i, cb)) for cb in (1, 2, 3)] +
                 [state(), state(), const(), const(), const()],
        out_specs=pl.BlockSpec((tm, RETW), lambda i: (i, 0)),
        out_shape=S((t, RETW), F32),
        compiler_params=_params("parallel"),
    )(proj, proj, proj, s_f, s_b, rc["f"]["D"] + rc["b"]["D"], rc["f"]["XI"], rc["b"]["XI"])


def _conv_halo_specs(t, tm, width, col):
    r = tm // HALO
    last = t // HALO - 1
    return [pl.BlockSpec((HALO, width), lambda i: (jnp.maximum(i * r - 1, 0), col)),
            pl.BlockSpec((tm, width), lambda i: (i, col)),
            pl.BlockSpec((HALO, width), lambda i: (jnp.minimum((i + 1) * r, last), col))]


def _fill_ext(ext, prev, cur, nxt, i, nt, tm):
    ext[0:HALO, :] = jnp.where(i > 0, prev, 0.0)
    ext[HALO:HALO + tm, :] = cur
    ext[HALO + tm:2 * HALO + tm, :] = jnp.where(i < nt - 1, nxt, 0.0)


def _glu(a_ref, g_ref):
    return a_ref[...].astype(F32) * _sigmoid(g_ref[...].astype(F32))


def _shifted_copies(ext, rot, tm):
    rows = tm + 2 * HALO - 8
    for b in range(1, 8):
        rot[b - 1, :, :] = ext[pl.ds(b, rows), :]


def _tap(ext, rot, r0, s, rb):
    a, b = divmod(s, 8)
    return ext[pl.ds(r0 + 8 * a, rb), :] if b == 0 else rot[b - 1, pl.ds(r0 + 8 * a, rb), :]


def _f_conv(proj, cw, cb, lng, lnb, l, name, comm=None):
    t = proj.shape[0]
    tm = 256
    nt = t // tm
    rb = 64

    def body(ap, ac, an, gp, gc, gn, cw_ref, cb_ref, lng_ref, lnb_ref, c_ref, y_ref, hext, hrot):
        i = pl.program_id(0)
        _fill_ext(hext, _glu(ap, gp), _glu(ac, gc), _glu(an, gn), i, nt, tm)
        _shifted_copies(hext, hrot, tm)
        for r0 in range(0, tm, rb):
            acc = jnp.zeros((rb, CVW), F32) + cb_ref[...]
            for j in range(KW):
                acc = acc + cw_ref[j:j + 1, :] * _tap(hext, hrot, r0, j + 1, rb)
            c_ref[r0:r0 + rb, :] = acc
            chat, _ = _standardize(acc)
            z = chat * lng_ref[...] + lnb_ref[...]
            y_ref[r0:r0 + rb, :] = (z * _sigmoid(z)).astype(BF16)

    vec = lambda: pl.BlockSpec((None, 1, CVW), lambda i: (l, 0, 0))
    return _launch(
        body, (nt,),
        _conv_halo_specs(t, tm, CVW, 10) + _conv_halo_specs(t, tm, CVW, 11) +
        [pl.BlockSpec((None, 32, CVW), lambda i: (l, 0, 0)), vec(), vec(), vec()],
        [pl.BlockSpec((tm, CVW), lambda i: (i, 0)), pl.BlockSpec((tm, CVW), lambda i: (i, 0))],
        [S((t, CVW), F32), S((t, CVW), BF16)],
        [pltpu.VMEM((tm + 2 * HALO, CVW), F32), pltpu.VMEM((7, tm + 2 * HALO - 8, CVW), F32)],
        (proj, proj, proj, proj, proj, proj, cw, cb, lng, lnb), name, ("parallel",), comm)


def _f_mixout(x, y_gm, y_cv, o, proj, w, name):
    t = x.shape[0]
    tm = 512

    def body(x_ref, ygm_ref, ycv_ref, o_ref, g_ref, w_ref, xm_ref, ycat_t_ref, ycat):
        ycat[:, 0:GMW] = ygm_ref[...]
        ycat[:, GMW + RETW:D] = ycv_ref[...]
        for h in range(HEADS):
            sl = slice(h * DH, (h + 1) * DH)
            ohat, _ = _standardize(o_ref[:, sl])
            g = g_ref[:, sl].astype(F32)
            ycat[:, GMW + h * DH:GMW + (h + 1) * DH] = (ohat * (g * _sigmoid(g))).astype(BF16)
        yc = ycat[...]
        ycat_t_ref[...] = yc.T
        xm_ref[...] = x_ref[...] + _dot(yc, w_ref[...])

    return pl.pallas_call(
        body, grid=(t // tm,), name=name,
        in_specs=[pl.BlockSpec((tm, D), lambda i: (i, 0)),
                  pl.BlockSpec((tm, GMW), lambda i: (i, 0)),
                  pl.BlockSpec((tm, CVW), lambda i: (i, 0)),
                  pl.BlockSpec((tm, RETW), lambda i: (i, 0)),
                  pl.BlockSpec((tm, RETW), lambda i: (i, 4)),
                  _resident((D, D), lambda i: (0, 0))],
        out_specs=[pl.BlockSpec((tm, D), lambda i: (i, 0)), pl.BlockSpec((D, tm), lambda i: (0, i))],
        out_shape=[S((t, D), F32), S((D, t), BF16)],
        scratch_shapes=[pltpu.VMEM((tm, D), BF16)],
        compiler_params=_params("parallel"),
    )(x, y_gm, y_cv, o, proj, w)


def _f_ffn(xm, g2, w1, w2, l, name, comm=None):
    t = xm.shape[0]
    tm = 512

    def body(x_ref, g_ref, w1_ref, w2_ref, xo_ref, h_ref, gu_ref, act_t_ref):
        xv = x_ref[...]
        h = (xv * _rms(xv) * g_ref[...]).astype(BF16)
        h_ref[...] = h
        acc = xv
        for a, b in FF_CHUNKS:
            gate = _dot_nt(h, w1_ref[a:b, :])
            up = _dot_nt(h, w1_ref[FFH + a:FFH + b, :])
            gu_ref[:, a:b] = gate.astype(BF16)
            gu_ref[:, FFH + a:FFH + b] = up.astype(BF16)
            av = ((gate * _sigmoid(gate)) * up).astype(BF16)
            act_t_ref[a:b, :] = av.T
            acc = acc + _dot(av, w2_ref[a:b, :])
        xo_ref[...] = acc

    return _launch(
        body, (t // tm,),
        [pl.BlockSpec((tm, D), lambda i: (i, 0)),
         pl.BlockSpec((None, 1, D), lambda i: (l, 0, 0)),
         _resident((2 * FFH, D), lambda i: (0, 0)),
         _resident((FFH, D), lambda i: (0, 0))],
        [pl.BlockSpec((tm, D), lambda i: (i, 0)), pl.BlockSpec((tm, D), lambda i: (i, 0)),
         pl.BlockSpec((tm, 2 * FFH), lambda i: (i, 0)), pl.BlockSpec((FFH, tm), lambda i: (0, i))],
        [S((t, D), F32), S((t, D), BF16), S((t, 2 * FFH), BF16), S((FFH, t), BF16)],
        [], (xm, g2, w1, w2), name, ("parallel",), comm)


def _b_ffn(top, xm, g2, gu, w1, w2, l, name, comm=None):
    t = xm.shape[0]
    tm = 256
    from_loss = isinstance(top, tuple)
    n_top = 3 if from_loss else 1

    def body(*refs):
        top_refs = refs[:n_top]
        x_ref, g_ref, gu_ref, w1_ref, w2_ref, dgu_ref, dxm_ref, dxb_ref, dg_ref = refs[n_top:n_top + 9]
        first = pl.program_id(0) == 0

        @pl.when(first)
        def _():
            dg_ref[...] = jnp.zeros_like(dg_ref)

        if from_loss:
            xo_ref, fg_ref, t_ref = top_refs
            loss_ref, dfg_ref = refs[n_top + 9:n_top + 11]

            @pl.when(first)
            def _():
                loss_ref[...] = jnp.zeros_like(loss_ref)
                dfg_ref[...] = jnp.zeros_like(dfg_ref)

            xo = xo_ref[...]
            ro = _rms(xo)
            xr = xo * ro
            err = xr * fg_ref[...] - t_ref[...]
            loss_ref[...] += (0.5 / D) * _col_sum(jnp.sum(err * err, axis=1, keepdims=True))
            dy = err * (1.0 / D)
            dfg_ref[...] += _col_sum(dy * xr)
            dxo = _rmsnorm_bwd(dy, xo, ro, fg_ref[...])
        else:
            dxo = top_refs[0][...]
        dxb = dxo.astype(BF16)
        dxb_ref[...] = dxb
        dh = jnp.zeros((tm, D), F32)
        for a, b in FF_CHUNKS:
            dact = _dot_nt(dxb, w2_ref[a:b, :])
            gate = gu_ref[:, a:b].astype(F32)
            up = gu_ref[:, FFH + a:FFH + b].astype(F32)
            sg, dsg = _silu_and_grad(gate)
            dgate = (dact * up * dsg).astype(BF16)
            dup = (dact * sg).astype(BF16)
            dgu_ref[a:b, :] = dgate.T
            dgu_ref[FFH + a:FFH + b, :] = dup.T
            dh = dh + _dot(dgate, w1_ref[a:b, :]) + _dot(dup, w1_ref[FFH + a:FFH + b, :])
        xv = x_ref[...]
        r = _rms(xv)
        dg_ref[...] += _col_sum(dh * xv * r)
        dxm_ref[...] = dxo + _rmsnorm_bwd(dh, xv, r, g_ref[...])

    tok = lambda: pl.BlockSpec((tm, D), lambda i: (i, 0))
    vec = lambda: pl.BlockSpec((1, D), lambda i: (0, 0))
    top_specs = [tok(), vec(), tok()] if from_loss else [tok()]
    extra_specs = [pl.BlockSpec((1, 1), lambda i: (0, 0)), vec()] if from_loss else []
    extra_shape = [S((1, 1), F32), S((1, D), F32)] if from_loss else []
    return _launch(
        body, (t // tm,),
        top_specs + [tok(), pl.BlockSpec((None, 1, D), lambda i: (l, 0, 0)),
                     pl.BlockSpec((tm, 2 * FFH), lambda i: (i, 0)),
                     _resident((2 * FFH, D), lambda i: (0, 0)),
                     _resident((FFH, D), lambda i: (0, 0))],
        [pl.BlockSpec((2 * FFH, tm), lambda i: (0, i)), tok(), tok(), vec()] + extra_specs,
        [S((2 * FFH, t), BF16), S((t, D), F32), S((t, D), BF16), S((1, D), F32)] + extra_shape,
        [],
        ((*top,) if from_loss else (top,)) + (xm, g2, gu, w1, w2), name, ("arbitrary",), comm)


def _mm_wgrad(at, b, pieces, at_rows, b_mode, group, name):
    bt = 2048
    t = at.shape[-1]
    bt = min(bt, t)
    nt = t // bt
    if at_rows:
        ka = at.shape[0] // pieces
        a_spec = pl.BlockSpec((ka, bt), lambda j, tt: (j, tt))
    else:
        ka = at.shape[0]
        a_spec = pl.BlockSpec((ka, bt), lambda j, tt: (0, tt))
    if b_mode == "shared":
        nb, b_spec = b.shape[1], pl.BlockSpec((bt, b.shape[1]), lambda j, tt: (tt, 0))
    elif b_mode == "cols":
        nb = b.shape[1] // pieces
        b_spec = pl.BlockSpec((bt, group * nb), lambda j, tt: (tt, j))
    else:
        nb, b_spec = b.shape[2], pl.BlockSpec((None, bt, b.shape[2]), lambda j, tt: (j, tt, 0))
    assert group == 1 or b_mode == "cols"

    def body(a_ref, b_ref, o_ref, acc):
        tt = pl.program_id(1)

        @pl.when(tt == 0)
        def _():
            acc[...] = jnp.zeros_like(acc)

        acc[...] += _dot(a_ref[...], b_ref[...])

        @pl.when(tt == nt - 1)
        def _():
            for k in range(group):
                o_ref[k] = acc[:, k * nb:(k + 1) * nb].astype(BF16)

    return pl.pallas_call(
        body, grid=(pieces // group, nt), name=name,
        in_specs=[a_spec, b_spec],
        out_specs=pl.BlockSpec((group, ka, nb), lambda j, tt: (j, 0, 0)),
        out_shape=S((pieces, ka, nb), BF16),
        scratch_shapes=[pltpu.VMEM((ka, group * nb), F32)],
        compiler_params=_params("parallel", "arbitrary"),
    )(at, b)


def _b_mixout(dxm, w, o, proj, c, lng, lnb, l, name, comm=None):
    t = dxm.shape[0]
    tm = 256

    def body(dxm_ref, w_ref, o_ref, g_ref, c_ref, lng_ref, lnb_ref,
             dxb_ref, dygm_ref, dO_ref, dg_ref, dc_ref, dlg_ref, dlb_ref, dcb_ref):
        @pl.when(pl.program_id(0) == 0)
        def _():
            dlg_ref[...] = jnp.zeros_like(dlg_ref)
            dlb_ref[...] = jnp.zeros_like(dlb_ref)
            dcb_ref[...] = jnp.zeros_like(dcb_ref)

        dxb = dxm_ref[...].astype(BF16)
        dxb_ref[...] = dxb
        dy = _dot_nt(dxb, w_ref[...])
        dygm_ref[...] = dy[:, 0:GMW]
        for h in range(HEADS):
            sl = slice(h * DH, (h + 1) * DH)
            ohat, rstd = _standardize(o_ref[:, sl])
            sg, dsg = _silu_and_grad(g_ref[:, sl].astype(F32))
            dyr = dy[:, GMW + h * DH:GMW + (h + 1) * DH]
            dg_ref[:, sl] = (dyr * ohat * dsg).astype(BF16)
            dO_ref[:, sl] = _standardize_bwd(dyr * sg, ohat, rstd)
        chat, rstd = _standardize(c_ref[...])
        z = chat * lng_ref[...] + lnb_ref[...]
        _, dsz = _silu_and_grad(z)
        dz = dy[:, GMW + RETW:D] * dsz
        dlg_ref[...] += _col_sum(dz * chat)
        dlb_ref[...] += _col_sum(dz)
        dc = _standardize_bwd(dz * lng_ref[...], chat, rstd)
        dcb_ref[...] += _col_sum(dc)
        dc_ref[...] = dc

    vec = lambda: pl.BlockSpec((None, 1, CVW), lambda i: (l, 0, 0))
    acc = lambda: pl.BlockSpec((1, CVW), lambda i: (0, 0))
    return _launch(
        body, (t // tm,),
        [pl.BlockSpec((tm, D), lambda i: (i, 0)),
         _resident((D, D), lambda i: (0, 0)),
         pl.BlockSpec((tm, RETW), lambda i: (i, 0)),
         pl.BlockSpec((tm, RETW), lambda i: (i, 4)),
         pl.BlockSpec((tm, CVW), lambda i: (i, 0)), vec(), vec()],
        [pl.BlockSpec((tm, D), lambda i: (i, 0)), pl.BlockSpec((tm, GMW), lambda i: (i, 0)),
         pl.BlockSpec((tm, RETW), lambda i: (i, 0)), pl.BlockSpec((tm, RETW), lambda i: (i, 0)),
         pl.BlockSpec((tm, CVW), lambda i: (i, 0)), acc(), acc(), acc()],
        [S((t, D), BF16), S((t, GMW), F32), S((t, RETW), F32), S((t, RETW), BF16), S((t, CVW), F32),
         S((1, CVW), F32), S((1, CVW), F32), S((1, CVW), F32)],
        [], (dxm, w, o, proj, c, lng, lnb), name, ("arbitrary",), comm)


def _b_gm(proj, dy, lng, lnb, ws_bf, wst_bf, bias, l, name, comm=None):
    t = proj.shape[0]
    tm = 512
    nt = t // tm

    def body(p_ref, dy_ref, lng_ref, lnb_ref, ws_ref, wst_ref, bias_ref,
             duv_ref, dws_ref, dbias_ref, dbs_ref, dlg_ref, dlb_ref):
        @pl.when(pl.program_id(0) == 0)
        def _():
            dws_ref[...] = jnp.zeros_like(dws_ref)
            dbias_ref[...] = jnp.zeros_like(dbias_ref)
            dbs_ref[...] = jnp.zeros_like(dbs_ref)
            dlg_ref[...] = jnp.zeros_like(dlg_ref)
            dlb_ref[...] = jnp.zeros_like(dlb_ref)

        for ci in range(tm // C):
            rows = slice(ci * C, (ci + 1) * C)
            u = p_ref[rows, 0:GMW].astype(F32)
            v = p_ref[rows, GMW:2 * GMW].astype(F32)
            au, dau, dav, vhat, rstd, vn, mixed, head = _gm_chunk_fwd(u, v, lng_ref[...], lnb_ref[...], ws_ref, bias_ref[...])
            dyc = dy_ref[rows, :]
            dmixed = dyc * au
            dmb = dmixed.astype(BF16)
            dbias_ref[...] += dmixed
            dvn = jnp.zeros((C, GMW), F32)
            for h in range(HEADS):
                dws_ref[h] += _dot_nt(jnp.where(head == h, dmixed, 0.0).astype(BF16), vn)
                dvn = dvn + jnp.where(head == h, _dot(wst_ref[h], dmb), 0.0)
            dlg_ref[...] += _col_sum(dvn * vhat)
            dlb_ref[...] += _col_sum(dvn)
            dav_in = _standardize_bwd(dvn * lng_ref[...], vhat, rstd)
            duv_ref[rows, 0:GMW] = (dyc * mixed * dau).astype(BF16)
            duv_ref[rows, GMW:2 * GMW] = (dav_in * dav).astype(BF16)

        @pl.when(pl.program_id(0) == nt - 1)
        def _():
            head = lax.broadcasted_iota(jnp.int32, (C, GMW), 1) // (GMW // HEADS)
            lane = lax.broadcasted_iota(jnp.int32, (C, 128), 1)
            fold = jnp.zeros((C, 128), F32)
            for h in range(HEADS):
                col = jnp.sum(jnp.where(head == h, dbias_ref[...], 0.0), axis=1, keepdims=True)
                fold = jnp.where(lane == h, col, fold)
            dbs_ref[...] = fold

    vec = lambda: pl.BlockSpec((None, 1, GMW), lambda i: (l, 0, 0))
    mats = lambda: pl.BlockSpec((None, HEADS, C, C), lambda i: (l, 0, 0, 0))
    return _launch(
        body, (nt,),
        [pl.BlockSpec((tm, 2 * GMW), lambda i: (i, 0)), pl.BlockSpec((tm, GMW), lambda i: (i, 0)),
         vec(), vec(), mats(), mats(), pl.BlockSpec((None, C, GMW), lambda i: (l, 0, 0))],
        [pl.BlockSpec((tm, 2 * GMW), lambda i: (i, 0)),
         pl.BlockSpec((HEADS, C, C), lambda i: (0, 0, 0)),
         pl.BlockSpec((C, GMW), lambda i: (0, 0)), pl.BlockSpec((C, 128), lambda i: (0, 0)),
         pl.BlockSpec((1, GMW), lambda i: (0, 0)), pl.BlockSpec((1, GMW), lambda i: (0, 0))],
        [S((t, 2 * GMW), BF16), S((HEADS, C, C), F32), S((C, GMW), F32), S((C, 128), F32),
         S((1, GMW), F32), S((1, GMW), F32)],
        [], (proj, dy, lng, lnb, ws_bf, wst_bf, bias), name, ("arbitrary",), comm)


def _b_conv(proj, dc, cw, l, name, comm=None):
    t = proj.shape[0]
    tm = 256
    nt = t // tm
    rb = 64

    def body(ap, ac, an, gp, gc, gn, dp, dcur, dn, cw_ref, dag_ref, dcw_ref, hext, dext, hrot, drot):
        i = pl.program_id(0)

        @pl.when(i == 0)
        def _():
            dcw_ref[...] = jnp.zeros_like(dcw_ref)

        _fill_ext(hext, _glu(ap, gp), _glu(ac, gc), _glu(an, gn), i, nt, tm)
        _fill_ext(dext, dp[...], dcur[...], dn[...], i, nt, tm)
        _shifted_copies(hext, hrot, tm)
        _shifted_copies(dext, drot, tm)
        for j in range(KW):
            dcw_ref[j:j + 1, :] += _col_sum(dcur[...] * _tap(hext, hrot, 0, j + 1, tm))
        for r0 in range(0, tm, rb):
            dh = jnp.zeros((rb, CVW), F32)
            for j in range(KW):
                dh = dh + cw_ref[j:j + 1, :] * _tap(dext, drot, r0, 2 * HALO - 1 - j, rb)
            a = ac[r0:r0 + rb, :].astype(F32)
            s = _sigmoid(gc[r0:r0 + rb, :].astype(F32))
            dag_ref[r0:r0 + rb, 0:CVW] = (dh * s).astype(BF16)
            dag_ref[r0:r0 + rb, CVW:2 * CVW] = (dh * a * s * (1.0 - s)).astype(BF16)

    dspecs = _conv_halo_specs(t, tm, CVW, 0)
    return _launch(
        body, (nt,),
        _conv_halo_specs(t, tm, CVW, 10) + _conv_halo_specs(t, tm, CVW, 11) + dspecs +
        [pl.BlockSpec((None, 32, CVW), lambda i: (l, 0, 0))],
        [pl.BlockSpec((tm, 2 * CVW), lambda i: (i, 0)), pl.BlockSpec((32, CVW), lambda i: (0, 0))],
        [S((t, 2 * CVW), BF16), S((32, CVW), F32)],
        [pltpu.VMEM((tm + 2 * HALO, CVW), F32), pltpu.VMEM((tm + 2 * HALO, CVW), F32),
         pltpu.VMEM((7, tm + 2 * HALO - 8, CVW), F32), pltpu.VMEM((7, tm + 2 * HALO - 8, CVW), F32)],
        (proj, proj, proj, proj, proj, proj, dc, dc, dc, cw), name, ("arbitrary",), comm)


def _b_ret_out(proj, cos2, sin2, dO, s_f, s_b, g_f, g_b, rc, name, comm=None):
    t = proj.shape[0]
    nchunks = min(RET_CHUNKS, t // C)
    tm = nchunks * C

    def body(q_ref, k_ref, v_ref, cos_ref, sin_ref, dO_ref, sf_ref, sb_ref, gf_ref, gb_ref,
             d_ref, xif_ref, xib_ref, zef_ref, zeb_ref, dq_ref, dk_ref, dv_ref):
        for ci in range(nchunks):
            rows = slice(ci * C, (ci + 1) * C)
            cos_v, sin_v = cos_ref[rows, :], sin_ref[rows, :]
            for h in range(HEADS):
                sl = slice(h * DH, (h + 1) * DH)
                qh, kh, vh = q_ref[rows, sl], k_ref[rows, sl], v_ref[rows, sl]
                dOh = dO_ref[rows, sl].astype(BF16)
                dm = d_ref[h]
                p = (_dot_nt(qh, kh) * dm).astype(BF16)
                dp = (_dot_nt(dOh, vh) * dm).astype(BF16)
                from_s = _dot_nt(dOh, jnp.concatenate([sf_ref[ci, h], sb_ref[ci, h]], axis=0))
                from_g = _dot_nt(vh, jnp.concatenate([gf_ref[ci, h], gb_ref[ci, h]], axis=0))
                kg = _dot(kh, jnp.concatenate([gf_ref[ci, h], gb_ref[ci, h]], axis=1))
                dqr = _dot(dp, kh) + xif_ref[h] * from_s[:, 0:DH] + xib_ref[h] * from_s[:, DH:2 * DH]
                dkr = (_dot_tn(dp, qh) + zef_ref[h] * from_g[:, 0:DH] + zeb_ref[h] * from_g[:, DH:2 * DH]) * SCALE
                dv = _dot_tn(p, dOh) + zef_ref[h] * kg[:, 0:DH] + zeb_ref[h] * kg[:, DH:2 * DH]
                dq_ref[rows, sl] = _rot_t(dqr, cos_v, sin_v).astype(BF16)
                dk_ref[rows, sl] = _rot_t(dkr, cos_v, sin_v).astype(BF16)
                dv_ref[rows, sl] = dv.astype(BF16)

    const = lambda: pl.BlockSpec((HEADS, C, DH), lambda i: (0, 0, 0))
    state = lambda: pl.BlockSpec((nchunks, HEADS, DH, DH), lambda i: (i, 0, 0, 0))
    tok = lambda: pl.BlockSpec((tm, RETW), lambda i: (i, 0))
    return _launch(
        body, (t // tm,),
        [pl.BlockSpec((tm, RETW), lambda i, cb=cb: (i, cb)) for cb in (1, 2, 3)] +
        [pl.BlockSpec((tm, DH), lambda i: (i, 0)), pl.BlockSpec((tm, DH), lambda i: (i, 0)), tok(),
         state(), state(), state(), state()] + [const() for _ in range(5)],
        [tok(), tok(), tok()],
        [S((t, RETW), BF16) for _ in range(3)], [],
        (proj, proj, proj, cos2, sin2, dO, s_f, s_b, g_f, g_b, rc["f"]["D"] + rc["b"]["D"],
         rc["f"]["XI"], rc["b"]["XI"], rc["f"]["ZETA"], rc["b"]["ZETA"]), name, ("parallel",), comm)


def _b_inproj(d_uv, dqkv, d_g, d_ag, w, x, g1, dxm, l, name, comm=None):
    t = x.shape[0]
    tm = 512

    def body(duv_ref, dq_ref, dk_ref, dv_ref, dg_ref, dag_ref, w_ref, x_ref, g_ref, dxm_ref,
             dp_ref, dx_ref, dn_ref):
        @pl.when(pl.program_id(0) == 0)
        def _():
            dn_ref[...] = jnp.zeros_like(dn_ref)

        for k, part in enumerate((duv_ref, dq_ref, dk_ref, dv_ref, dg_ref, dag_ref)):
            dp_ref[:, 512 * k:512 * (k + 1)] = part[...]
        dh = _dot_nt(dp_ref[...], w_ref[...])
        xv = x_ref[...]
        r = _rms(xv)
        dn_ref[...] += _col_sum(dh * xv * r)
        dx_ref[...] = dxm_ref[...] + _rmsnorm_bwd(dh, xv, r, g_ref[...])

    half = lambda: pl.BlockSpec((tm, 512), lambda i: (i, 0))
    full = lambda: pl.BlockSpec((tm, D), lambda i: (i, 0))
    return _launch(
        body, (t // tm,),
        [half() for _ in range(6)] +
        [_resident((D, INW), lambda i: (0, 0)), full(), pl.BlockSpec((None, 1, D), lambda i: (l, 0, 0)), full()],
        [pl.BlockSpec((tm, INW), lambda i: (i, 0)), full(), pl.BlockSpec((1, D), lambda i: (0, 0))],
        [S((t, INW), BF16), S((t, D), F32), S((1, D), F32)], [],
        (d_uv, *dqkv, d_g, d_ag, w, x, g1, dxm), name, ("arbitrary",), comm)


class _PairSwap:
    def __init__(self, units):
        self.inputs = list(units)
        self.out_shape = [S((4,) + u.shape[1:], BF16) for u in units]
        n = len(units)
        self.scratch = [pltpu.SemaphoreType.DMA((n, 4)), pltpu.SemaphoreType.DMA((n, 4))]

    def run(self, phase, ins, outs, scr):
        ssem, rsem = scr
        x, y, c, _ = _place()
        copies = lambda: [pltpu.make_async_remote_copy(src_ref=ins[u].at[2 * chip + (1 - c)], dst_ref=outs[u].at[chip],
                                                       send_sem=ssem.at[u, chip], recv_sem=rsem.at[u, chip],
                                                       device_id=(x, y, 1 - c), device_id_type=MESH)
                          for u in range(len(self.inputs)) for chip in range(4)]
        if phase == "start":
            for cp in copies():
                cp.start()
        elif phase == "finish":
            for cp in copies():
                cp.wait_recv()
            for cp in copies():
                cp.wait_send()


def _pair_add(g, q, name):
    _, mm, nn = g.shape
    bm = _row_block(mm)

    def body(c_ref, g_ref, q_ref, h_ref):
        h_ref[...] = (g_ref[...].astype(F32) + q_ref[...].astype(F32)).astype(BF16)

    blk = lambda: pl.BlockSpec((None, bm, nn), lambda qq, i, c_ref: (qq, i, 0))
    return pl.pallas_call(
        body, name=name,
        grid_spec=pltpu.PrefetchScalarGridSpec(
            num_scalar_prefetch=1, grid=(4, mm // bm),
            in_specs=[pl.BlockSpec((None, None, bm, nn), lambda qq, i, c_ref: (qq, c_ref[0], i, 0)), blk()],
            out_specs=blk()),
        out_shape=S((4, mm, nn), BF16),
        compiler_params=_params("parallel", "parallel"),
    )(lax.axis_index("c").astype(jnp.int32).reshape(1), g.reshape(4, 2, mm, nn), q)


_BIG = ("w_in", "w_out", "w_ffn_in", "w_ffn_out")
_KIND = dict(w_in="cols", w_out="rows", w_ffn_in="lead", w_ffn_out="rows")
CWP = 128
EARLY_ROWS, LATE_ROWS = 152, 8


def _step(x, tgt, wts, sh, cw_pad):
    t = x.shape[0]
    rc = _ret_consts()
    cos2, sin2 = (jnp.asarray(a) for a in _rope_tables(t))
    n1 = wts["norm1_g"].reshape(LAYERS, 1, D)
    n2 = wts["norm2_g"].reshape(LAYERS, 1, D)
    gm_lng = wts["gm_ln_g"].reshape(LAYERS, 1, GMW)
    gm_lnb = wts["gm_ln_b"].reshape(LAYERS, 1, GMW)
    ws_bf = wts["gm_ws"].astype(BF16)
    wst_bf = jnp.swapaxes(wts["gm_ws"], 2, 3).astype(BF16)
    bias = jnp.repeat(jnp.swapaxes(wts["gm_bs"], 1, 2), GMW // HEADS, axis=2)
    cb = wts["conv_b"].reshape(LAYERS, 1, CVW)
    cv_lng = wts["conv_ln_g"].reshape(LAYERS, 1, CVW)
    cv_lnb = wts["conv_ln_b"].reshape(LAYERS, 1, CVW)
    unit = lambda f, l: (sh[f][l], _KIND[f])
    cshard = CVW // NDEV

    full = {f: [None] * LAYERS for f in _BIG}
    full["w_in"][0], cw_all = _comm_only(_Gather([unit("w_in", 0), (cw_pad, "lead")]), "gather_first")
    cw = jnp.transpose(cw_all[:, :, :, :cshard], (1, 2, 0, 3)).reshape(LAYERS, 32, CVW)

    gcf, gcb = rc["f"]["gC"], rc["b"]["gC"]
    saved = []
    for l in range(LAYERS):
        first = l == 0
        (proj, h1), got = _f_inproj(x, n1, full["w_in"][l], cos2, sin2, l, f"f_inproj_{l}",
                                    _Gather([unit("w_ffn_in", 0)]) if first else None)
        if first:
            full["w_ffn_in"][0], = got
        y_gm = _f_gm(proj, gm_lng, gm_lnb, ws_bf, bias, l, f"f_gm_{l}")
        s_f, s_b = _scan_pair(proj, None, 2, rc["f"]["ZETA"], rc["b"]["ZETA"], gcf, gcb, True, f"f_ret_state_{l}")
        o = _f_ret_out(proj, s_f, s_b, rc, f"f_ret_out_{l}")
        (c, y_cv), got = _f_conv(proj, cw, cb, cv_lng, cv_lnb, l, f"f_conv_{l}",
                                 _Gather([unit("w_out", 0), unit("w_ffn_out", 0)]) if first else None)
        if first:
            full["w_out"][0], full["w_ffn_out"][0] = got
        xm, ycat = _f_mixout(x, y_gm, y_cv, o, proj, full["w_out"][l], f"f_mixout_{l}")
        w1t = full["w_ffn_in"][l].reshape(2 * FFH, D)
        (xo, h2, gu, act), got = _f_ffn(xm, n2, w1t, full["w_ffn_out"][l], l, f"f_ffn_{l}",
                                        _Gather([unit(f, 1) for f in _BIG]) if first else None)
        if first:
            full["w_in"][1], full["w_out"][1], full["w_ffn_in"][1], full["w_ffn_out"][1] = got
        saved.append(dict(x=x, proj=proj, h1=h1, o=o, s_f=s_f, s_b=s_b, c=c, xm=xm, ycat=ycat, h2=h2, gu=gu, act=act))
        x = xo

    parts = {f: [None] * LAYERS for f in _BIG}
    small = [None] * LAYERS
    norm1 = [None] * LAYERS
    upper = None
    top = (x, wts["final_g"].reshape(1, D), tgt)
    for l in reversed(range(LAYERS)):
        sv = saved[l]
        outs, got = _b_ffn(top, sv["xm"], n2, sv["gu"], full["w_ffn_in"][l].reshape(2 * FFH, D), full["w_ffn_out"][l], l,
                           f"b_ffn_{l}", _Scatter(upper) if upper else None)
        dgu_t, dxm, dxo_bf, d_n2 = outs[:4]
        if l == LAYERS - 1:
            loss, d_final = outs[4:]
        if upper:
            for f, p in zip(_BIG, got):
                parts[f][l + 1] = p
        g_f2 = _mm_wgrad(sv["act"], dxo_bf, NDEV // 2, True, "shared", 1, f"g_ffn_out_{l}").reshape(NDEV, FFH // NDEV, D)
        g_f1 = _mm_wgrad(dgu_t, sv["h2"], NDEV, True, "shared", 1, f"g_ffn_in_{l}")
        (dxm_bf, dy_gm, dO, d_g, dc, d_cvlg, d_cvlb, d_cb), (q_f1, q_f2) = _b_mixout(
            dxm, full["w_out"][l], sv["o"], sv["proj"], sv["c"], cv_lng, cv_lnb, l, f"b_mixout_{l}", _PairSwap([g_f1, g_f2]))
        h_f1 = _pair_add(g_f1, q_f1, f"pair_add_w_ffn_in_{l}")
        h_f2 = _pair_add(g_f2, q_f2, f"pair_add_w_ffn_out_{l}")
        g_out = _mm_wgrad(sv["ycat"], dxm_bf, 1, False, "shared", 1, f"g_out_{l}").reshape(NDEV, D // NDEV, D)
        last = l == 0
        (d_uv, d_ws, _, d_bs_fold, d_gmlg, d_gmlb), (q_out,) = _b_gm(
            sv["proj"], dy_gm, gm_lng, gm_lnb, ws_bf, wst_bf, bias, l, f"b_gm_{l}", _PairSwap([g_out]))
        h_out = _pair_add(g_out, q_out, f"pair_add_w_out_{l}")
        (d_ag, d_cw), got = _b_conv(sv["proj"], dc, cw, l, f"b_conv_{l}", _Scatter([h_f1]) if last else None)
        if last:
            parts["w_ffn_in"][l], = got
        small[l] = dict(gm_ln_g=d_gmlg[0], gm_ln_b=d_gmlb[0], gm_ws=d_ws, gm_bs=d_bs_fold[:, :HEADS].T, conv_w=d_cw[:KW],
                        conv_b=d_cb[0], conv_ln_g=d_cvlg[0], conv_ln_b=d_cvlb[0], norm2_g=d_n2[0])
        comm = None
        if last:
            early_g = {k: jnp.stack([small[ll][k] for ll in range(LAYERS)]) for k in small[0]}
            early_g["final_g"] = d_final[0]
            early_buf = _pack([early_g[k] for k in _SMALL_EARLY] + [loss], EARLY_ROWS)
            comm = _Comms([_Scatter([h_out, h_f2]), _Gather([(early_buf, "lead")])])
        g_f, g_b = _scan_pair(sv["proj"], dO, 1, rc["f"]["XI"], rc["b"]["XI"], gcf, gcb, False, f"b_ret_state_{l}")
        dqkv, got = _b_ret_out(sv["proj"], cos2, sin2, dO, sv["s_f"], sv["s_b"], g_f, g_b, rc, f"b_ret_out_{l}", comm)
        if last:
            parts["w_out"][l], parts["w_ffn_out"][l], early_parts = got
        (dproj, top, d_n1), _ = _b_inproj(d_uv, dqkv, d_g, d_ag, full["w_in"][l], sv["x"], n1, dxm, l, f"b_inproj_{l}")
        norm1[l] = d_n1[0]
        g_in = _mm_wgrad(sv["h1"], dproj, NDEV, False, "cols", 2, f"g_in_{l}")
        q_in, = _comm_only(_PairSwap([g_in]), f"pair_swap_w_in_{l}")
        h_in = _pair_add(g_in, q_in, f"pair_add_w_in_{l}")
        if last:
            tail = [h_in]
        else:
            upper = [h_in, h_out, h_f1, h_f2]
    late_buf = _pack([jnp.stack(norm1)], LATE_ROWS)
    parts["w_in"][0], late_parts = _comm_only(_Comms([_Scatter(tail), _Gather([(late_buf, "lead")])]), "exchange_last")
    return loss, top, parts, (early_parts, late_parts)


def _adamw(w, g, m, v):
    m = ADAM_B1 * m + (1.0 - ADAM_B1) * g
    v = ADAM_B2 * v + (1.0 - ADAM_B2) * (g * g)
    m_hat = m / (1.0 - ADAM_B1 ** ADAM_STEP)
    v_hat = v / (1.0 - ADAM_B2 ** ADAM_STEP)
    return -ADAM_LR * (m_hat / (jnp.sqrt(v_hat) + ADAM_EPS) + ADAM_WD * w), m, v


def _cast_blocks(ws):
    def body(*refs):
        ins, outs = refs[:len(ws)], refs[len(ws):]
        for k, src in enumerate(ins):
            for l in range(LAYERS):
                outs[k * LAYERS + l][...] = src[l].astype(BF16)

    outs = pl.pallas_call(body, name="cast_blocks", out_shape=[S(w.shape[1:], BF16) for w in ws for _ in range(LAYERS)],
                          compiler_params=_params())(*ws)
    return [list(outs[k * LAYERS:(k + 1) * LAYERS]) for k in range(len(ws))]


def _row_block(mm):
    return next(b for b in (256, 352, 128) if mm % b == 0)


def _sum_adam(parts, w, m, v, l, prev, name):
    _, mm, nn = parts.shape
    bm = _row_block(mm)

    def body(p_ref, w_ref, m_ref, v_ref, *rest):
        g_ref, d_ref, nm_ref, nv_ref = rest[-4:]
        g = p_ref[0].astype(F32)
        for s in range(1, 4):
            g = g + p_ref[s].astype(F32)
        g_ref[...] = g
        d_ref[...], nm_ref[...], nv_ref[...] = _adamw(w_ref[...], g, m_ref[...], v_ref[...])

    blk = lambda: pl.BlockSpec((None, bm, nn), lambda i: (l, i, 0))
    prev = list(prev) if prev else []
    return pl.pallas_call(
        body, grid=(mm // bm,), name=name,
        in_specs=[pl.BlockSpec((4, bm, nn), lambda i: (0, i, 0)), blk(), blk(), blk()] + [_ANY] * len(prev),
        out_specs=[blk() for _ in range(4)],
        out_shape=[S(w.shape, F32) for _ in range(4)],
        input_output_aliases={4 + j: j for j in range(len(prev))},
        compiler_params=_params("parallel"),
    )(parts, w, m, v, *prev)


def _sum_small(parts):
    n = len(parts)

    def body(*refs):
        for p_ref, o_ref in zip(refs[:n], refs[n:]):
            g = p_ref[0]
            for s in range(1, NDEV):
                g = g + p_ref[s]
            o_ref[...] = g

    return pl.pallas_call(body, name="sum_small", out_shape=[S(p.shape[1:], F32) for p in parts],
                          compiler_params=_params())(*parts)


def _adam_small(g, w, m, v):
    def body(g_ref, w_ref, m_ref, v_ref, d_ref, nm_ref, nv_ref):
        d_ref[...], nm_ref[...], nv_ref[...] = _adamw(w_ref[...], g_ref[...], m_ref[...], v_ref[...])

    return pl.pallas_call(body, name="adam_small", out_shape=[S(g.shape, F32)] * 3, compiler_params=_params())(g, w, m, v)


_SMALL = ("norm1_g", "gm_ln_g", "gm_ln_b", "gm_ws", "gm_bs", "conv_w", "conv_b", "conv_ln_g", "conv_ln_b",
          "norm2_g", "final_g")
_SMALL_EARLY = _SMALL[1:]
_NAMES = ("norm1_g", "w_in", "gm_ln_g", "gm_ln_b", "gm_ws", "gm_bs", "conv_w", "conv_b", "conv_ln_g", "conv_ln_b",
          "w_out", "norm2_g", "w_ffn_in", "w_ffn_out", "final_g")


def _pack(parts, rows):
    flat = jnp.concatenate([p.reshape(-1) for p in parts])
    return jnp.pad(flat, (0, rows * 1024 - flat.shape[0])).reshape(rows, 1024)


def _unpack(buf, shapes):
    flat = buf.reshape(-1)
    out, o = [], 0
    for shp in shapes:
        sz = int(np.prod(shp))
        out.append(flat[o:o + sz].reshape(shp))
        o += sz
    return out


def kernel(x, norm1_g, w_in, gm_ln_g, gm_ln_b, gm_ws, gm_bs, conv_w, conv_b, conv_ln_g, conv_ln_b, w_out, norm2_g, w_ffn_in, w_ffn_out, final_g, loss_target, m_norm1_g, m_w_in, m_gm_ln_g, m_gm_ln_b, m_gm_ws, m_gm_bs, m_conv_w, m_conv_b, m_conv_ln_g, m_conv_ln_b, m_w_out, m_norm2_g, m_w_ffn_in, m_w_ffn_out, m_final_g, v_norm1_g, v_w_in, v_gm_ln_g, v_gm_ln_b, v_gm_ws, v_gm_bs, v_conv_w, v_conv_b, v_conv_ln_g, v_conv_ln_b, v_w_out, v_norm2_g, v_w_ffn_in, v_w_ffn_out, v_final_g):
    w = dict(norm1_g=norm1_g, w_in=w_in, gm_ln_g=gm_ln_g, gm_ln_b=gm_ln_b, gm_ws=gm_ws, gm_bs=gm_bs, conv_w=conv_w,
             conv_b=conv_b, conv_ln_g=conv_ln_g, conv_ln_b=conv_ln_b, w_out=w_out, norm2_g=norm2_g, w_ffn_in=w_ffn_in,
             w_ffn_out=w_ffn_out, final_g=final_g)
    mo = dict(norm1_g=m_norm1_g, w_in=m_w_in, gm_ln_g=m_gm_ln_g, gm_ln_b=m_gm_ln_b, gm_ws=m_gm_ws, gm_bs=m_gm_bs,
              conv_w=m_conv_w, conv_b=m_conv_b, conv_ln_g=m_conv_ln_g, conv_ln_b=m_conv_ln_b, w_out=m_w_out,
              norm2_g=m_norm2_g, w_ffn_in=m_w_ffn_in, w_ffn_out=m_w_ffn_out, final_g=m_final_g)
    vo = dict(norm1_g=v_norm1_g, w_in=v_w_in, gm_ln_g=v_gm_ln_g, gm_ln_b=v_gm_ln_b, gm_ws=v_gm_ws, gm_bs=v_gm_bs,
              conv_w=v_conv_w, conv_b=v_conv_b, conv_ln_g=v_conv_ln_g, conv_ln_b=v_conv_ln_b, w_out=v_w_out,
              norm2_g=v_norm2_g, w_ffn_in=v_w_ffn_in, w_ffn_out=v_w_ffn_out, final_g=v_final_g)
    t = x.shape[1]
    me = 4 * lax.axis_index("x") + 2 * lax.axis_index("y") + lax.axis_index("c")
    cshard = conv_w.shape[2]

    cw_pad = jnp.pad(conv_w, ((0, 0), (0, 32 - KW), (0, CWP - cshard)))
    flip = lambda a: jnp.swapaxes(a, 1, 2)
    big = {f: tuple(flip(a[f]) if f == "w_ffn_in" else a[f] for a in (w, mo, vo)) for f in _BIG}
    sh = dict(zip(_BIG, _cast_blocks([big[f][0] for f in _BIG])))
    loss, dx, parts, small_parts = _step(x.reshape(t, D), loss_target.reshape(t, D), w, sh, cw_pad)

    grads, delta, new_m, new_v = {}, {}, {}, {}
    for f in _BIG:
        outs = None
        for l in reversed(range(LAYERS)):
            outs = _sum_adam(parts[f][l], *big[f], l, outs, f"sum_adam_{f}_{l}")
        grads[f], delta[f], new_m[f], new_v[f] = [flip(a) for a in outs] if f == "w_ffn_in" else outs

    early_sum, late_sum = _sum_small(small_parts)
    early_shapes = [(LAYERS, KW, CVW) if k == "conv_w" else w[k].shape for k in _SMALL_EARLY]
    grads["norm1_g"], = _unpack(late_sum, [w["norm1_g"].shape])
    *early, total = _unpack(early_sum, early_shapes + [()])
    for k, g in zip(_SMALL_EARLY, early):
        grads[k] = lax.dynamic_slice_in_dim(g, me * cshard, cshard, axis=2) if k == "conv_w" else g
    adam_rows = 144
    d_s, m_s, v_s = _adam_small(_pack([grads[k] for k in _SMALL], adam_rows), _pack([w[k] for k in _SMALL], adam_rows),
                                _pack([mo[k] for k in _SMALL], adam_rows), _pack([vo[k] for k in _SMALL], adam_rows))
    shapes = [w[k].shape for k in _SMALL]
    for dst, buf in ((delta, d_s), (new_m, m_s), (new_v, v_s)):
        for k, a in zip(_SMALL, _unpack(buf, shapes)):
            dst[k] = a

    return (total, dx.reshape(1, t, D), *[grads[k] for k in _NAMES], *[delta[k] for k in _NAMES],
            *[new_m[k] for k in _NAMES], *[new_v[k] for k in _NAMES])
```

```python
import functools

import numpy as np
import jax
import jax.numpy as jnp
from jax import lax
from jax.experimental import pallas as pl
from jax.experimental.pallas import tpu as pltpu

F32, BF16 = jnp.float32, jnp.bfloat16
S = jax.ShapeDtypeStruct

D = 1024
INW = 3072
GMW = 256
RETW = 512
CVW = 256
HEADS = 4
DH = 128
C = 128
KW = 31
HALO = 16
FFH = 2816
NDEV = 8
FFB = 2 * FFH // NDEV
FF_CHUNKS = ((0, 768), (768, 1536), (1536, 2304), (2304, FFH))
EPS = 1e-6
LAYERS = 2
SCALE = DH ** -0.5
VMEM_LIMIT = 56 * 1024 * 1024

ADAM_LR, ADAM_B1, ADAM_B2, ADAM_EPS, ADAM_WD, ADAM_STEP = 0.001, 0.9, 0.999, 1e-08, 0.01, 10

_SQRT_HALF = 0.7071067811865476
_INV_SQRT_2PI = 0.3989422804014327


def _params(*sem):
    return pltpu.CompilerParams(dimension_semantics=sem or None, vmem_limit_bytes=VMEM_LIMIT)


def _resident(shape, index_map):
    return pl.BlockSpec(shape, index_map, pipeline_mode=pl.Buffered(1))


def _dot(a, b):
    return jnp.dot(a, b, preferred_element_type=F32)


def _dot_nt(a, b):
    return lax.dot_general(a, b, (((1,), (1,)), ((), ())), preferred_element_type=F32)


def _dot_tn(a, b):
    return lax.dot_general(a, b, (((0,), (0,)), ((), ())), preferred_element_type=F32)


def _sigmoid(x):
    return 1.0 / (1.0 + jnp.exp(-x))


def _gelu_and_grad(x):
    cdf = 0.5 * (1.0 + lax.erf(x * _SQRT_HALF))
    return x * cdf, cdf + x * jnp.exp(-0.5 * x * x) * _INV_SQRT_2PI


def _silu_and_grad(x):
    s = _sigmoid(x)
    return x * s, s * (1.0 + x * (1.0 - s))


def _standardize(x):
    mu = jnp.mean(x, axis=-1, keepdims=True)
    d = x - mu
    rstd = lax.rsqrt(jnp.mean(d * d, axis=-1, keepdims=True) + EPS)
    return d * rstd, rstd


def _standardize_bwd(dxhat, xhat, rstd):
    m1 = jnp.mean(dxhat, axis=-1, keepdims=True)
    m2 = jnp.mean(dxhat * xhat, axis=-1, keepdims=True)
    return rstd * (dxhat - m1 - xhat * m2)


def _rms(x):
    return lax.rsqrt(jnp.mean(x * x, axis=-1, keepdims=True) + EPS)


def _rmsnorm_bwd(dy, x, r, g):
    u = dy * g
    return r * u - x * (r * r * r) * jnp.mean(u * x, axis=-1, keepdims=True)


def _col_sum(a):
    return jnp.sum(a, axis=0, keepdims=True)


def _rot(t, cos2, sin2):
    return t * cos2 + pltpu.roll(t, DH // 2, axis=1) * sin2


def _rot_t(dt, cos2, sin2):
    return dt * cos2 + pltpu.roll(dt * sin2, DH // 2, axis=1)


MESH = pl.DeviceIdType.MESH
_HBM = pl.BlockSpec(memory_space=pltpu.HBM)
_ANY = pl.BlockSpec(memory_space=pl.ANY)


def _place():
    x, y, c = lax.axis_index("x"), lax.axis_index("y"), lax.axis_index("c")
    return x, y, c, ((1 - x, y), (x, 1 - y), (1 - x, 1 - y))


def _slot(full, kind, width, i):
    if kind == "cols":
        return full.at[:, pl.ds(pl.multiple_of(i * width, 128), width)]
    if kind == "rows":
        return full.at[pl.ds(pl.multiple_of(i * width, 16), width), :]
    return full.at[i]


class _Gather:
    def __init__(self, units):
        self.units = units
        self.inputs = [u[0] for u in units]
        self.out_shape = []
        for src, kind in units:
            r, c = src.shape[-2:]
            shape = {"cols": (r, NDEV * c), "rows": (NDEV * r, c), "lead": (NDEV,) + src.shape}[kind]
            self.out_shape.append(S(shape, src.dtype))
        n = len(units)
        self.scratch = [pltpu.SemaphoreType.DMA((n, 7)), pltpu.SemaphoreType.DMA((n, 7)), pltpu.SemaphoreType.DMA((n,))]

    def run(self, phase, ins, outs, scr):
        ssem, rsem, lsem = scr
        x, y, c, chips = _place()
        me, sib = 4 * x + 2 * y + c, (x, y, 1 - c)
        idx = lambda chip, core: 4 * chip[0] + 2 * chip[1] + core
        for u, (src_arr, kind) in enumerate(self.units):
            src, full = ins[u], outs[u]
            width = src_arr.shape[-1] if kind == "cols" else src_arr.shape[-2]
            slot = functools.partial(_slot, full, kind, width)

            def copy(k, block, to, from_src=False):
                return pltpu.make_async_remote_copy(src_ref=src if from_src else slot(block), dst_ref=slot(block),
                                                    send_sem=ssem.at[u, k], recv_sem=rsem.at[u, k],
                                                    device_id=to, device_id_type=MESH)

            mine = lambda: pltpu.make_async_copy(src, slot(me), lsem.at[u])
            first = lambda: [copy(0, me, sib, True)] + [copy(1 + j, me, (*chip, c), True) for j, chip in enumerate(chips)]
            passed = lambda j: copy(4 + j, idx(chips[j], c), sib)
            if phase == "start":
                mine().start()
                for cp in first():
                    cp.start()
            elif phase == "forward":
                for j, chip in enumerate(chips):
                    copy(1 + j, idx(chip, c), sib).wait_recv()
                    passed(j).start()
            else:
                copy(0, idx((x, y), 1 - c), sib).wait_recv()
                for j, chip in enumerate(chips):
                    copy(4 + j, idx(chip, 1 - c), sib).wait_recv()
                for cp in first() + [passed(j) for j in range(3)]:
                    cp.wait_send()
                mine().wait()


class _Scatter:
    def __init__(self, units):
        self.units = units
        self.inputs = list(units)
        self.out_shape = [S(u.shape, u.dtype) for u in units]
        n = len(units)
        self.scratch = [pltpu.SemaphoreType.DMA((n, 3)), pltpu.SemaphoreType.DMA((n, 3)), pltpu.SemaphoreType.DMA((n,))]

    def run(self, phase, ins, outs, scr):
        ssem, rsem, lsem = scr
        x, y, c, chips = _place()
        myq = 2 * x + y
        for u in range(len(self.units)):
            h, p = ins[u], outs[u]

            def copy(k, chip, send_to_them):
                q = 2 * chip[0] + chip[1]
                return pltpu.make_async_remote_copy(src_ref=h.at[q], dst_ref=p.at[myq if send_to_them else q],
                                                    send_sem=ssem.at[u, k], recv_sem=rsem.at[u, k],
                                                    device_id=(*chip, c), device_id_type=MESH)

            mine = lambda: pltpu.make_async_copy(h.at[myq], p.at[myq], lsem.at[u])
            sends = lambda: [copy(k, chip, True) for k, chip in enumerate(chips)]
            if phase == "start":
                mine().start()
                for cp in sends():
                    cp.start()
            elif phase == "finish":
                for k, chip in enumerate(chips):
                    copy(k, chip, False).wait_recv()
                for cp in sends():
                    cp.wait_send()
                mine().wait()


class _Comms:
    def __init__(self, parts):
        self.parts = parts
        self.inputs = [a for p in parts for a in p.inputs]
        self.out_shape = [a for p in parts for a in p.out_shape]
        self.scratch = [a for p in parts for a in p.scratch]

    def run(self, phase, ins, outs, scr):
        i = o = s = 0
        for p in self.parts:
            ni, no, ns = len(p.inputs), len(p.out_shape), len(p.scratch)
            p.run(phase, ins[i:i + ni], outs[o:o + no], scr[s:s + ns])
            i, o, s = i + ni, o + no, s + ns


def _launch(body, grid, in_specs, out_specs, out_shape, scratch, args, name, sem, comm=None):
    if comm is None:
        outs = pl.pallas_call(body, grid=grid, name=name, in_specs=in_specs, out_specs=out_specs, out_shape=out_shape,
                              scratch_shapes=scratch, compiler_params=_params(*sem))(*args)
        return list(outs), []
    n_in, n_out, n_scr = len(args), len(out_shape), len(scratch)
    ci, co = len(comm.inputs), len(comm.out_shape)
    nsteps = int(np.prod(grid))
    fwd_step = (7 * nsteps) // 8

    def hosted(*refs):
        a = refs[:n_in]
        ca = refs[n_in:n_in + ci]
        o = refs[n_in + ci:n_in + ci + n_out]
        cout = refs[n_in + ci + n_out:n_in + ci + n_out + co]
        s = refs[n_in + ci + n_out + co:n_in + ci + n_out + co + n_scr]
        cs = refs[n_in + ci + n_out + co + n_scr:]
        step = pl.program_id(0)
        for d in range(1, len(grid)):
            step = step * grid[d] + pl.program_id(d)

        @pl.when(step == 0)
        def _():
            comm.run("start", ca, cout, cs)

        body(*a, *o, *s)

        @pl.when(step == fwd_step)
        def _():
            comm.run("forward", ca, cout, cs)

        @pl.when(step == nsteps - 1)
        def _():
            comm.run("finish", ca, cout, cs)

    outs = pl.pallas_call(
        hosted, grid=grid, name=name, in_specs=list(in_specs) + [_HBM] * ci, out_specs=list(out_specs) + [_HBM] * co,
        out_shape=list(out_shape) + comm.out_shape, scratch_shapes=list(scratch) + comm.scratch,
        compiler_params=_params(*["arbitrary"] * len(grid)))(*args, *comm.inputs)
    return list(outs[:n_out]), list(outs[n_out:])


def _comm_only(comm, name):
    ci, co = len(comm.inputs), len(comm.out_shape)

    def body(*refs):
        ca, cout, cs = refs[:ci], refs[ci:ci + co], refs[ci + co:]
        for phase in ("start", "forward", "finish"):
            comm.run(phase, ca, cout, cs)

    return pl.pallas_call(body, name=name, in_specs=[_HBM] * ci, out_specs=[_HBM] * co, out_shape=comm.out_shape,
                          scratch_shapes=comm.scratch, compiler_params=_params())(*comm.inputs)


def _ret_consts():
    idx = np.arange(C, dtype=np.float32)
    gf = (1.0 - np.exp2(-5.0 - np.arange(HEADS, dtype=np.float32))).astype(np.float32)
    out = {}
    for name, gamma, fwd in (("f", gf, True), ("b", gf[::-1].copy(), False)):
        lg = np.log(gamma).astype(np.float32)[:, None]
        diff = idx[:, None] - idx[None, :]
        if fwd:
            mask = diff >= 0
            dist = np.where(mask, diff, 0.0)
            zeta = np.exp(lg * (C - 1 - idx))
            xi = np.exp(lg * (idx + 1))
        else:
            mask = diff < 0
            dist = np.where(mask, -diff, 0.0)
            zeta = np.exp(lg * idx)
            xi = np.exp(lg * (C - idx))
        dm = np.where(mask[None], np.exp(lg[:, :, None] * dist[None]), 0.0).astype(np.float32)
        bc = lambda vec: np.ascontiguousarray(np.broadcast_to(vec.astype(np.float32)[:, :, None], (HEADS, C, DH)))
        out[name] = dict(D=dm, XI=bc(xi), ZETA=bc(zeta), gC=[float(v) for v in np.exp(lg[:, 0] * C).astype(np.float32)])
    return out


def _rope_tables(t):
    half = DH // 2
    inv_freq = (np.float32(10000.0) ** (-np.arange(half, dtype=np.float32) / np.float32(half))).astype(np.float32)
    ang = (np.arange(t, dtype=np.float32)[:, None] * inv_freq[None, :]).astype(np.float64)
    cos, sin = np.cos(ang).astype(np.float32), np.sin(ang).astype(np.float32)
    return np.concatenate([cos, cos], axis=1), np.concatenate([-sin, sin], axis=1)


def _f_inproj(x, g1, w, cos2, sin2, l, name, comm=None):
    t = x.shape[0]
    tm = 512

    def body(x_ref, g_ref, w_ref, cos_ref, sin_ref, proj_ref, ht_ref):
        xv = x_ref[...]
        h = (xv * _rms(xv) * g_ref[...]).astype(BF16)
        ht_ref[...] = h.T
        for nb in range(INW // 512):
            cs = slice(nb * 512, (nb + 1) * 512)
            res = _dot(h, w_ref[:, cs])
            if nb in (1, 2):
                for hh in range(HEADS):
                    r = _rot(res[:, hh * DH:(hh + 1) * DH], cos_ref[...], sin_ref[...])
                    proj_ref[:, nb * 512 + hh * DH:nb * 512 + (hh + 1) * DH] = (r * SCALE if nb == 2 else r).astype(BF16)
            else:
                proj_ref[:, cs] = res.astype(BF16)

    return _launch(
        body, (t // tm,),
        [pl.BlockSpec((tm, D), lambda i: (i, 0)),
         pl.BlockSpec((None, 1, D), lambda i: (l, 0, 0)),
         _resident((D, INW), lambda i: (0, 0)),
         pl.BlockSpec((tm, DH), lambda i: (i, 0)), pl.BlockSpec((tm, DH), lambda i: (i, 0))],
        [pl.BlockSpec((tm, INW), lambda i: (i, 0)), pl.BlockSpec((D, tm), lambda i: (0, i))],
        [S((t, INW), BF16), S((D, t), BF16)], [], (x, g1, w, cos2, sin2), name, ("parallel",), comm)


def _gm_chunk_fwd(u, v, lng, lnb, ws_ref, bias):
    au, dau = _gelu_and_grad(u)
    av, dav = _gelu_and_grad(v)
    vhat, rstd = _standardize(av)
    vn = (vhat * lng + lnb).astype(BF16)
    head = lax.broadcasted_iota(jnp.int32, (C, GMW), 1) // (GMW // HEADS)
    mixed = bias
    for h in range(HEADS):
        mixed = mixed + jnp.where(head == h, _dot(ws_ref[h], vn), 0.0)
    return au, dau, dav, vhat, rstd, vn, mixed, head


def _f_gm(proj, lng, lnb, ws_bf, bias, l, name):
    t = proj.shape[0]
    tm = 512

    def body(p_ref, lng_ref, lnb_ref, ws_ref, bias_ref, y_ref):
        for ci in range(tm // C):
            rows = slice(ci * C, (ci + 1) * C)
            u = p_ref[rows, 0:GMW].astype(F32)
            v = p_ref[rows, GMW:2 * GMW].astype(F32)
            au, _, _, _, _, _, mixed, _ = _gm_chunk_fwd(u, v, lng_ref[...], lnb_ref[...], ws_ref, bias_ref[...])
            y_ref[rows, :] = (au * mixed).astype(BF16)

    return pl.pallas_call(
        body, grid=(t // tm,), name=name,
        in_specs=[pl.BlockSpec((tm, 2 * GMW), lambda i: (i, 0)),
                  pl.BlockSpec((None, 1, GMW), lambda i: (l, 0, 0)),
                  pl.BlockSpec((None, 1, GMW), lambda i: (l, 0, 0)),
                  pl.BlockSpec((None, HEADS, C, C), lambda i: (l, 0, 0, 0)),
                  pl.BlockSpec((None, C, GMW), lambda i: (l, 0, 0))],
        out_specs=pl.BlockSpec((tm, GMW), lambda i: (i, 0)),
        out_shape=S((t, GMW), BF16),
        compiler_params=_params("parallel"),
    )(proj, lng, lnb, ws_bf, bias)


def _scan_pair(proj, other, col, wf, wb, gcf, gcb, first_is_f, name):
    t = proj.shape[0]
    n = t // C
    sc = min(SCAN_CHUNKS, n)
    nsteps = n // sc
    other_is_proj = other is None

    def body(a1, o1, a2, o2, w1_ref, w2_ref, out1, out2, st1, st2):
        @pl.when(pl.program_id(0) == 0)
        def _():
            st1[...] = jnp.zeros_like(st1)
            st2[...] = jnp.zeros_like(st2)

        def one(a_ref, o_ref, w_ref, gc, st, out, order):
            for h in range(HEADS):
                sl = slice(h * DH, (h + 1) * DH)
                incs = {}
                for j in order:
                    rows = slice(j * C, (j + 1) * C)
                    aw = (a_ref[rows, sl].astype(F32) * w_ref[h]).astype(BF16)
                    incs[j] = _dot_tn(aw, o_ref[rows, sl].astype(BF16))
                cur = st[h]
                for j in order:
                    out[j, h] = cur.astype(BF16)
                    cur = gc[h] * cur + incs[j]
                st[h] = cur

        g1, g2 = (gcf, gcb) if first_is_f else (gcb, gcf)
        one(a1, o1, w1_ref, g1, st1, out1, range(sc))
        one(a2, o2, w2_ref, g2, st2, out2, range(sc - 1, -1, -1))

    up = lambda i: i
    down = lambda i: nsteps - 1 - i

    def specs(ix):
        o_spec = pl.BlockSpec((sc * C, RETW), lambda i: (ix(i), 3 if other_is_proj else 0))
        return [pl.BlockSpec((sc * C, RETW), lambda i: (ix(i), col)), o_spec]

    const = lambda: pl.BlockSpec((HEADS, C, DH), lambda i: (0, 0, 0))
    oth = proj if other_is_proj else other
    w1, w2 = (wf, wb) if first_is_f else (wb, wf)
    out1, out2 = pl.pallas_call(
        body, grid=(nsteps,), name=name,
        in_specs=specs(up) + specs(down) + [const(), const()],
        out_specs=[pl.BlockSpec((sc, HEADS, DH, DH), lambda i: (up(i), 0, 0, 0)),
                   pl.BlockSpec((sc, HEADS, DH, DH), lambda i: (down(i), 0, 0, 0))],
        out_shape=[S((n, HEADS, DH, DH), BF16), S((n, HEADS, DH, DH), BF16)],
        scratch_shapes=[pltpu.VMEM((HEADS, DH, DH), F32), pltpu.VMEM((HEADS, DH, DH), F32)],
        compiler_params=_params("arbitrary"),
    )(proj, oth, proj, oth, w1, w2)
    return (out1, out2) if first_is_f else (out2, out1)


SCAN_CHUNKS = 8


RET_CHUNKS = 8


def _f_ret_out(proj, s_f, s_b, rc, name):
    t = proj.shape[0]
    nchunks = min(RET_CHUNKS, t // C)
    tm = nchunks * C

    def body(q_ref, k_ref, v_ref, sf_ref, sb_ref, d_ref, xif_ref, xib_ref, o_ref):
        for ci in range(nchunks):
            rows = slice(ci * C, (ci + 1) * C)
            for h in range(HEADS):
                sl = slice(h * DH, (h + 1) * DH)
                qh, kh, vh = q_ref[rows, sl], k_ref[rows, sl], v_ref[rows, sl]
                p = (_dot_nt(qh, kh) * d_ref[h]).astype(BF16)
                cross = _dot(qh, jnp.concatenate([sf_ref[ci, h], sb_ref[ci, h]], axis=1))
                o_ref[rows, sl] = _dot(p, vh) + xif_ref[h] * cross[:, 0:DH] + xib_ref[h] * cross[:, DH:2 * DH]

    const = lambda: pl.BlockSpec((HEADS, C, DH), lambda i: (0, 0, 0))
    state = lambda: pl.BlockSpec((nchunks, HEADS, DH, DH), lambda i: (i, 0, 0, 0))
    return pl.pallas_call(
        body, grid=(t // tm,), name=name,
        in_specs=[pl.BlockSpec((tm, RETW), lambda i, cb=cb: (i, cb)) for cb in (1, 2, 3)] +
                 [state(), state(), const(), const(), const()],
        out_specs=pl.BlockSpec((tm, RETW), lambda i: (i, 0)),
        out_shape=S((t, RETW), F32),
        compiler_params=_params("parallel"),
    )(proj, proj, proj, s_f, s_b, rc["f"]["D"] + rc["b"]["D"], rc["f"]["XI"], rc["b"]["XI"])


def _conv_halo_specs(t, tm, width, col):
    r = tm // HALO
    last = t // HALO - 1
    return [pl.BlockSpec((HALO, width), lambda i: (jnp.maximum(i * r - 1, 0), col)),
            pl.BlockSpec((tm, width), lambda i: (i, col)),
            pl.BlockSpec((HALO, width), lambda i: (jnp.minimum((i + 1) * r, last), col))]


def _fill_ext(ext, prev, cur, nxt, i, nt, tm):
    ext[0:HALO, :] = jnp.where(i > 0, prev, 0.0)
    ext[HALO:HALO + tm, :] = cur
    ext[HALO + tm:2 * HALO + tm, :] = jnp.where(i < nt - 1, nxt, 0.0)


def _glu(a_ref, g_ref):
    return a_ref[...].astype(F32) * _sigmoid(g_ref[...].astype(F32))


def _shifted_copies(ext, rot, tm):
    rows = tm + 2 * HALO - 8
    for b in range(1, 8):
        rot[b - 1, :, :] = ext[pl.ds(b, rows), :]


def _tap(ext, rot, r0, s, rb):
    a, b = divmod(s, 8)
    return ext[pl.ds(r0 + 8 * a, rb), :] if b == 0 else rot[b - 1, pl.ds(r0 + 8 * a, rb), :]


def _f_conv(proj, cw, cb, lng, lnb, l, name, comm=None):
    t = proj.shape[0]
    tm = 512
    nt = t // tm
    rb = 64

    def body(ap, ac, an, gp, gc, gn, cw_ref, cb_ref, lng_ref, lnb_ref, c_ref, y_ref, hext, hrot):
        i = pl.program_id(0)
        _fill_ext(hext, _glu(ap, gp), _glu(ac, gc), _glu(an, gn), i, nt, tm)
        _shifted_copies(hext, hrot, tm)
        for r0 in range(0, tm, rb):
            acc = jnp.zeros((rb, CVW), F32) + cb_ref[...]
            for j in range(KW):
                acc = acc + cw_ref[j:j + 1, :] * _tap(hext, hrot, r0, j + 1, rb)
            c_ref[r0:r0 + rb, :] = acc
            chat, _ = _standardize(acc)
            z = chat * lng_ref[...] + lnb_ref[...]
            y_ref[r0:r0 + rb, :] = (z * _sigmoid(z)).astype(BF16)

    vec = lambda: pl.BlockSpec((None, 1, CVW), lambda i: (l, 0, 0))
    return _launch(
        body, (nt,),
        _conv_halo_specs(t, tm, CVW, 10) + _conv_halo_specs(t, tm, CVW, 11) +
        [pl.BlockSpec((None, 32, CVW), lambda i: (l, 0, 0)), vec(), vec(), vec()],
        [pl.BlockSpec((tm, CVW), lambda i: (i, 0)), pl.BlockSpec((tm, CVW), lambda i: (i, 0))],
        [S((t, CVW), F32), S((t, CVW), BF16)],
        [pltpu.VMEM((tm + 2 * HALO, CVW), F32), pltpu.VMEM((7, tm + 2 * HALO - 8, CVW), F32)],
        (proj, proj, proj, proj, proj, proj, cw, cb, lng, lnb), name, ("parallel",), comm)


def _f_mixout(x, y_gm, y_cv, o, proj, w, name):
    t = x.shape[0]
    tm = 512

    def body(x_ref, ygm_ref, ycv_ref, o_ref, g_ref, w_ref, xm_ref, ycat_t_ref, ycat):
        ycat[:, 0:GMW] = ygm_ref[...]
        ycat[:, GMW + RETW:D] = ycv_ref[...]
        for h in range(HEADS):
            sl = slice(h * DH, (h + 1) * DH)
            ohat, _ = _standardize(o_ref[:, sl])
            g = g_ref[:, sl].astype(F32)
            ycat[:, GMW + h * DH:GMW + (h + 1) * DH] = (ohat * (g * _sigmoid(g))).astype(BF16)
        yc = ycat[...]
        ycat_t_ref[...] = yc.T
        xm_ref[...] = x_ref[...] + _dot(yc, w_ref[...])

    return pl.pallas_call(
        body, grid=(t // tm,), name=name,
        in_specs=[pl.BlockSpec((tm, D), lambda i: (i, 0)),
                  pl.BlockSpec((tm, GMW), lambda i: (i, 0)),
                  pl.BlockSpec((tm, CVW), lambda i: (i, 0)),
                  pl.BlockSpec((tm, RETW), lambda i: (i, 0)),
                  pl.BlockSpec((tm, RETW), lambda i: (i, 4)),
                  _resident((D, D), lambda i: (0, 0))],
        out_specs=[pl.BlockSpec((tm, D), lambda i: (i, 0)), pl.BlockSpec((D, tm), lambda i: (0, i))],
        out_shape=[S((t, D), F32), S((D, t), BF16)],
        scratch_shapes=[pltpu.VMEM((tm, D), BF16)],
        compiler_params=_params("parallel"),
    )(x, y_gm, y_cv, o, proj, w)


def _f_ffn(xm, g2, w1, w2, l, name, comm=None):
    t = xm.shape[0]
    tm = 512

    def body(x_ref, g_ref, w1_ref, w2_ref, xo_ref, h_ref, gu_ref, act_t_ref):
        xv = x_ref[...]
        h = (xv * _rms(xv) * g_ref[...]).astype(BF16)
        h_ref[...] = h
        acc = xv
        for a, b in FF_CHUNKS:
            gate = _dot_nt(h, w1_ref[a:b, :])
            up = _dot_nt(h, w1_ref[FFH + a:FFH + b, :])
            gu_ref[:, a:b] = gate.astype(BF16)
            gu_ref[:, FFH + a:FFH + b] = up.astype(BF16)
            av = ((gate * _sigmoid(gate)) * up).astype(BF16)
            act_t_ref[a:b, :] = av.T
            acc = acc + _dot(av, w2_ref[a:b, :])
        xo_ref[...] = acc

    return _launch(
        body, (t // tm,),
        [pl.BlockSpec((tm, D), lambda i: (i, 0)),
         pl.BlockSpec((None, 1, D), lambda i: (l, 0, 0)),
         _resident((2 * FFH, D), lambda i: (0, 0)),
         _resident((FFH, D), lambda i: (0, 0))],
        [pl.BlockSpec((tm, D), lambda i: (i, 0)), pl.BlockSpec((tm, D), lambda i: (i, 0)),
         pl.BlockSpec((tm, 2 * FFH), lambda i: (i, 0)), pl.BlockSpec((FFH, tm), lambda i: (0, i))],
        [S((t, D), F32), S((t, D), BF16), S((t, 2 * FFH), BF16), S((FFH, t), BF16)],
        [], (xm, g2, w1, w2), name, ("parallel",), comm)


def _b_ffn(top, xm, g2, gu, w1, w2, l, name, comm=None):
    t = xm.shape[0]
    tm = 256
    from_loss = isinstance(top, tuple)
    n_top = 3 if from_loss else 1

    def body(*refs):
        top_refs = refs[:n_top]
        x_ref, g_ref, gu_ref, w1_ref, w2_ref, dgu_ref, dxm_ref, dxb_ref, dg_ref = refs[n_top:n_top + 9]
        first = pl.program_id(0) == 0

        @pl.when(first)
        def _():
            dg_ref[...] = jnp.zeros_like(dg_ref)

        if from_loss:
            xo_ref, fg_ref, t_ref = top_refs
            loss_ref, dfg_ref = refs[n_top + 9:n_top + 11]

            @pl.when(first)
            def _():
                loss_ref[...] = jnp.zeros_like(loss_ref)
                dfg_ref[...] = jnp.zeros_like(dfg_ref)

            xo = xo_ref[...]
            ro = _rms(xo)
            xr = xo * ro
            err = xr * fg_ref[...] - t_ref[...]
            loss_ref[...] += (0.5 / D) * _col_sum(jnp.sum(err * err, axis=1, keepdims=True))
            dy = err * (1.0 / D)
            dfg_ref[...] += _col_sum(dy * xr)
            dxo = _rmsnorm_bwd(dy, xo, ro, fg_ref[...])
        else:
            dxo = top_refs[0][...]
        dxb = dxo.astype(BF16)
        dxb_ref[...] = dxb
        dh = jnp.zeros((tm, D), F32)
        for a, b in FF_CHUNKS:
            dact = _dot_nt(dxb, w2_ref[a:b, :])
            gate = gu_ref[:, a:b].astype(F32)
            up = gu_ref[:, FFH + a:FFH + b].astype(F32)
            sg, dsg = _silu_and_grad(gate)
            dgate = (dact * up * dsg).astype(BF16)
            dup = (dact * sg).astype(BF16)
            dgu_ref[a:b, :] = dgate.T
            dgu_ref[FFH + a:FFH + b, :] = dup.T
            dh = dh + _dot(dgate, w1_ref[a:b, :]) + _dot(dup, w1_ref[FFH + a:FFH + b, :])
        xv = x_ref[...]
        r = _rms(xv)
        dg_ref[...] += _col_sum(dh * xv * r)
        dxm_ref[...] = dxo + _rmsnorm_bwd(dh, xv, r, g_ref[...])

    tok = lambda: pl.BlockSpec((tm, D), lambda i: (i, 0))
    vec = lambda: pl.BlockSpec((1, D), lambda i: (0, 0))
    top_specs = [tok(), vec(), tok()] if from_loss else [tok()]
    extra_specs = [pl.BlockSpec((1, 1), lambda i: (0, 0)), vec()] if from_loss else []
    extra_shape = [S((1, 1), F32), S((1, D), F32)] if from_loss else []
    return _launch(
        body, (t // tm,),
        top_specs + [tok(), pl.BlockSpec((None, 1, D), lambda i: (l, 0, 0)),
                     pl.BlockSpec((tm, 2 * FFH), lambda i: (i, 0)),
                     _resident((2 * FFH, D), lambda i: (0, 0)),
                     _resident((FFH, D), lambda i: (0, 0))],
        [pl.BlockSpec((2 * FFH, tm), lambda i: (0, i)), tok(), tok(), vec()] + extra_specs,
        [S((2 * FFH, t), BF16), S((t, D), F32), S((t, D), BF16), S((1, D), F32)] + extra_shape,
        [],
        ((*top,) if from_loss else (top,)) + (xm, g2, gu, w1, w2), name, ("arbitrary",), comm)


def _mm_wgrad(at, b, pieces, at_rows, b_mode, group, name):
    bt = 2048
    t = at.shape[-1]
    bt = min(bt, t)
    nt = t // bt
    if at_rows:
        ka = at.shape[0] // pieces
        a_spec = pl.BlockSpec((ka, bt), lambda j, tt: (j, tt))
    else:
        ka = at.shape[0]
        a_spec = pl.BlockSpec((ka, bt), lambda j, tt: (0, tt))
    if b_mode == "shared":
        nb, b_spec = b.shape[1], pl.BlockSpec((bt, b.shape[1]), lambda j, tt: (tt, 0))
    elif b_mode == "cols":
        nb = b.shape[1] // pieces
        b_spec = pl.BlockSpec((bt, group * nb), lambda j, tt: (tt, j))
    else:
        nb, b_spec = b.shape[2], pl.BlockSpec((None, bt, b.shape[2]), lambda j, tt: (j, tt, 0))
    assert group == 1 or b_mode == "cols"

    def body(a_ref, b_ref, o_ref, acc):
        tt = pl.program_id(1)

        @pl.when(tt == 0)
        def _():
            acc[...] = jnp.zeros_like(acc)

        acc[...] += _dot(a_ref[...], b_ref[...])

        @pl.when(tt == nt - 1)
        def _():
            for k in range(group):
                o_ref[k] = acc[:, k * nb:(k + 1) * nb].astype(BF16)

    return pl.pallas_call(
        body, grid=(pieces // group, nt), name=name,
        in_specs=[a_spec, b_spec],
        out_specs=pl.BlockSpec((group, ka, nb), lambda j, tt: (j, 0, 0)),
        out_shape=S((pieces, ka, nb), BF16),
        scratch_shapes=[pltpu.VMEM((ka, group * nb), F32)],
        compiler_params=_params("parallel", "arbitrary"),
    )(at, b)


def _b_mixout(dxm, w, o, proj, c, lng, lnb, l, name, comm=None):
    t = dxm.shape[0]
    tm = 256

    def body(dxm_ref, w_ref, o_ref, g_ref, c_ref, lng_ref, lnb_ref,
             dxb_ref, dygm_ref, dO_ref, dg_ref, dc_ref, dlg_ref, dlb_ref, dcb_ref):
        @pl.when(pl.program_id(0) == 0)
        def _():
            dlg_ref[...] = jnp.zeros_like(dlg_ref)
            dlb_ref[...] = jnp.zeros_like(dlb_ref)
            dcb_ref[...] = jnp.zeros_like(dcb_ref)

        dxb = dxm_ref[...].astype(BF16)
        dxb_ref[...] = dxb
        dy = _dot_nt(dxb, w_ref[...])
        dygm_ref[...] = dy[:, 0:GMW]
        for h in range(HEADS):
            sl = slice(h * DH, (h + 1) * DH)
            ohat, rstd = _standardize(o_ref[:, sl])
            sg, dsg = _silu_and_grad(g_ref[:, sl].astype(F32))
            dyr = dy[:, GMW + h * DH:GMW + (h + 1) * DH]
            dg_ref[:, sl] = (dyr * ohat * dsg).astype(BF16)
            dO_ref[:, sl] = _standardize_bwd(dyr * sg, ohat, rstd)
        chat, rstd = _standardize(c_ref[...])
        z = chat * lng_ref[...] + lnb_ref[...]
        _, dsz = _silu_and_grad(z)
        dz = dy[:, GMW + RETW:D] * dsz
        dlg_ref[...] += _col_sum(dz * chat)
        dlb_ref[...] += _col_sum(dz)
        dc = _standardize_bwd(dz * lng_ref[...], chat, rstd)
        dcb_ref[...] += _col_sum(dc)
        dc_ref[...] = dc

    vec = lambda: pl.BlockSpec((None, 1, CVW), lambda i: (l, 0, 0))
    acc = lambda: pl.BlockSpec((1, CVW), lambda i: (0, 0))
    return _launch(
        body, (t // tm,),
        [pl.BlockSpec((tm, D), lambda i: (i, 0)),
         _resident((D, D), lambda i: (0, 0)),
         pl.BlockSpec((tm, RETW), lambda i: (i, 0)),
         pl.BlockSpec((tm, RETW), lambda i: (i, 4)),
         pl.BlockSpec((tm, CVW), lambda i: (i, 0)), vec(), vec()],
        [pl.BlockSpec((tm, D), lambda i: (i, 0)), pl.BlockSpec((tm, GMW), lambda i: (i, 0)),
         pl.BlockSpec((tm, RETW), lambda i: (i, 0)), pl.BlockSpec((tm, RETW), lambda i: (i, 0)),
         pl.BlockSpec((tm, CVW), lambda i: (i, 0)), acc(), acc(), acc()],
        [S((t, D), BF16), S((t, GMW), F32), S((t, RETW), F32), S((t, RETW), BF16), S((t, CVW), F32),
         S((1, CVW), F32), S((1, CVW), F32), S((1, CVW), F32)],
        [], (dxm, w, o, proj, c, lng, lnb), name, ("arbitrary",), comm)


def _b_gm(proj, dy, lng, lnb, ws_bf, wst_bf, bias, l, name, comm=None):
    t = proj.shape[0]
    tm = 512
    nt = t // tm

    def body(p_ref, dy_ref, lng_ref, lnb_ref, ws_ref, wst_ref, bias_ref,
             duv_ref, dws_ref, dbias_ref, dbs_ref, dlg_ref, dlb_ref):
        @pl.when(pl.program_id(0) == 0)
        def _():
            dws_ref[...] = jnp.zeros_like(dws_ref)
            dbias_ref[...] = jnp.zeros_like(dbias_ref)
            dbs_ref[...] = jnp.zeros_like(dbs_ref)
            dlg_ref[...] = jnp.zeros_like(dlg_ref)
            dlb_ref[...] = jnp.zeros_like(dlb_ref)

        for ci in range(tm // C):
            rows = slice(ci * C, (ci + 1) * C)
            u = p_ref[rows, 0:GMW].astype(F32)
            v = p_ref[rows, GMW:2 * GMW].astype(F32)
            au, dau, dav, vhat, rstd, vn, mixed, head = _gm_chunk_fwd(u, v, lng_ref[...], lnb_ref[...], ws_ref, bias_ref[...])
            dyc = dy_ref[rows, :]
            dmixed = dyc * au
            dmb = dmixed.astype(BF16)
            dbias_ref[...] += dmixed
            dvn = jnp.zeros((C, GMW), F32)
            for h in range(HEADS):
                dws_ref[h] += _dot_nt(jnp.where(head == h, dmixed, 0.0).astype(BF16), vn)
                dvn = dvn + jnp.where(head == h, _dot(wst_ref[h], dmb), 0.0)
            dlg_ref[...] += _col_sum(dvn * vhat)
            dlb_ref[...] += _col_sum(dvn)
            dav_in = _standardize_bwd(dvn * lng_ref[...], vhat, rstd)
            duv_ref[rows, 0:GMW] = (dyc * mixed * dau).astype(BF16)
            duv_ref[rows, GMW:2 * GMW] = (dav_in * dav).astype(BF16)

        @pl.when(pl.program_id(0) == nt - 1)
        def _():
            head = lax.broadcasted_iota(jnp.int32, (C, GMW), 1) // (GMW // HEADS)
            lane = lax.broadcasted_iota(jnp.int32, (C, 128), 1)
            fold = jnp.zeros((C, 128), F32)
            for h in range(HEADS):
                col = jnp.sum(jnp.where(head == h, dbias_ref[...], 0.0), axis=1, keepdims=True)
                fold = jnp.where(lane == h, col, fold)
            dbs_ref[...] = fold

    vec = lambda: pl.BlockSpec((None, 1, GMW), lambda i: (l, 0, 0))
    mats = lambda: pl.BlockSpec((None, HEADS, C, C), lambda i: (l, 0, 0, 0))
    return _launch(
        body, (nt,),
        [pl.BlockSpec((tm, 2 * GMW), lambda i: (i, 0)), pl.BlockSpec((tm, GMW), lambda i: (i, 0)),
         vec(), vec(), mats(), mats(), pl.BlockSpec((None, C, GMW), lambda i: (l, 0, 0))],
        [pl.BlockSpec((tm, 2 * GMW), lambda i: (i, 0)),
         pl.BlockSpec((HEADS, C, C), lambda i: (0, 0, 0)),
         pl.BlockSpec((C, GMW), lambda i: (0, 0)), pl.BlockSpec((C, 128), lambda i: (0, 0)),
         pl.BlockSpec((1, GMW), lambda i: (0, 0)), pl.BlockSpec((1, GMW), lambda i: (0, 0))],
        [S((t, 2 * GMW), BF16), S((HEADS, C, C), F32), S((C, GMW), F32), S((C, 128), F32),
         S((1, GMW), F32), S((1, GMW), F32)],
        [], (proj, dy, lng, lnb, ws_bf, wst_bf, bias), name, ("arbitrary",), comm)


def _b_conv(proj, dc, cw, l, name, comm=None):
    t = proj.shape[0]
    tm = 512
    nt = t // tm
    rb = 64

    def body(ap, ac, an, gp, gc, gn, dp, dcur, dn, cw_ref, dag_ref, dcw_ref, hext, dext, hrot, drot):
        i = pl.program_id(0)

        @pl.when(i == 0)
        def _():
            dcw_ref[...] = jnp.zeros_like(dcw_ref)

        _fill_ext(hext, _glu(ap, gp), _glu(ac, gc), _glu(an, gn), i, nt, tm)
        _fill_ext(dext, dp[...], dcur[...], dn[...], i, nt, tm)
        _shifted_copies(hext, hrot, tm)
        _shifted_copies(dext, drot, tm)
        for j in range(KW):
            dcw_ref[j:j + 1, :] += _col_sum(dcur[...] * _tap(hext, hrot, 0, j + 1, tm))
        for r0 in range(0, tm, rb):
            dh = jnp.zeros((rb, CVW), F32)
            for j in range(KW):
                dh = dh + cw_ref[j:j + 1, :] * _tap(dext, drot, r0, 2 * HALO - 1 - j, rb)
            a = ac[r0:r0 + rb, :].astype(F32)
            s = _sigmoid(gc[r0:r0 + rb, :].astype(F32))
            dag_ref[r0:r0 + rb, 0:CVW] = (dh * s).astype(BF16)
            dag_ref[r0:r0 + rb, CVW:2 * CVW] = (dh * a * s * (1.0 - s)).astype(BF16)

    dspecs = _conv_halo_specs(t, tm, CVW, 0)
    return _launch(
        body, (nt,),
        _conv_halo_specs(t, tm, CVW, 10) + _conv_halo_specs(t, tm, CVW, 11) + dspecs +
        [pl.BlockSpec((None, 32, CVW), lambda i: (l, 0, 0))],
        [pl.BlockSpec((tm, 2 * CVW), lambda i: (i, 0)), pl.BlockSpec((32, CVW), lambda i: (0, 0))],
        [S((t, 2 * CVW), BF16), S((32, CVW), F32)],
        [pltpu.VMEM((tm + 2 * HALO, CVW), F32), pltpu.VMEM((tm + 2 * HALO, CVW), F32),
         pltpu.VMEM((7, tm + 2 * HALO - 8, CVW), F32), pltpu.VMEM((7, tm + 2 * HALO - 8, CVW), F32)],
        (proj, proj, proj, proj, proj, proj, dc, dc, dc, cw), name, ("arbitrary",), comm)


def _b_ret_out(proj, cos2, sin2, dO, s_f, s_b, g_f, g_b, rc, name, comm=None):
    t = proj.shape[0]
    nchunks = min(RET_CHUNKS, t // C)
    tm = nchunks * C

    def body(q_ref, k_ref, v_ref, cos_ref, sin_ref, dO_ref, sf_ref, sb_ref, gf_ref, gb_ref,
             d_ref, xif_ref, xib_ref, zef_ref, zeb_ref, dq_ref, dk_ref, dv_ref):
        for ci in range(nchunks):
            rows = slice(ci * C, (ci + 1) * C)
            cos_v, sin_v = cos_ref[rows, :], sin_ref[rows, :]
            for h in range(HEADS):
                sl = slice(h * DH, (h + 1) * DH)
                qh, kh, vh = q_ref[rows, sl], k_ref[rows, sl], v_ref[rows, sl]
                dOh = dO_ref[rows, sl].astype(BF16)
                dm = d_ref[h]
                p = (_dot_nt(qh, kh) * dm).astype(BF16)
                dp = (_dot_nt(dOh, vh) * dm).astype(BF16)
                from_s = _dot_nt(dOh, jnp.concatenate([sf_ref[ci, h], sb_ref[ci, h]], axis=0))
                from_g = _dot_nt(vh, jnp.concatenate([gf_ref[ci, h], gb_ref[ci, h]], axis=0))
                kg = _dot(kh, jnp.concatenate([gf_ref[ci, h], gb_ref[ci, h]], axis=1))
                dqr = _dot(dp, kh) + xif_ref[h] * from_s[:, 0:DH] + xib_ref[h] * from_s[:, DH:2 * DH]
                dkr = (_dot_tn(dp, qh) + zef_ref[h] * from_g[:, 0:DH] + zeb_ref[h] * from_g[:, DH:2 * DH]) * SCALE
                dv = _dot_tn(p, dOh) + zef_ref[h] * kg[:, 0:DH] + zeb_ref[h] * kg[:, DH:2 * DH]
                dq_ref[rows, sl] = _rot_t(dqr, cos_v, sin_v).astype(BF16)
                dk_ref[rows, sl] = _rot_t(dkr, cos_v, sin_v).astype(BF16)
                dv_ref[rows, sl] = dv.astype(BF16)

    const = lambda: pl.BlockSpec((HEADS, C, DH), lambda i: (0, 0, 0))
    state = lambda: pl.BlockSpec((nchunks, HEADS, DH, DH), lambda i: (i, 0, 0, 0))
    tok = lambda: pl.BlockSpec((tm, RETW), lambda i: (i, 0))
    return _launch(
        body, (t // tm,),
        [pl.BlockSpec((tm, RETW), lambda i, cb=cb: (i, cb)) for cb in (1, 2, 3)] +
        [pl.BlockSpec((tm, DH), lambda i: (i, 0)), pl.BlockSpec((tm, DH), lambda i: (i, 0)), tok(),
         state(), state(), state(), state()] + [const() for _ in range(5)],
        [tok(), tok(), tok()],
        [S((t, RETW), BF16) for _ in range(3)], [],
        (proj, proj, proj, cos2, sin2, dO, s_f, s_b, g_f, g_b, rc["f"]["D"] + rc["b"]["D"],
         rc["f"]["XI"], rc["b"]["XI"], rc["f"]["ZETA"], rc["b"]["ZETA"]), name, ("parallel",), comm)


def _b_inproj(d_uv, dqkv, d_g, d_ag, w, x, g1, dxm, l, name, comm=None):
    t = x.shape[0]
    tm = 512

    def body(duv_ref, dq_ref, dk_ref, dv_ref, dg_ref, dag_ref, w_ref, x_ref, g_ref, dxm_ref,
             dp_ref, dx_ref, dn_ref):
        @pl.when(pl.program_id(0) == 0)
        def _():
            dn_ref[...] = jnp.zeros_like(dn_ref)

        for k, part in enumerate((duv_ref, dq_ref, dk_ref, dv_ref, dg_ref, dag_ref)):
            dp_ref[:, 512 * k:512 * (k + 1)] = part[...]
        dh = _dot_nt(dp_ref[...], w_ref[...])
        xv = x_ref[...]
        r = _rms(xv)
        dn_ref[...] += _col_sum(dh * xv * r)
        dx_ref[...] = dxm_ref[...] + _rmsnorm_bwd(dh, xv, r, g_ref[...])

    half = lambda: pl.BlockSpec((tm, 512), lambda i: (i, 0))
    full = lambda: pl.BlockSpec((tm, D), lambda i: (i, 0))
    return _launch(
        body, (t // tm,),
        [half() for _ in range(6)] +
        [_resident((D, INW), lambda i: (0, 0)), full(), pl.BlockSpec((None, 1, D), lambda i: (l, 0, 0)), full()],
        [pl.BlockSpec((tm, INW), lambda i: (i, 0)), full(), pl.BlockSpec((1, D), lambda i: (0, 0))],
        [S((t, INW), BF16), S((t, D), F32), S((1, D), F32)], [],
        (d_uv, *dqkv, d_g, d_ag, w, x, g1, dxm), name, ("arbitrary",), comm)


class _PairSwap:
    def __init__(self, units):
        self.inputs = list(units)
        self.out_shape = [S((4,) + u.shape[1:], BF16) for u in units]
        n = len(units)
        self.scratch = [pltpu.SemaphoreType.DMA((n, 4)), pltpu.SemaphoreType.DMA((n, 4))]

    def run(self, phase, ins, outs, scr):
        ssem, rsem = scr
        x, y, c, _ = _place()
        copies = lambda: [pltpu.make_async_remote_copy(src_ref=ins[u].at[2 * chip + (1 - c)], dst_ref=outs[u].at[chip],
                                                       send_sem=ssem.at[u, chip], recv_sem=rsem.at[u, chip],
                                                       device_id=(x, y, 1 - c), device_id_type=MESH)
                          for u in range(len(self.inputs)) for chip in range(4)]
        if phase == "start":
            for cp in copies():
                cp.start()
        elif phase == "finish":
            for cp in copies():
                cp.wait_recv()
            for cp in copies():
                cp.wait_send()


def _pair_add(g, q, name):
    _, mm, nn = g.shape
    bm = _row_block(mm)

    def body(c_ref, g_ref, q_ref, h_ref):
        h_ref[...] = (g_ref[...].astype(F32) + q_ref[...].astype(F32)).astype(BF16)

    blk = lambda: pl.BlockSpec((None, bm, nn), lambda qq, i, c_ref: (qq, i, 0))
    return pl.pallas_call(
        body, name=name,
        grid_spec=pltpu.PrefetchScalarGridSpec(
            num_scalar_prefetch=1, grid=(4, mm // bm),
            in_specs=[pl.BlockSpec((None, None, bm, nn), lambda qq, i, c_ref: (qq, c_ref[0], i, 0)), blk()],
            out_specs=blk()),
        out_shape=S((4, mm, nn), BF16),
        compiler_params=_params("parallel", "parallel"),
    )(lax.axis_index("c").astype(jnp.int32).reshape(1), g.reshape(4, 2, mm, nn), q)


_BIG = ("w_in", "w_out", "w_ffn_in", "w_ffn_out")
_KIND = dict(w_in="cols", w_out="rows", w_ffn_in="lead", w_ffn_out="rows")
CWP = 128
EARLY_ROWS, LATE_ROWS = 152, 8


def _step(x, tgt, wts, sh, cw_pad):
    t = x.shape[0]
    rc = _ret_consts()
    cos2, sin2 = (jnp.asarray(a) for a in _rope_tables(t))
    n1 = wts["norm1_g"].reshape(LAYERS, 1, D)
    n2 = wts["norm2_g"].reshape(LAYERS, 1, D)
    gm_lng = wts["gm_ln_g"].reshape(LAYERS, 1, GMW)
    gm_lnb = wts["gm_ln_b"].reshape(LAYERS, 1, GMW)
    ws_bf = wts["gm_ws"].astype(BF16)
    wst_bf = jnp.swapaxes(wts["gm_ws"], 2, 3).astype(BF16)
    bias = jnp.repeat(jnp.swapaxes(wts["gm_bs"], 1, 2), GMW // HEADS, axis=2)
    cb = wts["conv_b"].reshape(LAYERS, 1, CVW)
    cv_lng = wts["conv_ln_g"].reshape(LAYERS, 1, CVW)
    cv_lnb = wts["conv_ln_b"].reshape(LAYERS, 1, CVW)
    unit = lambda f, l: (sh[f][l], _KIND[f])
    cshard = CVW // NDEV

    full = {f: [None] * LAYERS for f in _BIG}
    full["w_in"][0], cw_all = _comm_only(_Gather([unit("w_in", 0), (cw_pad, "lead")]), "gather_first")
    cw = jnp.transpose(cw_all[:, :, :, :cshard], (1, 2, 0, 3)).reshape(LAYERS, 32, CVW)

    gcf, gcb = rc["f"]["gC"], rc["b"]["gC"]
    saved = []
    for l in range(LAYERS):
        first = l == 0
        (proj, h1), got = _f_inproj(x, n1, full["w_in"][l], cos2, sin2, l, f"f_inproj_{l}",
                                    _Gather([unit("w_ffn_in", 0)]) if first else None)
        if first:
            full["w_ffn_in"][0], = got
        y_gm = _f_gm(proj, gm_lng, gm_lnb, ws_bf, bias, l, f"f_gm_{l}")
        s_f, s_b = _scan_pair(proj, None, 2, rc["f"]["ZETA"], rc["b"]["ZETA"], gcf, gcb, True, f"f_ret_state_{l}")
        o = _f_ret_out(proj, s_f, s_b, rc, f"f_ret_out_{l}")
        (c, y_cv), got = _f_conv(proj, cw, cb, cv_lng, cv_lnb, l, f"f_conv_{l}",
                                 _Gather([unit("w_out", 0), unit("w_ffn_out", 0)]) if first else None)
        if first:
            full["w_out"][0], full["w_ffn_out"][0] = got
        xm, ycat = _f_mixout(x, y_gm, y_cv, o, proj, full["w_out"][l], f"f_mixout_{l}")
        w1t = full["w_ffn_in"][l].reshape(2 * FFH, D)
        (xo, h2, gu, act), got = _f_ffn(xm, n2, w1t, full["w_ffn_out"][l], l, f"f_ffn_{l}",
                                        _Gather([unit(f, 1) for f in _BIG]) if first else None)
        if first:
            full["w_in"][1], full["w_out"][1], full["w_ffn_in"][1], full["w_ffn_out"][1] = got
        saved.append(dict(x=x, proj=proj, h1=h1, o=o, s_f=s_f, s_b=s_b, c=c, xm=xm, ycat=ycat, h2=h2, gu=gu, act=act))
        x = xo

    parts = {f: [None] * LAYERS for f in _BIG}
    small = [None] * LAYERS
    norm1 = [None] * LAYERS
    upper = None
    top = (x, wts["final_g"].reshape(1, D), tgt)
    for l in reversed(range(LAYERS)):
        sv = saved[l]
        outs, got = _b_ffn(top, sv["xm"], n2, sv["gu"], full["w_ffn_in"][l].reshape(2 * FFH, D), full["w_ffn_out"][l], l,
                           f"b_ffn_{l}", _Scatter(upper) if upper else None)
        dgu_t, dxm, dxo_bf, d_n2 = outs[:4]
        if l == LAYERS - 1:
            loss, d_final = outs[4:]
        if upper:
            for f, p in zip(_BIG, got):
                parts[f][l + 1] = p
        g_f2 = _mm_wgrad(sv["act"], dxo_bf, NDEV // 2, True, "shared", 1, f"g_ffn_out_{l}").reshape(NDEV, FFH // NDEV, D)
        g_f1 = _mm_wgrad(dgu_t, sv["h2"], NDEV, True, "shared", 1, f"g_ffn_in_{l}")
        (dxm_bf, dy_gm, dO, d_g, dc, d_cvlg, d_cvlb, d_cb), (q_f1, q_f2) = _b_mixout(
            dxm, full["w_out"][l], sv["o"], sv["proj"], sv["c"], cv_lng, cv_lnb, l, f"b_mixout_{l}", _PairSwap([g_f1, g_f2]))
        h_f1 = _pair_add(g_f1, q_f1, f"pair_add_w_ffn_in_{l}")
        h_f2 = _pair_add(g_f2, q_f2, f"pair_add_w_ffn_out_{l}")
        g_out = _mm_wgrad(sv["ycat"], dxm_bf, 1, False, "shared", 1, f"g_out_{l}").reshape(NDEV, D // NDEV, D)
        last = l == 0
        (d_uv, d_ws, _, d_bs_fold, d_gmlg, d_gmlb), (q_out,) = _b_gm(
            sv["proj"], dy_gm, gm_lng, gm_lnb, ws_bf, wst_bf, bias, l, f"b_gm_{l}", _PairSwap([g_out]))
        h_out = _pair_add(g_out, q_out, f"pair_add_w_out_{l}")
        (d_ag, d_cw), got = _b_conv(sv["proj"], dc, cw, l, f"b_conv_{l}", _Scatter([h_f1]) if last else None)
        if last:
            parts["w_ffn_in"][l], = got
        small[l] = dict(gm_ln_g=d_gmlg[0], gm_ln_b=d_gmlb[0], gm_ws=d_ws, gm_bs=d_bs_fold[:, :HEADS].T, conv_w=d_cw[:KW],
                        conv_b=d_cb[0], conv_ln_g=d_cvlg[0], conv_ln_b=d_cvlb[0], norm2_g=d_n2[0])
        comm = None
        if last:
            early_g = {k: jnp.stack([small[ll][k] for ll in range(LAYERS)]) for k in small[0]}
            early_g["final_g"] = d_final[0]
            early_buf = _pack([early_g[k] for k in _SMALL_EARLY] + [loss], EARLY_ROWS)
            comm = _Comms([_Scatter([h_out, h_f2]), _Gather([(early_buf, "lead")])])
        g_f, g_b = _scan_pair(sv["proj"], dO, 1, rc["f"]["XI"], rc["b"]["XI"], gcf, gcb, False, f"b_ret_state_{l}")
        dqkv, got = _b_ret_out(sv["proj"], cos2, sin2, dO, sv["s_f"], sv["s_b"], g_f, g_b, rc, f"b_ret_out_{l}", comm)
        if last:
            parts["w_out"][l], parts["w_ffn_out"][l], early_parts = got
        (dproj, top, d_n1), _ = _b_inproj(d_uv, dqkv, d_g, d_ag, full["w_in"][l], sv["x"], n1, dxm, l, f"b_inproj_{l}")
        norm1[l] = d_n1[0]
        g_in = _mm_wgrad(sv["h1"], dproj, NDEV, False, "cols", 2, f"g_in_{l}")
        q_in, = _comm_only(_PairSwap([g_in]), f"pair_swap_w_in_{l}")
        h_in = _pair_add(g_in, q_in, f"pair_add_w_in_{l}")
        if last:
            tail = [h_in]
        else:
            upper = [h_in, h_out, h_f1, h_f2]
    late_buf = _pack([jnp.stack(norm1)], LATE_ROWS)
    parts["w_in"][0], late_parts = _comm_only(_Comms([_Scatter(tail), _Gather([(late_buf, "lead")])]), "exchange_last")
    return loss, top, parts, (early_parts, late_parts)


def _adamw(w, g, m, v):
    m = ADAM_B1 * m + (1.0 - ADAM_B1) * g
    v = ADAM_B2 * v + (1.0 - ADAM_B2) * (g * g)
    m_hat = m / (1.0 - ADAM_B1 ** ADAM_STEP)
    v_hat = v / (1.0 - ADAM_B2 ** ADAM_STEP)
    return -ADAM_LR * (m_hat / (jnp.sqrt(v_hat) + ADAM_EPS) + ADAM_WD * w), m, v


def _cast_blocks(ws):
    def body(*refs):
        ins, outs = refs[:len(ws)], refs[len(ws):]
        for k, src in enumerate(ins):
            for l in range(LAYERS):
                outs[k * LAYERS + l][...] = src[l].astype(BF16)

    outs = pl.pallas_call(body, name="cast_blocks", out_shape=[S(w.shape[1:], BF16) for w in ws for _ in range(LAYERS)],
                          compiler_params=_params())(*ws)
    return [list(outs[k * LAYERS:(k + 1) * LAYERS]) for k in range(len(ws))]


def _row_block(mm):
    return next(b for b in (256, 352, 128) if mm % b == 0)


def _sum_adam(parts, w, m, v, l, prev, name):
    _, mm, nn = parts.shape
    bm = _row_block(mm)

    def body(p_ref, w_ref, m_ref, v_ref, *rest):
        g_ref, d_ref, nm_ref, nv_ref = rest[-4:]
        g = p_ref[0].astype(F32)
        for s in range(1, 4):
            g = g + p_ref[s].astype(F32)
        g_ref[...] = g
        d_ref[...], nm_ref[...], nv_ref[...] = _adamw(w_ref[...], g, m_ref[...], v_ref[...])

    blk = lambda: pl.BlockSpec((None, bm, nn), lambda i: (l, i, 0))
    prev = list(prev) if prev else []
    return pl.pallas_call(
        body, grid=(mm // bm,), name=name,
        in_specs=[pl.BlockSpec((4, bm, nn), lambda i: (0, i, 0)), blk(), blk(), blk()] + [_ANY] * len(prev),
        out_specs=[blk() for _ in range(4)],
        out_shape=[S(w.shape, F32) for _ in range(4)],
        input_output_aliases={4 + j: j for j in range(len(prev))},
        compiler_params=_params("parallel"),
    )(parts, w, m, v, *prev)


def _sum_small(parts):
    n = len(parts)

    def body(*refs):
        for p_ref, o_ref in zip(refs[:n], refs[n:]):
            g = p_ref[0]
            for s in range(1, NDEV):
                g = g + p_ref[s]
            o_ref[...] = g

    return pl.pallas_call(body, name="sum_small", out_shape=[S(p.shape[1:], F32) for p in parts],
                          compiler_params=_params())(*parts)


def _adam_small(quads):
    n = len(quads)
    as2d = lambda a: a.reshape(1, -1) if a.ndim == 1 else a

    def body(*refs):
        ins, outs = refs[:4 * n], refs[4 * n:]
        for k in range(n):
            g_ref, w_ref, m_ref, v_ref = ins[4 * k:4 * k + 4]
            outs[3 * k][...], outs[3 * k + 1][...], outs[3 * k + 2][...] = _adamw(w_ref[...], g_ref[...], m_ref[...], v_ref[...])

    flat = [as2d(a) for q in quads for a in q]
    outs = pl.pallas_call(body, name="adam_small", out_shape=[S(as2d(q[1]).shape, F32) for q in quads for _ in range(3)],
                          compiler_params=_params())(*flat)
    return [tuple(o.reshape(q[1].shape) for o in outs[3 * k:3 * k + 3]) for k, q in enumerate(quads)]


_SMALL = ("norm1_g", "gm_ln_g", "gm_ln_b", "gm_ws", "gm_bs", "conv_w", "conv_b", "conv_ln_g", "conv_ln_b",
          "norm2_g", "final_g")
_SMALL_EARLY = _SMALL[1:]
_NAMES = ("norm1_g", "w_in", "gm_ln_g", "gm_ln_b", "gm_ws", "gm_bs", "conv_w", "conv_b", "conv_ln_g", "conv_ln_b",
          "w_out", "norm2_g", "w_ffn_in", "w_ffn_out", "final_g")


def _pack(parts, rows):
    flat = jnp.concatenate([p.reshape(-1) for p in parts])
    return jnp.pad(flat, (0, rows * 1024 - flat.shape[0])).reshape(rows, 1024)


def _unpack(buf, shapes):
    flat = buf.reshape(-1)
    out, o = [], 0
    for shp in shapes:
        sz = int(np.prod(shp))
        out.append(flat[o:o + sz].reshape(shp))
        o += sz
    return out


def kernel(x, norm1_g, w_in, gm_ln_g, gm_ln_b, gm_ws, gm_bs, conv_w, conv_b, conv_ln_g, conv_ln_b, w_out, norm2_g, w_ffn_in, w_ffn_out, final_g, loss_target, m_norm1_g, m_w_in, m_gm_ln_g, m_gm_ln_b, m_gm_ws, m_gm_bs, m_conv_w, m_conv_b, m_conv_ln_g, m_conv_ln_b, m_w_out, m_norm2_g, m_w_ffn_in, m_w_ffn_out, m_final_g, v_norm1_g, v_w_in, v_gm_ln_g, v_gm_ln_b, v_gm_ws, v_gm_bs, v_conv_w, v_conv_b, v_conv_ln_g, v_conv_ln_b, v_w_out, v_norm2_g, v_w_ffn_in, v_w_ffn_out, v_final_g):
    w = dict(norm1_g=norm1_g, w_in=w_in, gm_ln_g=gm_ln_g, gm_ln_b=gm_ln_b, gm_ws=gm_ws, gm_bs=gm_bs, conv_w=conv_w,
             conv_b=conv_b, conv_ln_g=conv_ln_g, conv_ln_b=conv_ln_b, w_out=w_out, norm2_g=norm2_g, w_ffn_in=w_ffn_in,
             w_ffn_out=w_ffn_out, final_g=final_g)
    mo = dict(norm1_g=m_norm1_g, w_in=m_w_in, gm_ln_g=m_gm_ln_g, gm_ln_b=m_gm_ln_b, gm_ws=m_gm_ws, gm_bs=m_gm_bs,
              conv_w=m_conv_w, conv_b=m_conv_b, conv_ln_g=m_conv_ln_g, conv_ln_b=m_conv_ln_b, w_out=m_w_out,
              norm2_g=m_norm2_g, w_ffn_in=m_w_ffn_in, w_ffn_out=m_w_ffn_out, final_g=m_final_g)
    vo = dict(norm1_g=v_norm1_g, w_in=v_w_in, gm_ln_g=v_gm_ln_g, gm_ln_b=v_gm_ln_b, gm_ws=v_gm_ws, gm_bs=v_gm_bs,
              conv_w=v_conv_w, conv_b=v_conv_b, conv_ln_g=v_conv_ln_g, conv_ln_b=v_conv_ln_b, w_out=v_w_out,
              norm2_g=v_norm2_g, w_ffn_in=v_w_ffn_in, w_ffn_out=v_w_ffn_out, final_g=v_final_g)
    t = x.shape[1]
    me = 4 * lax.axis_index("x") + 2 * lax.axis_index("y") + lax.axis_index("c")
    cshard = conv_w.shape[2]

    cw_pad = jnp.pad(conv_w, ((0, 0), (0, 32 - KW), (0, CWP - cshard)))
    flip = lambda a: jnp.swapaxes(a, 1, 2)
    big = {f: tuple(flip(a[f]) if f == "w_ffn_in" else a[f] for a in (w, mo, vo)) for f in _BIG}
    sh = dict(zip(_BIG, _cast_blocks([big[f][0] for f in _BIG])))
    loss, dx, parts, small_parts = _step(x.reshape(t, D), loss_target.reshape(t, D), w, sh, cw_pad)

    grads, delta, new_m, new_v = {}, {}, {}, {}
    for f in _BIG:
        outs = None
        for l in reversed(range(LAYERS)):
            outs = _sum_adam(parts[f][l], *big[f], l, outs, f"sum_adam_{f}_{l}")
        grads[f], delta[f], new_m[f], new_v[f] = [flip(a) for a in outs] if f == "w_ffn_in" else outs

    early_sum, late_sum = _sum_small(small_parts)
    early_shapes = [(LAYERS, KW, CVW) if k == "conv_w" else w[k].shape for k in _SMALL_EARLY]
    grads["norm1_g"], = _unpack(late_sum, [w["norm1_g"].shape])
    *early, total = _unpack(early_sum, early_shapes + [()])
    for k, g in zip(_SMALL_EARLY, early):
        grads[k] = lax.dynamic_slice_in_dim(g, me * cshard, cshard, axis=2) if k == "conv_w" else g
    for k, (d, nm, nv) in zip(_SMALL, _adam_small([(grads[k], w[k], mo[k], vo[k]) for k in _SMALL])):
        delta[k], new_m[k], new_v[k] = d, nm, nv

    return (total, dx.reshape(1, t, D), *[grads[k] for k in _NAMES], *[delta[k] for k in _NAMES],
            *[new_m[k] for k in _NAMES], *[new_v[k] for k in _NAMES])
```

```python
import functools

import numpy as np
import jax
import jax.numpy as jnp
from jax import lax
from jax.experimental import pallas as pl
from jax.experimental.pallas import tpu as pltpu

F32, BF16 = jnp.float32, jnp.bfloat16
S = jax.ShapeDtypeStruct

D = 1024
INW = 3072
Q_BLK, K_BLK, V_BLK, G_BLK = 1, 2, 3, 4
CONV_A_BLK, CONV_GATE_BLK = 10, 11
GMW = 256
RETW = 512
CVW = 256
HEADS = 4
DH = 128
C = 128
KW = 31
HALO = 16
FFH = 2816
NDEV = 8
FFB = 2 * FFH // NDEV
FF_CHUNKS = ((0, 768), (768, 1536), (1536, 2304), (2304, FFH))
EPS = 1e-6
LAYERS = 2
SCALE = DH ** -0.5
VMEM_LIMIT = 56 * 1024 * 1024

ADAM_LR, ADAM_B1, ADAM_B2, ADAM_EPS, ADAM_WD, ADAM_STEP = 0.001, 0.9, 0.999, 1e-08, 0.01, 10

_SQRT_HALF = 0.7071067811865476
_INV_SQRT_2PI = 0.3989422804014327


def _params(*sem):
    return pltpu.CompilerParams(dimension_semantics=sem or None, vmem_limit_bytes=VMEM_LIMIT)


def _resident(shape, index_map):
    return pl.BlockSpec(shape, index_map, pipeline_mode=pl.Buffered(1))


def _dot(a, b):
    return jnp.dot(a, b, preferred_element_type=F32)


def _dot_nt(a, b):
    return lax.dot_general(a, b, (((1,), (1,)), ((), ())), preferred_element_type=F32)


def _dot_tn(a, b):
    return lax.dot_general(a, b, (((0,), (0,)), ((), ())), preferred_element_type=F32)


def _sigmoid(x):
    return 1.0 / (1.0 + jnp.exp(-x))


def _gelu_and_grad(x):
    cdf = 0.5 * (1.0 + lax.erf(x * _SQRT_HALF))
    return x * cdf, cdf + x * jnp.exp(-0.5 * x * x) * _INV_SQRT_2PI


def _silu_and_grad(x):
    s = _sigmoid(x)
    return x * s, s * (1.0 + x * (1.0 - s))


def _standardize(x):
    mu = jnp.mean(x, axis=-1, keepdims=True)
    d = x - mu
    rstd = lax.rsqrt(jnp.mean(d * d, axis=-1, keepdims=True) + EPS)
    return d * rstd, rstd


def _standardize_bwd(dxhat, xhat, rstd):
    m1 = jnp.mean(dxhat, axis=-1, keepdims=True)
    m2 = jnp.mean(dxhat * xhat, axis=-1, keepdims=True)
    return rstd * (dxhat - m1 - xhat * m2)


def _rms(x):
    return lax.rsqrt(jnp.mean(x * x, axis=-1, keepdims=True) + EPS)


def _rmsnorm_bwd(dy, x, r, g):
    u = dy * g
    return r * u - x * (r * r * r) * jnp.mean(u * x, axis=-1, keepdims=True)


def _col_sum(a):
    return jnp.sum(a, axis=0, keepdims=True)


def _rot(t, cos2, sin2):
    return t * cos2 + pltpu.roll(t, DH // 2, axis=1) * sin2


def _rot_t(dt, cos2, sin2):
    return dt * cos2 + pltpu.roll(dt * sin2, DH // 2, axis=1)


MESH = pl.DeviceIdType.MESH
_HBM = pl.BlockSpec(memory_space=pltpu.HBM)
_ANY = pl.BlockSpec(memory_space=pl.ANY)


def _place():
    x, y, c = lax.axis_index("x"), lax.axis_index("y"), lax.axis_index("c")
    return x, y, c, ((1 - x, y), (x, 1 - y), (1 - x, 1 - y))


def _slot(full, kind, width, i):
    if kind == "cols":
        return full.at[:, pl.ds(pl.multiple_of(i * width, 128), width)]
    if kind == "rows":
        return full.at[pl.ds(pl.multiple_of(i * width, 16), width), :]
    return full.at[i]


class _Gather:
    def __init__(self, units):
        self.units = units
        self.inputs = [u[0] for u in units]
        self.out_shape = []
        for src, kind in units:
            r, c = src.shape[-2:]
            shape = {"cols": (r, NDEV * c), "rows": (NDEV * r, c), "lead": (NDEV,) + src.shape}[kind]
            self.out_shape.append(S(shape, src.dtype))
        n = len(units)
        self.scratch = [pltpu.SemaphoreType.DMA((n, 7)), pltpu.SemaphoreType.DMA((n, 7)), pltpu.SemaphoreType.DMA((n,))]

    def run(self, phase, ins, outs, scr):
        ssem, rsem, lsem = scr
        x, y, c, chips = _place()
        me, sib = 4 * x + 2 * y + c, (x, y, 1 - c)
        idx = lambda chip, core: 4 * chip[0] + 2 * chip[1] + core
        for u, (src_arr, kind) in enumerate(self.units):
            src, full = ins[u], outs[u]
            width = src_arr.shape[-1] if kind == "cols" else src_arr.shape[-2]
            slot = functools.partial(_slot, full, kind, width)

            def copy(k, block, to, from_src=False):
                return pltpu.make_async_remote_copy(src_ref=src if from_src else slot(block), dst_ref=slot(block),
                                                    send_sem=ssem.at[u, k], recv_sem=rsem.at[u, k],
                                                    device_id=to, device_id_type=MESH)

            mine = lambda: pltpu.make_async_copy(src, slot(me), lsem.at[u])
            first = lambda: [copy(0, me, sib, True)] + [copy(1 + j, me, (*chip, c), True) for j, chip in enumerate(chips)]
            passed = lambda j: copy(4 + j, idx(chips[j], c), sib)
            if phase == "start":
                mine().start()
                for cp in first():
                    cp.start()
            elif phase == "forward":
                for j, chip in enumerate(chips):
                    copy(1 + j, idx(chip, c), sib).wait_recv()
                    passed(j).start()
            else:
                copy(0, idx((x, y), 1 - c), sib).wait_recv()
                for j, chip in enumerate(chips):
                    copy(4 + j, idx(chip, 1 - c), sib).wait_recv()
                for cp in first() + [passed(j) for j in range(3)]:
                    cp.wait_send()
                mine().wait()


class _Scatter:
    def __init__(self, units):
        self.units = units
        self.inputs = list(units)
        self.out_shape = [S(u.shape, u.dtype) for u in units]
        n = len(units)
        self.scratch = [pltpu.SemaphoreType.DMA((n, 3)), pltpu.SemaphoreType.DMA((n, 3)), pltpu.SemaphoreType.DMA((n,))]

    def run(self, phase, ins, outs, scr):
        ssem, rsem, lsem = scr
        x, y, c, chips = _place()
        myq = 2 * x + y
        for u in range(len(self.units)):
            h, p = ins[u], outs[u]

            def copy(k, chip, send_to_them):
                q = 2 * chip[0] + chip[1]
                return pltpu.make_async_remote_copy(src_ref=h.at[q], dst_ref=p.at[myq if send_to_them else q],
                                                    send_sem=ssem.at[u, k], recv_sem=rsem.at[u, k],
                                                    device_id=(*chip, c), device_id_type=MESH)

            mine = lambda: pltpu.make_async_copy(h.at[myq], p.at[myq], lsem.at[u])
            sends = lambda: [copy(k, chip, True) for k, chip in enumerate(chips)]
            if phase == "start":
                mine().start()
                for cp in sends():
                    cp.start()
            elif phase == "finish":
                for k, chip in enumerate(chips):
                    copy(k, chip, False).wait_recv()
                for cp in sends():
                    cp.wait_send()
                mine().wait()


class _Comms:
    def __init__(self, parts):
        self.parts = parts
        self.inputs = [a for p in parts for a in p.inputs]
        self.out_shape = [a for p in parts for a in p.out_shape]
        self.scratch = [a for p in parts for a in p.scratch]

    def run(self, phase, ins, outs, scr):
        i = o = s = 0
        for p in self.parts:
            ni, no, ns = len(p.inputs), len(p.out_shape), len(p.scratch)
            p.run(phase, ins[i:i + ni], outs[o:o + no], scr[s:s + ns])
            i, o, s = i + ni, o + no, s + ns


def _launch(body, grid, in_specs, out_specs, out_shape, scratch, args, name, sem, comm=None):
    if comm is None:
        outs = pl.pallas_call(body, grid=grid, name=name, in_specs=in_specs, out_specs=out_specs, out_shape=out_shape,
                              scratch_shapes=scratch, compiler_params=_params(*sem))(*args)
        return list(outs), []
    n_in, n_out, n_scr = len(args), len(out_shape), len(scratch)
    ci, co = len(comm.inputs), len(comm.out_shape)
    nsteps = int(np.prod(grid))
    fwd_step = (7 * nsteps) // 8

    def hosted(*refs):
        a = refs[:n_in]
        ca = refs[n_in:n_in + ci]
        o = refs[n_in + ci:n_in + ci + n_out]
        cout = refs[n_in + ci + n_out:n_in + ci + n_out + co]
        s = refs[n_in + ci + n_out + co:n_in + ci + n_out + co + n_scr]
        cs = refs[n_in + ci + n_out + co + n_scr:]
        step = pl.program_id(0)
        for d in range(1, len(grid)):
            step = step * grid[d] + pl.program_id(d)

        @pl.when(step == 0)
        def _():
            comm.run("start", ca, cout, cs)

        body(*a, *o, *s)

        @pl.when(step == fwd_step)
        def _():
            comm.run("forward", ca, cout, cs)

        @pl.when(step == nsteps - 1)
        def _():
            comm.run("finish", ca, cout, cs)

    outs = pl.pallas_call(
        hosted, grid=grid, name=name, in_specs=list(in_specs) + [_HBM] * ci, out_specs=list(out_specs) + [_HBM] * co,
        out_shape=list(out_shape) + comm.out_shape, scratch_shapes=list(scratch) + comm.scratch,
        compiler_params=_params(*["arbitrary"] * len(grid)))(*args, *comm.inputs)
    return list(outs[:n_out]), list(outs[n_out:])


def _comm_only(comm, name):
    ci, co = len(comm.inputs), len(comm.out_shape)

    def body(*refs):
        ca, cout, cs = refs[:ci], refs[ci:ci + co], refs[ci + co:]
        for phase in ("start", "forward", "finish"):
            comm.run(phase, ca, cout, cs)

    return pl.pallas_call(body, name=name, in_specs=[_HBM] * ci, out_specs=[_HBM] * co, out_shape=comm.out_shape,
                          scratch_shapes=comm.scratch, compiler_params=_params())(*comm.inputs)


def _ret_consts():
    idx = np.arange(C, dtype=np.float32)
    gf = (1.0 - np.exp2(-5.0 - np.arange(HEADS, dtype=np.float32))).astype(np.float32)
    out = {}
    for name, gamma, fwd in (("f", gf, True), ("b", gf[::-1].copy(), False)):
        lg = np.log(gamma).astype(np.float32)[:, None]
        diff = idx[:, None] - idx[None, :]
        if fwd:
            mask = diff >= 0
            dist = np.where(mask, diff, 0.0)
            zeta = np.exp(lg * (C - 1 - idx))
            xi = np.exp(lg * (idx + 1))
        else:
            mask = diff < 0
            dist = np.where(mask, -diff, 0.0)
            zeta = np.exp(lg * idx)
            xi = np.exp(lg * (C - idx))
        dm = np.where(mask[None], np.exp(lg[:, :, None] * dist[None]), 0.0).astype(np.float32)
        bc = lambda vec: np.ascontiguousarray(np.broadcast_to(vec.astype(np.float32)[:, :, None], (HEADS, C, DH)))
        out[name] = dict(D=dm, XI=bc(xi), ZETA=bc(zeta), gC=[float(v) for v in np.exp(lg[:, 0] * C).astype(np.float32)])
    return out


def _rope_tables(t):
    half = DH // 2
    inv_freq = (np.float32(10000.0) ** (-np.arange(half, dtype=np.float32) / np.float32(half))).astype(np.float32)
    ang = (np.arange(t, dtype=np.float32)[:, None] * inv_freq[None, :]).astype(np.float64)
    cos, sin = np.cos(ang).astype(np.float32), np.sin(ang).astype(np.float32)
    return np.concatenate([cos, cos], axis=1), np.concatenate([-sin, sin], axis=1)


def _f_inproj(x, g1, w, cos2, sin2, l, name, comm=None):
    t = x.shape[0]
    tm = 512

    def body(x_ref, g_ref, w_ref, cos_ref, sin_ref, proj_ref, ht_ref):
        xv = x_ref[...]
        h = (xv * _rms(xv) * g_ref[...]).astype(BF16)
        ht_ref[...] = h.T
        for nb in range(INW // 512):
            cs = slice(nb * 512, (nb + 1) * 512)
            res = _dot(h, w_ref[:, cs])
            if nb in (Q_BLK, K_BLK):
                for hh in range(HEADS):
                    r = _rot(res[:, hh * DH:(hh + 1) * DH], cos_ref[...], sin_ref[...])
                    proj_ref[:, nb * 512 + hh * DH:nb * 512 + (hh + 1) * DH] = (r * SCALE if nb == K_BLK else r).astype(BF16)
            else:
                proj_ref[:, cs] = res.astype(BF16)

    return _launch(
        body, (t // tm,),
        [pl.BlockSpec((tm, D), lambda i: (i, 0)),
         pl.BlockSpec((None, 1, D), lambda i: (l, 0, 0)),
         _resident((D, INW), lambda i: (0, 0)),
         pl.BlockSpec((tm, DH), lambda i: (i, 0)), pl.BlockSpec((tm, DH), lambda i: (i, 0))],
        [pl.BlockSpec((tm, INW), lambda i: (i, 0)), pl.BlockSpec((D, tm), lambda i: (0, i))],
        [S((t, INW), BF16), S((D, t), BF16)], [], (x, g1, w, cos2, sin2), name, ("parallel",), comm)


def _gm_chunk_fwd(u, v, lng, lnb, ws_ref, bias):
    au, dau = _gelu_and_grad(u)
    av, dav = _gelu_and_grad(v)
    vhat, rstd = _standardize(av)
    vn = (vhat * lng + lnb).astype(BF16)
    head = lax.broadcasted_iota(jnp.int32, (C, GMW), 1) // (GMW // HEADS)
    mixed = bias
    for h in range(HEADS):
        mixed = mixed + jnp.where(head == h, _dot(ws_ref[h], vn), 0.0)
    return au, dau, dav, vhat, rstd, vn, mixed, head


def _f_gm(proj, lng, lnb, ws_bf, bias, l, name):
    t = proj.shape[0]
    tm = 512

    def body(p_ref, lng_ref, lnb_ref, ws_ref, bias_ref, y_ref):
        for ci in range(tm // C):
            rows = slice(ci * C, (ci + 1) * C)
            u = p_ref[rows, 0:GMW].astype(F32)
            v = p_ref[rows, GMW:2 * GMW].astype(F32)
            au, _, _, _, _, _, mixed, _ = _gm_chunk_fwd(u, v, lng_ref[...], lnb_ref[...], ws_ref, bias_ref[...])
            y_ref[rows, :] = (au * mixed).astype(BF16)

    return pl.pallas_call(
        body, grid=(t // tm,), name=name,
        in_specs=[pl.BlockSpec((tm, 2 * GMW), lambda i: (i, 0)),
                  pl.BlockSpec((None, 1, GMW), lambda i: (l, 0, 0)),
                  pl.BlockSpec((None, 1, GMW), lambda i: (l, 0, 0)),
                  pl.BlockSpec((None, HEADS, C, C), lambda i: (l, 0, 0, 0)),
                  pl.BlockSpec((None, C, GMW), lambda i: (l, 0, 0))],
        out_specs=pl.BlockSpec((tm, GMW), lambda i: (i, 0)),
        out_shape=S((t, GMW), BF16),
        compiler_params=_params("parallel"),
    )(proj, lng, lnb, ws_bf, bias)


def _scan_pair(proj, other, col, wf, wb, gcf, gcb, first_is_f, name):
    t = proj.shape[0]
    n = t // C
    sc = min(SCAN_CHUNKS, n)
    nsteps = n // sc
    other_is_proj = other is None

    def body(a1, o1, a2, o2, w1_ref, w2_ref, out1, out2, st1, st2):
        @pl.when(pl.program_id(0) == 0)
        def _():
            st1[...] = jnp.zeros_like(st1)
            st2[...] = jnp.zeros_like(st2)

        def one(a_ref, o_ref, w_ref, gc, st, out, order):
            for h in range(HEADS):
                sl = slice(h * DH, (h + 1) * DH)
                incs = {}
                for j in order:
                    rows = slice(j * C, (j + 1) * C)
                    aw = (a_ref[rows, sl].astype(F32) * w_ref[h]).astype(BF16)
                    incs[j] = _dot_tn(aw, o_ref[rows, sl].astype(BF16))
                cur = st[h]
                for j in order:
                    out[j, h] = cur.astype(BF16)
                    cur = gc[h] * cur + incs[j]
                st[h] = cur

        g1, g2 = (gcf, gcb) if first_is_f else (gcb, gcf)
        one(a1, o1, w1_ref, g1, st1, out1, range(sc))
        one(a2, o2, w2_ref, g2, st2, out2, range(sc - 1, -1, -1))

    up = lambda i: i
    down = lambda i: nsteps - 1 - i

    def specs(ix):
        o_spec = pl.BlockSpec((sc * C, RETW), lambda i: (ix(i), V_BLK if other_is_proj else 0))
        return [pl.BlockSpec((sc * C, RETW), lambda i: (ix(i), col)), o_spec]

    const = lambda: pl.BlockSpec((HEADS, C, DH), lambda i: (0, 0, 0))
    oth = proj if other_is_proj else other
    w1, w2 = (wf, wb) if first_is_f else (wb, wf)
    out1, out2 = pl.pallas_call(
        body, grid=(nsteps,), name=name,
        in_specs=specs(up) + specs(down) + [const(), const()],
        out_specs=[pl.BlockSpec((sc, HEADS, DH, DH), lambda i: (up(i), 0, 0, 0)),
                   pl.BlockSpec((sc, HEADS, DH, DH), lambda i: (down(i), 0, 0, 0))],
        out_shape=[S((n, HEADS, DH, DH), BF16), S((n, HEADS, DH, DH), BF16)],
        scratch_shapes=[pltpu.VMEM((HEADS, DH, DH), F32), pltpu.VMEM((HEADS, DH, DH), F32)],
        compiler_params=_params("arbitrary"),
    )(proj, oth, proj, oth, w1, w2)
    return (out1, out2) if first_is_f else (out2, out1)


SCAN_CHUNKS = 8


RET_CHUNKS = 8


def _f_ret_out(proj, s_f, s_b, rc, name):
    t = proj.shape[0]
    nchunks = min(RET_CHUNKS, t // C)
    tm = nchunks * C

    def body(q_ref, k_ref, v_ref, sf_ref, sb_ref, d_ref, xif_ref, xib_ref, o_ref):
        for ci in range(nchunks):
            rows = slice(ci * C, (ci + 1) * C)
            for h in range(HEADS):
                sl = slice(h * DH, (h + 1) * DH)
                qh, kh, vh = q_ref[rows, sl], k_ref[rows, sl], v_ref[rows, sl]
                p = (_dot_nt(qh, kh) * d_ref[h]).astype(BF16)
                cross = _dot(qh, jnp.concatenate([sf_ref[ci, h], sb_ref[ci, h]], axis=1))
                o_ref[rows, sl] = _dot(p, vh) + xif_ref[h] * cross[:, 0:DH] + xib_ref[h] * cross[:, DH:2 * DH]

    const = lambda: pl.BlockSpec((HEADS, C, DH), lambda i: (0, 0, 0))
    state = lambda: pl.BlockSpec((nchunks, HEADS, DH, DH), lambda i: (i, 0, 0, 0))
    return pl.pallas_call(
        body, grid=(t // tm,), name=name,
        in_specs=[pl.BlockSpec((tm, RETW), lambda i, cb=cb: (i, cb)) for cb in (Q_BLK, K_BLK, V_BLK)] +
                 [state(), state(), const(), const(), const()],
        out_specs=pl.BlockSpec((tm, RETW), lambda i: (i, 0)),
        out_shape=S((t, RETW), F32),
        compiler_params=_params("parallel"),
    )(proj, proj, proj, s_f, s_b, rc["f"]["D"] + rc["b"]["D"], rc["f"]["XI"], rc["b"]["XI"])


def _conv_halo_specs(t, tm, width, col):
    r = tm // HALO
    last = t // HALO - 1
    return [pl.BlockSpec((HALO, width), lambda i: (jnp.maximum(i * r - 1, 0), col)),
            pl.BlockSpec((tm, width), lambda i: (i, col)),
            pl.BlockSpec((HALO, width), lambda i: (jnp.minimum((i + 1) * r, last), col))]


def _fill_ext(ext, prev, cur, nxt, i, nt, tm):
    ext[0:HALO, :] = jnp.where(i > 0, prev, 0.0)
    ext[HALO:HALO + tm, :] = cur
    ext[HALO + tm:2 * HALO + tm, :] = jnp.where(i < nt - 1, nxt, 0.0)


def _glu(a_ref, g_ref):
    return a_ref[...].astype(F32) * _sigmoid(g_ref[...].astype(F32))


def _shifted_copies(ext, rot, tm):
    rows = tm + 2 * HALO - 8
    for b in range(1, 8):
        rot[b - 1, :, :] = ext[pl.ds(b, rows), :]


def _tap(ext, rot, r0, s, rb):
    a, b = divmod(s, 8)
    return ext[pl.ds(r0 + 8 * a, rb), :] if b == 0 else rot[b - 1, pl.ds(r0 + 8 * a, rb), :]


def _f_conv(proj, cw, cb, lng, lnb, l, name, comm=None):
    t = proj.shape[0]
    tm = 512
    nt = t // tm
    rb = 64

    def body(ap, ac, an, gp, gc, gn, cw_ref, cb_ref, lng_ref, lnb_ref, c_ref, y_ref, hext, hrot):
        i = pl.program_id(0)
        _fill_ext(hext, _glu(ap, gp), _glu(ac, gc), _glu(an, gn), i, nt, tm)
        _shifted_copies(hext, hrot, tm)
        for r0 in range(0, tm, rb):
            acc = jnp.zeros((rb, CVW), F32) + cb_ref[...]
            for j in range(KW):
                acc = acc + cw_ref[j:j + 1, :] * _tap(hext, hrot, r0, j + 1, rb)
            c_ref[r0:r0 + rb, :] = acc
            chat, _ = _standardize(acc)
            z = chat * lng_ref[...] + lnb_ref[...]
            y_ref[r0:r0 + rb, :] = (z * _sigmoid(z)).astype(BF16)

    vec = lambda: pl.BlockSpec((None, 1, CVW), lambda i: (l, 0, 0))
    return _launch(
        body, (nt,),
        _conv_halo_specs(t, tm, CVW, CONV_A_BLK) + _conv_halo_specs(t, tm, CVW, CONV_GATE_BLK) +
        [pl.BlockSpec((None, 32, CVW), lambda i: (l, 0, 0)), vec(), vec(), vec()],
        [pl.BlockSpec((tm, CVW), lambda i: (i, 0)), pl.BlockSpec((tm, CVW), lambda i: (i, 0))],
        [S((t, CVW), F32), S((t, CVW), BF16)],
        [pltpu.VMEM((tm + 2 * HALO, CVW), F32), pltpu.VMEM((7, tm + 2 * HALO - 8, CVW), F32)],
        (proj, proj, proj, proj, proj, proj, cw, cb, lng, lnb), name, ("parallel",), comm)


def _f_mixout(x, y_gm, y_cv, o, proj, w, name, comm=None):
    t = x.shape[0]
    tm = 512

    def body(x_ref, ygm_ref, ycv_ref, o_ref, g_ref, w_ref, xm_ref, ycat_t_ref, ycat):
        ycat[:, 0:GMW] = ygm_ref[...]
        ycat[:, GMW + RETW:D] = ycv_ref[...]
        for h in range(HEADS):
            sl = slice(h * DH, (h + 1) * DH)
            ohat, _ = _standardize(o_ref[:, sl])
            g = g_ref[:, sl].astype(F32)
            ycat[:, GMW + h * DH:GMW + (h + 1) * DH] = (ohat * (g * _sigmoid(g))).astype(BF16)
        yc = ycat[...]
        ycat_t_ref[...] = yc.T
        xm_ref[...] = x_ref[...] + _dot(yc, w_ref[...])

    return _launch(
        body, (t // tm,),
        [pl.BlockSpec((tm, D), lambda i: (i, 0)),
         pl.BlockSpec((tm, GMW), lambda i: (i, 0)),
         pl.BlockSpec((tm, CVW), lambda i: (i, 0)),
         pl.BlockSpec((tm, RETW), lambda i: (i, 0)),
         pl.BlockSpec((tm, RETW), lambda i: (i, G_BLK)),
         _resident((D, D), lambda i: (0, 0))],
        [pl.BlockSpec((tm, D), lambda i: (i, 0)), pl.BlockSpec((D, tm), lambda i: (0, i))],
        [S((t, D), F32), S((D, t), BF16)],
        [pltpu.VMEM((tm, D), BF16)], (x, y_gm, y_cv, o, proj, w), name, ("parallel",), comm)


def _f_ffn(xm, g2, w1, w2, l, name, comm=None):
    t = xm.shape[0]
    tm = 512

    def body(x_ref, g_ref, w1_ref, w2_ref, xo_ref, h_ref, gu_ref, act_t_ref):
        xv = x_ref[...]
        h = (xv * _rms(xv) * g_ref[...]).astype(BF16)
        h_ref[...] = h
        acc = xv
        for a, b in FF_CHUNKS:
            gate = _dot_nt(h, w1_ref[a:b, :])
            up = _dot_nt(h, w1_ref[FFH + a:FFH + b, :])
            gu_ref[:, a:b] = gate.astype(BF16)
            gu_ref[:, FFH + a:FFH + b] = up.astype(BF16)
            av = ((gate * _sigmoid(gate)) * up).astype(BF16)
            act_t_ref[a:b, :] = av.T
            acc = acc + _dot(av, w2_ref[a:b, :])
        xo_ref[...] = acc

    return _launch(
        body, (t // tm,),
        [pl.BlockSpec((tm, D), lambda i: (i, 0)),
         pl.BlockSpec((None, 1, D), lambda i: (l, 0, 0)),
         _resident((2 * FFH, D), lambda i: (0, 0)),
         _resident((FFH, D), lambda i: (0, 0))],
        [pl.BlockSpec((tm, D), lambda i: (i, 0)), pl.BlockSpec((tm, D), lambda i: (i, 0)),
         pl.BlockSpec((tm, 2 * FFH), lambda i: (i, 0)), pl.BlockSpec((FFH, tm), lambda i: (0, i))],
        [S((t, D), F32), S((t, D), BF16), S((t, 2 * FFH), BF16), S((FFH, t), BF16)],
        [], (xm, g2, w1, w2), name, ("parallel",), comm)


def _b_ffn(top, xm, g2, gu, w1, w2, l, name, comm=None):
    t = xm.shape[0]
    tm = 256
    from_loss = isinstance(top, tuple)
    n_top = 3 if from_loss else 1

    def body(*refs):
        top_refs = refs[:n_top]
        x_ref, g_ref, gu_ref, w1_ref, w2_ref, dgu_ref, dxm_ref, dxb_ref, dg_ref = refs[n_top:n_top + 9]
        first = pl.program_id(0) == 0

        @pl.when(first)
        def _():
            dg_ref[...] = jnp.zeros_like(dg_ref)

        if from_loss:
            xo_ref, fg_ref, t_ref = top_refs
            loss_ref, dfg_ref = refs[n_top + 9:n_top + 11]

            @pl.when(first)
            def _():
                loss_ref[...] = jnp.zeros_like(loss_ref)
                dfg_ref[...] = jnp.zeros_like(dfg_ref)

            xo = xo_ref[...]
            ro = _rms(xo)
            xr = xo * ro
            err = xr * fg_ref[...] - t_ref[...]
            loss_ref[...] += (0.5 / D) * _col_sum(jnp.sum(err * err, axis=1, keepdims=True))
            dy = err * (1.0 / D)
            dfg_ref[...] += _col_sum(dy * xr)
            dxo = _rmsnorm_bwd(dy, xo, ro, fg_ref[...])
        else:
            dxo = top_refs[0][...]
        dxb = dxo.astype(BF16)
        dxb_ref[...] = dxb
        dh = jnp.zeros((tm, D), F32)
        for a, b in FF_CHUNKS:
            dact = _dot_nt(dxb, w2_ref[a:b, :])
            gate = gu_ref[:, a:b].astype(F32)
            up = gu_ref[:, FFH + a:FFH + b].astype(F32)
            sg, dsg = _silu_and_grad(gate)
            dgate = (dact * up * dsg).astype(BF16)
            dup = (dact * sg).astype(BF16)
            dgu_ref[a:b, :] = dgate.T
            dgu_ref[FFH + a:FFH + b, :] = dup.T
            dh = dh + _dot(dgate, w1_ref[a:b, :]) + _dot(dup, w1_ref[FFH + a:FFH + b, :])
        xv = x_ref[...]
        r = _rms(xv)
        dg_ref[...] += _col_sum(dh * xv * r)
        dxm_ref[...] = dxo + _rmsnorm_bwd(dh, xv, r, g_ref[...])

    tok = lambda: pl.BlockSpec((tm, D), lambda i: (i, 0))
    vec = lambda: pl.BlockSpec((1, D), lambda i: (0, 0))
    top_specs = [tok(), vec(), tok()] if from_loss else [tok()]
    extra_specs = [pl.BlockSpec((1, 1), lambda i: (0, 0)), vec()] if from_loss else []
    extra_shape = [S((1, 1), F32), S((1, D), F32)] if from_loss else []
    return _launch(
        body, (t // tm,),
        top_specs + [tok(), pl.BlockSpec((None, 1, D), lambda i: (l, 0, 0)),
                     pl.BlockSpec((tm, 2 * FFH), lambda i: (i, 0)),
                     _resident((2 * FFH, D), lambda i: (0, 0)),
                     _resident((FFH, D), lambda i: (0, 0))],
        [pl.BlockSpec((2 * FFH, tm), lambda i: (0, i)), tok(), tok(), vec()] + extra_specs,
        [S((2 * FFH, t), BF16), S((t, D), F32), S((t, D), BF16), S((1, D), F32)] + extra_shape,
        [],
        ((*top,) if from_loss else (top,)) + (xm, g2, gu, w1, w2), name, ("arbitrary",), comm)


def _mm_wgrad(at, b, pieces, at_rows, b_mode, group, name):
    bt = 2048
    t = at.shape[-1]
    bt = min(bt, t)
    nt = t // bt
    if at_rows:
        ka = at.shape[0] // pieces
        a_spec = pl.BlockSpec((ka, bt), lambda j, tt: (j, tt))
    else:
        ka = at.shape[0]
        a_spec = pl.BlockSpec((ka, bt), lambda j, tt: (0, tt))
    if b_mode == "shared":
        nb, b_spec = b.shape[1], pl.BlockSpec((bt, b.shape[1]), lambda j, tt: (tt, 0))
    elif b_mode == "cols":
        nb = b.shape[1] // pieces
        b_spec = pl.BlockSpec((bt, group * nb), lambda j, tt: (tt, j))
    else:
        nb, b_spec = b.shape[2], pl.BlockSpec((None, bt, b.shape[2]), lambda j, tt: (j, tt, 0))
    assert group == 1 or b_mode == "cols"

    def body(a_ref, b_ref, o_ref, acc):
        tt = pl.program_id(1)

        @pl.when(tt == 0)
        def _():
            acc[...] = jnp.zeros_like(acc)

        acc[...] += _dot(a_ref[...], b_ref[...])

        @pl.when(tt == nt - 1)
        def _():
            for k in range(group):
                o_ref[k] = acc[:, k * nb:(k + 1) * nb].astype(BF16)

    return pl.pallas_call(
        body, grid=(pieces // group, nt), name=name,
        in_specs=[a_spec, b_spec],
        out_specs=pl.BlockSpec((group, ka, nb), lambda j, tt: (j, 0, 0)),
        out_shape=S((pieces, ka, nb), BF16),
        scratch_shapes=[pltpu.VMEM((ka, group * nb), F32)],
        compiler_params=_params("parallel", "arbitrary"),
    )(at, b)


def _b_mixout(dxm, w, o, proj, c, lng, lnb, l, name, comm=None):
    t = dxm.shape[0]
    tm = 256

    def body(dxm_ref, w_ref, o_ref, g_ref, c_ref, lng_ref, lnb_ref,
             dxb_ref, dygm_ref, dO_ref, dg_ref, dc_ref, dlg_ref, dlb_ref, dcb_ref):
        @pl.when(pl.program_id(0) == 0)
        def _():
            dlg_ref[...] = jnp.zeros_like(dlg_ref)
            dlb_ref[...] = jnp.zeros_like(dlb_ref)
            dcb_ref[...] = jnp.zeros_like(dcb_ref)

        dxb = dxm_ref[...].astype(BF16)
        dxb_ref[...] = dxb
        dy = _dot_nt(dxb, w_ref[...])
        dygm_ref[...] = dy[:, 0:GMW]
        for h in range(HEADS):
            sl = slice(h * DH, (h + 1) * DH)
            ohat, rstd = _standardize(o_ref[:, sl])
            sg, dsg = _silu_and_grad(g_ref[:, sl].astype(F32))
            dyr = dy[:, GMW + h * DH:GMW + (h + 1) * DH]
            dg_ref[:, sl] = (dyr * ohat * dsg).astype(BF16)
            dO_ref[:, sl] = _standardize_bwd(dyr * sg, ohat, rstd)
        chat, rstd = _standardize(c_ref[...])
        z = chat * lng_ref[...] + lnb_ref[...]
        _, dsz = _silu_and_grad(z)
        dz = dy[:, GMW + RETW:D] * dsz
        dlg_ref[...] += _col_sum(dz * chat)
        dlb_ref[...] += _col_sum(dz)
        dc = _standardize_bwd(dz * lng_ref[...], chat, rstd)
        dcb_ref[...] += _col_sum(dc)
        dc_ref[...] = dc

    vec = lambda: pl.BlockSpec((None, 1, CVW), lambda i: (l, 0, 0))
    acc = lambda: pl.BlockSpec((1, CVW), lambda i: (0, 0))
    return _launch(
        body, (t // tm,),
        [pl.BlockSpec((tm, D), lambda i: (i, 0)),
         _resident((D, D), lambda i: (0, 0)),
         pl.BlockSpec((tm, RETW), lambda i: (i, 0)),
         pl.BlockSpec((tm, RETW), lambda i: (i, G_BLK)),
         pl.BlockSpec((tm, CVW), lambda i: (i, 0)), vec(), vec()],
        [pl.BlockSpec((tm, D), lambda i: (i, 0)), pl.BlockSpec((tm, GMW), lambda i: (i, 0)),
         pl.BlockSpec((tm, RETW), lambda i: (i, 0)), pl.BlockSpec((tm, RETW), lambda i: (i, 0)),
         pl.BlockSpec((tm, CVW), lambda i: (i, 0)), acc(), acc(), acc()],
        [S((t, D), BF16), S((t, GMW), F32), S((t, RETW), F32), S((t, RETW), BF16), S((t, CVW), F32),
         S((1, CVW), F32), S((1, CVW), F32), S((1, CVW), F32)],
        [], (dxm, w, o, proj, c, lng, lnb), name, ("arbitrary",), comm)


def _b_gm(proj, dy, lng, lnb, ws_bf, wst_bf, bias, l, name, comm=None):
    t = proj.shape[0]
    tm = 512
    nt = t // tm

    def body(p_ref, dy_ref, lng_ref, lnb_ref, ws_ref, wst_ref, bias_ref,
             duv_ref, dws_ref, dbias_ref, dbs_ref, dlg_ref, dlb_ref):
        @pl.when(pl.program_id(0) == 0)
        def _():
            dws_ref[...] = jnp.zeros_like(dws_ref)
            dbias_ref[...] = jnp.zeros_like(dbias_ref)
            dbs_ref[...] = jnp.zeros_like(dbs_ref)
            dlg_ref[...] = jnp.zeros_like(dlg_ref)
            dlb_ref[...] = jnp.zeros_like(dlb_ref)

        for ci in range(tm // C):
            rows = slice(ci * C, (ci + 1) * C)
            u = p_ref[rows, 0:GMW].astype(F32)
            v = p_ref[rows, GMW:2 * GMW].astype(F32)
            au, dau, dav, vhat, rstd, vn, mixed, head = _gm_chunk_fwd(u, v, lng_ref[...], lnb_ref[...], ws_ref, bias_ref[...])
            dyc = dy_ref[rows, :]
            dmixed = dyc * au
            dmb = dmixed.astype(BF16)
            dbias_ref[...] += dmixed
            dvn = jnp.zeros((C, GMW), F32)
            for h in range(HEADS):
                dws_ref[h] += _dot_nt(jnp.where(head == h, dmixed, 0.0).astype(BF16), vn)
                dvn = dvn + jnp.where(head == h, _dot(wst_ref[h], dmb), 0.0)
            dlg_ref[...] += _col_sum(dvn * vhat)
            dlb_ref[...] += _col_sum(dvn)
            dav_in = _standardize_bwd(dvn * lng_ref[...], vhat, rstd)
            duv_ref[rows, 0:GMW] = (dyc * mixed * dau).astype(BF16)
            duv_ref[rows, GMW:2 * GMW] = (dav_in * dav).astype(BF16)

        @pl.when(pl.program_id(0) == nt - 1)
        def _():
            head = lax.broadcasted_iota(jnp.int32, (C, GMW), 1) // (GMW // HEADS)
            lane = lax.broadcasted_iota(jnp.int32, (C, 128), 1)
            fold = jnp.zeros((C, 128), F32)
            for h in range(HEADS):
                col = jnp.sum(jnp.where(head == h, dbias_ref[...], 0.0), axis=1, keepdims=True)
                fold = jnp.where(lane == h, col, fold)
            dbs_ref[...] = fold

    vec = lambda: pl.BlockSpec((None, 1, GMW), lambda i: (l, 0, 0))
    mats = lambda: pl.BlockSpec((None, HEADS, C, C), lambda i: (l, 0, 0, 0))
    return _launch(
        body, (nt,),
        [pl.BlockSpec((tm, 2 * GMW), lambda i: (i, 0)), pl.BlockSpec((tm, GMW), lambda i: (i, 0)),
         vec(), vec(), mats(), mats(), pl.BlockSpec((None, C, GMW), lambda i: (l, 0, 0))],
        [pl.BlockSpec((tm, 2 * GMW), lambda i: (i, 0)),
         pl.BlockSpec((HEADS, C, C), lambda i: (0, 0, 0)),
         pl.BlockSpec((C, GMW), lambda i: (0, 0)), pl.BlockSpec((C, 128), lambda i: (0, 0)),
         pl.BlockSpec((1, GMW), lambda i: (0, 0)), pl.BlockSpec((1, GMW), lambda i: (0, 0))],
        [S((t, 2 * GMW), BF16), S((HEADS, C, C), F32), S((C, GMW), F32), S((C, 128), F32),
         S((1, GMW), F32), S((1, GMW), F32)],
        [], (proj, dy, lng, lnb, ws_bf, wst_bf, bias), name, ("arbitrary",), comm)


def _b_conv(proj, dc, cw, l, name, comm=None):
    t = proj.shape[0]
    tm = 256
    nt = t // tm
    rb = 64

    def body(ap, ac, an, gp, gc, gn, dp, dcur, dn, cw_ref, dag_ref, dcw_ref, hext, dext, hrot, drot):
        i = pl.program_id(0)

        @pl.when(i == 0)
        def _():
            dcw_ref[...] = jnp.zeros_like(dcw_ref)

        _fill_ext(hext, _glu(ap, gp), _glu(ac, gc), _glu(an, gn), i, nt, tm)
        _fill_ext(dext, dp[...], dcur[...], dn[...], i, nt, tm)
        _shifted_copies(hext, hrot, tm)
        _shifted_copies(dext, drot, tm)
        for j in range(KW):
            dcw_ref[j:j + 1, :] += _col_sum(dcur[...] * _tap(hext, hrot, 0, j + 1, tm))
        for r0 in range(0, tm, rb):
            dh = jnp.zeros((rb, CVW), F32)
            for j in range(KW):
                dh = dh + cw_ref[j:j + 1, :] * _tap(dext, drot, r0, 2 * HALO - 1 - j, rb)
            a = ac[r0:r0 + rb, :].astype(F32)
            s = _sigmoid(gc[r0:r0 + rb, :].astype(F32))
            dag_ref[r0:r0 + rb, 0:CVW] = (dh * s).astype(BF16)
            dag_ref[r0:r0 + rb, CVW:2 * CVW] = (dh * a * s * (1.0 - s)).astype(BF16)

    dspecs = _conv_halo_specs(t, tm, CVW, 0)
    return _launch(
        body, (nt,),
        _conv_halo_specs(t, tm, CVW, CONV_A_BLK) + _conv_halo_specs(t, tm, CVW, CONV_GATE_BLK) + dspecs +
        [pl.BlockSpec((None, 32, CVW), lambda i: (l, 0, 0))],
        [pl.BlockSpec((tm, 2 * CVW), lambda i: (i, 0)), pl.BlockSpec((32, CVW), lambda i: (0, 0))],
        [S((t, 2 * CVW), BF16), S((32, CVW), F32)],
        [pltpu.VMEM((tm + 2 * HALO, CVW), F32), pltpu.VMEM((tm + 2 * HALO, CVW), F32),
         pltpu.VMEM((7, tm + 2 * HALO - 8, CVW), F32), pltpu.VMEM((7, tm + 2 * HALO - 8, CVW), F32)],
        (proj, proj, proj, proj, proj, proj, dc, dc, dc, cw), name, ("arbitrary",), comm)


def _b_ret_out(proj, cos2, sin2, dO, s_f, s_b, g_f, g_b, rc, name, comm=None):
    t = proj.shape[0]
    nchunks = min(RET_CHUNKS, t // C)
    tm = nchunks * C

    def body(q_ref, k_ref, v_ref, cos_ref, sin_ref, dO_ref, sf_ref, sb_ref, gf_ref, gb_ref,
             d_ref, xif_ref, xib_ref, zef_ref, zeb_ref, dq_ref, dk_ref, dv_ref):
        for ci in range(nchunks):
            rows = slice(ci * C, (ci + 1) * C)
            cos_v, sin_v = cos_ref[rows, :], sin_ref[rows, :]
            for h in range(HEADS):
                sl = slice(h * DH, (h + 1) * DH)
                qh, kh, vh = q_ref[rows, sl], k_ref[rows, sl], v_ref[rows, sl]
                dOh = dO_ref[rows, sl].astype(BF16)
                dm = d_ref[h]
                p = (_dot_nt(qh, kh) * dm).astype(BF16)
                dp = (_dot_nt(dOh, vh) * dm).astype(BF16)
                from_s = _dot_nt(dOh, jnp.concatenate([sf_ref[ci, h], sb_ref[ci, h]], axis=0))
                from_g = _dot_nt(vh, jnp.concatenate([gf_ref[ci, h], gb_ref[ci, h]], axis=0))
                kg = _dot(kh, jnp.concatenate([gf_ref[ci, h], gb_ref[ci, h]], axis=1))
                dqr = _dot(dp, kh) + xif_ref[h] * from_s[:, 0:DH] + xib_ref[h] * from_s[:, DH:2 * DH]
                dkr = (_dot_tn(dp, qh) + zef_ref[h] * from_g[:, 0:DH] + zeb_ref[h] * from_g[:, DH:2 * DH]) * SCALE
                dv = _dot_tn(p, dOh) + zef_ref[h] * kg[:, 0:DH] + zeb_ref[h] * kg[:, DH:2 * DH]
                dq_ref[rows, sl] = _rot_t(dqr, cos_v, sin_v).astype(BF16)
                dk_ref[rows, sl] = _rot_t(dkr, cos_v, sin_v).astype(BF16)
                dv_ref[rows, sl] = dv.astype(BF16)

    const = lambda: pl.BlockSpec((HEADS, C, DH), lambda i: (0, 0, 0))
    state = lambda: pl.BlockSpec((nchunks, HEADS, DH, DH), lambda i: (i, 0, 0, 0))
    tok = lambda: pl.BlockSpec((tm, RETW), lambda i: (i, 0))
    return _launch(
        body, (t // tm,),
        [pl.BlockSpec((tm, RETW), lambda i, cb=cb: (i, cb)) for cb in (Q_BLK, K_BLK, V_BLK)] +
        [pl.BlockSpec((tm, DH), lambda i: (i, 0)), pl.BlockSpec((tm, DH), lambda i: (i, 0)), tok(),
         state(), state(), state(), state()] + [const() for _ in range(5)],
        [tok(), tok(), tok()],
        [S((t, RETW), BF16) for _ in range(3)], [],
        (proj, proj, proj, cos2, sin2, dO, s_f, s_b, g_f, g_b, rc["f"]["D"] + rc["b"]["D"],
         rc["f"]["XI"], rc["b"]["XI"], rc["f"]["ZETA"], rc["b"]["ZETA"]), name, ("parallel",), comm)


def _b_inproj(d_uv, dqkv, d_g, d_ag, w, x, g1, dxm, l, name, comm=None):
    t = x.shape[0]
    tm = 512

    def body(duv_ref, dq_ref, dk_ref, dv_ref, dg_ref, dag_ref, w_ref, x_ref, g_ref, dxm_ref,
             dp_ref, dx_ref, dn_ref):
        @pl.when(pl.program_id(0) == 0)
        def _():
            dn_ref[...] = jnp.zeros_like(dn_ref)

        for k, part in enumerate((duv_ref, dq_ref, dk_ref, dv_ref, dg_ref, dag_ref)):
            dp_ref[:, 512 * k:512 * (k + 1)] = part[...]
        dh = _dot_nt(dp_ref[...], w_ref[...])
        xv = x_ref[...]
        r = _rms(xv)
        dn_ref[...] += _col_sum(dh * xv * r)
        dx_ref[...] = dxm_ref[...] + _rmsnorm_bwd(dh, xv, r, g_ref[...])

    half = lambda: pl.BlockSpec((tm, 512), lambda i: (i, 0))
    full = lambda: pl.BlockSpec((tm, D), lambda i: (i, 0))
    return _launch(
        body, (t // tm,),
        [half() for _ in range(6)] +
        [_resident((D, INW), lambda i: (0, 0)), full(), pl.BlockSpec((None, 1, D), lambda i: (l, 0, 0)), full()],
        [pl.BlockSpec((tm, INW), lambda i: (i, 0)), full(), pl.BlockSpec((1, D), lambda i: (0, 0))],
        [S((t, INW), BF16), S((t, D), F32), S((1, D), F32)], [],
        (d_uv, *dqkv, d_g, d_ag, w, x, g1, dxm), name, ("arbitrary",), comm)


class _PairSwap:
    def __init__(self, units):
        self.inputs = list(units)
        self.out_shape = [S((4,) + u.shape[1:], BF16) for u in units]
        n = len(units)
        self.scratch = [pltpu.SemaphoreType.DMA((n, 4)), pltpu.SemaphoreType.DMA((n, 4))]

    def run(self, phase, ins, outs, scr):
        ssem, rsem = scr
        x, y, c, _ = _place()
        copies = lambda: [pltpu.make_async_remote_copy(src_ref=ins[u].at[2 * chip + (1 - c)], dst_ref=outs[u].at[chip],
                                                       send_sem=ssem.at[u, chip], recv_sem=rsem.at[u, chip],
                                                       device_id=(x, y, 1 - c), device_id_type=MESH)
                          for u in range(len(self.inputs)) for chip in range(4)]
        if phase == "start":
            for cp in copies():
                cp.start()
        elif phase == "finish":
            for cp in copies():
                cp.wait_recv()
            for cp in copies():
                cp.wait_send()


def _pair_add(g, q, name):
    _, mm, nn = g.shape
    bm = _row_block(mm)

    def body(c_ref, g_ref, q_ref, h_ref):
        h_ref[...] = (g_ref[...].astype(F32) + q_ref[...].astype(F32)).astype(BF16)

    blk = lambda: pl.BlockSpec((None, bm, nn), lambda qq, i, c_ref: (qq, i, 0))
    return pl.pallas_call(
        body, name=name,
        grid_spec=pltpu.PrefetchScalarGridSpec(
            num_scalar_prefetch=1, grid=(4, mm // bm),
            in_specs=[pl.BlockSpec((None, None, bm, nn), lambda qq, i, c_ref: (qq, c_ref[0], i, 0)), blk()],
            out_specs=blk()),
        out_shape=S((4, mm, nn), BF16),
        compiler_params=_params("parallel", "parallel"),
    )(lax.axis_index("c").astype(jnp.int32).reshape(1), g.reshape(4, 2, mm, nn), q)


_BIG = ("w_in", "w_out", "w_ffn_in", "w_ffn_out")
_KIND = dict(w_in="cols", w_out="rows", w_ffn_in="lead", w_ffn_out="rows")
CWP = 128
EARLY_ROWS, LATE_ROWS = 152, 8


def _step(x, tgt, wts, sh, cw_pad):
    t = x.shape[0]
    rc = _ret_consts()
    cos2, sin2 = (jnp.asarray(a) for a in _rope_tables(t))
    n1 = wts["norm1_g"].reshape(LAYERS, 1, D)
    n2 = wts["norm2_g"].reshape(LAYERS, 1, D)
    gm_lng = wts["gm_ln_g"].reshape(LAYERS, 1, GMW)
    gm_lnb = wts["gm_ln_b"].reshape(LAYERS, 1, GMW)
    ws_bf = wts["gm_ws"].astype(BF16)
    wst_bf = jnp.swapaxes(wts["gm_ws"], 2, 3).astype(BF16)
    bias = jnp.repeat(jnp.swapaxes(wts["gm_bs"], 1, 2), GMW // HEADS, axis=2)
    cb = wts["conv_b"].reshape(LAYERS, 1, CVW)
    cv_lng = wts["conv_ln_g"].reshape(LAYERS, 1, CVW)
    cv_lnb = wts["conv_ln_b"].reshape(LAYERS, 1, CVW)
    unit = lambda f, l: (sh[f][l], _KIND[f])
    cshard = CVW // NDEV

    full = {f: [None] * LAYERS for f in _BIG}
    full["w_in"][0], cw_all = _comm_only(_Gather([unit("w_in", 0), (cw_pad, "lead")]), "gather_first")
    cw = jnp.transpose(cw_all[:, :, :, :cshard], (1, 2, 0, 3)).reshape(LAYERS, 32, CVW)

    gcf, gcb = rc["f"]["gC"], rc["b"]["gC"]
    saved = []
    for l in range(LAYERS):
        first = l == 0
        (proj, h1), got = _f_inproj(x, n1, full["w_in"][l], cos2, sin2, l, f"f_inproj_{l}",
                                    _Gather([unit("w_ffn_in", 0)]) if first else None)
        if first:
            full["w_ffn_in"][0], = got
        y_gm = _f_gm(proj, gm_lng, gm_lnb, ws_bf, bias, l, f"f_gm_{l}")
        s_f, s_b = _scan_pair(proj, None, 2, rc["f"]["ZETA"], rc["b"]["ZETA"], gcf, gcb, True, f"f_ret_state_{l}")
        o = _f_ret_out(proj, s_f, s_b, rc, f"f_ret_out_{l}")
        (c, y_cv), got = _f_conv(proj, cw, cb, cv_lng, cv_lnb, l, f"f_conv_{l}",
                                 _Gather([unit("w_out", 0)]) if first else None)
        if first:
            full["w_out"][0], = got
        (xm, ycat), got = _f_mixout(x, y_gm, y_cv, o, proj, full["w_out"][l], f"f_mixout_{l}",
                                    _Gather([unit("w_ffn_out", 0)]) if first else None)
        if first:
            full["w_ffn_out"][0], = got
        w1t = full["w_ffn_in"][l].reshape(2 * FFH, D)
        (xo, h2, gu, act), got = _f_ffn(xm, n2, w1t, full["w_ffn_out"][l], l, f"f_ffn_{l}",
                                        _Gather([unit(f, 1) for f in _BIG]) if first else None)
        if first:
            full["w_in"][1], full["w_out"][1], full["w_ffn_in"][1], full["w_ffn_out"][1] = got
        saved.append(dict(x=x, proj=proj, h1=h1, o=o, s_f=s_f, s_b=s_b, c=c, xm=xm, ycat=ycat, h2=h2, gu=gu, act=act))
        x = xo

    parts = {f: [None] * LAYERS for f in _BIG}
    small = [None] * LAYERS
    norm1 = [None] * LAYERS
    upper = None
    top = (x, wts["final_g"].reshape(1, D), tgt)
    for l in reversed(range(LAYERS)):
        sv = saved[l]
        outs, got = _b_ffn(top, sv["xm"], n2, sv["gu"], full["w_ffn_in"][l].reshape(2 * FFH, D), full["w_ffn_out"][l], l,
                           f"b_ffn_{l}", _Scatter(upper) if upper else None)
        dgu_t, dxm, dxo_bf, d_n2 = outs[:4]
        if l == LAYERS - 1:
            loss, d_final = outs[4:]
        if upper:
            for f, p in zip(_BIG, got):
                parts[f][l + 1] = p
        g_f2 = _mm_wgrad(sv["act"], dxo_bf, NDEV // 2, True, "shared", 1, f"g_ffn_out_{l}").reshape(NDEV, FFH // NDEV, D)
        g_f1 = _mm_wgrad(dgu_t, sv["h2"], NDEV, True, "shared", 1, f"g_ffn_in_{l}")
        (dxm_bf, dy_gm, dO, d_g, dc, d_cvlg, d_cvlb, d_cb), (q_f1, q_f2) = _b_mixout(
            dxm, full["w_out"][l], sv["o"], sv["proj"], sv["c"], cv_lng, cv_lnb, l, f"b_mixout_{l}", _PairSwap([g_f1, g_f2]))
        h_f1 = _pair_add(g_f1, q_f1, f"pair_add_w_ffn_in_{l}")
        h_f2 = _pair_add(g_f2, q_f2, f"pair_add_w_ffn_out_{l}")
        g_out = _mm_wgrad(sv["ycat"], dxm_bf, 1, False, "shared", 1, f"g_out_{l}").reshape(NDEV, D // NDEV, D)
        last = l == 0
        (d_uv, d_ws, _, d_bs_fold, d_gmlg, d_gmlb), (q_out,) = _b_gm(
            sv["proj"], dy_gm, gm_lng, gm_lnb, ws_bf, wst_bf, bias, l, f"b_gm_{l}", _PairSwap([g_out]))
        h_out = _pair_add(g_out, q_out, f"pair_add_w_out_{l}")
        (d_ag, d_cw), got = _b_conv(sv["proj"], dc, cw, l, f"b_conv_{l}", _Scatter([h_f1]) if last else None)
        if last:
            parts["w_ffn_in"][l], = got
        small[l] = dict(gm_ln_g=d_gmlg[0], gm_ln_b=d_gmlb[0], gm_ws=d_ws, gm_bs=d_bs_fold[:, :HEADS].T, conv_w=d_cw[:KW],
                        conv_b=d_cb[0], conv_ln_g=d_cvlg[0], conv_ln_b=d_cvlb[0], norm2_g=d_n2[0])
        comm = None
        if last:
            early_g = {k: jnp.stack([small[ll][k] for ll in range(LAYERS)]) for k in small[0]}
            early_g["final_g"] = d_final[0]
            early_buf = _pack([early_g[k] for k in _SMALL_EARLY] + [loss], EARLY_ROWS)
            comm = _Comms([_Scatter([h_out, h_f2]), _Gather([(early_buf, "lead")])])
        g_f, g_b = _scan_pair(sv["proj"], dO, 1, rc["f"]["XI"], rc["b"]["XI"], gcf, gcb, False, f"b_ret_state_{l}")
        dqkv, got = _b_ret_out(sv["proj"], cos2, sin2, dO, sv["s_f"], sv["s_b"], g_f, g_b, rc, f"b_ret_out_{l}", comm)
        if last:
            parts["w_out"][l], parts["w_ffn_out"][l], early_parts = got
        (dproj, top, d_n1), _ = _b_inproj(d_uv, dqkv, d_g, d_ag, full["w_in"][l], sv["x"], n1, dxm, l, f"b_inproj_{l}")
        norm1[l] = d_n1[0]
        g_in = _mm_wgrad(sv["h1"], dproj, NDEV, False, "cols", 2, f"g_in_{l}")
        q_in, = _comm_only(_PairSwap([g_in]), f"pair_swap_w_in_{l}")
        h_in = _pair_add(g_in, q_in, f"pair_add_w_in_{l}")
        if last:
            tail = [h_in]
        else:
            upper = [h_in, h_out, h_f1, h_f2]
    late_buf = _pack([jnp.stack(norm1)], LATE_ROWS)
    parts["w_in"][0], late_parts = _comm_only(_Comms([_Scatter(tail), _Gather([(late_buf, "lead")])]), "exchange_last")
    return loss, top, parts, (early_parts, late_parts)


def _adamw(w, g, m, v):
    m = ADAM_B1 * m + (1.0 - ADAM_B1) * g
    v = ADAM_B2 * v + (1.0 - ADAM_B2) * (g * g)
    m_hat = m / (1.0 - ADAM_B1 ** ADAM_STEP)
    v_hat = v / (1.0 - ADAM_B2 ** ADAM_STEP)
    return -ADAM_LR * (m_hat / (jnp.sqrt(v_hat) + ADAM_EPS) + ADAM_WD * w), m, v


def _cast_blocks(ws):
    def body(*refs):
        ins, outs = refs[:len(ws)], refs[len(ws):]
        for k, src in enumerate(ins):
            for l in range(LAYERS):
                outs[k * LAYERS + l][...] = src[l].astype(BF16)

    outs = pl.pallas_call(body, name="cast_blocks", out_shape=[S(w.shape[1:], BF16) for w in ws for _ in range(LAYERS)],
                          compiler_params=_params())(*ws)
    return [list(outs[k * LAYERS:(k + 1) * LAYERS]) for k in range(len(ws))]


def _row_block(mm):
    return next(b for b in (256, 352, 128) if mm % b == 0)


def _sum_adam(parts, w, m, v, l, prev, name):
    _, mm, nn = parts.shape
    bm = _row_block(mm)

    def body(p_ref, w_ref, m_ref, v_ref, *rest):
        g_ref, d_ref, nm_ref, nv_ref = rest[-4:]
        g = p_ref[0].astype(F32)
        for s in range(1, 4):
            g = g + p_ref[s].astype(F32)
        g_ref[...] = g
        d_ref[...], nm_ref[...], nv_ref[...] = _adamw(w_ref[...], g, m_ref[...], v_ref[...])

    blk = lambda: pl.BlockSpec((None, bm, nn), lambda i: (l, i, 0))
    prev = list(prev) if prev else []
    return pl.pallas_call(
        body, grid=(mm // bm,), name=name,
        in_specs=[pl.BlockSpec((4, bm, nn), lambda i: (0, i, 0)), blk(), blk(), blk()] + [_ANY] * len(prev),
        out_specs=[blk() for _ in range(4)],
        out_shape=[S(w.shape, F32) for _ in range(4)],
        input_output_aliases={4 + j: j for j in range(len(prev))},
        compiler_params=_params("parallel"),
    )(parts, w, m, v, *prev)


def _sum_small(parts):
    n = len(parts)

    def body(*refs):
        for p_ref, o_ref in zip(refs[:n], refs[n:]):
            g = p_ref[0]
            for s in range(1, NDEV):
                g = g + p_ref[s]
            o_ref[...] = g

    return pl.pallas_call(body, name="sum_small", out_shape=[S(p.shape[1:], F32) for p in parts],
                          compiler_params=_params())(*parts)


def _adam_small(quads):
    n = len(quads)
    as2d = lambda a: a.reshape(1, -1) if a.ndim == 1 else a

    def body(*refs):
        ins, outs = refs[:4 * n], refs[4 * n:]
        for k in range(n):
            g_ref, w_ref, m_ref, v_ref = ins[4 * k:4 * k + 4]
            outs[3 * k][...], outs[3 * k + 1][...], outs[3 * k + 2][...] = _adamw(w_ref[...], g_ref[...], m_ref[...], v_ref[...])

    flat = [as2d(a) for q in quads for a in q]
    outs = pl.pallas_call(body, name="adam_small", out_shape=[S(as2d(q[1]).shape, F32) for q in quads for _ in range(3)],
                          compiler_params=_params())(*flat)
    return [tuple(o.reshape(q[1].shape) for o in outs[3 * k:3 * k + 3]) for k, q in enumerate(quads)]


_SMALL = ("norm1_g", "gm_ln_g", "gm_ln_b", "gm_ws", "gm_bs", "conv_w", "conv_b", "conv_ln_g", "conv_ln_b",
          "norm2_g", "final_g")
_SMALL_EARLY = _SMALL[1:]
_NAMES = ("norm1_g", "w_in", "gm_ln_g", "gm_ln_b", "gm_ws", "gm_bs", "conv_w", "conv_b", "conv_ln_g", "conv_ln_b",
          "w_out", "norm2_g", "w_ffn_in", "w_ffn_out", "final_g")


def _pack(parts, rows):
    flat = jnp.concatenate([p.reshape(-1) for p in parts])
    return jnp.pad(flat, (0, rows * 1024 - flat.shape[0])).reshape(rows, 1024)


def _unpack(buf, shapes):
    flat = buf.reshape(-1)
    out, o = [], 0
    for shp in shapes:
        sz = int(np.prod(shp))
        out.append(flat[o:o + sz].reshape(shp))
        o += sz
    return out


def kernel(x, norm1_g, w_in, gm_ln_g, gm_ln_b, gm_ws, gm_bs, conv_w, conv_b, conv_ln_g, conv_ln_b, w_out, norm2_g, w_ffn_in, w_ffn_out, final_g, loss_target, m_norm1_g, m_w_in, m_gm_ln_g, m_gm_ln_b, m_gm_ws, m_gm_bs, m_conv_w, m_conv_b, m_conv_ln_g, m_conv_ln_b, m_w_out, m_norm2_g, m_w_ffn_in, m_w_ffn_out, m_final_g, v_norm1_g, v_w_in, v_gm_ln_g, v_gm_ln_b, v_gm_ws, v_gm_bs, v_conv_w, v_conv_b, v_conv_ln_g, v_conv_ln_b, v_w_out, v_norm2_g, v_w_ffn_in, v_w_ffn_out, v_final_g):
    w = dict(norm1_g=norm1_g, w_in=w_in, gm_ln_g=gm_ln_g, gm_ln_b=gm_ln_b, gm_ws=gm_ws, gm_bs=gm_bs, conv_w=conv_w,
             conv_b=conv_b, conv_ln_g=conv_ln_g, conv_ln_b=conv_ln_b, w_out=w_out, norm2_g=norm2_g, w_ffn_in=w_ffn_in,
             w_ffn_out=w_ffn_out, final_g=final_g)
    mo = dict(norm1_g=m_norm1_g, w_in=m_w_in, gm_ln_g=m_gm_ln_g, gm_ln_b=m_gm_ln_b, gm_ws=m_gm_ws, gm_bs=m_gm_bs,
              conv_w=m_conv_w, conv_b=m_conv_b, conv_ln_g=m_conv_ln_g, conv_ln_b=m_conv_ln_b, w_out=m_w_out,
              norm2_g=m_norm2_g, w_ffn_in=m_w_ffn_in, w_ffn_out=m_w_ffn_out, final_g=m_final_g)
    vo = dict(norm1_g=v_norm1_g, w_in=v_w_in, gm_ln_g=v_gm_ln_g, gm_ln_b=v_gm_ln_b, gm_ws=v_gm_ws, gm_bs=v_gm_bs,
              conv_w=v_conv_w, conv_b=v_conv_b, conv_ln_g=v_conv_ln_g, conv_ln_b=v_conv_ln_b, w_out=v_w_out,
              norm2_g=v_norm2_g, w_ffn_in=v_w_ffn_in, w_ffn_out=v_w_ffn_out, final_g=v_final_g)
    t = x.shape[1]
    me = 4 * lax.axis_index("x") + 2 * lax.axis_index("y") + lax.axis_index("c")
    cshard = conv_w.shape[2]

    cw_pad = jnp.pad(conv_w, ((0, 0), (0, 32 - KW), (0, CWP - cshard)))
    flip = lambda a: jnp.swapaxes(a, 1, 2)
    big = {f: tuple(flip(a[f]) if f == "w_ffn_in" else a[f] for a in (w, mo, vo)) for f in _BIG}
    sh = dict(zip(_BIG, _cast_blocks([big[f][0] for f in _BIG])))
    loss, dx, parts, small_parts = _step(x.reshape(t, D), loss_target.reshape(t, D), w, sh, cw_pad)

    grads, delta, new_m, new_v = {}, {}, {}, {}
    for f in _BIG:
        outs = None
        for l in reversed(range(LAYERS)):
            outs = _sum_adam(parts[f][l], *big[f], l, outs, f"sum_adam_{f}_{l}")
        grads[f], delta[f], new_m[f], new_v[f] = [flip(a) for a in outs] if f == "w_ffn_in" else outs

    early_sum, late_sum = _sum_small(small_parts)
    early_shapes = [(LAYERS, KW, CVW) if k == "conv_w" else w[k].shape for k in _SMALL_EARLY]
    grads["norm1_g"], = _unpack(late_sum, [w["norm1_g"].shape])
    *early, total = _unpack(early_sum, early_shapes + [()])
    for k, g in zip(_SMALL_EARLY, early):
        grads[k] = lax.dynamic_slice_in_dim(g, me * cshard, cshard, axis=2) if k == "conv_w" else g
    for k, (d, nm, nv) in zip(_SMALL, _adam_small([(grads[k], w[k], mo[k], vo[k]) for k in _SMALL])):
        delta[k], new_m[k], new_v[k] = d, nm, nv

    return (total, dx.reshape(1, t, D), *[grads[k] for k in _NAMES], *[delta[k] for k in _NAMES],
            *[new_m[k] for k in _NAMES], *[new_v[k] for k in _NAMES])
```

```python
import functools

import numpy as np
import jax
import jax.numpy as jnp
from jax import lax
from jax.experimental import pallas as pl
from jax.experimental.pallas import tpu as pltpu

F32, BF16 = jnp.float32, jnp.bfloat16
S = jax.ShapeDtypeStruct

D = 1024
INW = 3072
Q_BLK, K_BLK, V_BLK, G_BLK = 1, 2, 3, 4
CONV_A_BLK, CONV_GATE_BLK = 10, 11
GMW = 256
RETW = 512
CVW = 256
HEADS = 4
DH = 128
C = 128
KW = 31
HALO = 16
FFH = 2816
NDEV = 8
FFB = 2 * FFH // NDEV
FF_CHUNKS = ((0, 768), (768, 1536), (1536, 2304), (2304, FFH))
EPS = 1e-6
LAYERS = 2
SCALE = DH ** -0.5
VMEM_LIMIT = 56 * 1024 * 1024

ADAM_LR, ADAM_B1, ADAM_B2, ADAM_EPS, ADAM_WD, ADAM_STEP = 0.001, 0.9, 0.999, 1e-08, 0.01, 10

_SQRT_HALF = 0.7071067811865476
_INV_SQRT_2PI = 0.3989422804014327


def _params(*sem):
    return pltpu.CompilerParams(dimension_semantics=sem or None, vmem_limit_bytes=VMEM_LIMIT)


def _resident(shape, index_map):
    return pl.BlockSpec(shape, index_map, pipeline_mode=pl.Buffered(1))


def _dot(a, b):
    return jnp.dot(a, b, preferred_element_type=F32)


def _dot_nt(a, b):
    return lax.dot_general(a, b, (((1,), (1,)), ((), ())), preferred_element_type=F32)


def _dot_tn(a, b):
    return lax.dot_general(a, b, (((0,), (0,)), ((), ())), preferred_element_type=F32)


def _sigmoid(x):
    return 1.0 / (1.0 + jnp.exp(-x))


def _gelu_and_grad(x):
    cdf = 0.5 * (1.0 + lax.erf(x * _SQRT_HALF))
    return x * cdf, cdf + x * jnp.exp(-0.5 * x * x) * _INV_SQRT_2PI


def _silu_and_grad(x):
    s = _sigmoid(x)
    return x * s, s * (1.0 + x * (1.0 - s))


def _standardize(x):
    mu = jnp.mean(x, axis=-1, keepdims=True)
    d = x - mu
    rstd = lax.rsqrt(jnp.mean(d * d, axis=-1, keepdims=True) + EPS)
    return d * rstd, rstd


def _standardize_bwd(dxhat, xhat, rstd):
    m1 = jnp.mean(dxhat, axis=-1, keepdims=True)
    m2 = jnp.mean(dxhat * xhat, axis=-1, keepdims=True)
    return rstd * (dxhat - m1 - xhat * m2)


def _rms(x):
    return lax.rsqrt(jnp.mean(x * x, axis=-1, keepdims=True) + EPS)


def _rmsnorm_bwd(dy, x, r, g):
    u = dy * g
    return r * u - x * (r * r * r) * jnp.mean(u * x, axis=-1, keepdims=True)


def _col_sum(a):
    return jnp.sum(a, axis=0, keepdims=True)


def _rot(t, cos2, sin2):
    return t * cos2 + pltpu.roll(t, DH // 2, axis=1) * sin2


def _rot_t(dt, cos2, sin2):
    return dt * cos2 + pltpu.roll(dt * sin2, DH // 2, axis=1)


MESH = pl.DeviceIdType.MESH
_HBM = pl.BlockSpec(memory_space=pltpu.HBM)
_ANY = pl.BlockSpec(memory_space=pl.ANY)


def _place():
    x, y, c = lax.axis_index("x"), lax.axis_index("y"), lax.axis_index("c")
    return x, y, c, ((1 - x, y), (x, 1 - y), (1 - x, 1 - y))


def _slot(full, kind, width, i):
    if kind == "cols":
        return full.at[:, pl.ds(pl.multiple_of(i * width, 128), width)]
    if kind == "rows":
        return full.at[pl.ds(pl.multiple_of(i * width, 16), width), :]
    return full.at[i]


class _Gather:
    def __init__(self, units):
        self.units = units
        self.inputs = [u[0] for u in units]
        self.out_shape = []
        for src, kind in units:
            r, c = src.shape[-2:]
            shape = {"cols": (r, NDEV * c), "rows": (NDEV * r, c), "lead": (NDEV,) + src.shape}[kind]
            self.out_shape.append(S(shape, src.dtype))
        n = len(units)
        self.scratch = [pltpu.SemaphoreType.DMA((n, 7)), pltpu.SemaphoreType.DMA((n, 7)), pltpu.SemaphoreType.DMA((n,))]

    def run(self, phase, ins, outs, scr):
        ssem, rsem, lsem = scr
        x, y, c, chips = _place()
        me, sib = 4 * x + 2 * y + c, (x, y, 1 - c)
        idx = lambda chip, core: 4 * chip[0] + 2 * chip[1] + core
        for u, (src_arr, kind) in enumerate(self.units):
            src, full = ins[u], outs[u]
            width = src_arr.shape[-1] if kind == "cols" else src_arr.shape[-2]
            slot = functools.partial(_slot, full, kind, width)

            def copy(k, block, to, from_src=False):
                return pltpu.make_async_remote_copy(src_ref=src if from_src else slot(block), dst_ref=slot(block),
                                                    send_sem=ssem.at[u, k], recv_sem=rsem.at[u, k],
                                                    device_id=to, device_id_type=MESH)

            mine = lambda: pltpu.make_async_copy(src, slot(me), lsem.at[u])
            first = lambda: [copy(0, me, sib, True)] + [copy(1 + j, me, (*chip, c), True) for j, chip in enumerate(chips)]
            passed = lambda j: copy(4 + j, idx(chips[j], c), sib)
            if phase == "start":
                mine().start()
                for cp in first():
                    cp.start()
            elif phase == "forward":
                for j, chip in enumerate(chips):
                    copy(1 + j, idx(chip, c), sib).wait_recv()
                    passed(j).start()
            else:
                copy(0, idx((x, y), 1 - c), sib).wait_recv()
                for j, chip in enumerate(chips):
                    copy(4 + j, idx(chip, 1 - c), sib).wait_recv()
                for cp in first() + [passed(j) for j in range(3)]:
                    cp.wait_send()
                mine().wait()


class _Scatter:
    def __init__(self, units):
        self.units = units
        self.inputs = list(units)
        self.out_shape = [S(u.shape, u.dtype) for u in units]
        n = len(units)
        self.scratch = [pltpu.SemaphoreType.DMA((n, 3)), pltpu.SemaphoreType.DMA((n, 3)), pltpu.SemaphoreType.DMA((n,))]

    def run(self, phase, ins, outs, scr):
        ssem, rsem, lsem = scr
        x, y, c, chips = _place()
        myq = 2 * x + y
        for u in range(len(self.units)):
            h, p = ins[u], outs[u]

            def copy(k, chip, send_to_them):
                q = 2 * chip[0] + chip[1]
                return pltpu.make_async_remote_copy(src_ref=h.at[q], dst_ref=p.at[myq if send_to_them else q],
                                                    send_sem=ssem.at[u, k], recv_sem=rsem.at[u, k],
                                                    device_id=(*chip, c), device_id_type=MESH)

            mine = lambda: pltpu.make_async_copy(h.at[myq], p.at[myq], lsem.at[u])
            sends = lambda: [copy(k, chip, True) for k, chip in enumerate(chips)]
            if phase == "start":
                mine().start()
                for cp in sends():
                    cp.start()
            elif phase == "finish":
                for k, chip in enumerate(chips):
                    copy(k, chip, False).wait_recv()
                for cp in sends():
                    cp.wait_send()
                mine().wait()


class _Comms:
    def __init__(self, parts):
        self.parts = parts
        self.inputs = [a for p in parts for a in p.inputs]
        self.out_shape = [a for p in parts for a in p.out_shape]
        self.scratch = [a for p in parts for a in p.scratch]

    def run(self, phase, ins, outs, scr):
        i = o = s = 0
        for p in self.parts:
            ni, no, ns = len(p.inputs), len(p.out_shape), len(p.scratch)
            p.run(phase, ins[i:i + ni], outs[o:o + no], scr[s:s + ns])
            i, o, s = i + ni, o + no, s + ns


def _launch(body, grid, in_specs, out_specs, out_shape, scratch, args, name, sem, comm=None):
    if comm is None:
        outs = pl.pallas_call(body, grid=grid, name=name, in_specs=in_specs, out_specs=out_specs, out_shape=out_shape,
                              scratch_shapes=scratch, compiler_params=_params(*sem))(*args)
        return list(outs), []
    n_in, n_out, n_scr = len(args), len(out_shape), len(scratch)
    ci, co = len(comm.inputs), len(comm.out_shape)
    nsteps = int(np.prod(grid))
    fwd_step = (7 * nsteps) // 8

    def hosted(*refs):
        a = refs[:n_in]
        ca = refs[n_in:n_in + ci]
        o = refs[n_in + ci:n_in + ci + n_out]
        cout = refs[n_in + ci + n_out:n_in + ci + n_out + co]
        s = refs[n_in + ci + n_out + co:n_in + ci + n_out + co + n_scr]
        cs = refs[n_in + ci + n_out + co + n_scr:]
        step = pl.program_id(0)
        for d in range(1, len(grid)):
            step = step * grid[d] + pl.program_id(d)

        @pl.when(step == 0)
        def _():
            comm.run("start", ca, cout, cs)

        body(*a, *o, *s)

        @pl.when(step == fwd_step)
        def _():
            comm.run("forward", ca, cout, cs)

        @pl.when(step == nsteps - 1)
        def _():
            comm.run("finish", ca, cout, cs)

    outs = pl.pallas_call(
        hosted, grid=grid, name=name, in_specs=list(in_specs) + [_HBM] * ci, out_specs=list(out_specs) + [_HBM] * co,
        out_shape=list(out_shape) + comm.out_shape, scratch_shapes=list(scratch) + comm.scratch,
        compiler_params=_params(*["arbitrary"] * len(grid)))(*args, *comm.inputs)
    return list(outs[:n_out]), list(outs[n_out:])


def _comm_only(comm, name):
    ci, co = len(comm.inputs), len(comm.out_shape)

    def body(*refs):
        ca, cout, cs = refs[:ci], refs[ci:ci + co], refs[ci + co:]
        for phase in ("start", "forward", "finish"):
            comm.run(phase, ca, cout, cs)

    return pl.pallas_call(body, name=name, in_specs=[_HBM] * ci, out_specs=[_HBM] * co, out_shape=comm.out_shape,
                          scratch_shapes=comm.scratch, compiler_params=_params())(*comm.inputs)


def _ret_consts():
    idx = np.arange(C, dtype=np.float32)
    gf = (1.0 - np.exp2(-5.0 - np.arange(HEADS, dtype=np.float32))).astype(np.float32)
    out = {}
    for name, gamma, fwd in (("f", gf, True), ("b", gf[::-1].copy(), False)):
        lg = np.log(gamma).astype(np.float32)[:, None]
        diff = idx[:, None] - idx[None, :]
        if fwd:
            mask = diff >= 0
            dist = np.where(mask, diff, 0.0)
            zeta = np.exp(lg * (C - 1 - idx))
            xi = np.exp(lg * (idx + 1))
        else:
            mask = diff < 0
            dist = np.where(mask, -diff, 0.0)
            zeta = np.exp(lg * idx)
            xi = np.exp(lg * (C - idx))
        dm = np.where(mask[None], np.exp(lg[:, :, None] * dist[None]), 0.0).astype(np.float32)
        bc = lambda vec: np.ascontiguousarray(np.broadcast_to(vec.astype(np.float32)[:, :, None], (HEADS, C, DH)))
        out[name] = dict(D=dm, XI=bc(xi), ZETA=bc(zeta), gC=[float(v) for v in np.exp(lg[:, 0] * C).astype(np.float32)])
    return out


def _rope_tables(t):
    half = DH // 2
    inv_freq = (np.float32(10000.0) ** (-np.arange(half, dtype=np.float32) / np.float32(half))).astype(np.float32)
    ang = (np.arange(t, dtype=np.float32)[:, None] * inv_freq[None, :]).astype(np.float64)
    cos, sin = np.cos(ang).astype(np.float32), np.sin(ang).astype(np.float32)
    return np.concatenate([cos, cos], axis=1), np.concatenate([-sin, sin], axis=1)


def _f_inproj(x, g1, w, cos2, sin2, l, name, comm=None):
    t = x.shape[0]
    tm = 512

    def body(x_ref, g_ref, w_ref, cos_ref, sin_ref, proj_ref, ht_ref):
        xv = x_ref[...]
        h = (xv * _rms(xv) * g_ref[...]).astype(BF16)
        ht_ref[...] = h.T
        for nb in range(INW // 512):
            cs = slice(nb * 512, (nb + 1) * 512)
            res = _dot(h, w_ref[:, cs])
            if nb in (Q_BLK, K_BLK):
                for hh in range(HEADS):
                    r = _rot(res[:, hh * DH:(hh + 1) * DH], cos_ref[...], sin_ref[...])
                    proj_ref[:, nb * 512 + hh * DH:nb * 512 + (hh + 1) * DH] = (r * SCALE if nb == K_BLK else r).astype(BF16)
            else:
                proj_ref[:, cs] = res.astype(BF16)

    return _launch(
        body, (t // tm,),
        [pl.BlockSpec((tm, D), lambda i: (i, 0)),
         pl.BlockSpec((None, 1, D), lambda i: (l, 0, 0)),
         _resident((D, INW), lambda i: (0, 0)),
         pl.BlockSpec((tm, DH), lambda i: (i, 0)), pl.BlockSpec((tm, DH), lambda i: (i, 0))],
        [pl.BlockSpec((tm, INW), lambda i: (i, 0)), pl.BlockSpec((D, tm), lambda i: (0, i))],
        [S((t, INW), BF16), S((D, t), BF16)], [], (x, g1, w, cos2, sin2), name, ("parallel",), comm)


def _gm_chunk_fwd(u, v, lng, lnb, ws_ref, bias):
    au, dau = _gelu_and_grad(u)
    av, dav = _gelu_and_grad(v)
    vhat, rstd = _standardize(av)
    vn = (vhat * lng + lnb).astype(BF16)
    head = lax.broadcasted_iota(jnp.int32, (C, GMW), 1) // (GMW // HEADS)
    mixed = bias
    for h in range(HEADS):
        mixed = mixed + jnp.where(head == h, _dot(ws_ref[h], vn), 0.0)
    return au, dau, dav, vhat, rstd, vn, mixed, head


def _f_gm(proj, lng, lnb, ws_bf, bias, l, name):
    t = proj.shape[0]
    tm = 512

    def body(p_ref, lng_ref, lnb_ref, ws_ref, bias_ref, y_ref):
        for ci in range(tm // C):
            rows = slice(ci * C, (ci + 1) * C)
            u = p_ref[rows, 0:GMW].astype(F32)
            v = p_ref[rows, GMW:2 * GMW].astype(F32)
            au, _, _, _, _, _, mixed, _ = _gm_chunk_fwd(u, v, lng_ref[...], lnb_ref[...], ws_ref, bias_ref[...])
            y_ref[rows, :] = (au * mixed).astype(BF16)

    return pl.pallas_call(
        body, grid=(t // tm,), name=name,
        in_specs=[pl.BlockSpec((tm, 2 * GMW), lambda i: (i, 0)),
                  pl.BlockSpec((None, 1, GMW), lambda i: (l, 0, 0)),
                  pl.BlockSpec((None, 1, GMW), lambda i: (l, 0, 0)),
                  pl.BlockSpec((None, HEADS, C, C), lambda i: (l, 0, 0, 0)),
                  pl.BlockSpec((None, C, GMW), lambda i: (l, 0, 0))],
        out_specs=pl.BlockSpec((tm, GMW), lambda i: (i, 0)),
        out_shape=S((t, GMW), BF16),
        compiler_params=_params("parallel"),
    )(proj, lng, lnb, ws_bf, bias)


def _scan_pair(proj, other, col, wf, wb, gcf, gcb, first_is_f, name):
    t = proj.shape[0]
    n = t // C
    sc = min(SCAN_CHUNKS, n)
    nsteps = n // sc
    other_is_proj = other is None

    def body(a1, o1, a2, o2, w1_ref, w2_ref, out1, out2, st1, st2):
        @pl.when(pl.program_id(0) == 0)
        def _():
            st1[...] = jnp.zeros_like(st1)
            st2[...] = jnp.zeros_like(st2)

        def one(a_ref, o_ref, w_ref, gc, st, out, order):
            for h in range(HEADS):
                sl = slice(h * DH, (h + 1) * DH)
                incs = {}
                for j in order:
                    rows = slice(j * C, (j + 1) * C)
                    aw = (a_ref[rows, sl].astype(F32) * w_ref[h]).astype(BF16)
                    incs[j] = _dot_tn(aw, o_ref[rows, sl].astype(BF16))
                cur = st[h]
                for j in order:
                    out[j, h] = cur.astype(BF16)
                    cur = gc[h] * cur + incs[j]
                st[h] = cur

        g1, g2 = (gcf, gcb) if first_is_f else (gcb, gcf)
        one(a1, o1, w1_ref, g1, st1, out1, range(sc))
        one(a2, o2, w2_ref, g2, st2, out2, range(sc - 1, -1, -1))

    up = lambda i: i
    down = lambda i: nsteps - 1 - i

    def specs(ix):
        o_spec = pl.BlockSpec((sc * C, RETW), lambda i: (ix(i), V_BLK if other_is_proj else 0))
        return [pl.BlockSpec((sc * C, RETW), lambda i: (ix(i), col)), o_spec]

    const = lambda: pl.BlockSpec((HEADS, C, DH), lambda i: (0, 0, 0))
    oth = proj if other_is_proj else other
    w1, w2 = (wf, wb) if first_is_f else (wb, wf)
    out1, out2 = pl.pallas_call(
        body, grid=(nsteps,), name=name,
        in_specs=specs(up) + specs(down) + [const(), const()],
        out_specs=[pl.BlockSpec((sc, HEADS, DH, DH), lambda i: (up(i), 0, 0, 0)),
                   pl.BlockSpec((sc, HEADS, DH, DH), lambda i: (down(i), 0, 0, 0))],
        out_shape=[S((n, HEADS, DH, DH), BF16), S((n, HEADS, DH, DH), BF16)],
        scratch_shapes=[pltpu.VMEM((HEADS, DH, DH), F32), pltpu.VMEM((HEADS, DH, DH), F32)],
        compiler_params=_params("arbitrary"),
    )(proj, oth, proj, oth, w1, w2)
    return (out1, out2) if first_is_f else (out2, out1)


SCAN_CHUNKS = 8


RET_CHUNKS = 8


def _f_ret_out(proj, s_f, s_b, rc, name):
    t = proj.shape[0]
    nchunks = min(RET_CHUNKS, t // C)
    tm = nchunks * C

    def body(q_ref, k_ref, v_ref, sf_ref, sb_ref, d_ref, xif_ref, xib_ref, o_ref):
        for ci in range(nchunks):
            rows = slice(ci * C, (ci + 1) * C)
            for h in range(HEADS):
                sl = slice(h * DH, (h + 1) * DH)
                qh, kh, vh = q_ref[rows, sl], k_ref[rows, sl], v_ref[rows, sl]
                p = (_dot_nt(qh, kh) * d_ref[h]).astype(BF16)
                cross = _dot(qh, jnp.concatenate([sf_ref[ci, h], sb_ref[ci, h]], axis=1))
                o_ref[rows, sl] = _dot(p, vh) + xif_ref[h] * cross[:, 0:DH] + xib_ref[h] * cross[:, DH:2 * DH]

    const = lambda: pl.BlockSpec((HEADS, C, DH), lambda i: (0, 0, 0))
    state = lambda: pl.BlockSpec((nchunks, HEADS, DH, DH), lambda i: (i, 0, 0, 0))
    return pl.pallas_call(
        body, grid=(t // tm,), name=name,
        in_specs=[pl.BlockSpec((tm, RETW), lambda i, cb=cb: (i, cb)) for cb in (Q_BLK, K_BLK, V_BLK)] +
                 [state(), state(), const(), const(), const()],
        out_specs=pl.BlockSpec((tm, RETW), lambda i: (i, 0)),
        out_shape=S((t, RETW), F32),
        compiler_params=_params("parallel"),
    )(proj, proj, proj, s_f, s_b, rc["f"]["D"] + rc["b"]["D"], rc["f"]["XI"], rc["b"]["XI"])


def _conv_halo_specs(t, tm, width, col):
    r = tm // HALO
    last = t // HALO - 1
    return [pl.BlockSpec((HALO, width), lambda i: (jnp.maximum(i * r - 1, 0), col)),
            pl.BlockSpec((tm, width), lambda i: (i, col)),
            pl.BlockSpec((HALO, width), lambda i: (jnp.minimum((i + 1) * r, last), col))]


def _fill_ext(ext, prev, cur, nxt, i, nt, tm):
    ext[0:HALO, :] = jnp.where(i > 0, prev, 0.0)
    ext[HALO:HALO + tm, :] = cur
    ext[HALO + tm:2 * HALO + tm, :] = jnp.where(i < nt - 1, nxt, 0.0)


def _glu(a_ref, g_ref):
    return a_ref[...].astype(F32) * _sigmoid(g_ref[...].astype(F32))


def _shifted_copies(ext, rot, tm):
    rows = tm + 2 * HALO - 8
    for b in range(1, 8):
        rot[b - 1, :, :] = ext[pl.ds(b, rows), :]


def _tap(ext, rot, r0, s, rb):
    a, b = divmod(s, 8)
    return ext[pl.ds(r0 + 8 * a, rb), :] if b == 0 else rot[b - 1, pl.ds(r0 + 8 * a, rb), :]


def _f_conv(proj, cw, cb, lng, lnb, l, name, comm=None):
    t = proj.shape[0]
    tm = 512
    nt = t // tm
    rb = 64

    def body(ap, ac, an, gp, gc, gn, cw_ref, cb_ref, lng_ref, lnb_ref, c_ref, y_ref, hext, hrot):
        i = pl.program_id(0)
        _fill_ext(hext, _glu(ap, gp), _glu(ac, gc), _glu(an, gn), i, nt, tm)
        _shifted_copies(hext, hrot, tm)
        for r0 in range(0, tm, rb):
            acc = jnp.zeros((rb, CVW), F32) + cb_ref[...]
            for j in range(KW):
                acc = acc + cw_ref[j:j + 1, :] * _tap(hext, hrot, r0, j + 1, rb)
            c_ref[r0:r0 + rb, :] = acc
            chat, _ = _standardize(acc)
            z = chat * lng_ref[...] + lnb_ref[...]
            y_ref[r0:r0 + rb, :] = (z * _sigmoid(z)).astype(BF16)

    vec = lambda: pl.BlockSpec((None, 1, CVW), lambda i: (l, 0, 0))
    return _launch(
        body, (nt,),
        _conv_halo_specs(t, tm, CVW, CONV_A_BLK) + _conv_halo_specs(t, tm, CVW, CONV_GATE_BLK) +
        [pl.BlockSpec((None, 32, CVW), lambda i: (l, 0, 0)), vec(), vec(), vec()],
        [pl.BlockSpec((tm, CVW), lambda i: (i, 0)), pl.BlockSpec((tm, CVW), lambda i: (i, 0))],
        [S((t, CVW), F32), S((t, CVW), BF16)],
        [pltpu.VMEM((tm + 2 * HALO, CVW), F32), pltpu.VMEM((7, tm + 2 * HALO - 8, CVW), F32)],
        (proj, proj, proj, proj, proj, proj, cw, cb, lng, lnb), name, ("parallel",), comm)


def _f_mixout(x, y_gm, y_cv, o, proj, w, name, comm=None):
    t = x.shape[0]
    tm = 512

    def body(x_ref, ygm_ref, ycv_ref, o_ref, g_ref, w_ref, xm_ref, ycat_t_ref, ycat):
        ycat[:, 0:GMW] = ygm_ref[...]
        ycat[:, GMW + RETW:D] = ycv_ref[...]
        for h in range(HEADS):
            sl = slice(h * DH, (h + 1) * DH)
            ohat, _ = _standardize(o_ref[:, sl])
            g = g_ref[:, sl].astype(F32)
            ycat[:, GMW + h * DH:GMW + (h + 1) * DH] = (ohat * (g * _sigmoid(g))).astype(BF16)
        yc = ycat[...]
        ycat_t_ref[...] = yc.T
        xm_ref[...] = x_ref[...] + _dot(yc, w_ref[...])

    return _launch(
        body, (t // tm,),
        [pl.BlockSpec((tm, D), lambda i: (i, 0)),
         pl.BlockSpec((tm, GMW), lambda i: (i, 0)),
         pl.BlockSpec((tm, CVW), lambda i: (i, 0)),
         pl.BlockSpec((tm, RETW), lambda i: (i, 0)),
         pl.BlockSpec((tm, RETW), lambda i: (i, G_BLK)),
         _resident((D, D), lambda i: (0, 0))],
        [pl.BlockSpec((tm, D), lambda i: (i, 0)), pl.BlockSpec((D, tm), lambda i: (0, i))],
        [S((t, D), F32), S((D, t), BF16)],
        [pltpu.VMEM((tm, D), BF16)], (x, y_gm, y_cv, o, proj, w), name, ("parallel",), comm)


def _f_ffn(xm, g2, w1, w2, l, name, comm=None):
    t = xm.shape[0]
    tm = 512

    def body(x_ref, g_ref, w1_ref, w2_ref, xo_ref, h_ref, gu_ref, act_t_ref):
        xv = x_ref[...]
        h = (xv * _rms(xv) * g_ref[...]).astype(BF16)
        h_ref[...] = h
        acc = xv
        for a, b in FF_CHUNKS:
            gate = _dot_nt(h, w1_ref[a:b, :])
            up = _dot_nt(h, w1_ref[FFH + a:FFH + b, :])
            gu_ref[:, a:b] = gate.astype(BF16)
            gu_ref[:, FFH + a:FFH + b] = up.astype(BF16)
            av = ((gate * _sigmoid(gate)) * up).astype(BF16)
            act_t_ref[a:b, :] = av.T
            acc = acc + _dot(av, w2_ref[a:b, :])
        xo_ref[...] = acc

    return _launch(
        body, (t // tm,),
        [pl.BlockSpec((tm, D), lambda i: (i, 0)),
         pl.BlockSpec((None, 1, D), lambda i: (l, 0, 0)),
         _resident((2 * FFH, D), lambda i: (0, 0)),
         _resident((FFH, D), lambda i: (0, 0))],
        [pl.BlockSpec((tm, D), lambda i: (i, 0)), pl.BlockSpec((tm, D), lambda i: (i, 0)),
         pl.BlockSpec((tm, 2 * FFH), lambda i: (i, 0)), pl.BlockSpec((FFH, tm), lambda i: (0, i))],
        [S((t, D), F32), S((t, D), BF16), S((t, 2 * FFH), BF16), S((FFH, t), BF16)],
        [], (xm, g2, w1, w2), name, ("parallel",), comm)


def _b_ffn(top, xm, g2, gu, w1, w2, l, name, comm=None):
    t = xm.shape[0]
    tm = 256
    from_loss = isinstance(top, tuple)
    n_top = 3 if from_loss else 1

    def body(*refs):
        top_refs = refs[:n_top]
        x_ref, g_ref, gu_ref, w1_ref, w2_ref, dgu_ref, dxm_ref, dxb_ref, dg_ref = refs[n_top:n_top + 9]
        first = pl.program_id(0) == 0

        @pl.when(first)
        def _():
            dg_ref[...] = jnp.zeros_like(dg_ref)

        if from_loss:
            xo_ref, fg_ref, t_ref = top_refs
            loss_ref, dfg_ref = refs[n_top + 9:n_top + 11]

            @pl.when(first)
            def _():
                loss_ref[...] = jnp.zeros_like(loss_ref)
                dfg_ref[...] = jnp.zeros_like(dfg_ref)

            xo = xo_ref[...]
            ro = _rms(xo)
            xr = xo * ro
            err = xr * fg_ref[...] - t_ref[...]
            loss_ref[...] += (0.5 / D) * _col_sum(jnp.sum(err * err, axis=1, keepdims=True))
            dy = err * (1.0 / D)
            dfg_ref[...] += _col_sum(dy * xr)
            dxo = _rmsnorm_bwd(dy, xo, ro, fg_ref[...])
        else:
            dxo = top_refs[0][...]
        dxb = dxo.astype(BF16)
        dxb_ref[...] = dxb
        dh = jnp.zeros((tm, D), F32)
        for a, b in FF_CHUNKS:
            dact = _dot_nt(dxb, w2_ref[a:b, :])
            gate = gu_ref[:, a:b].astype(F32)
            up = gu_ref[:, FFH + a:FFH + b].astype(F32)
            sg, dsg = _silu_and_grad(gate)
            dgate = (dact * up * dsg).astype(BF16)
            dup = (dact * sg).astype(BF16)
            dgu_ref[a:b, :] = dgate.T
            dgu_ref[FFH + a:FFH + b, :] = dup.T
            dh = dh + _dot(dgate, w1_ref[a:b, :]) + _dot(dup, w1_ref[FFH + a:FFH + b, :])
        xv = x_ref[...]
        r = _rms(xv)
        dg_ref[...] += _col_sum(dh * xv * r)
        dxm_ref[...] = dxo + _rmsnorm_bwd(dh, xv, r, g_ref[...])

    tok = lambda: pl.BlockSpec((tm, D), lambda i: (i, 0))
    vec = lambda: pl.BlockSpec((1, D), lambda i: (0, 0))
    top_specs = [tok(), vec(), tok()] if from_loss else [tok()]
    extra_specs = [pl.BlockSpec((1, 1), lambda i: (0, 0)), vec()] if from_loss else []
    extra_shape = [S((1, 1), F32), S((1, D), F32)] if from_loss else []
    return _launch(
        body, (t // tm,),
        top_specs + [tok(), pl.BlockSpec((None, 1, D), lambda i: (l, 0, 0)),
                     pl.BlockSpec((tm, 2 * FFH), lambda i: (i, 0)),
                     _resident((2 * FFH, D), lambda i: (0, 0)),
                     _resident((FFH, D), lambda i: (0, 0))],
        [pl.BlockSpec((2 * FFH, tm), lambda i: (0, i)), tok(), tok(), vec()] + extra_specs,
        [S((2 * FFH, t), BF16), S((t, D), F32), S((t, D), BF16), S((1, D), F32)] + extra_shape,
        [],
        ((*top,) if from_loss else (top,)) + (xm, g2, gu, w1, w2), name, ("arbitrary",), comm)


def _mm_wgrad(at, b, pieces, at_rows, b_mode, group, name):
    bt = 2048
    t = at.shape[-1]
    bt = min(bt, t)
    nt = t // bt
    if at_rows:
        ka = at.shape[0] // pieces
        a_spec = pl.BlockSpec((ka, bt), lambda j, tt: (j, tt))
    else:
        ka = at.shape[0]
        a_spec = pl.BlockSpec((ka, bt), lambda j, tt: (0, tt))
    if b_mode == "shared":
        nb, b_spec = b.shape[1], pl.BlockSpec((bt, b.shape[1]), lambda j, tt: (tt, 0))
    elif b_mode == "cols":
        nb = b.shape[1] // pieces
        b_spec = pl.BlockSpec((bt, group * nb), lambda j, tt: (tt, j))
    else:
        nb, b_spec = b.shape[2], pl.BlockSpec((None, bt, b.shape[2]), lambda j, tt: (j, tt, 0))
    assert group == 1 or b_mode == "cols"

    def body(a_ref, b_ref, o_ref, acc):
        tt = pl.program_id(1)

        @pl.when(tt == 0)
        def _():
            acc[...] = jnp.zeros_like(acc)

        acc[...] += _dot(a_ref[...], b_ref[...])

        @pl.when(tt == nt - 1)
        def _():
            for k in range(group):
                o_ref[k] = acc[:, k * nb:(k + 1) * nb].astype(BF16)

    return pl.pallas_call(
        body, grid=(pieces // group, nt), name=name,
        in_specs=[a_spec, b_spec],
        out_specs=pl.BlockSpec((group, ka, nb), lambda j, tt: (j, 0, 0)),
        out_shape=S((pieces, ka, nb), BF16),
        scratch_shapes=[pltpu.VMEM((ka, group * nb), F32)],
        compiler_params=_params("parallel", "arbitrary"),
    )(at, b)


def _b_mixout(dxm, w, o, proj, c, lng, lnb, l, name, comm=None):
    t = dxm.shape[0]
    tm = 512

    def body(dxm_ref, w_ref, o_ref, g_ref, c_ref, lng_ref, lnb_ref,
             dxb_ref, dygm_ref, dO_ref, dg_ref, dc_ref, dlg_ref, dlb_ref, dcb_ref):
        @pl.when(pl.program_id(0) == 0)
        def _():
            dlg_ref[...] = jnp.zeros_like(dlg_ref)
            dlb_ref[...] = jnp.zeros_like(dlb_ref)
            dcb_ref[...] = jnp.zeros_like(dcb_ref)

        dxb = dxm_ref[...].astype(BF16)
        dxb_ref[...] = dxb
        dy = _dot_nt(dxb, w_ref[...])
        dygm_ref[...] = dy[:, 0:GMW]
        for h in range(HEADS):
            sl = slice(h * DH, (h + 1) * DH)
            ohat, rstd = _standardize(o_ref[:, sl])
            sg, dsg = _silu_and_grad(g_ref[:, sl].astype(F32))
            dyr = dy[:, GMW + h * DH:GMW + (h + 1) * DH]
            dg_ref[:, sl] = (dyr * ohat * dsg).astype(BF16)
            dO_ref[:, sl] = _standardize_bwd(dyr * sg, ohat, rstd)
        chat, rstd = _standardize(c_ref[...])
        z = chat * lng_ref[...] + lnb_ref[...]
        _, dsz = _silu_and_grad(z)
        dz = dy[:, GMW + RETW:D] * dsz
        dlg_ref[...] += _col_sum(dz * chat)
        dlb_ref[...] += _col_sum(dz)
        dc = _standardize_bwd(dz * lng_ref[...], chat, rstd)
        dcb_ref[...] += _col_sum(dc)
        dc_ref[...] = dc

    vec = lambda: pl.BlockSpec((None, 1, CVW), lambda i: (l, 0, 0))
    acc = lambda: pl.BlockSpec((1, CVW), lambda i: (0, 0))
    return _launch(
        body, (t // tm,),
        [pl.BlockSpec((tm, D), lambda i: (i, 0)),
         _resident((D, D), lambda i: (0, 0)),
         pl.BlockSpec((tm, RETW), lambda i: (i, 0)),
         pl.BlockSpec((tm, RETW), lambda i: (i, G_BLK)),
         pl.BlockSpec((tm, CVW), lambda i: (i, 0)), vec(), vec()],
        [pl.BlockSpec((tm, D), lambda i: (i, 0)), pl.BlockSpec((tm, GMW), lambda i: (i, 0)),
         pl.BlockSpec((tm, RETW), lambda i: (i, 0)), pl.BlockSpec((tm, RETW), lambda i: (i, 0)),
         pl.BlockSpec((tm, CVW), lambda i: (i, 0)), acc(), acc(), acc()],
        [S((t, D), BF16), S((t, GMW), F32), S((t, RETW), F32), S((t, RETW), BF16), S((t, CVW), F32),
         S((1, CVW), F32), S((1, CVW), F32), S((1, CVW), F32)],
        [], (dxm, w, o, proj, c, lng, lnb), name, ("arbitrary",), comm)


def _b_gm(proj, dy, lng, lnb, ws_bf, wst_bf, bias, l, name, comm=None):
    t = proj.shape[0]
    tm = 512
    nt = t // tm

    def body(p_ref, dy_ref, lng_ref, lnb_ref, ws_ref, wst_ref, bias_ref,
             duv_ref, dws_ref, dbias_ref, dbs_ref, dlg_ref, dlb_ref):
        @pl.when(pl.program_id(0) == 0)
        def _():
            dws_ref[...] = jnp.zeros_like(dws_ref)
            dbias_ref[...] = jnp.zeros_like(dbias_ref)
            dbs_ref[...] = jnp.zeros_like(dbs_ref)
            dlg_ref[...] = jnp.zeros_like(dlg_ref)
            dlb_ref[...] = jnp.zeros_like(dlb_ref)

        for ci in range(tm // C):
            rows = slice(ci * C, (ci + 1) * C)
            u = p_ref[rows, 0:GMW].astype(F32)
            v = p_ref[rows, GMW:2 * GMW].astype(F32)
            au, dau, dav, vhat, rstd, vn, mixed, head = _gm_chunk_fwd(u, v, lng_ref[...], lnb_ref[...], ws_ref, bias_ref[...])
            dyc = dy_ref[rows, :]
            dmixed = dyc * au
            dmb = dmixed.astype(BF16)
            dbias_ref[...] += dmixed
            dvn = jnp.zeros((C, GMW), F32)
            for h in range(HEADS):
                dws_ref[h] += _dot_nt(jnp.where(head == h, dmixed, 0.0).astype(BF16), vn)
                dvn = dvn + jnp.where(head == h, _dot(wst_ref[h], dmb), 0.0)
            dlg_ref[...] += _col_sum(dvn * vhat)
            dlb_ref[...] += _col_sum(dvn)
            dav_in = _standardize_bwd(dvn * lng_ref[...], vhat, rstd)
            duv_ref[rows, 0:GMW] = (dyc * mixed * dau).astype(BF16)
            duv_ref[rows, GMW:2 * GMW] = (dav_in * dav).astype(BF16)

        @pl.when(pl.program_id(0) == nt - 1)
        def _():
            head = lax.broadcasted_iota(jnp.int32, (C, GMW), 1) // (GMW // HEADS)
            lane = lax.broadcasted_iota(jnp.int32, (C, 128), 1)
            fold = jnp.zeros((C, 128), F32)
            for h in range(HEADS):
                col = jnp.sum(jnp.where(head == h, dbias_ref[...], 0.0), axis=1, keepdims=True)
                fold = jnp.where(lane == h, col, fold)
            dbs_ref[...] = fold

    vec = lambda: pl.BlockSpec((None, 1, GMW), lambda i: (l, 0, 0))
    mats = lambda: pl.BlockSpec((None, HEADS, C, C), lambda i: (l, 0, 0, 0))
    return _launch(
        body, (nt,),
        [pl.BlockSpec((tm, 2 * GMW), lambda i: (i, 0)), pl.BlockSpec((tm, GMW), lambda i: (i, 0)),
         vec(), vec(), mats(), mats(), pl.BlockSpec((None, C, GMW), lambda i: (l, 0, 0))],
        [pl.BlockSpec((tm, 2 * GMW), lambda i: (i, 0)),
         pl.BlockSpec((HEADS, C, C), lambda i: (0, 0, 0)),
         pl.BlockSpec((C, GMW), lambda i: (0, 0)), pl.BlockSpec((C, 128), lambda i: (0, 0)),
         pl.BlockSpec((1, GMW), lambda i: (0, 0)), pl.BlockSpec((1, GMW), lambda i: (0, 0))],
        [S((t, 2 * GMW), BF16), S((HEADS, C, C), F32), S((C, GMW), F32), S((C, 128), F32),
         S((1, GMW), F32), S((1, GMW), F32)],
        [], (proj, dy, lng, lnb, ws_bf, wst_bf, bias), name, ("arbitrary",), comm)


def _b_conv(proj, dc, cw, l, name, comm=None):
    t = proj.shape[0]
    tm = 256
    nt = t // tm
    rb = 64

    def body(ap, ac, an, gp, gc, gn, dp, dcur, dn, cw_ref, dag_ref, dcw_ref, hext, dext, hrot, drot):
        i = pl.program_id(0)

        @pl.when(i == 0)
        def _():
            dcw_ref[...] = jnp.zeros_like(dcw_ref)

        _fill_ext(hext, _glu(ap, gp), _glu(ac, gc), _glu(an, gn), i, nt, tm)
        _fill_ext(dext, dp[...], dcur[...], dn[...], i, nt, tm)
        _shifted_copies(hext, hrot, tm)
        _shifted_copies(dext, drot, tm)
        for j in range(KW):
            dcw_ref[j:j + 1, :] += _col_sum(dcur[...] * _tap(hext, hrot, 0, j + 1, tm))
        for r0 in range(0, tm, rb):
            dh = jnp.zeros((rb, CVW), F32)
            for j in range(KW):
                dh = dh + cw_ref[j:j + 1, :] * _tap(dext, drot, r0, 2 * HALO - 1 - j, rb)
            a = ac[r0:r0 + rb, :].astype(F32)
            s = _sigmoid(gc[r0:r0 + rb, :].astype(F32))
            dag_ref[r0:r0 + rb, 0:CVW] = (dh * s).astype(BF16)
            dag_ref[r0:r0 + rb, CVW:2 * CVW] = (dh * a * s * (1.0 - s)).astype(BF16)

    dspecs = _conv_halo_specs(t, tm, CVW, 0)
    return _launch(
        body, (nt,),
        _conv_halo_specs(t, tm, CVW, CONV_A_BLK) + _conv_halo_specs(t, tm, CVW, CONV_GATE_BLK) + dspecs +
        [pl.BlockSpec((None, 32, CVW), lambda i: (l, 0, 0))],
        [pl.BlockSpec((tm, 2 * CVW), lambda i: (i, 0)), pl.BlockSpec((32, CVW), lambda i: (0, 0))],
        [S((t, 2 * CVW), BF16), S((32, CVW), F32)],
        [pltpu.VMEM((tm + 2 * HALO, CVW), F32), pltpu.VMEM((tm + 2 * HALO, CVW), F32),
         pltpu.VMEM((7, tm + 2 * HALO - 8, CVW), F32), pltpu.VMEM((7, tm + 2 * HALO - 8, CVW), F32)],
        (proj, proj, proj, proj, proj, proj, dc, dc, dc, cw), name, ("arbitrary",), comm)


def _b_ret_out(proj, cos2, sin2, dO, s_f, s_b, g_f, g_b, rc, name, comm=None):
    t = proj.shape[0]
    nchunks = min(RET_CHUNKS, t // C)
    tm = nchunks * C

    def body(q_ref, k_ref, v_ref, cos_ref, sin_ref, dO_ref, sf_ref, sb_ref, gf_ref, gb_ref,
             d_ref, xif_ref, xib_ref, zef_ref, zeb_ref, dq_ref, dk_ref, dv_ref):
        for ci in range(nchunks):
            rows = slice(ci * C, (ci + 1) * C)
            cos_v, sin_v = cos_ref[rows, :], sin_ref[rows, :]
            for h in range(HEADS):
                sl = slice(h * DH, (h + 1) * DH)
                qh, kh, vh = q_ref[rows, sl], k_ref[rows, sl], v_ref[rows, sl]
                dOh = dO_ref[rows, sl].astype(BF16)
                dm = d_ref[h]
                p = (_dot_nt(qh, kh) * dm).astype(BF16)
                dp = (_dot_nt(dOh, vh) * dm).astype(BF16)
                from_s = _dot_nt(dOh, jnp.concatenate([sf_ref[ci, h], sb_ref[ci, h]], axis=0))
                from_g = _dot_nt(vh, jnp.concatenate([gf_ref[ci, h], gb_ref[ci, h]], axis=0))
                kg = _dot(kh, jnp.concatenate([gf_ref[ci, h], gb_ref[ci, h]], axis=1))
                dqr = _dot(dp, kh) + xif_ref[h] * from_s[:, 0:DH] + xib_ref[h] * from_s[:, DH:2 * DH]
                dkr = (_dot_tn(dp, qh) + zef_ref[h] * from_g[:, 0:DH] + zeb_ref[h] * from_g[:, DH:2 * DH]) * SCALE
                dv = _dot_tn(p, dOh) + zef_ref[h] * kg[:, 0:DH] + zeb_ref[h] * kg[:, DH:2 * DH]
                dq_ref[rows, sl] = _rot_t(dqr, cos_v, sin_v).astype(BF16)
                dk_ref[rows, sl] = _rot_t(dkr, cos_v, sin_v).astype(BF16)
                dv_ref[rows, sl] = dv.astype(BF16)

    const = lambda: pl.BlockSpec((HEADS, C, DH), lambda i: (0, 0, 0))
    state = lambda: pl.BlockSpec((nchunks, HEADS, DH, DH), lambda i: (i, 0, 0, 0))
    tok = lambda: pl.BlockSpec((tm, RETW), lambda i: (i, 0))
    return _launch(
        body, (t // tm,),
        [pl.BlockSpec((tm, RETW), lambda i, cb=cb: (i, cb)) for cb in (Q_BLK, K_BLK, V_BLK)] +
        [pl.BlockSpec((tm, DH), lambda i: (i, 0)), pl.BlockSpec((tm, DH), lambda i: (i, 0)), tok(),
         state(), state(), state(), state()] + [const() for _ in range(5)],
        [tok(), tok(), tok()],
        [S((t, RETW), BF16) for _ in range(3)], [],
        (proj, proj, proj, cos2, sin2, dO, s_f, s_b, g_f, g_b, rc["f"]["D"] + rc["b"]["D"],
         rc["f"]["XI"], rc["b"]["XI"], rc["f"]["ZETA"], rc["b"]["ZETA"]), name, ("parallel",), comm)


def _b_inproj(d_uv, dqkv, d_g, d_ag, w, x, g1, dxm, l, name, comm=None):
    t = x.shape[0]
    tm = 512

    def body(duv_ref, dq_ref, dk_ref, dv_ref, dg_ref, dag_ref, w_ref, x_ref, g_ref, dxm_ref,
             dp_ref, dx_ref, dn_ref):
        @pl.when(pl.program_id(0) == 0)
        def _():
            dn_ref[...] = jnp.zeros_like(dn_ref)

        for k, part in enumerate((duv_ref, dq_ref, dk_ref, dv_ref, dg_ref, dag_ref)):
            dp_ref[:, 512 * k:512 * (k + 1)] = part[...]
        dh = _dot_nt(dp_ref[...], w_ref[...])
        xv = x_ref[...]
        r = _rms(xv)
        dn_ref[...] += _col_sum(dh * xv * r)
        dx_ref[...] = dxm_ref[...] + _rmsnorm_bwd(dh, xv, r, g_ref[...])

    half = lambda: pl.BlockSpec((tm, 512), lambda i: (i, 0))
    full = lambda: pl.BlockSpec((tm, D), lambda i: (i, 0))
    return _launch(
        body, (t // tm,),
        [half() for _ in range(6)] +
        [_resident((D, INW), lambda i: (0, 0)), full(), pl.BlockSpec((None, 1, D), lambda i: (l, 0, 0)), full()],
        [pl.BlockSpec((tm, INW), lambda i: (i, 0)), full(), pl.BlockSpec((1, D), lambda i: (0, 0))],
        [S((t, INW), BF16), S((t, D), F32), S((1, D), F32)], [],
        (d_uv, *dqkv, d_g, d_ag, w, x, g1, dxm), name, ("arbitrary",), comm)


class _PairSwap:
    def __init__(self, units):
        self.inputs = list(units)
        self.out_shape = [S((4,) + u.shape[1:], BF16) for u in units]
        n = len(units)
        self.scratch = [pltpu.SemaphoreType.DMA((n, 4)), pltpu.SemaphoreType.DMA((n, 4))]

    def run(self, phase, ins, outs, scr):
        ssem, rsem = scr
        x, y, c, _ = _place()
        copies = lambda: [pltpu.make_async_remote_copy(src_ref=ins[u].at[2 * chip + (1 - c)], dst_ref=outs[u].at[chip],
                                                       send_sem=ssem.at[u, chip], recv_sem=rsem.at[u, chip],
                                                       device_id=(x, y, 1 - c), device_id_type=MESH)
                          for u in range(len(self.inputs)) for chip in range(4)]
        if phase == "start":
            for cp in copies():
                cp.start()
        elif phase == "finish":
            for cp in copies():
                cp.wait_recv()
            for cp in copies():
                cp.wait_send()


def _pair_add(g, q, name):
    _, mm, nn = g.shape
    bm = _row_block(mm)

    def body(c_ref, g_ref, q_ref, h_ref):
        h_ref[...] = (g_ref[...].astype(F32) + q_ref[...].astype(F32)).astype(BF16)

    blk = lambda: pl.BlockSpec((None, bm, nn), lambda qq, i, c_ref: (qq, i, 0))
    return pl.pallas_call(
        body, name=name,
        grid_spec=pltpu.PrefetchScalarGridSpec(
            num_scalar_prefetch=1, grid=(4, mm // bm),
            in_specs=[pl.BlockSpec((None, None, bm, nn), lambda qq, i, c_ref: (qq, c_ref[0], i, 0)), blk()],
            out_specs=blk()),
        out_shape=S((4, mm, nn), BF16),
        compiler_params=_params("parallel", "parallel"),
    )(lax.axis_index("c").astype(jnp.int32).reshape(1), g.reshape(4, 2, mm, nn), q)


_BIG = ("w_in", "w_out", "w_ffn_in", "w_ffn_out")
_KIND = dict(w_in="cols", w_out="rows", w_ffn_in="lead", w_ffn_out="rows")
CWP = 128
EARLY_ROWS, LATE_ROWS = 152, 8


def _step(x, tgt, wts, sh, cw_pad):
    t = x.shape[0]
    rc = _ret_consts()
    cos2, sin2 = (jnp.asarray(a) for a in _rope_tables(t))
    n1 = wts["norm1_g"].reshape(LAYERS, 1, D)
    n2 = wts["norm2_g"].reshape(LAYERS, 1, D)
    gm_lng = wts["gm_ln_g"].reshape(LAYERS, 1, GMW)
    gm_lnb = wts["gm_ln_b"].reshape(LAYERS, 1, GMW)
    ws_bf = wts["gm_ws"].astype(BF16)
    wst_bf = jnp.swapaxes(wts["gm_ws"], 2, 3).astype(BF16)
    bias = jnp.repeat(jnp.swapaxes(wts["gm_bs"], 1, 2), GMW // HEADS, axis=2)
    cb = wts["conv_b"].reshape(LAYERS, 1, CVW)
    cv_lng = wts["conv_ln_g"].reshape(LAYERS, 1, CVW)
    cv_lnb = wts["conv_ln_b"].reshape(LAYERS, 1, CVW)
    unit = lambda f, l: (sh[f][l], _KIND[f])
    cshard = CVW // NDEV

    full = {f: [None] * LAYERS for f in _BIG}
    full["w_in"][0], cw_all = _comm_only(_Gather([unit("w_in", 0), (cw_pad, "lead")]), "gather_first")
    cw = jnp.transpose(cw_all[:, :, :, :cshard], (1, 2, 0, 3)).reshape(LAYERS, 32, CVW)

    gcf, gcb = rc["f"]["gC"], rc["b"]["gC"]
    saved = []
    for l in range(LAYERS):
        first = l == 0
        (proj, h1), got = _f_inproj(x, n1, full["w_in"][l], cos2, sin2, l, f"f_inproj_{l}",
                                    _Gather([unit("w_ffn_in", 0)]) if first else None)
        if first:
            full["w_ffn_in"][0], = got
        y_gm = _f_gm(proj, gm_lng, gm_lnb, ws_bf, bias, l, f"f_gm_{l}")
        s_f, s_b = _scan_pair(proj, None, 2, rc["f"]["ZETA"], rc["b"]["ZETA"], gcf, gcb, True, f"f_ret_state_{l}")
        o = _f_ret_out(proj, s_f, s_b, rc, f"f_ret_out_{l}")
        (c, y_cv), got = _f_conv(proj, cw, cb, cv_lng, cv_lnb, l, f"f_conv_{l}",
                                 _Gather([unit("w_out", 0)]) if first else None)
        if first:
            full["w_out"][0], = got
        (xm, ycat), got = _f_mixout(x, y_gm, y_cv, o, proj, full["w_out"][l], f"f_mixout_{l}",
                                    _Gather([unit("w_ffn_out", 0)]) if first else None)
        if first:
            full["w_ffn_out"][0], = got
        w1t = full["w_ffn_in"][l].reshape(2 * FFH, D)
        (xo, h2, gu, act), got = _f_ffn(xm, n2, w1t, full["w_ffn_out"][l], l, f"f_ffn_{l}",
                                        _Gather([unit(f, 1) for f in _BIG]) if first else None)
        if first:
            full["w_in"][1], full["w_out"][1], full["w_ffn_in"][1], full["w_ffn_out"][1] = got
        saved.append(dict(x=x, proj=proj, h1=h1, o=o, s_f=s_f, s_b=s_b, c=c, xm=xm, ycat=ycat, h2=h2, gu=gu, act=act))
        x = xo

    parts = {f: [None] * LAYERS for f in _BIG}
    small = [None] * LAYERS
    norm1 = [None] * LAYERS
    upper = None
    top = (x, wts["final_g"].reshape(1, D), tgt)
    for l in reversed(range(LAYERS)):
        sv = saved[l]
        outs, got = _b_ffn(top, sv["xm"], n2, sv["gu"], full["w_ffn_in"][l].reshape(2 * FFH, D), full["w_ffn_out"][l], l,
                           f"b_ffn_{l}", _Scatter(upper) if upper else None)
        dgu_t, dxm, dxo_bf, d_n2 = outs[:4]
        if l == LAYERS - 1:
            loss, d_final = outs[4:]
        if upper:
            for f, p in zip(_BIG, got):
                parts[f][l + 1] = p
        g_f2 = _mm_wgrad(sv["act"], dxo_bf, NDEV // 2, True, "shared", 1, f"g_ffn_out_{l}").reshape(NDEV, FFH // NDEV, D)
        g_f1 = _mm_wgrad(dgu_t, sv["h2"], NDEV, True, "shared", 1, f"g_ffn_in_{l}")
        (dxm_bf, dy_gm, dO, d_g, dc, d_cvlg, d_cvlb, d_cb), (q_f1, q_f2) = _b_mixout(
            dxm, full["w_out"][l], sv["o"], sv["proj"], sv["c"], cv_lng, cv_lnb, l, f"b_mixout_{l}", _PairSwap([g_f1, g_f2]))
        h_f1 = _pair_add(g_f1, q_f1, f"pair_add_w_ffn_in_{l}")
        h_f2 = _pair_add(g_f2, q_f2, f"pair_add_w_ffn_out_{l}")
        g_out = _mm_wgrad(sv["ycat"], dxm_bf, 1, False, "shared", 1, f"g_out_{l}").reshape(NDEV, D // NDEV, D)
        last = l == 0
        (d_uv, d_ws, _, d_bs_fold, d_gmlg, d_gmlb), (q_out,) = _b_gm(
            sv["proj"], dy_gm, gm_lng, gm_lnb, ws_bf, wst_bf, bias, l, f"b_gm_{l}", _PairSwap([g_out]))
        h_out = _pair_add(g_out, q_out, f"pair_add_w_out_{l}")
        (d_ag, d_cw), got = _b_conv(sv["proj"], dc, cw, l, f"b_conv_{l}", _Scatter([h_f1]) if last else None)
        if last:
            parts["w_ffn_in"][l], = got
        small[l] = dict(gm_ln_g=d_gmlg[0], gm_ln_b=d_gmlb[0], gm_ws=d_ws, gm_bs=d_bs_fold[:, :HEADS].T, conv_w=d_cw[:KW],
                        conv_b=d_cb[0], conv_ln_g=d_cvlg[0], conv_ln_b=d_cvlb[0], norm2_g=d_n2[0])
        comm = None
        if last:
            early_g = {k: jnp.stack([small[ll][k] for ll in range(LAYERS)]) for k in small[0]}
            early_g["final_g"] = d_final[0]
            early_buf = _pack([early_g[k] for k in _SMALL_EARLY] + [loss], EARLY_ROWS)
            comm = _Comms([_Scatter([h_out, h_f2]), _Gather([(early_buf, "lead")])])
        g_f, g_b = _scan_pair(sv["proj"], dO, 1, rc["f"]["XI"], rc["b"]["XI"], gcf, gcb, False, f"b_ret_state_{l}")
        dqkv, got = _b_ret_out(sv["proj"], cos2, sin2, dO, sv["s_f"], sv["s_b"], g_f, g_b, rc, f"b_ret_out_{l}", comm)
        if last:
            parts["w_out"][l], parts["w_ffn_out"][l], early_parts = got
        (dproj, top, d_n1), _ = _b_inproj(d_uv, dqkv, d_g, d_ag, full["w_in"][l], sv["x"], n1, dxm, l, f"b_inproj_{l}")
        norm1[l] = d_n1[0]
        g_in = _mm_wgrad(sv["h1"], dproj, NDEV, False, "cols", 4, f"g_in_{l}")
        q_in, = _comm_only(_PairSwap([g_in]), f"pair_swap_w_in_{l}")
        h_in = _pair_add(g_in, q_in, f"pair_add_w_in_{l}")
        if last:
            tail = [h_in]
        else:
            upper = [h_in, h_out, h_f1, h_f2]
    late_buf = _pack([jnp.stack(norm1)], LATE_ROWS)
    parts["w_in"][0], late_parts = _comm_only(_Comms([_Scatter(tail), _Gather([(late_buf, "lead")])]), "exchange_last")
    return loss, top, parts, (early_parts, late_parts)


def _adamw(w, g, m, v):
    m = ADAM_B1 * m + (1.0 - ADAM_B1) * g
    v = ADAM_B2 * v + (1.0 - ADAM_B2) * (g * g)
    m_hat = m / (1.0 - ADAM_B1 ** ADAM_STEP)
    v_hat = v / (1.0 - ADAM_B2 ** ADAM_STEP)
    return -ADAM_LR * (m_hat / (jnp.sqrt(v_hat) + ADAM_EPS) + ADAM_WD * w), m, v


def _cast_blocks(ws):
    def body(*refs):
        ins, outs = refs[:len(ws)], refs[len(ws):]
        for k, src in enumerate(ins):
            for l in range(LAYERS):
                outs[k * LAYERS + l][...] = src[l].astype(BF16)

    outs = pl.pallas_call(body, name="cast_blocks", out_shape=[S(w.shape[1:], BF16) for w in ws for _ in range(LAYERS)],
                          compiler_params=_params())(*ws)
    return [list(outs[k * LAYERS:(k + 1) * LAYERS]) for k in range(len(ws))]


def _row_block(mm):
    return next(b for b in (256, 352, 128) if mm % b == 0)


def _sum_adam(parts, w, m, v, l, prev, name):
    _, mm, nn = parts.shape
    bm = _row_block(mm)

    def body(p_ref, w_ref, m_ref, v_ref, *rest):
        g_ref, d_ref, nm_ref, nv_ref = rest[-4:]
        g = p_ref[0].astype(F32)
        for s in range(1, 4):
            g = g + p_ref[s].astype(F32)
        g_ref[...] = g
        d_ref[...], nm_ref[...], nv_ref[...] = _adamw(w_ref[...], g, m_ref[...], v_ref[...])

    blk = lambda: pl.BlockSpec((None, bm, nn), lambda i: (l, i, 0))
    prev = list(prev) if prev else []
    return pl.pallas_call(
        body, grid=(mm // bm,), name=name,
        in_specs=[pl.BlockSpec((4, bm, nn), lambda i: (0, i, 0)), blk(), blk(), blk()] + [_ANY] * len(prev),
        out_specs=[blk() for _ in range(4)],
        out_shape=[S(w.shape, F32) for _ in range(4)],
        input_output_aliases={4 + j: j for j in range(len(prev))},
        compiler_params=_params("parallel"),
    )(parts, w, m, v, *prev)


def _sum_small(parts):
    n = len(parts)

    def body(*refs):
        for p_ref, o_ref in zip(refs[:n], refs[n:]):
            g = p_ref[0]
            for s in range(1, NDEV):
                g = g + p_ref[s]
            o_ref[...] = g

    return pl.pallas_call(body, name="sum_small", out_shape=[S(p.shape[1:], F32) for p in parts],
                          compiler_params=_params())(*parts)


def _adam_small(quads):
    n = len(quads)
    as2d = lambda a: a.reshape(1, -1) if a.ndim == 1 else a

    def body(*refs):
        ins, outs = refs[:4 * n], refs[4 * n:]
        for k in range(n):
            g_ref, w_ref, m_ref, v_ref = ins[4 * k:4 * k + 4]
            outs[3 * k][...], outs[3 * k + 1][...], outs[3 * k + 2][...] = _adamw(w_ref[...], g_ref[...], m_ref[...], v_ref[...])

    flat = [as2d(a) for q in quads for a in q]
    outs = pl.pallas_call(body, name="adam_small", out_shape=[S(as2d(q[1]).shape, F32) for q in quads for _ in range(3)],
                          compiler_params=_params())(*flat)
    return [tuple(o.reshape(q[1].shape) for o in outs[3 * k:3 * k + 3]) for k, q in enumerate(quads)]


_SMALL = ("norm1_g", "gm_ln_g", "gm_ln_b", "gm_ws", "gm_bs", "conv_w", "conv_b", "conv_ln_g", "conv_ln_b",
          "norm2_g", "final_g")
_SMALL_EARLY = _SMALL[1:]
_NAMES = ("norm1_g", "w_in", "gm_ln_g", "gm_ln_b", "gm_ws", "gm_bs", "conv_w", "conv_b", "conv_ln_g", "conv_ln_b",
          "w_out", "norm2_g", "w_ffn_in", "w_ffn_out", "final_g")


def _pack(parts, rows):
    flat = jnp.concatenate([p.reshape(-1) for p in parts])
    return jnp.pad(flat, (0, rows * 1024 - flat.shape[0])).reshape(rows, 1024)


def _unpack(buf, shapes):
    flat = buf.reshape(-1)
    out, o = [], 0
    for shp in shapes:
        sz = int(np.prod(shp))
        out.append(flat[o:o + sz].reshape(shp))
        o += sz
    return out


def kernel(x, norm1_g, w_in, gm_ln_g, gm_ln_b, gm_ws, gm_bs, conv_w, conv_b, conv_ln_g, conv_ln_b, w_out, norm2_g, w_ffn_in, w_ffn_out, final_g, loss_target, m_norm1_g, m_w_in, m_gm_ln_g, m_gm_ln_b, m_gm_ws, m_gm_bs, m_conv_w, m_conv_b, m_conv_ln_g, m_conv_ln_b, m_w_out, m_norm2_g, m_w_ffn_in, m_w_ffn_out, m_final_g, v_norm1_g, v_w_in, v_gm_ln_g, v_gm_ln_b, v_gm_ws, v_gm_bs, v_conv_w, v_conv_b, v_conv_ln_g, v_conv_ln_b, v_w_out, v_norm2_g, v_w_ffn_in, v_w_ffn_out, v_final_g):
    w = dict(norm1_g=norm1_g, w_in=w_in, gm_ln_g=gm_ln_g, gm_ln_b=gm_ln_b, gm_ws=gm_ws, gm_bs=gm_bs, conv_w=conv_w,
             conv_b=conv_b, conv_ln_g=conv_ln_g, conv_ln_b=conv_ln_b, w_out=w_out, norm2_g=norm2_g, w_ffn_in=w_ffn_in,
             w_ffn_out=w_ffn_out, final_g=final_g)
    mo = dict(norm1_g=m_norm1_g, w_in=m_w_in, gm_ln_g=m_gm_ln_g, gm_ln_b=m_gm_ln_b, gm_ws=m_gm_ws, gm_bs=m_gm_bs,
              conv_w=m_conv_w, conv_b=m_conv_b, conv_ln_g=m_conv_ln_g, conv_ln_b=m_conv_ln_b, w_out=m_w_out,
              norm2_g=m_norm2_g, w_ffn_in=m_w_ffn_in, w_ffn_out=m_w_ffn_out, final_g=m_final_g)
    vo = dict(norm1_g=v_norm1_g, w_in=v_w_in, gm_ln_g=v_gm_ln_g, gm_ln_b=v_gm_ln_b, gm_ws=v_gm_ws, gm_bs=v_gm_bs,
              conv_w=v_conv_w, conv_b=v_conv_b, conv_ln_g=v_conv_ln_g, conv_ln_b=v_conv_ln_b, w_out=v_w_out,
              norm2_g=v_norm2_g, w_ffn_in=v_w_ffn_in, w_ffn_out=v_w_ffn_out, final_g=v_final_g)
    t = x.shape[1]
    me = 4 * lax.axis_index("x") + 2 * lax.axis_index("y") + lax.axis_index("c")
    cshard = conv_w.shape[2]

    cw_pad = jnp.pad(conv_w, ((0, 0), (0, 32 - KW), (0, CWP - cshard)))
    flip = lambda a: jnp.swapaxes(a, 1, 2)
    big = {f: tuple(flip(a[f]) if f == "w_ffn_in" else a[f] for a in (w, mo, vo)) for f in _BIG}
    sh = dict(zip(_BIG, _cast_blocks([big[f][0] for f in _BIG])))
    loss, dx, parts, small_parts = _step(x.reshape(t, D), loss_target.reshape(t, D), w, sh, cw_pad)

    grads, delta, new_m, new_v = {}, {}, {}, {}
    for f in _BIG:
        outs = None
        for l in reversed(range(LAYERS)):
            outs = _sum_adam(parts[f][l], *big[f], l, outs, f"sum_adam_{f}_{l}")
        grads[f], delta[f], new_m[f], new_v[f] = [flip(a) for a in outs] if f == "w_ffn_in" else outs

    early_sum, late_sum = _sum_small(small_parts)
    early_shapes = [(LAYERS, KW, CVW) if k == "conv_w" else w[k].shape for k in _SMALL_EARLY]
    grads["norm1_g"], = _unpack(late_sum, [w["norm1_g"].shape])
    *early, total = _unpack(early_sum, early_shapes + [()])
    for k, g in zip(_SMALL_EARLY, early):
        grads[k] = lax.dynamic_slice_in_dim(g, me * cshard, cshard, axis=2) if k == "conv_w" else g
    for k, (d, nm, nv) in zip(_SMALL, _adam_small([(grads[k], w[k], mo[k], vo[k]) for k in _SMALL])):
        delta[k], new_m[k], new_v[k] = d, nm, nv

    return (total, dx.reshape(1, t, D), *[grads[k] for k in _NAMES], *[delta[k] for k in _NAMES],
            *[new_m[k] for k in _NAMES], *[new_v[k] for k in _NAMES])
```

```python
import functools

import numpy as np
import jax
import jax.numpy as jnp
from jax import lax
from jax.experimental import pallas as pl
from jax.experimental.pallas import tpu as pltpu

F32, BF16 = jnp.float32, jnp.bfloat16
S = jax.ShapeDtypeStruct

D = 1024
INW = 3072
Q_BLK, K_BLK, V_BLK, G_BLK = 1, 2, 3, 4
CONV_A_BLK, CONV_GATE_BLK = 10, 11
GMW = 256
RETW = 512
CVW = 256
HEADS = 4
DH = 128
C = 128
KW = 31
HALO = 16
FFH = 2816
NDEV = 8
FFB = 2 * FFH // NDEV
FF_CHUNKS = ((0, 768), (768, 1536), (1536, 2304), (2304, FFH))
EPS = 1e-6
LAYERS = 2
SCALE = DH ** -0.5
VMEM_LIMIT = 56 * 1024 * 1024

ADAM_LR, ADAM_B1, ADAM_B2, ADAM_EPS, ADAM_WD, ADAM_STEP = 0.001, 0.9, 0.999, 1e-08, 0.01, 10

_SQRT_HALF = 0.7071067811865476
_INV_SQRT_2PI = 0.3989422804014327


def _params(*sem):
    return pltpu.CompilerParams(dimension_semantics=sem or None, vmem_limit_bytes=VMEM_LIMIT)


def _resident(shape, index_map):
    return pl.BlockSpec(shape, index_map, pipeline_mode=pl.Buffered(1))


def _dot(a, b):
    return jnp.dot(a, b, preferred_element_type=F32)


def _dot_nt(a, b):
    return lax.dot_general(a, b, (((1,), (1,)), ((), ())), preferred_element_type=F32)


def _dot_tn(a, b):
    return lax.dot_general(a, b, (((0,), (0,)), ((), ())), preferred_element_type=F32)


def _sigmoid(x):
    return 1.0 / (1.0 + jnp.exp(-x))


def _gelu_and_grad(x):
    cdf = 0.5 * (1.0 + lax.erf(x * _SQRT_HALF))
    return x * cdf, cdf + x * jnp.exp(-0.5 * x * x) * _INV_SQRT_2PI


def _silu_and_grad(x):
    s = _sigmoid(x)
    return x * s, s * (1.0 + x * (1.0 - s))


def _standardize(x):
    mu = jnp.mean(x, axis=-1, keepdims=True)
    d = x - mu
    rstd = lax.rsqrt(jnp.mean(d * d, axis=-1, keepdims=True) + EPS)
    return d * rstd, rstd


def _standardize_bwd(dxhat, xhat, rstd):
    m1 = jnp.mean(dxhat, axis=-1, keepdims=True)
    m2 = jnp.mean(dxhat * xhat, axis=-1, keepdims=True)
    return rstd * (dxhat - m1 - xhat * m2)


def _rms(x):
    return lax.rsqrt(jnp.mean(x * x, axis=-1, keepdims=True) + EPS)


def _rmsnorm_bwd(dy, x, r, g):
    u = dy * g
    return r * u - x * (r * r * r) * jnp.mean(u * x, axis=-1, keepdims=True)


def _col_sum(a):
    return jnp.sum(a, axis=0, keepdims=True)


def _rot(t, cos2, sin2):
    return t * cos2 + pltpu.roll(t, DH // 2, axis=1) * sin2


def _rot_t(dt, cos2, sin2):
    return dt * cos2 + pltpu.roll(dt * sin2, DH // 2, axis=1)


MESH = pl.DeviceIdType.MESH
_HBM = pl.BlockSpec(memory_space=pltpu.HBM)
_ANY = pl.BlockSpec(memory_space=pl.ANY)


def _place():
    x, y, c = lax.axis_index("x"), lax.axis_index("y"), lax.axis_index("c")
    return x, y, c, ((1 - x, y), (x, 1 - y), (1 - x, 1 - y))


def _slot(full, kind, width, i):
    if kind == "cols":
        return full.at[:, pl.ds(pl.multiple_of(i * width, 128), width)]
    if kind == "rows":
        return full.at[pl.ds(pl.multiple_of(i * width, 16), width), :]
    return full.at[i]


class _Gather:
    def __init__(self, units):
        self.units = units
        self.inputs = [u[0] for u in units]
        self.out_shape = []
        for src, kind in units:
            r, c = src.shape[-2:]
            shape = {"cols": (r, NDEV * c), "rows": (NDEV * r, c), "lead": (NDEV,) + src.shape}[kind]
            self.out_shape.append(S(shape, src.dtype))
        n = len(units)
        self.scratch = [pltpu.SemaphoreType.DMA((n, 7)), pltpu.SemaphoreType.DMA((n, 7)), pltpu.SemaphoreType.DMA((n,))]

    def run(self, phase, ins, outs, scr):
        ssem, rsem, lsem = scr
        x, y, c, chips = _place()
        me, sib = 4 * x + 2 * y + c, (x, y, 1 - c)
        idx = lambda chip, core: 4 * chip[0] + 2 * chip[1] + core
        for u, (src_arr, kind) in enumerate(self.units):
            src, full = ins[u], outs[u]
            width = src_arr.shape[-1] if kind == "cols" else src_arr.shape[-2]
            slot = functools.partial(_slot, full, kind, width)

            def copy(k, block, to, from_src=False):
                return pltpu.make_async_remote_copy(src_ref=src if from_src else slot(block), dst_ref=slot(block),
                                                    send_sem=ssem.at[u, k], recv_sem=rsem.at[u, k],
                                                    device_id=to, device_id_type=MESH)

            mine = lambda: pltpu.make_async_copy(src, slot(me), lsem.at[u])
            first = lambda: [copy(0, me, sib, True)] + [copy(1 + j, me, (*chip, c), True) for j, chip in enumerate(chips)]
            passed = lambda j: copy(4 + j, idx(chips[j], c), sib)
            if phase == "start":
                mine().start()
                for cp in first():
                    cp.start()
            elif phase == "forward":
                for j, chip in enumerate(chips):
                    copy(1 + j, idx(chip, c), sib).wait_recv()
                    passed(j).start()
            else:
                copy(0, idx((x, y), 1 - c), sib).wait_recv()
                for j, chip in enumerate(chips):
                    copy(4 + j, idx(chip, 1 - c), sib).wait_recv()
                for cp in first() + [passed(j) for j in range(3)]:
                    cp.wait_send()
                mine().wait()


class _Scatter:
    def __init__(self, units):
        self.units = units
        self.inputs = list(units)
        self.out_shape = [S(u.shape, u.dtype) for u in units]
        n = len(units)
        self.scratch = [pltpu.SemaphoreType.DMA((n, 3)), pltpu.SemaphoreType.DMA((n, 3)), pltpu.SemaphoreType.DMA((n,))]

    def run(self, phase, ins, outs, scr):
        ssem, rsem, lsem = scr
        x, y, c, chips = _place()
        myq = 2 * x + y
        for u in range(len(self.units)):
            h, p = ins[u], outs[u]

            def copy(k, chip, send_to_them):
                q = 2 * chip[0] + chip[1]
                return pltpu.make_async_remote_copy(src_ref=h.at[q], dst_ref=p.at[myq if send_to_them else q],
                                                    send_sem=ssem.at[u, k], recv_sem=rsem.at[u, k],
                                                    device_id=(*chip, c), device_id_type=MESH)

            mine = lambda: pltpu.make_async_copy(h.at[myq], p.at[myq], lsem.at[u])
            sends = lambda: [copy(k, chip, True) for k, chip in enumerate(chips)]
            if phase == "start":
                mine().start()
                for cp in sends():
                    cp.start()
            elif phase == "finish":
                for k, chip in enumerate(chips):
                    copy(k, chip, False).wait_recv()
                for cp in sends():
                    cp.wait_send()
                mine().wait()


class _Comms:
    def __init__(self, parts):
        self.parts = parts
        self.inputs = [a for p in parts for a in p.inputs]
        self.out_shape = [a for p in parts for a in p.out_shape]
        self.scratch = [a for p in parts for a in p.scratch]

    def run(self, phase, ins, outs, scr):
        i = o = s = 0
        for p in self.parts:
            ni, no, ns = len(p.inputs), len(p.out_shape), len(p.scratch)
            p.run(phase, ins[i:i + ni], outs[o:o + no], scr[s:s + ns])
            i, o, s = i + ni, o + no, s + ns


def _launch(body, grid, in_specs, out_specs, out_shape, scratch, args, name, sem, comm=None):
    if comm is None:
        outs = pl.pallas_call(body, grid=grid, name=name, in_specs=in_specs, out_specs=out_specs, out_shape=out_shape,
                              scratch_shapes=scratch, compiler_params=_params(*sem))(*args)
        return list(outs), []
    n_in, n_out, n_scr = len(args), len(out_shape), len(scratch)
    ci, co = len(comm.inputs), len(comm.out_shape)
    nsteps = int(np.prod(grid))
    fwd_step = (7 * nsteps) // 8

    def hosted(*refs):
        a = refs[:n_in]
        ca = refs[n_in:n_in + ci]
        o = refs[n_in + ci:n_in + ci + n_out]
        cout = refs[n_in + ci + n_out:n_in + ci + n_out + co]
        s = refs[n_in + ci + n_out + co:n_in + ci + n_out + co + n_scr]
        cs = refs[n_in + ci + n_out + co + n_scr:]
        step = pl.program_id(0)
        for d in range(1, len(grid)):
            step = step * grid[d] + pl.program_id(d)

        @pl.when(step == 0)
        def _():
            comm.run("start", ca, cout, cs)

        body(*a, *o, *s)

        @pl.when(step == fwd_step)
        def _():
            comm.run("forward", ca, cout, cs)

        @pl.when(step == nsteps - 1)
        def _():
            comm.run("finish", ca, cout, cs)

    outs = pl.pallas_call(
        hosted, grid=grid, name=name, in_specs=list(in_specs) + [_HBM] * ci, out_specs=list(out_specs) + [_HBM] * co,
        out_shape=list(out_shape) + comm.out_shape, scratch_shapes=list(scratch) + comm.scratch,
        compiler_params=_params(*["arbitrary"] * len(grid)))(*args, *comm.inputs)
    return list(outs[:n_out]), list(outs[n_out:])


def _comm_only(comm, name):
    ci, co = len(comm.inputs), len(comm.out_shape)

    def body(*refs):
        ca, cout, cs = refs[:ci], refs[ci:ci + co], refs[ci + co:]
        for phase in ("start", "forward", "finish"):
            comm.run(phase, ca, cout, cs)

    return pl.pallas_call(body, name=name, in_specs=[_HBM] * ci, out_specs=[_HBM] * co, out_shape=comm.out_shape,
                          scratch_shapes=comm.scratch, compiler_params=_params())(*comm.inputs)


def _ret_consts():
    idx = np.arange(C, dtype=np.float32)
    gf = (1.0 - np.exp2(-5.0 - np.arange(HEADS, dtype=np.float32))).astype(np.float32)
    out = {}
    for name, gamma, fwd in (("f", gf, True), ("b", gf[::-1].copy(), False)):
        lg = np.log(gamma).astype(np.float32)[:, None]
        diff = idx[:, None] - idx[None, :]
        if fwd:
            mask = diff >= 0
            dist = np.where(mask, diff, 0.0)
            zeta = np.exp(lg * (C - 1 - idx))
            xi = np.exp(lg * (idx + 1))
        else:
            mask = diff < 0
            dist = np.where(mask, -diff, 0.0)
            zeta = np.exp(lg * idx)
            xi = np.exp(lg * (C - idx))
        dm = np.where(mask[None], np.exp(lg[:, :, None] * dist[None]), 0.0).astype(np.float32)
        bc = lambda vec: np.ascontiguousarray(np.broadcast_to(vec.astype(np.float32)[:, :, None], (HEADS, C, DH)))
        out[name] = dict(D=dm, XI=bc(xi), ZETA=bc(zeta), gC=[float(v) for v in np.exp(lg[:, 0] * C).astype(np.float32)])
    return out


def _rope_tables(t):
    half = DH // 2
    inv_freq = (np.float32(10000.0) ** (-np.arange(half, dtype=np.float32) / np.float32(half))).astype(np.float32)
    ang = (np.arange(t, dtype=np.float32)[:, None] * inv_freq[None, :]).astype(np.float64)
    cos, sin = np.cos(ang).astype(np.float32), np.sin(ang).astype(np.float32)
    return np.concatenate([cos, cos], axis=1), np.concatenate([-sin, sin], axis=1)


def _f_inproj(x, g1, w, cos2, sin2, l, name, comm=None):
    t = x.shape[0]
    tm = 512

    def body(x_ref, g_ref, w_ref, cos_ref, sin_ref, proj_ref, ht_ref):
        xv = x_ref[...]
        h = (xv * _rms(xv) * g_ref[...]).astype(BF16)
        ht_ref[...] = h.T
        for nb in range(INW // 512):
            cs = slice(nb * 512, (nb + 1) * 512)
            res = _dot(h, w_ref[:, cs])
            if nb in (Q_BLK, K_BLK):
                for hh in range(HEADS):
                    r = _rot(res[:, hh * DH:(hh + 1) * DH], cos_ref[...], sin_ref[...])
                    proj_ref[:, nb * 512 + hh * DH:nb * 512 + (hh + 1) * DH] = (r * SCALE if nb == K_BLK else r).astype(BF16)
            else:
                proj_ref[:, cs] = res.astype(BF16)

    return _launch(
        body, (t // tm,),
        [pl.BlockSpec((tm, D), lambda i: (i, 0)),
         pl.BlockSpec((None, 1, D), lambda i: (l, 0, 0)),
         _resident((D, INW), lambda i: (0, 0)),
         pl.BlockSpec((tm, DH), lambda i: (i, 0)), pl.BlockSpec((tm, DH), lambda i: (i, 0))],
        [pl.BlockSpec((tm, INW), lambda i: (i, 0)), pl.BlockSpec((D, tm), lambda i: (0, i))],
        [S((t, INW), BF16), S((D, t), BF16)], [], (x, g1, w, cos2, sin2), name, ("parallel",), comm)


def _gm_chunk_fwd(u, v, lng, lnb, ws_ref, bias):
    au, dau = _gelu_and_grad(u)
    av, dav = _gelu_and_grad(v)
    vhat, rstd = _standardize(av)
    vn = (vhat * lng + lnb).astype(BF16)
    head = lax.broadcasted_iota(jnp.int32, (C, GMW), 1) // (GMW // HEADS)
    mixed = bias
    for h in range(HEADS):
        mixed = mixed + jnp.where(head == h, _dot(ws_ref[h], vn), 0.0)
    return au, dau, dav, vhat, rstd, vn, mixed, head


def _f_gm(proj, lng, lnb, ws_bf, bias, l, name):
    t = proj.shape[0]
    tm = min(1024, t)

    def body(p_ref, lng_ref, lnb_ref, ws_ref, bias_ref, y_ref):
        for ci in range(tm // C):
            rows = slice(ci * C, (ci + 1) * C)
            u = p_ref[rows, 0:GMW].astype(F32)
            v = p_ref[rows, GMW:2 * GMW].astype(F32)
            au, _, _, _, _, _, mixed, _ = _gm_chunk_fwd(u, v, lng_ref[...], lnb_ref[...], ws_ref, bias_ref[...])
            y_ref[rows, :] = (au * mixed).astype(BF16)

    return pl.pallas_call(
        body, grid=(t // tm,), name=name,
        in_specs=[pl.BlockSpec((tm, 2 * GMW), lambda i: (i, 0)),
                  pl.BlockSpec((None, 1, GMW), lambda i: (l, 0, 0)),
                  pl.BlockSpec((None, 1, GMW), lambda i: (l, 0, 0)),
                  pl.BlockSpec((None, HEADS, C, C), lambda i: (l, 0, 0, 0)),
                  pl.BlockSpec((None, C, GMW), lambda i: (l, 0, 0))],
        out_specs=pl.BlockSpec((tm, GMW), lambda i: (i, 0)),
        out_shape=S((t, GMW), BF16),
        compiler_params=_params("parallel"),
    )(proj, lng, lnb, ws_bf, bias)


def _scan_pair(proj, other, col, wf, wb, gcf, gcb, first_is_f, name):
    t = proj.shape[0]
    n = t // C
    sc = min(SCAN_CHUNKS, n)
    nsteps = n // sc
    other_is_proj = other is None

    def body(a1, o1, a2, o2, w1_ref, w2_ref, out1, out2, st1, st2):
        @pl.when(pl.program_id(0) == 0)
        def _():
            st1[...] = jnp.zeros_like(st1)
            st2[...] = jnp.zeros_like(st2)

        def one(a_ref, o_ref, w_ref, gc, st, out, order):
            for h in range(HEADS):
                sl = slice(h * DH, (h + 1) * DH)
                incs = {}
                for j in order:
                    rows = slice(j * C, (j + 1) * C)
                    aw = (a_ref[rows, sl].astype(F32) * w_ref[h]).astype(BF16)
                    incs[j] = _dot_tn(aw, o_ref[rows, sl].astype(BF16))
                cur = st[h]
                for j in order:
                    out[j, h] = cur.astype(BF16)
                    cur = gc[h] * cur + incs[j]
                st[h] = cur

        g1, g2 = (gcf, gcb) if first_is_f else (gcb, gcf)
        one(a1, o1, w1_ref, g1, st1, out1, range(sc))
        one(a2, o2, w2_ref, g2, st2, out2, range(sc - 1, -1, -1))

    up = lambda i: i
    down = lambda i: nsteps - 1 - i

    def specs(ix):
        o_spec = pl.BlockSpec((sc * C, RETW), lambda i: (ix(i), V_BLK if other_is_proj else 0))
        return [pl.BlockSpec((sc * C, RETW), lambda i: (ix(i), col)), o_spec]

    const = lambda: pl.BlockSpec((HEADS, C, DH), lambda i: (0, 0, 0))
    oth = proj if other_is_proj else other
    w1, w2 = (wf, wb) if first_is_f else (wb, wf)
    out1, out2 = pl.pallas_call(
        body, grid=(nsteps,), name=name,
        in_specs=specs(up) + specs(down) + [const(), const()],
        out_specs=[pl.BlockSpec((sc, HEADS, DH, DH), lambda i: (up(i), 0, 0, 0)),
                   pl.BlockSpec((sc, HEADS, DH, DH), lambda i: (down(i), 0, 0, 0))],
        out_shape=[S((n, HEADS, DH, DH), BF16), S((n, HEADS, DH, DH), BF16)],
        scratch_shapes=[pltpu.VMEM((HEADS, DH, DH), F32), pltpu.VMEM((HEADS, DH, DH), F32)],
        compiler_params=_params("arbitrary"),
    )(proj, oth, proj, oth, w1, w2)
    return (out1, out2) if first_is_f else (out2, out1)


SCAN_CHUNKS = 16


RET_CHUNKS = 8


def _f_ret_out(proj, s_f, s_b, rc, name):
    t = proj.shape[0]
    nchunks = min(RET_CHUNKS, t // C)
    tm = nchunks * C

    def body(q_ref, k_ref, v_ref, sf_ref, sb_ref, d_ref, xif_ref, xib_ref, o_ref):
        for ci in range(nchunks):
            rows = slice(ci * C, (ci + 1) * C)
            for h in range(HEADS):
                sl = slice(h * DH, (h + 1) * DH)
                qh, kh, vh = q_ref[rows, sl], k_ref[rows, sl], v_ref[rows, sl]
                p = (_dot_nt(qh, kh) * d_ref[h]).astype(BF16)
                cross = _dot(qh, jnp.concatenate([sf_ref[ci, h], sb_ref[ci, h]], axis=1))
                o_ref[rows, sl] = _dot(p, vh) + xif_ref[h] * cross[:, 0:DH] + xib_ref[h] * cross[:, DH:2 * DH]

    const = lambda: pl.BlockSpec((HEADS, C, DH), lambda i: (0, 0, 0))
    state = lambda: pl.BlockSpec((nchunks, HEADS, DH, DH), lambda i: (i, 0, 0, 0))
    return pl.pallas_call(
        body, grid=(t // tm,), name=name,
        in_specs=[pl.BlockSpec((tm, RETW), lambda i, cb=cb: (i, cb)) for cb in (Q_BLK, K_BLK, V_BLK)] +
                 [state(), state(), const(), const(), const()],
        out_specs=pl.BlockSpec((tm, RETW), lambda i: (i, 0)),
        out_shape=S((t, RETW), F32),
        compiler_params=_params("parallel"),
    )(proj, proj, proj, s_f, s_b, rc["f"]["D"] + rc["b"]["D"], rc["f"]["XI"], rc["b"]["XI"])


def _conv_halo_specs(t, tm, width, col):
    r = tm // HALO
    last = t // HALO - 1
    return [pl.BlockSpec((HALO, width), lambda i: (jnp.maximum(i * r - 1, 0), col)),
            pl.BlockSpec((tm, width), lambda i: (i, col)),
            pl.BlockSpec((HALO, width), lambda i: (jnp.minimum((i + 1) * r, last), col))]


def _fill_ext(ext, prev, cur, nxt, i, nt, tm):
    ext[0:HALO, :] = jnp.where(i > 0, prev, 0.0)
    ext[HALO:HALO + tm, :] = cur
    ext[HALO + tm:2 * HALO + tm, :] = jnp.where(i < nt - 1, nxt, 0.0)


def _glu(a_ref, g_ref):
    return a_ref[...].astype(F32) * _sigmoid(g_ref[...].astype(F32))


def _shifted_copies(ext, rot, tm):
    rows = tm + 2 * HALO - 8
    for b in range(1, 8):
        rot[b - 1, :, :] = ext[pl.ds(b, rows), :]


def _tap(ext, rot, r0, s, rb):
    a, b = divmod(s, 8)
    return ext[pl.ds(r0 + 8 * a, rb), :] if b == 0 else rot[b - 1, pl.ds(r0 + 8 * a, rb), :]


def _f_conv(proj, cw, cb, lng, lnb, l, name, comm=None):
    t = proj.shape[0]
    tm = 512
    nt = t // tm
    rb = 64

    def body(ap, ac, an, gp, gc, gn, cw_ref, cb_ref, lng_ref, lnb_ref, c_ref, y_ref, hext, hrot):
        i = pl.program_id(0)
        _fill_ext(hext, _glu(ap, gp), _glu(ac, gc), _glu(an, gn), i, nt, tm)
        _shifted_copies(hext, hrot, tm)
        for r0 in range(0, tm, rb):
            acc = jnp.zeros((rb, CVW), F32) + cb_ref[...]
            for j in range(KW):
                acc = acc + cw_ref[j:j + 1, :] * _tap(hext, hrot, r0, j + 1, rb)
            c_ref[r0:r0 + rb, :] = acc
            chat, _ = _standardize(acc)
            z = chat * lng_ref[...] + lnb_ref[...]
            y_ref[r0:r0 + rb, :] = (z * _sigmoid(z)).astype(BF16)

    vec = lambda: pl.BlockSpec((None, 1, CVW), lambda i: (l, 0, 0))
    return _launch(
        body, (nt,),
        _conv_halo_specs(t, tm, CVW, CONV_A_BLK) + _conv_halo_specs(t, tm, CVW, CONV_GATE_BLK) +
        [pl.BlockSpec((None, 32, CVW), lambda i: (l, 0, 0)), vec(), vec(), vec()],
        [pl.BlockSpec((tm, CVW), lambda i: (i, 0)), pl.BlockSpec((tm, CVW), lambda i: (i, 0))],
        [S((t, CVW), F32), S((t, CVW), BF16)],
        [pltpu.VMEM((tm + 2 * HALO, CVW), F32), pltpu.VMEM((7, tm + 2 * HALO - 8, CVW), F32)],
        (proj, proj, proj, proj, proj, proj, cw, cb, lng, lnb), name, ("parallel",), comm)


def _f_mixout(x, y_gm, y_cv, o, proj, w, name, comm=None):
    t = x.shape[0]
    tm = 512

    def body(x_ref, ygm_ref, ycv_ref, o_ref, g_ref, w_ref, xm_ref, ycat_t_ref, ycat):
        ycat[:, 0:GMW] = ygm_ref[...]
        ycat[:, GMW + RETW:D] = ycv_ref[...]
        for h in range(HEADS):
            sl = slice(h * DH, (h + 1) * DH)
            ohat, _ = _standardize(o_ref[:, sl])
            g = g_ref[:, sl].astype(F32)
            ycat[:, GMW + h * DH:GMW + (h + 1) * DH] = (ohat * (g * _sigmoid(g))).astype(BF16)
        yc = ycat[...]
        ycat_t_ref[...] = yc.T
        xm_ref[...] = x_ref[...] + _dot(yc, w_ref[...])

    return _launch(
        body, (t // tm,),
        [pl.BlockSpec((tm, D), lambda i: (i, 0)),
         pl.BlockSpec((tm, GMW), lambda i: (i, 0)),
         pl.BlockSpec((tm, CVW), lambda i: (i, 0)),
         pl.BlockSpec((tm, RETW), lambda i: (i, 0)),
         pl.BlockSpec((tm, RETW), lambda i: (i, G_BLK)),
         _resident((D, D), lambda i: (0, 0))],
        [pl.BlockSpec((tm, D), lambda i: (i, 0)), pl.BlockSpec((D, tm), lambda i: (0, i))],
        [S((t, D), F32), S((D, t), BF16)],
        [pltpu.VMEM((tm, D), BF16)], (x, y_gm, y_cv, o, proj, w), name, ("parallel",), comm)


def _f_ffn(xm, g2, w1, w2, l, name, comm=None):
    t = xm.shape[0]
    tm = 512

    def body(x_ref, g_ref, w1_ref, w2_ref, xo_ref, h_ref, gu_ref, act_t_ref):
        xv = x_ref[...]
        h = (xv * _rms(xv) * g_ref[...]).astype(BF16)
        h_ref[...] = h
        acc = xv
        for a, b in FF_CHUNKS:
            gate = _dot_nt(h, w1_ref[a:b, :])
            up = _dot_nt(h, w1_ref[FFH + a:FFH + b, :])
            gu_ref[:, a:b] = gate.astype(BF16)
            gu_ref[:, FFH + a:FFH + b] = up.astype(BF16)
            av = ((gate * _sigmoid(gate)) * up).astype(BF16)
            act_t_ref[a:b, :] = av.T
            acc = acc + _dot(av, w2_ref[a:b, :])
        xo_ref[...] = acc

    return _launch(
        body, (t // tm,),
        [pl.BlockSpec((tm, D), lambda i: (i, 0)),
         pl.BlockSpec((None, 1, D), lambda i: (l, 0, 0)),
         _resident((2 * FFH, D), lambda i: (0, 0)),
         _resident((FFH, D), lambda i: (0, 0))],
        [pl.BlockSpec((tm, D), lambda i: (i, 0)), pl.BlockSpec((tm, D), lambda i: (i, 0)),
         pl.BlockSpec((tm, 2 * FFH), lambda i: (i, 0)), pl.BlockSpec((FFH, tm), lambda i: (0, i))],
        [S((t, D), F32), S((t, D), BF16), S((t, 2 * FFH), BF16), S((FFH, t), BF16)],
        [], (xm, g2, w1, w2), name, ("parallel",), comm)


def _b_ffn(top, xm, g2, gu, w1, w2, l, name, comm=None):
    t = xm.shape[0]
    tm = 256
    from_loss = isinstance(top, tuple)
    n_top = 3 if from_loss else 1

    def body(*refs):
        top_refs = refs[:n_top]
        x_ref, g_ref, gu_ref, w1_ref, w2_ref, dgu_ref, dxm_ref, dxb_ref, dg_ref = refs[n_top:n_top + 9]
        first = pl.program_id(0) == 0

        @pl.when(first)
        def _():
            dg_ref[...] = jnp.zeros_like(dg_ref)

        if from_loss:
            xo_ref, fg_ref, t_ref = top_refs
            loss_ref, dfg_ref = refs[n_top + 9:n_top + 11]

            @pl.when(first)
            def _():
                loss_ref[...] = jnp.zeros_like(loss_ref)
                dfg_ref[...] = jnp.zeros_like(dfg_ref)

            xo = xo_ref[...]
            ro = _rms(xo)
            xr = xo * ro
            err = xr * fg_ref[...] - t_ref[...]
            loss_ref[...] += (0.5 / D) * _col_sum(jnp.sum(err * err, axis=1, keepdims=True))
            dy = err * (1.0 / D)
            dfg_ref[...] += _col_sum(dy * xr)
            dxo = _rmsnorm_bwd(dy, xo, ro, fg_ref[...])
        else:
            dxo = top_refs[0][...]
        dxb = dxo.astype(BF16)
        dxb_ref[...] = dxb
        dh = jnp.zeros((tm, D), F32)
        for a, b in FF_CHUNKS:
            dact = _dot_nt(dxb, w2_ref[a:b, :])
            gate = gu_ref[:, a:b].astype(F32)
            up = gu_ref[:, FFH + a:FFH + b].astype(F32)
            sg, dsg = _silu_and_grad(gate)
            dgate = (dact * up * dsg).astype(BF16)
            dup = (dact * sg).astype(BF16)
            dgu_ref[a:b, :] = dgate.T
            dgu_ref[FFH + a:FFH + b, :] = dup.T
            dh = dh + _dot(dgate, w1_ref[a:b, :]) + _dot(dup, w1_ref[FFH + a:FFH + b, :])
        xv = x_ref[...]
        r = _rms(xv)
        dg_ref[...] += _col_sum(dh * xv * r)
        dxm_ref[...] = dxo + _rmsnorm_bwd(dh, xv, r, g_ref[...])

    tok = lambda: pl.BlockSpec((tm, D), lambda i: (i, 0))
    vec = lambda: pl.BlockSpec((1, D), lambda i: (0, 0))
    top_specs = [tok(), vec(), tok()] if from_loss else [tok()]
    extra_specs = [pl.BlockSpec((1, 1), lambda i: (0, 0)), vec()] if from_loss else []
    extra_shape = [S((1, 1), F32), S((1, D), F32)] if from_loss else []
    return _launch(
        body, (t // tm,),
        top_specs + [tok(), pl.BlockSpec((None, 1, D), lambda i: (l, 0, 0)),
                     pl.BlockSpec((tm, 2 * FFH), lambda i: (i, 0)),
                     _resident((2 * FFH, D), lambda i: (0, 0)),
                     _resident((FFH, D), lambda i: (0, 0))],
        [pl.BlockSpec((2 * FFH, tm), lambda i: (0, i)), tok(), tok(), vec()] + extra_specs,
        [S((2 * FFH, t), BF16), S((t, D), F32), S((t, D), BF16), S((1, D), F32)] + extra_shape,
        [],
        ((*top,) if from_loss else (top,)) + (xm, g2, gu, w1, w2), name, ("arbitrary",), comm)


def _mm_wgrad(at, b, pieces, at_rows, b_mode, group, name):
    bt = 2048
    t = at.shape[-1]
    bt = min(bt, t)
    nt = t // bt
    if at_rows:
        ka = at.shape[0] // pieces
        a_spec = pl.BlockSpec((ka, bt), lambda j, tt: (j, tt))
    else:
        ka = at.shape[0]
        a_spec = pl.BlockSpec((ka, bt), lambda j, tt: (0, tt))
    if b_mode == "shared":
        nb, b_spec = b.shape[1], pl.BlockSpec((bt, b.shape[1]), lambda j, tt: (tt, 0))
    elif b_mode == "cols":
        nb = b.shape[1] // pieces
        b_spec = pl.BlockSpec((bt, group * nb), lambda j, tt: (tt, j))
    else:
        nb, b_spec = b.shape[2], pl.BlockSpec((None, bt, b.shape[2]), lambda j, tt: (j, tt, 0))
    assert group == 1 or b_mode == "cols"

    def body(a_ref, b_ref, o_ref, acc):
        tt = pl.program_id(1)

        @pl.when(tt == 0)
        def _():
            acc[...] = jnp.zeros_like(acc)

        acc[...] += _dot(a_ref[...], b_ref[...])

        @pl.when(tt == nt - 1)
        def _():
            for k in range(group):
                o_ref[k] = acc[:, k * nb:(k + 1) * nb].astype(BF16)

    return pl.pallas_call(
        body, grid=(pieces // group, nt), name=name,
        in_specs=[a_spec, b_spec],
        out_specs=pl.BlockSpec((group, ka, nb), lambda j, tt: (j, 0, 0)),
        out_shape=S((pieces, ka, nb), BF16),
        scratch_shapes=[pltpu.VMEM((ka, group * nb), F32)],
        compiler_params=_params("parallel", "arbitrary"),
    )(at, b)


def _b_mixout(dxm, w, o, proj, c, lng, lnb, l, name, comm=None):
    t = dxm.shape[0]
    tm = 512

    def body(dxm_ref, w_ref, o_ref, g_ref, c_ref, lng_ref, lnb_ref,
             dxb_ref, dygm_ref, dO_ref, dg_ref, dc_ref, dlg_ref, dlb_ref, dcb_ref):
        @pl.when(pl.program_id(0) == 0)
        def _():
            dlg_ref[...] = jnp.zeros_like(dlg_ref)
            dlb_ref[...] = jnp.zeros_like(dlb_ref)
            dcb_ref[...] = jnp.zeros_like(dcb_ref)

        dxb = dxm_ref[...].astype(BF16)
        dxb_ref[...] = dxb
        dy = _dot_nt(dxb, w_ref[...])
        dygm_ref[...] = dy[:, 0:GMW]
        for h in range(HEADS):
            sl = slice(h * DH, (h + 1) * DH)
            ohat, rstd = _standardize(o_ref[:, sl])
            sg, dsg = _silu_and_grad(g_ref[:, sl].astype(F32))
            dyr = dy[:, GMW + h * DH:GMW + (h + 1) * DH]
            dg_ref[:, sl] = (dyr * ohat * dsg).astype(BF16)
            dO_ref[:, sl] = _standardize_bwd(dyr * sg, ohat, rstd)
        chat, rstd = _standardize(c_ref[...])
        z = chat * lng_ref[...] + lnb_ref[...]
        _, dsz = _silu_and_grad(z)
        dz = dy[:, GMW + RETW:D] * dsz
        dlg_ref[...] += _col_sum(dz * chat)
        dlb_ref[...] += _col_sum(dz)
        dc = _standardize_bwd(dz * lng_ref[...], chat, rstd)
        dcb_ref[...] += _col_sum(dc)
        dc_ref[...] = dc

    vec = lambda: pl.BlockSpec((None, 1, CVW), lambda i: (l, 0, 0))
    acc = lambda: pl.BlockSpec((1, CVW), lambda i: (0, 0))
    return _launch(
        body, (t // tm,),
        [pl.BlockSpec((tm, D), lambda i: (i, 0)),
         _resident((D, D), lambda i: (0, 0)),
         pl.BlockSpec((tm, RETW), lambda i: (i, 0)),
         pl.BlockSpec((tm, RETW), lambda i: (i, G_BLK)),
         pl.BlockSpec((tm, CVW), lambda i: (i, 0)), vec(), vec()],
        [pl.BlockSpec((tm, D), lambda i: (i, 0)), pl.BlockSpec((tm, GMW), lambda i: (i, 0)),
         pl.BlockSpec((tm, RETW), lambda i: (i, 0)), pl.BlockSpec((tm, RETW), lambda i: (i, 0)),
         pl.BlockSpec((tm, CVW), lambda i: (i, 0)), acc(), acc(), acc()],
        [S((t, D), BF16), S((t, GMW), F32), S((t, RETW), F32), S((t, RETW), BF16), S((t, CVW), F32),
         S((1, CVW), F32), S((1, CVW), F32), S((1, CVW), F32)],
        [], (dxm, w, o, proj, c, lng, lnb), name, ("arbitrary",), comm)


def _b_gm(proj, dy, lng, lnb, ws_bf, wst_bf, bias, l, name, comm=None):
    t = proj.shape[0]
    tm = min(1024, t)
    nt = t // tm

    def body(p_ref, dy_ref, lng_ref, lnb_ref, ws_ref, wst_ref, bias_ref,
             duv_ref, dws_ref, dbias_ref, dbs_ref, dlg_ref, dlb_ref):
        @pl.when(pl.program_id(0) == 0)
        def _():
            dws_ref[...] = jnp.zeros_like(dws_ref)
            dbias_ref[...] = jnp.zeros_like(dbias_ref)
            dbs_ref[...] = jnp.zeros_like(dbs_ref)
            dlg_ref[...] = jnp.zeros_like(dlg_ref)
            dlb_ref[...] = jnp.zeros_like(dlb_ref)

        for ci in range(tm // C):
            rows = slice(ci * C, (ci + 1) * C)
            u = p_ref[rows, 0:GMW].astype(F32)
            v = p_ref[rows, GMW:2 * GMW].astype(F32)
            au, dau, dav, vhat, rstd, vn, mixed, head = _gm_chunk_fwd(u, v, lng_ref[...], lnb_ref[...], ws_ref, bias_ref[...])
            dyc = dy_ref[rows, :]
            dmixed = dyc * au
            dmb = dmixed.astype(BF16)
            dbias_ref[...] += dmixed
            dvn = jnp.zeros((C, GMW), F32)
            for h in range(HEADS):
                dws_ref[h] += _dot_nt(jnp.where(head == h, dmixed, 0.0).astype(BF16), vn)
                dvn = dvn + jnp.where(head == h, _dot(wst_ref[h], dmb), 0.0)
            dlg_ref[...] += _col_sum(dvn * vhat)
            dlb_ref[...] += _col_sum(dvn)
            dav_in = _standardize_bwd(dvn * lng_ref[...], vhat, rstd)
            duv_ref[rows, 0:GMW] = (dyc * mixed * dau).astype(BF16)
            duv_ref[rows, GMW:2 * GMW] = (dav_in * dav).astype(BF16)

        @pl.when(pl.program_id(0) == nt - 1)
        def _():
            head = lax.broadcasted_iota(jnp.int32, (C, GMW), 1) // (GMW // HEADS)
            lane = lax.broadcasted_iota(jnp.int32, (C, 128), 1)
            fold = jnp.zeros((C, 128), F32)
            for h in range(HEADS):
                col = jnp.sum(jnp.where(head == h, dbias_ref[...], 0.0), axis=1, keepdims=True)
                fold = jnp.where(lane == h, col, fold)
            dbs_ref[...] = fold

    vec = lambda: pl.BlockSpec((None, 1, GMW), lambda i: (l, 0, 0))
    mats = lambda: pl.BlockSpec((None, HEADS, C, C), lambda i: (l, 0, 0, 0))
    return _launch(
        body, (nt,),
        [pl.BlockSpec((tm, 2 * GMW), lambda i: (i, 0)), pl.BlockSpec((tm, GMW), lambda i: (i, 0)),
         vec(), vec(), mats(), mats(), pl.BlockSpec((None, C, GMW), lambda i: (l, 0, 0))],
        [pl.BlockSpec((tm, 2 * GMW), lambda i: (i, 0)),
         pl.BlockSpec((HEADS, C, C), lambda i: (0, 0, 0)),
         pl.BlockSpec((C, GMW), lambda i: (0, 0)), pl.BlockSpec((C, 128), lambda i: (0, 0)),
         pl.BlockSpec((1, GMW), lambda i: (0, 0)), pl.BlockSpec((1, GMW), lambda i: (0, 0))],
        [S((t, 2 * GMW), BF16), S((HEADS, C, C), F32), S((C, GMW), F32), S((C, 128), F32),
         S((1, GMW), F32), S((1, GMW), F32)],
        [], (proj, dy, lng, lnb, ws_bf, wst_bf, bias), name, ("arbitrary",), comm)


def _b_conv(proj, dc, cw, l, name, comm=None):
    t = proj.shape[0]
    tm = 256
    nt = t // tm
    rb = 64

    def body(ap, ac, an, gp, gc, gn, dp, dcur, dn, cw_ref, dag_ref, dcw_ref, hext, dext, hrot, drot):
        i = pl.program_id(0)

        @pl.when(i == 0)
        def _():
            dcw_ref[...] = jnp.zeros_like(dcw_ref)

        _fill_ext(hext, _glu(ap, gp), _glu(ac, gc), _glu(an, gn), i, nt, tm)
        _fill_ext(dext, dp[...], dcur[...], dn[...], i, nt, tm)
        _shifted_copies(hext, hrot, tm)
        _shifted_copies(dext, drot, tm)
        for j in range(KW):
            dcw_ref[j:j + 1, :] += _col_sum(dcur[...] * _tap(hext, hrot, 0, j + 1, tm))
        for r0 in range(0, tm, rb):
            dh = jnp.zeros((rb, CVW), F32)
            for j in range(KW):
                dh = dh + cw_ref[j:j + 1, :] * _tap(dext, drot, r0, 2 * HALO - 1 - j, rb)
            a = ac[r0:r0 + rb, :].astype(F32)
            s = _sigmoid(gc[r0:r0 + rb, :].astype(F32))
            dag_ref[r0:r0 + rb, 0:CVW] = (dh * s).astype(BF16)
            dag_ref[r0:r0 + rb, CVW:2 * CVW] = (dh * a * s * (1.0 - s)).astype(BF16)

    dspecs = _conv_halo_specs(t, tm, CVW, 0)
    return _launch(
        body, (nt,),
        _conv_halo_specs(t, tm, CVW, CONV_A_BLK) + _conv_halo_specs(t, tm, CVW, CONV_GATE_BLK) + dspecs +
        [pl.BlockSpec((None, 32, CVW), lambda i: (l, 0, 0))],
        [pl.BlockSpec((tm, 2 * CVW), lambda i: (i, 0)), pl.BlockSpec((32, CVW), lambda i: (0, 0))],
        [S((t, 2 * CVW), BF16), S((32, CVW), F32)],
        [pltpu.VMEM((tm + 2 * HALO, CVW), F32), pltpu.VMEM((tm + 2 * HALO, CVW), F32),
         pltpu.VMEM((7, tm + 2 * HALO - 8, CVW), F32), pltpu.VMEM((7, tm + 2 * HALO - 8, CVW), F32)],
        (proj, proj, proj, proj, proj, proj, dc, dc, dc, cw), name, ("arbitrary",), comm)


def _b_ret_out(proj, cos2, sin2, dO, s_f, s_b, g_f, g_b, rc, name, comm=None):
    t = proj.shape[0]
    nchunks = min(RET_CHUNKS, t // C)
    tm = nchunks * C

    def body(q_ref, k_ref, v_ref, cos_ref, sin_ref, dO_ref, sf_ref, sb_ref, gf_ref, gb_ref,
             d_ref, xif_ref, xib_ref, zef_ref, zeb_ref, dq_ref, dk_ref, dv_ref):
        for ci in range(nchunks):
            rows = slice(ci * C, (ci + 1) * C)
            cos_v, sin_v = cos_ref[rows, :], sin_ref[rows, :]
            for h in range(HEADS):
                sl = slice(h * DH, (h + 1) * DH)
                qh, kh, vh = q_ref[rows, sl], k_ref[rows, sl], v_ref[rows, sl]
                dOh = dO_ref[rows, sl].astype(BF16)
                dm = d_ref[h]
                p = (_dot_nt(qh, kh) * dm).astype(BF16)
                dp = (_dot_nt(dOh, vh) * dm).astype(BF16)
                from_s = _dot_nt(dOh, jnp.concatenate([sf_ref[ci, h], sb_ref[ci, h]], axis=0))
                from_g = _dot_nt(vh, jnp.concatenate([gf_ref[ci, h], gb_ref[ci, h]], axis=0))
                kg = _dot(kh, jnp.concatenate([gf_ref[ci, h], gb_ref[ci, h]], axis=1))
                dqr = _dot(dp, kh) + xif_ref[h] * from_s[:, 0:DH] + xib_ref[h] * from_s[:, DH:2 * DH]
                dkr = (_dot_tn(dp, qh) + zef_ref[h] * from_g[:, 0:DH] + zeb_ref[h] * from_g[:, DH:2 * DH]) * SCALE
                dv = _dot_tn(p, dOh) + zef_ref[h] * kg[:, 0:DH] + zeb_ref[h] * kg[:, DH:2 * DH]
                dq_ref[rows, sl] = _rot_t(dqr, cos_v, sin_v).astype(BF16)
                dk_ref[rows, sl] = _rot_t(dkr, cos_v, sin_v).astype(BF16)
                dv_ref[rows, sl] = dv.astype(BF16)

    const = lambda: pl.BlockSpec((HEADS, C, DH), lambda i: (0, 0, 0))
    state = lambda: pl.BlockSpec((nchunks, HEADS, DH, DH), lambda i: (i, 0, 0, 0))
    tok = lambda: pl.BlockSpec((tm, RETW), lambda i: (i, 0))
    return _launch(
        body, (t // tm,),
        [pl.BlockSpec((tm, RETW), lambda i, cb=cb: (i, cb)) for cb in (Q_BLK, K_BLK, V_BLK)] +
        [pl.BlockSpec((tm, DH), lambda i: (i, 0)), pl.BlockSpec((tm, DH), lambda i: (i, 0)), tok(),
         state(), state(), state(), state()] + [const() for _ in range(5)],
        [tok(), tok(), tok()],
        [S((t, RETW), BF16) for _ in range(3)], [],
        (proj, proj, proj, cos2, sin2, dO, s_f, s_b, g_f, g_b, rc["f"]["D"] + rc["b"]["D"],
         rc["f"]["XI"], rc["b"]["XI"], rc["f"]["ZETA"], rc["b"]["ZETA"]), name, ("parallel",), comm)


def _b_inproj(d_uv, dqkv, d_g, d_ag, w, x, g1, dxm, l, name, comm=None):
    t = x.shape[0]
    tm = 512

    def body(duv_ref, dq_ref, dk_ref, dv_ref, dg_ref, dag_ref, w_ref, x_ref, g_ref, dxm_ref,
             dp_ref, dx_ref, dn_ref):
        @pl.when(pl.program_id(0) == 0)
        def _():
            dn_ref[...] = jnp.zeros_like(dn_ref)

        for k, part in enumerate((duv_ref, dq_ref, dk_ref, dv_ref, dg_ref, dag_ref)):
            dp_ref[:, 512 * k:512 * (k + 1)] = part[...]
        dh = _dot_nt(dp_ref[...], w_ref[...])
        xv = x_ref[...]
        r = _rms(xv)
        dn_ref[...] += _col_sum(dh * xv * r)
        dx_ref[...] = dxm_ref[...] + _rmsnorm_bwd(dh, xv, r, g_ref[...])

    half = lambda: pl.BlockSpec((tm, 512), lambda i: (i, 0))
    full = lambda: pl.BlockSpec((tm, D), lambda i: (i, 0))
    return _launch(
        body, (t // tm,),
        [half() for _ in range(6)] +
        [_resident((D, INW), lambda i: (0, 0)), full(), pl.BlockSpec((None, 1, D), lambda i: (l, 0, 0)), full()],
        [pl.BlockSpec((tm, INW), lambda i: (i, 0)), full(), pl.BlockSpec((1, D), lambda i: (0, 0))],
        [S((t, INW), BF16), S((t, D), F32), S((1, D), F32)], [],
        (d_uv, *dqkv, d_g, d_ag, w, x, g1, dxm), name, ("arbitrary",), comm)


class _PairSwap:
    def __init__(self, units):
        self.inputs = list(units)
        self.out_shape = [S((4,) + u.shape[1:], BF16) for u in units]
        n = len(units)
        self.scratch = [pltpu.SemaphoreType.DMA((n, 4)), pltpu.SemaphoreType.DMA((n, 4))]

    def run(self, phase, ins, outs, scr):
        ssem, rsem = scr
        x, y, c, _ = _place()
        copies = lambda: [pltpu.make_async_remote_copy(src_ref=ins[u].at[2 * chip + (1 - c)], dst_ref=outs[u].at[chip],
                                                       send_sem=ssem.at[u, chip], recv_sem=rsem.at[u, chip],
                                                       device_id=(x, y, 1 - c), device_id_type=MESH)
                          for u in range(len(self.inputs)) for chip in range(4)]
        if phase == "start":
            for cp in copies():
                cp.start()
        elif phase == "finish":
            for cp in copies():
                cp.wait_recv()
            for cp in copies():
                cp.wait_send()


def _pair_add(g, q, name):
    _, mm, nn = g.shape
    bm = _row_block(mm)

    def body(c_ref, g_ref, q_ref, h_ref):
        h_ref[...] = (g_ref[...].astype(F32) + q_ref[...].astype(F32)).astype(BF16)

    blk = lambda: pl.BlockSpec((None, bm, nn), lambda qq, i, c_ref: (qq, i, 0))
    return pl.pallas_call(
        body, name=name,
        grid_spec=pltpu.PrefetchScalarGridSpec(
            num_scalar_prefetch=1, grid=(4, mm // bm),
            in_specs=[pl.BlockSpec((None, None, bm, nn), lambda qq, i, c_ref: (qq, c_ref[0], i, 0)), blk()],
            out_specs=blk()),
        out_shape=S((4, mm, nn), BF16),
        compiler_params=_params("parallel", "parallel"),
    )(lax.axis_index("c").astype(jnp.int32).reshape(1), g.reshape(4, 2, mm, nn), q)


_BIG = ("w_in", "w_out", "w_ffn_in", "w_ffn_out")
_KIND = dict(w_in="cols", w_out="rows", w_ffn_in="lead", w_ffn_out="rows")
CWP = 128
EARLY_ROWS, LATE_ROWS = 152, 8


def _step(x, tgt, wts, sh, cw_pad):
    t = x.shape[0]
    rc = _ret_consts()
    cos2, sin2 = (jnp.asarray(a) for a in _rope_tables(t))
    n1 = wts["norm1_g"].reshape(LAYERS, 1, D)
    n2 = wts["norm2_g"].reshape(LAYERS, 1, D)
    gm_lng = wts["gm_ln_g"].reshape(LAYERS, 1, GMW)
    gm_lnb = wts["gm_ln_b"].reshape(LAYERS, 1, GMW)
    ws_bf = wts["gm_ws"].astype(BF16)
    wst_bf = jnp.swapaxes(wts["gm_ws"], 2, 3).astype(BF16)
    bias = jnp.repeat(jnp.swapaxes(wts["gm_bs"], 1, 2), GMW // HEADS, axis=2)
    cb = wts["conv_b"].reshape(LAYERS, 1, CVW)
    cv_lng = wts["conv_ln_g"].reshape(LAYERS, 1, CVW)
    cv_lnb = wts["conv_ln_b"].reshape(LAYERS, 1, CVW)
    unit = lambda f, l: (sh[f][l], _KIND[f])
    cshard = CVW // NDEV

    full = {f: [None] * LAYERS for f in _BIG}
    full["w_in"][0], cw_all = _comm_only(_Gather([unit("w_in", 0), (cw_pad, "lead")]), "gather_first")
    cw = jnp.transpose(cw_all[:, :, :, :cshard], (1, 2, 0, 3)).reshape(LAYERS, 32, CVW)

    gcf, gcb = rc["f"]["gC"], rc["b"]["gC"]
    saved = []
    for l in range(LAYERS):
        first = l == 0
        (proj, h1), got = _f_inproj(x, n1, full["w_in"][l], cos2, sin2, l, f"f_inproj_{l}",
                                    _Gather([unit("w_ffn_in", 0)]) if first else None)
        if first:
            full["w_ffn_in"][0], = got
        y_gm = _f_gm(proj, gm_lng, gm_lnb, ws_bf, bias, l, f"f_gm_{l}")
        s_f, s_b = _scan_pair(proj, None, 2, rc["f"]["ZETA"], rc["b"]["ZETA"], gcf, gcb, True, f"f_ret_state_{l}")
        o = _f_ret_out(proj, s_f, s_b, rc, f"f_ret_out_{l}")
        (c, y_cv), got = _f_conv(proj, cw, cb, cv_lng, cv_lnb, l, f"f_conv_{l}",
                                 _Gather([unit("w_out", 0)]) if first else None)
        if first:
            full["w_out"][0], = got
        (xm, ycat), got = _f_mixout(x, y_gm, y_cv, o, proj, full["w_out"][l], f"f_mixout_{l}",
                                    _Gather([unit("w_ffn_out", 0)]) if first else None)
        if first:
            full["w_ffn_out"][0], = got
        w1t = full["w_ffn_in"][l].reshape(2 * FFH, D)
        (xo, h2, gu, act), got = _f_ffn(xm, n2, w1t, full["w_ffn_out"][l], l, f"f_ffn_{l}",
                                        _Gather([unit(f, 1) for f in _BIG]) if first else None)
        if first:
            full["w_in"][1], full["w_out"][1], full["w_ffn_in"][1], full["w_ffn_out"][1] = got
        saved.append(dict(x=x, proj=proj, h1=h1, o=o, s_f=s_f, s_b=s_b, c=c, xm=xm, ycat=ycat, h2=h2, gu=gu, act=act))
        x = xo

    parts = {f: [None] * LAYERS for f in _BIG}
    small = [None] * LAYERS
    norm1 = [None] * LAYERS
    upper = None
    top = (x, wts["final_g"].reshape(1, D), tgt)
    for l in reversed(range(LAYERS)):
        sv = saved[l]
        outs, got = _b_ffn(top, sv["xm"], n2, sv["gu"], full["w_ffn_in"][l].reshape(2 * FFH, D), full["w_ffn_out"][l], l,
                           f"b_ffn_{l}", _Scatter(upper) if upper else None)
        dgu_t, dxm, dxo_bf, d_n2 = outs[:4]
        if l == LAYERS - 1:
            loss, d_final = outs[4:]
        if upper:
            for f, p in zip(_BIG, got):
                parts[f][l + 1] = p
        g_f2 = _mm_wgrad(sv["act"], dxo_bf, NDEV // 2, True, "shared", 1, f"g_ffn_out_{l}").reshape(NDEV, FFH // NDEV, D)
        g_f1 = _mm_wgrad(dgu_t, sv["h2"], NDEV, True, "shared", 1, f"g_ffn_in_{l}")
        (dxm_bf, dy_gm, dO, d_g, dc, d_cvlg, d_cvlb, d_cb), (q_f1, q_f2) = _b_mixout(
            dxm, full["w_out"][l], sv["o"], sv["proj"], sv["c"], cv_lng, cv_lnb, l, f"b_mixout_{l}", _PairSwap([g_f1, g_f2]))
        h_f1 = _pair_add(g_f1, q_f1, f"pair_add_w_ffn_in_{l}")
        h_f2 = _pair_add(g_f2, q_f2, f"pair_add_w_ffn_out_{l}")
        g_out = _mm_wgrad(sv["ycat"], dxm_bf, 1, False, "shared", 1, f"g_out_{l}").reshape(NDEV, D // NDEV, D)
        last = l == 0
        (d_uv, d_ws, _, d_bs_fold, d_gmlg, d_gmlb), (q_out,) = _b_gm(
            sv["proj"], dy_gm, gm_lng, gm_lnb, ws_bf, wst_bf, bias, l, f"b_gm_{l}", _PairSwap([g_out]))
        h_out = _pair_add(g_out, q_out, f"pair_add_w_out_{l}")
        (d_ag, d_cw), got = _b_conv(sv["proj"], dc, cw, l, f"b_conv_{l}", _Scatter([h_f1]) if last else None)
        if last:
            parts["w_ffn_in"][l], = got
        small[l] = dict(gm_ln_g=d_gmlg[0], gm_ln_b=d_gmlb[0], gm_ws=d_ws, gm_bs=d_bs_fold[:, :HEADS].T, conv_w=d_cw[:KW],
                        conv_b=d_cb[0], conv_ln_g=d_cvlg[0], conv_ln_b=d_cvlb[0], norm2_g=d_n2[0])
        comm = None
        if last:
            early_g = {k: jnp.stack([small[ll][k] for ll in range(LAYERS)]) for k in small[0]}
            early_g["final_g"] = d_final[0]
            early_buf = _pack([early_g[k] for k in _SMALL_EARLY] + [loss], EARLY_ROWS)
            comm = _Comms([_Scatter([h_out, h_f2]), _Gather([(early_buf, "lead")])])
        g_f, g_b = _scan_pair(sv["proj"], dO, 1, rc["f"]["XI"], rc["b"]["XI"], gcf, gcb, False, f"b_ret_state_{l}")
        dqkv, got = _b_ret_out(sv["proj"], cos2, sin2, dO, sv["s_f"], sv["s_b"], g_f, g_b, rc, f"b_ret_out_{l}", comm)
        if last:
            parts["w_out"][l], parts["w_ffn_out"][l], early_parts = got
        (dproj, top, d_n1), _ = _b_inproj(d_uv, dqkv, d_g, d_ag, full["w_in"][l], sv["x"], n1, dxm, l, f"b_inproj_{l}")
        norm1[l] = d_n1[0]
        g_in = _mm_wgrad(sv["h1"], dproj, NDEV, False, "cols", 4, f"g_in_{l}")
        q_in, = _comm_only(_PairSwap([g_in]), f"pair_swap_w_in_{l}")
        h_in = _pair_add(g_in, q_in, f"pair_add_w_in_{l}")
        if last:
            tail = [h_in]
        else:
            upper = [h_in, h_out, h_f1, h_f2]
    late_buf = _pack([jnp.stack(norm1)], LATE_ROWS)
    parts["w_in"][0], late_parts = _comm_only(_Comms([_Scatter(tail), _Gather([(late_buf, "lead")])]), "exchange_last")
    return loss, top, parts, (early_parts, late_parts)


def _adamw(w, g, m, v):
    m = ADAM_B1 * m + (1.0 - ADAM_B1) * g
    v = ADAM_B2 * v + (1.0 - ADAM_B2) * (g * g)
    m_hat = m / (1.0 - ADAM_B1 ** ADAM_STEP)
    v_hat = v / (1.0 - ADAM_B2 ** ADAM_STEP)
    return -ADAM_LR * (m_hat / (jnp.sqrt(v_hat) + ADAM_EPS) + ADAM_WD * w), m, v


def _cast_blocks(ws):
    def body(*refs):
        ins, outs = refs[:len(ws)], refs[len(ws):]
        for k, src in enumerate(ins):
            for l in range(LAYERS):
                outs[k * LAYERS + l][...] = src[l].astype(BF16)

    outs = pl.pallas_call(body, name="cast_blocks", out_shape=[S(w.shape[1:], BF16) for w in ws for _ in range(LAYERS)],
                          compiler_params=_params())(*ws)
    return [list(outs[k * LAYERS:(k + 1) * LAYERS]) for k in range(len(ws))]


def _row_block(mm):
    return next(b for b in (512, 352, 128) if mm % b == 0)


def _sum_adam(parts, w, m, v, l, prev, name):
    _, mm, nn = parts.shape
    bm = _row_block(mm)

    def body(p_ref, w_ref, m_ref, v_ref, *rest):
        g_ref, d_ref, nm_ref, nv_ref = rest[-4:]
        g = p_ref[0].astype(F32)
        for s in range(1, 4):
            g = g + p_ref[s].astype(F32)
        g_ref[...] = g
        d_ref[...], nm_ref[...], nv_ref[...] = _adamw(w_ref[...], g, m_ref[...], v_ref[...])

    blk = lambda: pl.BlockSpec((None, bm, nn), lambda i: (l, i, 0))
    prev = list(prev) if prev else []
    return pl.pallas_call(
        body, grid=(mm // bm,), name=name,
        in_specs=[pl.BlockSpec((4, bm, nn), lambda i: (0, i, 0)), blk(), blk(), blk()] + [_ANY] * len(prev),
        out_specs=[blk() for _ in range(4)],
        out_shape=[S(w.shape, F32) for _ in range(4)],
        input_output_aliases={4 + j: j for j in range(len(prev))},
        compiler_params=_params("parallel"),
    )(parts, w, m, v, *prev)


def _sum_small(parts):
    n = len(parts)

    def body(*refs):
        for p_ref, o_ref in zip(refs[:n], refs[n:]):
            g = p_ref[0]
            for s in range(1, NDEV):
                g = g + p_ref[s]
            o_ref[...] = g

    return pl.pallas_call(body, name="sum_small", out_shape=[S(p.shape[1:], F32) for p in parts],
                          compiler_params=_params())(*parts)


def _adam_small(quads):
    n = len(quads)
    as2d = lambda a: a.reshape(1, -1) if a.ndim == 1 else a

    def body(*refs):
        ins, outs = refs[:4 * n], refs[4 * n:]
        for k in range(n):
            g_ref, w_ref, m_ref, v_ref = ins[4 * k:4 * k + 4]
            outs[3 * k][...], outs[3 * k + 1][...], outs[3 * k + 2][...] = _adamw(w_ref[...], g_ref[...], m_ref[...], v_ref[...])

    flat = [as2d(a) for q in quads for a in q]
    outs = pl.pallas_call(body, name="adam_small", out_shape=[S(as2d(q[1]).shape, F32) for q in quads for _ in range(3)],
                          compiler_params=_params())(*flat)
    return [tuple(o.reshape(q[1].shape) for o in outs[3 * k:3 * k + 3]) for k, q in enumerate(quads)]


_SMALL = ("norm1_g", "gm_ln_g", "gm_ln_b", "gm_ws", "gm_bs", "conv_w", "conv_b", "conv_ln_g", "conv_ln_b",
          "norm2_g", "final_g")
_SMALL_EARLY = _SMALL[1:]
_NAMES = ("norm1_g", "w_in", "gm_ln_g", "gm_ln_b", "gm_ws", "gm_bs", "conv_w", "conv_b", "conv_ln_g", "conv_ln_b",
          "w_out", "norm2_g", "w_ffn_in", "w_ffn_out", "final_g")


def _pack(parts, rows):
    flat = jnp.concatenate([p.reshape(-1) for p in parts])
    return jnp.pad(flat, (0, rows * 1024 - flat.shape[0])).reshape(rows, 1024)


def _unpack(buf, shapes):
    flat = buf.reshape(-1)
    out, o = [], 0
    for shp in shapes:
        sz = int(np.prod(shp))
        out.append(flat[o:o + sz].reshape(shp))
        o += sz
    return out


def kernel(x, norm1_g, w_in, gm_ln_g, gm_ln_b, gm_ws, gm_bs, conv_w, conv_b, conv_ln_g, conv_ln_b, w_out, norm2_g, w_ffn_in, w_ffn_out, final_g, loss_target, m_norm1_g, m_w_in, m_gm_ln_g, m_gm_ln_b, m_gm_ws, m_gm_bs, m_conv_w, m_conv_b, m_conv_ln_g, m_conv_ln_b, m_w_out, m_norm2_g, m_w_ffn_in, m_w_ffn_out, m_final_g, v_norm1_g, v_w_in, v_gm_ln_g, v_gm_ln_b, v_gm_ws, v_gm_bs, v_conv_w, v_conv_b, v_conv_ln_g, v_conv_ln_b, v_w_out, v_norm2_g, v_w_ffn_in, v_w_ffn_out, v_final_g):
    w = dict(norm1_g=norm1_g, w_in=w_in, gm_ln_g=gm_ln_g, gm_ln_b=gm_ln_b, gm_ws=gm_ws, gm_bs=gm_bs, conv_w=conv_w,
             conv_b=conv_b, conv_ln_g=conv_ln_g, conv_ln_b=conv_ln_b, w_out=w_out, norm2_g=norm2_g, w_ffn_in=w_ffn_in,
             w_ffn_out=w_ffn_out, final_g=final_g)
    mo = dict(norm1_g=m_norm1_g, w_in=m_w_in, gm_ln_g=m_gm_ln_g, gm_ln_b=m_gm_ln_b, gm_ws=m_gm_ws, gm_bs=m_gm_bs,
              conv_w=m_conv_w, conv_b=m_conv_b, conv_ln_g=m_conv_ln_g, conv_ln_b=m_conv_ln_b, w_out=m_w_out,
              norm2_g=m_norm2_g, w_ffn_in=m_w_ffn_in, w_ffn_out=m_w_ffn_out, final_g=m_final_g)
    vo = dict(norm1_g=v_norm1_g, w_in=v_w_in, gm_ln_g=v_gm_ln_g, gm_ln_b=v_gm_ln_b, gm_ws=v_gm_ws, gm_bs=v_gm_bs,
              conv_w=v_conv_w, conv_b=v_conv_b, conv_ln_g=v_conv_ln_g, conv_ln_b=v_conv_ln_b, w_out=v_w_out,
              norm2_g=v_norm2_g, w_ffn_in=v_w_ffn_in, w_ffn_out=v_w_ffn_out, final_g=v_final_g)
    t = x.shape[1]
    me = 4 * lax.axis_index("x") + 2 * lax.axis_index("y") + lax.axis_index("c")
    cshard = conv_w.shape[2]

    cw_pad = jnp.pad(conv_w, ((0, 0), (0, 32 - KW), (0, CWP - cshard)))
    flip = lambda a: jnp.swapaxes(a, 1, 2)
    big = {f: tuple(flip(a[f]) if f == "w_ffn_in" else a[f] for a in (w, mo, vo)) for f in _BIG}
    sh = dict(zip(_BIG, _cast_blocks([big[f][0] for f in _BIG])))
    loss, dx, parts, small_parts = _step(x.reshape(t, D), loss_target.reshape(t, D), w, sh, cw_pad)

    grads, delta, new_m, new_v = {}, {}, {}, {}
    for f in _BIG:
        outs = None
        for l in reversed(range(LAYERS)):
            outs = _sum_adam(parts[f][l], *big[f], l, outs, f"sum_adam_{f}_{l}")
        grads[f], delta[f], new_m[f], new_v[f] = [flip(a) for a in outs] if f == "w_ffn_in" else outs

    early_sum, late_sum = _sum_small(small_parts)
    early_shapes = [(LAYERS, KW, CVW) if k == "conv_w" else w[k].shape for k in _SMALL_EARLY]
    grads["norm1_g"], = _unpack(late_sum, [w["norm1_g"].shape])
    *early, total = _unpack(early_sum, early_shapes + [()])
    for k, g in zip(_SMALL_EARLY, early):
        grads[k] = lax.dynamic_slice_in_dim(g, me * cshard, cshard, axis=2) if k == "conv_w" else g
    for k, (d, nm, nv) in zip(_SMALL, _adam_small([(grads[k], w[k], mo[k], vo[k]) for k in _SMALL])):
        delta[k], new_m[k], new_v[k] = d, nm, nv

    return (total, dx.reshape(1, t, D), *[grads[k] for k in _NAMES], *[delta[k] for k in _NAMES],
            *[new_m[k] for k in _NAMES], *[new_v[k] for k in _NAMES])
```

```python
import functools

import numpy as np
import jax
import jax.numpy as jnp
from jax import lax
from jax.experimental import pallas as pl
from jax.experimental.pallas import tpu as pltpu

F32, BF16 = jnp.float32, jnp.bfloat16
S = jax.ShapeDtypeStruct

D = 1024
INW = 3072
Q_BLK, K_BLK, V_BLK, G_BLK = 1, 2, 3, 4
CONV_A_BLK, CONV_GATE_BLK = 10, 11
GMW = 256
RETW = 512
CVW = 256
HEADS = 4
DH = 128
C = 128
KW = 31
HALO = 16
FFH = 2816
NDEV = 8
FFB = 2 * FFH // NDEV
FF_CHUNKS = ((0, 768), (768, 1536), (1536, 2304), (2304, FFH))
EPS = 1e-6
LAYERS = 2
SCALE = DH ** -0.5
VMEM_LIMIT = 56 * 1024 * 1024

ADAM_LR, ADAM_B1, ADAM_B2, ADAM_EPS, ADAM_WD, ADAM_STEP = 0.001, 0.9, 0.999, 1e-08, 0.01, 10

_SQRT_HALF = 0.7071067811865476
_INV_SQRT_2PI = 0.3989422804014327


def _params(*sem):
    return pltpu.CompilerParams(dimension_semantics=sem or None, vmem_limit_bytes=VMEM_LIMIT)


def _resident(shape, index_map):
    return pl.BlockSpec(shape, index_map, pipeline_mode=pl.Buffered(1))


def _dot(a, b):
    return jnp.dot(a, b, preferred_element_type=F32)


def _dot_nt(a, b):
    return lax.dot_general(a, b, (((1,), (1,)), ((), ())), preferred_element_type=F32)


def _dot_tn(a, b):
    return lax.dot_general(a, b, (((0,), (0,)), ((), ())), preferred_element_type=F32)


def _sigmoid(x):
    return 1.0 / (1.0 + jnp.exp(-x))


def _gelu_and_grad(x):
    cdf = 0.5 * (1.0 + lax.erf(x * _SQRT_HALF))
    return x * cdf, cdf + x * jnp.exp(-0.5 * x * x) * _INV_SQRT_2PI


def _silu_and_grad(x):
    s = _sigmoid(x)
    return x * s, s * (1.0 + x * (1.0 - s))


def _standardize(x):
    mu = jnp.mean(x, axis=-1, keepdims=True)
    d = x - mu
    rstd = lax.rsqrt(jnp.mean(d * d, axis=-1, keepdims=True) + EPS)
    return d * rstd, rstd


def _standardize_bwd(dxhat, xhat, rstd):
    m1 = jnp.mean(dxhat, axis=-1, keepdims=True)
    m2 = jnp.mean(dxhat * xhat, axis=-1, keepdims=True)
    return rstd * (dxhat - m1 - xhat * m2)


def _rms(x):
    return lax.rsqrt(jnp.mean(x * x, axis=-1, keepdims=True) + EPS)


def _rmsnorm_bwd(dy, x, r, g):
    u = dy * g
    return r * u - x * (r * r * r) * jnp.mean(u * x, axis=-1, keepdims=True)


def _col_sum(a):
    return jnp.sum(a, axis=0, keepdims=True)


def _rot(t, cos2, sin2):
    return t * cos2 + pltpu.roll(t, DH // 2, axis=1) * sin2


def _rot_t(dt, cos2, sin2):
    return dt * cos2 + pltpu.roll(dt * sin2, DH // 2, axis=1)


MESH = pl.DeviceIdType.MESH
_HBM = pl.BlockSpec(memory_space=pltpu.HBM)
_ANY = pl.BlockSpec(memory_space=pl.ANY)


def _place():
    x, y, c = lax.axis_index("x"), lax.axis_index("y"), lax.axis_index("c")
    return x, y, c, ((1 - x, y), (x, 1 - y), (1 - x, 1 - y))


def _slot(full, kind, width, i):
    if kind == "cols":
        return full.at[:, pl.ds(pl.multiple_of(i * width, 128), width)]
    if kind == "rows":
        return full.at[pl.ds(pl.multiple_of(i * width, 16), width), :]
    return full.at[i]


class _Gather:
    def __init__(self, units):
        self.units = units
        self.inputs = [u[0] for u in units]
        self.out_shape = []
        for src, kind in units:
            r, c = src.shape[-2:]
            shape = {"cols": (r, NDEV * c), "rows": (NDEV * r, c), "lead": (NDEV,) + src.shape}[kind]
            self.out_shape.append(S(shape, src.dtype))
        n = len(units)
        self.scratch = [pltpu.SemaphoreType.DMA((n, 7)), pltpu.SemaphoreType.DMA((n, 7)), pltpu.SemaphoreType.DMA((n,))]

    def run(self, phase, ins, outs, scr):
        ssem, rsem, lsem = scr
        x, y, c, chips = _place()
        me, sib = 4 * x + 2 * y + c, (x, y, 1 - c)
        idx = lambda chip, core: 4 * chip[0] + 2 * chip[1] + core
        for u, (src_arr, kind) in enumerate(self.units):
            src, full = ins[u], outs[u]
            width = src_arr.shape[-1] if kind == "cols" else src_arr.shape[-2]
            slot = functools.partial(_slot, full, kind, width)

            def copy(k, block, to, from_src=False):
                return pltpu.make_async_remote_copy(src_ref=src if from_src else slot(block), dst_ref=slot(block),
                                                    send_sem=ssem.at[u, k], recv_sem=rsem.at[u, k],
                                                    device_id=to, device_id_type=MESH)

            mine = lambda: pltpu.make_async_copy(src, slot(me), lsem.at[u])
            first = lambda: [copy(0, me, sib, True)] + [copy(1 + j, me, (*chip, c), True) for j, chip in enumerate(chips)]
            passed = lambda j: copy(4 + j, idx(chips[j], c), sib)
            if phase == "start":
                mine().start()
                for cp in first():
                    cp.start()
            elif phase == "forward":
                for j, chip in enumerate(chips):
                    copy(1 + j, idx(chip, c), sib).wait_recv()
                    passed(j).start()
            else:
                copy(0, idx((x, y), 1 - c), sib).wait_recv()
                for j, chip in enumerate(chips):
                    copy(4 + j, idx(chip, 1 - c), sib).wait_recv()
                for cp in first() + [passed(j) for j in range(3)]:
                    cp.wait_send()
                mine().wait()


class _Scatter:
    def __init__(self, units):
        self.units = units
        self.inputs = list(units)
        self.out_shape = [S(u.shape, u.dtype) for u in units]
        n = len(units)
        self.scratch = [pltpu.SemaphoreType.DMA((n, 3)), pltpu.SemaphoreType.DMA((n, 3)), pltpu.SemaphoreType.DMA((n,))]

    def run(self, phase, ins, outs, scr):
        ssem, rsem, lsem = scr
        x, y, c, chips = _place()
        myq = 2 * x + y
        for u in range(len(self.units)):
            h, p = ins[u], outs[u]

            def copy(k, chip, send_to_them):
                q = 2 * chip[0] + chip[1]
                return pltpu.make_async_remote_copy(src_ref=h.at[q], dst_ref=p.at[myq if send_to_them else q],
                                                    send_sem=ssem.at[u, k], recv_sem=rsem.at[u, k],
                                                    device_id=(*chip, c), device_id_type=MESH)

            mine = lambda: pltpu.make_async_copy(h.at[myq], p.at[myq], lsem.at[u])
            sends = lambda: [copy(k, chip, True) for k, chip in enumerate(chips)]
            if phase == "start":
                mine().start()
                for cp in sends():
                    cp.start()
            elif phase == "finish":
                for k, chip in enumerate(chips):
                    copy(k, chip, False).wait_recv()
                for cp in sends():
                    cp.wait_send()
                mine().wait()


class _Comms:
    def __init__(self, parts):
        self.parts = parts
        self.inputs = [a for p in parts for a in p.inputs]
        self.out_shape = [a for p in parts for a in p.out_shape]
        self.scratch = [a for p in parts for a in p.scratch]

    def run(self, phase, ins, outs, scr):
        i = o = s = 0
        for p in self.parts:
            ni, no, ns = len(p.inputs), len(p.out_shape), len(p.scratch)
            p.run(phase, ins[i:i + ni], outs[o:o + no], scr[s:s + ns])
            i, o, s = i + ni, o + no, s + ns


def _launch(body, grid, in_specs, out_specs, out_shape, scratch, args, name, sem, comm=None):
    if comm is None:
        outs = pl.pallas_call(body, grid=grid, name=name, in_specs=in_specs, out_specs=out_specs, out_shape=out_shape,
                              scratch_shapes=scratch, compiler_params=_params(*sem))(*args)
        return list(outs), []
    n_in, n_out, n_scr = len(args), len(out_shape), len(scratch)
    ci, co = len(comm.inputs), len(comm.out_shape)
    nsteps = int(np.prod(grid))
    fwd_step = (7 * nsteps) // 8

    def hosted(*refs):
        a = refs[:n_in]
        ca = refs[n_in:n_in + ci]
        o = refs[n_in + ci:n_in + ci + n_out]
        cout = refs[n_in + ci + n_out:n_in + ci + n_out + co]
        s = refs[n_in + ci + n_out + co:n_in + ci + n_out + co + n_scr]
        cs = refs[n_in + ci + n_out + co + n_scr:]
        step = pl.program_id(0)
        for d in range(1, len(grid)):
            step = step * grid[d] + pl.program_id(d)

        @pl.when(step == 0)
        def _():
            comm.run("start", ca, cout, cs)

        body(*a, *o, *s)

        @pl.when(step == fwd_step)
        def _():
            comm.run("forward", ca, cout, cs)

        @pl.when(step == nsteps - 1)
        def _():
            comm.run("finish", ca, cout, cs)

    outs = pl.pallas_call(
        hosted, grid=grid, name=name, in_specs=list(in_specs) + [_HBM] * ci, out_specs=list(out_specs) + [_HBM] * co,
        out_shape=list(out_shape) + comm.out_shape, scratch_shapes=list(scratch) + comm.scratch,
        compiler_params=_params(*["arbitrary"] * len(grid)))(*args, *comm.inputs)
    return list(outs[:n_out]), list(outs[n_out:])


def _comm_only(comm, name):
    ci, co = len(comm.inputs), len(comm.out_shape)

    def body(*refs):
        ca, cout, cs = refs[:ci], refs[ci:ci + co], refs[ci + co:]
        for phase in ("start", "forward", "finish"):
            comm.run(phase, ca, cout, cs)

    return pl.pallas_call(body, name=name, in_specs=[_HBM] * ci, out_specs=[_HBM] * co, out_shape=comm.out_shape,
                          scratch_shapes=comm.scratch, compiler_params=_params())(*comm.inputs)


def _ret_consts():
    idx = np.arange(C, dtype=np.float32)
    gf = (1.0 - np.exp2(-5.0 - np.arange(HEADS, dtype=np.float32))).astype(np.float32)
    out = {}
    for name, gamma, fwd in (("f", gf, True), ("b", gf[::-1].copy(), False)):
        lg = np.log(gamma).astype(np.float32)[:, None]
        diff = idx[:, None] - idx[None, :]
        if fwd:
            mask = diff >= 0
            dist = np.where(mask, diff, 0.0)
            zeta = np.exp(lg * (C - 1 - idx))
            xi = np.exp(lg * (idx + 1))
        else:
            mask = diff < 0
            dist = np.where(mask, -diff, 0.0)
            zeta = np.exp(lg * idx)
            xi = np.exp(lg * (C - idx))
        dm = np.where(mask[None], np.exp(lg[:, :, None] * dist[None]), 0.0).astype(np.float32)
        bc = lambda vec: np.ascontiguousarray(np.broadcast_to(vec.astype(np.float32)[:, :, None], (HEADS, C, DH)))
        out[name] = dict(D=dm, XI=bc(xi), ZETA=bc(zeta), gC=[float(v) for v in np.exp(lg[:, 0] * C).astype(np.float32)])
    return out


def _rope_tables(t):
    half = DH // 2
    inv_freq = (np.float32(10000.0) ** (-np.arange(half, dtype=np.float32) / np.float32(half))).astype(np.float32)
    ang = (np.arange(t, dtype=np.float32)[:, None] * inv_freq[None, :]).astype(np.float64)
    cos, sin = np.cos(ang).astype(np.float32), np.sin(ang).astype(np.float32)
    return np.concatenate([cos, cos], axis=1), np.concatenate([-sin, sin], axis=1)


def _f_inproj(x, g1, w, cos2, sin2, l, name, comm=None):
    t = x.shape[0]
    tm = 512 if comm else min(1024, t)

    def body(x_ref, g_ref, w_ref, cos_ref, sin_ref, proj_ref, ht_ref):
        xv = x_ref[...]
        h = (xv * _rms(xv) * g_ref[...]).astype(BF16)
        ht_ref[...] = h.T
        for nb in range(INW // 512):
            cs = slice(nb * 512, (nb + 1) * 512)
            res = _dot(h, w_ref[:, cs])
            if nb in (Q_BLK, K_BLK):
                for hh in range(HEADS):
                    r = _rot(res[:, hh * DH:(hh + 1) * DH], cos_ref[...], sin_ref[...])
                    proj_ref[:, nb * 512 + hh * DH:nb * 512 + (hh + 1) * DH] = (r * SCALE if nb == K_BLK else r).astype(BF16)
            else:
                proj_ref[:, cs] = res.astype(BF16)

    return _launch(
        body, (t // tm,),
        [pl.BlockSpec((tm, D), lambda i: (i, 0)),
         pl.BlockSpec((None, 1, D), lambda i: (l, 0, 0)),
         _resident((D, INW), lambda i: (0, 0)),
         pl.BlockSpec((tm, DH), lambda i: (i, 0)), pl.BlockSpec((tm, DH), lambda i: (i, 0))],
        [pl.BlockSpec((tm, INW), lambda i: (i, 0)), pl.BlockSpec((D, tm), lambda i: (0, i))],
        [S((t, INW), BF16), S((D, t), BF16)], [], (x, g1, w, cos2, sin2), name, ("parallel",), comm)


def _gm_chunk_fwd(u, v, lng, lnb, ws_ref, bias):
    au, dau = _gelu_and_grad(u)
    av, dav = _gelu_and_grad(v)
    vhat, rstd = _standardize(av)
    vn = (vhat * lng + lnb).astype(BF16)
    head = lax.broadcasted_iota(jnp.int32, (C, GMW), 1) // (GMW // HEADS)
    mixed = bias
    for h in range(HEADS):
        mixed = mixed + jnp.where(head == h, _dot(ws_ref[h], vn), 0.0)
    return au, dau, dav, vhat, rstd, vn, mixed, head


def _f_gm(proj, lng, lnb, ws_bf, bias, l, name):
    t = proj.shape[0]
    tm = min(1024, t)

    def body(p_ref, lng_ref, lnb_ref, ws_ref, bias_ref, y_ref):
        for ci in range(tm // C):
            rows = slice(ci * C, (ci + 1) * C)
            u = p_ref[rows, 0:GMW].astype(F32)
            v = p_ref[rows, GMW:2 * GMW].astype(F32)
            au, _, _, _, _, _, mixed, _ = _gm_chunk_fwd(u, v, lng_ref[...], lnb_ref[...], ws_ref, bias_ref[...])
            y_ref[rows, :] = (au * mixed).astype(BF16)

    return pl.pallas_call(
        body, grid=(t // tm,), name=name,
        in_specs=[pl.BlockSpec((tm, 2 * GMW), lambda i: (i, 0)),
                  pl.BlockSpec((None, 1, GMW), lambda i: (l, 0, 0)),
                  pl.BlockSpec((None, 1, GMW), lambda i: (l, 0, 0)),
                  pl.BlockSpec((None, HEADS, C, C), lambda i: (l, 0, 0, 0)),
                  pl.BlockSpec((None, C, GMW), lambda i: (l, 0, 0))],
        out_specs=pl.BlockSpec((tm, GMW), lambda i: (i, 0)),
        out_shape=S((t, GMW), BF16),
        compiler_params=_params("parallel"),
    )(proj, lng, lnb, ws_bf, bias)


def _scan_pair(proj, other, col, wf, wb, gcf, gcb, first_is_f, name):
    t = proj.shape[0]
    n = t // C
    sc = min(SCAN_CHUNKS, n)
    nsteps = n // sc
    other_is_proj = other is None

    def body(a1, o1, a2, o2, w1_ref, w2_ref, out1, out2, st1, st2):
        @pl.when(pl.program_id(0) == 0)
        def _():
            st1[...] = jnp.zeros_like(st1)
            st2[...] = jnp.zeros_like(st2)

        def one(a_ref, o_ref, w_ref, gc, st, out, order):
            for h in range(HEADS):
                sl = slice(h * DH, (h + 1) * DH)
                incs = {}
                for j in order:
                    rows = slice(j * C, (j + 1) * C)
                    aw = (a_ref[rows, sl].astype(F32) * w_ref[h]).astype(BF16)
                    incs[j] = _dot_tn(aw, o_ref[rows, sl].astype(BF16))
                cur = st[h]
                for j in order:
                    out[j, h] = cur.astype(BF16)
                    cur = gc[h] * cur + incs[j]
                st[h] = cur

        g1, g2 = (gcf, gcb) if first_is_f else (gcb, gcf)
        one(a1, o1, w1_ref, g1, st1, out1, range(sc))
        one(a2, o2, w2_ref, g2, st2, out2, range(sc - 1, -1, -1))

    up = lambda i: i
    down = lambda i: nsteps - 1 - i

    def specs(ix):
        o_spec = pl.BlockSpec((sc * C, RETW), lambda i: (ix(i), V_BLK if other_is_proj else 0))
        return [pl.BlockSpec((sc * C, RETW), lambda i: (ix(i), col)), o_spec]

    const = lambda: pl.BlockSpec((HEADS, C, DH), lambda i: (0, 0, 0))
    oth = proj if other_is_proj else other
    w1, w2 = (wf, wb) if first_is_f else (wb, wf)
    out1, out2 = pl.pallas_call(
        body, grid=(nsteps,), name=name,
        in_specs=specs(up) + specs(down) + [const(), const()],
        out_specs=[pl.BlockSpec((sc, HEADS, DH, DH), lambda i: (up(i), 0, 0, 0)),
                   pl.BlockSpec((sc, HEADS, DH, DH), lambda i: (down(i), 0, 0, 0))],
        out_shape=[S((n, HEADS, DH, DH), BF16), S((n, HEADS, DH, DH), BF16)],
        scratch_shapes=[pltpu.VMEM((HEADS, DH, DH), F32), pltpu.VMEM((HEADS, DH, DH), F32)],
        compiler_params=_params("arbitrary"),
    )(proj, oth, proj, oth, w1, w2)
    return (out1, out2) if first_is_f else (out2, out1)


SCAN_CHUNKS = 16


RET_CHUNKS = 8


def _f_ret_out(proj, s_f, s_b, rc, name):
    t = proj.shape[0]
    nchunks = min(RET_CHUNKS, t // C)
    tm = nchunks * C

    def body(q_ref, k_ref, v_ref, sf_ref, sb_ref, d_ref, xif_ref, xib_ref, o_ref):
        for ci in range(nchunks):
            rows = slice(ci * C, (ci + 1) * C)
            for h in range(HEADS):
                sl = slice(h * DH, (h + 1) * DH)
                qh, kh, vh = q_ref[rows, sl], k_ref[rows, sl], v_ref[rows, sl]
                p = (_dot_nt(qh, kh) * d_ref[h]).astype(BF16)
                cross = _dot(qh, jnp.concatenate([sf_ref[ci, h], sb_ref[ci, h]], axis=1))
                o_ref[rows, sl] = _dot(p, vh) + xif_ref[h] * cross[:, 0:DH] + xib_ref[h] * cross[:, DH:2 * DH]

    const = lambda: pl.BlockSpec((HEADS, C, DH), lambda i: (0, 0, 0))
    state = lambda: pl.BlockSpec((nchunks, HEADS, DH, DH), lambda i: (i, 0, 0, 0))
    return pl.pallas_call(
        body, grid=(t // tm,), name=name,
        in_specs=[pl.BlockSpec((tm, RETW), lambda i, cb=cb: (i, cb)) for cb in (Q_BLK, K_BLK, V_BLK)] +
                 [state(), state(), const(), const(), const()],
        out_specs=pl.BlockSpec((tm, RETW), lambda i: (i, 0)),
        out_shape=S((t, RETW), F32),
        compiler_params=_params("parallel"),
    )(proj, proj, proj, s_f, s_b, rc["f"]["D"] + rc["b"]["D"], rc["f"]["XI"], rc["b"]["XI"])


def _conv_halo_specs(t, tm, width, col):
    r = tm // HALO
    last = t // HALO - 1
    return [pl.BlockSpec((HALO, width), lambda i: (jnp.maximum(i * r - 1, 0), col)),
            pl.BlockSpec((tm, width), lambda i: (i, col)),
            pl.BlockSpec((HALO, width), lambda i: (jnp.minimum((i + 1) * r, last), col))]


def _fill_ext(ext, prev, cur, nxt, i, nt, tm):
    ext[0:HALO, :] = jnp.where(i > 0, prev, 0.0)
    ext[HALO:HALO + tm, :] = cur
    ext[HALO + tm:2 * HALO + tm, :] = jnp.where(i < nt - 1, nxt, 0.0)


def _glu(a_ref, g_ref):
    return a_ref[...].astype(F32) * _sigmoid(g_ref[...].astype(F32))


def _shifted_copies(ext, rot, tm):
    rows = tm + 2 * HALO - 8
    for b in range(1, 8):
        rot[b - 1, :, :] = ext[pl.ds(b, rows), :]


def _tap(ext, rot, r0, s, rb):
    a, b = divmod(s, 8)
    return ext[pl.ds(r0 + 8 * a, rb), :] if b == 0 else rot[b - 1, pl.ds(r0 + 8 * a, rb), :]


def _f_conv(proj, cw, cb, lng, lnb, l, name, comm=None):
    t = proj.shape[0]
    tm = 512
    nt = t // tm
    rb = 64

    def body(ap, ac, an, gp, gc, gn, cw_ref, cb_ref, lng_ref, lnb_ref, c_ref, y_ref, hext, hrot):
        i = pl.program_id(0)
        _fill_ext(hext, _glu(ap, gp), _glu(ac, gc), _glu(an, gn), i, nt, tm)
        _shifted_copies(hext, hrot, tm)
        for r0 in range(0, tm, rb):
            acc = jnp.zeros((rb, CVW), F32) + cb_ref[...]
            for j in range(KW):
                acc = acc + cw_ref[j:j + 1, :] * _tap(hext, hrot, r0, j + 1, rb)
            c_ref[r0:r0 + rb, :] = acc
            chat, _ = _standardize(acc)
            z = chat * lng_ref[...] + lnb_ref[...]
            y_ref[r0:r0 + rb, :] = (z * _sigmoid(z)).astype(BF16)

    vec = lambda: pl.BlockSpec((None, 1, CVW), lambda i: (l, 0, 0))
    return _launch(
        body, (nt,),
        _conv_halo_specs(t, tm, CVW, CONV_A_BLK) + _conv_halo_specs(t, tm, CVW, CONV_GATE_BLK) +
        [pl.BlockSpec((None, 32, CVW), lambda i: (l, 0, 0)), vec(), vec(), vec()],
        [pl.BlockSpec((tm, CVW), lambda i: (i, 0)), pl.BlockSpec((tm, CVW), lambda i: (i, 0))],
        [S((t, CVW), F32), S((t, CVW), BF16)],
        [pltpu.VMEM((tm + 2 * HALO, CVW), F32), pltpu.VMEM((7, tm + 2 * HALO - 8, CVW), F32)],
        (proj, proj, proj, proj, proj, proj, cw, cb, lng, lnb), name, ("parallel",), comm)


def _f_mixout(x, y_gm, y_cv, o, proj, w, name, comm=None):
    t = x.shape[0]
    tm = 512 if comm else min(1024, t)

    def body(x_ref, ygm_ref, ycv_ref, o_ref, g_ref, w_ref, xm_ref, ycat_t_ref, ycat):
        ycat[:, 0:GMW] = ygm_ref[...]
        ycat[:, GMW + RETW:D] = ycv_ref[...]
        for h in range(HEADS):
            sl = slice(h * DH, (h + 1) * DH)
            ohat, _ = _standardize(o_ref[:, sl])
            g = g_ref[:, sl].astype(F32)
            ycat[:, GMW + h * DH:GMW + (h + 1) * DH] = (ohat * (g * _sigmoid(g))).astype(BF16)
        yc = ycat[...]
        ycat_t_ref[...] = yc.T
        xm_ref[...] = x_ref[...] + _dot(yc, w_ref[...])

    return _launch(
        body, (t // tm,),
        [pl.BlockSpec((tm, D), lambda i: (i, 0)),
         pl.BlockSpec((tm, GMW), lambda i: (i, 0)),
         pl.BlockSpec((tm, CVW), lambda i: (i, 0)),
         pl.BlockSpec((tm, RETW), lambda i: (i, 0)),
         pl.BlockSpec((tm, RETW), lambda i: (i, G_BLK)),
         _resident((D, D), lambda i: (0, 0))],
        [pl.BlockSpec((tm, D), lambda i: (i, 0)), pl.BlockSpec((D, tm), lambda i: (0, i))],
        [S((t, D), F32), S((D, t), BF16)],
        [pltpu.VMEM((tm, D), BF16)], (x, y_gm, y_cv, o, proj, w), name, ("parallel",), comm)


def _f_ffn(xm, g2, w1, w2, l, name, comm=None):
    t = xm.shape[0]
    tm = 512

    def body(x_ref, g_ref, w1_ref, w2_ref, xo_ref, h_ref, gu_ref, act_t_ref):
        xv = x_ref[...]
        h = (xv * _rms(xv) * g_ref[...]).astype(BF16)
        h_ref[...] = h
        acc = xv
        for a, b in FF_CHUNKS:
            gate = _dot_nt(h, w1_ref[a:b, :])
            up = _dot_nt(h, w1_ref[FFH + a:FFH + b, :])
            gu_ref[:, a:b] = gate.astype(BF16)
            gu_ref[:, FFH + a:FFH + b] = up.astype(BF16)
            av = ((gate * _sigmoid(gate)) * up).astype(BF16)
            act_t_ref[a:b, :] = av.T
            acc = acc + _dot(av, w2_ref[a:b, :])
        xo_ref[...] = acc

    return _launch(
        body, (t // tm,),
        [pl.BlockSpec((tm, D), lambda i: (i, 0)),
         pl.BlockSpec((None, 1, D), lambda i: (l, 0, 0)),
         _resident((2 * FFH, D), lambda i: (0, 0)),
         _resident((FFH, D), lambda i: (0, 0))],
        [pl.BlockSpec((tm, D), lambda i: (i, 0)), pl.BlockSpec((tm, D), lambda i: (i, 0)),
         pl.BlockSpec((tm, 2 * FFH), lambda i: (i, 0)), pl.BlockSpec((FFH, tm), lambda i: (0, i))],
        [S((t, D), F32), S((t, D), BF16), S((t, 2 * FFH), BF16), S((FFH, t), BF16)],
        [], (xm, g2, w1, w2), name, ("parallel",), comm)


def _b_ffn(top, xm, g2, gu, w1, w2, l, name, comm=None):
    t = xm.shape[0]
    tm = 256
    from_loss = isinstance(top, tuple)
    n_top = 3 if from_loss else 1

    def body(*refs):
        top_refs = refs[:n_top]
        x_ref, g_ref, gu_ref, w1_ref, w2_ref, dgu_ref, dxm_ref, dxb_ref, dg_ref = refs[n_top:n_top + 9]
        first = pl.program_id(0) == 0

        @pl.when(first)
        def _():
            dg_ref[...] = jnp.zeros_like(dg_ref)

        if from_loss:
            xo_ref, fg_ref, t_ref = top_refs
            loss_ref, dfg_ref = refs[n_top + 9:n_top + 11]

            @pl.when(first)
            def _():
                loss_ref[...] = jnp.zeros_like(loss_ref)
                dfg_ref[...] = jnp.zeros_like(dfg_ref)

            xo = xo_ref[...]
            ro = _rms(xo)
            xr = xo * ro
            err = xr * fg_ref[...] - t_ref[...]
            loss_ref[...] += (0.5 / D) * _col_sum(jnp.sum(err * err, axis=1, keepdims=True))
            dy = err * (1.0 / D)
            dfg_ref[...] += _col_sum(dy * xr)
            dxo = _rmsnorm_bwd(dy, xo, ro, fg_ref[...])
        else:
            dxo = top_refs[0][...]
        dxb = dxo.astype(BF16)
        dxb_ref[...] = dxb
        dh = jnp.zeros((tm, D), F32)
        for a, b in FF_CHUNKS:
            dact = _dot_nt(dxb, w2_ref[a:b, :])
            gate = gu_ref[:, a:b].astype(F32)
            up = gu_ref[:, FFH + a:FFH + b].astype(F32)
            sg, dsg = _silu_and_grad(gate)
            dgate = (dact * up * dsg).astype(BF16)
            dup = (dact * sg).astype(BF16)
            dgu_ref[a:b, :] = dgate.T
            dgu_ref[FFH + a:FFH + b, :] = dup.T
            dh = dh + _dot(dgate, w1_ref[a:b, :]) + _dot(dup, w1_ref[FFH + a:FFH + b, :])
        xv = x_ref[...]
        r = _rms(xv)
        dg_ref[...] += _col_sum(dh * xv * r)
        dxm_ref[...] = dxo + _rmsnorm_bwd(dh, xv, r, g_ref[...])

    tok = lambda: pl.BlockSpec((tm, D), lambda i: (i, 0))
    vec = lambda: pl.BlockSpec((1, D), lambda i: (0, 0))
    top_specs = [tok(), vec(), tok()] if from_loss else [tok()]
    extra_specs = [pl.BlockSpec((1, 1), lambda i: (0, 0)), vec()] if from_loss else []
    extra_shape = [S((1, 1), F32), S((1, D), F32)] if from_loss else []
    return _launch(
        body, (t // tm,),
        top_specs + [tok(), pl.BlockSpec((None, 1, D), lambda i: (l, 0, 0)),
                     pl.BlockSpec((tm, 2 * FFH), lambda i: (i, 0)),
                     _resident((2 * FFH, D), lambda i: (0, 0)),
                     _resident((FFH, D), lambda i: (0, 0))],
        [pl.BlockSpec((2 * FFH, tm), lambda i: (0, i)), tok(), tok(), vec()] + extra_specs,
        [S((2 * FFH, t), BF16), S((t, D), F32), S((t, D), BF16), S((1, D), F32)] + extra_shape,
        [],
        ((*top,) if from_loss else (top,)) + (xm, g2, gu, w1, w2), name, ("arbitrary",), comm)


def _mm_wgrad(at, b, pieces, at_rows, b_mode, group, name):
    bt = 2048
    t = at.shape[-1]
    bt = min(bt, t)
    nt = t // bt
    if at_rows:
        ka = at.shape[0] // pieces
        a_spec = pl.BlockSpec((ka, bt), lambda j, tt: (j, tt))
    else:
        ka = at.shape[0]
        a_spec = pl.BlockSpec((ka, bt), lambda j, tt: (0, tt))
    if b_mode == "shared":
        nb, b_spec = b.shape[1], pl.BlockSpec((bt, b.shape[1]), lambda j, tt: (tt, 0))
    elif b_mode == "cols":
        nb = b.shape[1] // pieces
        b_spec = pl.BlockSpec((bt, group * nb), lambda j, tt: (tt, j))
    else:
        nb, b_spec = b.shape[2], pl.BlockSpec((None, bt, b.shape[2]), lambda j, tt: (j, tt, 0))
    assert group == 1 or b_mode == "cols"

    def body(a_ref, b_ref, o_ref, acc):
        tt = pl.program_id(1)

        @pl.when(tt == 0)
        def _():
            acc[...] = jnp.zeros_like(acc)

        acc[...] += _dot(a_ref[...], b_ref[...])

        @pl.when(tt == nt - 1)
        def _():
            for k in range(group):
                o_ref[k] = acc[:, k * nb:(k + 1) * nb].astype(BF16)

    return pl.pallas_call(
        body, grid=(pieces // group, nt), name=name,
        in_specs=[a_spec, b_spec],
        out_specs=pl.BlockSpec((group, ka, nb), lambda j, tt: (j, 0, 0)),
        out_shape=S((pieces, ka, nb), BF16),
        scratch_shapes=[pltpu.VMEM((ka, group * nb), F32)],
        compiler_params=_params("parallel", "arbitrary"),
    )(at, b)


def _b_mixout(dxm, w, o, proj, c, lng, lnb, l, name, comm=None):
    t = dxm.shape[0]
    tm = 512

    def body(dxm_ref, w_ref, o_ref, g_ref, c_ref, lng_ref, lnb_ref,
             dxb_ref, dygm_ref, dO_ref, dg_ref, dc_ref, dlg_ref, dlb_ref, dcb_ref):
        @pl.when(pl.program_id(0) == 0)
        def _():
            dlg_ref[...] = jnp.zeros_like(dlg_ref)
            dlb_ref[...] = jnp.zeros_like(dlb_ref)
            dcb_ref[...] = jnp.zeros_like(dcb_ref)

        dxb = dxm_ref[...].astype(BF16)
        dxb_ref[...] = dxb
        dy = _dot_nt(dxb, w_ref[...])
        dygm_ref[...] = dy[:, 0:GMW]
        for h in range(HEADS):
            sl = slice(h * DH, (h + 1) * DH)
            ohat, rstd = _standardize(o_ref[:, sl])
            sg, dsg = _silu_and_grad(g_ref[:, sl].astype(F32))
            dyr = dy[:, GMW + h * DH:GMW + (h + 1) * DH]
            dg_ref[:, sl] = (dyr * ohat * dsg).astype(BF16)
            dO_ref[:, sl] = _standardize_bwd(dyr * sg, ohat, rstd)
        chat, rstd = _standardize(c_ref[...])
        z = chat * lng_ref[...] + lnb_ref[...]
        _, dsz = _silu_and_grad(z)
        dz = dy[:, GMW + RETW:D] * dsz
        dlg_ref[...] += _col_sum(dz * chat)
        dlb_ref[...] += _col_sum(dz)
        dc = _standardize_bwd(dz * lng_ref[...], chat, rstd)
        dcb_ref[...] += _col_sum(dc)
        dc_ref[...] = dc

    vec = lambda: pl.BlockSpec((None, 1, CVW), lambda i: (l, 0, 0))
    acc = lambda: pl.BlockSpec((1, CVW), lambda i: (0, 0))
    return _launch(
        body, (t // tm,),
        [pl.BlockSpec((tm, D), lambda i: (i, 0)),
         _resident((D, D), lambda i: (0, 0)),
         pl.BlockSpec((tm, RETW), lambda i: (i, 0)),
         pl.BlockSpec((tm, RETW), lambda i: (i, G_BLK)),
         pl.BlockSpec((tm, CVW), lambda i: (i, 0)), vec(), vec()],
        [pl.BlockSpec((tm, D), lambda i: (i, 0)), pl.BlockSpec((tm, GMW), lambda i: (i, 0)),
         pl.BlockSpec((tm, RETW), lambda i: (i, 0)), pl.BlockSpec((tm, RETW), lambda i: (i, 0)),
         pl.BlockSpec((tm, CVW), lambda i: (i, 0)), acc(), acc(), acc()],
        [S((t, D), BF16), S((t, GMW), F32), S((t, RETW), F32), S((t, RETW), BF16), S((t, CVW), F32),
         S((1, CVW), F32), S((1, CVW), F32), S((1, CVW), F32)],
        [], (dxm, w, o, proj, c, lng, lnb), name, ("arbitrary",), comm)


def _b_gm(proj, dy, lng, lnb, ws_bf, wst_bf, bias, l, name, comm=None):
    t = proj.shape[0]
    tm = min(1024, t)
    nt = t // tm

    def body(p_ref, dy_ref, lng_ref, lnb_ref, ws_ref, wst_ref, bias_ref,
             duv_ref, dws_ref, dbias_ref, dbs_ref, dlg_ref, dlb_ref):
        @pl.when(pl.program_id(0) == 0)
        def _():
            dws_ref[...] = jnp.zeros_like(dws_ref)
            dbias_ref[...] = jnp.zeros_like(dbias_ref)
            dbs_ref[...] = jnp.zeros_like(dbs_ref)
            dlg_ref[...] = jnp.zeros_like(dlg_ref)
            dlb_ref[...] = jnp.zeros_like(dlb_ref)

        for ci in range(tm // C):
            rows = slice(ci * C, (ci + 1) * C)
            u = p_ref[rows, 0:GMW].astype(F32)
            v = p_ref[rows, GMW:2 * GMW].astype(F32)
            au, dau, dav, vhat, rstd, vn, mixed, head = _gm_chunk_fwd(u, v, lng_ref[...], lnb_ref[...], ws_ref, bias_ref[...])
            dyc = dy_ref[rows, :]
            dmixed = dyc * au
            dmb = dmixed.astype(BF16)
            dbias_ref[...] += dmixed
            dvn = jnp.zeros((C, GMW), F32)
            for h in range(HEADS):
                dws_ref[h] += _dot_nt(jnp.where(head == h, dmixed, 0.0).astype(BF16), vn)
                dvn = dvn + jnp.where(head == h, _dot(wst_ref[h], dmb), 0.0)
            dlg_ref[...] += _col_sum(dvn * vhat)
            dlb_ref[...] += _col_sum(dvn)
            dav_in = _standardize_bwd(dvn * lng_ref[...], vhat, rstd)
            duv_ref[rows, 0:GMW] = (dyc * mixed * dau).astype(BF16)
            duv_ref[rows, GMW:2 * GMW] = (dav_in * dav).astype(BF16)

        @pl.when(pl.program_id(0) == nt - 1)
        def _():
            head = lax.broadcasted_iota(jnp.int32, (C, GMW), 1) // (GMW // HEADS)
            lane = lax.broadcasted_iota(jnp.int32, (C, 128), 1)
            fold = jnp.zeros((C, 128), F32)
            for h in range(HEADS):
                col = jnp.sum(jnp.where(head == h, dbias_ref[...], 0.0), axis=1, keepdims=True)
                fold = jnp.where(lane == h, col, fold)
            dbs_ref[...] = fold

    vec = lambda: pl.BlockSpec((None, 1, GMW), lambda i: (l, 0, 0))
    mats = lambda: pl.BlockSpec((None, HEADS, C, C), lambda i: (l, 0, 0, 0))
    return _launch(
        body, (nt,),
        [pl.BlockSpec((tm, 2 * GMW), lambda i: (i, 0)), pl.BlockSpec((tm, GMW), lambda i: (i, 0)),
         vec(), vec(), mats(), mats(), pl.BlockSpec((None, C, GMW), lambda i: (l, 0, 0))],
        [pl.BlockSpec((tm, 2 * GMW), lambda i: (i, 0)),
         pl.BlockSpec((HEADS, C, C), lambda i: (0, 0, 0)),
         pl.BlockSpec((C, GMW), lambda i: (0, 0)), pl.BlockSpec((C, 128), lambda i: (0, 0)),
         pl.BlockSpec((1, GMW), lambda i: (0, 0)), pl.BlockSpec((1, GMW), lambda i: (0, 0))],
        [S((t, 2 * GMW), BF16), S((HEADS, C, C), F32), S((C, GMW), F32), S((C, 128), F32),
         S((1, GMW), F32), S((1, GMW), F32)],
        [], (proj, dy, lng, lnb, ws_bf, wst_bf, bias), name, ("arbitrary",), comm)


def _b_conv(proj, dc, cw, l, name, comm=None):
    t = proj.shape[0]
    tm = 256
    nt = t // tm
    rb = 64

    def body(ap, ac, an, gp, gc, gn, dp, dcur, dn, cw_ref, dag_ref, dcw_ref, hext, dext, hrot, drot):
        i = pl.program_id(0)

        @pl.when(i == 0)
        def _():
            dcw_ref[...] = jnp.zeros_like(dcw_ref)

        _fill_ext(hext, _glu(ap, gp), _glu(ac, gc), _glu(an, gn), i, nt, tm)
        _fill_ext(dext, dp[...], dcur[...], dn[...], i, nt, tm)
        _shifted_copies(hext, hrot, tm)
        _shifted_copies(dext, drot, tm)
        for j in range(KW):
            dcw_ref[j:j + 1, :] += _col_sum(dcur[...] * _tap(hext, hrot, 0, j + 1, tm))
        for r0 in range(0, tm, rb):
            dh = jnp.zeros((rb, CVW), F32)
            for j in range(KW):
                dh = dh + cw_ref[j:j + 1, :] * _tap(dext, drot, r0, 2 * HALO - 1 - j, rb)
            a = ac[r0:r0 + rb, :].astype(F32)
            s = _sigmoid(gc[r0:r0 + rb, :].astype(F32))
            dag_ref[r0:r0 + rb, 0:CVW] = (dh * s).astype(BF16)
            dag_ref[r0:r0 + rb, CVW:2 * CVW] = (dh * a * s * (1.0 - s)).astype(BF16)

    dspecs = _conv_halo_specs(t, tm, CVW, 0)
    return _launch(
        body, (nt,),
        _conv_halo_specs(t, tm, CVW, CONV_A_BLK) + _conv_halo_specs(t, tm, CVW, CONV_GATE_BLK) + dspecs +
        [pl.BlockSpec((None, 32, CVW), lambda i: (l, 0, 0))],
        [pl.BlockSpec((tm, 2 * CVW), lambda i: (i, 0)), pl.BlockSpec((32, CVW), lambda i: (0, 0))],
        [S((t, 2 * CVW), BF16), S((32, CVW), F32)],
        [pltpu.VMEM((tm + 2 * HALO, CVW), F32), pltpu.VMEM((tm + 2 * HALO, CVW), F32),
         pltpu.VMEM((7, tm + 2 * HALO - 8, CVW), F32), pltpu.VMEM((7, tm + 2 * HALO - 8, CVW), F32)],
        (proj, proj, proj, proj, proj, proj, dc, dc, dc, cw), name, ("arbitrary",), comm)


def _b_ret_out(proj, cos2, sin2, dO, s_f, s_b, g_f, g_b, rc, name, comm=None):
    t = proj.shape[0]
    nchunks = min(RET_CHUNKS, t // C)
    tm = nchunks * C

    def body(q_ref, k_ref, v_ref, cos_ref, sin_ref, dO_ref, sf_ref, sb_ref, gf_ref, gb_ref,
             d_ref, xif_ref, xib_ref, zef_ref, zeb_ref, dq_ref, dk_ref, dv_ref):
        for ci in range(nchunks):
            rows = slice(ci * C, (ci + 1) * C)
            cos_v, sin_v = cos_ref[rows, :], sin_ref[rows, :]
            for h in range(HEADS):
                sl = slice(h * DH, (h + 1) * DH)
                qh, kh, vh = q_ref[rows, sl], k_ref[rows, sl], v_ref[rows, sl]
                dOh = dO_ref[rows, sl].astype(BF16)
                dm = d_ref[h]
                p = (_dot_nt(qh, kh) * dm).astype(BF16)
                dp = (_dot_nt(dOh, vh) * dm).astype(BF16)
                from_s = _dot_nt(dOh, jnp.concatenate([sf_ref[ci, h], sb_ref[ci, h]], axis=0))
                from_g = _dot_nt(vh, jnp.concatenate([gf_ref[ci, h], gb_ref[ci, h]], axis=0))
                kg = _dot(kh, jnp.concatenate([gf_ref[ci, h], gb_ref[ci, h]], axis=1))
                dqr = _dot(dp, kh) + xif_ref[h] * from_s[:, 0:DH] + xib_ref[h] * from_s[:, DH:2 * DH]
                dkr = (_dot_tn(dp, qh) + zef_ref[h] * from_g[:, 0:DH] + zeb_ref[h] * from_g[:, DH:2 * DH]) * SCALE
                dv = _dot_tn(p, dOh) + zef_ref[h] * kg[:, 0:DH] + zeb_ref[h] * kg[:, DH:2 * DH]
                dq_ref[rows, sl] = _rot_t(dqr, cos_v, sin_v).astype(BF16)
                dk_ref[rows, sl] = _rot_t(dkr, cos_v, sin_v).astype(BF16)
                dv_ref[rows, sl] = dv.astype(BF16)

    const = lambda: pl.BlockSpec((HEADS, C, DH), lambda i: (0, 0, 0))
    state = lambda: pl.BlockSpec((nchunks, HEADS, DH, DH), lambda i: (i, 0, 0, 0))
    tok = lambda: pl.BlockSpec((tm, RETW), lambda i: (i, 0))
    return _launch(
        body, (t // tm,),
        [pl.BlockSpec((tm, RETW), lambda i, cb=cb: (i, cb)) for cb in (Q_BLK, K_BLK, V_BLK)] +
        [pl.BlockSpec((tm, DH), lambda i: (i, 0)), pl.BlockSpec((tm, DH), lambda i: (i, 0)), tok(),
         state(), state(), state(), state()] + [const() for _ in range(5)],
        [tok(), tok(), tok()],
        [S((t, RETW), BF16) for _ in range(3)], [],
        (proj, proj, proj, cos2, sin2, dO, s_f, s_b, g_f, g_b, rc["f"]["D"] + rc["b"]["D"],
         rc["f"]["XI"], rc["b"]["XI"], rc["f"]["ZETA"], rc["b"]["ZETA"]), name, ("parallel",), comm)


def _b_inproj(d_uv, dqkv, d_g, d_ag, w, x, g1, dxm, l, name, comm=None):
    t = x.shape[0]
    tm = 512

    def body(duv_ref, dq_ref, dk_ref, dv_ref, dg_ref, dag_ref, w_ref, x_ref, g_ref, dxm_ref,
             dp_ref, dx_ref, dn_ref):
        @pl.when(pl.program_id(0) == 0)
        def _():
            dn_ref[...] = jnp.zeros_like(dn_ref)

        for k, part in enumerate((duv_ref, dq_ref, dk_ref, dv_ref, dg_ref, dag_ref)):
            dp_ref[:, 512 * k:512 * (k + 1)] = part[...]
        dh = _dot_nt(dp_ref[...], w_ref[...])
        xv = x_ref[...]
        r = _rms(xv)
        dn_ref[...] += _col_sum(dh * xv * r)
        dx_ref[...] = dxm_ref[...] + _rmsnorm_bwd(dh, xv, r, g_ref[...])

    half = lambda: pl.BlockSpec((tm, 512), lambda i: (i, 0))
    full = lambda: pl.BlockSpec((tm, D), lambda i: (i, 0))
    return _launch(
        body, (t // tm,),
        [half() for _ in range(6)] +
        [_resident((D, INW), lambda i: (0, 0)), full(), pl.BlockSpec((None, 1, D), lambda i: (l, 0, 0)), full()],
        [pl.BlockSpec((tm, INW), lambda i: (i, 0)), full(), pl.BlockSpec((1, D), lambda i: (0, 0))],
        [S((t, INW), BF16), S((t, D), F32), S((1, D), F32)], [],
        (d_uv, *dqkv, d_g, d_ag, w, x, g1, dxm), name, ("arbitrary",), comm)


class _PairSwap:
    def __init__(self, units):
        self.inputs = list(units)
        self.out_shape = [S((4,) + u.shape[1:], BF16) for u in units]
        n = len(units)
        self.scratch = [pltpu.SemaphoreType.DMA((n, 4)), pltpu.SemaphoreType.DMA((n, 4))]

    def run(self, phase, ins, outs, scr):
        ssem, rsem = scr
        x, y, c, _ = _place()
        copies = lambda: [pltpu.make_async_remote_copy(src_ref=ins[u].at[2 * chip + (1 - c)], dst_ref=outs[u].at[chip],
                                                       send_sem=ssem.at[u, chip], recv_sem=rsem.at[u, chip],
                                                       device_id=(x, y, 1 - c), device_id_type=MESH)
                          for u in range(len(self.inputs)) for chip in range(4)]
        if phase == "start":
            for cp in copies():
                cp.start()
        elif phase == "finish":
            for cp in copies():
                cp.wait_recv()
            for cp in copies():
                cp.wait_send()


def _pair_add(g, q, name):
    _, mm, nn = g.shape
    bm = _row_block(mm)

    def body(c_ref, g_ref, q_ref, h_ref):
        h_ref[...] = (g_ref[...].astype(F32) + q_ref[...].astype(F32)).astype(BF16)

    blk = lambda: pl.BlockSpec((None, bm, nn), lambda qq, i, c_ref: (qq, i, 0))
    return pl.pallas_call(
        body, name=name,
        grid_spec=pltpu.PrefetchScalarGridSpec(
            num_scalar_prefetch=1, grid=(4, mm // bm),
            in_specs=[pl.BlockSpec((None, None, bm, nn), lambda qq, i, c_ref: (qq, c_ref[0], i, 0)), blk()],
            out_specs=blk()),
        out_shape=S((4, mm, nn), BF16),
        compiler_params=_params("parallel", "parallel"),
    )(lax.axis_index("c").astype(jnp.int32).reshape(1), g.reshape(4, 2, mm, nn), q)


_BIG = ("w_in", "w_out", "w_ffn_in", "w_ffn_out")
_KIND = dict(w_in="cols", w_out="rows", w_ffn_in="lead", w_ffn_out="rows")
CWP = 128
EARLY_ROWS, LATE_ROWS = 152, 8


def _step(x, tgt, wts, sh, cw_pad):
    t = x.shape[0]
    rc = _ret_consts()
    cos2, sin2 = (jnp.asarray(a) for a in _rope_tables(t))
    n1 = wts["norm1_g"].reshape(LAYERS, 1, D)
    n2 = wts["norm2_g"].reshape(LAYERS, 1, D)
    gm_lng = wts["gm_ln_g"].reshape(LAYERS, 1, GMW)
    gm_lnb = wts["gm_ln_b"].reshape(LAYERS, 1, GMW)
    ws_bf = wts["gm_ws"].astype(BF16)
    wst_bf = jnp.swapaxes(wts["gm_ws"], 2, 3).astype(BF16)
    bias = jnp.repeat(jnp.swapaxes(wts["gm_bs"], 1, 2), GMW // HEADS, axis=2)
    cb = wts["conv_b"].reshape(LAYERS, 1, CVW)
    cv_lng = wts["conv_ln_g"].reshape(LAYERS, 1, CVW)
    cv_lnb = wts["conv_ln_b"].reshape(LAYERS, 1, CVW)
    unit = lambda f, l: (sh[f][l], _KIND[f])
    cshard = CVW // NDEV

    full = {f: [None] * LAYERS for f in _BIG}
    full["w_in"][0], cw_all = _comm_only(_Gather([unit("w_in", 0), (cw_pad, "lead")]), "gather_first")
    cw = jnp.transpose(cw_all[:, :, :, :cshard], (1, 2, 0, 3)).reshape(LAYERS, 32, CVW)

    gcf, gcb = rc["f"]["gC"], rc["b"]["gC"]
    saved = []
    for l in range(LAYERS):
        first = l == 0
        (proj, h1), got = _f_inproj(x, n1, full["w_in"][l], cos2, sin2, l, f"f_inproj_{l}",
                                    _Gather([unit("w_ffn_in", 0)]) if first else None)
        if first:
            full["w_ffn_in"][0], = got
        y_gm = _f_gm(proj, gm_lng, gm_lnb, ws_bf, bias, l, f"f_gm_{l}")
        s_f, s_b = _scan_pair(proj, None, 2, rc["f"]["ZETA"], rc["b"]["ZETA"], gcf, gcb, True, f"f_ret_state_{l}")
        o = _f_ret_out(proj, s_f, s_b, rc, f"f_ret_out_{l}")
        (c, y_cv), got = _f_conv(proj, cw, cb, cv_lng, cv_lnb, l, f"f_conv_{l}",
                                 _Gather([unit("w_out", 0)]) if first else None)
        if first:
            full["w_out"][0], = got
        (xm, ycat), got = _f_mixout(x, y_gm, y_cv, o, proj, full["w_out"][l], f"f_mixout_{l}",
                                    _Gather([unit("w_ffn_out", 0)]) if first else None)
        if first:
            full["w_ffn_out"][0], = got
        w1t = full["w_ffn_in"][l].reshape(2 * FFH, D)
        (xo, h2, gu, act), got = _f_ffn(xm, n2, w1t, full["w_ffn_out"][l], l, f"f_ffn_{l}",
                                        _Gather([unit(f, 1) for f in _BIG]) if first else None)
        if first:
            full["w_in"][1], full["w_out"][1], full["w_ffn_in"][1], full["w_ffn_out"][1] = got
        saved.append(dict(x=x, proj=proj, h1=h1, o=o, s_f=s_f, s_b=s_b, c=c, xm=xm, ycat=ycat, h2=h2, gu=gu, act=act))
        x = xo

    parts = {f: [None] * LAYERS for f in _BIG}
    small = [None] * LAYERS
    norm1 = [None] * LAYERS
    upper = None
    top = (x, wts["final_g"].reshape(1, D), tgt)
    for l in reversed(range(LAYERS)):
        sv = saved[l]
        outs, got = _b_ffn(top, sv["xm"], n2, sv["gu"], full["w_ffn_in"][l].reshape(2 * FFH, D), full["w_ffn_out"][l], l,
                           f"b_ffn_{l}", _Scatter(upper) if upper else None)
        dgu_t, dxm, dxo_bf, d_n2 = outs[:4]
        if l == LAYERS - 1:
            loss, d_final = outs[4:]
        if upper:
            for f, p in zip(_BIG, got):
                parts[f][l + 1] = p
        g_f2 = _mm_wgrad(sv["act"], dxo_bf, NDEV // 2, True, "shared", 1, f"g_ffn_out_{l}").reshape(NDEV, FFH // NDEV, D)
        g_f1 = _mm_wgrad(dgu_t, sv["h2"], NDEV, True, "shared", 1, f"g_ffn_in_{l}")
        (dxm_bf, dy_gm, dO, d_g, dc, d_cvlg, d_cvlb, d_cb), (q_f1, q_f2) = _b_mixout(
            dxm, full["w_out"][l], sv["o"], sv["proj"], sv["c"], cv_lng, cv_lnb, l, f"b_mixout_{l}", _PairSwap([g_f1, g_f2]))
        h_f1 = _pair_add(g_f1, q_f1, f"pair_add_w_ffn_in_{l}")
        h_f2 = _pair_add(g_f2, q_f2, f"pair_add_w_ffn_out_{l}")
        g_out = _mm_wgrad(sv["ycat"], dxm_bf, 1, False, "shared", 1, f"g_out_{l}").reshape(NDEV, D // NDEV, D)
        last = l == 0
        (d_uv, d_ws, _, d_bs_fold, d_gmlg, d_gmlb), (q_out,) = _b_gm(
            sv["proj"], dy_gm, gm_lng, gm_lnb, ws_bf, wst_bf, bias, l, f"b_gm_{l}", _PairSwap([g_out]))
        h_out = _pair_add(g_out, q_out, f"pair_add_w_out_{l}")
        (d_ag, d_cw), got = _b_conv(sv["proj"], dc, cw, l, f"b_conv_{l}", _Scatter([h_f1]) if last else None)
        if last:
            parts["w_ffn_in"][l], = got
        small[l] = dict(gm_ln_g=d_gmlg[0], gm_ln_b=d_gmlb[0], gm_ws=d_ws, gm_bs=d_bs_fold[:, :HEADS].T, conv_w=d_cw[:KW],
                        conv_b=d_cb[0], conv_ln_g=d_cvlg[0], conv_ln_b=d_cvlb[0], norm2_g=d_n2[0])
        comm = None
        if last:
            early_g = {k: jnp.stack([small[ll][k] for ll in range(LAYERS)]) for k in small[0]}
            early_g["final_g"] = d_final[0]
            early_buf = _pack([early_g[k] for k in _SMALL_EARLY] + [loss], EARLY_ROWS)
            comm = _Comms([_Scatter([h_out, h_f2]), _Gather([(early_buf, "lead")])])
        g_f, g_b = _scan_pair(sv["proj"], dO, 1, rc["f"]["XI"], rc["b"]["XI"], gcf, gcb, False, f"b_ret_state_{l}")
        dqkv, got = _b_ret_out(sv["proj"], cos2, sin2, dO, sv["s_f"], sv["s_b"], g_f, g_b, rc, f"b_ret_out_{l}", comm)
        if last:
            parts["w_out"][l], parts["w_ffn_out"][l], early_parts = got
        (dproj, top, d_n1), _ = _b_inproj(d_uv, dqkv, d_g, d_ag, full["w_in"][l], sv["x"], n1, dxm, l, f"b_inproj_{l}")
        norm1[l] = d_n1[0]
        g_in = _mm_wgrad(sv["h1"], dproj, NDEV, False, "cols", 4, f"g_in_{l}")
        q_in, = _comm_only(_PairSwap([g_in]), f"pair_swap_w_in_{l}")
        h_in = _pair_add(g_in, q_in, f"pair_add_w_in_{l}")
        if last:
            tail = [h_in]
        else:
            upper = [h_in, h_out, h_f1, h_f2]
    late_buf = _pack([jnp.stack(norm1)], LATE_ROWS)
    parts["w_in"][0], late_parts = _comm_only(_Comms([_Scatter(tail), _Gather([(late_buf, "lead")])]), "exchange_last")
    return loss, top, parts, (early_parts, late_parts)


def _adamw(w, g, m, v):
    m = ADAM_B1 * m + (1.0 - ADAM_B1) * g
    v = ADAM_B2 * v + (1.0 - ADAM_B2) * (g * g)
    m_hat = m / (1.0 - ADAM_B1 ** ADAM_STEP)
    v_hat = v / (1.0 - ADAM_B2 ** ADAM_STEP)
    return -ADAM_LR * (m_hat / (jnp.sqrt(v_hat) + ADAM_EPS) + ADAM_WD * w), m, v


def _cast_blocks(ws):
    def body(*refs):
        ins, outs = refs[:len(ws)], refs[len(ws):]
        for k, src in enumerate(ins):
            for l in range(LAYERS):
                outs[k * LAYERS + l][...] = src[l].astype(BF16)

    outs = pl.pallas_call(body, name="cast_blocks", out_shape=[S(w.shape[1:], BF16) for w in ws for _ in range(LAYERS)],
                          compiler_params=_params())(*ws)
    return [list(outs[k * LAYERS:(k + 1) * LAYERS]) for k in range(len(ws))]


def _row_block(mm):
    return next(b for b in (512, 352, 128) if mm % b == 0)


def _sum_adam(parts, w, m, v, l, prev, name):
    _, mm, nn = parts.shape
    bm = _row_block(mm)

    def body(p_ref, w_ref, m_ref, v_ref, *rest):
        g_ref, d_ref, nm_ref, nv_ref = rest[-4:]
        g = p_ref[0].astype(F32)
        for s in range(1, 4):
            g = g + p_ref[s].astype(F32)
        g_ref[...] = g
        d_ref[...], nm_ref[...], nv_ref[...] = _adamw(w_ref[...], g, m_ref[...], v_ref[...])

    blk = lambda: pl.BlockSpec((None, bm, nn), lambda i: (l, i, 0))
    prev = list(prev) if prev else []
    return pl.pallas_call(
        body, grid=(mm // bm,), name=name,
        in_specs=[pl.BlockSpec((4, bm, nn), lambda i: (0, i, 0)), blk(), blk(), blk()] + [_ANY] * len(prev),
        out_specs=[blk() for _ in range(4)],
        out_shape=[S(w.shape, F32) for _ in range(4)],
        input_output_aliases={4 + j: j for j in range(len(prev))},
        compiler_params=_params("parallel"),
    )(parts, w, m, v, *prev)


def _sum_small(parts):
    n = len(parts)

    def body(*refs):
        for p_ref, o_ref in zip(refs[:n], refs[n:]):
            g = p_ref[0]
            for s in range(1, NDEV):
                g = g + p_ref[s]
            o_ref[...] = g

    return pl.pallas_call(body, name="sum_small", out_shape=[S(p.shape[1:], F32) for p in parts],
                          compiler_params=_params())(*parts)


def _adam_small(quads):
    n = len(quads)
    as2d = lambda a: a.reshape(1, -1) if a.ndim == 1 else a

    def body(*refs):
        ins, outs = refs[:4 * n], refs[4 * n:]
        for k in range(n):
            g_ref, w_ref, m_ref, v_ref = ins[4 * k:4 * k + 4]
            outs[3 * k][...], outs[3 * k + 1][...], outs[3 * k + 2][...] = _adamw(w_ref[...], g_ref[...], m_ref[...], v_ref[...])

    flat = [as2d(a) for q in quads for a in q]
    outs = pl.pallas_call(body, name="adam_small", out_shape=[S(as2d(q[1]).shape, F32) for q in quads for _ in range(3)],
                          compiler_params=_params())(*flat)
    return [tuple(o.reshape(q[1].shape) for o in outs[3 * k:3 * k + 3]) for k, q in enumerate(quads)]


_SMALL = ("norm1_g", "gm_ln_g", "gm_ln_b", "gm_ws", "gm_bs", "conv_w", "conv_b", "conv_ln_g", "conv_ln_b",
          "norm2_g", "final_g")
_SMALL_EARLY = _SMALL[1:]
_NAMES = ("norm1_g", "w_in", "gm_ln_g", "gm_ln_b", "gm_ws", "gm_bs", "conv_w", "conv_b", "conv_ln_g", "conv_ln_b",
          "w_out", "norm2_g", "w_ffn_in", "w_ffn_out", "final_g")


def _pack(parts, rows):
    flat = jnp.concatenate([p.reshape(-1) for p in parts])
    return jnp.pad(flat, (0, rows * 1024 - flat.shape[0])).reshape(rows, 1024)


def _unpack(buf, shapes):
    flat = buf.reshape(-1)
    out, o = [], 0
    for shp in shapes:
        sz = int(np.prod(shp))
        out.append(flat[o:o + sz].reshape(shp))
        o += sz
    return out


def kernel(x, norm1_g, w_in, gm_ln_g, gm_ln_b, gm_ws, gm_bs, conv_w, conv_b, conv_ln_g, conv_ln_b, w_out, norm2_g, w_ffn_in, w_ffn_out, final_g, loss_target, m_norm1_g, m_w_in, m_gm_ln_g, m_gm_ln_b, m_gm_ws, m_gm_bs, m_conv_w, m_conv_b, m_conv_ln_g, m_conv_ln_b, m_w_out, m_norm2_g, m_w_ffn_in, m_w_ffn_out, m_final_g, v_norm1_g, v_w_in, v_gm_ln_g, v_gm_ln_b, v_gm_ws, v_gm_bs, v_conv_w, v_conv_b, v_conv_ln_g, v_conv_ln_b, v_w_out, v_norm2_g, v_w_ffn_in, v_w_ffn_out, v_final_g):
    w = dict(norm1_g=norm1_g, w_in=w_in, gm_ln_g=gm_ln_g, gm_ln_b=gm_ln_b, gm_ws=gm_ws, gm_bs=gm_bs, conv_w=conv_w,
             conv_b=conv_b, conv_ln_g=conv_ln_g, conv_ln_b=conv_ln_b, w_out=w_out, norm2_g=norm2_g, w_ffn_in=w_ffn_in,
             w_ffn_out=w_ffn_out, final_g=final_g)
    mo = dict(norm1_g=m_norm1_g, w_in=m_w_in, gm_ln_g=m_gm_ln_g, gm_ln_b=m_gm_ln_b, gm_ws=m_gm_ws, gm_bs=m_gm_bs,
              conv_w=m_conv_w, conv_b=m_conv_b, conv_ln_g=m_conv_ln_g, conv_ln_b=m_conv_ln_b, w_out=m_w_out,
              norm2_g=m_norm2_g, w_ffn_in=m_w_ffn_in, w_ffn_out=m_w_ffn_out, final_g=m_final_g)
    vo = dict(norm1_g=v_norm1_g, w_in=v_w_in, gm_ln_g=v_gm_ln_g, gm_ln_b=v_gm_ln_b, gm_ws=v_gm_ws, gm_bs=v_gm_bs,
              conv_w=v_conv_w, conv_b=v_conv_b, conv_ln_g=v_conv_ln_g, conv_ln_b=v_conv_ln_b, w_out=v_w_out,
              norm2_g=v_norm2_g, w_ffn_in=v_w_ffn_in, w_ffn_out=v_w_ffn_out, final_g=v_final_g)
    t = x.shape[1]
    me = 4 * lax.axis_index("x") + 2 * lax.axis_index("y") + lax.axis_index("c")
    cshard = conv_w.shape[2]

    cw_pad = jnp.pad(conv_w, ((0, 0), (0, 32 - KW), (0, CWP - cshard)))
    flip = lambda a: jnp.swapaxes(a, 1, 2)
    big = {f: tuple(flip(a[f]) if f == "w_ffn_in" else a[f] for a in (w, mo, vo)) for f in _BIG}
    sh = dict(zip(_BIG, _cast_blocks([big[f][0] for f in _BIG])))
    loss, dx, parts, small_parts = _step(x.reshape(t, D), loss_target.reshape(t, D), w, sh, cw_pad)

    grads, delta, new_m, new_v = {}, {}, {}, {}
    for f in _BIG:
        outs = None
        for l in reversed(range(LAYERS)):
            outs = _sum_adam(parts[f][l], *big[f], l, outs, f"sum_adam_{f}_{l}")
        grads[f], delta[f], new_m[f], new_v[f] = [flip(a) for a in outs] if f == "w_ffn_in" else outs

    early_sum, late_sum = _sum_small(small_parts)
    early_shapes = [(LAYERS, KW, CVW) if k == "conv_w" else w[k].shape for k in _SMALL_EARLY]
    grads["norm1_g"], = _unpack(late_sum, [w["norm1_g"].shape])
    *early, total = _unpack(early_sum, early_shapes + [()])
    for k, g in zip(_SMALL_EARLY, early):
        grads[k] = lax.dynamic_slice_in_dim(g, me * cshard, cshard, axis=2) if k == "conv_w" else g
    for k, (d, nm, nv) in zip(_SMALL, _adam_small([(grads[k], w[k], mo[k], vo[k]) for k in _SMALL])):
        delta[k], new_m[k], new_v[k] = d, nm, nv

    return (total, dx.reshape(1, t, D), *[grads[k] for k in _NAMES], *[delta[k] for k in _NAMES],
            *[new_m[k] for k in _NAMES], *[new_v[k] for k in _NAMES])
```

```python
import functools

import numpy as np
import jax
import jax.numpy as jnp
from jax import lax
from jax.experimental import pallas as pl
from jax.experimental.pallas import tpu as pltpu

F32, BF16 = jnp.float32, jnp.bfloat16
S = jax.ShapeDtypeStruct

D = 1024
INW = 3072
Q_BLK, K_BLK, V_BLK, G_BLK = 1, 2, 3, 4
CONV_A_BLK, CONV_GATE_BLK = 10, 11
GMW = 256
RETW = 512
CVW = 256
HEADS = 4
DH = 128
C = 128
KW = 31
HALO = 16
FFH = 2816
NDEV = 8
FFB = 2 * FFH // NDEV
FF_CHUNKS = ((0, 768), (768, 1536), (1536, 2304), (2304, FFH))
EPS = 1e-6
LAYERS = 2
SCALE = DH ** -0.5
VMEM_LIMIT = 56 * 1024 * 1024

ADAM_LR, ADAM_B1, ADAM_B2, ADAM_EPS, ADAM_WD, ADAM_STEP = 0.001, 0.9, 0.999, 1e-08, 0.01, 10

_SQRT_HALF = 0.7071067811865476
_INV_SQRT_2PI = 0.3989422804014327


def _params(*sem):
    return pltpu.CompilerParams(dimension_semantics=sem or None, vmem_limit_bytes=VMEM_LIMIT)


def _resident(shape, index_map):
    return pl.BlockSpec(shape, index_map, pipeline_mode=pl.Buffered(1))


def _dot(a, b):
    return jnp.dot(a, b, preferred_element_type=F32)


def _dot_nt(a, b):
    return lax.dot_general(a, b, (((1,), (1,)), ((), ())), preferred_element_type=F32)


def _dot_tn(a, b):
    return lax.dot_general(a, b, (((0,), (0,)), ((), ())), preferred_element_type=F32)


def _sigmoid(x):
    return 1.0 / (1.0 + jnp.exp(-x))


def _gelu_and_grad(x):
    cdf = 0.5 * (1.0 + lax.erf(x * _SQRT_HALF))
    return x * cdf, cdf + x * jnp.exp(-0.5 * x * x) * _INV_SQRT_2PI


def _silu_and_grad(x):
    s = _sigmoid(x)
    return x * s, s * (1.0 + x * (1.0 - s))


def _standardize(x):
    mu = jnp.mean(x, axis=-1, keepdims=True)
    d = x - mu
    rstd = lax.rsqrt(jnp.mean(d * d, axis=-1, keepdims=True) + EPS)
    return d * rstd, rstd


def _standardize_bwd(dxhat, xhat, rstd):
    m1 = jnp.mean(dxhat, axis=-1, keepdims=True)
    m2 = jnp.mean(dxhat * xhat, axis=-1, keepdims=True)
    return rstd * (dxhat - m1 - xhat * m2)


def _rms(x):
    return lax.rsqrt(jnp.mean(x * x, axis=-1, keepdims=True) + EPS)


def _rmsnorm_bwd(dy, x, r, g):
    u = dy * g
    return r * u - x * (r * r * r) * jnp.mean(u * x, axis=-1, keepdims=True)


def _col_sum(a):
    return jnp.sum(a, axis=0, keepdims=True)


def _rot(t, cos2, sin2):
    return t * cos2 + pltpu.roll(t, DH // 2, axis=1) * sin2


def _rot_t(dt, cos2, sin2):
    return dt * cos2 + pltpu.roll(dt * sin2, DH // 2, axis=1)


MESH = pl.DeviceIdType.MESH
_HBM = pl.BlockSpec(memory_space=pltpu.HBM)
_ANY = pl.BlockSpec(memory_space=pl.ANY)


def _place():
    x, y, c = lax.axis_index("x"), lax.axis_index("y"), lax.axis_index("c")
    return x, y, c, ((1 - x, y), (x, 1 - y), (1 - x, 1 - y))


def _slot(full, kind, width, i):
    if kind == "cols":
        return full.at[:, pl.ds(pl.multiple_of(i * width, 128), width)]
    if kind == "rows":
        return full.at[pl.ds(pl.multiple_of(i * width, 16), width), :]
    return full.at[i]


class _Gather:
    def __init__(self, units):
        self.units = units
        self.inputs = [u[0] for u in units]
        self.out_shape = []
        for src, kind in units:
            r, c = src.shape[-2:]
            shape = {"cols": (r, NDEV * c), "rows": (NDEV * r, c), "lead": (NDEV,) + src.shape}[kind]
            self.out_shape.append(S(shape, src.dtype))
        n = len(units)
        self.scratch = [pltpu.SemaphoreType.DMA((n, 7)), pltpu.SemaphoreType.DMA((n, 7)), pltpu.SemaphoreType.DMA((n,))]

    def run(self, phase, ins, outs, scr):
        ssem, rsem, lsem = scr
        x, y, c, chips = _place()
        me, sib = 4 * x + 2 * y + c, (x, y, 1 - c)
        idx = lambda chip, core: 4 * chip[0] + 2 * chip[1] + core
        for u, (src_arr, kind) in enumerate(self.units):
            src, full = ins[u], outs[u]
            width = src_arr.shape[-1] if kind == "cols" else src_arr.shape[-2]
            slot = functools.partial(_slot, full, kind, width)

            def copy(k, block, to, from_src=False):
                return pltpu.make_async_remote_copy(src_ref=src if from_src else slot(block), dst_ref=slot(block),
                                                    send_sem=ssem.at[u, k], recv_sem=rsem.at[u, k],
                                                    device_id=to, device_id_type=MESH)

            mine = lambda: pltpu.make_async_copy(src, slot(me), lsem.at[u])
            first = lambda: [copy(0, me, sib, True)] + [copy(1 + j, me, (*chip, c), True) for j, chip in enumerate(chips)]
            passed = lambda j: copy(4 + j, idx(chips[j], c), sib)
            if phase == "start":
                mine().start()
                for cp in first():
                    cp.start()
            elif phase == "forward":
                for j, chip in enumerate(chips):
                    copy(1 + j, idx(chip, c), sib).wait_recv()
                    passed(j).start()
            else:
                copy(0, idx((x, y), 1 - c), sib).wait_recv()
                for j, chip in enumerate(chips):
                    copy(4 + j, idx(chip, 1 - c), sib).wait_recv()
                for cp in first() + [passed(j) for j in range(3)]:
                    cp.wait_send()
                mine().wait()


class _Scatter:
    def __init__(self, units):
        self.units = units
        self.inputs = list(units)
        self.out_shape = [S(u.shape, u.dtype) for u in units]
        n = len(units)
        self.scratch = [pltpu.SemaphoreType.DMA((n, 3)), pltpu.SemaphoreType.DMA((n, 3)), pltpu.SemaphoreType.DMA((n,))]

    def run(self, phase, ins, outs, scr):
        ssem, rsem, lsem = scr
        x, y, c, chips = _place()
        myq = 2 * x + y
        for u in range(len(self.units)):
            h, p = ins[u], outs[u]

            def copy(k, chip, send_to_them):
                q = 2 * chip[0] + chip[1]
                return pltpu.make_async_remote_copy(src_ref=h.at[q], dst_ref=p.at[myq if send_to_them else q],
                                                    send_sem=ssem.at[u, k], recv_sem=rsem.at[u, k],
                                                    device_id=(*chip, c), device_id_type=MESH)

            mine = lambda: pltpu.make_async_copy(h.at[myq], p.at[myq], lsem.at[u])
            sends = lambda: [copy(k, chip, True) for k, chip in enumerate(chips)]
            if phase == "start":
                mine().start()
                for cp in sends():
                    cp.start()
            elif phase == "finish":
                for k, chip in enumerate(chips):
                    copy(k, chip, False).wait_recv()
                for cp in sends():
                    cp.wait_send()
                mine().wait()


class _Comms:
    def __init__(self, parts):
        self.parts = parts
        self.inputs = [a for p in parts for a in p.inputs]
        self.out_shape = [a for p in parts for a in p.out_shape]
        self.scratch = [a for p in parts for a in p.scratch]

    def run(self, phase, ins, outs, scr):
        i = o = s = 0
        for p in self.parts:
            ni, no, ns = len(p.inputs), len(p.out_shape), len(p.scratch)
            p.run(phase, ins[i:i + ni], outs[o:o + no], scr[s:s + ns])
            i, o, s = i + ni, o + no, s + ns


def _launch(body, grid, in_specs, out_specs, out_shape, scratch, args, name, sem, comm=None):
    if comm is None:
        outs = pl.pallas_call(body, grid=grid, name=name, in_specs=in_specs, out_specs=out_specs, out_shape=out_shape,
                              scratch_shapes=scratch, compiler_params=_params(*sem))(*args)
        return list(outs), []
    n_in, n_out, n_scr = len(args), len(out_shape), len(scratch)
    ci, co = len(comm.inputs), len(comm.out_shape)
    nsteps = int(np.prod(grid))
    fwd_step = (7 * nsteps) // 8

    def hosted(*refs):
        a = refs[:n_in]
        ca = refs[n_in:n_in + ci]
        o = refs[n_in + ci:n_in + ci + n_out]
        cout = refs[n_in + ci + n_out:n_in + ci + n_out + co]
        s = refs[n_in + ci + n_out + co:n_in + ci + n_out + co + n_scr]
        cs = refs[n_in + ci + n_out + co + n_scr:]
        step = pl.program_id(0)
        for d in range(1, len(grid)):
            step = step * grid[d] + pl.program_id(d)

        @pl.when(step == 0)
        def _():
            comm.run("start", ca, cout, cs)

        body(*a, *o, *s)

        @pl.when(step == fwd_step)
        def _():
            comm.run("forward", ca, cout, cs)

        @pl.when(step == nsteps - 1)
        def _():
            comm.run("finish", ca, cout, cs)

    outs = pl.pallas_call(
        hosted, grid=grid, name=name, in_specs=list(in_specs) + [_HBM] * ci, out_specs=list(out_specs) + [_HBM] * co,
        out_shape=list(out_shape) + comm.out_shape, scratch_shapes=list(scratch) + comm.scratch,
        compiler_params=_params(*["arbitrary"] * len(grid)))(*args, *comm.inputs)
    return list(outs[:n_out]), list(outs[n_out:])


def _comm_only(comm, name):
    ci, co = len(comm.inputs), len(comm.out_shape)

    def body(*refs):
        ca, cout, cs = refs[:ci], refs[ci:ci + co], refs[ci + co:]
        for phase in ("start", "forward", "finish"):
            comm.run(phase, ca, cout, cs)

    return pl.pallas_call(body, name=name, in_specs=[_HBM] * ci, out_specs=[_HBM] * co, out_shape=comm.out_shape,
                          scratch_shapes=comm.scratch, compiler_params=_params())(*comm.inputs)


def _ret_consts():
    idx = np.arange(C, dtype=np.float32)
    gf = (1.0 - np.exp2(-5.0 - np.arange(HEADS, dtype=np.float32))).astype(np.float32)
    out = {}
    for name, gamma, fwd in (("f", gf, True), ("b", gf[::-1].copy(), False)):
        lg = np.log(gamma).astype(np.float32)[:, None]
        diff = idx[:, None] - idx[None, :]
        if fwd:
            mask = diff >= 0
            dist = np.where(mask, diff, 0.0)
            zeta = np.exp(lg * (C - 1 - idx))
            xi = np.exp(lg * (idx + 1))
        else:
            mask = diff < 0
            dist = np.where(mask, -diff, 0.0)
            zeta = np.exp(lg * idx)
            xi = np.exp(lg * (C - idx))
        dm = np.where(mask[None], np.exp(lg[:, :, None] * dist[None]), 0.0).astype(np.float32)
        bc = lambda vec: np.ascontiguousarray(np.broadcast_to(vec.astype(np.float32)[:, :, None], (HEADS, C, DH)))
        out[name] = dict(D=dm, XI=bc(xi), ZETA=bc(zeta), gC=[float(v) for v in np.exp(lg[:, 0] * C).astype(np.float32)])
    return out


def _rope_tables(t):
    half = DH // 2
    inv_freq = (np.float32(10000.0) ** (-np.arange(half, dtype=np.float32) / np.float32(half))).astype(np.float32)
    ang = (np.arange(t, dtype=np.float32)[:, None] * inv_freq[None, :]).astype(np.float64)
    cos, sin = np.cos(ang).astype(np.float32), np.sin(ang).astype(np.float32)
    return np.concatenate([cos, cos], axis=1), np.concatenate([-sin, sin], axis=1)


def _f_inproj(x, g1, w, cos2, sin2, l, name, comm=None):
    t = x.shape[0]
    tm = 512 if comm else min(1024, t)

    def body(x_ref, g_ref, w_ref, cos_ref, sin_ref, proj_ref, ht_ref):
        xv = x_ref[...]
        h = (xv * _rms(xv) * g_ref[...]).astype(BF16)
        ht_ref[...] = h.T
        for nb in range(INW // 512):
            cs = slice(nb * 512, (nb + 1) * 512)
            res = _dot(h, w_ref[:, cs])
            if nb in (Q_BLK, K_BLK):
                for hh in range(HEADS):
                    r = _rot(res[:, hh * DH:(hh + 1) * DH], cos_ref[...], sin_ref[...])
                    proj_ref[:, nb * 512 + hh * DH:nb * 512 + (hh + 1) * DH] = (r * SCALE if nb == K_BLK else r).astype(BF16)
            else:
                proj_ref[:, cs] = res.astype(BF16)

    return _launch(
        body, (t // tm,),
        [pl.BlockSpec((tm, D), lambda i: (i, 0)),
         pl.BlockSpec((None, 1, D), lambda i: (l, 0, 0)),
         _resident((D, INW), lambda i: (0, 0)),
         pl.BlockSpec((tm, DH), lambda i: (i, 0)), pl.BlockSpec((tm, DH), lambda i: (i, 0))],
        [pl.BlockSpec((tm, INW), lambda i: (i, 0)), pl.BlockSpec((D, tm), lambda i: (0, i))],
        [S((t, INW), BF16), S((D, t), BF16)], [], (x, g1, w, cos2, sin2), name, ("parallel",), comm)


def _gm_chunk_fwd(u, v, lng, lnb, ws_ref, bias):
    au, dau = _gelu_and_grad(u)
    av, dav = _gelu_and_grad(v)
    vhat, rstd = _standardize(av)
    vn = (vhat * lng + lnb).astype(BF16)
    head = lax.broadcasted_iota(jnp.int32, (C, GMW), 1) // (GMW // HEADS)
    mixed = bias
    for h in range(HEADS):
        mixed = mixed + jnp.where(head == h, _dot(ws_ref[h], vn), 0.0)
    return au, dau, dav, vhat, rstd, vn, mixed, head


def _f_gm(proj, lng, lnb, ws_bf, bias, l, name):
    t = proj.shape[0]
    tm = min(1024, t)

    def body(p_ref, lng_ref, lnb_ref, ws_ref, bias_ref, y_ref):
        for ci in range(tm // C):
            rows = slice(ci * C, (ci + 1) * C)
            u = p_ref[rows, 0:GMW].astype(F32)
            v = p_ref[rows, GMW:2 * GMW].astype(F32)
            au, _, _, _, _, _, mixed, _ = _gm_chunk_fwd(u, v, lng_ref[...], lnb_ref[...], ws_ref, bias_ref[...])
            y_ref[rows, :] = (au * mixed).astype(BF16)

    return pl.pallas_call(
        body, grid=(t // tm,), name=name,
        in_specs=[pl.BlockSpec((tm, 2 * GMW), lambda i: (i, 0)),
                  pl.BlockSpec((None, 1, GMW), lambda i: (l, 0, 0)),
                  pl.BlockSpec((None, 1, GMW), lambda i: (l, 0, 0)),
                  pl.BlockSpec((None, HEADS, C, C), lambda i: (l, 0, 0, 0)),
                  pl.BlockSpec((None, C, GMW), lambda i: (l, 0, 0))],
        out_specs=pl.BlockSpec((tm, GMW), lambda i: (i, 0)),
        out_shape=S((t, GMW), BF16),
        compiler_params=_params("parallel"),
    )(proj, lng, lnb, ws_bf, bias)


def _scan_pair(proj, other, col, wf, wb, gcf, gcb, first_is_f, name):
    t = proj.shape[0]
    n = t // C
    sc = min(SCAN_CHUNKS, n)
    nsteps = n // sc
    other_is_proj = other is None

    def body(a1, o1, a2, o2, w1_ref, w2_ref, out1, out2, st1, st2):
        @pl.when(pl.program_id(0) == 0)
        def _():
            st1[...] = jnp.zeros_like(st1)
            st2[...] = jnp.zeros_like(st2)

        def one(a_ref, o_ref, w_ref, gc, st, out, order):
            for h in range(HEADS):
                sl = slice(h * DH, (h + 1) * DH)
                incs = {}
                for j in order:
                    rows = slice(j * C, (j + 1) * C)
                    aw = (a_ref[rows, sl].astype(F32) * w_ref[h]).astype(BF16)
                    incs[j] = _dot_tn(aw, o_ref[rows, sl].astype(BF16))
                cur = st[h]
                for j in order:
                    out[j, h] = cur.astype(BF16)
                    cur = gc[h] * cur + incs[j]
                st[h] = cur

        g1, g2 = (gcf, gcb) if first_is_f else (gcb, gcf)
        one(a1, o1, w1_ref, g1, st1, out1, range(sc))
        one(a2, o2, w2_ref, g2, st2, out2, range(sc - 1, -1, -1))

    up = lambda i: i
    down = lambda i: nsteps - 1 - i

    def specs(ix):
        o_spec = pl.BlockSpec((sc * C, RETW), lambda i: (ix(i), V_BLK if other_is_proj else 0))
        return [pl.BlockSpec((sc * C, RETW), lambda i: (ix(i), col)), o_spec]

    const = lambda: pl.BlockSpec((HEADS, C, DH), lambda i: (0, 0, 0))
    oth = proj if other_is_proj else other
    w1, w2 = (wf, wb) if first_is_f else (wb, wf)
    out1, out2 = pl.pallas_call(
        body, grid=(nsteps,), name=name,
        in_specs=specs(up) + specs(down) + [const(), const()],
        out_specs=[pl.BlockSpec((sc, HEADS, DH, DH), lambda i: (up(i), 0, 0, 0)),
                   pl.BlockSpec((sc, HEADS, DH, DH), lambda i: (down(i), 0, 0, 0))],
        out_shape=[S((n, HEADS, DH, DH), BF16), S((n, HEADS, DH, DH), BF16)],
        scratch_shapes=[pltpu.VMEM((HEADS, DH, DH), F32), pltpu.VMEM((HEADS, DH, DH), F32)],
        compiler_params=_params("arbitrary"),
    )(proj, oth, proj, oth, w1, w2)
    return (out1, out2) if first_is_f else (out2, out1)


SCAN_CHUNKS = 16


RET_CHUNKS = 8
RET_FWD_CHUNKS = 16


def _f_ret_out(proj, s_f, s_b, rc, name):
    t = proj.shape[0]
    nchunks = min(RET_FWD_CHUNKS, t // C)
    tm = nchunks * C

    def body(q_ref, k_ref, v_ref, sf_ref, sb_ref, d_ref, xif_ref, xib_ref, o_ref):
        for ci in range(nchunks):
            rows = slice(ci * C, (ci + 1) * C)
            for h in range(HEADS):
                sl = slice(h * DH, (h + 1) * DH)
                qh, kh, vh = q_ref[rows, sl], k_ref[rows, sl], v_ref[rows, sl]
                p = (_dot_nt(qh, kh) * d_ref[h]).astype(BF16)
                cross = _dot(qh, jnp.concatenate([sf_ref[ci, h], sb_ref[ci, h]], axis=1))
                o_ref[rows, sl] = _dot(p, vh) + xif_ref[h] * cross[:, 0:DH] + xib_ref[h] * cross[:, DH:2 * DH]

    const = lambda: pl.BlockSpec((HEADS, C, DH), lambda i: (0, 0, 0))
    state = lambda: pl.BlockSpec((nchunks, HEADS, DH, DH), lambda i: (i, 0, 0, 0))
    return pl.pallas_call(
        body, grid=(t // tm,), name=name,
        in_specs=[pl.BlockSpec((tm, RETW), lambda i, cb=cb: (i, cb)) for cb in (Q_BLK, K_BLK, V_BLK)] +
                 [state(), state(), const(), const(), const()],
        out_specs=pl.BlockSpec((tm, RETW), lambda i: (i, 0)),
        out_shape=S((t, RETW), F32),
        compiler_params=_params("parallel"),
    )(proj, proj, proj, s_f, s_b, rc["f"]["D"] + rc["b"]["D"], rc["f"]["XI"], rc["b"]["XI"])


def _conv_halo_specs(t, tm, width, col):
    r = tm // HALO
    last = t // HALO - 1
    return [pl.BlockSpec((HALO, width), lambda i: (jnp.maximum(i * r - 1, 0), col)),
            pl.BlockSpec((tm, width), lambda i: (i, col)),
            pl.BlockSpec((HALO, width), lambda i: (jnp.minimum((i + 1) * r, last), col))]


def _fill_ext(ext, prev, cur, nxt, i, nt, tm):
    ext[0:HALO, :] = jnp.where(i > 0, prev, 0.0)
    ext[HALO:HALO + tm, :] = cur
    ext[HALO + tm:2 * HALO + tm, :] = jnp.where(i < nt - 1, nxt, 0.0)


def _glu(a_ref, g_ref):
    return a_ref[...].astype(F32) * _sigmoid(g_ref[...].astype(F32))


def _shifted_copies(ext, rot, tm):
    rows = tm + 2 * HALO - 8
    for b in range(1, 8):
        rot[b - 1, :, :] = ext[pl.ds(b, rows), :]


def _tap(ext, rot, r0, s, rb):
    a, b = divmod(s, 8)
    return ext[pl.ds(r0 + 8 * a, rb), :] if b == 0 else rot[b - 1, pl.ds(r0 + 8 * a, rb), :]


def _f_conv(proj, cw, cb, lng, lnb, l, name, comm=None):
    t = proj.shape[0]
    tm = 512
    nt = t // tm
    rb = 64

    def body(ap, ac, an, gp, gc, gn, cw_ref, cb_ref, lng_ref, lnb_ref, c_ref, y_ref, hext, hrot):
        i = pl.program_id(0)
        _fill_ext(hext, _glu(ap, gp), _glu(ac, gc), _glu(an, gn), i, nt, tm)
        _shifted_copies(hext, hrot, tm)
        for r0 in range(0, tm, rb):
            acc = jnp.zeros((rb, CVW), F32) + cb_ref[...]
            for j in range(KW):
                acc = acc + cw_ref[j:j + 1, :] * _tap(hext, hrot, r0, j + 1, rb)
            c_ref[r0:r0 + rb, :] = acc
            chat, _ = _standardize(acc)
            z = chat * lng_ref[...] + lnb_ref[...]
            y_ref[r0:r0 + rb, :] = (z * _sigmoid(z)).astype(BF16)

    vec = lambda: pl.BlockSpec((None, 1, CVW), lambda i: (l, 0, 0))
    return _launch(
        body, (nt,),
        _conv_halo_specs(t, tm, CVW, CONV_A_BLK) + _conv_halo_specs(t, tm, CVW, CONV_GATE_BLK) +
        [pl.BlockSpec((None, 32, CVW), lambda i: (l, 0, 0)), vec(), vec(), vec()],
        [pl.BlockSpec((tm, CVW), lambda i: (i, 0)), pl.BlockSpec((tm, CVW), lambda i: (i, 0))],
        [S((t, CVW), F32), S((t, CVW), BF16)],
        [pltpu.VMEM((tm + 2 * HALO, CVW), F32), pltpu.VMEM((7, tm + 2 * HALO - 8, CVW), F32)],
        (proj, proj, proj, proj, proj, proj, cw, cb, lng, lnb), name, ("parallel",), comm)


def _f_mixout(x, y_gm, y_cv, o, proj, w, name, comm=None):
    t = x.shape[0]
    tm = 512 if comm else min(1024, t)

    def body(x_ref, ygm_ref, ycv_ref, o_ref, g_ref, w_ref, xm_ref, ycat_t_ref, ycat):
        ycat[:, 0:GMW] = ygm_ref[...]
        ycat[:, GMW + RETW:D] = ycv_ref[...]
        for h in range(HEADS):
            sl = slice(h * DH, (h + 1) * DH)
            ohat, _ = _standardize(o_ref[:, sl])
            g = g_ref[:, sl].astype(F32)
            ycat[:, GMW + h * DH:GMW + (h + 1) * DH] = (ohat * (g * _sigmoid(g))).astype(BF16)
        yc = ycat[...]
        ycat_t_ref[...] = yc.T
        xm_ref[...] = x_ref[...] + _dot(yc, w_ref[...])

    return _launch(
        body, (t // tm,),
        [pl.BlockSpec((tm, D), lambda i: (i, 0)),
         pl.BlockSpec((tm, GMW), lambda i: (i, 0)),
         pl.BlockSpec((tm, CVW), lambda i: (i, 0)),
         pl.BlockSpec((tm, RETW), lambda i: (i, 0)),
         pl.BlockSpec((tm, RETW), lambda i: (i, G_BLK)),
         _resident((D, D), lambda i: (0, 0))],
        [pl.BlockSpec((tm, D), lambda i: (i, 0)), pl.BlockSpec((D, tm), lambda i: (0, i))],
        [S((t, D), F32), S((D, t), BF16)],
        [pltpu.VMEM((tm, D), BF16)], (x, y_gm, y_cv, o, proj, w), name, ("parallel",), comm)


def _f_ffn(xm, g2, w1, w2, l, name, comm=None):
    t = xm.shape[0]
    tm = 512

    def body(x_ref, g_ref, w1_ref, w2_ref, xo_ref, h_ref, gu_ref, act_t_ref):
        xv = x_ref[...]
        h = (xv * _rms(xv) * g_ref[...]).astype(BF16)
        h_ref[...] = h
        acc = xv
        for a, b in FF_CHUNKS:
            gate = _dot_nt(h, w1_ref[a:b, :])
            up = _dot_nt(h, w1_ref[FFH + a:FFH + b, :])
            gu_ref[:, a:b] = gate.astype(BF16)
            gu_ref[:, FFH + a:FFH + b] = up.astype(BF16)
            av = ((gate * _sigmoid(gate)) * up).astype(BF16)
            act_t_ref[a:b, :] = av.T
            acc = acc + _dot(av, w2_ref[a:b, :])
        xo_ref[...] = acc

    return _launch(
        body, (t // tm,),
        [pl.BlockSpec((tm, D), lambda i: (i, 0)),
         pl.BlockSpec((None, 1, D), lambda i: (l, 0, 0)),
         _resident((2 * FFH, D), lambda i: (0, 0)),
         _resident((FFH, D), lambda i: (0, 0))],
        [pl.BlockSpec((tm, D), lambda i: (i, 0)), pl.BlockSpec((tm, D), lambda i: (i, 0)),
         pl.BlockSpec((tm, 2 * FFH), lambda i: (i, 0)), pl.BlockSpec((FFH, tm), lambda i: (0, i))],
        [S((t, D), F32), S((t, D), BF16), S((t, 2 * FFH), BF16), S((FFH, t), BF16)],
        [], (xm, g2, w1, w2), name, ("parallel",), comm)


def _b_ffn(top, xm, g2, gu, w1, w2, l, name, comm=None):
    t = xm.shape[0]
    tm = 256
    from_loss = isinstance(top, tuple)
    n_top = 3 if from_loss else 1

    def body(*refs):
        top_refs = refs[:n_top]
        x_ref, g_ref, gu_ref, w1_ref, w2_ref, dgu_ref, dxm_ref, dxb_ref, dg_ref = refs[n_top:n_top + 9]
        first = pl.program_id(0) == 0

        @pl.when(first)
        def _():
            dg_ref[...] = jnp.zeros_like(dg_ref)

        if from_loss:
            xo_ref, fg_ref, t_ref = top_refs
            loss_ref, dfg_ref = refs[n_top + 9:n_top + 11]

            @pl.when(first)
            def _():
                loss_ref[...] = jnp.zeros_like(loss_ref)
                dfg_ref[...] = jnp.zeros_like(dfg_ref)

            xo = xo_ref[...]
            ro = _rms(xo)
            xr = xo * ro
            err = xr * fg_ref[...] - t_ref[...]
            loss_ref[...] += (0.5 / D) * _col_sum(jnp.sum(err * err, axis=1, keepdims=True))
            dy = err * (1.0 / D)
            dfg_ref[...] += _col_sum(dy * xr)
            dxo = _rmsnorm_bwd(dy, xo, ro, fg_ref[...])
        else:
            dxo = top_refs[0][...]
        dxb = dxo.astype(BF16)
        dxb_ref[...] = dxb
        dh = jnp.zeros((tm, D), F32)
        for a, b in FF_CHUNKS:
            dact = _dot_nt(dxb, w2_ref[a:b, :])
            gate = gu_ref[:, a:b].astype(F32)
            up = gu_ref[:, FFH + a:FFH + b].astype(F32)
            sg, dsg = _silu_and_grad(gate)
            dgate = (dact * up * dsg).astype(BF16)
            dup = (dact * sg).astype(BF16)
            dgu_ref[a:b, :] = dgate.T
            dgu_ref[FFH + a:FFH + b, :] = dup.T
            dh = dh + _dot(dgate, w1_ref[a:b, :]) + _dot(dup, w1_ref[FFH + a:FFH + b, :])
        xv = x_ref[...]
        r = _rms(xv)
        dg_ref[...] += _col_sum(dh * xv * r)
        dxm_ref[...] = dxo + _rmsnorm_bwd(dh, xv, r, g_ref[...])

    tok = lambda: pl.BlockSpec((tm, D), lambda i: (i, 0))
    vec = lambda: pl.BlockSpec((1, D), lambda i: (0, 0))
    top_specs = [tok(), vec(), tok()] if from_loss else [tok()]
    extra_specs = [pl.BlockSpec((1, 1), lambda i: (0, 0)), vec()] if from_loss else []
    extra_shape = [S((1, 1), F32), S((1, D), F32)] if from_loss else []
    return _launch(
        body, (t // tm,),
        top_specs + [tok(), pl.BlockSpec((None, 1, D), lambda i: (l, 0, 0)),
                     pl.BlockSpec((tm, 2 * FFH), lambda i: (i, 0)),
                     _resident((2 * FFH, D), lambda i: (0, 0)),
                     _resident((FFH, D), lambda i: (0, 0))],
        [pl.BlockSpec((2 * FFH, tm), lambda i: (0, i)), tok(), tok(), vec()] + extra_specs,
        [S((2 * FFH, t), BF16), S((t, D), F32), S((t, D), BF16), S((1, D), F32)] + extra_shape,
        [],
        ((*top,) if from_loss else (top,)) + (xm, g2, gu, w1, w2), name, ("arbitrary",), comm)


def _mm_wgrad(at, b, pieces, at_rows, b_mode, group, name):
    t = at.shape[-1]
    ka = at.shape[0] // pieces if at_rows else at.shape[0]
    width = group * (b.shape[1] // pieces if b_mode == "cols" else b.shape[-1])
    bt = next(c for c in (4096, 2048, 1024) if 4 * c * (ka + width) + 4 * ka * width <= 2 * VMEM_LIMIT // 3)
    bt = min(bt, t)
    nt = t // bt
    if at_rows:
        a_spec = pl.BlockSpec((ka, bt), lambda j, tt: (j, tt))
    else:
        a_spec = pl.BlockSpec((ka, bt), lambda j, tt: (0, tt))
    if b_mode == "shared":
        nb, b_spec = b.shape[1], pl.BlockSpec((bt, b.shape[1]), lambda j, tt: (tt, 0))
    elif b_mode == "cols":
        nb = b.shape[1] // pieces
        b_spec = pl.BlockSpec((bt, group * nb), lambda j, tt: (tt, j))
    else:
        nb, b_spec = b.shape[2], pl.BlockSpec((None, bt, b.shape[2]), lambda j, tt: (j, tt, 0))
    assert group == 1 or b_mode == "cols"

    def body(a_ref, b_ref, o_ref, acc):
        tt = pl.program_id(1)

        @pl.when(tt == 0)
        def _():
            acc[...] = jnp.zeros_like(acc)

        acc[...] += _dot(a_ref[...], b_ref[...])

        @pl.when(tt == nt - 1)
        def _():
            for k in range(group):
                o_ref[k] = acc[:, k * nb:(k + 1) * nb].astype(BF16)

    return pl.pallas_call(
        body, grid=(pieces // group, nt), name=name,
        in_specs=[a_spec, b_spec],
        out_specs=pl.BlockSpec((group, ka, nb), lambda j, tt: (j, 0, 0)),
        out_shape=S((pieces, ka, nb), BF16),
        scratch_shapes=[pltpu.VMEM((ka, group * nb), F32)],
        compiler_params=_params("parallel", "arbitrary"),
    )(at, b)


def _b_mixout(dxm, w, o, proj, c, lng, lnb, l, name, comm=None):
    t = dxm.shape[0]
    tm = 512

    def body(dxm_ref, w_ref, o_ref, g_ref, c_ref, lng_ref, lnb_ref,
             dxb_ref, dygm_ref, dO_ref, dg_ref, dc_ref, dlg_ref, dlb_ref, dcb_ref):
        @pl.when(pl.program_id(0) == 0)
        def _():
            dlg_ref[...] = jnp.zeros_like(dlg_ref)
            dlb_ref[...] = jnp.zeros_like(dlb_ref)
            dcb_ref[...] = jnp.zeros_like(dcb_ref)

        dxb = dxm_ref[...].astype(BF16)
        dxb_ref[...] = dxb
        dy = _dot_nt(dxb, w_ref[...])
        dygm_ref[...] = dy[:, 0:GMW]
        for h in range(HEADS):
            sl = slice(h * DH, (h + 1) * DH)
            ohat, rstd = _standardize(o_ref[:, sl])
            sg, dsg = _silu_and_grad(g_ref[:, sl].astype(F32))
            dyr = dy[:, GMW + h * DH:GMW + (h + 1) * DH]
            dg_ref[:, sl] = (dyr * ohat * dsg).astype(BF16)
            dO_ref[:, sl] = _standardize_bwd(dyr * sg, ohat, rstd)
        chat, rstd = _standardize(c_ref[...])
        z = chat * lng_ref[...] + lnb_ref[...]
        _, dsz = _silu_and_grad(z)
        dz = dy[:, GMW + RETW:D] * dsz
        dlg_ref[...] += _col_sum(dz * chat)
        dlb_ref[...] += _col_sum(dz)
        dc = _standardize_bwd(dz * lng_ref[...], chat, rstd)
        dcb_ref[...] += _col_sum(dc)
        dc_ref[...] = dc

    vec = lambda: pl.BlockSpec((None, 1, CVW), lambda i: (l, 0, 0))
    acc = lambda: pl.BlockSpec((1, CVW), lambda i: (0, 0))
    return _launch(
        body, (t // tm,),
        [pl.BlockSpec((tm, D), lambda i: (i, 0)),
         _resident((D, D), lambda i: (0, 0)),
         pl.BlockSpec((tm, RETW), lambda i: (i, 0)),
         pl.BlockSpec((tm, RETW), lambda i: (i, G_BLK)),
         pl.BlockSpec((tm, CVW), lambda i: (i, 0)), vec(), vec()],
        [pl.BlockSpec((tm, D), lambda i: (i, 0)), pl.BlockSpec((tm, GMW), lambda i: (i, 0)),
         pl.BlockSpec((tm, RETW), lambda i: (i, 0)), pl.BlockSpec((tm, RETW), lambda i: (i, 0)),
         pl.BlockSpec((tm, CVW), lambda i: (i, 0)), acc(), acc(), acc()],
        [S((t, D), BF16), S((t, GMW), F32), S((t, RETW), F32), S((t, RETW), BF16), S((t, CVW), F32),
         S((1, CVW), F32), S((1, CVW), F32), S((1, CVW), F32)],
        [], (dxm, w, o, proj, c, lng, lnb), name, ("arbitrary",), comm)


def _b_gm(proj, dy, lng, lnb, ws_bf, wst_bf, bias, l, name, comm=None):
    t = proj.shape[0]
    tm = min(1024, t)
    nt = t // tm

    def body(p_ref, dy_ref, lng_ref, lnb_ref, ws_ref, wst_ref, bias_ref,
             duv_ref, dws_ref, dbias_ref, dbs_ref, dlg_ref, dlb_ref):
        @pl.when(pl.program_id(0) == 0)
        def _():
            dws_ref[...] = jnp.zeros_like(dws_ref)
            dbias_ref[...] = jnp.zeros_like(dbias_ref)
            dbs_ref[...] = jnp.zeros_like(dbs_ref)
            dlg_ref[...] = jnp.zeros_like(dlg_ref)
            dlb_ref[...] = jnp.zeros_like(dlb_ref)

        for ci in range(tm // C):
            rows = slice(ci * C, (ci + 1) * C)
            u = p_ref[rows, 0:GMW].astype(F32)
            v = p_ref[rows, GMW:2 * GMW].astype(F32)
            au, dau, dav, vhat, rstd, vn, mixed, head = _gm_chunk_fwd(u, v, lng_ref[...], lnb_ref[...], ws_ref, bias_ref[...])
            dyc = dy_ref[rows, :]
            dmixed = dyc * au
            dmb = dmixed.astype(BF16)
            dbias_ref[...] += dmixed
            dvn = jnp.zeros((C, GMW), F32)
            for h in range(HEADS):
                dws_ref[h] += _dot_nt(jnp.where(head == h, dmixed, 0.0).astype(BF16), vn)
                dvn = dvn + jnp.where(head == h, _dot(wst_ref[h], dmb), 0.0)
            dlg_ref[...] += _col_sum(dvn * vhat)
            dlb_ref[...] += _col_sum(dvn)
            dav_in = _standardize_bwd(dvn * lng_ref[...], vhat, rstd)
            duv_ref[rows, 0:GMW] = (dyc * mixed * dau).astype(BF16)
            duv_ref[rows, GMW:2 * GMW] = (dav_in * dav).astype(BF16)

        @pl.when(pl.program_id(0) == nt - 1)
        def _():
            head = lax.broadcasted_iota(jnp.int32, (C, GMW), 1) // (GMW // HEADS)
            lane = lax.broadcasted_iota(jnp.int32, (C, 128), 1)
            fold = jnp.zeros((C, 128), F32)
            for h in range(HEADS):
                col = jnp.sum(jnp.where(head == h, dbias_ref[...], 0.0), axis=1, keepdims=True)
                fold = jnp.where(lane == h, col, fold)
            dbs_ref[...] = fold

    vec = lambda: pl.BlockSpec((None, 1, GMW), lambda i: (l, 0, 0))
    mats = lambda: pl.BlockSpec((None, HEADS, C, C), lambda i: (l, 0, 0, 0))
    return _launch(
        body, (nt,),
        [pl.BlockSpec((tm, 2 * GMW), lambda i: (i, 0)), pl.BlockSpec((tm, GMW), lambda i: (i, 0)),
         vec(), vec(), mats(), mats(), pl.BlockSpec((None, C, GMW), lambda i: (l, 0, 0))],
        [pl.BlockSpec((tm, 2 * GMW), lambda i: (i, 0)),
         pl.BlockSpec((HEADS, C, C), lambda i: (0, 0, 0)),
         pl.BlockSpec((C, GMW), lambda i: (0, 0)), pl.BlockSpec((C, 128), lambda i: (0, 0)),
         pl.BlockSpec((1, GMW), lambda i: (0, 0)), pl.BlockSpec((1, GMW), lambda i: (0, 0))],
        [S((t, 2 * GMW), BF16), S((HEADS, C, C), F32), S((C, GMW), F32), S((C, 128), F32),
         S((1, GMW), F32), S((1, GMW), F32)],
        [], (proj, dy, lng, lnb, ws_bf, wst_bf, bias), name, ("arbitrary",), comm)


def _b_conv(proj, dc, cw, l, name, comm=None):
    t = proj.shape[0]
    tm = 256
    nt = t // tm
    rb = 64

    def body(ap, ac, an, gp, gc, gn, dp, dcur, dn, cw_ref, dag_ref, dcw_ref, hext, dext, hrot, drot):
        i = pl.program_id(0)

        @pl.when(i == 0)
        def _():
            dcw_ref[...] = jnp.zeros_like(dcw_ref)

        _fill_ext(hext, _glu(ap, gp), _glu(ac, gc), _glu(an, gn), i, nt, tm)
        _fill_ext(dext, dp[...], dcur[...], dn[...], i, nt, tm)
        _shifted_copies(hext, hrot, tm)
        _shifted_copies(dext, drot, tm)
        for j in range(KW):
            dcw_ref[j:j + 1, :] += _col_sum(dcur[...] * _tap(hext, hrot, 0, j + 1, tm))
        for r0 in range(0, tm, rb):
            dh = jnp.zeros((rb, CVW), F32)
            for j in range(KW):
                dh = dh + cw_ref[j:j + 1, :] * _tap(dext, drot, r0, 2 * HALO - 1 - j, rb)
            a = ac[r0:r0 + rb, :].astype(F32)
            s = _sigmoid(gc[r0:r0 + rb, :].astype(F32))
            dag_ref[r0:r0 + rb, 0:CVW] = (dh * s).astype(BF16)
            dag_ref[r0:r0 + rb, CVW:2 * CVW] = (dh * a * s * (1.0 - s)).astype(BF16)

    dspecs = _conv_halo_specs(t, tm, CVW, 0)
    return _launch(
        body, (nt,),
        _conv_halo_specs(t, tm, CVW, CONV_A_BLK) + _conv_halo_specs(t, tm, CVW, CONV_GATE_BLK) + dspecs +
        [pl.BlockSpec((None, 32, CVW), lambda i: (l, 0, 0))],
        [pl.BlockSpec((tm, 2 * CVW), lambda i: (i, 0)), pl.BlockSpec((32, CVW), lambda i: (0, 0))],
        [S((t, 2 * CVW), BF16), S((32, CVW), F32)],
        [pltpu.VMEM((tm + 2 * HALO, CVW), F32), pltpu.VMEM((tm + 2 * HALO, CVW), F32),
         pltpu.VMEM((7, tm + 2 * HALO - 8, CVW), F32), pltpu.VMEM((7, tm + 2 * HALO - 8, CVW), F32)],
        (proj, proj, proj, proj, proj, proj, dc, dc, dc, cw), name, ("arbitrary",), comm)


def _b_ret_out(proj, cos2, sin2, dO, s_f, s_b, g_f, g_b, rc, name, comm=None):
    t = proj.shape[0]
    nchunks = min(RET_CHUNKS, t // C)
    tm = nchunks * C

    def body(q_ref, k_ref, v_ref, cos_ref, sin_ref, dO_ref, sf_ref, sb_ref, gf_ref, gb_ref,
             d_ref, xif_ref, xib_ref, zef_ref, zeb_ref, dq_ref, dk_ref, dv_ref):
        for ci in range(nchunks):
            rows = slice(ci * C, (ci + 1) * C)
            cos_v, sin_v = cos_ref[rows, :], sin_ref[rows, :]
            for h in range(HEADS):
                sl = slice(h * DH, (h + 1) * DH)
                qh, kh, vh = q_ref[rows, sl], k_ref[rows, sl], v_ref[rows, sl]
                dOh = dO_ref[rows, sl].astype(BF16)
                dm = d_ref[h]
                p = (_dot_nt(qh, kh) * dm).astype(BF16)
                dp = (_dot_nt(dOh, vh) * dm).astype(BF16)
                from_s = _dot_nt(dOh, jnp.concatenate([sf_ref[ci, h], sb_ref[ci, h]], axis=0))
                from_g = _dot_nt(vh, jnp.concatenate([gf_ref[ci, h], gb_ref[ci, h]], axis=0))
                kg = _dot(kh, jnp.concatenate([gf_ref[ci, h], gb_ref[ci, h]], axis=1))
                dqr = _dot(dp, kh) + xif_ref[h] * from_s[:, 0:DH] + xib_ref[h] * from_s[:, DH:2 * DH]
                dkr = (_dot_tn(dp, qh) + zef_ref[h] * from_g[:, 0:DH] + zeb_ref[h] * from_g[:, DH:2 * DH]) * SCALE
                dv = _dot_tn(p, dOh) + zef_ref[h] * kg[:, 0:DH] + zeb_ref[h] * kg[:, DH:2 * DH]
                dq_ref[rows, sl] = _rot_t(dqr, cos_v, sin_v).astype(BF16)
                dk_ref[rows, sl] = _rot_t(dkr, cos_v, sin_v).astype(BF16)
                dv_ref[rows, sl] = dv.astype(BF16)

    const = lambda: pl.BlockSpec((HEADS, C, DH), lambda i: (0, 0, 0))
    state = lambda: pl.BlockSpec((nchunks, HEADS, DH, DH), lambda i: (i, 0, 0, 0))
    tok = lambda: pl.BlockSpec((tm, RETW), lambda i: (i, 0))
    return _launch(
        body, (t // tm,),
        [pl.BlockSpec((tm, RETW), lambda i, cb=cb: (i, cb)) for cb in (Q_BLK, K_BLK, V_BLK)] +
        [pl.BlockSpec((tm, DH), lambda i: (i, 0)), pl.BlockSpec((tm, DH), lambda i: (i, 0)), tok(),
         state(), state(), state(), state()] + [const() for _ in range(5)],
        [tok(), tok(), tok()],
        [S((t, RETW), BF16) for _ in range(3)], [],
        (proj, proj, proj, cos2, sin2, dO, s_f, s_b, g_f, g_b, rc["f"]["D"] + rc["b"]["D"],
         rc["f"]["XI"], rc["b"]["XI"], rc["f"]["ZETA"], rc["b"]["ZETA"]), name, ("parallel",), comm)


def _b_inproj(d_uv, dqkv, d_g, d_ag, w, x, g1, dxm, l, name, comm=None):
    t = x.shape[0]
    tm = 512

    def body(duv_ref, dq_ref, dk_ref, dv_ref, dg_ref, dag_ref, w_ref, x_ref, g_ref, dxm_ref,
             dp_ref, dx_ref, dn_ref):
        @pl.when(pl.program_id(0) == 0)
        def _():
            dn_ref[...] = jnp.zeros_like(dn_ref)

        for k, part in enumerate((duv_ref, dq_ref, dk_ref, dv_ref, dg_ref, dag_ref)):
            dp_ref[:, 512 * k:512 * (k + 1)] = part[...]
        dh = _dot_nt(dp_ref[...], w_ref[...])
        xv = x_ref[...]
        r = _rms(xv)
        dn_ref[...] += _col_sum(dh * xv * r)
        dx_ref[...] = dxm_ref[...] + _rmsnorm_bwd(dh, xv, r, g_ref[...])

    half = lambda: pl.BlockSpec((tm, 512), lambda i: (i, 0))
    full = lambda: pl.BlockSpec((tm, D), lambda i: (i, 0))
    return _launch(
        body, (t // tm,),
        [half() for _ in range(6)] +
        [_resident((D, INW), lambda i: (0, 0)), full(), pl.BlockSpec((None, 1, D), lambda i: (l, 0, 0)), full()],
        [pl.BlockSpec((tm, INW), lambda i: (i, 0)), full(), pl.BlockSpec((1, D), lambda i: (0, 0))],
        [S((t, INW), BF16), S((t, D), F32), S((1, D), F32)], [],
        (d_uv, *dqkv, d_g, d_ag, w, x, g1, dxm), name, ("arbitrary",), comm)


class _PairSwap:
    def __init__(self, units):
        self.inputs = list(units)
        self.out_shape = [S((4,) + u.shape[1:], BF16) for u in units]
        n = len(units)
        self.scratch = [pltpu.SemaphoreType.DMA((n, 4)), pltpu.SemaphoreType.DMA((n, 4))]

    def run(self, phase, ins, outs, scr):
        ssem, rsem = scr
        x, y, c, _ = _place()
        copies = lambda: [pltpu.make_async_remote_copy(src_ref=ins[u].at[2 * chip + (1 - c)], dst_ref=outs[u].at[chip],
                                                       send_sem=ssem.at[u, chip], recv_sem=rsem.at[u, chip],
                                                       device_id=(x, y, 1 - c), device_id_type=MESH)
                          for u in range(len(self.inputs)) for chip in range(4)]
        if phase == "start":
            for cp in copies():
                cp.start()
        elif phase == "finish":
            for cp in copies():
                cp.wait_recv()
            for cp in copies():
                cp.wait_send()


def _pair_add(g, q, name):
    _, mm, nn = g.shape
    bm = _row_block(mm)

    def body(c_ref, g_ref, q_ref, h_ref):
        h_ref[...] = (g_ref[...].astype(F32) + q_ref[...].astype(F32)).astype(BF16)

    blk = lambda: pl.BlockSpec((None, bm, nn), lambda qq, i, c_ref: (qq, i, 0))
    return pl.pallas_call(
        body, name=name,
        grid_spec=pltpu.PrefetchScalarGridSpec(
            num_scalar_prefetch=1, grid=(4, mm // bm),
            in_specs=[pl.BlockSpec((None, None, bm, nn), lambda qq, i, c_ref: (qq, c_ref[0], i, 0)), blk()],
            out_specs=blk()),
        out_shape=S((4, mm, nn), BF16),
        compiler_params=_params("parallel", "parallel"),
    )(lax.axis_index("c").astype(jnp.int32).reshape(1), g.reshape(4, 2, mm, nn), q)


_BIG = ("w_in", "w_out", "w_ffn_in", "w_ffn_out")
_KIND = dict(w_in="cols", w_out="rows", w_ffn_in="lead", w_ffn_out="rows")
CWP = 128
EARLY_ROWS, LATE_ROWS = 152, 8


def _step(x, tgt, wts, sh, cw_pad):
    t = x.shape[0]
    rc = _ret_consts()
    cos2, sin2 = (jnp.asarray(a) for a in _rope_tables(t))
    n1 = wts["norm1_g"].reshape(LAYERS, 1, D)
    n2 = wts["norm2_g"].reshape(LAYERS, 1, D)
    gm_lng = wts["gm_ln_g"].reshape(LAYERS, 1, GMW)
    gm_lnb = wts["gm_ln_b"].reshape(LAYERS, 1, GMW)
    ws_bf = wts["gm_ws"].astype(BF16)
    wst_bf = jnp.swapaxes(wts["gm_ws"], 2, 3).astype(BF16)
    bias = jnp.repeat(jnp.swapaxes(wts["gm_bs"], 1, 2), GMW // HEADS, axis=2)
    cb = wts["conv_b"].reshape(LAYERS, 1, CVW)
    cv_lng = wts["conv_ln_g"].reshape(LAYERS, 1, CVW)
    cv_lnb = wts["conv_ln_b"].reshape(LAYERS, 1, CVW)
    unit = lambda f, l: (sh[f][l], _KIND[f])
    cshard = CVW // NDEV

    full = {f: [None] * LAYERS for f in _BIG}
    full["w_in"][0], cw_all = _comm_only(_Gather([unit("w_in", 0), (cw_pad, "lead")]), "gather_first")
    cw = jnp.transpose(cw_all[:, :, :, :cshard], (1, 2, 0, 3)).reshape(LAYERS, 32, CVW)

    gcf, gcb = rc["f"]["gC"], rc["b"]["gC"]
    saved = []
    for l in range(LAYERS):
        first = l == 0
        (proj, h1), got = _f_inproj(x, n1, full["w_in"][l], cos2, sin2, l, f"f_inproj_{l}",
                                    _Gather([unit("w_ffn_in", 0)]) if first else None)
        if first:
            full["w_ffn_in"][0], = got
        y_gm = _f_gm(proj, gm_lng, gm_lnb, ws_bf, bias, l, f"f_gm_{l}")
        s_f, s_b = _scan_pair(proj, None, 2, rc["f"]["ZETA"], rc["b"]["ZETA"], gcf, gcb, True, f"f_ret_state_{l}")
        o = _f_ret_out(proj, s_f, s_b, rc, f"f_ret_out_{l}")
        (c, y_cv), got = _f_conv(proj, cw, cb, cv_lng, cv_lnb, l, f"f_conv_{l}",
                                 _Gather([unit("w_out", 0)]) if first else None)
        if first:
            full["w_out"][0], = got
        (xm, ycat), got = _f_mixout(x, y_gm, y_cv, o, proj, full["w_out"][l], f"f_mixout_{l}",
                                    _Gather([unit("w_ffn_out", 0)]) if first else None)
        if first:
            full["w_ffn_out"][0], = got
        w1t = full["w_ffn_in"][l].reshape(2 * FFH, D)
        (xo, h2, gu, act), got = _f_ffn(xm, n2, w1t, full["w_ffn_out"][l], l, f"f_ffn_{l}",
                                        _Gather([unit(f, 1) for f in _BIG]) if first else None)
        if first:
            full["w_in"][1], full["w_out"][1], full["w_ffn_in"][1], full["w_ffn_out"][1] = got
        saved.append(dict(x=x, proj=proj, h1=h1, o=o, s_f=s_f, s_b=s_b, c=c, xm=xm, ycat=ycat, h2=h2, gu=gu, act=act))
        x = xo

    parts = {f: [None] * LAYERS for f in _BIG}
    small = [None] * LAYERS
    norm1 = [None] * LAYERS
    upper = None
    top = (x, wts["final_g"].reshape(1, D), tgt)
    for l in reversed(range(LAYERS)):
        sv = saved[l]
        outs, got = _b_ffn(top, sv["xm"], n2, sv["gu"], full["w_ffn_in"][l].reshape(2 * FFH, D), full["w_ffn_out"][l], l,
                           f"b_ffn_{l}", _Scatter(upper) if upper else None)
        dgu_t, dxm, dxo_bf, d_n2 = outs[:4]
        if l == LAYERS - 1:
            loss, d_final = outs[4:]
        if upper:
            for f, p in zip(_BIG, got):
                parts[f][l + 1] = p
        g_f2 = _mm_wgrad(sv["act"], dxo_bf, NDEV // 2, True, "shared", 1, f"g_ffn_out_{l}").reshape(NDEV, FFH // NDEV, D)
        g_f1 = _mm_wgrad(dgu_t, sv["h2"], NDEV, True, "shared", 1, f"g_ffn_in_{l}")
        (dxm_bf, dy_gm, dO, d_g, dc, d_cvlg, d_cvlb, d_cb), (q_f1, q_f2) = _b_mixout(
            dxm, full["w_out"][l], sv["o"], sv["proj"], sv["c"], cv_lng, cv_lnb, l, f"b_mixout_{l}", _PairSwap([g_f1, g_f2]))
        h_f1 = _pair_add(g_f1, q_f1, f"pair_add_w_ffn_in_{l}")
        h_f2 = _pair_add(g_f2, q_f2, f"pair_add_w_ffn_out_{l}")
        g_out = _mm_wgrad(sv["ycat"], dxm_bf, 1, False, "shared", 1, f"g_out_{l}").reshape(NDEV, D // NDEV, D)
        last = l == 0
        (d_uv, d_ws, _, d_bs_fold, d_gmlg, d_gmlb), (q_out,) = _b_gm(
            sv["proj"], dy_gm, gm_lng, gm_lnb, ws_bf, wst_bf, bias, l, f"b_gm_{l}", _PairSwap([g_out]))
        h_out = _pair_add(g_out, q_out, f"pair_add_w_out_{l}")
        (d_ag, d_cw), got = _b_conv(sv["proj"], dc, cw, l, f"b_conv_{l}", _Scatter([h_f1]) if last else None)
        if last:
            parts["w_ffn_in"][l], = got
        small[l] = dict(gm_ln_g=d_gmlg[0], gm_ln_b=d_gmlb[0], gm_ws=d_ws, gm_bs=d_bs_fold[:, :HEADS].T, conv_w=d_cw[:KW],
                        conv_b=d_cb[0], conv_ln_g=d_cvlg[0], conv_ln_b=d_cvlb[0], norm2_g=d_n2[0])
        comm = None
        if last:
            early_g = {k: jnp.stack([small[ll][k] for ll in range(LAYERS)]) for k in small[0]}
            early_g["final_g"] = d_final[0]
            early_buf = _pack([early_g[k] for k in _SMALL_EARLY] + [loss], EARLY_ROWS)
            comm = _Comms([_Scatter([h_out, h_f2]), _Gather([(early_buf, "lead")])])
        g_f, g_b = _scan_pair(sv["proj"], dO, 1, rc["f"]["XI"], rc["b"]["XI"], gcf, gcb, False, f"b_ret_state_{l}")
        dqkv, got = _b_ret_out(sv["proj"], cos2, sin2, dO, sv["s_f"], sv["s_b"], g_f, g_b, rc, f"b_ret_out_{l}", comm)
        if last:
            parts["w_out"][l], parts["w_ffn_out"][l], early_parts = got
        (dproj, top, d_n1), _ = _b_inproj(d_uv, dqkv, d_g, d_ag, full["w_in"][l], sv["x"], n1, dxm, l, f"b_inproj_{l}")
        norm1[l] = d_n1[0]
        g_in = _mm_wgrad(sv["h1"], dproj, NDEV, False, "cols", 4, f"g_in_{l}")
        q_in, = _comm_only(_PairSwap([g_in]), f"pair_swap_w_in_{l}")
        h_in = _pair_add(g_in, q_in, f"pair_add_w_in_{l}")
        if last:
            tail = [h_in]
        else:
            upper = [h_in, h_out, h_f1, h_f2]
    late_buf = _pack([jnp.stack(norm1)], LATE_ROWS)
    parts["w_in"][0], late_parts = _comm_only(_Comms([_Scatter(tail), _Gather([(late_buf, "lead")])]), "exchange_last")
    return loss, top, parts, (early_parts, late_parts)


def _adamw(w, g, m, v):
    m = ADAM_B1 * m + (1.0 - ADAM_B1) * g
    v = ADAM_B2 * v + (1.0 - ADAM_B2) * (g * g)
    m_hat = m / (1.0 - ADAM_B1 ** ADAM_STEP)
    v_hat = v / (1.0 - ADAM_B2 ** ADAM_STEP)
    return -ADAM_LR * (m_hat / (jnp.sqrt(v_hat) + ADAM_EPS) + ADAM_WD * w), m, v


def _cast_blocks(ws):
    def body(*refs):
        ins, outs = refs[:len(ws)], refs[len(ws):]
        for k, src in enumerate(ins):
            for l in range(LAYERS):
                outs[k * LAYERS + l][...] = src[l].astype(BF16)

    outs = pl.pallas_call(body, name="cast_blocks", out_shape=[S(w.shape[1:], BF16) for w in ws for _ in range(LAYERS)],
                          compiler_params=_params())(*ws)
    return [list(outs[k * LAYERS:(k + 1) * LAYERS]) for k in range(len(ws))]


def _row_block(mm):
    return next(b for b in (512, 352, 128) if mm % b == 0)


def _sum_adam(parts, w, m, v, l, prev, name):
    _, mm, nn = parts.shape
    bm = _row_block(mm)

    def body(p_ref, w_ref, m_ref, v_ref, *rest):
        g_ref, d_ref, nm_ref, nv_ref = rest[-4:]
        g = p_ref[0].astype(F32)
        for s in range(1, 4):
            g = g + p_ref[s].astype(F32)
        g_ref[...] = g
        d_ref[...], nm_ref[...], nv_ref[...] = _adamw(w_ref[...], g, m_ref[...], v_ref[...])

    blk = lambda: pl.BlockSpec((None, bm, nn), lambda i: (l, i, 0))
    prev = list(prev) if prev else []
    return pl.pallas_call(
        body, grid=(mm // bm,), name=name,
        in_specs=[pl.BlockSpec((4, bm, nn), lambda i: (0, i, 0)), blk(), blk(), blk()] + [_ANY] * len(prev),
        out_specs=[blk() for _ in range(4)],
        out_shape=[S(w.shape, F32) for _ in range(4)],
        input_output_aliases={4 + j: j for j in range(len(prev))},
        compiler_params=_params("parallel"),
    )(parts, w, m, v, *prev)


def _sum_small(parts):
    n = len(parts)

    def body(*refs):
        for p_ref, o_ref in zip(refs[:n], refs[n:]):
            g = p_ref[0]
            for s in range(1, NDEV):
                g = g + p_ref[s]
            o_ref[...] = g

    return pl.pallas_call(body, name="sum_small", out_shape=[S(p.shape[1:], F32) for p in parts],
                          compiler_params=_params())(*parts)


def _adam_small(quads):
    n = len(quads)
    as2d = lambda a: a.reshape(1, -1) if a.ndim == 1 else a

    def body(*refs):
        ins, outs = refs[:4 * n], refs[4 * n:]
        for k in range(n):
            g_ref, w_ref, m_ref, v_ref = ins[4 * k:4 * k + 4]
            outs[3 * k][...], outs[3 * k + 1][...], outs[3 * k + 2][...] = _adamw(w_ref[...], g_ref[...], m_ref[...], v_ref[...])

    flat = [as2d(a) for q in quads for a in q]
    outs = pl.pallas_call(body, name="adam_small", out_shape=[S(as2d(q[1]).shape, F32) for q in quads for _ in range(3)],
                          compiler_params=_params())(*flat)
    return [tuple(o.reshape(q[1].shape) for o in outs[3 * k:3 * k + 3]) for k, q in enumerate(quads)]


_SMALL = ("norm1_g", "gm_ln_g", "gm_ln_b", "gm_ws", "gm_bs", "conv_w", "conv_b", "conv_ln_g", "conv_ln_b",
          "norm2_g", "final_g")
_SMALL_EARLY = _SMALL[1:]
_NAMES = ("norm1_g", "w_in", "gm_ln_g", "gm_ln_b", "gm_ws", "gm_bs", "conv_w", "conv_b", "conv_ln_g", "conv_ln_b",
          "w_out", "norm2_g", "w_ffn_in", "w_ffn_out", "final_g")


def _pack(parts, rows):
    flat = jnp.concatenate([p.reshape(-1) for p in parts])
    return jnp.pad(flat, (0, rows * 1024 - flat.shape[0])).reshape(rows, 1024)


def _unpack(buf, shapes):
    flat = buf.reshape(-1)
    out, o = [], 0
    for shp in shapes:
        sz = int(np.prod(shp))
        out.append(flat[o:o + sz].reshape(shp))
        o += sz
    return out


def kernel(x, norm1_g, w_in, gm_ln_g, gm_ln_b, gm_ws, gm_bs, conv_w, conv_b, conv_ln_g, conv_ln_b, w_out, norm2_g, w_ffn_in, w_ffn_out, final_g, loss_target, m_norm1_g, m_w_in, m_gm_ln_g, m_gm_ln_b, m_gm_ws, m_gm_bs, m_conv_w, m_conv_b, m_conv_ln_g, m_conv_ln_b, m_w_out, m_norm2_g, m_w_ffn_in, m_w_ffn_out, m_final_g, v_norm1_g, v_w_in, v_gm_ln_g, v_gm_ln_b, v_gm_ws, v_gm_bs, v_conv_w, v_conv_b, v_conv_ln_g, v_conv_ln_b, v_w_out, v_norm2_g, v_w_ffn_in, v_w_ffn_out, v_final_g):
    w = dict(norm1_g=norm1_g, w_in=w_in, gm_ln_g=gm_ln_g, gm_ln_b=gm_ln_b, gm_ws=gm_ws, gm_bs=gm_bs, conv_w=conv_w,
             conv_b=conv_b, conv_ln_g=conv_ln_g, conv_ln_b=conv_ln_b, w_out=w_out, norm2_g=norm2_g, w_ffn_in=w_ffn_in,
             w_ffn_out=w_ffn_out, final_g=final_g)
    mo = dict(norm1_g=m_norm1_g, w_in=m_w_in, gm_ln_g=m_gm_ln_g, gm_ln_b=m_gm_ln_b, gm_ws=m_gm_ws, gm_bs=m_gm_bs,
              conv_w=m_conv_w, conv_b=m_conv_b, conv_ln_g=m_conv_ln_g, conv_ln_b=m_conv_ln_b, w_out=m_w_out,
              norm2_g=m_norm2_g, w_ffn_in=m_w_ffn_in, w_ffn_out=m_w_ffn_out, final_g=m_final_g)
    vo = dict(norm1_g=v_norm1_g, w_in=v_w_in, gm_ln_g=v_gm_ln_g, gm_ln_b=v_gm_ln_b, gm_ws=v_gm_ws, gm_bs=v_gm_bs,
              conv_w=v_conv_w, conv_b=v_conv_b, conv_ln_g=v_conv_ln_g, conv_ln_b=v_conv_ln_b, w_out=v_w_out,
              norm2_g=v_norm2_g, w_ffn_in=v_w_ffn_in, w_ffn_out=v_w_ffn_out, final_g=v_final_g)
    t = x.shape[1]
    me = 4 * lax.axis_index("x") + 2 * lax.axis_index("y") + lax.axis_index("c")
    cshard = conv_w.shape[2]

    cw_pad = jnp.pad(conv_w, ((0, 0), (0, 32 - KW), (0, CWP - cshard)))
    flip = lambda a: jnp.swapaxes(a, 1, 2)
    big = {f: tuple(flip(a[f]) if f == "w_ffn_in" else a[f] for a in (w, mo, vo)) for f in _BIG}
    sh = dict(zip(_BIG, _cast_blocks([big[f][0] for f in _BIG])))
    loss, dx, parts, small_parts = _step(x.reshape(t, D), loss_target.reshape(t, D), w, sh, cw_pad)

    grads, delta, new_m, new_v = {}, {}, {}, {}
    for f in _BIG:
        outs = None
        for l in reversed(range(LAYERS)):
            outs = _sum_adam(parts[f][l], *big[f], l, outs, f"sum_adam_{f}_{l}")
        grads[f], delta[f], new_m[f], new_v[f] = [flip(a) for a in outs] if f == "w_ffn_in" else outs

    early_sum, late_sum = _sum_small(small_parts)
    early_shapes = [(LAYERS, KW, CVW) if k == "conv_w" else w[k].shape for k in _SMALL_EARLY]
    grads["norm1_g"], = _unpack(late_sum, [w["norm1_g"].shape])
    *early, total = _unpack(early_sum, early_shapes + [()])
    for k, g in zip(_SMALL_EARLY, early):
        grads[k] = lax.dynamic_slice_in_dim(g, me * cshard, cshard, axis=2) if k == "conv_w" else g
    for k, (d, nm, nv) in zip(_SMALL, _adam_small([(grads[k], w[k], mo[k], vo[k]) for k in _SMALL])):
        delta[k], new_m[k], new_v[k] = d, nm, nv

    return (total, dx.reshape(1, t, D), *[grads[k] for k in _NAMES], *[delta[k] for k in _NAMES],
            *[new_m[k] for k in _NAMES], *[new_v[k] for k in _NAMES])
```

```python
import functools

import numpy as np
import jax
import jax.numpy as jnp
from jax import lax
from jax.experimental import pallas as pl
from jax.experimental.pallas import tpu as pltpu

F32, BF16 = jnp.float32, jnp.bfloat16
S = jax.ShapeDtypeStruct

D = 1024
INW = 3072
Q_BLK, K_BLK, V_BLK, G_BLK = 1, 2, 3, 4
CONV_A_BLK, CONV_GATE_BLK = 10, 11
GMW = 256
RETW = 512
CVW = 256
HEADS = 4
DH = 128
C = 128
KW = 31
HALO = 16
FFH = 2816
NDEV = 8
FFB = 2 * FFH // NDEV
FF_CHUNKS = ((0, 768), (768, 1536), (1536, 2304), (2304, FFH))
EPS = 1e-6
LAYERS = 2
SCALE = DH ** -0.5
VMEM_LIMIT = 56 * 1024 * 1024

ADAM_LR, ADAM_B1, ADAM_B2, ADAM_EPS, ADAM_WD, ADAM_STEP = 0.001, 0.9, 0.999, 1e-08, 0.01, 10

_SQRT_HALF = 0.7071067811865476
_INV_SQRT_2PI = 0.3989422804014327


def _params(*sem):
    return pltpu.CompilerParams(dimension_semantics=sem or None, vmem_limit_bytes=VMEM_LIMIT)


def _resident(shape, index_map):
    return pl.BlockSpec(shape, index_map, pipeline_mode=pl.Buffered(1))


def _dot(a, b):
    return jnp.dot(a, b, preferred_element_type=F32)


def _dot_nt(a, b):
    return lax.dot_general(a, b, (((1,), (1,)), ((), ())), preferred_element_type=F32)


def _dot_tn(a, b):
    return lax.dot_general(a, b, (((0,), (0,)), ((), ())), preferred_element_type=F32)


def _sigmoid(x):
    return 1.0 / (1.0 + jnp.exp(-x))


def _gelu_and_grad(x):
    cdf = 0.5 * (1.0 + lax.erf(x * _SQRT_HALF))
    return x * cdf, cdf + x * jnp.exp(-0.5 * x * x) * _INV_SQRT_2PI


def _silu_and_grad(x):
    s = _sigmoid(x)
    return x * s, s * (1.0 + x * (1.0 - s))


def _standardize(x):
    mu = jnp.mean(x, axis=-1, keepdims=True)
    d = x - mu
    rstd = lax.rsqrt(jnp.mean(d * d, axis=-1, keepdims=True) + EPS)
    return d * rstd, rstd


def _standardize_bwd(dxhat, xhat, rstd):
    m1 = jnp.mean(dxhat, axis=-1, keepdims=True)
    m2 = jnp.mean(dxhat * xhat, axis=-1, keepdims=True)
    return rstd * (dxhat - m1 - xhat * m2)


def _rms(x):
    return lax.rsqrt(jnp.mean(x * x, axis=-1, keepdims=True) + EPS)


def _rmsnorm_bwd(dy, x, r, g):
    u = dy * g
    return r * u - x * (r * r * r) * jnp.mean(u * x, axis=-1, keepdims=True)


def _col_sum(a):
    return jnp.sum(a, axis=0, keepdims=True)


def _rot(t, cos2, sin2):
    return t * cos2 + pltpu.roll(t, DH // 2, axis=1) * sin2


def _rot_t(dt, cos2, sin2):
    return dt * cos2 + pltpu.roll(dt * sin2, DH // 2, axis=1)


MESH = pl.DeviceIdType.MESH
_HBM = pl.BlockSpec(memory_space=pltpu.HBM)
_ANY = pl.BlockSpec(memory_space=pl.ANY)


def _place():
    x, y, c = lax.axis_index("x"), lax.axis_index("y"), lax.axis_index("c")
    return x, y, c, ((1 - x, y), (x, 1 - y), (1 - x, 1 - y))


def _slot(full, kind, width, i):
    if kind == "cols":
        return full.at[:, pl.ds(pl.multiple_of(i * width, 128), width)]
    if kind == "rows":
        return full.at[pl.ds(pl.multiple_of(i * width, 16), width), :]
    return full.at[i]


class _Gather:
    def __init__(self, units):
        self.units = units
        self.inputs = [u[0] for u in units]
        self.out_shape = []
        for src, kind in units:
            r, c = src.shape[-2:]
            shape = {"cols": (r, NDEV * c), "rows": (NDEV * r, c), "lead": (NDEV,) + src.shape}[kind]
            self.out_shape.append(S(shape, src.dtype))
        n = len(units)
        self.scratch = [pltpu.SemaphoreType.DMA((n, 7)), pltpu.SemaphoreType.DMA((n, 7)), pltpu.SemaphoreType.DMA((n,))]

    def run(self, phase, ins, outs, scr):
        ssem, rsem, lsem = scr
        x, y, c, chips = _place()
        me, sib = 4 * x + 2 * y + c, (x, y, 1 - c)
        idx = lambda chip, core: 4 * chip[0] + 2 * chip[1] + core
        for u, (src_arr, kind) in enumerate(self.units):
            src, full = ins[u], outs[u]
            width = src_arr.shape[-1] if kind == "cols" else src_arr.shape[-2]
            slot = functools.partial(_slot, full, kind, width)

            def copy(k, block, to, from_src=False):
                return pltpu.make_async_remote_copy(src_ref=src if from_src else slot(block), dst_ref=slot(block),
                                                    send_sem=ssem.at[u, k], recv_sem=rsem.at[u, k],
                                                    device_id=to, device_id_type=MESH)

            mine = lambda: pltpu.make_async_copy(src, slot(me), lsem.at[u])
            first = lambda: [copy(0, me, sib, True)] + [copy(1 + j, me, (*chip, c), True) for j, chip in enumerate(chips)]
            passed = lambda j: copy(4 + j, idx(chips[j], c), sib)
            if phase == "start":
                mine().start()
                for cp in first():
                    cp.start()
            elif phase == "forward":
                for j, chip in enumerate(chips):
                    copy(1 + j, idx(chip, c), sib).wait_recv()
                    passed(j).start()
            else:
                copy(0, idx((x, y), 1 - c), sib).wait_recv()
                for j, chip in enumerate(chips):
                    copy(4 + j, idx(chip, 1 - c), sib).wait_recv()
                for cp in first() + [passed(j) for j in range(3)]:
                    cp.wait_send()
                mine().wait()


class _Scatter:
    def __init__(self, units):
        self.units = units
        self.inputs = list(units)
        self.out_shape = [S(u.shape, u.dtype) for u in units]
        n = len(units)
        self.scratch = [pltpu.SemaphoreType.DMA((n, 3)), pltpu.SemaphoreType.DMA((n, 3)), pltpu.SemaphoreType.DMA((n,))]

    def run(self, phase, ins, outs, scr):
        ssem, rsem, lsem = scr
        x, y, c, chips = _place()
        myq = 2 * x + y
        for u in range(len(self.units)):
            h, p = ins[u], outs[u]

            def copy(k, chip, send_to_them):
                q = 2 * chip[0] + chip[1]
                return pltpu.make_async_remote_copy(src_ref=h.at[q], dst_ref=p.at[myq if send_to_them else q],
                                                    send_sem=ssem.at[u, k], recv_sem=rsem.at[u, k],
                                                    device_id=(*chip, c), device_id_type=MESH)

            mine = lambda: pltpu.make_async_copy(h.at[myq], p.at[myq], lsem.at[u])
            sends = lambda: [copy(k, chip, True) for k, chip in enumerate(chips)]
            if phase == "start":
                mine().start()
                for cp in sends():
                    cp.start()
            elif phase == "finish":
                for k, chip in enumerate(chips):
                    copy(k, chip, False).wait_recv()
                for cp in sends():
                    cp.wait_send()
                mine().wait()


class _Comms:
    def __init__(self, parts):
        self.parts = parts
        self.inputs = [a for p in parts for a in p.inputs]
        self.out_shape = [a for p in parts for a in p.out_shape]
        self.scratch = [a for p in parts for a in p.scratch]

    def run(self, phase, ins, outs, scr):
        i = o = s = 0
        for p in self.parts:
            ni, no, ns = len(p.inputs), len(p.out_shape), len(p.scratch)
            p.run(phase, ins[i:i + ni], outs[o:o + no], scr[s:s + ns])
            i, o, s = i + ni, o + no, s + ns


def _launch(body, grid, in_specs, out_specs, out_shape, scratch, args, name, sem, comm=None):
    if comm is None:
        outs = pl.pallas_call(body, grid=grid, name=name, in_specs=in_specs, out_specs=out_specs, out_shape=out_shape,
                              scratch_shapes=scratch, compiler_params=_params(*sem))(*args)
        return list(outs), []
    n_in, n_out, n_scr = len(args), len(out_shape), len(scratch)
    ci, co = len(comm.inputs), len(comm.out_shape)
    nsteps = int(np.prod(grid))
    fwd_step = (7 * nsteps) // 8

    def hosted(*refs):
        a = refs[:n_in]
        ca = refs[n_in:n_in + ci]
        o = refs[n_in + ci:n_in + ci + n_out]
        cout = refs[n_in + ci + n_out:n_in + ci + n_out + co]
        s = refs[n_in + ci + n_out + co:n_in + ci + n_out + co + n_scr]
        cs = refs[n_in + ci + n_out + co + n_scr:]
        step = pl.program_id(0)
        for d in range(1, len(grid)):
            step = step * grid[d] + pl.program_id(d)

        @pl.when(step == 0)
        def _():
            comm.run("start", ca, cout, cs)

        body(*a, *o, *s)

        @pl.when(step == fwd_step)
        def _():
            comm.run("forward", ca, cout, cs)

        @pl.when(step == nsteps - 1)
        def _():
            comm.run("finish", ca, cout, cs)

    outs = pl.pallas_call(
        hosted, grid=grid, name=name, in_specs=list(in_specs) + [_HBM] * ci, out_specs=list(out_specs) + [_HBM] * co,
        out_shape=list(out_shape) + comm.out_shape, scratch_shapes=list(scratch) + comm.scratch,
        compiler_params=_params(*["arbitrary"] * len(grid)))(*args, *comm.inputs)
    return list(outs[:n_out]), list(outs[n_out:])


def _comm_only(comm, name):
    ci, co = len(comm.inputs), len(comm.out_shape)

    def body(*refs):
        ca, cout, cs = refs[:ci], refs[ci:ci + co], refs[ci + co:]
        for phase in ("start", "forward", "finish"):
            comm.run(phase, ca, cout, cs)

    return pl.pallas_call(body, name=name, in_specs=[_HBM] * ci, out_specs=[_HBM] * co, out_shape=comm.out_shape,
                          scratch_shapes=comm.scratch, compiler_params=_params())(*comm.inputs)


def _ret_consts():
    idx = np.arange(C, dtype=np.float32)
    gf = (1.0 - np.exp2(-5.0 - np.arange(HEADS, dtype=np.float32))).astype(np.float32)
    out = {}
    for name, gamma, fwd in (("f", gf, True), ("b", gf[::-1].copy(), False)):
        lg = np.log(gamma).astype(np.float32)[:, None]
        diff = idx[:, None] - idx[None, :]
        if fwd:
            mask = diff >= 0
            dist = np.where(mask, diff, 0.0)
            zeta = np.exp(lg * (C - 1 - idx))
            xi = np.exp(lg * (idx + 1))
        else:
            mask = diff < 0
            dist = np.where(mask, -diff, 0.0)
            zeta = np.exp(lg * idx)
            xi = np.exp(lg * (C - idx))
        dm = np.where(mask[None], np.exp(lg[:, :, None] * dist[None]), 0.0).astype(np.float32)
        bc = lambda vec: np.ascontiguousarray(np.broadcast_to(vec.astype(np.float32)[:, :, None], (HEADS, C, DH)))
        out[name] = dict(D=dm, XI=bc(xi), ZETA=bc(zeta), gC=[float(v) for v in np.exp(lg[:, 0] * C).astype(np.float32)])
    return out


def _rope_tables(t):
    half = DH // 2
    inv_freq = (np.float32(10000.0) ** (-np.arange(half, dtype=np.float32) / np.float32(half))).astype(np.float32)
    ang = (np.arange(t, dtype=np.float32)[:, None] * inv_freq[None, :]).astype(np.float64)
    cos, sin = np.cos(ang).astype(np.float32), np.sin(ang).astype(np.float32)
    return np.concatenate([cos, cos], axis=1), np.concatenate([-sin, sin], axis=1)


def _f_inproj(x, g1, w, cos2, sin2, l, name, comm=None):
    t = x.shape[0]
    tm = 512 if comm else min(1024, t)

    def body(x_ref, g_ref, w_ref, cos_ref, sin_ref, proj_ref, ht_ref):
        xv = x_ref[...]
        h = (xv * _rms(xv) * g_ref[...]).astype(BF16)
        ht_ref[...] = h.T
        for nb in range(INW // 512):
            cs = slice(nb * 512, (nb + 1) * 512)
            res = _dot(h, w_ref[:, cs])
            if nb in (Q_BLK, K_BLK):
                for hh in range(HEADS):
                    r = _rot(res[:, hh * DH:(hh + 1) * DH], cos_ref[...], sin_ref[...])
                    proj_ref[:, nb * 512 + hh * DH:nb * 512 + (hh + 1) * DH] = (r * SCALE if nb == K_BLK else r).astype(BF16)
            else:
                proj_ref[:, cs] = res.astype(BF16)

    return _launch(
        body, (t // tm,),
        [pl.BlockSpec((tm, D), lambda i: (i, 0)),
         pl.BlockSpec((None, 1, D), lambda i: (l, 0, 0)),
         _resident((D, INW), lambda i: (0, 0)),
         pl.BlockSpec((tm, DH), lambda i: (i, 0)), pl.BlockSpec((tm, DH), lambda i: (i, 0))],
        [pl.BlockSpec((tm, INW), lambda i: (i, 0)), pl.BlockSpec((D, tm), lambda i: (0, i))],
        [S((t, INW), BF16), S((D, t), BF16)], [], (x, g1, w, cos2, sin2), name, ("parallel",), comm)


def _gm_chunk_fwd(u, v, lng, lnb, ws_ref, bias):
    au, dau = _gelu_and_grad(u)
    av, dav = _gelu_and_grad(v)
    vhat, rstd = _standardize(av)
    vn = (vhat * lng + lnb).astype(BF16)
    head = lax.broadcasted_iota(jnp.int32, (C, GMW), 1) // (GMW // HEADS)
    mixed = bias
    for h in range(HEADS):
        mixed = mixed + jnp.where(head == h, _dot(ws_ref[h], vn), 0.0)
    return au, dau, dav, vhat, rstd, vn, mixed, head


def _f_gm(proj, lng, lnb, ws_bf, bias, l, name):
    t = proj.shape[0]
    tm = min(1024, t)

    def body(p_ref, lng_ref, lnb_ref, ws_ref, bias_ref, y_ref):
        for ci in range(tm // C):
            rows = slice(ci * C, (ci + 1) * C)
            u = p_ref[rows, 0:GMW].astype(F32)
            v = p_ref[rows, GMW:2 * GMW].astype(F32)
            au, _, _, _, _, _, mixed, _ = _gm_chunk_fwd(u, v, lng_ref[...], lnb_ref[...], ws_ref, bias_ref[...])
            y_ref[rows, :] = (au * mixed).astype(BF16)

    return pl.pallas_call(
        body, grid=(t // tm,), name=name,
        in_specs=[pl.BlockSpec((tm, 2 * GMW), lambda i: (i, 0)),
                  pl.BlockSpec((None, 1, GMW), lambda i: (l, 0, 0)),
                  pl.BlockSpec((None, 1, GMW), lambda i: (l, 0, 0)),
                  pl.BlockSpec((None, HEADS, C, C), lambda i: (l, 0, 0, 0)),
                  pl.BlockSpec((None, C, GMW), lambda i: (l, 0, 0))],
        out_specs=pl.BlockSpec((tm, GMW), lambda i: (i, 0)),
        out_shape=S((t, GMW), BF16),
        compiler_params=_params("parallel"),
    )(proj, lng, lnb, ws_bf, bias)


def _scan_pair(proj, other, col, wf, wb, gcf, gcb, first_is_f, name):
    t = proj.shape[0]
    n = t // C
    sc = min(SCAN_CHUNKS, n)
    nsteps = n // sc
    other_is_proj = other is None

    def body(a1, o1, a2, o2, w1_ref, w2_ref, out1, out2, st1, st2):
        @pl.when(pl.program_id(0) == 0)
        def _():
            st1[...] = jnp.zeros_like(st1)
            st2[...] = jnp.zeros_like(st2)

        def one(a_ref, o_ref, w_ref, gc, st, out, order):
            for h in range(HEADS):
                sl = slice(h * DH, (h + 1) * DH)
                incs = {}
                for j in order:
                    rows = slice(j * C, (j + 1) * C)
                    aw = (a_ref[rows, sl].astype(F32) * w_ref[h]).astype(BF16)
                    incs[j] = _dot_tn(aw, o_ref[rows, sl].astype(BF16))
                cur = st[h]
                for j in order:
                    out[j, h] = cur.astype(BF16)
                    cur = gc[h] * cur + incs[j]
                st[h] = cur

        g1, g2 = (gcf, gcb) if first_is_f else (gcb, gcf)
        one(a1, o1, w1_ref, g1, st1, out1, range(sc))
        one(a2, o2, w2_ref, g2, st2, out2, range(sc - 1, -1, -1))

    up = lambda i: i
    down = lambda i: nsteps - 1 - i

    def specs(ix):
        o_spec = pl.BlockSpec((sc * C, RETW), lambda i: (ix(i), V_BLK if other_is_proj else 0))
        return [pl.BlockSpec((sc * C, RETW), lambda i: (ix(i), col)), o_spec]

    const = lambda: pl.BlockSpec((HEADS, C, DH), lambda i: (0, 0, 0))
    oth = proj if other_is_proj else other
    w1, w2 = (wf, wb) if first_is_f else (wb, wf)
    out1, out2 = pl.pallas_call(
        body, grid=(nsteps,), name=name,
        in_specs=specs(up) + specs(down) + [const(), const()],
        out_specs=[pl.BlockSpec((sc, HEADS, DH, DH), lambda i: (up(i), 0, 0, 0)),
                   pl.BlockSpec((sc, HEADS, DH, DH), lambda i: (down(i), 0, 0, 0))],
        out_shape=[S((n, HEADS, DH, DH), BF16), S((n, HEADS, DH, DH), BF16)],
        scratch_shapes=[pltpu.VMEM((HEADS, DH, DH), F32), pltpu.VMEM((HEADS, DH, DH), F32)],
        compiler_params=_params("arbitrary"),
    )(proj, oth, proj, oth, w1, w2)
    return (out1, out2) if first_is_f else (out2, out1)


SCAN_CHUNKS = 16


RET_CHUNKS = 8
RET_FWD_CHUNKS = 16


def _f_ret_out(proj, s_f, s_b, rc, name):
    t = proj.shape[0]
    nchunks = min(RET_FWD_CHUNKS, t // C)
    tm = nchunks * C

    def body(q_ref, k_ref, v_ref, sf_ref, sb_ref, d_ref, xif_ref, xib_ref, o_ref):
        for ci in range(nchunks):
            rows = slice(ci * C, (ci + 1) * C)
            for h in range(HEADS):
                sl = slice(h * DH, (h + 1) * DH)
                qh, kh, vh = q_ref[rows, sl], k_ref[rows, sl], v_ref[rows, sl]
                p = (_dot_nt(qh, kh) * d_ref[h]).astype(BF16)
                cross = _dot(qh, jnp.concatenate([sf_ref[ci, h], sb_ref[ci, h]], axis=1))
                o_ref[rows, sl] = _dot(p, vh) + xif_ref[h] * cross[:, 0:DH] + xib_ref[h] * cross[:, DH:2 * DH]

    const = lambda: pl.BlockSpec((HEADS, C, DH), lambda i: (0, 0, 0))
    state = lambda: pl.BlockSpec((nchunks, HEADS, DH, DH), lambda i: (i, 0, 0, 0))
    return pl.pallas_call(
        body, grid=(t // tm,), name=name,
        in_specs=[pl.BlockSpec((tm, RETW), lambda i, cb=cb: (i, cb)) for cb in (Q_BLK, K_BLK, V_BLK)] +
                 [state(), state(), const(), const(), const()],
        out_specs=pl.BlockSpec((tm, RETW), lambda i: (i, 0)),
        out_shape=S((t, RETW), F32),
        compiler_params=_params("parallel"),
    )(proj, proj, proj, s_f, s_b, rc["f"]["D"] + rc["b"]["D"], rc["f"]["XI"], rc["b"]["XI"])


def _conv_halo_specs(t, tm, width, col):
    r = tm // HALO
    last = t // HALO - 1
    return [pl.BlockSpec((HALO, width), lambda i: (jnp.maximum(i * r - 1, 0), col)),
            pl.BlockSpec((tm, width), lambda i: (i, col)),
            pl.BlockSpec((HALO, width), lambda i: (jnp.minimum((i + 1) * r, last), col))]


def _fill_ext(ext, prev, cur, nxt, i, nt, tm):
    ext[0:HALO, :] = jnp.where(i > 0, prev, 0.0)
    ext[HALO:HALO + tm, :] = cur
    ext[HALO + tm:2 * HALO + tm, :] = jnp.where(i < nt - 1, nxt, 0.0)


def _glu(a_ref, g_ref):
    return a_ref[...].astype(F32) * _sigmoid(g_ref[...].astype(F32))


def _shifted_copies(ext, rot, tm):
    rows = tm + 2 * HALO - 8
    for b in range(1, 8):
        rot[b - 1, :, :] = ext[pl.ds(b, rows), :]


def _tap(ext, rot, r0, s, rb):
    a, b = divmod(s, 8)
    return ext[pl.ds(r0 + 8 * a, rb), :] if b == 0 else rot[b - 1, pl.ds(r0 + 8 * a, rb), :]


def _f_conv(proj, cw, cb, lng, lnb, l, name, comm=None):
    t = proj.shape[0]
    tm = 512
    nt = t // tm
    rb = 64

    def body(ap, ac, an, gp, gc, gn, cw_ref, cb_ref, lng_ref, lnb_ref, c_ref, y_ref, hext, hrot):
        i = pl.program_id(0)
        _fill_ext(hext, _glu(ap, gp), _glu(ac, gc), _glu(an, gn), i, nt, tm)
        _shifted_copies(hext, hrot, tm)
        for r0 in range(0, tm, rb):
            acc = jnp.zeros((rb, CVW), F32) + cb_ref[...]
            for j in range(KW):
                acc = acc + cw_ref[j:j + 1, :] * _tap(hext, hrot, r0, j + 1, rb)
            c_ref[r0:r0 + rb, :] = acc
            chat, _ = _standardize(acc)
            z = chat * lng_ref[...] + lnb_ref[...]
            y_ref[r0:r0 + rb, :] = (z * _sigmoid(z)).astype(BF16)

    vec = lambda: pl.BlockSpec((None, 1, CVW), lambda i: (l, 0, 0))
    return _launch(
        body, (nt,),
        _conv_halo_specs(t, tm, CVW, CONV_A_BLK) + _conv_halo_specs(t, tm, CVW, CONV_GATE_BLK) +
        [pl.BlockSpec((None, 32, CVW), lambda i: (l, 0, 0)), vec(), vec(), vec()],
        [pl.BlockSpec((tm, CVW), lambda i: (i, 0)), pl.BlockSpec((tm, CVW), lambda i: (i, 0))],
        [S((t, CVW), F32), S((t, CVW), BF16)],
        [pltpu.VMEM((tm + 2 * HALO, CVW), F32), pltpu.VMEM((7, tm + 2 * HALO - 8, CVW), F32)],
        (proj, proj, proj, proj, proj, proj, cw, cb, lng, lnb), name, ("parallel",), comm)


def _f_mixout(x, y_gm, y_cv, o, proj, w, name, comm=None):
    t = x.shape[0]
    tm = 512 if comm else min(1024, t)

    def body(x_ref, ygm_ref, ycv_ref, o_ref, g_ref, w_ref, xm_ref, ycat_t_ref, ycat):
        ycat[:, 0:GMW] = ygm_ref[...]
        ycat[:, GMW + RETW:D] = ycv_ref[...]
        for h in range(HEADS):
            sl = slice(h * DH, (h + 1) * DH)
            ohat, _ = _standardize(o_ref[:, sl])
            g = g_ref[:, sl].astype(F32)
            ycat[:, GMW + h * DH:GMW + (h + 1) * DH] = (ohat * (g * _sigmoid(g))).astype(BF16)
        yc = ycat[...]
        ycat_t_ref[...] = yc.T
        xm_ref[...] = x_ref[...] + _dot(yc, w_ref[...])

    return _launch(
        body, (t // tm,),
        [pl.BlockSpec((tm, D), lambda i: (i, 0)),
         pl.BlockSpec((tm, GMW), lambda i: (i, 0)),
         pl.BlockSpec((tm, CVW), lambda i: (i, 0)),
         pl.BlockSpec((tm, RETW), lambda i: (i, 0)),
         pl.BlockSpec((tm, RETW), lambda i: (i, G_BLK)),
         _resident((D, D), lambda i: (0, 0))],
        [pl.BlockSpec((tm, D), lambda i: (i, 0)), pl.BlockSpec((D, tm), lambda i: (0, i))],
        [S((t, D), F32), S((D, t), BF16)],
        [pltpu.VMEM((tm, D), BF16)], (x, y_gm, y_cv, o, proj, w), name, ("parallel",), comm)


def _f_ffn(xm, g2, w1, w2, l, name, comm=None):
    t = xm.shape[0]
    tm = 512

    def body(x_ref, g_ref, w1_ref, w2_ref, xo_ref, h_ref, gu_ref, act_t_ref):
        xv = x_ref[...]
        h = (xv * _rms(xv) * g_ref[...]).astype(BF16)
        h_ref[...] = h
        acc = xv
        for a, b in FF_CHUNKS:
            gate = _dot_nt(h, w1_ref[a:b, :])
            up = _dot_nt(h, w1_ref[FFH + a:FFH + b, :])
            gu_ref[:, a:b] = gate.astype(BF16)
            gu_ref[:, FFH + a:FFH + b] = up.astype(BF16)
            av = ((gate * _sigmoid(gate)) * up).astype(BF16)
            act_t_ref[a:b, :] = av.T
            acc = acc + _dot(av, w2_ref[a:b, :])
        xo_ref[...] = acc

    return _launch(
        body, (t // tm,),
        [pl.BlockSpec((tm, D), lambda i: (i, 0)),
         pl.BlockSpec((None, 1, D), lambda i: (l, 0, 0)),
         _resident((2 * FFH, D), lambda i: (0, 0)),
         _resident((FFH, D), lambda i: (0, 0))],
        [pl.BlockSpec((tm, D), lambda i: (i, 0)), pl.BlockSpec((tm, D), lambda i: (i, 0)),
         pl.BlockSpec((tm, 2 * FFH), lambda i: (i, 0)), pl.BlockSpec((FFH, tm), lambda i: (0, i))],
        [S((t, D), F32), S((t, D), BF16), S((t, 2 * FFH), BF16), S((FFH, t), BF16)],
        [], (xm, g2, w1, w2), name, ("parallel",), comm)


def _b_ffn(top, xm, g2, gu, w1, w2, l, name, comm=None):
    t = xm.shape[0]
    tm = 256
    from_loss = isinstance(top, tuple)
    n_top = 3 if from_loss else 1

    def body(*refs):
        top_refs = refs[:n_top]
        x_ref, g_ref, gu_ref, w1_ref, w2_ref, dgu_ref, dxm_ref, dxb_ref, dg_ref = refs[n_top:n_top + 9]
        first = pl.program_id(0) == 0

        @pl.when(first)
        def _():
            dg_ref[...] = jnp.zeros_like(dg_ref)

        if from_loss:
            xo_ref, fg_ref, t_ref = top_refs
            loss_ref, dfg_ref = refs[n_top + 9:n_top + 11]

            @pl.when(first)
            def _():
                loss_ref[...] = jnp.zeros_like(loss_ref)
                dfg_ref[...] = jnp.zeros_like(dfg_ref)

            xo = xo_ref[...]
            ro = _rms(xo)
            xr = xo * ro
            err = xr * fg_ref[...] - t_ref[...]
            loss_ref[...] += (0.5 / D) * _col_sum(jnp.sum(err * err, axis=1, keepdims=True))
            dy = err * (1.0 / D)
            dfg_ref[...] += _col_sum(dy * xr)
            dxo = _rmsnorm_bwd(dy, xo, ro, fg_ref[...])
        else:
            dxo = top_refs[0][...]
        dxb = dxo.astype(BF16)
        dxb_ref[...] = dxb
        dh = jnp.zeros((tm, D), F32)
        for a, b in FF_CHUNKS:
            dact = _dot_nt(dxb, w2_ref[a:b, :])
            gate = gu_ref[:, a:b].astype(F32)
            up = gu_ref[:, FFH + a:FFH + b].astype(F32)
            sg, dsg = _silu_and_grad(gate)
            dgate = (dact * up * dsg).astype(BF16)
            dup = (dact * sg).astype(BF16)
            dgu_ref[a:b, :] = dgate.T
            dgu_ref[FFH + a:FFH + b, :] = dup.T
            dh = dh + _dot(dgate, w1_ref[a:b, :]) + _dot(dup, w1_ref[FFH + a:FFH + b, :])
        xv = x_ref[...]
        r = _rms(xv)
        dg_ref[...] += _col_sum(dh * xv * r)
        dxm_ref[...] = dxo + _rmsnorm_bwd(dh, xv, r, g_ref[...])

    tok = lambda: pl.BlockSpec((tm, D), lambda i: (i, 0))
    vec = lambda: pl.BlockSpec((1, D), lambda i: (0, 0))
    top_specs = [tok(), vec(), tok()] if from_loss else [tok()]
    extra_specs = [pl.BlockSpec((1, 1), lambda i: (0, 0)), vec()] if from_loss else []
    extra_shape = [S((1, 1), F32), S((1, D), F32)] if from_loss else []
    return _launch(
        body, (t // tm,),
        top_specs + [tok(), pl.BlockSpec((None, 1, D), lambda i: (l, 0, 0)),
                     pl.BlockSpec((tm, 2 * FFH), lambda i: (i, 0)),
                     _resident((2 * FFH, D), lambda i: (0, 0)),
                     _resident((FFH, D), lambda i: (0, 0))],
        [pl.BlockSpec((2 * FFH, tm), lambda i: (0, i)), tok(), tok(), vec()] + extra_specs,
        [S((2 * FFH, t), BF16), S((t, D), F32), S((t, D), BF16), S((1, D), F32)] + extra_shape,
        [],
        ((*top,) if from_loss else (top,)) + (xm, g2, gu, w1, w2), name, ("arbitrary",), comm)


def _mm_wgrad(at, b, pieces, at_rows, b_mode, group, name):
    t = at.shape[-1]
    ka = at.shape[0] // pieces if at_rows else at.shape[0]
    width = group * (b.shape[1] // pieces if b_mode == "cols" else b.shape[-1])
    bt = next(c for c in (4096, 2048, 1024) if 4 * c * (ka + width) + 4 * ka * width <= 2 * VMEM_LIMIT // 3)
    bt = min(bt, t)
    nt = t // bt
    if at_rows:
        a_spec = pl.BlockSpec((ka, bt), lambda j, tt: (j, tt))
    else:
        a_spec = pl.BlockSpec((ka, bt), lambda j, tt: (0, tt))
    if b_mode == "shared":
        nb, b_spec = b.shape[1], pl.BlockSpec((bt, b.shape[1]), lambda j, tt: (tt, 0))
    elif b_mode == "cols":
        nb = b.shape[1] // pieces
        b_spec = pl.BlockSpec((bt, group * nb), lambda j, tt: (tt, j))
    else:
        nb, b_spec = b.shape[2], pl.BlockSpec((None, bt, b.shape[2]), lambda j, tt: (j, tt, 0))
    assert group == 1 or b_mode == "cols"

    def body(a_ref, b_ref, o_ref, acc):
        tt = pl.program_id(1)

        @pl.when(tt == 0)
        def _():
            acc[...] = jnp.zeros_like(acc)

        acc[...] += _dot(a_ref[...], b_ref[...])

        @pl.when(tt == nt - 1)
        def _():
            for k in range(group):
                o_ref[k] = acc[:, k * nb:(k + 1) * nb].astype(BF16)

    return pl.pallas_call(
        body, grid=(pieces // group, nt), name=name,
        in_specs=[a_spec, b_spec],
        out_specs=pl.BlockSpec((group, ka, nb), lambda j, tt: (j, 0, 0)),
        out_shape=S((pieces, ka, nb), BF16),
        scratch_shapes=[pltpu.VMEM((ka, group * nb), F32)],
        compiler_params=_params("parallel", "arbitrary"),
    )(at, b)


def _b_mixout(dxm, w, o, proj, c, lng, lnb, l, name, comm=None):
    t = dxm.shape[0]
    tm = 512

    def body(dxm_ref, w_ref, o_ref, g_ref, c_ref, lng_ref, lnb_ref,
             dxb_ref, dygm_ref, dO_ref, dg_ref, dc_ref, dlg_ref, dlb_ref, dcb_ref):
        @pl.when(pl.program_id(0) == 0)
        def _():
            dlg_ref[...] = jnp.zeros_like(dlg_ref)
            dlb_ref[...] = jnp.zeros_like(dlb_ref)
            dcb_ref[...] = jnp.zeros_like(dcb_ref)

        dxb = dxm_ref[...].astype(BF16)
        dxb_ref[...] = dxb
        dy = _dot_nt(dxb, w_ref[...])
        dygm_ref[...] = dy[:, 0:GMW]
        for h in range(HEADS):
            sl = slice(h * DH, (h + 1) * DH)
            ohat, rstd = _standardize(o_ref[:, sl])
            sg, dsg = _silu_and_grad(g_ref[:, sl].astype(F32))
            dyr = dy[:, GMW + h * DH:GMW + (h + 1) * DH]
            dg_ref[:, sl] = (dyr * ohat * dsg).astype(BF16)
            dO_ref[:, sl] = _standardize_bwd(dyr * sg, ohat, rstd)
        chat, rstd = _standardize(c_ref[...])
        z = chat * lng_ref[...] + lnb_ref[...]
        _, dsz = _silu_and_grad(z)
        dz = dy[:, GMW + RETW:D] * dsz
        dlg_ref[...] += _col_sum(dz * chat)
        dlb_ref[...] += _col_sum(dz)
        dc = _standardize_bwd(dz * lng_ref[...], chat, rstd)
        dcb_ref[...] += _col_sum(dc)
        dc_ref[...] = dc

    vec = lambda: pl.BlockSpec((None, 1, CVW), lambda i: (l, 0, 0))
    acc = lambda: pl.BlockSpec((1, CVW), lambda i: (0, 0))
    return _launch(
        body, (t // tm,),
        [pl.BlockSpec((tm, D), lambda i: (i, 0)),
         _resident((D, D), lambda i: (0, 0)),
         pl.BlockSpec((tm, RETW), lambda i: (i, 0)),
         pl.BlockSpec((tm, RETW), lambda i: (i, G_BLK)),
         pl.BlockSpec((tm, CVW), lambda i: (i, 0)), vec(), vec()],
        [pl.BlockSpec((tm, D), lambda i: (i, 0)), pl.BlockSpec((tm, GMW), lambda i: (i, 0)),
         pl.BlockSpec((tm, RETW), lambda i: (i, 0)), pl.BlockSpec((tm, RETW), lambda i: (i, 0)),
         pl.BlockSpec((tm, CVW), lambda i: (i, 0)), acc(), acc(), acc()],
        [S((t, D), BF16), S((t, GMW), F32), S((t, RETW), F32), S((t, RETW), BF16), S((t, CVW), F32),
         S((1, CVW), F32), S((1, CVW), F32), S((1, CVW), F32)],
        [], (dxm, w, o, proj, c, lng, lnb), name, ("arbitrary",), comm)


def _b_gm(proj, dy, lng, lnb, ws_bf, wst_bf, bias, l, name, comm=None):
    t = proj.shape[0]
    tm = min(1024, t)
    nt = t // tm

    def body(p_ref, dy_ref, lng_ref, lnb_ref, ws_ref, wst_ref, bias_ref,
             duv_ref, dws_ref, dbias_ref, dbs_ref, dlg_ref, dlb_ref):
        @pl.when(pl.program_id(0) == 0)
        def _():
            dws_ref[...] = jnp.zeros_like(dws_ref)
            dbias_ref[...] = jnp.zeros_like(dbias_ref)
            dbs_ref[...] = jnp.zeros_like(dbs_ref)
            dlg_ref[...] = jnp.zeros_like(dlg_ref)
            dlb_ref[...] = jnp.zeros_like(dlb_ref)

        for ci in range(tm // C):
            rows = slice(ci * C, (ci + 1) * C)
            u = p_ref[rows, 0:GMW].astype(F32)
            v = p_ref[rows, GMW:2 * GMW].astype(F32)
            au, dau, dav, vhat, rstd, vn, mixed, head = _gm_chunk_fwd(u, v, lng_ref[...], lnb_ref[...], ws_ref, bias_ref[...])
            dyc = dy_ref[rows, :]
            dmixed = dyc * au
            dmb = dmixed.astype(BF16)
            dbias_ref[...] += dmixed
            dvn = jnp.zeros((C, GMW), F32)
            for h in range(HEADS):
                dws_ref[h] += _dot_nt(jnp.where(head == h, dmixed, 0.0).astype(BF16), vn)
                dvn = dvn + jnp.where(head == h, _dot(wst_ref[h], dmb), 0.0)
            dlg_ref[...] += _col_sum(dvn * vhat)
            dlb_ref[...] += _col_sum(dvn)
            dav_in = _standardize_bwd(dvn * lng_ref[...], vhat, rstd)
            duv_ref[rows, 0:GMW] = (dyc * mixed * dau).astype(BF16)
            duv_ref[rows, GMW:2 * GMW] = (dav_in * dav).astype(BF16)

        @pl.when(pl.program_id(0) == nt - 1)
        def _():
            head = lax.broadcasted_iota(jnp.int32, (C, GMW), 1) // (GMW // HEADS)
            lane = lax.broadcasted_iota(jnp.int32, (C, 128), 1)
            fold = jnp.zeros((C, 128), F32)
            for h in range(HEADS):
                col = jnp.sum(jnp.where(head == h, dbias_ref[...], 0.0), axis=1, keepdims=True)
                fold = jnp.where(lane == h, col, fold)
            dbs_ref[...] = fold

    vec = lambda: pl.BlockSpec((None, 1, GMW), lambda i: (l, 0, 0))
    mats = lambda: pl.BlockSpec((None, HEADS, C, C), lambda i: (l, 0, 0, 0))
    return _launch(
        body, (nt,),
        [pl.BlockSpec((tm, 2 * GMW), lambda i: (i, 0)), pl.BlockSpec((tm, GMW), lambda i: (i, 0)),
         vec(), vec(), mats(), mats(), pl.BlockSpec((None, C, GMW), lambda i: (l, 0, 0))],
        [pl.BlockSpec((tm, 2 * GMW), lambda i: (i, 0)),
         pl.BlockSpec((HEADS, C, C), lambda i: (0, 0, 0)),
         pl.BlockSpec((C, GMW), lambda i: (0, 0)), pl.BlockSpec((C, 128), lambda i: (0, 0)),
         pl.BlockSpec((1, GMW), lambda i: (0, 0)), pl.BlockSpec((1, GMW), lambda i: (0, 0))],
        [S((t, 2 * GMW), BF16), S((HEADS, C, C), F32), S((C, GMW), F32), S((C, 128), F32),
         S((1, GMW), F32), S((1, GMW), F32)],
        [], (proj, dy, lng, lnb, ws_bf, wst_bf, bias), name, ("arbitrary",), comm)


def _b_conv(proj, dc, cw, l, name, comm=None):
    t = proj.shape[0]
    tm = 256
    nt = t // tm
    rb = 64

    def body(ap, ac, an, gp, gc, gn, dp, dcur, dn, cw_ref, dag_ref, dcw_ref, hext, dext, hrot, drot):
        i = pl.program_id(0)

        @pl.when(i == 0)
        def _():
            dcw_ref[...] = jnp.zeros_like(dcw_ref)

        _fill_ext(hext, _glu(ap, gp), _glu(ac, gc), _glu(an, gn), i, nt, tm)
        _fill_ext(dext, dp[...], dcur[...], dn[...], i, nt, tm)
        _shifted_copies(hext, hrot, tm)
        _shifted_copies(dext, drot, tm)
        for j in range(KW):
            dcw_ref[j:j + 1, :] += _col_sum(dcur[...] * _tap(hext, hrot, 0, j + 1, tm))
        for r0 in range(0, tm, rb):
            dh = jnp.zeros((rb, CVW), F32)
            for j in range(KW):
                dh = dh + cw_ref[j:j + 1, :] * _tap(dext, drot, r0, 2 * HALO - 1 - j, rb)
            a = ac[r0:r0 + rb, :].astype(F32)
            s = _sigmoid(gc[r0:r0 + rb, :].astype(F32))
            dag_ref[r0:r0 + rb, 0:CVW] = (dh * s).astype(BF16)
            dag_ref[r0:r0 + rb, CVW:2 * CVW] = (dh * a * s * (1.0 - s)).astype(BF16)

    dspecs = _conv_halo_specs(t, tm, CVW, 0)
    return _launch(
        body, (nt,),
        _conv_halo_specs(t, tm, CVW, CONV_A_BLK) + _conv_halo_specs(t, tm, CVW, CONV_GATE_BLK) + dspecs +
        [pl.BlockSpec((None, 32, CVW), lambda i: (l, 0, 0))],
        [pl.BlockSpec((tm, 2 * CVW), lambda i: (i, 0)), pl.BlockSpec((32, CVW), lambda i: (0, 0))],
        [S((t, 2 * CVW), BF16), S((32, CVW), F32)],
        [pltpu.VMEM((tm + 2 * HALO, CVW), F32), pltpu.VMEM((tm + 2 * HALO, CVW), F32),
         pltpu.VMEM((7, tm + 2 * HALO - 8, CVW), F32), pltpu.VMEM((7, tm + 2 * HALO - 8, CVW), F32)],
        (proj, proj, proj, proj, proj, proj, dc, dc, dc, cw), name, ("arbitrary",), comm)


def _b_ret_out(proj, cos2, sin2, dO, s_f, s_b, g_f, g_b, rc, name, comm=None):
    t = proj.shape[0]
    nchunks = min(RET_CHUNKS, t // C)
    tm = nchunks * C

    def body(q_ref, k_ref, v_ref, cos_ref, sin_ref, dO_ref, sf_ref, sb_ref, gf_ref, gb_ref,
             d_ref, xif_ref, xib_ref, zef_ref, zeb_ref, dq_ref, dk_ref, dv_ref):
        for ci in range(nchunks):
            rows = slice(ci * C, (ci + 1) * C)
            cos_v, sin_v = cos_ref[rows, :], sin_ref[rows, :]
            for h in range(HEADS):
                sl = slice(h * DH, (h + 1) * DH)
                qh, kh, vh = q_ref[rows, sl], k_ref[rows, sl], v_ref[rows, sl]
                dOh = dO_ref[rows, sl].astype(BF16)
                dm = d_ref[h]
                p = (_dot_nt(qh, kh) * dm).astype(BF16)
                dp = (_dot_nt(dOh, vh) * dm).astype(BF16)
                from_s = _dot_nt(dOh, jnp.concatenate([sf_ref[ci, h], sb_ref[ci, h]], axis=0))
                from_g = _dot_nt(vh, jnp.concatenate([gf_ref[ci, h], gb_ref[ci, h]], axis=0))
                kg = _dot(kh, jnp.concatenate([gf_ref[ci, h], gb_ref[ci, h]], axis=1))
                dqr = _dot(dp, kh) + xif_ref[h] * from_s[:, 0:DH] + xib_ref[h] * from_s[:, DH:2 * DH]
                dkr = (_dot_tn(dp, qh) + zef_ref[h] * from_g[:, 0:DH] + zeb_ref[h] * from_g[:, DH:2 * DH]) * SCALE
                dv = _dot_tn(p, dOh) + zef_ref[h] * kg[:, 0:DH] + zeb_ref[h] * kg[:, DH:2 * DH]
                dq_ref[rows, sl] = _rot_t(dqr, cos_v, sin_v).astype(BF16)
                dk_ref[rows, sl] = _rot_t(dkr, cos_v, sin_v).astype(BF16)
                dv_ref[rows, sl] = dv.astype(BF16)

    const = lambda: pl.BlockSpec((HEADS, C, DH), lambda i: (0, 0, 0))
    state = lambda: pl.BlockSpec((nchunks, HEADS, DH, DH), lambda i: (i, 0, 0, 0))
    tok = lambda: pl.BlockSpec((tm, RETW), lambda i: (i, 0))
    return _launch(
        body, (t // tm,),
        [pl.BlockSpec((tm, RETW), lambda i, cb=cb: (i, cb)) for cb in (Q_BLK, K_BLK, V_BLK)] +
        [pl.BlockSpec((tm, DH), lambda i: (i, 0)), pl.BlockSpec((tm, DH), lambda i: (i, 0)), tok(),
         state(), state(), state(), state()] + [const() for _ in range(5)],
        [tok(), tok(), tok()],
        [S((t, RETW), BF16) for _ in range(3)], [],
        (proj, proj, proj, cos2, sin2, dO, s_f, s_b, g_f, g_b, rc["f"]["D"] + rc["b"]["D"],
         rc["f"]["XI"], rc["b"]["XI"], rc["f"]["ZETA"], rc["b"]["ZETA"]), name, ("parallel",), comm)


def _b_inproj(d_uv, dqkv, d_g, d_ag, w, x, g1, dxm, l, name, comm=None):
    t = x.shape[0]
    tm = 512

    def body(duv_ref, dq_ref, dk_ref, dv_ref, dg_ref, dag_ref, w_ref, x_ref, g_ref, dxm_ref,
             dp_ref, dx_ref, dn_ref):
        @pl.when(pl.program_id(0) == 0)
        def _():
            dn_ref[...] = jnp.zeros_like(dn_ref)

        for k, part in enumerate((duv_ref, dq_ref, dk_ref, dv_ref, dg_ref, dag_ref)):
            dp_ref[:, 512 * k:512 * (k + 1)] = part[...]
        dh = _dot_nt(dp_ref[...], w_ref[...])
        xv = x_ref[...]
        r = _rms(xv)
        dn_ref[...] += _col_sum(dh * xv * r)
        dx_ref[...] = dxm_ref[...] + _rmsnorm_bwd(dh, xv, r, g_ref[...])

    half = lambda: pl.BlockSpec((tm, 512), lambda i: (i, 0))
    full = lambda: pl.BlockSpec((tm, D), lambda i: (i, 0))
    return _launch(
        body, (t // tm,),
        [half() for _ in range(6)] +
        [_resident((D, INW), lambda i: (0, 0)), full(), pl.BlockSpec((None, 1, D), lambda i: (l, 0, 0)), full()],
        [pl.BlockSpec((tm, INW), lambda i: (i, 0)), full(), pl.BlockSpec((1, D), lambda i: (0, 0))],
        [S((t, INW), BF16), S((t, D), F32), S((1, D), F32)], [],
        (d_uv, *dqkv, d_g, d_ag, w, x, g1, dxm), name, ("arbitrary",), comm)


class _PairSwap:
    def __init__(self, units):
        self.inputs = list(units)
        self.out_shape = [S((4,) + u.shape[1:], BF16) for u in units]
        n = len(units)
        self.scratch = [pltpu.SemaphoreType.DMA((n, 4)), pltpu.SemaphoreType.DMA((n, 4))]

    def run(self, phase, ins, outs, scr):
        ssem, rsem = scr
        x, y, c, _ = _place()
        copies = lambda: [pltpu.make_async_remote_copy(src_ref=ins[u].at[2 * chip + (1 - c)], dst_ref=outs[u].at[chip],
                                                       send_sem=ssem.at[u, chip], recv_sem=rsem.at[u, chip],
                                                       device_id=(x, y, 1 - c), device_id_type=MESH)
                          for u in range(len(self.inputs)) for chip in range(4)]
        if phase == "start":
            for cp in copies():
                cp.start()
        elif phase == "finish":
            for cp in copies():
                cp.wait_recv()
            for cp in copies():
                cp.wait_send()


def _pair_add(g, q, name):
    _, mm, nn = g.shape
    bm = _row_block(mm)

    def body(c_ref, g_ref, q_ref, h_ref):
        h_ref[...] = (g_ref[...].astype(F32) + q_ref[...].astype(F32)).astype(BF16)

    blk = lambda: pl.BlockSpec((None, bm, nn), lambda qq, i, c_ref: (qq, i, 0))
    return pl.pallas_call(
        body, name=name,
        grid_spec=pltpu.PrefetchScalarGridSpec(
            num_scalar_prefetch=1, grid=(4, mm // bm),
            in_specs=[pl.BlockSpec((None, None, bm, nn), lambda qq, i, c_ref: (qq, c_ref[0], i, 0)), blk()],
            out_specs=blk()),
        out_shape=S((4, mm, nn), BF16),
        compiler_params=_params("parallel", "parallel"),
    )(lax.axis_index("c").astype(jnp.int32).reshape(1), g.reshape(4, 2, mm, nn), q)


_BIG = ("w_in", "w_out", "w_ffn_in", "w_ffn_out")
_KIND = dict(w_in="cols", w_out="rows", w_ffn_in="lead", w_ffn_out="rows")
CWP = 128
EARLY_ROWS, LATE_ROWS = 152, 8


def _step(x, tgt, wts, sh, cw_pad):
    t = x.shape[0]
    rc = _ret_consts()
    cos2, sin2 = (jnp.asarray(a) for a in _rope_tables(t))
    n1 = wts["norm1_g"].reshape(LAYERS, 1, D)
    n2 = wts["norm2_g"].reshape(LAYERS, 1, D)
    gm_lng = wts["gm_ln_g"].reshape(LAYERS, 1, GMW)
    gm_lnb = wts["gm_ln_b"].reshape(LAYERS, 1, GMW)
    ws_bf = wts["gm_ws"].astype(BF16)
    wst_bf = jnp.swapaxes(wts["gm_ws"], 2, 3).astype(BF16)
    bias = jnp.repeat(jnp.swapaxes(wts["gm_bs"], 1, 2), GMW // HEADS, axis=2)
    cb = wts["conv_b"].reshape(LAYERS, 1, CVW)
    cv_lng = wts["conv_ln_g"].reshape(LAYERS, 1, CVW)
    cv_lnb = wts["conv_ln_b"].reshape(LAYERS, 1, CVW)
    unit = lambda f, l: (sh[f][l], _KIND[f])
    cshard = CVW // NDEV

    full = {f: [None] * LAYERS for f in _BIG}
    full["w_in"][0], cw_all = _comm_only(_Gather([unit("w_in", 0), (cw_pad, "lead")]), "gather_first")
    cw = jnp.transpose(cw_all[:, :, :, :cshard], (1, 2, 0, 3)).reshape(LAYERS, 32, CVW)

    gcf, gcb = rc["f"]["gC"], rc["b"]["gC"]
    saved = []
    for l in range(LAYERS):
        first = l == 0
        (proj, h1), got = _f_inproj(x, n1, full["w_in"][l], cos2, sin2, l, f"f_inproj_{l}",
                                    _Gather([unit("w_ffn_in", 0)]) if first else None)
        if first:
            full["w_ffn_in"][0], = got
        y_gm = _f_gm(proj, gm_lng, gm_lnb, ws_bf, bias, l, f"f_gm_{l}")
        s_f, s_b = _scan_pair(proj, None, 2, rc["f"]["ZETA"], rc["b"]["ZETA"], gcf, gcb, True, f"f_ret_state_{l}")
        o = _f_ret_out(proj, s_f, s_b, rc, f"f_ret_out_{l}")
        (c, y_cv), got = _f_conv(proj, cw, cb, cv_lng, cv_lnb, l, f"f_conv_{l}",
                                 _Gather([unit("w_out", 0)]) if first else None)
        if first:
            full["w_out"][0], = got
        (xm, ycat), got = _f_mixout(x, y_gm, y_cv, o, proj, full["w_out"][l], f"f_mixout_{l}",
                                    _Gather([unit("w_ffn_out", 0)]) if first else None)
        if first:
            full["w_ffn_out"][0], = got
        w1t = full["w_ffn_in"][l].reshape(2 * FFH, D)
        (xo, h2, gu, act), got = _f_ffn(xm, n2, w1t, full["w_ffn_out"][l], l, f"f_ffn_{l}",
                                        _Gather([unit(f, 1) for f in _BIG]) if first else None)
        if first:
            full["w_in"][1], full["w_out"][1], full["w_ffn_in"][1], full["w_ffn_out"][1] = got
        saved.append(dict(x=x, proj=proj, h1=h1, o=o, s_f=s_f, s_b=s_b, c=c, xm=xm, ycat=ycat, h2=h2, gu=gu, act=act))
        x = xo

    parts = {f: [None] * LAYERS for f in _BIG}
    small = [None] * LAYERS
    norm1 = [None] * LAYERS
    upper = None
    top = (x, wts["final_g"].reshape(1, D), tgt)
    for l in reversed(range(LAYERS)):
        sv = saved[l]
        outs, got = _b_ffn(top, sv["xm"], n2, sv["gu"], full["w_ffn_in"][l].reshape(2 * FFH, D), full["w_ffn_out"][l], l,
                           f"b_ffn_{l}", _Comms([_Scatter(upper), _PairSwap([upper_in])]) if upper else None)
        dgu_t, dxm, dxo_bf, d_n2 = outs[:4]
        if l == LAYERS - 1:
            loss, d_final = outs[4:]
        if upper:
            parts["w_out"][l + 1], parts["w_ffn_in"][l + 1], parts["w_ffn_out"][l + 1], q_up = got
            h_in_up = _pair_add(upper_in, q_up, f"pair_add_w_in_{l + 1}")
        g_f2 = _mm_wgrad(sv["act"], dxo_bf, NDEV // 2, True, "shared", 1, f"g_ffn_out_{l}").reshape(NDEV, FFH // NDEV, D)
        g_f1 = _mm_wgrad(dgu_t, sv["h2"], NDEV, True, "shared", 1, f"g_ffn_in_{l}")
        swap = _PairSwap([g_f1, g_f2])
        (dxm_bf, dy_gm, dO, d_g, dc, d_cvlg, d_cvlb, d_cb), got = _b_mixout(
            dxm, full["w_out"][l], sv["o"], sv["proj"], sv["c"], cv_lng, cv_lnb, l, f"b_mixout_{l}",
            _Comms([swap, _Scatter([h_in_up])]) if upper else swap)
        q_f1, q_f2 = got[:2]
        if upper:
            parts["w_in"][l + 1] = got[2]
        h_f1 = _pair_add(g_f1, q_f1, f"pair_add_w_ffn_in_{l}")
        h_f2 = _pair_add(g_f2, q_f2, f"pair_add_w_ffn_out_{l}")
        g_out = _mm_wgrad(sv["ycat"], dxm_bf, 1, False, "shared", 1, f"g_out_{l}").reshape(NDEV, D // NDEV, D)
        last = l == 0
        (d_uv, d_ws, _, d_bs_fold, d_gmlg, d_gmlb), (q_out,) = _b_gm(
            sv["proj"], dy_gm, gm_lng, gm_lnb, ws_bf, wst_bf, bias, l, f"b_gm_{l}", _PairSwap([g_out]))
        h_out = _pair_add(g_out, q_out, f"pair_add_w_out_{l}")
        (d_ag, d_cw), got = _b_conv(sv["proj"], dc, cw, l, f"b_conv_{l}", _Scatter([h_f1]) if last else None)
        if last:
            parts["w_ffn_in"][l], = got
        small[l] = dict(gm_ln_g=d_gmlg[0], gm_ln_b=d_gmlb[0], gm_ws=d_ws, gm_bs=d_bs_fold[:, :HEADS].T, conv_w=d_cw[:KW],
                        conv_b=d_cb[0], conv_ln_g=d_cvlg[0], conv_ln_b=d_cvlb[0], norm2_g=d_n2[0])
        comm = None
        if last:
            early_g = {k: jnp.stack([small[ll][k] for ll in range(LAYERS)]) for k in small[0]}
            early_g["final_g"] = d_final[0]
            early_buf = _pack([early_g[k] for k in _SMALL_EARLY] + [loss], EARLY_ROWS)
            comm = _Comms([_Scatter([h_out, h_f2]), _Gather([(early_buf, "lead")])])
        g_f, g_b = _scan_pair(sv["proj"], dO, 1, rc["f"]["XI"], rc["b"]["XI"], gcf, gcb, False, f"b_ret_state_{l}")
        dqkv, got = _b_ret_out(sv["proj"], cos2, sin2, dO, sv["s_f"], sv["s_b"], g_f, g_b, rc, f"b_ret_out_{l}", comm)
        if last:
            parts["w_out"][l], parts["w_ffn_out"][l], early_parts = got
        (dproj, top, d_n1), _ = _b_inproj(d_uv, dqkv, d_g, d_ag, full["w_in"][l], sv["x"], n1, dxm, l, f"b_inproj_{l}")
        norm1[l] = d_n1[0]
        g_in = _mm_wgrad(sv["h1"], dproj, NDEV, False, "cols", 4, f"g_in_{l}")
        if last:
            q_in, = _comm_only(_PairSwap([g_in]), f"pair_swap_w_in_{l}")
            tail = [_pair_add(g_in, q_in, f"pair_add_w_in_{l}")]
        else:
            upper, upper_in = [h_out, h_f1, h_f2], g_in
    late_buf = _pack([jnp.stack(norm1)], LATE_ROWS)
    parts["w_in"][0], late_parts = _comm_only(_Comms([_Scatter(tail), _Gather([(late_buf, "lead")])]), "exchange_last")
    return loss, top, parts, (early_parts, late_parts)


def _adamw(w, g, m, v):
    m = ADAM_B1 * m + (1.0 - ADAM_B1) * g
    v = ADAM_B2 * v + (1.0 - ADAM_B2) * (g * g)
    m_hat = m / (1.0 - ADAM_B1 ** ADAM_STEP)
    v_hat = v / (1.0 - ADAM_B2 ** ADAM_STEP)
    return -ADAM_LR * (m_hat / (jnp.sqrt(v_hat) + ADAM_EPS) + ADAM_WD * w), m, v


def _cast_blocks(ws):
    def body(*refs):
        ins, outs = refs[:len(ws)], refs[len(ws):]
        for k, src in enumerate(ins):
            for l in range(LAYERS):
                outs[k * LAYERS + l][...] = src[l].astype(BF16)

    outs = pl.pallas_call(body, name="cast_blocks", out_shape=[S(w.shape[1:], BF16) for w in ws for _ in range(LAYERS)],
                          compiler_params=_params())(*ws)
    return [list(outs[k * LAYERS:(k + 1) * LAYERS]) for k in range(len(ws))]


def _row_block(mm):
    return next(b for b in (512, 352, 128) if mm % b == 0)


def _sum_adam(parts, w, m, v, l, prev, name):
    _, mm, nn = parts.shape
    bm = _row_block(mm)

    def body(p_ref, w_ref, m_ref, v_ref, *rest):
        g_ref, d_ref, nm_ref, nv_ref = rest[-4:]
        g = p_ref[0].astype(F32)
        for s in range(1, 4):
            g = g + p_ref[s].astype(F32)
        g_ref[...] = g
        d_ref[...], nm_ref[...], nv_ref[...] = _adamw(w_ref[...], g, m_ref[...], v_ref[...])

    blk = lambda: pl.BlockSpec((None, bm, nn), lambda i: (l, i, 0))
    prev = list(prev) if prev else []
    return pl.pallas_call(
        body, grid=(mm // bm,), name=name,
        in_specs=[pl.BlockSpec((4, bm, nn), lambda i: (0, i, 0)), blk(), blk(), blk()] + [_ANY] * len(prev),
        out_specs=[blk() for _ in range(4)],
        out_shape=[S(w.shape, F32) for _ in range(4)],
        input_output_aliases={4 + j: j for j in range(len(prev))},
        compiler_params=_params("parallel"),
    )(parts, w, m, v, *prev)


def _sum_small(parts):
    n = len(parts)

    def body(*refs):
        for p_ref, o_ref in zip(refs[:n], refs[n:]):
            g = p_ref[0]
            for s in range(1, NDEV):
                g = g + p_ref[s]
            o_ref[...] = g

    return pl.pallas_call(body, name="sum_small", out_shape=[S(p.shape[1:], F32) for p in parts],
                          compiler_params=_params())(*parts)


def _adam_small(quads):
    n = len(quads)
    as2d = lambda a: a.reshape(1, -1) if a.ndim == 1 else a

    def body(*refs):
        ins, outs = refs[:4 * n], refs[4 * n:]
        for k in range(n):
            g_ref, w_ref, m_ref, v_ref = ins[4 * k:4 * k + 4]
            outs[3 * k][...], outs[3 * k + 1][...], outs[3 * k + 2][...] = _adamw(w_ref[...], g_ref[...], m_ref[...], v_ref[...])

    flat = [as2d(a) for q in quads for a in q]
    outs = pl.pallas_call(body, name="adam_small", out_shape=[S(as2d(q[1]).shape, F32) for q in quads for _ in range(3)],
                          compiler_params=_params())(*flat)
    return [tuple(o.reshape(q[1].shape) for o in outs[3 * k:3 * k + 3]) for k, q in enumerate(quads)]


_SMALL = ("norm1_g", "gm_ln_g", "gm_ln_b", "gm_ws", "gm_bs", "conv_w", "conv_b", "conv_ln_g", "conv_ln_b",
          "norm2_g", "final_g")
_SMALL_EARLY = _SMALL[1:]
_NAMES = ("norm1_g", "w_in", "gm_ln_g", "gm_ln_b", "gm_ws", "gm_bs", "conv_w", "conv_b", "conv_ln_g", "conv_ln_b",
          "w_out", "norm2_g", "w_ffn_in", "w_ffn_out", "final_g")


def _pack(parts, rows):
    flat = jnp.concatenate([p.reshape(-1) for p in parts])
    return jnp.pad(flat, (0, rows * 1024 - flat.shape[0])).reshape(rows, 1024)


def _unpack(buf, shapes):
    flat = buf.reshape(-1)
    out, o = [], 0
    for shp in shapes:
        sz = int(np.prod(shp))
        out.append(flat[o:o + sz].reshape(shp))
        o += sz
    return out


def kernel(x, norm1_g, w_in, gm_ln_g, gm_ln_b, gm_ws, gm_bs, conv_w, conv_b, conv_ln_g, conv_ln_b, w_out, norm2_g, w_ffn_in, w_ffn_out, final_g, loss_target, m_norm1_g, m_w_in, m_gm_ln_g, m_gm_ln_b, m_gm_ws, m_gm_bs, m_conv_w, m_conv_b, m_conv_ln_g, m_conv_ln_b, m_w_out, m_norm2_g, m_w_ffn_in, m_w_ffn_out, m_final_g, v_norm1_g, v_w_in, v_gm_ln_g, v_gm_ln_b, v_gm_ws, v_gm_bs, v_conv_w, v_conv_b, v_conv_ln_g, v_conv_ln_b, v_w_out, v_norm2_g, v_w_ffn_in, v_w_ffn_out, v_final_g):
    w = dict(norm1_g=norm1_g, w_in=w_in, gm_ln_g=gm_ln_g, gm_ln_b=gm_ln_b, gm_ws=gm_ws, gm_bs=gm_bs, conv_w=conv_w,
             conv_b=conv_b, conv_ln_g=conv_ln_g, conv_ln_b=conv_ln_b, w_out=w_out, norm2_g=norm2_g, w_ffn_in=w_ffn_in,
             w_ffn_out=w_ffn_out, final_g=final_g)
    mo = dict(norm1_g=m_norm1_g, w_in=m_w_in, gm_ln_g=m_gm_ln_g, gm_ln_b=m_gm_ln_b, gm_ws=m_gm_ws, gm_bs=m_gm_bs,
              conv_w=m_conv_w, conv_b=m_conv_b, conv_ln_g=m_conv_ln_g, conv_ln_b=m_conv_ln_b, w_out=m_w_out,
              norm2_g=m_norm2_g, w_ffn_in=m_w_ffn_in, w_ffn_out=m_w_ffn_out, final_g=m_final_g)
    vo = dict(norm1_g=v_norm1_g, w_in=v_w_in, gm_ln_g=v_gm_ln_g, gm_ln_b=v_gm_ln_b, gm_ws=v_gm_ws, gm_bs=v_gm_bs,
              conv_w=v_conv_w, conv_b=v_conv_b, conv_ln_g=v_conv_ln_g, conv_ln_b=v_conv_ln_b, w_out=v_w_out,
              norm2_g=v_norm2_g, w_ffn_in=v_w_ffn_in, w_ffn_out=v_w_ffn_out, final_g=v_final_g)
    t = x.shape[1]
    me = 4 * lax.axis_index("x") + 2 * lax.axis_index("y") + lax.axis_index("c")
    cshard = conv_w.shape[2]

    cw_pad = jnp.pad(conv_w, ((0, 0), (0, 32 - KW), (0, CWP - cshard)))
    flip = lambda a: jnp.swapaxes(a, 1, 2)
    big = {f: tuple(flip(a[f]) if f == "w_ffn_in" else a[f] for a in (w, mo, vo)) for f in _BIG}
    sh = dict(zip(_BIG, _cast_blocks([big[f][0] for f in _BIG])))
    loss, dx, parts, small_parts = _step(x.reshape(t, D), loss_target.reshape(t, D), w, sh, cw_pad)

    grads, delta, new_m, new_v = {}, {}, {}, {}
    for f in _BIG:
        outs = None
        for l in reversed(range(LAYERS)):
            outs = _sum_adam(parts[f][l], *big[f], l, outs, f"sum_adam_{f}_{l}")
        grads[f], delta[f], new_m[f], new_v[f] = [flip(a) for a in outs] if f == "w_ffn_in" else outs

    early_sum, late_sum = _sum_small(small_parts)
    early_shapes = [(LAYERS, KW, CVW) if k == "conv_w" else w[k].shape for k in _SMALL_EARLY]
    grads["norm1_g"], = _unpack(late_sum, [w["norm1_g"].shape])
    *early, total = _unpack(early_sum, early_shapes + [()])
    for k, g in zip(_SMALL_EARLY, early):
        grads[k] = lax.dynamic_slice_in_dim(g, me * cshard, cshard, axis=2) if k == "conv_w" else g
    for k, (d, nm, nv) in zip(_SMALL, _adam_small([(grads[k], w[k], mo[k], vo[k]) for k in _SMALL])):
        delta[k], new_m[k], new_v[k] = d, nm, nv

    return (total, dx.reshape(1, t, D), *[grads[k] for k in _NAMES], *[delta[k] for k in _NAMES],
            *[new_m[k] for k in _NAMES], *[new_v[k] for k in _NAMES])
```
